```python
import math
import jax, jax.numpy as jnp
from jax import lax
import numpy as np

D_MODEL = 1024
BATCH = 2
SEQ = 8192
DEPTH = 1

CONV_DIM = 512
CONV_WIDTH = 31
CONV_GROUPS = 8
N_HEADS = 8
HEAD_DIM = 64
ATTN_DIM = N_HEADS * HEAD_DIM
Q_BLOCK = 128
N_EXPERTS = 64
TOP_K = 8
N_GROUPS = 8
TOPK_GROUPS = 4
EXPERT_DIM = 256
SHARED_DIM = 256
ROUTED_SCALE = 2.5
TOKEN_CHUNK_MAX = 1024
EPS = 1e-6

OFF_CONV = 0
OFF_Q = OFF_CONV + 2 * CONV_DIM
OFF_K = OFF_Q + ATTN_DIM
OFF_V = OFF_K + ATTN_DIM
OFF_F = OFF_V + ATTN_DIM
OFF_GC = OFF_F + N_HEADS
OFF_GA = OFF_GC + D_MODEL
IN_COLS = OFF_GA + D_MODEL

kernel_name = "hybrid_conformer_fox_moe_block"


def rms_norm(x, g):
    xf = x.astype(jnp.float32)
    y = xf * lax.rsqrt(jnp.mean(xf * xf, axis=-1, keepdims=True) + EPS)
    return (y * g).astype(x.dtype)


def group_norm(u, g, b):
    B, S, C = u.shape
    uf = u.astype(jnp.float32).reshape(B, S, CONV_GROUPS, C // CONV_GROUPS)
    mu = jnp.mean(uf, axis=-1, keepdims=True)
    var = jnp.mean(jnp.square(uf - mu), axis=-1, keepdims=True)
    y = ((uf - mu) * lax.rsqrt(var + EPS)).reshape(B, S, C)
    return (y * g + b).astype(u.dtype)


def conformer_conv(a, gate, w_dw, b_dw, gn_g, gn_b, w_conv_out):
    u = a * jax.nn.sigmoid(gate)
    u = lax.conv_general_dilated(
        u, w_dw[:, None, :].astype(u.dtype), window_strides=(1,),
        padding=[(CONV_WIDTH - 1, 0)],
        dimension_numbers=('NWC', 'WIO', 'NWC'),
        feature_group_count=CONV_DIM) + b_dw
    u = jax.nn.silu(group_norm(u, gn_g, gn_b))
    return u @ w_conv_out


def forgetting_attention(q, k, v, log_f):
    S = q.shape[2]
    cum = jnp.cumsum(log_f, axis=-1)
    scale = HEAD_DIM ** -0.5
    neg = jnp.finfo(jnp.float32).min
    outs = []
    for i in range(S // Q_BLOCK):
        q0, q1 = i * Q_BLOCK, (i + 1) * Q_BLOCK
        s = jnp.einsum('bhqd,bhkd->bhqk', q[:, :, q0:q1], k[:, :, :q1],
                       preferred_element_type=jnp.float32) * scale
        s = s + cum[:, :, q0:q1, None] - cum[:, :, None, :q1]
        causal = (q0 + jnp.arange(Q_BLOCK))[:, None] >= jnp.arange(q1)[None, :]
        p = jax.nn.softmax(jnp.where(causal, s, neg), axis=-1)
        outs.append(jnp.einsum('bhqk,bhkd->bhqd', p.astype(v.dtype), v[:, :, :q1]))
    return jnp.concatenate(outs, axis=2)


def mixer_sublayer(h, w_in, w_dw, b_dw, gn_g, gn_b, w_conv_out, q_norm_g, k_norm_g,
                   b_forget, w_attn_out, w_out):
    B, S, _ = h.shape
    p = h @ w_in
    y_conv = conformer_conv(p[..., OFF_CONV:OFF_CONV + CONV_DIM],
                            p[..., OFF_CONV + CONV_DIM:OFF_Q],
                            w_dw, b_dw, gn_g, gn_b, w_conv_out)
    q = rms_norm(p[..., OFF_Q:OFF_K].reshape(B, S, N_HEADS, HEAD_DIM), q_norm_g)
    k = rms_norm(p[..., OFF_K:OFF_V].reshape(B, S, N_HEADS, HEAD_DIM), k_norm_g)
    v = p[..., OFF_V:OFF_F].reshape(B, S, N_HEADS, HEAD_DIM)
    log_f = jax.nn.log_sigmoid((p[..., OFF_F:OFF_GC] + b_forget).astype(jnp.float32))
    o = forgetting_attention(q.transpose(0, 2, 1, 3), k.transpose(0, 2, 1, 3),
                             v.transpose(0, 2, 1, 3), log_f.transpose(0, 2, 1))
    y_attn = o.transpose(0, 2, 1, 3).reshape(B, S, ATTN_DIM) @ w_attn_out
    merged = (jax.nn.sigmoid(p[..., OFF_GC:OFF_GA]) * y_conv
              + jax.nn.sigmoid(p[..., OFF_GA:IN_COLS]) * y_attn)
    return merged @ w_out


def moe_sublayer(h, w_router, router_bias, w_experts_gate_up, w_experts_down,
                 w_shared_gate_up, w_shared_down):
    B, S, D = h.shape
    T = B * S
    t = h.reshape(T, D)
    scores = jax.nn.sigmoid((t @ w_router).astype(jnp.float32))
    biased = scores + router_bias.astype(jnp.float32)
    grp = biased.reshape(T, N_GROUPS, N_EXPERTS // N_GROUPS)
    grp_score = lax.top_k(grp, 2)[0].sum(-1)
    _, gidx = lax.top_k(grp_score, TOPK_GROUPS)
    gmask = jax.nn.one_hot(gidx, N_GROUPS, dtype=jnp.float32).sum(1) > 0
    emask = jnp.repeat(gmask, N_EXPERTS // N_GROUPS, axis=-1)
    _, eidx = lax.top_k(jnp.where(emask, biased, -jnp.inf), TOP_K)
    w = jnp.take_along_axis(scores, eidx, axis=-1)
    w = w / jnp.sum(w, axis=-1, keepdims=True) * ROUTED_SCALE
    combine = jnp.sum(jax.nn.one_hot(eidx, N_EXPERTS, dtype=jnp.float32) * w[..., None],
                      axis=1).astype(h.dtype)
    chunk = math.gcd(T, TOKEN_CHUNK_MAX)

    def expert_chunk(args):
        tc, cc = args
        gu = jnp.einsum('cd,edf->cef', tc, w_experts_gate_up)
        act = jax.nn.silu(gu[..., :EXPERT_DIM]) * gu[..., EXPERT_DIM:] * cc[..., None]
        return jnp.einsum('cef,efd->cd', act, w_experts_down)

    routed = lax.map(expert_chunk, (t.reshape(T // chunk, chunk, D),
                                    combine.reshape(T // chunk, chunk, N_EXPERTS))).reshape(T, D)
    sgu = t @ w_shared_gate_up
    shared = (jax.nn.silu(sgu[:, :SHARED_DIM]) * sgu[:, SHARED_DIM:]) @ w_shared_down
    return (routed + shared).reshape(B, S, D)


def setup_inputs(seed: int = 0) -> dict:
    key = jax.random.key(seed)
    ks = jax.random.split(key, 24)
    L, D = DEPTH, D_MODEL
    f32 = jnp.float32

    def nrm(k, shape, scale):
        return jax.random.normal(k, shape, f32) * scale

    return {
        "x": nrm(ks[0], (BATCH, SEQ, D), 1.0),
        "c": nrm(ks[1], (BATCH, D), 1.0),
        "w_ada": nrm(ks[2], (L, D, 6 * D), 0.1 * D ** -0.5),
        "b_ada": nrm(ks[3], (L, 6 * D), 0.02),
        "norm1_g": 1.0 + nrm(ks[4], (L, D), 0.02),
        "w_in": nrm(ks[5], (L, D, IN_COLS), D ** -0.5),
        "w_dw": nrm(ks[6], (L, CONV_WIDTH, CONV_DIM), CONV_WIDTH ** -0.5),
        "b_dw": nrm(ks[7], (L, CONV_DIM), 0.02),
        "conv_gn_g": 1.0 + nrm(ks[8], (L, CONV_DIM), 0.02),
        "conv_gn_b": nrm(ks[9], (L, CONV_DIM), 0.02),
        "w_conv_out": nrm(ks[10], (L, CONV_DIM, D), CONV_DIM ** -0.5),
        "q_norm_g": 1.0 + nrm(ks[11], (L, HEAD_DIM), 0.02),
        "k_norm_g": 1.0 + nrm(ks[12], (L, HEAD_DIM), 0.02),
        "b_forget": 2.0 + nrm(ks[13], (L, N_HEADS), 0.1),
        "w_attn_out": nrm(ks[14], (L, ATTN_DIM, D), ATTN_DIM ** -0.5),
        "w_out": nrm(ks[15], (L, D, D), D ** -0.5),
        "norm2_g": 1.0 + nrm(ks[16], (L, D), 0.02),
        "w_router": nrm(ks[17], (L, D, N_EXPERTS), D ** -0.5),
        "router_bias": nrm(ks[18], (L, N_EXPERTS), 0.01),
        "w_experts_gate_up": nrm(ks[19], (L, N_EXPERTS, D, 2 * EXPERT_DIM), D ** -0.5),
        "w_experts_down": nrm(ks[20], (L, N_EXPERTS, EXPERT_DIM, D), EXPERT_DIM ** -0.5),
        "w_shared_gate_up": nrm(ks[21], (L, D, 2 * SHARED_DIM), D ** -0.5),
        "w_shared_down": nrm(ks[22], (L, SHARED_DIM, D), SHARED_DIM ** -0.5),
    }


def reference(x, c, w_ada, b_ada, norm1_g, w_in, w_dw, b_dw, conv_gn_g, conv_gn_b,
              w_conv_out, q_norm_g, k_norm_g, b_forget, w_attn_out, w_out, norm2_g,
              w_router, router_bias, w_experts_gate_up, w_experts_down,
              w_shared_gate_up, w_shared_down):
    for l in range(DEPTH):
        mod = jax.nn.silu(c) @ w_ada[l] + b_ada[l]
        shift1, scale1, gate1, shift2, scale2, gate2 = jnp.split(mod[:, None, :], 6, axis=-1)
        h = rms_norm(x, norm1_g[l]) * (1.0 + scale1) + shift1
        x = x + gate1 * mixer_sublayer(h, w_in[l], w_dw[l], b_dw[l], conv_gn_g[l], conv_gn_b[l],
                                       w_conv_out[l], q_norm_g[l], k_norm_g[l], b_forget[l],
                                       w_attn_out[l], w_out[l])
        h = rms_norm(x, norm2_g[l]) * (1.0 + scale2) + shift2
        x = x + gate2 * moe_sublayer(h, w_router[l], router_bias[l], w_experts_gate_up[l],
                                     w_experts_down[l], w_shared_gate_up[l], w_shared_down[l])
    return x
```

```python
import functools

import numpy as np
import jax
import jax.numpy as jnp
from jax import lax
from jax.experimental import pallas as pl
from jax.experimental.pallas import tpu as pltpu

F32 = jnp.float32
BF16 = jnp.bfloat16

CONV_DIM = 512
CONV_WIDTH = 31
CONV_GROUPS = 8
N_HEADS = 8
HEAD_DIM = 64
ATTN_DIM = N_HEADS * HEAD_DIM
N_EXPERTS = 64
TOP_K = 8
N_GROUPS = 8
TOPK_GROUPS = 4
EXPERT_DIM = 256
SHARED_DIM = 256
ROUTED_SCALE = 2.5
EPS = 1e-6

LANES = 128
SUBLANES = 8
HEAD_PAD = LANES
AUG0 = HEAD_DIM
VMEM_LIMIT = 56 * 1024 * 1024

ROW_TILE = 256
ATTN_TILE = 512
MOE_TILE = 1024
CONV_HALO = 32

NEG_BIG = -1e30


def _dot(a, b):
    return jnp.dot(a, b, preferred_element_type=F32)


def _dot_nt(a, b):
    return lax.dot_general(a, b, (((1,), (1,)), ((), ())), preferred_element_type=F32)


def _split2(x):
    hi = x.astype(BF16)
    lo = (x - hi.astype(F32)).astype(BF16)
    return hi, lo


def _split3(x):
    hi = x.astype(BF16)
    r = x - hi.astype(F32)
    mid = r.astype(BF16)
    lo = (r - mid.astype(F32)).astype(BF16)
    return hi, mid, lo


def _sigmoid(x):
    return 1.0 / (1.0 + jnp.exp(-x))


def _silu(x):
    return x * _sigmoid(x)


def _params(*sem):
    return pltpu.CompilerParams(dimension_semantics=sem, vmem_limit_bytes=VMEM_LIMIT)


def _const_spec(shape):
    n = len(shape)
    return pl.BlockSpec(shape, lambda *_: (0,) * n)


def _ada_kernel(c_ref, w_ref, b_ref, o_ref):
    c = c_ref[...]
    a_hi, a_lo = _split2(_silu(c))
    w_hi, w_lo = _split2(w_ref[...])
    o_ref[...] = _dot(a_hi, w_hi) + _dot(a_hi, w_lo) + _dot(a_lo, w_hi) + b_ref[...]


def _ada(c_pad, w_ada, b_ada):
    d = c_pad.shape[1]
    n = w_ada.shape[1]
    return pl.pallas_call(
        _ada_kernel,
        grid=(n // d,),
        in_specs=[_const_spec(c_pad.shape),
                  pl.BlockSpec((d, d), lambda j: (0, j)),
                  pl.BlockSpec((1, d), lambda j: (0, j))],
        out_specs=pl.BlockSpec((c_pad.shape[0], d), lambda j: (0, j)),
        out_shape=jax.ShapeDtypeStruct((c_pad.shape[0], n), F32),
        compiler_params=_params("arbitrary"),
        name="ada",
    )(c_pad, w_ada, b_ada)


def _inproj_kernel(x_ref, mod_ref, g1_ref, wcv_ref, wq_ref, wk_ref, wv_ref, wf_ref, wgc_ref,
                   wga_ref, bf_ref, qg_ref, kg_ref, pq_ref, pk_ref, cq_ref, ck_ref, tri_ref,
                   u_ref, qa_ref, ka_ref, v_ref, sgc_ref, sga_ref, cum_ref, carry_ref):
    @pl.when(pl.program_id(1) == 0)
    def _():
        carry_ref[...] = jnp.zeros_like(carry_ref)

    x = x_ref[0]
    ms = jnp.mean(x * x, axis=-1, keepdims=True)
    h = (x * lax.rsqrt(ms + EPS) * g1_ref[...]) * (1.0 + mod_ref[0, 1:2, :]) + mod_ref[0, 0:1, :]
    hb = h.astype(BF16)

    pc = _dot(hb, wcv_ref[...])
    u_ref[0] = (pc[:, :CONV_DIM] * _sigmoid(pc[:, CONV_DIM:])).astype(BF16)

    sgc_ref[0] = _sigmoid(_dot(hb, wgc_ref[...])).astype(BF16)
    sga_ref[0] = _sigmoid(_dot(hb, wga_ref[...])).astype(BF16)

    z = _dot(hb, wf_ref[...]) + bf_ref[...]
    lf = jnp.minimum(z, 0.0) - jnp.log1p(jnp.exp(-jnp.abs(z)))
    l_hi, l_mid, l_lo = _split3(lf)
    tri = tri_ref[...]
    cum = _dot(tri, l_hi) + _dot(tri, l_mid) + _dot(tri, l_lo) + carry_ref[...]
    carry_ref[...] = cum[ROW_TILE - 1:ROW_TILE, :]
    cum_ref[0] = cum

    c_hi, c_mid, c_lo = _split3(cum)
    addq = _dot(c_hi, pq_ref[0]) + _dot(c_mid, pq_ref[1]) + _dot(c_lo, pq_ref[2]) + cq_ref[...]
    addk = _dot(c_hi, pk_ref[0]) + _dot(c_mid, pk_ref[1]) + _dot(c_lo, pk_ref[2]) + ck_ref[...]

    pq = _dot(hb, wq_ref[...])
    pk = _dot(hb, wk_ref[...])
    pv = _dot(hb, wv_ref[...])
    inv_hd = 1.0 / HEAD_DIM
    for hd in range(N_HEADS):
        sl = slice(hd * HEAD_PAD, (hd + 1) * HEAD_PAD)
        qb = pq[:, sl]
        qn = qb * lax.rsqrt(jnp.sum(qb * qb, axis=-1, keepdims=True) * inv_hd + EPS) * qg_ref[:, sl]
        qa_ref[0, hd] = (qn + addq[:, sl]).astype(BF16)
        kb = pk[:, sl]
        kn = kb * lax.rsqrt(jnp.sum(kb * kb, axis=-1, keepdims=True) * inv_hd + EPS) * kg_ref[:, sl]
        ka_ref[0, hd] = (kn + addk[:, sl]).astype(BF16)
        v_ref[0, hd] = pv[:, sl].astype(BF16)


def _inproj(x, mod, g1, wcv, wq, wk, wv, wf, wgc, wga, bf, qg, kg, pq, pk, cq, ck, tri):
    b, s, d = x.shape
    tm = ROW_TILE
    hp = N_HEADS * HEAD_PAD
    row = lambda w: pl.BlockSpec((1, tm, w), lambda bi, i: (bi, i, 0))
    head = pl.BlockSpec((1, N_HEADS, tm, HEAD_PAD), lambda bi, i: (bi, 0, i, 0))
    consts = [g1, wcv, wq, wk, wv, wf, wgc, wga, bf, qg, kg, pq, pk, cq, ck, tri]
    return pl.pallas_call(
        _inproj_kernel,
        grid=(b, s // tm),
        in_specs=[row(d), pl.BlockSpec((1, 6, d), lambda bi, i: (bi, 0, 0))]
                 + [_const_spec(a.shape) for a in consts],
        out_specs=[row(CONV_DIM), head, head, head, row(d), row(d), row(LANES)],
        out_shape=[jax.ShapeDtypeStruct((b, s, CONV_DIM), BF16),
                   jax.ShapeDtypeStruct((b, N_HEADS, s, HEAD_PAD), BF16),
                   jax.ShapeDtypeStruct((b, N_HEADS, s, HEAD_PAD), BF16),
                   jax.ShapeDtypeStruct((b, N_HEADS, s, HEAD_PAD), BF16),
                   jax.ShapeDtypeStruct((b, s, d), BF16),
                   jax.ShapeDtypeStruct((b, s, d), BF16),
                   jax.ShapeDtypeStruct((b, s, LANES), F32)],
        scratch_shapes=[pltpu.VMEM((1, LANES), F32)],
        compiler_params=_params("arbitrary", "arbitrary"),
        name="inproj",
    )(x, mod, *consts)


def _attn_kernel(q_ref, k_ref, v_ref, o_ref, acc_ref, m_ref, l_ref):
    i = pl.program_id(2)
    t = ATTN_TILE
    q = q_ref[0, 0]
    m_ref[...] = jnp.full_like(m_ref, -jnp.inf)
    l_ref[...] = jnp.zeros_like(l_ref)
    acc_ref[...] = jnp.zeros_like(acc_ref)

    def step(j, masked):
        start = pl.multiple_of(j * t, t)
        s = _dot_nt(q, k_ref[0, 0, pl.ds(start, t), :])
        if masked:
            rows = lax.broadcasted_iota(jnp.int32, s.shape, 0)
            cols = lax.broadcasted_iota(jnp.int32, s.shape, 1)
            s = jnp.where(rows >= cols, s, NEG_BIG)
        m_prev = m_ref[...]
        m_new = jnp.maximum(m_prev, jnp.max(s, axis=-1, keepdims=True))
        alpha = jnp.exp(m_prev - m_new)
        p = jnp.exp(s - m_new)
        l_ref[...] = alpha * l_ref[...] + jnp.sum(p, axis=-1, keepdims=True)
        acc_ref[...] = alpha * acc_ref[...] + _dot(p.astype(BF16), v_ref[0, 0, pl.ds(start, t), :])
        m_ref[...] = m_new

    def body(j, carry):
        step(j, False)
        return carry

    lax.fori_loop(0, i, body, 0)
    step(i, True)
    o_ref[0] = (acc_ref[...] / l_ref[...]).astype(BF16)


def _attention(qa, ka, v):
    b, nh, s, hp = qa.shape
    t = ATTN_TILE
    return pl.pallas_call(
        _attn_kernel,
        grid=(b, nh, s // t),
        in_specs=[pl.BlockSpec((1, 1, t, hp), lambda bi, hi, i: (bi, hi, i, 0)),
                  pl.BlockSpec((1, 1, s, hp), lambda bi, hi, i: (bi, hi, 0, 0)),
                  pl.BlockSpec((1, 1, s, hp), lambda bi, hi, i: (bi, hi, 0, 0))],
        out_specs=pl.BlockSpec((1, t, hp), lambda bi, hi, i: (bi, i, hi)),
        out_shape=jax.ShapeDtypeStruct((b, s, nh * hp), BF16),
        scratch_shapes=[pltpu.VMEM((t, hp), F32), pltpu.VMEM((t, 1), F32), pltpu.VMEM((t, 1), F32)],
        compiler_params=_params("arbitrary", "arbitrary", "arbitrary"),
        name="attn",
    )(qa, ka, v)


def _merge_kernel(u_ref, halo_ref, o_ref, sgc_ref, sga_ref, x_ref, mod_ref, wdw_ref, bdw_ref,
                  gng_ref, gnb_ref, gg_ref, wco_ref, wao_ref, wout_ref, x1_ref, buf_ref):
    tm = ROW_TILE
    halo = halo_ref[0].astype(F32)
    halo = jnp.where(pl.program_id(1) == 0, jnp.zeros_like(halo), halo)
    buf_ref[0:CONV_HALO, :] = halo
    buf_ref[CONV_HALO:, :] = u_ref[0].astype(F32)

    base = CONV_HALO - (CONV_WIDTH - 1)
    y = jnp.zeros((tm, CONV_DIM), F32)
    for j in range(CONV_WIDTH):
        y = y + wdw_ref[j:j + 1, :] * buf_ref[base + j:base + j + tm, :]
    y = y + bdw_ref[...]

    gg = gg_ref[...]
    y_hi, y_lo = _split2(y)
    dlt = y - (_dot(y_hi, gg) + _dot(y_lo, gg))
    s_hi, s_lo = _split2(dlt * dlt)
    var = _dot(s_hi, gg) + _dot(s_lo, gg)
    yn = dlt * lax.rsqrt(var + EPS) * gng_ref[...] + gnb_ref[...]
    y_conv = _dot(_silu(yn).astype(BF16), wco_ref[...])

    y_attn = _dot(o_ref[0], wao_ref[...])
    merged = sgc_ref[0].astype(F32) * y_conv + sga_ref[0].astype(F32) * y_attn
    mix = _dot(merged.astype(BF16), wout_ref[...])
    x1_ref[0] = x_ref[0] + mod_ref[0, 2:3, :] * mix


def _merge(u, o, sgc, sga, x, mod, wdw, bdw, gng, gnb, gg, wco, wao, wout):
    b, s, d = x.shape
    tm = ROW_TILE
    per = tm // CONV_HALO
    row = lambda w: pl.BlockSpec((1, tm, w), lambda bi, i: (bi, i, 0))
    consts = [wdw, bdw, gng, gnb, gg, wco, wao, wout]
    return pl.pallas_call(
        _merge_kernel,
        grid=(b, s // tm),
        in_specs=[row(CONV_DIM),
                  pl.BlockSpec((1, CONV_HALO, CONV_DIM),
                               lambda bi, i: (bi, jnp.maximum(i * per - 1, 0), 0)),
                  row(o.shape[2]), row(d), row(d), row(d),
                  pl.BlockSpec((1, 6, d), lambda bi, i: (bi, 0, 0))]
                 + [_const_spec(a.shape) for a in consts],
        out_specs=row(d),
        out_shape=jax.ShapeDtypeStruct((b, s, d), F32),
        scratch_shapes=[pltpu.VMEM((CONV_HALO + tm, CONV_DIM), F32)],
        compiler_params=_params("arbitrary", "arbitrary"),
        name="merge",
    )(u, u, o, sgc, sga, x, mod, *consts)


def _router_kernel(x1_ref, mod_ref, g2_ref, wr_hi_ref, wr_lo_ref, rb_ref, h2_ref, comb_ref):
    x = x1_ref[0]
    ms = jnp.mean(x * x, axis=-1, keepdims=True)
    h = (x * lax.rsqrt(ms + EPS) * g2_ref[...]) * (1.0 + mod_ref[0, 4:5, :]) + mod_ref[0, 3:4, :]
    h2_ref[0] = h.astype(BF16)

    h_hi, h_lo = _split2(h)
    logits = _dot_nt(wr_hi_ref[...], h_hi) + _dot_nt(wr_hi_ref[...], h_lo) + _dot_nt(wr_lo_ref[...], h_hi)
    scores = _sigmoid(logits)
    biased = scores + rb_ref[...]

    per = N_EXPERTS // N_GROUPS
    rows = lax.broadcasted_iota(jnp.int32, (per, biased.shape[1]), 0)
    gscore = []
    for g in range(N_GROUPS):
        blk = biased[g * per:(g + 1) * per, :]
        top1 = jnp.max(blk, axis=0, keepdims=True)
        first = jnp.min(jnp.where(blk == top1, rows, per), axis=0, keepdims=True)
        top2 = jnp.max(jnp.where(rows == first, -jnp.inf, blk), axis=0, keepdims=True)
        gscore.append(top1 + top2)

    cand = []
    for g in range(N_GROUPS):
        rank = jnp.zeros_like(gscore[g], dtype=jnp.int32)
        for g2 in range(N_GROUPS):
            if g2 == g:
                continue
            ahead = gscore[g2] > gscore[g]
            if g2 < g:
                ahead = ahead | (gscore[g2] == gscore[g])
            rank = rank + ahead.astype(jnp.int32)
        keep = rank < TOPK_GROUPS
        cand.append(jnp.where(keep, biased[g * per:(g + 1) * per, :], -jnp.inf))
    cand = jnp.concatenate(cand, axis=0)

    eidx = lax.broadcasted_iota(jnp.int32, cand.shape, 0)
    rank = jnp.zeros(cand.shape, jnp.int32)
    for e2 in range(N_EXPERTS):
        other = cand[e2:e2 + 1, :]
        ahead = (other > cand) | ((other == cand) & (eidx > e2))
        rank = rank + ahead.astype(jnp.int32)
    sel = (rank < TOP_K) & (cand > -jnp.inf)
    w = jnp.where(sel, scores, 0.0)
    comb_ref[...] = w / jnp.sum(w, axis=0, keepdims=True) * ROUTED_SCALE


def _router(x1, mod, g2, wr_hi, wr_lo, rb):
    b, s, d = x1.shape
    tm = ROW_TILE
    nt = s // tm
    return pl.pallas_call(
        _router_kernel,
        grid=(b, nt),
        in_specs=[pl.BlockSpec((1, tm, d), lambda bi, i: (bi, i, 0)),
                  pl.BlockSpec((1, 6, d), lambda bi, i: (bi, 0, 0)),
                  _const_spec(g2.shape), _const_spec(wr_hi.shape), _const_spec(wr_lo.shape),
                  _const_spec(rb.shape)],
        out_specs=[pl.BlockSpec((1, tm, d), lambda bi, i: (bi, i, 0)),
                   pl.BlockSpec((N_EXPERTS, tm), lambda bi, i: (0, bi * nt + i))],
        out_shape=[jax.ShapeDtypeStruct((b, s, d), BF16),
                   jax.ShapeDtypeStruct((N_EXPERTS, b * s), F32)],
        compiler_params=_params("arbitrary", "arbitrary"),
        name="router",
    )(x1, mod, g2, wr_hi, wr_lo, rb)


def _moe_kernel(h_ref, comb_ref, wgu_ref, wd_ref, wsgu_ref, wsd_ref, x1_ref, mod_ref, out_ref, acc_ref):
    e = pl.program_id(1)
    h = h_ref[...]

    @pl.when(e == 0)
    def _():
        sgu = _dot(h, wsgu_ref[...])
        act = _silu(sgu[:, :SHARED_DIM]) * sgu[:, SHARED_DIM:]
        acc_ref[...] = _dot(act.astype(BF16), wsd_ref[...])

    pick = (lax.broadcasted_iota(jnp.int32, (N_EXPERTS, EXPERT_DIM), 0) == e).astype(BF16)
    c_hi, c_lo = _split2(comb_ref[...])
    gate = _dot(c_hi, pick) + _dot(c_lo, pick)

    gu = _dot(h, wgu_ref[0].astype(BF16))
    act = _silu(gu[:, :EXPERT_DIM]) * gu[:, EXPERT_DIM:] * gate
    acc_ref[...] += _dot(act.astype(BF16), wd_ref[0].astype(BF16))

    @pl.when(e == N_EXPERTS - 1)
    def _():
        out_ref[...] = x1_ref[...] + mod_ref[0, 5:6, :] * acc_ref[...]


def _moe(h2, comb, wgu, wd, wsgu, wsd, x1, mod, tiles_per_batch):
    t, d = h2.shape
    tm = MOE_TILE
    return pl.pallas_call(
        _moe_kernel,
        grid=(t // tm, N_EXPERTS),
        in_specs=[pl.BlockSpec((tm, d), lambda ti, e: (ti, 0)),
                  pl.BlockSpec((tm, N_EXPERTS), lambda ti, e: (ti, 0)),
                  pl.BlockSpec((1, d, 2 * EXPERT_DIM), lambda ti, e: (e, 0, 0)),
                  pl.BlockSpec((1, EXPERT_DIM, d), lambda ti, e: (e, 0, 0)),
                  _const_spec(wsgu.shape), _const_spec(wsd.shape),
                  pl.BlockSpec((tm, d), lambda ti, e: (ti, 0)),
                  pl.BlockSpec((1, 6, d), lambda ti, e: (ti // tiles_per_batch, 0, 0))],
        out_specs=pl.BlockSpec((tm, d), lambda ti, e: (ti, 0)),
        out_shape=jax.ShapeDtypeStruct((t, d), F32),
        scratch_shapes=[pltpu.VMEM((tm, d), F32)],
        compiler_params=_params("arbitrary", "arbitrary"),
        name="moe",
    )(h2, comb, wgu, wd, wsgu, wsd, x1, mod)


def _pad_heads(w):
    d = w.shape[0]
    w = w.reshape(d, N_HEADS, HEAD_DIM)
    return jnp.pad(w, ((0, 0), (0, 0), (0, HEAD_PAD - HEAD_DIM))).reshape(d, N_HEADS * HEAD_PAD)


def _placement():
    pq = np.zeros((3, LANES, N_HEADS * HEAD_PAD), np.float32)
    pk = np.zeros((3, LANES, N_HEADS * HEAD_PAD), np.float32)
    cq = np.zeros((1, N_HEADS * HEAD_PAD), np.float32)
    ck = np.zeros((1, N_HEADS * HEAD_PAD), np.float32)
    for hd in range(N_HEADS):
        for k in range(3):
            pq[k, hd, hd * HEAD_PAD + AUG0 + k] = 1.0
            ck[0, hd * HEAD_PAD + AUG0 + k] = 1.0
            pk[k, hd, hd * HEAD_PAD + AUG0 + 3 + k] = -1.0
            cq[0, hd * HEAD_PAD + AUG0 + 3 + k] = 1.0
    return (jnp.asarray(pq, BF16), jnp.asarray(pk, BF16), jnp.asarray(cq), jnp.asarray(ck))


def kernel(x, c, w_ada, b_ada, norm1_g, w_in, w_dw, b_dw, conv_gn_g, conv_gn_b, w_conv_out,
           q_norm_g, k_norm_g, b_forget, w_attn_out, w_out, norm2_g, w_router, router_bias,
           w_experts_gate_up, w_experts_down, w_shared_gate_up, w_shared_down):
    depth = w_ada.shape[0]
    b, s, d = x.shape
    off_q = 2 * CONV_DIM
    off_k = off_q + ATTN_DIM
    off_v = off_k + ATTN_DIM
    off_f = off_v + ATTN_DIM
    off_gc = off_f + N_HEADS
    off_ga = off_gc + d

    pq, pk, cq, ck = _placement()
    tri = jnp.asarray(np.tril(np.ones((ROW_TILE, ROW_TILE), np.float32)), BF16)
    grp = np.arange(CONV_DIM) // (CONV_DIM // CONV_GROUPS)
    gg = jnp.asarray((grp[:, None] == grp[None, :]).astype(np.float32) / (CONV_DIM // CONV_GROUPS), BF16)
    c_pad = jnp.pad(c, ((0, SUBLANES - b), (0, 0)))

    for l in range(depth):
        mod = _ada(c_pad, w_ada[l], b_ada[l][None, :])[:b].reshape(b, 6, d)

        wi = w_in[l]
        bf = jnp.pad(b_forget[l][None, :], ((0, 0), (0, LANES - N_HEADS)))
        wf = jnp.pad(wi[:, off_f:off_gc], ((0, 0), (0, LANES - N_HEADS))).astype(BF16)
        gpad = lambda g, sc: jnp.tile(jnp.pad(g * sc, (0, HEAD_PAD - HEAD_DIM)), N_HEADS)[None, :]
        u, qa, ka, v, sgc, sga, _ = _inproj(
            x, mod, norm1_g[l][None, :],
            wi[:, :off_q].astype(BF16),
            _pad_heads(wi[:, off_q:off_k]).astype(BF16),
            _pad_heads(wi[:, off_k:off_v]).astype(BF16),
            _pad_heads(wi[:, off_v:off_f]).astype(BF16),
            wf, wi[:, off_gc:off_ga].astype(BF16), wi[:, off_ga:].astype(BF16),
            bf, gpad(q_norm_g[l], HEAD_DIM ** -0.5), gpad(k_norm_g[l], 1.0),
            pq, pk, cq, ck, tri)

        o = _attention(qa, ka, v)

        wao = jnp.pad(w_attn_out[l].reshape(N_HEADS, HEAD_DIM, d),
                      ((0, 0), (0, HEAD_PAD - HEAD_DIM), (0, 0))).reshape(N_HEADS * HEAD_PAD, d)
        wdw = jnp.pad(w_dw[l], ((0, CONV_HALO - CONV_WIDTH), (0, 0)))
        x1 = _merge(u, o, sgc, sga, x, mod, wdw, b_dw[l][None, :], conv_gn_g[l][None, :],
                    conv_gn_b[l][None, :], gg, w_conv_out[l].astype(BF16), wao.astype(BF16),
                    w_out[l].astype(BF16))

        wr = w_router[l].T
        wr_hi = wr.astype(BF16)
        wr_lo = (wr - wr_hi.astype(F32)).astype(BF16)
        h2, comb_t = _router(x1, mod, norm2_g[l][None, :], wr_hi, wr_lo, router_bias[l][:, None])

        out = _moe(h2.reshape(b * s, d), comb_t.T, w_experts_gate_up[l], w_experts_down[l],
                   w_shared_gate_up[l].astype(BF16), w_shared_down[l].astype(BF16),
                   x1.reshape(b * s, d), mod, s // MOE_TILE)
        x = out.reshape(b, s, d)
    return x
```

```python
import functools

import numpy as np
import jax
import jax.numpy as jnp
from jax import lax
from jax.experimental import pallas as pl
from jax.experimental.pallas import tpu as pltpu

F32 = jnp.float32
BF16 = jnp.bfloat16

CONV_DIM = 512
CONV_WIDTH = 31
CONV_GROUPS = 8
N_HEADS = 8
HEAD_DIM = 64
ATTN_DIM = N_HEADS * HEAD_DIM
N_EXPERTS = 64
TOP_K = 8
N_GROUPS = 8
TOPK_GROUPS = 4
EXPERT_DIM = 256
SHARED_DIM = 256
ROUTED_SCALE = 2.5
EPS = 1e-6

LANES = 128
SUBLANES = 8
HEAD_PAD = LANES
AUG0 = HEAD_DIM
VMEM_LIMIT = 56 * 1024 * 1024

ROW_TILE = 256
ATTN_TILE = 512
MOE_TILE = 1024
CONV_HALO = 32

NEG_BIG = -1e30
EXP_UNDERFLOW = 104.0
FIXED_SHIFT_BOUND = 40.0


def _dot(a, b):
    return jnp.dot(a, b, preferred_element_type=F32)


def _dot_nt(a, b):
    return lax.dot_general(a, b, (((1,), (1,)), ((), ())), preferred_element_type=F32)


def _split2(x):
    hi = x.astype(BF16)
    lo = (x - hi.astype(F32)).astype(BF16)
    return hi, lo


def _split3(x):
    hi = x.astype(BF16)
    r = x - hi.astype(F32)
    mid = r.astype(BF16)
    lo = (r - mid.astype(F32)).astype(BF16)
    return hi, mid, lo


def _sigmoid(x):
    return 1.0 / (1.0 + jnp.exp(-x))


def _silu(x):
    return x * _sigmoid(x)


def _params(*sem):
    return pltpu.CompilerParams(dimension_semantics=sem, vmem_limit_bytes=VMEM_LIMIT)


def _const_spec(shape):
    n = len(shape)
    return pl.BlockSpec(shape, lambda *_: (0,) * n)


def _ada_kernel(c_ref, w_ref, b_ref, o_ref):
    c = c_ref[...]
    a_hi, a_lo = _split2(_silu(c))
    w_hi, w_lo = _split2(w_ref[...])
    o_ref[...] = _dot(a_hi, w_hi) + _dot(a_hi, w_lo) + _dot(a_lo, w_hi) + b_ref[...]


def _ada(c_pad, w_ada, b_ada):
    d = c_pad.shape[1]
    n = w_ada.shape[1]
    return pl.pallas_call(
        _ada_kernel,
        grid=(n // d,),
        in_specs=[_const_spec(c_pad.shape),
                  pl.BlockSpec((d, d), lambda j: (0, j)),
                  pl.BlockSpec((1, d), lambda j: (0, j))],
        out_specs=pl.BlockSpec((c_pad.shape[0], d), lambda j: (0, j)),
        out_shape=jax.ShapeDtypeStruct((c_pad.shape[0], n), F32),
        compiler_params=_params("arbitrary"),
        name="ada",
    )(c_pad, w_ada, b_ada)


def _inproj_kernel(x_ref, mod_ref, g1_ref, wcv_ref, wq_ref, wk_ref, wv_ref, wf_ref, wgc_ref,
                   wga_ref, bf_ref, qg_ref, kg_ref, pq_ref, pk_ref, cq_ref, ck_ref, tri_ref,
                   u_ref, qa_ref, ka_ref, v_ref, sgc_ref, sga_ref, cum_ref, carry_ref):
    @pl.when(pl.program_id(1) == 0)
    def _():
        carry_ref[...] = jnp.zeros_like(carry_ref)

    x = x_ref[0]
    ms = jnp.mean(x * x, axis=-1, keepdims=True)
    h = (x * lax.rsqrt(ms + EPS) * g1_ref[...]) * (1.0 + mod_ref[0, 1:2, :]) + mod_ref[0, 0:1, :]
    hb = h.astype(BF16)

    pc = _dot(hb, wcv_ref[...])
    u_ref[0] = (pc[:, :CONV_DIM] * _sigmoid(pc[:, CONV_DIM:])).astype(BF16)

    sgc_ref[0] = _sigmoid(_dot(hb, wgc_ref[...])).astype(BF16)
    sga_ref[0] = _sigmoid(_dot(hb, wga_ref[...])).astype(BF16)

    z = _dot(hb, wf_ref[...]) + bf_ref[...]
    lf = jnp.minimum(z, 0.0) - jnp.log1p(jnp.exp(-jnp.abs(z)))
    l_hi, l_mid, l_lo = _split3(lf)
    tri = tri_ref[...]
    cum = _dot(tri, l_hi) + _dot(tri, l_mid) + _dot(tri, l_lo) + carry_ref[...]
    carry_ref[...] = cum[ROW_TILE - 1:ROW_TILE, :]
    cum_ref[0] = cum

    c_hi, c_mid, c_lo = _split3(cum)
    addq = _dot(c_hi, pq_ref[0]) + _dot(c_mid, pq_ref[1]) + _dot(c_lo, pq_ref[2]) + cq_ref[...]
    addk = _dot(c_hi, pk_ref[0]) + _dot(c_mid, pk_ref[1]) + _dot(c_lo, pk_ref[2]) + ck_ref[...]

    pq = _dot(hb, wq_ref[...])
    pk = _dot(hb, wk_ref[...])
    pv = _dot(hb, wv_ref[...])
    inv_hd = 1.0 / HEAD_DIM
    vone = (lax.broadcasted_iota(jnp.int32, (1, HEAD_PAD), 1) == HEAD_DIM).astype(F32)
    for hd in range(N_HEADS):
        sl = slice(hd * HEAD_PAD, (hd + 1) * HEAD_PAD)
        qb = pq[:, sl]
        qn = qb * lax.rsqrt(jnp.sum(qb * qb, axis=-1, keepdims=True) * inv_hd + EPS) * qg_ref[:, sl]
        qa_ref[0, hd] = (qn + addq[:, sl]).astype(BF16)
        kb = pk[:, sl]
        kn = kb * lax.rsqrt(jnp.sum(kb * kb, axis=-1, keepdims=True) * inv_hd + EPS) * kg_ref[:, sl]
        ka_ref[0, hd] = (kn + addk[:, sl]).astype(BF16)
        v_ref[0, hd] = (pv[:, sl] + vone).astype(BF16)


def _inproj(x, mod, g1, wcv, wq, wk, wv, wf, wgc, wga, bf, qg, kg, pq, pk, cq, ck, tri):
    b, s, d = x.shape
    tm = ROW_TILE
    hp = N_HEADS * HEAD_PAD
    row = lambda w: pl.BlockSpec((1, tm, w), lambda bi, i: (bi, i, 0))
    head = pl.BlockSpec((1, N_HEADS, tm, HEAD_PAD), lambda bi, i: (bi, 0, i, 0))
    consts = [g1, wcv, wq, wk, wv, wf, wgc, wga, bf, qg, kg, pq, pk, cq, ck, tri]
    return pl.pallas_call(
        _inproj_kernel,
        grid=(b, s // tm),
        in_specs=[row(d), pl.BlockSpec((1, 6, d), lambda bi, i: (bi, 0, 0))]
                 + [_const_spec(a.shape) for a in consts],
        out_specs=[row(CONV_DIM), head, head, head, row(d), row(d), row(LANES)],
        out_shape=[jax.ShapeDtypeStruct((b, s, CONV_DIM), BF16),
                   jax.ShapeDtypeStruct((b, N_HEADS, s, HEAD_PAD), BF16),
                   jax.ShapeDtypeStruct((b, N_HEADS, s, HEAD_PAD), BF16),
                   jax.ShapeDtypeStruct((b, N_HEADS, s, HEAD_PAD), BF16),
                   jax.ShapeDtypeStruct((b, s, d), BF16),
                   jax.ShapeDtypeStruct((b, s, d), BF16),
                   jax.ShapeDtypeStruct((b, s, LANES), F32)],
        scratch_shapes=[pltpu.VMEM((1, LANES), F32)],
        compiler_params=_params("arbitrary", "arbitrary"),
        name="inproj",
    )(x, mod, *consts)


def _skip_count(cs_ref, ce_ref, base, i, thresh):
    c0 = cs_ref[base + i]
    return lax.fori_loop(0, i, lambda j, n: n + (c0 - ce_ref[base + j] < thresh).astype(jnp.int32), 0)


def _attn_kernel(cs_ref, ce_ref, prm_ref, q_ref, k_ref, v_ref, o_ref, acc_ref, m_ref):
    t = ATTN_TILE
    nb = q_ref.shape[2] // t
    base = (pl.program_id(0) * pl.num_programs(1) + pl.program_id(1)) * nb
    bound = prm_ref[0]
    thresh = -(EXP_UNDERFLOW + 2.0 * bound)
    rows = lax.broadcasted_iota(jnp.int32, (t, t), 0)
    cols = lax.broadcasted_iota(jnp.int32, (t, t), 1)
    causal = rows >= cols

    def scores(q, j):
        k0 = pl.multiple_of(j * t, t)
        return _dot_nt(q, k_ref[0, 0, pl.ds(k0, t), :]), v_ref[0, 0, pl.ds(k0, t), :]

    def finish(q0):
        acc = acc_ref[...]
        o_ref[0, pl.ds(q0, t), :] = (acc / acc[:, HEAD_DIM:HEAD_DIM + 1]).astype(BF16)

    def fixed_shift(i, carry):
        q0 = pl.multiple_of(i * t, t)
        q = q_ref[0, 0, pl.ds(q0, t), :]
        acc_ref[...] = jnp.zeros_like(acc_ref)

        def kv(j, c):
            s, vb = scores(q, j)
            acc_ref[...] += _dot(jnp.exp(s).astype(BF16), vb)
            return c

        lax.fori_loop(_skip_count(cs_ref, ce_ref, base, i, thresh), i, kv, 0)
        s, vb = scores(q, i)
        acc_ref[...] += _dot(jnp.exp(jnp.where(causal, s, NEG_BIG)).astype(BF16), vb)
        finish(q0)
        return carry

    def running_max(i, carry):
        q0 = pl.multiple_of(i * t, t)
        q = q_ref[0, 0, pl.ds(q0, t), :]
        m_ref[...] = jnp.full_like(m_ref, -jnp.inf)
        acc_ref[...] = jnp.zeros_like(acc_ref)

        def step(j, masked):
            s, vb = scores(q, j)
            if masked:
                s = jnp.where(causal, s, NEG_BIG)
            m_prev = m_ref[...]
            m_new = jnp.maximum(m_prev, jnp.max(s, axis=-1, keepdims=True))
            p = jnp.exp(s - m_new)
            acc_ref[...] = jnp.exp(m_prev - m_new) * acc_ref[...] + _dot(p.astype(BF16), vb)
            m_ref[...] = m_new

        def kv(j, c):
            step(j, False)
            return c

        lax.fori_loop(_skip_count(cs_ref, ce_ref, base, i, thresh), i, kv, 0)
        step(i, True)
        finish(q0)
        return carry

    @pl.when(bound <= FIXED_SHIFT_BOUND)
    def _():
        lax.fori_loop(0, nb, fixed_shift, 0)

    @pl.when(bound > FIXED_SHIFT_BOUND)
    def _():
        lax.fori_loop(0, nb, running_max, 0)


def _attention(cs, ce, prm, qa, ka, v):
    b, nh, s, hp = qa.shape
    t = ATTN_TILE
    seq = pl.BlockSpec((1, 1, s, hp), lambda bi, hi, *_: (bi, hi, 0, 0))
    return pl.pallas_call(
        _attn_kernel,
        grid_spec=pltpu.PrefetchScalarGridSpec(
            num_scalar_prefetch=3,
            grid=(b, nh),
            in_specs=[seq, seq, seq],
            out_specs=pl.BlockSpec((1, s, hp), lambda bi, hi, *_: (bi, 0, hi)),
            scratch_shapes=[pltpu.VMEM((t, hp), F32), pltpu.VMEM((t, 1), F32)]),
        out_shape=jax.ShapeDtypeStruct((b, s, nh * hp), BF16),
        compiler_params=_params("arbitrary", "arbitrary"),
        name="attn",
    )(cs, ce, prm, qa, ka, v)


def _merge_kernel(u_ref, halo_ref, o_ref, sgc_ref, sga_ref, x_ref, mod_ref, wdw_ref, bdw_ref,
                  gng_ref, gnb_ref, gg_ref, wco_ref, wao_ref, wout_ref, x1_ref, buf_ref):
    tm = ROW_TILE
    halo = halo_ref[0].astype(F32)
    halo = jnp.where(pl.program_id(1) == 0, jnp.zeros_like(halo), halo)
    buf_ref[0:CONV_HALO, :] = halo
    buf_ref[CONV_HALO:, :] = u_ref[0].astype(F32)

    base = CONV_HALO - (CONV_WIDTH - 1)
    y = jnp.zeros((tm, CONV_DIM), F32)
    for j in range(CONV_WIDTH):
        y = y + wdw_ref[j:j + 1, :] * buf_ref[base + j:base + j + tm, :]
    y = y + bdw_ref[...]

    gg = gg_ref[...]
    y_hi, y_lo = _split2(y)
    dlt = y - (_dot(y_hi, gg) + _dot(y_lo, gg))
    s_hi, s_lo = _split2(dlt * dlt)
    var = _dot(s_hi, gg) + _dot(s_lo, gg)
    yn = dlt * lax.rsqrt(var + EPS) * gng_ref[...] + gnb_ref[...]
    y_conv = _dot(_silu(yn).astype(BF16), wco_ref[...])

    y_attn = _dot(o_ref[0], wao_ref[...])
    merged = sgc_ref[0].astype(F32) * y_conv + sga_ref[0].astype(F32) * y_attn
    mix = _dot(merged.astype(BF16), wout_ref[...])
    x1_ref[0] = x_ref[0] + mod_ref[0, 2:3, :] * mix


def _merge(u, o, sgc, sga, x, mod, wdw, bdw, gng, gnb, gg, wco, wao, wout):
    b, s, d = x.shape
    tm = ROW_TILE
    per = tm // CONV_HALO
    row = lambda w: pl.BlockSpec((1, tm, w), lambda bi, i: (bi, i, 0))
    consts = [wdw, bdw, gng, gnb, gg, wco, wao, wout]
    return pl.pallas_call(
        _merge_kernel,
        grid=(b, s // tm),
        in_specs=[row(CONV_DIM),
                  pl.BlockSpec((1, CONV_HALO, CONV_DIM),
                               lambda bi, i: (bi, jnp.maximum(i * per - 1, 0), 0)),
                  row(o.shape[2]), row(d), row(d), row(d),
                  pl.BlockSpec((1, 6, d), lambda bi, i: (bi, 0, 0))]
                 + [_const_spec(a.shape) for a in consts],
        out_specs=row(d),
        out_shape=jax.ShapeDtypeStruct((b, s, d), F32),
        scratch_shapes=[pltpu.VMEM((CONV_HALO + tm, CONV_DIM), F32)],
        compiler_params=_params("arbitrary", "arbitrary"),
        name="merge",
    )(u, u, o, sgc, sga, x, mod, *consts)


def _router_kernel(x1_ref, mod_ref, g2_ref, wr_hi_ref, wr_lo_ref, rb_ref, h2_ref, comb_ref):
    x = x1_ref[0]
    ms = jnp.mean(x * x, axis=-1, keepdims=True)
    h = (x * lax.rsqrt(ms + EPS) * g2_ref[...]) * (1.0 + mod_ref[0, 4:5, :]) + mod_ref[0, 3:4, :]
    h2_ref[0] = h.astype(BF16)

    h_hi, h_lo = _split2(h)
    logits = _dot_nt(wr_hi_ref[...], h_hi) + _dot_nt(wr_hi_ref[...], h_lo) + _dot_nt(wr_lo_ref[...], h_hi)
    scores = _sigmoid(logits)
    biased = scores + rb_ref[...]

    per = N_EXPERTS // N_GROUPS
    rows = lax.broadcasted_iota(jnp.int32, (per, biased.shape[1]), 0)
    gscore = []
    for g in range(N_GROUPS):
        blk = biased[g * per:(g + 1) * per, :]
        top1 = jnp.max(blk, axis=0, keepdims=True)
        first = jnp.min(jnp.where(blk == top1, rows, per), axis=0, keepdims=True)
        top2 = jnp.max(jnp.where(rows == first, -jnp.inf, blk), axis=0, keepdims=True)
        gscore.append(top1 + top2)

    cand = []
    for g in range(N_GROUPS):
        rank = jnp.zeros_like(gscore[g], dtype=jnp.int32)
        for g2 in range(N_GROUPS):
            if g2 == g:
                continue
            ahead = gscore[g2] > gscore[g]
            if g2 < g:
                ahead = ahead | (gscore[g2] == gscore[g])
            rank = rank + ahead.astype(jnp.int32)
        keep = rank < TOPK_GROUPS
        cand.append(jnp.where(keep, biased[g * per:(g + 1) * per, :], -jnp.inf))
    cand = jnp.concatenate(cand, axis=0)

    eidx = lax.broadcasted_iota(jnp.int32, cand.shape, 0)
    rank = jnp.zeros(cand.shape, jnp.int32)
    for e2 in range(N_EXPERTS):
        other = cand[e2:e2 + 1, :]
        ahead = (other > cand) | ((other == cand) & (eidx > e2))
        rank = rank + ahead.astype(jnp.int32)
    sel = (rank < TOP_K) & (cand > -jnp.inf)
    w = jnp.where(sel, scores, 0.0)
    comb_ref[...] = w / jnp.sum(w, axis=0, keepdims=True) * ROUTED_SCALE


def _router(x1, mod, g2, wr_hi, wr_lo, rb):
    b, s, d = x1.shape
    tm = ROW_TILE
    nt = s // tm
    return pl.pallas_call(
        _router_kernel,
        grid=(b, nt),
        in_specs=[pl.BlockSpec((1, tm, d), lambda bi, i: (bi, i, 0)),
                  pl.BlockSpec((1, 6, d), lambda bi, i: (bi, 0, 0)),
                  _const_spec(g2.shape), _const_spec(wr_hi.shape), _const_spec(wr_lo.shape),
                  _const_spec(rb.shape)],
        out_specs=[pl.BlockSpec((1, tm, d), lambda bi, i: (bi, i, 0)),
                   pl.BlockSpec((N_EXPERTS, tm), lambda bi, i: (0, bi * nt + i))],
        out_shape=[jax.ShapeDtypeStruct((b, s, d), BF16),
                   jax.ShapeDtypeStruct((N_EXPERTS, b * s), F32)],
        compiler_params=_params("arbitrary", "arbitrary"),
        name="router",
    )(x1, mod, g2, wr_hi, wr_lo, rb)


def _moe_kernel(h_ref, comb_ref, wgu_ref, wd_ref, wsgu_ref, wsd_ref, x1_ref, mod_ref, out_ref, acc_ref):
    e = pl.program_id(1)
    h = h_ref[...]

    @pl.when(e == 0)
    def _():
        sgu = _dot(h, wsgu_ref[...])
        act = _silu(sgu[:, :SHARED_DIM]) * sgu[:, SHARED_DIM:]
        acc_ref[...] = _dot(act.astype(BF16), wsd_ref[...])

    pick = (lax.broadcasted_iota(jnp.int32, (N_EXPERTS, EXPERT_DIM), 0) == e).astype(BF16)
    c_hi, c_lo = _split2(comb_ref[...])
    gate = _dot(c_hi, pick) + _dot(c_lo, pick)

    gu = _dot(h, wgu_ref[0].astype(BF16))
    act = _silu(gu[:, :EXPERT_DIM]) * gu[:, EXPERT_DIM:] * gate
    acc_ref[...] += _dot(act.astype(BF16), wd_ref[0].astype(BF16))

    @pl.when(e == N_EXPERTS - 1)
    def _():
        out_ref[...] = x1_ref[...] + mod_ref[0, 5:6, :] * acc_ref[...]


def _moe(h2, comb, wgu, wd, wsgu, wsd, x1, mod, tiles_per_batch):
    t, d = h2.shape
    tm = MOE_TILE
    return pl.pallas_call(
        _moe_kernel,
        grid=(t // tm, N_EXPERTS),
        in_specs=[pl.BlockSpec((tm, d), lambda ti, e: (ti, 0)),
                  pl.BlockSpec((tm, N_EXPERTS), lambda ti, e: (ti, 0)),
                  pl.BlockSpec((1, d, 2 * EXPERT_DIM), lambda ti, e: (e, 0, 0)),
                  pl.BlockSpec((1, EXPERT_DIM, d), lambda ti, e: (e, 0, 0)),
                  _const_spec(wsgu.shape), _const_spec(wsd.shape),
                  pl.BlockSpec((tm, d), lambda ti, e: (ti, 0)),
                  pl.BlockSpec((1, 6, d), lambda ti, e: (ti // tiles_per_batch, 0, 0))],
        out_specs=pl.BlockSpec((tm, d), lambda ti, e: (ti, 0)),
        out_shape=jax.ShapeDtypeStruct((t, d), F32),
        scratch_shapes=[pltpu.VMEM((tm, d), F32)],
        compiler_params=_params("arbitrary", "arbitrary"),
        name="moe",
    )(h2, comb, wgu, wd, wsgu, wsd, x1, mod)


def _pad_heads(w):
    d = w.shape[0]
    w = w.reshape(d, N_HEADS, HEAD_DIM)
    return jnp.pad(w, ((0, 0), (0, 0), (0, HEAD_PAD - HEAD_DIM))).reshape(d, N_HEADS * HEAD_PAD)


def _placement():
    pq = np.zeros((3, LANES, N_HEADS * HEAD_PAD), np.float32)
    pk = np.zeros((3, LANES, N_HEADS * HEAD_PAD), np.float32)
    cq = np.zeros((1, N_HEADS * HEAD_PAD), np.float32)
    ck = np.zeros((1, N_HEADS * HEAD_PAD), np.float32)
    for hd in range(N_HEADS):
        for k in range(3):
            pq[k, hd, hd * HEAD_PAD + AUG0 + k] = 1.0
            ck[0, hd * HEAD_PAD + AUG0 + k] = 1.0
            pk[k, hd, hd * HEAD_PAD + AUG0 + 3 + k] = -1.0
            cq[0, hd * HEAD_PAD + AUG0 + 3 + k] = 1.0
    return (jnp.asarray(pq, BF16), jnp.asarray(pk, BF16), jnp.asarray(cq), jnp.asarray(ck))


def kernel(x, c, w_ada, b_ada, norm1_g, w_in, w_dw, b_dw, conv_gn_g, conv_gn_b, w_conv_out,
           q_norm_g, k_norm_g, b_forget, w_attn_out, w_out, norm2_g, w_router, router_bias,
           w_experts_gate_up, w_experts_down, w_shared_gate_up, w_shared_down):
    depth = w_ada.shape[0]
    b, s, d = x.shape
    off_q = 2 * CONV_DIM
    off_k = off_q + ATTN_DIM
    off_v = off_k + ATTN_DIM
    off_f = off_v + ATTN_DIM
    off_gc = off_f + N_HEADS
    off_ga = off_gc + d

    pq, pk, cq, ck = _placement()
    tri = jnp.asarray(np.tril(np.ones((ROW_TILE, ROW_TILE), np.float32)), BF16)
    grp = np.arange(CONV_DIM) // (CONV_DIM // CONV_GROUPS)
    gg = jnp.asarray((grp[:, None] == grp[None, :]).astype(np.float32) / (CONV_DIM // CONV_GROUPS), BF16)
    c_pad = jnp.pad(c, ((0, SUBLANES - b), (0, 0)))

    for l in range(depth):
        mod = _ada(c_pad, w_ada[l], b_ada[l][None, :])[:b].reshape(b, 6, d)

        wi = w_in[l]
        bf = jnp.pad(b_forget[l][None, :], ((0, 0), (0, LANES - N_HEADS)))
        wf = jnp.pad(wi[:, off_f:off_gc], ((0, 0), (0, LANES - N_HEADS))).astype(BF16)
        gpad = lambda g, sc: jnp.tile(jnp.pad(g * sc, (0, HEAD_PAD - HEAD_DIM)), N_HEADS)[None, :]
        qscale = HEAD_DIM ** -0.5
        u, qa, ka, v, sgc, sga, cum = _inproj(
            x, mod, norm1_g[l][None, :],
            wi[:, :off_q].astype(BF16),
            _pad_heads(wi[:, off_q:off_k]).astype(BF16),
            _pad_heads(wi[:, off_k:off_v]).astype(BF16),
            _pad_heads(wi[:, off_v:off_f]).astype(BF16),
            wf, wi[:, off_gc:off_ga].astype(BF16), wi[:, off_ga:].astype(BF16),
            bf, gpad(q_norm_g[l], qscale), gpad(k_norm_g[l], 1.0),
            pq, pk, cq, ck, tri)

        flat = lambda a: a[:, :, :N_HEADS].transpose(0, 2, 1).reshape(-1)
        cs = flat(cum[:, 0::ATTN_TILE])
        ce = flat(cum[:, ATTN_TILE - 1::ATTN_TILE])
        bound = (1.02 * HEAD_DIM * qscale) * jnp.max(jnp.abs(q_norm_g[l])) * jnp.max(jnp.abs(k_norm_g[l]))
        o = _attention(cs, ce, bound.reshape(1), qa, ka, v)

        wao = jnp.pad(w_attn_out[l].reshape(N_HEADS, HEAD_DIM, d),
                      ((0, 0), (0, HEAD_PAD - HEAD_DIM), (0, 0))).reshape(N_HEADS * HEAD_PAD, d)
        wdw = jnp.pad(w_dw[l], ((0, CONV_HALO - CONV_WIDTH), (0, 0)))
        x1 = _merge(u, o, sgc, sga, x, mod, wdw, b_dw[l][None, :], conv_gn_g[l][None, :],
                    conv_gn_b[l][None, :], gg, w_conv_out[l].astype(BF16), wao.astype(BF16),
                    w_out[l].astype(BF16))

        wr = w_router[l].T
        wr_hi = wr.astype(BF16)
        wr_lo = (wr - wr_hi.astype(F32)).astype(BF16)
        h2, comb_t = _router(x1, mod, norm2_g[l][None, :], wr_hi, wr_lo, router_bias[l][:, None])

        out = _moe(h2.reshape(b * s, d), comb_t.T, w_experts_gate_up[l], w_experts_down[l],
                   w_shared_gate_up[l].astype(BF16), w_shared_down[l].astype(BF16),
                   x1.reshape(b * s, d), mod, s // MOE_TILE)
        x = out.reshape(b, s, d)
    return x
```

```python
import functools

import numpy as np
import jax
import jax.numpy as jnp
from jax import lax
from jax.experimental import pallas as pl
from jax.experimental.pallas import tpu as pltpu
from jax.experimental.pallas import tpu_sc as plsc

F32 = jnp.float32
BF16 = jnp.bfloat16

CONV_DIM = 512
CONV_WIDTH = 31
CONV_GROUPS = 8
N_HEADS = 8
HEAD_DIM = 64
ATTN_DIM = N_HEADS * HEAD_DIM
N_EXPERTS = 64
TOP_K = 8
N_GROUPS = 8
TOPK_GROUPS = 4
EXPERT_DIM = 256
SHARED_DIM = 256
ROUTED_SCALE = 2.5
EPS = 1e-6

LANES = 128
SUBLANES = 8
HEAD_PAD = LANES
AUG0 = HEAD_DIM
VMEM_LIMIT = 56 * 1024 * 1024

ROW_TILE = 256
ATTN_TILE = 512
EXPERT_TILE = 512
POS_TILE = 512
SC_CHUNK = 128
CONV_HALO = 32

NEG_BIG = -1e30
EXP_UNDERFLOW = 104.0
FIXED_SHIFT_BOUND = 40.0


def _dot(a, b):
    return jnp.dot(a, b, preferred_element_type=F32)


def _dot_nt(a, b):
    return lax.dot_general(a, b, (((1,), (1,)), ((), ())), preferred_element_type=F32)


def _split2(x):
    hi = x.astype(BF16)
    lo = (x - hi.astype(F32)).astype(BF16)
    return hi, lo


def _split3(x):
    hi = x.astype(BF16)
    r = x - hi.astype(F32)
    mid = r.astype(BF16)
    lo = (r - mid.astype(F32)).astype(BF16)
    return hi, mid, lo


def _pack_halves(v):
    n = v.shape[1] // 2
    lo = lax.bitcast_convert_type(v[:, :n].astype(BF16).astype(F32), jnp.uint32)
    hi = lax.bitcast_convert_type(v[:, n:].astype(BF16).astype(F32), jnp.uint32)
    return lax.bitcast_convert_type(hi | lax.shift_right_logical(lo, jnp.uint32(16)), jnp.int32)


def _unpack_halves(w):
    u = lax.bitcast_convert_type(w, jnp.uint32)
    lo = lax.bitcast_convert_type(lax.shift_left(u, jnp.uint32(16)), F32)
    hi = lax.bitcast_convert_type(u & jnp.uint32(0xFFFF0000), F32)
    return lo, hi


def _sigmoid(x):
    return 1.0 / (1.0 + jnp.exp(-x))


def _silu(x):
    return x * _sigmoid(x)


def _params(*sem):
    return pltpu.CompilerParams(dimension_semantics=sem, vmem_limit_bytes=VMEM_LIMIT)


def _const_spec(shape):
    n = len(shape)
    return pl.BlockSpec(shape, lambda *_: (0,) * n)


def _ada_kernel(c_ref, w_ref, b_ref, o_ref):
    c = c_ref[...]
    a_hi, a_lo = _split2(_silu(c))
    w_hi, w_lo = _split2(w_ref[...])
    o_ref[...] = _dot(a_hi, w_hi) + _dot(a_hi, w_lo) + _dot(a_lo, w_hi) + b_ref[...]


def _ada(c_pad, w_ada, b_ada):
    d = c_pad.shape[1]
    n = w_ada.shape[1]
    return pl.pallas_call(
        _ada_kernel,
        grid=(n // d,),
        in_specs=[_const_spec(c_pad.shape),
                  pl.BlockSpec((d, d), lambda j: (0, j)),
                  pl.BlockSpec((1, d), lambda j: (0, j))],
        out_specs=pl.BlockSpec((c_pad.shape[0], d), lambda j: (0, j)),
        out_shape=jax.ShapeDtypeStruct((c_pad.shape[0], n), F32),
        compiler_params=_params("arbitrary"),
        name="ada",
    )(c_pad, w_ada, b_ada)


def _inproj_kernel(x_ref, mod_ref, g1_ref, wcv_ref, wq_ref, wk_ref, wv_ref, wf_ref, wgc_ref,
                   wga_ref, bf_ref, qg_ref, kg_ref, pq_ref, pk_ref, cq_ref, ck_ref, tri_ref,
                   u_ref, qa_ref, ka_ref, v_ref, sgc_ref, sga_ref, cum_ref, carry_ref):
    @pl.when(pl.program_id(1) == 0)
    def _():
        carry_ref[...] = jnp.zeros_like(carry_ref)

    x = x_ref[0]
    ms = jnp.mean(x * x, axis=-1, keepdims=True)
    h = (x * lax.rsqrt(ms + EPS) * g1_ref[...]) * (1.0 + mod_ref[0, 1:2, :]) + mod_ref[0, 0:1, :]
    hb = h.astype(BF16)

    pc = _dot(hb, wcv_ref[...])
    u_ref[0] = (pc[:, :CONV_DIM] * _sigmoid(pc[:, CONV_DIM:])).astype(BF16)

    sgc_ref[0] = _sigmoid(_dot(hb, wgc_ref[...])).astype(BF16)
    sga_ref[0] = _sigmoid(_dot(hb, wga_ref[...])).astype(BF16)

    z = _dot(hb, wf_ref[...]) + bf_ref[...]
    lf = jnp.minimum(z, 0.0) - jnp.log1p(jnp.exp(-jnp.abs(z)))
    l_hi, l_mid, l_lo = _split3(lf)
    tri = tri_ref[...]
    cum = _dot(tri, l_hi) + _dot(tri, l_mid) + _dot(tri, l_lo) + carry_ref[...]
    carry_ref[...] = cum[ROW_TILE - 1:ROW_TILE, :]
    cum_ref[0] = cum

    c_hi, c_mid, c_lo = _split3(cum)
    addq = _dot(c_hi, pq_ref[0]) + _dot(c_mid, pq_ref[1]) + _dot(c_lo, pq_ref[2]) + cq_ref[...]
    addk = _dot(c_hi, pk_ref[0]) + _dot(c_mid, pk_ref[1]) + _dot(c_lo, pk_ref[2]) + ck_ref[...]

    pq = _dot(hb, wq_ref[...])
    pk = _dot(hb, wk_ref[...])
    pv = _dot(hb, wv_ref[...])
    inv_hd = 1.0 / HEAD_DIM
    vone = (lax.broadcasted_iota(jnp.int32, (1, HEAD_PAD), 1) == HEAD_DIM).astype(F32)
    for hd in range(N_HEADS):
        sl = slice(hd * HEAD_PAD, (hd + 1) * HEAD_PAD)
        qb = pq[:, sl]
        qn = qb * lax.rsqrt(jnp.sum(qb * qb, axis=-1, keepdims=True) * inv_hd + EPS) * qg_ref[:, sl]
        qa_ref[0, hd] = (qn + addq[:, sl]).astype(BF16)
        kb = pk[:, sl]
        kn = kb * lax.rsqrt(jnp.sum(kb * kb, axis=-1, keepdims=True) * inv_hd + EPS) * kg_ref[:, sl]
        ka_ref[0, hd] = (kn + addk[:, sl]).astype(BF16)
        v_ref[0, hd] = (pv[:, sl] + vone).astype(BF16)


def _inproj(x, mod, g1, wcv, wq, wk, wv, wf, wgc, wga, bf, qg, kg, pq, pk, cq, ck, tri):
    b, s, d = x.shape
    tm = ROW_TILE
    hp = N_HEADS * HEAD_PAD
    row = lambda w: pl.BlockSpec((1, tm, w), lambda bi, i: (bi, i, 0))
    head = pl.BlockSpec((1, N_HEADS, tm, HEAD_PAD), lambda bi, i: (bi, 0, i, 0))
    consts = [g1, wcv, wq, wk, wv, wf, wgc, wga, bf, qg, kg, pq, pk, cq, ck, tri]
    return pl.pallas_call(
        _inproj_kernel,
        grid=(b, s // tm),
        in_specs=[row(d), pl.BlockSpec((1, 6, d), lambda bi, i: (bi, 0, 0))]
                 + [_const_spec(a.shape) for a in consts],
        out_specs=[row(CONV_DIM), head, head, head, row(d), row(d), row(LANES)],
        out_shape=[jax.ShapeDtypeStruct((b, s, CONV_DIM), BF16),
                   jax.ShapeDtypeStruct((b, N_HEADS, s, HEAD_PAD), BF16),
                   jax.ShapeDtypeStruct((b, N_HEADS, s, HEAD_PAD), BF16),
                   jax.ShapeDtypeStruct((b, N_HEADS, s, HEAD_PAD), BF16),
                   jax.ShapeDtypeStruct((b, s, d), BF16),
                   jax.ShapeDtypeStruct((b, s, d), BF16),
                   jax.ShapeDtypeStruct((b, s, LANES), F32)],
        scratch_shapes=[pltpu.VMEM((1, LANES), F32)],
        compiler_params=_params("arbitrary", "arbitrary"),
        name="inproj",
    )(x, mod, *consts)


def _skip_count(cs_ref, ce_ref, base, i, thresh):
    c0 = cs_ref[base + i]
    return lax.fori_loop(0, i, lambda j, n: n + (c0 - ce_ref[base + j] < thresh).astype(jnp.int32), 0)


def _attn_kernel(cs_ref, ce_ref, prm_ref, q_ref, k_ref, v_ref, o_ref, acc_ref, m_ref):
    t = ATTN_TILE
    nb = q_ref.shape[2] // t
    base = (pl.program_id(0) * pl.num_programs(1) + pl.program_id(1)) * nb
    bound = prm_ref[0]
    thresh = -(EXP_UNDERFLOW + 2.0 * bound)
    rows = lax.broadcasted_iota(jnp.int32, (t, t), 0)
    cols = lax.broadcasted_iota(jnp.int32, (t, t), 1)
    causal = rows >= cols

    def scores(q, j):
        k0 = pl.multiple_of(j * t, t)
        return _dot_nt(q, k_ref[0, 0, pl.ds(k0, t), :]), v_ref[0, 0, pl.ds(k0, t), :]

    def finish(q0):
        acc = acc_ref[...]
        o_ref[0, pl.ds(q0, t), :] = (acc / acc[:, HEAD_DIM:HEAD_DIM + 1]).astype(BF16)

    def fixed_shift(i, carry):
        q0 = pl.multiple_of(i * t, t)
        q = q_ref[0, 0, pl.ds(q0, t), :]
        acc_ref[...] = jnp.zeros_like(acc_ref)

        def kv(j, c):
            s, vb = scores(q, j)
            acc_ref[...] += _dot(jnp.exp(s).astype(BF16), vb)
            return c

        lax.fori_loop(_skip_count(cs_ref, ce_ref, base, i, thresh), i, kv, 0)
        s, vb = scores(q, i)
        acc_ref[...] += _dot(jnp.exp(jnp.where(causal, s, NEG_BIG)).astype(BF16), vb)
        finish(q0)
        return carry

    def running_max(i, carry):
        q0 = pl.multiple_of(i * t, t)
        q = q_ref[0, 0, pl.ds(q0, t), :]
        m_ref[...] = jnp.full_like(m_ref, -jnp.inf)
        acc_ref[...] = jnp.zeros_like(acc_ref)

        def step(j, masked):
            s, vb = scores(q, j)
            if masked:
                s = jnp.where(causal, s, NEG_BIG)
            m_prev = m_ref[...]
            m_new = jnp.maximum(m_prev, jnp.max(s, axis=-1, keepdims=True))
            p = jnp.exp(s - m_new)
            acc_ref[...] = jnp.exp(m_prev - m_new) * acc_ref[...] + _dot(p.astype(BF16), vb)
            m_ref[...] = m_new

        def kv(j, c):
            step(j, False)
            return c

        lax.fori_loop(_skip_count(cs_ref, ce_ref, base, i, thresh), i, kv, 0)
        step(i, True)
        finish(q0)
        return carry

    @pl.when(bound <= FIXED_SHIFT_BOUND)
    def _():
        lax.fori_loop(0, nb, fixed_shift, 0)

    @pl.when(bound > FIXED_SHIFT_BOUND)
    def _():
        lax.fori_loop(0, nb, running_max, 0)


def _attention(cs, ce, prm, qa, ka, v):
    b, nh, s, hp = qa.shape
    t = ATTN_TILE
    seq = pl.BlockSpec((1, 1, s, hp), lambda bi, hi, *_: (bi, hi, 0, 0))
    return pl.pallas_call(
        _attn_kernel,
        grid_spec=pltpu.PrefetchScalarGridSpec(
            num_scalar_prefetch=3,
            grid=(b, nh),
            in_specs=[seq, seq, seq],
            out_specs=pl.BlockSpec((1, s, hp), lambda bi, hi, *_: (bi, 0, hi)),
            scratch_shapes=[pltpu.VMEM((t, hp), F32), pltpu.VMEM((t, 1), F32)]),
        out_shape=jax.ShapeDtypeStruct((b, s, nh * hp), BF16),
        compiler_params=_params("arbitrary", "arbitrary"),
        name="attn",
    )(cs, ce, prm, qa, ka, v)


def _merge_kernel(u_ref, halo_ref, o_ref, sgc_ref, sga_ref, x_ref, mod_ref, wdw_ref, bdw_ref,
                  gng_ref, gnb_ref, gg_ref, wco_ref, wao_ref, wout_ref, x1_ref, buf_ref):
    tm = ROW_TILE
    halo = halo_ref[0].astype(F32)
    halo = jnp.where(pl.program_id(1) == 0, jnp.zeros_like(halo), halo)
    buf_ref[0:CONV_HALO, :] = halo
    buf_ref[CONV_HALO:, :] = u_ref[0].astype(F32)

    base = CONV_HALO - (CONV_WIDTH - 1)
    y = jnp.zeros((tm, CONV_DIM), F32)
    for j in range(CONV_WIDTH):
        y = y + wdw_ref[j:j + 1, :] * buf_ref[base + j:base + j + tm, :]
    y = y + bdw_ref[...]

    gg = gg_ref[...]
    y_hi, y_lo = _split2(y)
    dlt = y - (_dot(y_hi, gg) + _dot(y_lo, gg))
    s_hi, s_lo = _split2(dlt * dlt)
    var = _dot(s_hi, gg) + _dot(s_lo, gg)
    yn = dlt * lax.rsqrt(var + EPS) * gng_ref[...] + gnb_ref[...]
    y_conv = _dot(_silu(yn).astype(BF16), wco_ref[...])

    y_attn = _dot(o_ref[0], wao_ref[...])
    merged = sgc_ref[0].astype(F32) * y_conv + sga_ref[0].astype(F32) * y_attn
    mix = _dot(merged.astype(BF16), wout_ref[...])
    x1_ref[0] = x_ref[0] + mod_ref[0, 2:3, :] * mix


def _merge(u, o, sgc, sga, x, mod, wdw, bdw, gng, gnb, gg, wco, wao, wout):
    b, s, d = x.shape
    tm = ROW_TILE
    per = tm // CONV_HALO
    row = lambda w: pl.BlockSpec((1, tm, w), lambda bi, i: (bi, i, 0))
    consts = [wdw, bdw, gng, gnb, gg, wco, wao, wout]
    return pl.pallas_call(
        _merge_kernel,
        grid=(b, s // tm),
        in_specs=[row(CONV_DIM),
                  pl.BlockSpec((1, CONV_HALO, CONV_DIM),
                               lambda bi, i: (bi, jnp.maximum(i * per - 1, 0), 0)),
                  row(o.shape[2]), row(d), row(d), row(d),
                  pl.BlockSpec((1, 6, d), lambda bi, i: (bi, 0, 0))]
                 + [_const_spec(a.shape) for a in consts],
        out_specs=row(d),
        out_shape=jax.ShapeDtypeStruct((b, s, d), F32),
        scratch_shapes=[pltpu.VMEM((CONV_HALO + tm, CONV_DIM), F32)],
        compiler_params=_params("arbitrary", "arbitrary"),
        name="merge",
    )(u, u, o, sgc, sga, x, mod, *consts)


def _router_kernel(x1_ref, mod_ref, g2_ref, wr_hi_ref, wr_lo_ref, rb_ref, h2_ref, h2w_ref, comb_ref):
    x = x1_ref[0]
    ms = jnp.mean(x * x, axis=-1, keepdims=True)
    h = (x * lax.rsqrt(ms + EPS) * g2_ref[...]) * (1.0 + mod_ref[0, 4:5, :]) + mod_ref[0, 3:4, :]
    h2_ref[0] = h.astype(BF16)
    h2w_ref[0] = _pack_halves(h)

    h_hi, h_lo = _split2(h)
    logits = _dot_nt(wr_hi_ref[...], h_hi) + _dot_nt(wr_hi_ref[...], h_lo) + _dot_nt(wr_lo_ref[...], h_hi)
    scores = _sigmoid(logits)
    biased = scores + rb_ref[...]

    per = N_EXPERTS // N_GROUPS
    rows = lax.broadcasted_iota(jnp.int32, (per, biased.shape[1]), 0)
    gscore = []
    for g in range(N_GROUPS):
        blk = biased[g * per:(g + 1) * per, :]
        top1 = jnp.max(blk, axis=0, keepdims=True)
        first = jnp.min(jnp.where(blk == top1, rows, per), axis=0, keepdims=True)
        top2 = jnp.max(jnp.where(rows == first, -jnp.inf, blk), axis=0, keepdims=True)
        gscore.append(top1 + top2)

    cand = []
    for g in range(N_GROUPS):
        rank = jnp.zeros_like(gscore[g], dtype=jnp.int32)
        for g2 in range(N_GROUPS):
            if g2 == g:
                continue
            ahead = gscore[g2] > gscore[g]
            if g2 < g:
                ahead = ahead | (gscore[g2] == gscore[g])
            rank = rank + ahead.astype(jnp.int32)
        keep = rank < TOPK_GROUPS
        cand.append(jnp.where(keep, biased[g * per:(g + 1) * per, :], -jnp.inf))
    cand = jnp.concatenate(cand, axis=0)

    eidx = lax.broadcasted_iota(jnp.int32, cand.shape, 0)
    rank = jnp.zeros(cand.shape, jnp.int32)
    for e2 in range(N_EXPERTS):
        other = cand[e2:e2 + 1, :]
        ahead = (other > cand) | ((other == cand) & (eidx > e2))
        rank = rank + ahead.astype(jnp.int32)
    sel = (rank < TOP_K) & (cand > -jnp.inf)
    w = jnp.where(sel, scores, 0.0)
    comb_ref[...] = w / jnp.sum(w, axis=0, keepdims=True) * ROUTED_SCALE


def _router(x1, mod, g2, wr_hi, wr_lo, rb):
    b, s, d = x1.shape
    tm = ROW_TILE
    nt = s // tm
    return pl.pallas_call(
        _router_kernel,
        grid=(b, nt),
        in_specs=[pl.BlockSpec((1, tm, d), lambda bi, i: (bi, i, 0)),
                  pl.BlockSpec((1, 6, d), lambda bi, i: (bi, 0, 0)),
                  _const_spec(g2.shape), _const_spec(wr_hi.shape), _const_spec(wr_lo.shape),
                  _const_spec(rb.shape)],
        out_specs=[pl.BlockSpec((1, tm, d), lambda bi, i: (bi, i, 0)),
                   pl.BlockSpec((1, tm, d // 2), lambda bi, i: (bi, i, 0)),
                   pl.BlockSpec((N_EXPERTS, tm), lambda bi, i: (0, bi * nt + i))],
        out_shape=[jax.ShapeDtypeStruct((b, s, d), BF16),
                   jax.ShapeDtypeStruct((b, s, d // 2), jnp.int32),
                   jax.ShapeDtypeStruct((N_EXPERTS, b * s), F32)],
        compiler_params=_params("arbitrary", "arbitrary"),
        name="router",
    )(x1, mod, g2, wr_hi, wr_lo, rb)


def _pos_kernel(comb_ref, tri_ref, lstrict_ref, posk_ref, wk_ref, gend_ref, cnt_ref, base_ref):
    phase = pl.program_id(0)
    i = pl.program_id(1)
    tp = comb_ref.shape[1]
    comb = comb_ref[...]
    sel = comb != 0.0
    selb = jnp.where(sel, 1.0, 0.0).astype(BF16)
    tile_count = _dot(selb, jnp.ones((tp, tp), BF16))

    @pl.when(phase == 0)
    def _():
        @pl.when(i == 0)
        def _():
            cnt_ref[...] = jnp.zeros_like(cnt_ref)
            gend_ref[...] = jnp.zeros_like(gend_ref)

        cnt_ref[...] += tile_count
        posk_ref[...] = jnp.zeros_like(posk_ref)
        wk_ref[...] = jnp.zeros_like(wk_ref)

    @pl.when(phase == 1)
    def _():
        @pl.when(i == 0)
        def _():
            seg = jnp.floor((cnt_ref[...] + (EXPERT_TILE - 1.0)) * (1.0 / EXPERT_TILE)) * EXPERT_TILE
            s_hi, s_mid, s_lo = _split3(seg)
            ls = lstrict_ref[...]
            start = _dot(ls, s_hi) + _dot(ls, s_mid) + _dot(ls, s_lo)
            base_ref[...] = start
            gend_ref[...] = (start + seg)[:, :LANES]

        rank = _dot(selb, tri_ref[...])
        pos = base_ref[...] + rank - 1.0
        base_ref[...] += tile_count
        slot = _dot(lstrict_ref[...], selb)
        rows_p, rows_w = [], []
        for k in range(TOP_K):
            m = sel & (slot == k)
            rows_p.append(jnp.sum(jnp.where(m, pos, 0.0), axis=0, keepdims=True))
            rows_w.append(jnp.sum(jnp.where(m, comb, 0.0), axis=0, keepdims=True))
        posk_ref[...] = jnp.concatenate(rows_p, axis=0).astype(jnp.int32)
        wk_ref[...] = jnp.concatenate(rows_w, axis=0)


def _positions(comb_t, tri, lstrict):
    ne, t = comb_t.shape
    tp = POS_TILE
    out_tok = pl.BlockSpec((TOP_K, tp), lambda p, i: (0, i * p))
    return pl.pallas_call(
        _pos_kernel,
        grid=(2, t // tp),
        in_specs=[pl.BlockSpec((ne, tp), lambda p, i: (0, i)),
                  _const_spec(tri.shape), _const_spec(lstrict.shape)],
        out_specs=[out_tok, out_tok, _const_spec((ne, LANES))],
        out_shape=[jax.ShapeDtypeStruct((TOP_K, t), jnp.int32),
                   jax.ShapeDtypeStruct((TOP_K, t), F32),
                   jax.ShapeDtypeStruct((ne, LANES), F32)],
        scratch_shapes=[pltpu.VMEM((ne, tp), F32), pltpu.VMEM((ne, tp), F32)],
        compiler_params=_params("arbitrary", "arbitrary"),
        name="positions",
    )(comb_t, tri, lstrict)


def _sc_workers():
    info = plsc.get_sparse_core_info()
    return info.num_cores, info.num_cores * info.num_subcores


def _sc_scatter_rows(rows, pos, n_out):
    nc, nw = _sc_workers()
    n, w = rows.shape
    nk = pos.shape[0]
    ch = SC_CHUNK
    per_w = n // nw
    assert per_w * nw == n and per_w % ch == 0

    @functools.partial(
        pl.kernel, mesh=plsc.VectorSubcoreMesh(core_axis_name="c", subcore_axis_name="s"),
        out_type=jax.ShapeDtypeStruct((n_out, w), rows.dtype),
        scratch_types=[pltpu.VMEM((nk, ch), jnp.int32), pltpu.VMEM((ch, w), rows.dtype),
                       pltpu.SemaphoreType.DMA])
    def scatter(rows_hbm, pos_hbm, out_hbm, idx_v, rows_v, sem):
        base = (lax.axis_index("s") * nc + lax.axis_index("c")) * per_w

        @pl.loop(0, per_w // ch)
        def _(ci):
            off = pl.multiple_of(base + ci * ch, ch)
            pltpu.sync_copy(pos_hbm.at[:, pl.ds(off, ch)], idx_v)
            pltpu.sync_copy(rows_hbm.at[pl.ds(off, ch)], rows_v)
            copies = [pltpu.make_async_copy(rows_v, out_hbm.at[idx_v.at[k]], sem) for k in range(nk)]
            for cp in copies:
                cp.start()
            for cp in copies:
                cp.wait()

    return scatter(rows, pos)


def _sc_gather_rows(table, idx):
    nc, nw = _sc_workers()
    n = idx.shape[0]
    w = table.shape[1]
    ch = SC_CHUNK
    per_w = n // nw
    assert per_w * nw == n and per_w % ch == 0

    @functools.partial(
        pl.kernel, mesh=plsc.VectorSubcoreMesh(core_axis_name="c", subcore_axis_name="s"),
        out_type=jax.ShapeDtypeStruct((n, w), table.dtype),
        scratch_types=[pltpu.VMEM((ch,), jnp.int32), pltpu.VMEM((ch, w), table.dtype),
                       pltpu.SemaphoreType.DMA])
    def gather(table_hbm, idx_hbm, out_hbm, idx_v, rows_v, sem):
        base = (lax.axis_index("s") * nc + lax.axis_index("c")) * per_w

        @pl.loop(0, per_w // ch)
        def _(ci):
            off = pl.multiple_of(base + ci * ch, ch)
            pltpu.sync_copy(idx_hbm.at[pl.ds(off, ch)], idx_v)
            cp = pltpu.make_async_copy(table_hbm.at[idx_v], rows_v, sem)
            cp.start()
            cp.wait()
            pltpu.sync_copy(rows_v, out_hbm.at[pl.ds(off, ch)])

    return gather(table, idx)


def _expert_kernel(te_ref, nu_ref, xs_ref, wgu_ref, wd_ref, y_ref, wgu_b, wd_b):
    i = pl.program_id(0)

    @pl.when(i < nu_ref[0])
    def _():
        @pl.when((i == 0) | (te_ref[i] != te_ref[jnp.maximum(i - 1, 0)]))
        def _():
            wgu_b[...] = wgu_ref[0].astype(BF16)
            wd_b[...] = wd_ref[0].astype(BF16)

        lo, hi = _unpack_halves(xs_ref[...])
        half = wgu_b.shape[0] // 2
        gu = _dot(lo.astype(BF16), wgu_b[:half, :]) + _dot(hi.astype(BF16), wgu_b[half:, :])
        act = _silu(gu[:, :EXPERT_DIM]) * gu[:, EXPERT_DIM:]
        y_ref[...] = _pack_halves(_dot(act.astype(BF16), wd_b[...]))


def _experts(tile_e, n_used, xs, wgu, wd):
    r, w = xs.shape
    tm = EXPERT_TILE
    d = wgu.shape[1]
    rows = pl.BlockSpec((tm, w), lambda i, te, nu: (jnp.minimum(i, nu[0] - 1), 0))
    return pl.pallas_call(
        _expert_kernel,
        grid_spec=pltpu.PrefetchScalarGridSpec(
            num_scalar_prefetch=2,
            grid=(r // tm,),
            in_specs=[rows,
                      pl.BlockSpec((1, d, 2 * EXPERT_DIM), lambda i, te, nu: (te[i], 0, 0)),
                      pl.BlockSpec((1, EXPERT_DIM, d), lambda i, te, nu: (te[i], 0, 0))],
            out_specs=rows,
            scratch_shapes=[pltpu.VMEM((d, 2 * EXPERT_DIM), BF16), pltpu.VMEM((EXPERT_DIM, d), BF16)]),
        out_shape=jax.ShapeDtypeStruct((r, w), jnp.int32),
        compiler_params=_params("arbitrary"),
        name="experts",
    )(tile_e, n_used, xs, wgu, wd)


def _final_kernel(yk_ref, wk_ref, h_ref, x1_ref, mod_ref, wsgu_ref, wsd_ref, out_ref):
    half = yk_ref.shape[2]
    acc_lo = jnp.zeros((yk_ref.shape[1], half), F32)
    acc_hi = jnp.zeros((yk_ref.shape[1], half), F32)
    for k in range(TOP_K):
        lo, hi = _unpack_halves(yk_ref[k])
        w = wk_ref[:, k:k + 1]
        acc_lo = acc_lo + jnp.where(w != 0.0, w * lo, 0.0)
        acc_hi = acc_hi + jnp.where(w != 0.0, w * hi, 0.0)
    routed = jnp.concatenate([acc_lo, acc_hi], axis=1)
    sgu = _dot(h_ref[...], wsgu_ref[...])
    act = _silu(sgu[:, :SHARED_DIM]) * sgu[:, SHARED_DIM:]
    shared = _dot(act.astype(BF16), wsd_ref[...])
    out_ref[...] = x1_ref[...] + mod_ref[0, 5:6, :] * (routed + shared)


def _final(yk, wk_t, h2, x1, mod, wsgu, wsd, tiles_per_batch):
    t, d = h2.shape
    tm = ROW_TILE
    row = lambda w: pl.BlockSpec((tm, w), lambda i: (i, 0))
    return pl.pallas_call(
        _final_kernel,
        grid=(t // tm,),
        in_specs=[pl.BlockSpec((TOP_K, tm, d // 2), lambda i: (0, i, 0)),
                  row(TOP_K), row(d), row(d),
                  pl.BlockSpec((1, 6, d), lambda i: (i // tiles_per_batch, 0, 0)),
                  _const_spec(wsgu.shape), _const_spec(wsd.shape)],
        out_specs=row(d),
        out_shape=jax.ShapeDtypeStruct((t, d), F32),
        compiler_params=_params("arbitrary"),
        name="final",
    )(yk, wk_t, h2, x1, mod, wsgu, wsd)


def _pad_heads(w):
    d = w.shape[0]
    w = w.reshape(d, N_HEADS, HEAD_DIM)
    return jnp.pad(w, ((0, 0), (0, 0), (0, HEAD_PAD - HEAD_DIM))).reshape(d, N_HEADS * HEAD_PAD)


def _placement():
    pq = np.zeros((3, LANES, N_HEADS * HEAD_PAD), np.float32)
    pk = np.zeros((3, LANES, N_HEADS * HEAD_PAD), np.float32)
    cq = np.zeros((1, N_HEADS * HEAD_PAD), np.float32)
    ck = np.zeros((1, N_HEADS * HEAD_PAD), np.float32)
    for hd in range(N_HEADS):
        for k in range(3):
            pq[k, hd, hd * HEAD_PAD + AUG0 + k] = 1.0
            ck[0, hd * HEAD_PAD + AUG0 + k] = 1.0
            pk[k, hd, hd * HEAD_PAD + AUG0 + 3 + k] = -1.0
            cq[0, hd * HEAD_PAD + AUG0 + 3 + k] = 1.0
    return (jnp.asarray(pq, BF16), jnp.asarray(pk, BF16), jnp.asarray(cq), jnp.asarray(ck))


def kernel(x, c, w_ada, b_ada, norm1_g, w_in, w_dw, b_dw, conv_gn_g, conv_gn_b, w_conv_out,
           q_norm_g, k_norm_g, b_forget, w_attn_out, w_out, norm2_g, w_router, router_bias,
           w_experts_gate_up, w_experts_down, w_shared_gate_up, w_shared_down):
    depth = w_ada.shape[0]
    b, s, d = x.shape
    off_q = 2 * CONV_DIM
    off_k = off_q + ATTN_DIM
    off_v = off_k + ATTN_DIM
    off_f = off_v + ATTN_DIM
    off_gc = off_f + N_HEADS
    off_ga = off_gc + d

    pq, pk, cq, ck = _placement()
    tri = jnp.asarray(np.tril(np.ones((ROW_TILE, ROW_TILE), np.float32)), BF16)
    grp = np.arange(CONV_DIM) // (CONV_DIM // CONV_GROUPS)
    gg = jnp.asarray((grp[:, None] == grp[None, :]).astype(np.float32) / (CONV_DIM // CONV_GROUPS), BF16)
    c_pad = jnp.pad(c, ((0, SUBLANES - b), (0, 0)))
    tri_pos = jnp.asarray(np.triu(np.ones((POS_TILE, POS_TILE), np.float32)), BF16)
    lstrict = jnp.asarray(np.tril(np.ones((N_EXPERTS, N_EXPERTS), np.float32), -1), BF16)

    for l in range(depth):
        mod = _ada(c_pad, w_ada[l], b_ada[l][None, :])[:b].reshape(b, 6, d)

        wi = w_in[l]
        bf = jnp.pad(b_forget[l][None, :], ((0, 0), (0, LANES - N_HEADS)))
        wf = jnp.pad(wi[:, off_f:off_gc], ((0, 0), (0, LANES - N_HEADS))).astype(BF16)
        gpad = lambda g, sc: jnp.tile(jnp.pad(g * sc, (0, HEAD_PAD - HEAD_DIM)), N_HEADS)[None, :]
        qscale = HEAD_DIM ** -0.5
        u, qa, ka, v, sgc, sga, cum = _inproj(
            x, mod, norm1_g[l][None, :],
            wi[:, :off_q].astype(BF16),
            _pad_heads(wi[:, off_q:off_k]).astype(BF16),
            _pad_heads(wi[:, off_k:off_v]).astype(BF16),
            _pad_heads(wi[:, off_v:off_f]).astype(BF16),
            wf, wi[:, off_gc:off_ga].astype(BF16), wi[:, off_ga:].astype(BF16),
            bf, gpad(q_norm_g[l], qscale), gpad(k_norm_g[l], 1.0),
            pq, pk, cq, ck, tri)

        flat = lambda a: a[:, :, :N_HEADS].transpose(0, 2, 1).reshape(-1)
        cs = flat(cum[:, 0::ATTN_TILE])
        ce = flat(cum[:, ATTN_TILE - 1::ATTN_TILE])
        bound = (1.02 * HEAD_DIM * qscale) * jnp.max(jnp.abs(q_norm_g[l])) * jnp.max(jnp.abs(k_norm_g[l]))
        o = _attention(cs, ce, bound.reshape(1), qa, ka, v)

        wao = jnp.pad(w_attn_out[l].reshape(N_HEADS, HEAD_DIM, d),
                      ((0, 0), (0, HEAD_PAD - HEAD_DIM), (0, 0))).reshape(N_HEADS * HEAD_PAD, d)
        wdw = jnp.pad(w_dw[l], ((0, CONV_HALO - CONV_WIDTH), (0, 0)))
        x1 = _merge(u, o, sgc, sga, x, mod, wdw, b_dw[l][None, :], conv_gn_g[l][None, :],
                    conv_gn_b[l][None, :], gg, w_conv_out[l].astype(BF16), wao.astype(BF16),
                    w_out[l].astype(BF16))

        wr = w_router[l].T
        wr_hi = wr.astype(BF16)
        wr_lo = (wr - wr_hi.astype(F32)).astype(BF16)
        h2, h2w, comb_t = _router(x1, mod, norm2_g[l][None, :], wr_hi, wr_lo, router_bias[l][:, None])

        t = b * s
        posk, wk, gend = _positions(comb_t, tri_pos, lstrict)
        n_tiles = (t * TOP_K) // EXPERT_TILE + N_EXPERTS
        seg_end = gend[:, 0].astype(jnp.int32)
        n_used = seg_end[-1:] // EXPERT_TILE
        tile_start = jnp.arange(n_tiles, dtype=jnp.int32) * EXPERT_TILE
        tile_start = jnp.minimum(tile_start, seg_end[-1] - EXPERT_TILE)
        tile_e = jnp.sum((seg_end[None, :] <= tile_start[:, None]).astype(jnp.int32), axis=1)

        xs = _sc_scatter_rows(h2w.reshape(t, d // 2), posk, n_tiles * EXPERT_TILE)
        ys = _experts(tile_e, n_used, xs, w_experts_gate_up[l], w_experts_down[l])
        yk = _sc_gather_rows(ys, posk.reshape(-1)).reshape(TOP_K, t, d // 2)
        out = _final(yk, wk.T, h2.reshape(t, d), x1.reshape(t, d), mod,
                     w_shared_gate_up[l].astype(BF16), w_shared_down[l].astype(BF16), s // ROW_TILE)
        x = out.reshape(b, s, d)
    return x
```

```python
import functools

import numpy as np
import jax
import jax.numpy as jnp
from jax import lax
from jax.experimental import pallas as pl
from jax.experimental.pallas import tpu as pltpu
from jax.experimental.pallas import tpu_sc as plsc

F32 = jnp.float32
BF16 = jnp.bfloat16

CONV_DIM = 512
CONV_WIDTH = 31
CONV_GROUPS = 8
N_HEADS = 8
HEAD_DIM = 64
ATTN_DIM = N_HEADS * HEAD_DIM
N_EXPERTS = 64
TOP_K = 8
N_GROUPS = 8
TOPK_GROUPS = 4
EXPERT_DIM = 256
SHARED_DIM = 256
ROUTED_SCALE = 2.5
EPS = 1e-6

LANES = 128
SUBLANES = 8
HEAD_PAD = LANES
AUG0 = HEAD_DIM
VMEM_LIMIT = 56 * 1024 * 1024

ROW_TILE = 512
ATTN_TILE = 512
EXPERT_TILE = 512
POS_TILE = 512
SC_CHUNK = 128
CONV_HALO = 32

NEG_BIG = -1e30
EXP_UNDERFLOW = 104.0
FIXED_SHIFT_BOUND = 40.0


def _dot(a, b):
    return jnp.dot(a, b, preferred_element_type=F32)


def _dot_nt(a, b):
    return lax.dot_general(a, b, (((1,), (1,)), ((), ())), preferred_element_type=F32)


def _split2(x):
    hi = x.astype(BF16)
    lo = (x - hi.astype(F32)).astype(BF16)
    return hi, lo


def _split3(x):
    hi = x.astype(BF16)
    r = x - hi.astype(F32)
    mid = r.astype(BF16)
    lo = (r - mid.astype(F32)).astype(BF16)
    return hi, mid, lo


def _pack_halves(v):
    n = v.shape[1] // 2
    lo = lax.bitcast_convert_type(v[:, :n].astype(BF16).astype(F32), jnp.uint32)
    hi = lax.bitcast_convert_type(v[:, n:].astype(BF16).astype(F32), jnp.uint32)
    return lax.bitcast_convert_type(hi | lax.shift_right_logical(lo, jnp.uint32(16)), jnp.int32)


def _unpack_halves(w):
    u = lax.bitcast_convert_type(w, jnp.uint32)
    lo = lax.bitcast_convert_type(lax.shift_left(u, jnp.uint32(16)), F32)
    hi = lax.bitcast_convert_type(u & jnp.uint32(0xFFFF0000), F32)
    return lo, hi


def _sigmoid(x):
    return 1.0 / (1.0 + jnp.exp(-x))


def _silu(x):
    return x * _sigmoid(x)


def _params(*sem):
    return pltpu.CompilerParams(dimension_semantics=sem, vmem_limit_bytes=VMEM_LIMIT)


def _const_spec(shape):
    n = len(shape)
    return pl.BlockSpec(shape, lambda *_: (0,) * n, pipeline_mode=pl.Buffered(1))


def _ada_kernel(c_ref, w_ref, b_ref, o_ref):
    c = c_ref[...]
    a_hi, a_lo = _split2(_silu(c))
    w_hi, w_lo = _split2(w_ref[...])
    o_ref[...] = _dot(a_hi, w_hi) + _dot(a_hi, w_lo) + _dot(a_lo, w_hi) + b_ref[...]


def _ada(c_pad, w_ada, b_ada):
    d = c_pad.shape[1]
    n = w_ada.shape[1]
    return pl.pallas_call(
        _ada_kernel,
        grid=(n // d,),
        in_specs=[_const_spec(c_pad.shape),
                  pl.BlockSpec((d, d), lambda j: (0, j)),
                  pl.BlockSpec((1, d), lambda j: (0, j))],
        out_specs=pl.BlockSpec((c_pad.shape[0], d), lambda j: (0, j)),
        out_shape=jax.ShapeDtypeStruct((c_pad.shape[0], n), F32),
        compiler_params=_params("arbitrary"),
        name="ada",
    )(c_pad, w_ada, b_ada)


def _lane_pieces(x):
    hi, mid, lo = _split3(x)
    return (hi.astype(F32) + pltpu.roll(mid.astype(F32), N_HEADS, 1)
            + pltpu.roll(lo.astype(F32), 2 * N_HEADS, 1)).astype(BF16)


def _head_tile(p, col0, hd):
    per = HEAD_PAD // HEAD_DIM
    g, part = divmod(hd, per)
    blk = p[:, col0 + g * HEAD_PAD:col0 + (g + 1) * HEAD_PAD]
    if part:
        blk = pltpu.roll(blk, HEAD_PAD - part * HEAD_DIM, 1)
    lane = lax.broadcasted_iota(jnp.int32, (1, HEAD_PAD), 1)
    return jnp.where(lane < HEAD_DIM, blk, 0.0)


def _inproj_kernel(x_ref, mod_ref, g1_ref, wcv_ref, wq_ref, wk_ref, wv_ref, wf_ref, wgc_ref,
                   wga_ref, bf_ref, qg_ref, kg_ref, pq_ref, pk_ref, cq_ref, ck_ref, tri_ref,
                   u_ref, qa_ref, ka_ref, v_ref, sgc_ref, sga_ref, cum_ref, carry_ref):
    @pl.when(pl.program_id(1) == 0)
    def _():
        carry_ref[...] = jnp.zeros_like(carry_ref)

    x = x_ref[0]
    ms = jnp.mean(x * x, axis=-1, keepdims=True)
    h = (x * lax.rsqrt(ms + EPS) * g1_ref[...]) * (1.0 + mod_ref[0, 1:2, :]) + mod_ref[0, 0:1, :]
    hb = h.astype(BF16)

    pc = _dot(hb, wcv_ref[...])
    u_ref[0] = (pc[:, :CONV_DIM] * _sigmoid(pc[:, CONV_DIM:])).astype(BF16)

    sgc_ref[0] = _sigmoid(_dot(hb, wgc_ref[...])).astype(BF16)
    sga_ref[0] = _sigmoid(_dot(hb, wga_ref[...])).astype(BF16)

    z = _dot(hb, wf_ref[...]) + bf_ref[...]
    lf = jnp.minimum(z, 0.0) - jnp.log1p(jnp.exp(-jnp.abs(z)))
    lane = lax.broadcasted_iota(jnp.int32, (1, LANES), 1)
    psum = _dot(tri_ref[...], _lane_pieces(jnp.where(lane < N_HEADS, lf, 0.0)))
    cum = (psum + pltpu.roll(psum, LANES - N_HEADS, 1) + pltpu.roll(psum, LANES - 2 * N_HEADS, 1)
           + carry_ref[...])
    cum = jnp.where(lane < N_HEADS, cum, 0.0)
    carry_ref[...] = cum[ROW_TILE - 1:ROW_TILE, :]
    cum_ref[0] = cum

    pieces = _lane_pieces(cum)
    addq = _dot(pieces, pq_ref[...]) + cq_ref[...]
    addk = _dot(pieces, pk_ref[...]) + ck_ref[...]

    pq = _dot(hb, wq_ref[...])
    pk = _dot(hb, wk_ref[...])
    pv = _dot(hb, wv_ref[...])
    inv_hd = 1.0 / HEAD_DIM
    vone = (lax.broadcasted_iota(jnp.int32, (1, HEAD_PAD), 1) == HEAD_DIM).astype(F32)
    for hd in range(N_HEADS):
        sl = slice(hd * HEAD_PAD, (hd + 1) * HEAD_PAD)
        qb = pq[:, sl]
        qn = qb * lax.rsqrt(jnp.sum(qb * qb, axis=-1, keepdims=True) * inv_hd + EPS) * qg_ref[:, sl]
        qa_ref[0, hd] = (qn + addq[:, sl]).astype(BF16)
        kb = pk[:, sl]
        kn = kb * lax.rsqrt(jnp.sum(kb * kb, axis=-1, keepdims=True) * inv_hd + EPS) * kg_ref[:, sl]
        ka_ref[0, hd] = (kn + addk[:, sl]).astype(BF16)
        v_ref[0, hd] = (_head_tile(pv, 0, hd) + vone).astype(BF16)


def _inproj(x, mod, g1, wcv, wq, wk, wv, wf, wgc, wga, bf, qg, kg, pq, pk, cq, ck, tri):
    b, s, d = x.shape
    tm = ROW_TILE
    hp = N_HEADS * HEAD_PAD
    row = lambda w: pl.BlockSpec((1, tm, w), lambda bi, i: (bi, i, 0))
    head = pl.BlockSpec((1, N_HEADS, tm, HEAD_PAD), lambda bi, i: (bi, 0, i, 0))
    consts = [g1, wcv, wq, wk, wv, wf, wgc, wga, bf, qg, kg, pq, pk, cq, ck, tri]
    return pl.pallas_call(
        _inproj_kernel,
        grid=(b, s // tm),
        in_specs=[row(d), pl.BlockSpec((1, 6, d), lambda bi, i: (bi, 0, 0))]
                 + [_const_spec(a.shape) for a in consts],
        out_specs=[row(CONV_DIM), head, head, head, row(d), row(d), row(LANES)],
        out_shape=[jax.ShapeDtypeStruct((b, s, CONV_DIM), BF16),
                   jax.ShapeDtypeStruct((b, N_HEADS, s, HEAD_PAD), BF16),
                   jax.ShapeDtypeStruct((b, N_HEADS, s, HEAD_PAD), BF16),
                   jax.ShapeDtypeStruct((b, N_HEADS, s, HEAD_PAD), BF16),
                   jax.ShapeDtypeStruct((b, s, d), BF16),
                   jax.ShapeDtypeStruct((b, s, d), BF16),
                   jax.ShapeDtypeStruct((b, s, LANES), F32)],
        scratch_shapes=[pltpu.VMEM((1, LANES), F32)],
        compiler_params=_params("arbitrary", "arbitrary"),
        name="inproj",
    )(x, mod, *consts)


def _skip_count(cs_ref, ce_ref, base, i, thresh):
    c0 = cs_ref[base + i]
    return lax.fori_loop(0, i, lambda j, n: n + (c0 - ce_ref[base + j] < thresh).astype(jnp.int32), 0)


def _attn_kernel(cs_ref, ce_ref, prm_ref, q_ref, k_ref, v_ref, o_ref, acc_ref, m_ref):
    t = ATTN_TILE
    nb = q_ref.shape[2] // t
    base = (pl.program_id(0) * pl.num_programs(1) + pl.program_id(1)) * nb
    bound = prm_ref[0]
    thresh = -(EXP_UNDERFLOW + 2.0 * bound)
    rows = lax.broadcasted_iota(jnp.int32, (t, t), 0)
    cols = lax.broadcasted_iota(jnp.int32, (t, t), 1)
    causal = rows >= cols

    def scores(q, j):
        k0 = pl.multiple_of(j * t, t)
        return _dot_nt(q, k_ref[0, 0, pl.ds(k0, t), :]), v_ref[0, 0, pl.ds(k0, t), :]

    def finish(q0):
        acc = acc_ref[...]
        o_ref[0, pl.ds(q0, t), :] = (acc / acc[:, HEAD_DIM:HEAD_DIM + 1]).astype(BF16)

    def fixed_shift(i, carry):
        q0 = pl.multiple_of(i * t, t)
        q = q_ref[0, 0, pl.ds(q0, t), :]
        acc_ref[...] = jnp.zeros_like(acc_ref)

        def kv(j, c):
            s, vb = scores(q, j)
            acc_ref[...] += _dot(jnp.exp(s).astype(BF16), vb)
            return c

        lax.fori_loop(_skip_count(cs_ref, ce_ref, base, i, thresh), i, kv, 0)
        s, vb = scores(q, i)
        acc_ref[...] += _dot(jnp.exp(jnp.where(causal, s, NEG_BIG)).astype(BF16), vb)
        finish(q0)
        return carry

    def running_max(i, carry):
        q0 = pl.multiple_of(i * t, t)
        q = q_ref[0, 0, pl.ds(q0, t), :]
        m_ref[...] = jnp.full_like(m_ref, -jnp.inf)
        acc_ref[...] = jnp.zeros_like(acc_ref)

        def step(j, masked):
            s, vb = scores(q, j)
            if masked:
                s = jnp.where(causal, s, NEG_BIG)
            m_prev = m_ref[...]
            m_new = jnp.maximum(m_prev, jnp.max(s, axis=-1, keepdims=True))
            p = jnp.exp(s - m_new)
            acc_ref[...] = jnp.exp(m_prev - m_new) * acc_ref[...] + _dot(p.astype(BF16), vb)
            m_ref[...] = m_new

        def kv(j, c):
            step(j, False)
            return c

        lax.fori_loop(_skip_count(cs_ref, ce_ref, base, i, thresh), i, kv, 0)
        step(i, True)
        finish(q0)
        return carry

    @pl.when(bound <= FIXED_SHIFT_BOUND)
    def _():
        lax.fori_loop(0, nb, fixed_shift, 0)

    @pl.when(bound > FIXED_SHIFT_BOUND)
    def _():
        lax.fori_loop(0, nb, running_max, 0)


def _attention(cs, ce, prm, qa, ka, v):
    b, nh, s, hp = qa.shape
    t = ATTN_TILE
    seq = pl.BlockSpec((1, 1, s, hp), lambda bi, hi, *_: (bi, hi, 0, 0))
    return pl.pallas_call(
        _attn_kernel,
        grid_spec=pltpu.PrefetchScalarGridSpec(
            num_scalar_prefetch=3,
            grid=(b, nh),
            in_specs=[seq, seq, seq],
            out_specs=pl.BlockSpec((1, s, hp), lambda bi, hi, *_: (bi, 0, hi)),
            scratch_shapes=[pltpu.VMEM((t, hp), F32), pltpu.VMEM((t, 1), F32)]),
        out_shape=jax.ShapeDtypeStruct((b, s, nh * hp), BF16),
        compiler_params=_params("arbitrary", "arbitrary"),
        name="attn",
    )(cs, ce, prm, qa, ka, v)


def _merge_kernel(u_ref, halo_ref, o_ref, sgc_ref, sga_ref, x_ref, mod_ref, wdw_ref, bdw_ref,
                  gng_ref, gnb_ref, gg_ref, wco_ref, wao_ref, wout_ref, x1_ref, buf_ref):
    tm = ROW_TILE
    halo = halo_ref[0].astype(F32)
    halo = jnp.where(pl.program_id(1) == 0, jnp.zeros_like(halo), halo)
    ucur = u_ref[0].astype(F32)
    for cb in range(CONV_DIM // LANES):
        buf_ref[cb, 0:CONV_HALO, :] = halo[:, cb * LANES:(cb + 1) * LANES]
        buf_ref[cb, CONV_HALO:, :] = ucur[:, cb * LANES:(cb + 1) * LANES]

    base = CONV_HALO - (CONV_WIDTH - 1)
    ys = []
    for cb in range(CONV_DIM // LANES):
        acc = jnp.zeros((tm, LANES), F32)
        for j in range(CONV_WIDTH):
            acc = acc + wdw_ref[j:j + 1, cb * LANES:(cb + 1) * LANES] * buf_ref[cb, base + j:base + j + tm, :]
        ys.append(acc)
    y = jnp.concatenate(ys, axis=1) + bdw_ref[...]

    gg = gg_ref[...]
    y_hi, y_lo = _split2(y)
    dlt = y - (_dot(y_hi, gg) + _dot(y_lo, gg))
    s_hi, s_lo = _split2(dlt * dlt)
    var = _dot(s_hi, gg) + _dot(s_lo, gg)
    yn = dlt * lax.rsqrt(var + EPS) * gng_ref[...] + gnb_ref[...]
    y_conv = _dot(_silu(yn).astype(BF16), wco_ref[...])

    per = HEAD_PAD // HEAD_DIM
    lane = lax.broadcasted_iota(jnp.int32, (1, HEAD_PAD), 1)
    packed = []
    for g in range(N_HEADS // per):
        tile = o_ref[0, :, g * per * HEAD_PAD:(g * per + 1) * HEAD_PAD].astype(F32)
        for part in range(1, per):
            nxt = o_ref[0, :, (g * per + part) * HEAD_PAD:(g * per + part + 1) * HEAD_PAD].astype(F32)
            tile = jnp.where(lane < part * HEAD_DIM, tile, pltpu.roll(nxt, part * HEAD_DIM, 1))
        packed.append(tile)
    y_attn = _dot(jnp.concatenate(packed, axis=1).astype(BF16), wao_ref[...])
    merged = sgc_ref[0].astype(F32) * y_conv + sga_ref[0].astype(F32) * y_attn
    mix = _dot(merged.astype(BF16), wout_ref[...])
    x1_ref[0] = x_ref[0] + mod_ref[0, 2:3, :] * mix


def _merge(u, o, sgc, sga, x, mod, wdw, bdw, gng, gnb, gg, wco, wao, wout):
    b, s, d = x.shape
    tm = ROW_TILE
    per = tm // CONV_HALO
    row = lambda w: pl.BlockSpec((1, tm, w), lambda bi, i: (bi, i, 0))
    consts = [wdw, bdw, gng, gnb, gg, wco, wao, wout]
    return pl.pallas_call(
        _merge_kernel,
        grid=(b, s // tm),
        in_specs=[row(CONV_DIM),
                  pl.BlockSpec((1, CONV_HALO, CONV_DIM),
                               lambda bi, i: (bi, jnp.maximum(i * per - 1, 0), 0)),
                  row(o.shape[2]), row(d), row(d), row(d),
                  pl.BlockSpec((1, 6, d), lambda bi, i: (bi, 0, 0))]
                 + [_const_spec(a.shape) for a in consts],
        out_specs=row(d),
        out_shape=jax.ShapeDtypeStruct((b, s, d), F32),
        scratch_shapes=[pltpu.VMEM((CONV_DIM // LANES, CONV_HALO + tm, LANES), F32)],
        compiler_params=_params("arbitrary", "arbitrary"),
        name="merge",
    )(u, u, o, sgc, sga, x, mod, *consts)


def _router_kernel(x1_ref, mod_ref, g2_ref, wr_hi_ref, wr_lo_ref, rb_ref, h2_ref, h2w_ref, comb_ref):
    x = x1_ref[0]
    ms = jnp.mean(x * x, axis=-1, keepdims=True)
    h = (x * lax.rsqrt(ms + EPS) * g2_ref[...]) * (1.0 + mod_ref[0, 4:5, :]) + mod_ref[0, 3:4, :]
    h2_ref[0] = h.astype(BF16)
    h2w_ref[0] = _pack_halves(h)

    h_hi, h_lo = _split2(h)
    logits = _dot_nt(wr_hi_ref[...], h_hi) + _dot_nt(wr_hi_ref[...], h_lo) + _dot_nt(wr_lo_ref[...], h_hi)
    scores = _sigmoid(logits)
    biased = scores + rb_ref[...]

    per = N_EXPERTS // N_GROUPS
    rows = lax.broadcasted_iota(jnp.int32, (per, biased.shape[1]), 0)
    gscore = []
    for g in range(N_GROUPS):
        blk = biased[g * per:(g + 1) * per, :]
        top1 = jnp.max(blk, axis=0, keepdims=True)
        first = jnp.min(jnp.where(blk == top1, rows, per), axis=0, keepdims=True)
        top2 = jnp.max(jnp.where(rows == first, -jnp.inf, blk), axis=0, keepdims=True)
        gscore.append(top1 + top2)

    cand = []
    for g in range(N_GROUPS):
        rank = jnp.zeros_like(gscore[g], dtype=jnp.int32)
        for g2 in range(N_GROUPS):
            if g2 == g:
                continue
            ahead = gscore[g2] > gscore[g]
            if g2 < g:
                ahead = ahead | (gscore[g2] == gscore[g])
            rank = rank + ahead.astype(jnp.int32)
        keep = rank < TOPK_GROUPS
        cand.append(jnp.where(keep, biased[g * per:(g + 1) * per, :], -jnp.inf))
    cand = jnp.concatenate(cand, axis=0)

    eidx = lax.broadcasted_iota(jnp.int32, cand.shape, 0)
    rank = jnp.zeros(cand.shape, jnp.int32)
    for e2 in range(N_EXPERTS):
        other = cand[e2:e2 + 1, :]
        ahead = (other > cand) | ((other == cand) & (eidx > e2))
        rank = rank + ahead.astype(jnp.int32)
    sel = (rank < TOP_K) & (cand > -jnp.inf)
    w = jnp.where(sel, scores, 0.0)
    comb_ref[...] = w / jnp.sum(w, axis=0, keepdims=True) * ROUTED_SCALE


def _router(x1, mod, g2, wr_hi, wr_lo, rb):
    b, s, d = x1.shape
    tm = ROW_TILE
    nt = s // tm
    return pl.pallas_call(
        _router_kernel,
        grid=(b, nt),
        in_specs=[pl.BlockSpec((1, tm, d), lambda bi, i: (bi, i, 0)),
                  pl.BlockSpec((1, 6, d), lambda bi, i: (bi, 0, 0)),
                  _const_spec(g2.shape), _const_spec(wr_hi.shape), _const_spec(wr_lo.shape),
                  _const_spec(rb.shape)],
        out_specs=[pl.BlockSpec((1, tm, d), lambda bi, i: (bi, i, 0)),
                   pl.BlockSpec((1, tm, d // 2), lambda bi, i: (bi, i, 0)),
                   pl.BlockSpec((N_EXPERTS, tm), lambda bi, i: (0, bi * nt + i))],
        out_shape=[jax.ShapeDtypeStruct((b, s, d), BF16),
                   jax.ShapeDtypeStruct((b, s, d // 2), jnp.int32),
                   jax.ShapeDtypeStruct((N_EXPERTS, b * s), F32)],
        compiler_params=_params("arbitrary", "arbitrary"),
        name="router",
    )(x1, mod, g2, wr_hi, wr_lo, rb)


def _pos_kernel(comb_ref, tri_ref, lstrict_ref, posk_ref, wk_ref, gend_ref, cnt_ref, base_ref):
    phase = pl.program_id(0)
    i = pl.program_id(1)
    tp = comb_ref.shape[1]
    comb = comb_ref[...]
    sel = comb != 0.0
    selb = jnp.where(sel, 1.0, 0.0).astype(BF16)
    tile_count = _dot(selb, jnp.ones((tp, tp), BF16))

    @pl.when(phase == 0)
    def _():
        @pl.when(i == 0)
        def _():
            cnt_ref[...] = jnp.zeros_like(cnt_ref)
            gend_ref[...] = jnp.zeros_like(gend_ref)

        cnt_ref[...] += tile_count
        posk_ref[...] = jnp.zeros_like(posk_ref)
        wk_ref[...] = jnp.zeros_like(wk_ref)

    @pl.when(phase == 1)
    def _():
        @pl.when(i == 0)
        def _():
            seg = jnp.floor((cnt_ref[...] + (EXPERT_TILE - 1.0)) * (1.0 / EXPERT_TILE)) * EXPERT_TILE
            s_hi, s_mid, s_lo = _split3(seg)
            ls = lstrict_ref[...]
            start = _dot(ls, s_hi) + _dot(ls, s_mid) + _dot(ls, s_lo)
            base_ref[...] = start
            gend_ref[...] = (start + seg)[:, :LANES]

        rank = _dot(selb, tri_ref[...])
        pos = base_ref[...] + rank - 1.0
        base_ref[...] += tile_count
        slot = _dot(lstrict_ref[...], selb)
        rows_p, rows_w = [], []
        for k in range(TOP_K):
            m = sel & (slot == k)
            rows_p.append(jnp.sum(jnp.where(m, pos, 0.0), axis=0, keepdims=True))
            rows_w.append(jnp.sum(jnp.where(m, comb, 0.0), axis=0, keepdims=True))
        posk_ref[...] = jnp.concatenate(rows_p, axis=0).astype(jnp.int32)
        wk_ref[...] = jnp.concatenate(rows_w, axis=0)


def _positions(comb_t, tri, lstrict):
    ne, t = comb_t.shape
    tp = POS_TILE
    out_tok = pl.BlockSpec((TOP_K, tp), lambda p, i: (0, i * p))
    return pl.pallas_call(
        _pos_kernel,
        grid=(2, t // tp),
        in_specs=[pl.BlockSpec((ne, tp), lambda p, i: (0, i)),
                  _const_spec(tri.shape), _const_spec(lstrict.shape)],
        out_specs=[out_tok, out_tok, pl.BlockSpec((ne, LANES), lambda p, i: (0, 0))],
        out_shape=[jax.ShapeDtypeStruct((TOP_K, t), jnp.int32),
                   jax.ShapeDtypeStruct((TOP_K, t), F32),
                   jax.ShapeDtypeStruct((ne, LANES), F32)],
        scratch_shapes=[pltpu.VMEM((ne, tp), F32), pltpu.VMEM((ne, tp), F32)],
        compiler_params=_params("arbitrary", "arbitrary"),
        name="positions",
    )(comb_t, tri, lstrict)


def _sc_workers():
    info = plsc.get_sparse_core_info()
    return info.num_cores, info.num_cores * info.num_subcores


def _sc_scatter_rows(rows, pos, n_out):
    nc, nw = _sc_workers()
    n, w = rows.shape
    nk = pos.shape[0]
    ch = SC_CHUNK
    per_w = n // nw
    assert per_w * nw == n and per_w % ch == 0

    @functools.partial(
        pl.kernel, mesh=plsc.VectorSubcoreMesh(core_axis_name="c", subcore_axis_name="s"),
        out_type=jax.ShapeDtypeStruct((n_out, w), rows.dtype),
        scratch_types=[pltpu.VMEM((nk, ch), jnp.int32), pltpu.VMEM((ch, w), rows.dtype),
                       pltpu.SemaphoreType.DMA])
    def scatter(rows_hbm, pos_hbm, out_hbm, idx_v, rows_v, sem):
        base = (lax.axis_index("s") * nc + lax.axis_index("c")) * per_w

        @pl.loop(0, per_w // ch)
        def _(ci):
            off = pl.multiple_of(base + ci * ch, ch)
            pltpu.sync_copy(pos_hbm.at[:, pl.ds(off, ch)], idx_v)
            pltpu.sync_copy(rows_hbm.at[pl.ds(off, ch)], rows_v)
            copies = [pltpu.make_async_copy(rows_v, out_hbm.at[idx_v.at[k]], sem) for k in range(nk)]
            for cp in copies:
                cp.start()
            for cp in copies:
                cp.wait()

    return scatter(rows, pos)


def _sc_gather_rows(table, idx):
    nc, nw = _sc_workers()
    n = idx.shape[0]
    w = table.shape[1]
    ch = SC_CHUNK
    per_w = n // nw
    assert per_w * nw == n and per_w % ch == 0

    @functools.partial(
        pl.kernel, mesh=plsc.VectorSubcoreMesh(core_axis_name="c", subcore_axis_name="s"),
        out_type=jax.ShapeDtypeStruct((n, w), table.dtype),
        scratch_types=[pltpu.VMEM((ch,), jnp.int32), pltpu.VMEM((ch, w), table.dtype),
                       pltpu.SemaphoreType.DMA])
    def gather(table_hbm, idx_hbm, out_hbm, idx_v, rows_v, sem):
        base = (lax.axis_index("s") * nc + lax.axis_index("c")) * per_w

        @pl.loop(0, per_w // ch)
        def _(ci):
            off = pl.multiple_of(base + ci * ch, ch)
            pltpu.sync_copy(idx_hbm.at[pl.ds(off, ch)], idx_v)
            cp = pltpu.make_async_copy(table_hbm.at[idx_v], rows_v, sem)
            cp.start()
            cp.wait()
            pltpu.sync_copy(rows_v, out_hbm.at[pl.ds(off, ch)])

    return gather(table, idx)


def _expert_kernel(te_ref, nu_ref, xs_ref, wgu_ref, wd_ref, y_ref, wgu_b, wd_b):
    i = pl.program_id(0)

    @pl.when(i < nu_ref[0])
    def _():
        @pl.when((i == 0) | (te_ref[i] != te_ref[jnp.maximum(i - 1, 0)]))
        def _():
            wgu_b[...] = wgu_ref[0].astype(BF16)
            wd_b[...] = wd_ref[0].astype(BF16)

        lo, hi = _unpack_halves(xs_ref[...])
        half = wgu_b.shape[0] // 2
        gu = _dot(lo.astype(BF16), wgu_b[:half, :]) + _dot(hi.astype(BF16), wgu_b[half:, :])
        act = _silu(gu[:, :EXPERT_DIM]) * gu[:, EXPERT_DIM:]
        y_ref[...] = _pack_halves(_dot(act.astype(BF16), wd_b[...]))


def _experts(tile_e, n_used, xs, wgu, wd):
    r, w = xs.shape
    tm = EXPERT_TILE
    d = wgu.shape[1]
    rows = pl.BlockSpec((tm, w), lambda i, te, nu: (jnp.minimum(i, nu[0] - 1), 0))
    return pl.pallas_call(
        _expert_kernel,
        grid_spec=pltpu.PrefetchScalarGridSpec(
            num_scalar_prefetch=2,
            grid=(r // tm,),
            in_specs=[rows,
                      pl.BlockSpec((1, d, 2 * EXPERT_DIM), lambda i, te, nu: (te[i], 0, 0)),
                      pl.BlockSpec((1, EXPERT_DIM, d), lambda i, te, nu: (te[i], 0, 0))],
            out_specs=rows,
            scratch_shapes=[pltpu.VMEM((d, 2 * EXPERT_DIM), BF16), pltpu.VMEM((EXPERT_DIM, d), BF16)]),
        out_shape=jax.ShapeDtypeStruct((r, w), jnp.int32),
        compiler_params=_params("arbitrary"),
        name="experts",
    )(tile_e, n_used, xs, wgu, wd)


def _final_kernel(yk_ref, wk_ref, h_ref, x1_ref, mod_ref, wsgu_ref, wsd_ref, out_ref):
    half = yk_ref.shape[2]
    acc_lo = jnp.zeros((yk_ref.shape[1], half), F32)
    acc_hi = jnp.zeros((yk_ref.shape[1], half), F32)
    for k in range(TOP_K):
        lo, hi = _unpack_halves(yk_ref[k])
        w = wk_ref[:, k:k + 1]
        acc_lo = acc_lo + jnp.where(w != 0.0, w * lo, 0.0)
        acc_hi = acc_hi + jnp.where(w != 0.0, w * hi, 0.0)
    routed = jnp.concatenate([acc_lo, acc_hi], axis=1)
    sgu = _dot(h_ref[...], wsgu_ref[...])
    act = _silu(sgu[:, :SHARED_DIM]) * sgu[:, SHARED_DIM:]
    shared = _dot(act.astype(BF16), wsd_ref[...])
    out_ref[...] = x1_ref[...] + mod_ref[0, 5:6, :] * (routed + shared)


def _final(yk, wk_t, h2, x1, mod, wsgu, wsd, tiles_per_batch):
    t, d = h2.shape
    tm = ROW_TILE
    row = lambda w: pl.BlockSpec((tm, w), lambda i: (i, 0))
    return pl.pallas_call(
        _final_kernel,
        grid=(t // tm,),
        in_specs=[pl.BlockSpec((TOP_K, tm, d // 2), lambda i: (0, i, 0)),
                  row(TOP_K), row(d), row(d),
                  pl.BlockSpec((1, 6, d), lambda i: (i // tiles_per_batch, 0, 0)),
                  _const_spec(wsgu.shape), _const_spec(wsd.shape)],
        out_specs=row(d),
        out_shape=jax.ShapeDtypeStruct((t, d), F32),
        compiler_params=_params("arbitrary"),
        name="final",
    )(yk, wk_t, h2, x1, mod, wsgu, wsd)


def _pad_heads(w):
    d = w.shape[0]
    w = w.reshape(d, N_HEADS, HEAD_DIM)
    return jnp.pad(w, ((0, 0), (0, 0), (0, HEAD_PAD - HEAD_DIM))).reshape(d, N_HEADS * HEAD_PAD)


def _placement():
    pq = np.zeros((LANES, N_HEADS * HEAD_PAD), np.float32)
    pk = np.zeros((LANES, N_HEADS * HEAD_PAD), np.float32)
    cq = np.zeros((1, N_HEADS * HEAD_PAD), np.float32)
    ck = np.zeros((1, N_HEADS * HEAD_PAD), np.float32)
    for hd in range(N_HEADS):
        for k in range(3):
            pq[k * N_HEADS + hd, hd * HEAD_PAD + AUG0 + k] = 1.0
            ck[0, hd * HEAD_PAD + AUG0 + k] = 1.0
            pk[k * N_HEADS + hd, hd * HEAD_PAD + AUG0 + 3 + k] = -1.0
            cq[0, hd * HEAD_PAD + AUG0 + 3 + k] = 1.0
    return (jnp.asarray(pq, BF16), jnp.asarray(pk, BF16), jnp.asarray(cq), jnp.asarray(ck))


def kernel(x, c, w_ada, b_ada, norm1_g, w_in, w_dw, b_dw, conv_gn_g, conv_gn_b, w_conv_out,
           q_norm_g, k_norm_g, b_forget, w_attn_out, w_out, norm2_g, w_router, router_bias,
           w_experts_gate_up, w_experts_down, w_shared_gate_up, w_shared_down):
    depth = w_ada.shape[0]
    b, s, d = x.shape
    off_q = 2 * CONV_DIM
    off_f = off_q + 3 * ATTN_DIM
    off_gc = off_f + N_HEADS
    off_ga = off_gc + d

    pq, pk, cq, ck = _placement()
    tri = jnp.asarray(np.tril(np.ones((ROW_TILE, ROW_TILE), np.float32)), BF16)
    grp = np.arange(CONV_DIM) // (CONV_DIM // CONV_GROUPS)
    gg = jnp.asarray((grp[:, None] == grp[None, :]).astype(np.float32) / (CONV_DIM // CONV_GROUPS), BF16)
    c_pad = jnp.pad(c, ((0, SUBLANES - b), (0, 0)))
    tri_pos = jnp.asarray(np.triu(np.ones((POS_TILE, POS_TILE), np.float32)), BF16)
    lstrict = jnp.asarray(np.tril(np.ones((N_EXPERTS, N_EXPERTS), np.float32), -1), BF16)

    for l in range(depth):
        mod = _ada(c_pad, w_ada[l], b_ada[l][None, :])[:b].reshape(b, 6, d)

        wi = w_in[l]
        bf = jnp.pad(b_forget[l][None, :], ((0, 0), (0, LANES - N_HEADS)))
        wf = jnp.pad(wi[:, off_f:off_gc], ((0, 0), (0, LANES - N_HEADS))).astype(BF16)
        gpad = lambda g, sc: jnp.tile(jnp.pad(g * sc, (0, HEAD_PAD - HEAD_DIM)), N_HEADS)[None, :]
        qscale = HEAD_DIM ** -0.5
        u, qa, ka, v, sgc, sga, cum = _inproj(
            x, mod, norm1_g[l][None, :],
            wi[:, :off_q].astype(BF16),
            _pad_heads(wi[:, off_q:off_q + ATTN_DIM]).astype(BF16),
            _pad_heads(wi[:, off_q + ATTN_DIM:off_q + 2 * ATTN_DIM]).astype(BF16),
            wi[:, off_q + 2 * ATTN_DIM:off_f].astype(BF16),
            wf,wi[:, off_gc:off_ga].astype(BF16), wi[:, off_ga:].astype(BF16),
            bf, gpad(q_norm_g[l], qscale), gpad(k_norm_g[l], 1.0),
            pq, pk, cq, ck, tri)

        flat = lambda a: a[:, :, :N_HEADS].transpose(0, 2, 1).reshape(-1)
        cs = flat(cum[:, 0::ATTN_TILE])
        ce = flat(cum[:, ATTN_TILE - 1::ATTN_TILE])
        bound = (1.02 * HEAD_DIM * qscale) * jnp.max(jnp.abs(q_norm_g[l])) * jnp.max(jnp.abs(k_norm_g[l]))
        o = _attention(cs, ce, bound.reshape(1), qa, ka, v)

        wdw = jnp.pad(w_dw[l], ((0, CONV_HALO - CONV_WIDTH), (0, 0)))
        x1 = _merge(u, o, sgc, sga, x, mod, wdw, b_dw[l][None, :], conv_gn_g[l][None, :],
                    conv_gn_b[l][None, :], gg, w_conv_out[l].astype(BF16), w_attn_out[l].astype(BF16),
                    w_out[l].astype(BF16))

        wr = w_router[l].T
        wr_hi = wr.astype(BF16)
        wr_lo = (wr - wr_hi.astype(F32)).astype(BF16)
        h2, h2w, comb_t = _router(x1, mod, norm2_g[l][None, :], wr_hi, wr_lo, router_bias[l][:, None])

        t = b * s
        posk, wk, gend = _positions(comb_t, tri_pos, lstrict)
        n_tiles = (t * TOP_K) // EXPERT_TILE + N_EXPERTS
        seg_end = gend[:, 0].astype(jnp.int32)
        n_used = seg_end[-1:] // EXPERT_TILE
        tile_start = jnp.arange(n_tiles, dtype=jnp.int32) * EXPERT_TILE
        tile_start = jnp.minimum(tile_start, seg_end[-1] - EXPERT_TILE)
        tile_e = jnp.sum((seg_end[None, :] <= tile_start[:, None]).astype(jnp.int32), axis=1)

        xs = _sc_scatter_rows(h2w.reshape(t, d // 2), posk, n_tiles * EXPERT_TILE)
        ys = _experts(tile_e, n_used, xs, w_experts_gate_up[l], w_experts_down[l])
        yk = _sc_gather_rows(ys, posk.reshape(-1)).reshape(TOP_K, t, d // 2)
        out = _final(yk, wk.T, h2.reshape(t, d), x1.reshape(t, d), mod,
                     w_shared_gate_up[l].astype(BF16), w_shared_down[l].astype(BF16), s // ROW_TILE)
        x = out.reshape(b, s, d)
    return x
```

```python
import functools

import numpy as np
import jax
import jax.numpy as jnp
from jax import lax
from jax.experimental import pallas as pl
from jax.experimental.pallas import tpu as pltpu
from jax.experimental.pallas import tpu_sc as plsc

F32 = jnp.float32
BF16 = jnp.bfloat16

CONV_DIM = 512
CONV_WIDTH = 31
CONV_GROUPS = 8
N_HEADS = 8
HEAD_DIM = 64
ATTN_DIM = N_HEADS * HEAD_DIM
N_EXPERTS = 64
TOP_K = 8
N_GROUPS = 8
TOPK_GROUPS = 4
EXPERT_DIM = 256
SHARED_DIM = 256
ROUTED_SCALE = 2.5
EPS = 1e-6

LANES = 128
SUBLANES = 8
HEAD_PAD = LANES
AUG0 = HEAD_DIM
VMEM_LIMIT = 56 * 1024 * 1024

ROW_TILE = 512
ATTN_TILE = 512
EXPERT_TILE = 512
POS_TILE = 512
SC_CHUNK = 128
CONV_HALO = 32

NEG_BIG = -1e30
EXP_UNDERFLOW = 104.0
FIXED_SHIFT_BOUND = 40.0


def _dot(a, b):
    return jnp.dot(a, b, preferred_element_type=F32)


def _dot_nt(a, b):
    return lax.dot_general(a, b, (((1,), (1,)), ((), ())), preferred_element_type=F32)


def _split2(x):
    hi = x.astype(BF16)
    lo = (x - hi.astype(F32)).astype(BF16)
    return hi, lo


def _split3(x):
    hi = x.astype(BF16)
    r = x - hi.astype(F32)
    mid = r.astype(BF16)
    lo = (r - mid.astype(F32)).astype(BF16)
    return hi, mid, lo


def _pack_halves(v):
    n = v.shape[1] // 2
    lo = lax.bitcast_convert_type(v[:, :n].astype(BF16).astype(F32), jnp.uint32)
    hi = lax.bitcast_convert_type(v[:, n:].astype(BF16).astype(F32), jnp.uint32)
    return lax.bitcast_convert_type(hi | lax.shift_right_logical(lo, jnp.uint32(16)), jnp.int32)


def _unpack_halves(w):
    u = lax.bitcast_convert_type(w, jnp.uint32)
    lo = lax.bitcast_convert_type(lax.shift_left(u, jnp.uint32(16)), F32)
    hi = lax.bitcast_convert_type(u & jnp.uint32(0xFFFF0000), F32)
    return lo, hi


def _sigmoid(x):
    return 1.0 / (1.0 + jnp.exp(-x))


def _silu(x):
    return x * _sigmoid(x)


def _params(*sem):
    return pltpu.CompilerParams(dimension_semantics=sem, vmem_limit_bytes=VMEM_LIMIT)


def _const_spec(shape):
    n = len(shape)
    return pl.BlockSpec(shape, lambda *_: (0,) * n, pipeline_mode=pl.Buffered(1))


def _ada_kernel(c_ref, w_ref, b_ref, o_ref):
    c = c_ref[...]
    a_hi, a_lo = _split2(_silu(c))
    w_hi, w_lo = _split2(w_ref[...])
    o_ref[...] = _dot(a_hi, w_hi) + _dot(a_hi, w_lo) + _dot(a_lo, w_hi) + b_ref[...]


def _ada(c_pad, w_ada, b_ada):
    d = c_pad.shape[1]
    n = w_ada.shape[1]
    return pl.pallas_call(
        _ada_kernel,
        grid=(n // d,),
        in_specs=[_const_spec(c_pad.shape),
                  pl.BlockSpec((d, d), lambda j: (0, j)),
                  pl.BlockSpec((1, d), lambda j: (0, j))],
        out_specs=pl.BlockSpec((c_pad.shape[0], d), lambda j: (0, j)),
        out_shape=jax.ShapeDtypeStruct((c_pad.shape[0], n), F32),
        compiler_params=_params("arbitrary"),
        name="ada",
    )(c_pad, w_ada, b_ada)


def _lane_pieces(x):
    hi, mid, lo = _split3(x)
    return (hi.astype(F32) + pltpu.roll(mid.astype(F32), N_HEADS, 1)
            + pltpu.roll(lo.astype(F32), 2 * N_HEADS, 1)).astype(BF16)


def _head_tile(p, col0, hd):
    per = HEAD_PAD // HEAD_DIM
    g, part = divmod(hd, per)
    blk = p[:, col0 + g * HEAD_PAD:col0 + (g + 1) * HEAD_PAD]
    if part:
        blk = pltpu.roll(blk, HEAD_PAD - part * HEAD_DIM, 1)
    lane = lax.broadcasted_iota(jnp.int32, (1, HEAD_PAD), 1)
    return jnp.where(lane < HEAD_DIM, blk, 0.0)


def _inproj_kernel(x_ref, mod_ref, g1_ref, wcv_ref, wq_ref, wk_ref, wv_ref, wf_ref, wgc_ref,
                   wga_ref, bf_ref, qg_ref, kg_ref, pq_ref, pk_ref, cq_ref, ck_ref, tri_ref,
                   u_ref, qa_ref, ka_ref, v_ref, sgc_ref, sga_ref, cum_ref, carry_ref):
    @pl.when(pl.program_id(1) == 0)
    def _():
        carry_ref[...] = jnp.zeros_like(carry_ref)

    x = x_ref[0]
    ms = jnp.mean(x * x, axis=-1, keepdims=True)
    h = (x * lax.rsqrt(ms + EPS) * g1_ref[...]) * (1.0 + mod_ref[0, 1:2, :]) + mod_ref[0, 0:1, :]
    hb = h.astype(BF16)

    pc = _dot(hb, wcv_ref[...])
    u_ref[0] = (pc[:, :CONV_DIM] * _sigmoid(pc[:, CONV_DIM:])).astype(BF16)

    sgc_ref[0] = _sigmoid(_dot(hb, wgc_ref[...])).astype(BF16)
    sga_ref[0] = _sigmoid(_dot(hb, wga_ref[...])).astype(BF16)

    z = _dot(hb, wf_ref[...]) + bf_ref[...]
    lf = jnp.minimum(z, 0.0) - jnp.log1p(jnp.exp(-jnp.abs(z)))
    lane = lax.broadcasted_iota(jnp.int32, (1, LANES), 1)
    psum = _dot(tri_ref[...], _lane_pieces(jnp.where(lane < N_HEADS, lf, 0.0)))
    cum = (psum + pltpu.roll(psum, LANES - N_HEADS, 1) + pltpu.roll(psum, LANES - 2 * N_HEADS, 1)
           + carry_ref[...])
    cum = jnp.where(lane < N_HEADS, cum, 0.0)
    carry_ref[...] = cum[ROW_TILE - 1:ROW_TILE, :]
    cum_ref[0] = cum

    pieces = _lane_pieces(cum)
    addq = _dot(pieces, pq_ref[...]) + cq_ref[...]
    addk = _dot(pieces, pk_ref[...]) + ck_ref[...]

    pq = _dot(hb, wq_ref[...])
    pk = _dot(hb, wk_ref[...])
    pv = _dot(hb, wv_ref[...])
    inv_hd = 1.0 / HEAD_DIM
    vone = (lax.broadcasted_iota(jnp.int32, (1, HEAD_PAD), 1) == HEAD_DIM).astype(F32)
    for hd in range(N_HEADS):
        sl = slice(hd * HEAD_PAD, (hd + 1) * HEAD_PAD)
        qb = pq[:, sl]
        qn = qb * lax.rsqrt(jnp.sum(qb * qb, axis=-1, keepdims=True) * inv_hd + EPS) * qg_ref[:, sl]
        qa_ref[0, hd] = (qn + addq[:, sl]).astype(BF16)
        kb = pk[:, sl]
        kn = kb * lax.rsqrt(jnp.sum(kb * kb, axis=-1, keepdims=True) * inv_hd + EPS) * kg_ref[:, sl]
        ka_ref[0, hd] = (kn + addk[:, sl]).astype(BF16)
        v_ref[0, hd] = (_head_tile(pv, 0, hd) + vone).astype(BF16)


def _inproj(x, mod, g1, wcv, wq, wk, wv, wf, wgc, wga, bf, qg, kg, pq, pk, cq, ck, tri):
    b, s, d = x.shape
    tm = ROW_TILE
    hp = N_HEADS * HEAD_PAD
    row = lambda w: pl.BlockSpec((1, tm, w), lambda bi, i: (bi, i, 0))
    head = pl.BlockSpec((1, N_HEADS, tm, HEAD_PAD), lambda bi, i: (bi, 0, i, 0))
    consts = [g1, wcv, wq, wk, wv, wf, wgc, wga, bf, qg, kg, pq, pk, cq, ck, tri]
    return pl.pallas_call(
        _inproj_kernel,
        grid=(b, s // tm),
        in_specs=[row(d), pl.BlockSpec((1, 6, d), lambda bi, i: (bi, 0, 0))]
                 + [_const_spec(a.shape) for a in consts],
        out_specs=[row(CONV_DIM), head, head, head, row(d), row(d), row(LANES)],
        out_shape=[jax.ShapeDtypeStruct((b, s, CONV_DIM), BF16),
                   jax.ShapeDtypeStruct((b, N_HEADS, s, HEAD_PAD), BF16),
                   jax.ShapeDtypeStruct((b, N_HEADS, s, HEAD_PAD), BF16),
                   jax.ShapeDtypeStruct((b, N_HEADS, s, HEAD_PAD), BF16),
                   jax.ShapeDtypeStruct((b, s, d), BF16),
                   jax.ShapeDtypeStruct((b, s, d), BF16),
                   jax.ShapeDtypeStruct((b, s, LANES), F32)],
        scratch_shapes=[pltpu.VMEM((1, LANES), F32)],
        compiler_params=_params("arbitrary", "arbitrary"),
        name="inproj",
    )(x, mod, *consts)


def _skip_count(cs_ref, ce_ref, base, i, thresh):
    c0 = cs_ref[base + i]
    return lax.fori_loop(0, i, lambda j, n: n + (c0 - ce_ref[base + j] < thresh).astype(jnp.int32), 0)


def _attn_kernel(cs_ref, ce_ref, prm_ref, q_ref, k_ref, v_ref, o_ref, acc_ref, m_ref):
    t = ATTN_TILE
    nb = q_ref.shape[2] // t
    base = (pl.program_id(0) * pl.num_programs(1) + pl.program_id(1)) * nb
    bound = prm_ref[0]
    thresh = -(EXP_UNDERFLOW + 2.0 * bound)
    rows = lax.broadcasted_iota(jnp.int32, (t, t), 0)
    cols = lax.broadcasted_iota(jnp.int32, (t, t), 1)
    causal = rows >= cols

    def scores(q, j):
        k0 = pl.multiple_of(j * t, t)
        return _dot_nt(q, k_ref[0, 0, pl.ds(k0, t), :]), v_ref[0, 0, pl.ds(k0, t), :]

    def finish(q0):
        acc = acc_ref[...]
        o_ref[0, pl.ds(q0, t), :] = (acc / acc[:, HEAD_DIM:HEAD_DIM + 1]).astype(BF16)

    def fixed_shift(i, carry):
        q0 = pl.multiple_of(i * t, t)
        q = q_ref[0, 0, pl.ds(q0, t), :]
        acc_ref[...] = jnp.zeros_like(acc_ref)

        def kv(j, c):
            s, vb = scores(q, j)
            acc_ref[...] += _dot(jnp.exp(s).astype(BF16), vb)
            return c

        lax.fori_loop(_skip_count(cs_ref, ce_ref, base, i, thresh), i, kv, 0)
        s, vb = scores(q, i)
        acc_ref[...] += _dot(jnp.exp(jnp.where(causal, s, NEG_BIG)).astype(BF16), vb)
        finish(q0)
        return carry

    def running_max(i, carry):
        q0 = pl.multiple_of(i * t, t)
        q = q_ref[0, 0, pl.ds(q0, t), :]
        m_ref[...] = jnp.full_like(m_ref, -jnp.inf)
        acc_ref[...] = jnp.zeros_like(acc_ref)

        def step(j, masked):
            s, vb = scores(q, j)
            if masked:
                s = jnp.where(causal, s, NEG_BIG)
            m_prev = m_ref[...]
            m_new = jnp.maximum(m_prev, jnp.max(s, axis=-1, keepdims=True))
            p = jnp.exp(s - m_new)
            acc_ref[...] = jnp.exp(m_prev - m_new) * acc_ref[...] + _dot(p.astype(BF16), vb)
            m_ref[...] = m_new

        def kv(j, c):
            step(j, False)
            return c

        lax.fori_loop(_skip_count(cs_ref, ce_ref, base, i, thresh), i, kv, 0)
        step(i, True)
        finish(q0)
        return carry

    @pl.when(bound <= FIXED_SHIFT_BOUND)
    def _():
        lax.fori_loop(0, nb, fixed_shift, 0)

    @pl.when(bound > FIXED_SHIFT_BOUND)
    def _():
        lax.fori_loop(0, nb, running_max, 0)


def _attention(cs, ce, prm, qa, ka, v):
    b, nh, s, hp = qa.shape
    t = ATTN_TILE
    seq = pl.BlockSpec((1, 1, s, hp), lambda bi, hi, *_: (bi, hi, 0, 0))
    return pl.pallas_call(
        _attn_kernel,
        grid_spec=pltpu.PrefetchScalarGridSpec(
            num_scalar_prefetch=3,
            grid=(b, nh),
            in_specs=[seq, seq, seq],
            out_specs=pl.BlockSpec((1, s, hp), lambda bi, hi, *_: (bi, 0, hi)),
            scratch_shapes=[pltpu.VMEM((t, hp), F32), pltpu.VMEM((t, 1), F32)]),
        out_shape=jax.ShapeDtypeStruct((b, s, nh * hp), BF16),
        compiler_params=_params("arbitrary", "arbitrary"),
        name="attn",
    )(cs, ce, prm, qa, ka, v)


def _merge_kernel(u_ref, halo_ref, o_ref, sgc_ref, sga_ref, x_ref, mod_ref, wdw_ref, bdw_ref,
                  gng_ref, gnb_ref, gg_ref, wco_ref, wao_ref, wout_ref, x1_ref, buf_ref):
    tm = ROW_TILE
    halo = halo_ref[0].astype(F32)
    halo = jnp.where(pl.program_id(1) == 0, jnp.zeros_like(halo), halo)
    ucur = u_ref[0].astype(F32)
    for cb in range(CONV_DIM // LANES):
        buf_ref[cb, 0:CONV_HALO, :] = halo[:, cb * LANES:(cb + 1) * LANES]
        buf_ref[cb, CONV_HALO:, :] = ucur[:, cb * LANES:(cb + 1) * LANES]

    base = CONV_HALO - (CONV_WIDTH - 1)
    ys = []
    for cb in range(CONV_DIM // LANES):
        acc = jnp.zeros((tm, LANES), F32)
        for j in range(CONV_WIDTH):
            acc = acc + wdw_ref[j:j + 1, cb * LANES:(cb + 1) * LANES] * buf_ref[cb, base + j:base + j + tm, :]
        ys.append(acc)
    y = jnp.concatenate(ys, axis=1) + bdw_ref[...]

    gg = gg_ref[...]
    y_hi, y_lo = _split2(y)
    dlt = y - (_dot(y_hi, gg) + _dot(y_lo, gg))
    s_hi, s_lo = _split2(dlt * dlt)
    var = _dot(s_hi, gg) + _dot(s_lo, gg)
    yn = dlt * lax.rsqrt(var + EPS) * gng_ref[...] + gnb_ref[...]
    y_conv = _dot(_silu(yn).astype(BF16), wco_ref[...])

    per = HEAD_PAD // HEAD_DIM
    lane = lax.broadcasted_iota(jnp.int32, (1, HEAD_PAD), 1)
    packed = []
    for g in range(N_HEADS // per):
        tile = o_ref[0, :, g * per * HEAD_PAD:(g * per + 1) * HEAD_PAD].astype(F32)
        for part in range(1, per):
            nxt = o_ref[0, :, (g * per + part) * HEAD_PAD:(g * per + part + 1) * HEAD_PAD].astype(F32)
            tile = jnp.where(lane < part * HEAD_DIM, tile, pltpu.roll(nxt, part * HEAD_DIM, 1))
        packed.append(tile)
    y_attn = _dot(jnp.concatenate(packed, axis=1).astype(BF16), wao_ref[...])
    merged = sgc_ref[0].astype(F32) * y_conv + sga_ref[0].astype(F32) * y_attn
    mix = _dot(merged.astype(BF16), wout_ref[...])
    x1_ref[0] = x_ref[0] + mod_ref[0, 2:3, :] * mix


def _merge(u, o, sgc, sga, x, mod, wdw, bdw, gng, gnb, gg, wco, wao, wout):
    b, s, d = x.shape
    tm = ROW_TILE
    per = tm // CONV_HALO
    row = lambda w: pl.BlockSpec((1, tm, w), lambda bi, i: (bi, i, 0))
    consts = [wdw, bdw, gng, gnb, gg, wco, wao, wout]
    return pl.pallas_call(
        _merge_kernel,
        grid=(b, s // tm),
        in_specs=[row(CONV_DIM),
                  pl.BlockSpec((1, CONV_HALO, CONV_DIM),
                               lambda bi, i: (bi, jnp.maximum(i * per - 1, 0), 0)),
                  row(o.shape[2]), row(d), row(d), row(d),
                  pl.BlockSpec((1, 6, d), lambda bi, i: (bi, 0, 0))]
                 + [_const_spec(a.shape) for a in consts],
        out_specs=row(d),
        out_shape=jax.ShapeDtypeStruct((b, s, d), F32),
        scratch_shapes=[pltpu.VMEM((CONV_DIM // LANES, CONV_HALO + tm, LANES), F32)],
        compiler_params=_params("arbitrary", "arbitrary"),
        name="merge",
    )(u, u, o, sgc, sga, x, mod, *consts)


def _router_kernel(x1_ref, mod_ref, g2_ref, wr_hi_ref, wr_lo_ref, rb_ref, h2_ref, h2w_ref, comb_ref):
    x = x1_ref[0]
    ms = jnp.mean(x * x, axis=-1, keepdims=True)
    h = (x * lax.rsqrt(ms + EPS) * g2_ref[...]) * (1.0 + mod_ref[0, 4:5, :]) + mod_ref[0, 3:4, :]
    h2_ref[0] = h.astype(BF16)
    h2w_ref[0] = _pack_halves(h)

    h_hi, h_lo = _split2(h)
    logits = _dot_nt(wr_hi_ref[...], h_hi) + _dot_nt(wr_hi_ref[...], h_lo) + _dot_nt(wr_lo_ref[...], h_hi)
    scores = _sigmoid(logits)
    biased = scores + rb_ref[...]

    per = N_EXPERTS // N_GROUPS
    rows = lax.broadcasted_iota(jnp.int32, (per, biased.shape[1]), 0)
    gscore = []
    for g in range(N_GROUPS):
        blk = biased[g * per:(g + 1) * per, :]
        top1 = jnp.max(blk, axis=0, keepdims=True)
        first = jnp.min(jnp.where(blk == top1, rows, per), axis=0, keepdims=True)
        top2 = jnp.max(jnp.where(rows == first, -jnp.inf, blk), axis=0, keepdims=True)
        gscore.append(top1 + top2)

    cand = []
    for g in range(N_GROUPS):
        rank = jnp.zeros_like(gscore[g], dtype=jnp.int32)
        for g2 in range(N_GROUPS):
            if g2 == g:
                continue
            ahead = gscore[g2] > gscore[g]
            if g2 < g:
                ahead = ahead | (gscore[g2] == gscore[g])
            rank = rank + ahead.astype(jnp.int32)
        keep = rank < TOPK_GROUPS
        cand.append(jnp.where(keep, biased[g * per:(g + 1) * per, :], -jnp.inf))
    cand = jnp.concatenate(cand, axis=0)

    eidx = lax.broadcasted_iota(jnp.int32, cand.shape, 0)
    rank = jnp.zeros(cand.shape, jnp.int32)
    for e2 in range(N_EXPERTS):
        other = cand[e2:e2 + 1, :]
        ahead = (other > cand) | ((other == cand) & (eidx > e2))
        rank = rank + ahead.astype(jnp.int32)
    sel = (rank < TOP_K) & (cand > -jnp.inf)
    w = jnp.where(sel, scores, 0.0)
    comb_ref[...] = w / jnp.sum(w, axis=0, keepdims=True) * ROUTED_SCALE


def _router(x1, mod, g2, wr_hi, wr_lo, rb):
    b, s, d = x1.shape
    tm = ROW_TILE
    nt = s // tm
    return pl.pallas_call(
        _router_kernel,
        grid=(b, nt),
        in_specs=[pl.BlockSpec((1, tm, d), lambda bi, i: (bi, i, 0)),
                  pl.BlockSpec((1, 6, d), lambda bi, i: (bi, 0, 0)),
                  _const_spec(g2.shape), _const_spec(wr_hi.shape), _const_spec(wr_lo.shape),
                  _const_spec(rb.shape)],
        out_specs=[pl.BlockSpec((1, tm, d), lambda bi, i: (bi, i, 0)),
                   pl.BlockSpec((1, tm, d // 2), lambda bi, i: (bi, i, 0)),
                   pl.BlockSpec((N_EXPERTS, tm), lambda bi, i: (0, bi * nt + i))],
        out_shape=[jax.ShapeDtypeStruct((b, s, d), BF16),
                   jax.ShapeDtypeStruct((b, s, d // 2), jnp.int32),
                   jax.ShapeDtypeStruct((N_EXPERTS, b * s), F32)],
        compiler_params=_params("arbitrary", "arbitrary"),
        name="router",
    )(x1, mod, g2, wr_hi, wr_lo, rb)


def _pos_kernel(comb_ref, tri_ref, lstrict_ref, posk_ref, wk_ref, gend_ref, cnt_ref, base_ref):
    phase = pl.program_id(0)
    i = pl.program_id(1)
    tp = comb_ref.shape[1]
    comb = comb_ref[...]
    sel = comb != 0.0
    selb = jnp.where(sel, 1.0, 0.0).astype(BF16)
    tile_count = _dot(selb, jnp.ones((tp, tp), BF16))

    @pl.when(phase == 0)
    def _():
        @pl.when(i == 0)
        def _():
            cnt_ref[...] = jnp.zeros_like(cnt_ref)
            gend_ref[...] = jnp.zeros_like(gend_ref)

        cnt_ref[...] += tile_count
        posk_ref[...] = jnp.zeros_like(posk_ref)
        wk_ref[...] = jnp.zeros_like(wk_ref)

    @pl.when(phase == 1)
    def _():
        @pl.when(i == 0)
        def _():
            seg = jnp.floor((cnt_ref[...] + (EXPERT_TILE - 1.0)) * (1.0 / EXPERT_TILE)) * EXPERT_TILE
            s_hi, s_mid, s_lo = _split3(seg)
            ls = lstrict_ref[...]
            start = _dot(ls, s_hi) + _dot(ls, s_mid) + _dot(ls, s_lo)
            base_ref[...] = start
            gend_ref[...] = (start + seg)[:, :LANES]

        rank = _dot(selb, tri_ref[...])
        pos = base_ref[...] + rank - 1.0
        base_ref[...] += tile_count
        slot = _dot(lstrict_ref[...], selb)
        rows_p, rows_w = [], []
        for k in range(TOP_K):
            m = sel & (slot == k)
            rows_p.append(jnp.sum(jnp.where(m, pos, 0.0), axis=0, keepdims=True))
            rows_w.append(jnp.sum(jnp.where(m, comb, 0.0), axis=0, keepdims=True))
        posk_ref[...] = jnp.concatenate(rows_p, axis=0).astype(jnp.int32)
        wk_ref[...] = jnp.concatenate(rows_w, axis=0)


def _positions(comb_t, tri, lstrict):
    ne, t = comb_t.shape
    tp = POS_TILE
    out_tok = pl.BlockSpec((TOP_K, tp), lambda p, i: (0, i * p))
    return pl.pallas_call(
        _pos_kernel,
        grid=(2, t // tp),
        in_specs=[pl.BlockSpec((ne, tp), lambda p, i: (0, i)),
                  _const_spec(tri.shape), _const_spec(lstrict.shape)],
        out_specs=[out_tok, out_tok, pl.BlockSpec((ne, LANES), lambda p, i: (0, 0))],
        out_shape=[jax.ShapeDtypeStruct((TOP_K, t), jnp.int32),
                   jax.ShapeDtypeStruct((TOP_K, t), F32),
                   jax.ShapeDtypeStruct((ne, LANES), F32)],
        scratch_shapes=[pltpu.VMEM((ne, tp), F32), pltpu.VMEM((ne, tp), F32)],
        compiler_params=_params("arbitrary", "arbitrary"),
        name="positions",
    )(comb_t, tri, lstrict)


def _sc_workers():
    info = plsc.get_sparse_core_info()
    return info.num_cores, info.num_cores * info.num_subcores


def _sc_scatter_rows(rows, pos, n_out):
    nc, nw = _sc_workers()
    n, w = rows.shape
    nk = pos.shape[0]
    ch = SC_CHUNK
    per_w = n // nw
    assert per_w * nw == n and per_w % ch == 0

    @functools.partial(
        pl.kernel, mesh=plsc.VectorSubcoreMesh(core_axis_name="c", subcore_axis_name="s"),
        out_type=jax.ShapeDtypeStruct((n_out, w), rows.dtype),
        scratch_types=[pltpu.VMEM((nk, ch), jnp.int32), pltpu.VMEM((ch, w), rows.dtype),
                       pltpu.SemaphoreType.DMA])
    def scatter(rows_hbm, pos_hbm, out_hbm, idx_v, rows_v, sem):
        base = (lax.axis_index("s") * nc + lax.axis_index("c")) * per_w

        @pl.loop(0, per_w // ch)
        def _(ci):
            off = pl.multiple_of(base + ci * ch, ch)
            pltpu.sync_copy(pos_hbm.at[:, pl.ds(off, ch)], idx_v)
            pltpu.sync_copy(rows_hbm.at[pl.ds(off, ch)], rows_v)
            copies = [pltpu.make_async_copy(rows_v, out_hbm.at[idx_v.at[k]], sem) for k in range(nk)]
            for cp in copies:
                cp.start()
            for cp in copies:
                cp.wait()

    return scatter(rows, pos)


def _sc_gather_rows(table, idx):
    nc, nw = _sc_workers()
    n = idx.shape[0]
    w = table.shape[1]
    ch = SC_CHUNK
    per_w = n // nw
    assert per_w * nw == n and per_w % ch == 0

    @functools.partial(
        pl.kernel, mesh=plsc.VectorSubcoreMesh(core_axis_name="c", subcore_axis_name="s"),
        out_type=jax.ShapeDtypeStruct((n, w), table.dtype),
        scratch_types=[pltpu.VMEM((ch,), jnp.int32), pltpu.VMEM((ch, w), table.dtype),
                       pltpu.SemaphoreType.DMA])
    def gather(table_hbm, idx_hbm, out_hbm, idx_v, rows_v, sem):
        base = (lax.axis_index("s") * nc + lax.axis_index("c")) * per_w

        @pl.loop(0, per_w // ch)
        def _(ci):
            off = pl.multiple_of(base + ci * ch, ch)
            pltpu.sync_copy(idx_hbm.at[pl.ds(off, ch)], idx_v)
            cp = pltpu.make_async_copy(table_hbm.at[idx_v], rows_v, sem)
            cp.start()
            cp.wait()
            pltpu.sync_copy(rows_v, out_hbm.at[pl.ds(off, ch)])

    return gather(table, idx)


def _expert_kernel(t0_ref, t1_ref, xs_hbm, wgu_ref, wd_ref, ys_hbm, xbuf, ybuf, sem_in, sem_out,
                   wgu_b, wd_b):
    e = pl.program_id(0)
    tm = EXPERT_TILE
    n_used = t1_ref[pl.num_programs(0) - 1]

    def read(g, slot):
        return pltpu.make_async_copy(xs_hbm.at[pl.ds(pl.multiple_of(g * tm, tm), tm)],
                                     xbuf.at[slot], sem_in.at[slot])

    def write(g, slot):
        return pltpu.make_async_copy(ybuf.at[slot], ys_hbm.at[pl.ds(pl.multiple_of(g * tm, tm), tm)],
                                     sem_out.at[slot])

    @pl.when(e == 0)
    def _():
        read(0, 0).start()

    @pl.when(t1_ref[e] > t0_ref[e])
    def _():
        wgu_b[...] = wgu_ref[0].astype(BF16)
        wd_b[...] = wd_ref[0].astype(BF16)
        half = wgu_b.shape[0] // 2

        def tile(g, carry):
            slot = g % 2
            read(g, slot).wait()

            @pl.when(g + 1 < n_used)
            def _():
                read(g + 1, 1 - slot).start()

            @pl.when(g >= 2)
            def _():
                write(g - 2, slot).wait()

            lo, hi = _unpack_halves(xbuf[slot])
            gu = _dot(lo.astype(BF16), wgu_b[:half, :]) + _dot(hi.astype(BF16), wgu_b[half:, :])
            act = _silu(gu[:, :EXPERT_DIM]) * gu[:, EXPERT_DIM:]
            ybuf[slot] = _pack_halves(_dot(act.astype(BF16), wd_b[...]))
            write(g, slot).start()

            @pl.when(g == n_used - 1)
            def _():
                write(g, slot).wait()

                @pl.when(g >= 1)
                def _():
                    write(g - 1, 1 - slot).wait()

            return carry

        lax.fori_loop(t0_ref[e], t1_ref[e], tile, 0)


def _experts(tile_lo, tile_hi, xs, wgu, wd):
    r, w = xs.shape
    tm = EXPERT_TILE
    ne, d = wgu.shape[0], wgu.shape[1]
    return pl.pallas_call(
        _expert_kernel,
        grid_spec=pltpu.PrefetchScalarGridSpec(
            num_scalar_prefetch=2,
            grid=(ne,),
            in_specs=[pl.BlockSpec(memory_space=pl.ANY),
                      pl.BlockSpec((1, d, 2 * EXPERT_DIM), lambda e, t0, t1: (e, 0, 0)),
                      pl.BlockSpec((1, EXPERT_DIM, d), lambda e, t0, t1: (e, 0, 0))],
            out_specs=pl.BlockSpec(memory_space=pl.ANY),
            scratch_shapes=[pltpu.VMEM((2, tm, w), jnp.int32), pltpu.VMEM((2, tm, w), jnp.int32),
                            pltpu.SemaphoreType.DMA((2,)), pltpu.SemaphoreType.DMA((2,)),
                            pltpu.VMEM((d, 2 * EXPERT_DIM), BF16), pltpu.VMEM((EXPERT_DIM, d), BF16)]),
        out_shape=jax.ShapeDtypeStruct((r, w), jnp.int32),
        compiler_params=_params("arbitrary"),
        name="experts",
    )(tile_lo, tile_hi, xs, wgu, wd)


def _final_kernel(yk_ref, wk_ref, h_ref, x1_ref, mod_ref, wsgu_ref, wsd_ref, out_ref):
    half = yk_ref.shape[2]
    acc_lo = jnp.zeros((yk_ref.shape[1], half), F32)
    acc_hi = jnp.zeros((yk_ref.shape[1], half), F32)
    for k in range(TOP_K):
        lo, hi = _unpack_halves(yk_ref[k])
        w = wk_ref[:, k:k + 1]
        acc_lo = acc_lo + jnp.where(w != 0.0, w * lo, 0.0)
        acc_hi = acc_hi + jnp.where(w != 0.0, w * hi, 0.0)
    routed = jnp.concatenate([acc_lo, acc_hi], axis=1)
    sgu = _dot(h_ref[...], wsgu_ref[...])
    act = _silu(sgu[:, :SHARED_DIM]) * sgu[:, SHARED_DIM:]
    shared = _dot(act.astype(BF16), wsd_ref[...])
    out_ref[...] = x1_ref[...] + mod_ref[0, 5:6, :] * (routed + shared)


def _final(yk, wk_t, h2, x1, mod, wsgu, wsd, tiles_per_batch):
    t, d = h2.shape
    tm = ROW_TILE
    row = lambda w: pl.BlockSpec((tm, w), lambda i: (i, 0))
    return pl.pallas_call(
        _final_kernel,
        grid=(t // tm,),
        in_specs=[pl.BlockSpec((TOP_K, tm, d // 2), lambda i: (0, i, 0)),
                  row(TOP_K), row(d), row(d),
                  pl.BlockSpec((1, 6, d), lambda i: (i // tiles_per_batch, 0, 0)),
                  _const_spec(wsgu.shape), _const_spec(wsd.shape)],
        out_specs=row(d),
        out_shape=jax.ShapeDtypeStruct((t, d), F32),
        compiler_params=_params("arbitrary"),
        name="final",
    )(yk, wk_t, h2, x1, mod, wsgu, wsd)


def _pad_heads(w):
    d = w.shape[0]
    w = w.reshape(d, N_HEADS, HEAD_DIM)
    return jnp.pad(w, ((0, 0), (0, 0), (0, HEAD_PAD - HEAD_DIM))).reshape(d, N_HEADS * HEAD_PAD)


def _placement():
    pq = np.zeros((LANES, N_HEADS * HEAD_PAD), np.float32)
    pk = np.zeros((LANES, N_HEADS * HEAD_PAD), np.float32)
    cq = np.zeros((1, N_HEADS * HEAD_PAD), np.float32)
    ck = np.zeros((1, N_HEADS * HEAD_PAD), np.float32)
    for hd in range(N_HEADS):
        for k in range(3):
            pq[k * N_HEADS + hd, hd * HEAD_PAD + AUG0 + k] = 1.0
            ck[0, hd * HEAD_PAD + AUG0 + k] = 1.0
            pk[k * N_HEADS + hd, hd * HEAD_PAD + AUG0 + 3 + k] = -1.0
            cq[0, hd * HEAD_PAD + AUG0 + 3 + k] = 1.0
    return (jnp.asarray(pq, BF16), jnp.asarray(pk, BF16), jnp.asarray(cq), jnp.asarray(ck))


def kernel(x, c, w_ada, b_ada, norm1_g, w_in, w_dw, b_dw, conv_gn_g, conv_gn_b, w_conv_out,
           q_norm_g, k_norm_g, b_forget, w_attn_out, w_out, norm2_g, w_router, router_bias,
           w_experts_gate_up, w_experts_down, w_shared_gate_up, w_shared_down):
    depth = w_ada.shape[0]
    b, s, d = x.shape
    off_q = 2 * CONV_DIM
    off_f = off_q + 3 * ATTN_DIM
    off_gc = off_f + N_HEADS
    off_ga = off_gc + d

    pq, pk, cq, ck = _placement()
    tri = jnp.asarray(np.tril(np.ones((ROW_TILE, ROW_TILE), np.float32)), BF16)
    grp = np.arange(CONV_DIM) // (CONV_DIM // CONV_GROUPS)
    gg = jnp.asarray((grp[:, None] == grp[None, :]).astype(np.float32) / (CONV_DIM // CONV_GROUPS), BF16)
    c_pad = jnp.pad(c, ((0, SUBLANES - b), (0, 0)))
    tri_pos = jnp.asarray(np.triu(np.ones((POS_TILE, POS_TILE), np.float32)), BF16)
    lstrict = jnp.asarray(np.tril(np.ones((N_EXPERTS, N_EXPERTS), np.float32), -1), BF16)

    for l in range(depth):
        mod = _ada(c_pad, w_ada[l], b_ada[l][None, :])[:b].reshape(b, 6, d)

        wi = w_in[l]
        bf = jnp.pad(b_forget[l][None, :], ((0, 0), (0, LANES - N_HEADS)))
        wf = jnp.pad(wi[:, off_f:off_gc], ((0, 0), (0, LANES - N_HEADS))).astype(BF16)
        gpad = lambda g, sc: jnp.tile(jnp.pad(g * sc, (0, HEAD_PAD - HEAD_DIM)), N_HEADS)[None, :]
        qscale = HEAD_DIM ** -0.5
        u, qa, ka, v, sgc, sga, cum = _inproj(
            x, mod, norm1_g[l][None, :],
            wi[:, :off_q].astype(BF16),
            _pad_heads(wi[:, off_q:off_q + ATTN_DIM]).astype(BF16),
            _pad_heads(wi[:, off_q + ATTN_DIM:off_q + 2 * ATTN_DIM]).astype(BF16),
            wi[:, off_q + 2 * ATTN_DIM:off_f].astype(BF16),
            wf,wi[:, off_gc:off_ga].astype(BF16), wi[:, off_ga:].astype(BF16),
            bf, gpad(q_norm_g[l], qscale), gpad(k_norm_g[l], 1.0),
            pq, pk, cq, ck, tri)

        flat = lambda a: a[:, :, :N_HEADS].transpose(0, 2, 1).reshape(-1)
        cs = flat(cum[:, 0::ATTN_TILE])
        ce = flat(cum[:, ATTN_TILE - 1::ATTN_TILE])
        bound = (1.02 * HEAD_DIM * qscale) * jnp.max(jnp.abs(q_norm_g[l])) * jnp.max(jnp.abs(k_norm_g[l]))
        o = _attention(cs, ce, bound.reshape(1), qa, ka, v)

        wdw = jnp.pad(w_dw[l], ((0, CONV_HALO - CONV_WIDTH), (0, 0)))
        x1 = _merge(u, o, sgc, sga, x, mod, wdw, b_dw[l][None, :], conv_gn_g[l][None, :],
                    conv_gn_b[l][None, :], gg, w_conv_out[l].astype(BF16), w_attn_out[l].astype(BF16),
                    w_out[l].astype(BF16))

        wr = w_router[l].T
        wr_hi = wr.astype(BF16)
        wr_lo = (wr - wr_hi.astype(F32)).astype(BF16)
        h2, h2w, comb_t = _router(x1, mod, norm2_g[l][None, :], wr_hi, wr_lo, router_bias[l][:, None])

        t = b * s
        posk, wk, gend = _positions(comb_t, tri_pos, lstrict)
        n_tiles = (t * TOP_K) // EXPERT_TILE + N_EXPERTS
        tile_hi = gend[:, 0].astype(jnp.int32) // EXPERT_TILE
        tile_lo = jnp.concatenate([jnp.zeros((1,), jnp.int32), tile_hi[:-1]])

        xs = _sc_scatter_rows(h2w.reshape(t, d // 2), posk, n_tiles * EXPERT_TILE)
        ys = _experts(tile_lo, tile_hi, xs, w_experts_gate_up[l], w_experts_down[l])
        yk = _sc_gather_rows(ys, posk.reshape(-1)).reshape(TOP_K, t, d // 2)
        out = _final(yk, wk.T, h2.reshape(t, d), x1.reshape(t, d), mod,
                     w_shared_gate_up[l].astype(BF16), w_shared_down[l].astype(BF16), s // ROW_TILE)
        x = out.reshape(b, s, d)
    return x
```

```python
import functools

import numpy as np
import jax
import jax.numpy as jnp
from jax import lax
from jax.experimental import pallas as pl
from jax.experimental.pallas import tpu as pltpu
from jax.experimental.pallas import tpu_sc as plsc

F32 = jnp.float32
BF16 = jnp.bfloat16

CONV_DIM = 512
CONV_WIDTH = 31
CONV_GROUPS = 8
N_HEADS = 8
HEAD_DIM = 64
ATTN_DIM = N_HEADS * HEAD_DIM
N_EXPERTS = 64
TOP_K = 8
N_GROUPS = 8
TOPK_GROUPS = 4
EXPERT_DIM = 256
SHARED_DIM = 256
ROUTED_SCALE = 2.5
EPS = 1e-6

LANES = 128
SUBLANES = 8
HEAD_PAD = LANES
AUG0 = HEAD_DIM
VMEM_LIMIT = 56 * 1024 * 1024

ROW_TILE = 512
ATTN_TILE = 512
ATTN_KV = 256
EXPERT_TILE = 512
POS_TILE = 512
SC_CHUNK = 128
CONV_HALO = 32

NEG_BIG = -1e30
EXP_UNDERFLOW = 104.0
FIXED_SHIFT_BOUND = 40.0


def _dot(a, b):
    return jnp.dot(a, b, preferred_element_type=F32)


def _dot_nt(a, b):
    return lax.dot_general(a, b, (((1,), (1,)), ((), ())), preferred_element_type=F32)


def _split2(x):
    hi = x.astype(BF16)
    lo = (x - hi.astype(F32)).astype(BF16)
    return hi, lo


def _split3(x):
    hi = x.astype(BF16)
    r = x - hi.astype(F32)
    mid = r.astype(BF16)
    lo = (r - mid.astype(F32)).astype(BF16)
    return hi, mid, lo


def _pack_halves(v):
    n = v.shape[1] // 2
    lo = lax.bitcast_convert_type(v[:, :n].astype(BF16).astype(F32), jnp.uint32)
    hi = lax.bitcast_convert_type(v[:, n:].astype(BF16).astype(F32), jnp.uint32)
    return lax.bitcast_convert_type(hi | lax.shift_right_logical(lo, jnp.uint32(16)), jnp.int32)


def _unpack_halves(w):
    u = lax.bitcast_convert_type(w, jnp.uint32)
    lo = lax.bitcast_convert_type(lax.shift_left(u, jnp.uint32(16)), F32)
    hi = lax.bitcast_convert_type(u & jnp.uint32(0xFFFF0000), F32)
    return lo, hi


def _sigmoid(x):
    return 1.0 / (1.0 + jnp.exp(-x))


def _silu(x):
    return x * _sigmoid(x)


def _params(*sem):
    return pltpu.CompilerParams(dimension_semantics=sem, vmem_limit_bytes=VMEM_LIMIT)


def _const_spec(shape):
    n = len(shape)
    return pl.BlockSpec(shape, lambda *_: (0,) * n, pipeline_mode=pl.Buffered(1))


def _ada_kernel(c_ref, w_ref, b_ref, o_ref):
    c = c_ref[...]
    a_hi, a_lo = _split2(_silu(c))
    w_hi, w_lo = _split2(w_ref[...])
    o_ref[...] = _dot(a_hi, w_hi) + _dot(a_hi, w_lo) + _dot(a_lo, w_hi) + b_ref[...]


def _ada(c_pad, w_ada, b_ada):
    d = c_pad.shape[1]
    n = w_ada.shape[1]
    return pl.pallas_call(
        _ada_kernel,
        grid=(n // d,),
        in_specs=[_const_spec(c_pad.shape),
                  pl.BlockSpec((d, d), lambda j: (0, j)),
                  pl.BlockSpec((1, d), lambda j: (0, j))],
        out_specs=pl.BlockSpec((c_pad.shape[0], d), lambda j: (0, j)),
        out_shape=jax.ShapeDtypeStruct((c_pad.shape[0], n), F32),
        compiler_params=_params("arbitrary"),
        name="ada",
    )(c_pad, w_ada, b_ada)


def _lane_pieces(x):
    hi, mid, lo = _split3(x)
    return (hi.astype(F32) + pltpu.roll(mid.astype(F32), N_HEADS, 1)
            + pltpu.roll(lo.astype(F32), 2 * N_HEADS, 1)).astype(BF16)


def _head_tile(p, col0, hd):
    per = HEAD_PAD // HEAD_DIM
    g, part = divmod(hd, per)
    blk = p[:, col0 + g * HEAD_PAD:col0 + (g + 1) * HEAD_PAD]
    if part:
        blk = pltpu.roll(blk, HEAD_PAD - part * HEAD_DIM, 1)
    lane = lax.broadcasted_iota(jnp.int32, (1, HEAD_PAD), 1)
    return jnp.where(lane < HEAD_DIM, blk, 0.0)


def _inproj_kernel(x_ref, mod_ref, g1_ref, wcv_ref, wq_ref, wk_ref, wv_ref, wf_ref, wgc_ref,
                   wga_ref, bf_ref, qg_ref, kg_ref, pq_ref, pk_ref, cq_ref, ck_ref, tri_ref,
                   u_ref, qa_ref, ka_ref, v_ref, sgc_ref, sga_ref, cum_ref, carry_ref):
    @pl.when(pl.program_id(1) == 0)
    def _():
        carry_ref[...] = jnp.zeros_like(carry_ref)

    x = x_ref[0]
    ms = jnp.mean(x * x, axis=-1, keepdims=True)
    h = (x * lax.rsqrt(ms + EPS) * g1_ref[...]) * (1.0 + mod_ref[0, 1:2, :]) + mod_ref[0, 0:1, :]
    hb = h.astype(BF16)

    pc = _dot(hb, wcv_ref[...])
    u_ref[0] = (pc[:, :CONV_DIM] * _sigmoid(pc[:, CONV_DIM:])).astype(BF16)

    sgc_ref[0] = _sigmoid(_dot(hb, wgc_ref[...])).astype(BF16)
    sga_ref[0] = _sigmoid(_dot(hb, wga_ref[...])).astype(BF16)

    z = _dot(hb, wf_ref[...]) + bf_ref[...]
    lf = jnp.minimum(z, 0.0) - jnp.log1p(jnp.exp(-jnp.abs(z)))
    lane = lax.broadcasted_iota(jnp.int32, (1, LANES), 1)
    psum = _dot(tri_ref[...], _lane_pieces(jnp.where(lane < N_HEADS, lf, 0.0)))
    cum = (psum + pltpu.roll(psum, LANES - N_HEADS, 1) + pltpu.roll(psum, LANES - 2 * N_HEADS, 1)
           + carry_ref[...])
    cum = jnp.where(lane < N_HEADS, cum, 0.0)
    carry_ref[...] = cum[ROW_TILE - 1:ROW_TILE, :]
    cum_ref[0] = cum

    pieces = _lane_pieces(cum)
    addq = _dot(pieces, pq_ref[...]) + cq_ref[...]
    addk = _dot(pieces, pk_ref[...]) + ck_ref[...]

    pq = _dot(hb, wq_ref[...])
    pk = _dot(hb, wk_ref[...])
    pv = _dot(hb, wv_ref[...])
    inv_hd = 1.0 / HEAD_DIM
    vone = (lax.broadcasted_iota(jnp.int32, (1, HEAD_PAD), 1) == HEAD_DIM).astype(F32)
    for hd in range(N_HEADS):
        sl = slice(hd * HEAD_PAD, (hd + 1) * HEAD_PAD)
        qb = pq[:, sl]
        qn = qb * lax.rsqrt(jnp.sum(qb * qb, axis=-1, keepdims=True) * inv_hd + EPS) * qg_ref[:, sl]
        qa_ref[0, hd] = (qn + addq[:, sl]).astype(BF16)
        kb = pk[:, sl]
        kn = kb * lax.rsqrt(jnp.sum(kb * kb, axis=-1, keepdims=True) * inv_hd + EPS) * kg_ref[:, sl]
        ka_ref[0, hd] = (kn + addk[:, sl]).astype(BF16)
        v_ref[0, hd] = (_head_tile(pv, 0, hd) + vone).astype(BF16)


def _inproj(x, mod, g1, wcv, wq, wk, wv, wf, wgc, wga, bf, qg, kg, pq, pk, cq, ck, tri):
    b, s, d = x.shape
    tm = ROW_TILE
    row = lambda w: pl.BlockSpec((1, tm, w), lambda bi, i: (bi, i, 0))
    head = pl.BlockSpec((1, N_HEADS, tm, HEAD_PAD), lambda bi, i: (bi, 0, i, 0))
    consts = [g1, wcv, wq, wk, wv, wf, wgc, wga, bf, qg, kg, pq, pk, cq, ck, tri]
    return pl.pallas_call(
        _inproj_kernel,
        grid=(b, s // tm),
        in_specs=[row(d), pl.BlockSpec((1, 6, d), lambda bi, i: (bi, 0, 0))]
                 + [_const_spec(a.shape) for a in consts],
        out_specs=[row(CONV_DIM), head, head, head, row(d), row(d), row(LANES)],
        out_shape=[jax.ShapeDtypeStruct((b, s, CONV_DIM), BF16),
                   jax.ShapeDtypeStruct((b, N_HEADS, s, HEAD_PAD), BF16),
                   jax.ShapeDtypeStruct((b, N_HEADS, s, HEAD_PAD), BF16),
                   jax.ShapeDtypeStruct((b, N_HEADS, s, HEAD_PAD), BF16),
                   jax.ShapeDtypeStruct((b, s, d), BF16),
                   jax.ShapeDtypeStruct((b, s, d), BF16),
                   jax.ShapeDtypeStruct((b, s, LANES), F32)],
        scratch_shapes=[pltpu.VMEM((1, LANES), F32)],
        compiler_params=_params("arbitrary", "arbitrary"),
        name="inproj",
    )(x, mod, *consts)


def _attn_kernel(cs_ref, ce_ref, prm_ref, q_ref, k_ref, v_ref, o_ref, acc_ref, m_ref):
    tq, tk = ATTN_TILE, ATTN_KV
    per = tq // tk
    nq = q_ref.shape[2] // tq
    bh = pl.program_id(0) * pl.num_programs(1) + pl.program_id(1)
    bound = prm_ref[0]
    thresh = -(EXP_UNDERFLOW + 2.0 * bound)

    def keys(j):
        k0 = pl.multiple_of(j * tk, tk)
        return k_ref[0, 0, pl.ds(k0, tk), :], v_ref[0, 0, pl.ds(k0, tk), :]

    def skip_count(i):
        c0 = cs_ref[bh * nq + i]
        base = bh * nq * per
        return lax.fori_loop(0, i * per, lambda j, n: n + (c0 - ce_ref[base + j] < thresh).astype(jnp.int32), 0)

    def causal(shape):
        return lax.broadcasted_iota(jnp.int32, shape, 0) >= lax.broadcasted_iota(jnp.int32, shape, 1)

    def finish(q0):
        acc = acc_ref[...]
        o_ref[0, pl.ds(q0, tq), :] = (acc / acc[:, HEAD_DIM:HEAD_DIM + 1]).astype(BF16)

    def fixed_shift(i, carry):
        q0 = pl.multiple_of(i * tq, tq)
        q = q_ref[0, 0, pl.ds(q0, tq), :]
        acc_ref[...] = jnp.zeros_like(acc_ref)

        def weighted(j):
            kb, vb = keys(j)
            return _dot(jnp.exp(_dot_nt(q, kb)).astype(BF16), vb)

        first = skip_count(i)
        pairs = (i * per - first) // 2

        def kv2(jj, c):
            j = first + 2 * jj
            acc_ref[...] += weighted(j) + weighted(j + 1)
            return c

        lax.fori_loop(0, pairs, kv2, 0)

        @pl.when(first + 2 * pairs < i * per)
        def _():
            acc_ref[...] += weighted(i * per - 1)

        for d in range(per):
            kb, vb = keys(i * per + d)
            s = _dot_nt(q[d * tk:, :], kb)
            acc_ref[d * tk:, :] += _dot(jnp.exp(jnp.where(causal(s.shape), s, NEG_BIG)).astype(BF16), vb)
        finish(q0)
        return carry

    def running_max(i, carry):
        q0 = pl.multiple_of(i * tq, tq)
        q = q_ref[0, 0, pl.ds(q0, tq), :]
        m_ref[...] = jnp.full_like(m_ref, -jnp.inf)
        acc_ref[...] = jnp.zeros_like(acc_ref)

        def step(j, r0, masked):
            kb, vb = keys(j)
            s = _dot_nt(q[r0:, :], kb)
            if masked:
                s = jnp.where(causal(s.shape), s, NEG_BIG)
            m_prev = m_ref[r0:, :]
            m_new = jnp.maximum(m_prev, jnp.max(s, axis=-1, keepdims=True))
            p = jnp.exp(s - m_new)
            acc_ref[r0:, :] = jnp.exp(m_prev - m_new) * acc_ref[r0:, :] + _dot(p.astype(BF16), vb)
            m_ref[r0:, :] = m_new

        def kv(j, c):
            step(j, 0, False)
            return c

        lax.fori_loop(skip_count(i), i * per, kv, 0)
        for d in range(per):
            step(i * per + d, d * tk, True)
        finish(q0)
        return carry

    @pl.when(bound <= FIXED_SHIFT_BOUND)
    def _():
        lax.fori_loop(0, nq, fixed_shift, 0)

    @pl.when(bound > FIXED_SHIFT_BOUND)
    def _():
        lax.fori_loop(0, nq, running_max, 0)


def _attention(cs, ce, prm, qa, ka, v):
    b, nh, s, hp = qa.shape
    t = ATTN_TILE
    seq = pl.BlockSpec((1, 1, s, hp), lambda bi, hi, *_: (bi, hi, 0, 0))
    return pl.pallas_call(
        _attn_kernel,
        grid_spec=pltpu.PrefetchScalarGridSpec(
            num_scalar_prefetch=3,
            grid=(b, nh),
            in_specs=[seq, seq, seq],
            out_specs=pl.BlockSpec((1, s, hp), lambda bi, hi, *_: (bi, 0, hi)),
            scratch_shapes=[pltpu.VMEM((t, hp), F32), pltpu.VMEM((t, 1), F32)]),
        out_shape=jax.ShapeDtypeStruct((b, s, nh * hp), BF16),
        compiler_params=_params("arbitrary", "arbitrary"),
        name="attn",
    )(cs, ce, prm, qa, ka, v)


def _merge_kernel(u_ref, halo_ref, o_ref, sgc_ref, sga_ref, x_ref, mod_ref, wdw_ref, bdw_ref,
                  gng_ref, gnb_ref, gg_ref, wco_ref, wao_ref, wout_ref, x1_ref, buf_ref):
    tm = ROW_TILE
    halo = halo_ref[0].astype(F32)
    halo = jnp.where(pl.program_id(1) == 0, jnp.zeros_like(halo), halo)
    ucur = u_ref[0].astype(F32)
    for cb in range(CONV_DIM // LANES):
        buf_ref[cb, 0:CONV_HALO, :] = halo[:, cb * LANES:(cb + 1) * LANES]
        buf_ref[cb, CONV_HALO:, :] = ucur[:, cb * LANES:(cb + 1) * LANES]

    base = CONV_HALO - (CONV_WIDTH - 1)
    ys = []
    for cb in range(CONV_DIM // LANES):
        acc = jnp.zeros((tm, LANES), F32)
        for j in range(CONV_WIDTH):
            acc = acc + wdw_ref[j:j + 1, cb * LANES:(cb + 1) * LANES] * buf_ref[cb, base + j:base + j + tm, :]
        ys.append(acc)
    y = jnp.concatenate(ys, axis=1) + bdw_ref[...]

    gg = gg_ref[...]
    y_hi, y_lo = _split2(y)
    dlt = y - (_dot(y_hi, gg) + _dot(y_lo, gg))
    s_hi, s_lo = _split2(dlt * dlt)
    var = _dot(s_hi, gg) + _dot(s_lo, gg)
    yn = dlt * lax.rsqrt(var + EPS) * gng_ref[...] + gnb_ref[...]
    y_conv = _dot(_silu(yn).astype(BF16), wco_ref[...])

    per = HEAD_PAD // HEAD_DIM
    lane = lax.broadcasted_iota(jnp.int32, (1, HEAD_PAD), 1)
    packed = []
    for g in range(N_HEADS // per):
        tile = o_ref[0, :, g * per * HEAD_PAD:(g * per + 1) * HEAD_PAD].astype(F32)
        for part in range(1, per):
            nxt = o_ref[0, :, (g * per + part) * HEAD_PAD:(g * per + part + 1) * HEAD_PAD].astype(F32)
            tile = jnp.where(lane < part * HEAD_DIM, tile, pltpu.roll(nxt, part * HEAD_DIM, 1))
        packed.append(tile)
    y_attn = _dot(jnp.concatenate(packed, axis=1).astype(BF16), wao_ref[...])
    merged = sgc_ref[0].astype(F32) * y_conv + sga_ref[0].astype(F32) * y_attn
    mix = _dot(merged.astype(BF16), wout_ref[...])
    x1_ref[0] = x_ref[0] + mod_ref[0, 2:3, :] * mix


def _merge(u, o, sgc, sga, x, mod, wdw, bdw, gng, gnb, gg, wco, wao, wout):
    b, s, d = x.shape
    tm = ROW_TILE
    per = tm // CONV_HALO
    row = lambda w: pl.BlockSpec((1, tm, w), lambda bi, i: (bi, i, 0))
    consts = [wdw, bdw, gng, gnb, gg, wco, wao, wout]
    return pl.pallas_call(
        _merge_kernel,
        grid=(b, s // tm),
        in_specs=[row(CONV_DIM),
                  pl.BlockSpec((1, CONV_HALO, CONV_DIM),
                               lambda bi, i: (bi, jnp.maximum(i * per - 1, 0), 0)),
                  row(o.shape[2]), row(d), row(d), row(d),
                  pl.BlockSpec((1, 6, d), lambda bi, i: (bi, 0, 0))]
                 + [_const_spec(a.shape) for a in consts],
        out_specs=row(d),
        out_shape=jax.ShapeDtypeStruct((b, s, d), F32),
        scratch_shapes=[pltpu.VMEM((CONV_DIM // LANES, CONV_HALO + tm, LANES), F32)],
        compiler_params=_params("arbitrary", "arbitrary"),
        name="merge",
    )(u, u, o, sgc, sga, x, mod, *consts)


def _router_kernel(x1_ref, mod_ref, g2_ref, wr_hi_ref, wr_lo_ref, rb_ref, h2_ref, h2w_ref, comb_ref):
    x = x1_ref[0]
    ms = jnp.mean(x * x, axis=-1, keepdims=True)
    h = (x * lax.rsqrt(ms + EPS) * g2_ref[...]) * (1.0 + mod_ref[0, 4:5, :]) + mod_ref[0, 3:4, :]
    h2_ref[0] = h.astype(BF16)
    h2w_ref[0] = _pack_halves(h)

    h_hi, h_lo = _split2(h)
    logits = _dot_nt(wr_hi_ref[...], h_hi) + _dot_nt(wr_hi_ref[...], h_lo) + _dot_nt(wr_lo_ref[...], h_hi)
    scores = _sigmoid(logits)
    biased = scores + rb_ref[...]

    per = N_EXPERTS // N_GROUPS
    rows = lax.broadcasted_iota(jnp.int32, (per, biased.shape[1]), 0)
    gscore = []
    for g in range(N_GROUPS):
        blk = biased[g * per:(g + 1) * per, :]
        top1 = jnp.max(blk, axis=0, keepdims=True)
        first = jnp.min(jnp.where(blk == top1, rows, per), axis=0, keepdims=True)
        top2 = jnp.max(jnp.where(rows == first, -jnp.inf, blk), axis=0, keepdims=True)
        gscore.append(top1 + top2)

    cand = []
    for g in range(N_GROUPS):
        rank = jnp.zeros_like(gscore[g], dtype=jnp.int32)
        for g2 in range(N_GROUPS):
            if g2 == g:
                continue
            ahead = gscore[g2] > gscore[g]
            if g2 < g:
                ahead = ahead | (gscore[g2] == gscore[g])
            rank = rank + ahead.astype(jnp.int32)
        keep = rank < TOPK_GROUPS
        cand.append(jnp.where(keep, biased[g * per:(g + 1) * per, :], -jnp.inf))
    cand = jnp.concatenate(cand, axis=0)

    eidx = lax.broadcasted_iota(jnp.int32, cand.shape, 0)
    work = cand
    for _ in range(TOP_K):
        top = jnp.max(work, axis=0, keepdims=True)
        first = jnp.min(jnp.where(work == top, eidx, N_EXPERTS), axis=0, keepdims=True)
        work = jnp.where(eidx == first, -jnp.inf, work)
    sel = (work != cand) & (cand > -jnp.inf)
    w = jnp.where(sel, scores, 0.0)
    comb_ref[...] = w / jnp.sum(w, axis=0, keepdims=True) * ROUTED_SCALE


def _router(x1, mod, g2, wr_hi, wr_lo, rb):
    b, s, d = x1.shape
    tm = ROW_TILE
    nt = s // tm
    return pl.pallas_call(
        _router_kernel,
        grid=(b, nt),
        in_specs=[pl.BlockSpec((1, tm, d), lambda bi, i: (bi, i, 0)),
                  pl.BlockSpec((1, 6, d), lambda bi, i: (bi, 0, 0)),
                  _const_spec(g2.shape), _const_spec(wr_hi.shape), _const_spec(wr_lo.shape),
                  _const_spec(rb.shape)],
        out_specs=[pl.BlockSpec((1, tm, d), lambda bi, i: (bi, i, 0)),
                   pl.BlockSpec((1, tm, d // 2), lambda bi, i: (bi, i, 0)),
                   pl.BlockSpec((N_EXPERTS, tm), lambda bi, i: (0, bi * nt + i))],
        out_shape=[jax.ShapeDtypeStruct((b, s, d), BF16),
                   jax.ShapeDtypeStruct((b, s, d // 2), jnp.int32),
                   jax.ShapeDtypeStruct((N_EXPERTS, b * s), F32)],
        compiler_params=_params("arbitrary", "arbitrary"),
        name="router",
    )(x1, mod, g2, wr_hi, wr_lo, rb)


def _pos_kernel(comb_ref, tri_ref, lstrict_ref, posk_ref, wk_ref, gend_ref, cnt_ref, base_ref):
    phase = pl.program_id(0)
    i = pl.program_id(1)
    tp = comb_ref.shape[1]
    comb = comb_ref[...]
    sel = comb != 0.0
    selb = jnp.where(sel, 1.0, 0.0).astype(BF16)
    tile_count = _dot(selb, jnp.ones((tp, tp), BF16))

    @pl.when(phase == 0)
    def _():
        @pl.when(i == 0)
        def _():
            cnt_ref[...] = jnp.zeros_like(cnt_ref)
            gend_ref[...] = jnp.zeros_like(gend_ref)

        cnt_ref[...] += tile_count
        posk_ref[...] = jnp.zeros_like(posk_ref)
        wk_ref[...] = jnp.zeros_like(wk_ref)

    @pl.when(phase == 1)
    def _():
        @pl.when(i == 0)
        def _():
            seg = jnp.floor((cnt_ref[...] + (EXPERT_TILE - 1.0)) * (1.0 / EXPERT_TILE)) * EXPERT_TILE
            s_hi, s_mid, s_lo = _split3(seg)
            ls = lstrict_ref[...]
            start = _dot(ls, s_hi) + _dot(ls, s_mid) + _dot(ls, s_lo)
            base_ref[...] = start
            gend_ref[...] = (start + seg)[:, :LANES]

        rank = _dot(selb, tri_ref[...])
        pos = base_ref[...] + rank - 1.0
        base_ref[...] += tile_count
        slot = _dot(lstrict_ref[...], selb)
        rows_p, rows_w = [], []
        for k in range(TOP_K):
            m = sel & (slot == k)
            rows_p.append(jnp.sum(jnp.where(m, pos, 0.0), axis=0, keepdims=True))
            rows_w.append(jnp.sum(jnp.where(m, comb, 0.0), axis=0, keepdims=True))
        posk_ref[...] = jnp.concatenate(rows_p, axis=0).astype(jnp.int32)
        wk_ref[...] = jnp.concatenate(rows_w, axis=0)


def _positions(comb_t, tri, lstrict):
    ne, t = comb_t.shape
    tp = POS_TILE
    out_tok = pl.BlockSpec((TOP_K, tp), lambda p, i: (0, i * p))
    return pl.pallas_call(
        _pos_kernel,
        grid=(2, t // tp),
        in_specs=[pl.BlockSpec((ne, tp), lambda p, i: (0, i)),
                  _const_spec(tri.shape), _const_spec(lstrict.shape)],
        out_specs=[out_tok, out_tok, pl.BlockSpec((ne, LANES), lambda p, i: (0, 0))],
        out_shape=[jax.ShapeDtypeStruct((TOP_K, t), jnp.int32),
                   jax.ShapeDtypeStruct((TOP_K, t), F32),
                   jax.ShapeDtypeStruct((ne, LANES), F32)],
        scratch_shapes=[pltpu.VMEM((ne, tp), F32), pltpu.VMEM((ne, tp), F32)],
        compiler_params=_params("arbitrary", "arbitrary"),
        name="positions",
    )(comb_t, tri, lstrict)


def _sc_workers():
    info = plsc.get_sparse_core_info()
    return info.num_cores, info.num_cores * info.num_subcores


def _sc_scatter_rows(rows, pos, n_out):
    nc, nw = _sc_workers()
    n, w = rows.shape
    nk = pos.shape[0]
    ch = SC_CHUNK
    per_w = n // nw
    assert per_w * nw == n and per_w % ch == 0

    @functools.partial(
        pl.kernel, mesh=plsc.VectorSubcoreMesh(core_axis_name="c", subcore_axis_name="s"),
        out_type=jax.ShapeDtypeStruct((n_out, w), rows.dtype),
        scratch_types=[pltpu.VMEM((nk, ch), jnp.int32), pltpu.VMEM((ch, w), rows.dtype),
                       pltpu.SemaphoreType.DMA])
    def scatter(rows_hbm, pos_hbm, out_hbm, idx_v, rows_v, sem):
        base = (lax.axis_index("s") * nc + lax.axis_index("c")) * per_w

        @pl.loop(0, per_w // ch)
        def _(ci):
            off = pl.multiple_of(base + ci * ch, ch)
            pltpu.sync_copy(pos_hbm.at[:, pl.ds(off, ch)], idx_v)
            pltpu.sync_copy(rows_hbm.at[pl.ds(off, ch)], rows_v)
            copies = [pltpu.make_async_copy(rows_v, out_hbm.at[idx_v.at[k]], sem) for k in range(nk)]
            for cp in copies:
                cp.start()
            for cp in copies:
                cp.wait()

    return scatter(rows, pos)


def _sc_gather_rows(table, idx):
    nc, nw = _sc_workers()
    n = idx.shape[0]
    w = table.shape[1]
    ch = SC_CHUNK
    per_w = n // nw
    assert per_w * nw == n and per_w % ch == 0

    @functools.partial(
        pl.kernel, mesh=plsc.VectorSubcoreMesh(core_axis_name="c", subcore_axis_name="s"),
        out_type=jax.ShapeDtypeStruct((n, w), table.dtype),
        scratch_types=[pltpu.VMEM((ch,), jnp.int32), pltpu.VMEM((ch, w), table.dtype),
                       pltpu.SemaphoreType.DMA])
    def gather(table_hbm, idx_hbm, out_hbm, idx_v, rows_v, sem):
        base = (lax.axis_index("s") * nc + lax.axis_index("c")) * per_w

        @pl.loop(0, per_w // ch)
        def _(ci):
            off = pl.multiple_of(base + ci * ch, ch)
            pltpu.sync_copy(idx_hbm.at[pl.ds(off, ch)], idx_v)
            cp = pltpu.make_async_copy(table_hbm.at[idx_v], rows_v, sem)
            cp.start()
            cp.wait()
            pltpu.sync_copy(rows_v, out_hbm.at[pl.ds(off, ch)])

    return gather(table, idx)


def _expert_kernel(te_ref, nu_ref, xs_ref, wgu_ref, wd_ref, y_ref, wgu_b, wd_b):
    i = pl.program_id(0)

    @pl.when(i < nu_ref[0])
    def _():
        @pl.when((i == 0) | (te_ref[i] != te_ref[jnp.maximum(i - 1, 0)]))
        def _():
            wgu_b[...] = wgu_ref[0].astype(BF16)
            wd_b[...] = wd_ref[0].astype(BF16)

        lo, hi = _unpack_halves(xs_ref[...])
        half = wgu_b.shape[0] // 2
        gu = _dot(lo.astype(BF16), wgu_b[:half, :]) + _dot(hi.astype(BF16), wgu_b[half:, :])
        act = _silu(gu[:, :EXPERT_DIM]) * gu[:, EXPERT_DIM:]
        y_ref[...] = _pack_halves(_dot(act.astype(BF16), wd_b[...]))


def _experts(tile_e, n_used, xs, wgu, wd):
    r, w = xs.shape
    tm = EXPERT_TILE
    d = wgu.shape[1]
    rows = pl.BlockSpec((tm, w), lambda i, te, nu: (jnp.minimum(i, nu[0] - 1), 0))
    return pl.pallas_call(
        _expert_kernel,
        grid_spec=pltpu.PrefetchScalarGridSpec(
            num_scalar_prefetch=2,
            grid=(r // tm,),
            in_specs=[rows,
                      pl.BlockSpec((1, d, 2 * EXPERT_DIM), lambda i, te, nu: (te[i], 0, 0)),
                      pl.BlockSpec((1, EXPERT_DIM, d), lambda i, te, nu: (te[i], 0, 0))],
            out_specs=rows,
            scratch_shapes=[pltpu.VMEM((d, 2 * EXPERT_DIM), BF16), pltpu.VMEM((EXPERT_DIM, d), BF16)]),
        out_shape=jax.ShapeDtypeStruct((r, w), jnp.int32),
        compiler_params=_params("arbitrary"),
        name="experts",
    )(tile_e, n_used, xs, wgu, wd)


def _final_kernel(yk_ref, wk_ref, h_ref, x1_ref, mod_ref, wsgu_ref, wsd_ref, out_ref):
    half = yk_ref.shape[2]
    acc_lo = jnp.zeros((yk_ref.shape[1], half), F32)
    acc_hi = jnp.zeros((yk_ref.shape[1], half), F32)
    for k in range(TOP_K):
        lo, hi = _unpack_halves(yk_ref[k])
        w = wk_ref[:, k:k + 1]
        acc_lo = acc_lo + jnp.where(w != 0.0, w * lo, 0.0)
        acc_hi = acc_hi + jnp.where(w != 0.0, w * hi, 0.0)
    routed = jnp.concatenate([acc_lo, acc_hi], axis=1)
    sgu = _dot(h_ref[...], wsgu_ref[...])
    act = _silu(sgu[:, :SHARED_DIM]) * sgu[:, SHARED_DIM:]
    shared = _dot(act.astype(BF16), wsd_ref[...])
    out_ref[...] = x1_ref[...] + mod_ref[0, 5:6, :] * (routed + shared)


def _final(yk, wk_t, h2, x1, mod, wsgu, wsd, tiles_per_batch):
    t, d = h2.shape
    tm = ROW_TILE
    row = lambda w: pl.BlockSpec((tm, w), lambda i: (i, 0))
    return pl.pallas_call(
        _final_kernel,
        grid=(t // tm,),
        in_specs=[pl.BlockSpec((TOP_K, tm, d // 2), lambda i: (0, i, 0)),
                  row(TOP_K), row(d), row(d),
                  pl.BlockSpec((1, 6, d), lambda i: (i // tiles_per_batch, 0, 0)),
                  _const_spec(wsgu.shape), _const_spec(wsd.shape)],
        out_specs=row(d),
        out_shape=jax.ShapeDtypeStruct((t, d), F32),
        compiler_params=_params("arbitrary"),
        name="final",
    )(yk, wk_t, h2, x1, mod, wsgu, wsd)


def _pad_heads(w):
    d = w.shape[0]
    w = w.reshape(d, N_HEADS, HEAD_DIM)
    return jnp.pad(w, ((0, 0), (0, 0), (0, HEAD_PAD - HEAD_DIM))).reshape(d, N_HEADS * HEAD_PAD)


def _placement():
    pq = np.zeros((LANES, N_HEADS * HEAD_PAD), np.float32)
    pk = np.zeros((LANES, N_HEADS * HEAD_PAD), np.float32)
    cq = np.zeros((1, N_HEADS * HEAD_PAD), np.float32)
    ck = np.zeros((1, N_HEADS * HEAD_PAD), np.float32)
    for hd in range(N_HEADS):
        for k in range(3):
            pq[k * N_HEADS + hd, hd * HEAD_PAD + AUG0 + k] = 1.0
            ck[0, hd * HEAD_PAD + AUG0 + k] = 1.0
            pk[k * N_HEADS + hd, hd * HEAD_PAD + AUG0 + 3 + k] = -1.0
            cq[0, hd * HEAD_PAD + AUG0 + 3 + k] = 1.0
    return (jnp.asarray(pq, BF16), jnp.asarray(pk, BF16), jnp.asarray(cq), jnp.asarray(ck))


def kernel(x, c, w_ada, b_ada, norm1_g, w_in, w_dw, b_dw, conv_gn_g, conv_gn_b, w_conv_out,
           q_norm_g, k_norm_g, b_forget, w_attn_out, w_out, norm2_g, w_router, router_bias,
           w_experts_gate_up, w_experts_down, w_shared_gate_up, w_shared_down):
    depth = w_ada.shape[0]
    b, s, d = x.shape
    off_q = 2 * CONV_DIM
    off_f = off_q + 3 * ATTN_DIM
    off_gc = off_f + N_HEADS
    off_ga = off_gc + d

    pq, pk, cq, ck = _placement()
    tri = jnp.asarray(np.tril(np.ones((ROW_TILE, ROW_TILE), np.float32)), BF16)
    grp = np.arange(CONV_DIM) // (CONV_DIM // CONV_GROUPS)
    gg = jnp.asarray((grp[:, None] == grp[None, :]).astype(np.float32) / (CONV_DIM // CONV_GROUPS), BF16)
    c_pad = jnp.pad(c, ((0, SUBLANES - b), (0, 0)))
    tri_pos = jnp.asarray(np.triu(np.ones((POS_TILE, POS_TILE), np.float32)), BF16)
    lstrict = jnp.asarray(np.tril(np.ones((N_EXPERTS, N_EXPERTS), np.float32), -1), BF16)

    for l in range(depth):
        mod = _ada(c_pad, w_ada[l], b_ada[l][None, :])[:b].reshape(b, 6, d)

        wi = w_in[l]
        bf = jnp.pad(b_forget[l][None, :], ((0, 0), (0, LANES - N_HEADS)))
        wf = jnp.pad(wi[:, off_f:off_gc], ((0, 0), (0, LANES - N_HEADS))).astype(BF16)
        gpad = lambda g, sc: jnp.tile(jnp.pad(g * sc, (0, HEAD_PAD - HEAD_DIM)), N_HEADS)[None, :]
        qscale = HEAD_DIM ** -0.5
        u, qa, ka, v, sgc, sga, cum = _inproj(
            x, mod, norm1_g[l][None, :],
            wi[:, :off_q].astype(BF16),
            _pad_heads(wi[:, off_q:off_q + ATTN_DIM]).astype(BF16),
            _pad_heads(wi[:, off_q + ATTN_DIM:off_q + 2 * ATTN_DIM]).astype(BF16),
            wi[:, off_q + 2 * ATTN_DIM:off_f].astype(BF16),
            wf, wi[:, off_gc:off_ga].astype(BF16), wi[:, off_ga:].astype(BF16),
            bf, gpad(q_norm_g[l], qscale), gpad(k_norm_g[l], 1.0),
            pq, pk, cq, ck, tri)

        flat = lambda a: a[:, :, :N_HEADS].transpose(0, 2, 1).reshape(-1)
        cs = flat(cum[:, 0::ATTN_TILE])
        ce = flat(cum[:, ATTN_KV - 1::ATTN_KV])
        bound = (1.02 * HEAD_DIM * qscale) * jnp.max(jnp.abs(q_norm_g[l])) * jnp.max(jnp.abs(k_norm_g[l]))
        o = _attention(cs, ce, bound.reshape(1), qa, ka, v)

        wdw = jnp.pad(w_dw[l], ((0, CONV_HALO - CONV_WIDTH), (0, 0)))
        x1 = _merge(u, o, sgc, sga, x, mod, wdw, b_dw[l][None, :], conv_gn_g[l][None, :],
                    conv_gn_b[l][None, :], gg, w_conv_out[l].astype(BF16), w_attn_out[l].astype(BF16),
                    w_out[l].astype(BF16))

        wr = w_router[l].T
        wr_hi = wr.astype(BF16)
        wr_lo = (wr - wr_hi.astype(F32)).astype(BF16)
        h2, h2w, comb_t = _router(x1, mod, norm2_g[l][None, :], wr_hi, wr_lo, router_bias[l][:, None])

        t = b * s
        posk, wk, gend = _positions(comb_t, tri_pos, lstrict)
        n_tiles = (t * TOP_K) // EXPERT_TILE + N_EXPERTS
        seg_end = gend[:, 0].astype(jnp.int32)
        n_used = seg_end[-1:] // EXPERT_TILE
        tile_start = jnp.arange(n_tiles, dtype=jnp.int32) * EXPERT_TILE
        tile_start = jnp.minimum(tile_start, seg_end[-1] - EXPERT_TILE)
        tile_e = jnp.sum((seg_end[None, :] <= tile_start[:, None]).astype(jnp.int32), axis=1)

        xs = _sc_scatter_rows(h2w.reshape(t, d // 2), posk, n_tiles * EXPERT_TILE)
        ys = _experts(tile_e, n_used, xs, w_experts_gate_up[l], w_experts_down[l])
        yk = _sc_gather_rows(ys, posk.reshape(-1)).reshape(TOP_K, t, d // 2)
        out = _final(yk, wk.T, h2.reshape(t, d), x1.reshape(t, d), mod,
                     w_shared_gate_up[l].astype(BF16), w_shared_down[l].astype(BF16), s // ROW_TILE)
        x = out.reshape(b, s, d)
    return x
```

```python
import functools

import numpy as np
import jax
import jax.numpy as jnp
from jax import lax
from jax.experimental import pallas as pl
from jax.experimental.pallas import tpu as pltpu
from jax.experimental.pallas import tpu_sc as plsc

F32 = jnp.float32
BF16 = jnp.bfloat16

CONV_DIM = 512
CONV_WIDTH = 31
CONV_GROUPS = 8
N_HEADS = 8
HEAD_DIM = 64
ATTN_DIM = N_HEADS * HEAD_DIM
N_EXPERTS = 64
TOP_K = 8
N_GROUPS = 8
TOPK_GROUPS = 4
EXPERT_DIM = 256
SHARED_DIM = 256
ROUTED_SCALE = 2.5
EPS = 1e-6

LANES = 128
SUBLANES = 8
HEAD_PAD = LANES
AUG0 = HEAD_DIM
VMEM_LIMIT = 56 * 1024 * 1024

ROW_TILE = 512
ATTN_TILE = 512
EXPERT_TILE = 512
POS_TILE = 512
SC_CHUNK = 128
CONV_HALO = 32

NEG_BIG = -1e30
EXP_UNDERFLOW = 104.0
FIXED_SHIFT_BOUND = 40.0


def _dot(a, b):
    return jnp.dot(a, b, preferred_element_type=F32)


def _dot_nt(a, b):
    return lax.dot_general(a, b, (((1,), (1,)), ((), ())), preferred_element_type=F32)


def _split2(x):
    hi = x.astype(BF16)
    lo = (x - hi.astype(F32)).astype(BF16)
    return hi, lo


def _split3(x):
    hi = x.astype(BF16)
    r = x - hi.astype(F32)
    mid = r.astype(BF16)
    lo = (r - mid.astype(F32)).astype(BF16)
    return hi, mid, lo


def _pack_halves(v):
    n = v.shape[1] // 2
    lo = lax.bitcast_convert_type(v[:, :n].astype(BF16).astype(F32), jnp.uint32)
    hi = lax.bitcast_convert_type(v[:, n:].astype(BF16).astype(F32), jnp.uint32)
    return lax.bitcast_convert_type(hi | lax.shift_right_logical(lo, jnp.uint32(16)), jnp.int32)


def _unpack_halves(w):
    u = lax.bitcast_convert_type(w, jnp.uint32)
    lo = lax.bitcast_convert_type(lax.shift_left(u, jnp.uint32(16)), F32)
    hi = lax.bitcast_convert_type(u & jnp.uint32(0xFFFF0000), F32)
    return lo, hi


def _sigmoid(x):
    return 1.0 / (1.0 + jnp.exp(-x))


def _silu(x):
    return x * _sigmoid(x)


def _params(*sem):
    return pltpu.CompilerParams(dimension_semantics=sem, vmem_limit_bytes=VMEM_LIMIT)


def _const_spec(shape):
    n = len(shape)
    return pl.BlockSpec(shape, lambda *_: (0,) * n, pipeline_mode=pl.Buffered(1))


def _ada_kernel(c_ref, w_ref, b_ref, o_ref):
    c = c_ref[...]
    a_hi, a_lo = _split2(_silu(c))
    w_hi, w_lo = _split2(w_ref[...])
    o_ref[...] = _dot(a_hi, w_hi) + _dot(a_hi, w_lo) + _dot(a_lo, w_hi) + b_ref[...]


def _ada(c_pad, w_ada, b_ada):
    d = c_pad.shape[1]
    n = w_ada.shape[1]
    return pl.pallas_call(
        _ada_kernel,
        grid=(n // d,),
        in_specs=[_const_spec(c_pad.shape),
                  pl.BlockSpec((d, d), lambda j: (0, j)),
                  pl.BlockSpec((1, d), lambda j: (0, j))],
        out_specs=pl.BlockSpec((c_pad.shape[0], d), lambda j: (0, j)),
        out_shape=jax.ShapeDtypeStruct((c_pad.shape[0], n), F32),
        compiler_params=_params("arbitrary"),
        name="ada",
    )(c_pad, w_ada, b_ada)


def _lane_pieces(x):
    hi, mid, lo = _split3(x)
    return (hi.astype(F32) + pltpu.roll(mid.astype(F32), N_HEADS, 1)
            + pltpu.roll(lo.astype(F32), 2 * N_HEADS, 1)).astype(BF16)


def _head_tile(p, col0, hd):
    per = HEAD_PAD // HEAD_DIM
    g, part = divmod(hd, per)
    blk = p[:, col0 + g * HEAD_PAD:col0 + (g + 1) * HEAD_PAD]
    if part:
        blk = pltpu.roll(blk, HEAD_PAD - part * HEAD_DIM, 1)
    lane = lax.broadcasted_iota(jnp.int32, (1, HEAD_PAD), 1)
    return jnp.where(lane < HEAD_DIM, blk, 0.0)


def _inproj_kernel(x_ref, mod_ref, g1_ref, wcv_ref, wq_ref, wk_ref, wv_ref, wf_ref, wgc_ref,
                   wga_ref, bf_ref, qg_ref, kg_ref, pq_ref, pk_ref, cq_ref, ck_ref, tri_ref,
                   u_ref, qa_ref, ka_ref, v_ref, sgc_ref, sga_ref, cum_ref, carry_ref):
    @pl.when(pl.program_id(1) == 0)
    def _():
        carry_ref[...] = jnp.zeros_like(carry_ref)

    x = x_ref[0]
    ms = jnp.mean(x * x, axis=-1, keepdims=True)
    h = (x * lax.rsqrt(ms + EPS) * g1_ref[...]) * (1.0 + mod_ref[0, 1:2, :]) + mod_ref[0, 0:1, :]
    hb = h.astype(BF16)

    pc = _dot(hb, wcv_ref[...])
    u_ref[0] = (pc[:, :CONV_DIM] * _sigmoid(pc[:, CONV_DIM:])).astype(BF16)

    z = _dot(hb, wf_ref[...]) + bf_ref[...]
    lf = jnp.minimum(z, 0.0) - jnp.log1p(jnp.exp(-jnp.abs(z)))
    lane = lax.broadcasted_iota(jnp.int32, (1, LANES), 1)
    psum = _dot(tri_ref[...], _lane_pieces(jnp.where(lane < N_HEADS, lf, 0.0)))
    cum = (psum + pltpu.roll(psum, LANES - N_HEADS, 1) + pltpu.roll(psum, LANES - 2 * N_HEADS, 1)
           + carry_ref[...])
    cum = jnp.where(lane < N_HEADS, cum, 0.0)
    carry_ref[...] = cum[ROW_TILE - 1:ROW_TILE, :]
    cum_ref[0] = cum

    pieces = _lane_pieces(cum)
    addq = _dot(pieces, pq_ref[...]) + cq_ref[...]
    addk = _dot(pieces, pk_ref[...]) + ck_ref[...]

    pq = _dot(hb, wq_ref[...])
    pk = _dot(hb, wk_ref[...])
    pv = _dot(hb, wv_ref[...])
    inv_hd = 1.0 / HEAD_DIM
    vone = (lax.broadcasted_iota(jnp.int32, (1, HEAD_PAD), 1) == HEAD_DIM).astype(F32)
    for hd in range(N_HEADS):
        sl = slice(hd * HEAD_PAD, (hd + 1) * HEAD_PAD)
        qb = pq[:, sl]
        qn = qb * lax.rsqrt(jnp.sum(qb * qb, axis=-1, keepdims=True) * inv_hd + EPS) * qg_ref[:, sl]
        qa_ref[0, hd] = (qn + addq[:, sl]).astype(BF16)
        kb = pk[:, sl]
        kn = kb * lax.rsqrt(jnp.sum(kb * kb, axis=-1, keepdims=True) * inv_hd + EPS) * kg_ref[:, sl]
        ka_ref[0, hd] = (kn + addk[:, sl]).astype(BF16)
        v_ref[0, hd] = (_head_tile(pv, 0, hd) + vone).astype(BF16)

    sgc_ref[0] = _sigmoid(_dot(hb, wgc_ref[...])).astype(BF16)
    sga_ref[0] = _sigmoid(_dot(hb, wga_ref[...])).astype(BF16)


def _inproj(x, mod, g1, wcv, wq, wk, wv, wf, wgc, wga, bf, qg, kg, pq, pk, cq, ck, tri):
    b, s, d = x.shape
    tm = ROW_TILE
    row = lambda w: pl.BlockSpec((1, tm, w), lambda bi, i: (bi, i, 0))
    head = pl.BlockSpec((1, N_HEADS, tm, HEAD_PAD), lambda bi, i: (bi, 0, i, 0))
    consts = [g1, wcv, wq, wk, wv, wf, wgc, wga, bf, qg, kg, pq, pk, cq, ck, tri]
    return pl.pallas_call(
        _inproj_kernel,
        grid=(b, s // tm),
        in_specs=[row(d), pl.BlockSpec((1, 6, d), lambda bi, i: (bi, 0, 0))]
                 + [_const_spec(a.shape) for a in consts],
        out_specs=[row(CONV_DIM), head, head, head, row(d), row(d), row(LANES)],
        out_shape=[jax.ShapeDtypeStruct((b, s, CONV_DIM), BF16),
                   jax.ShapeDtypeStruct((b, N_HEADS, s, HEAD_PAD), BF16),
                   jax.ShapeDtypeStruct((b, N_HEADS, s, HEAD_PAD), BF16),
                   jax.ShapeDtypeStruct((b, N_HEADS, s, HEAD_PAD), BF16),
                   jax.ShapeDtypeStruct((b, s, d), BF16),
                   jax.ShapeDtypeStruct((b, s, d), BF16),
                   jax.ShapeDtypeStruct((b, s, LANES), F32)],
        scratch_shapes=[pltpu.VMEM((1, LANES), F32)],
        compiler_params=_params("arbitrary", "arbitrary"),
        name="inproj",
    )(x, mod, *consts)


def _attn_kernel(cs_ref, ce_ref, prm_ref, q_ref, k_ref, v_ref, o_ref, acc_ref, m_ref):
    t = ATTN_TILE
    nb = q_ref.shape[2] // t
    base = (pl.program_id(0) * pl.num_programs(1) + pl.program_id(1)) * nb
    bound = prm_ref[0]
    thresh = -(EXP_UNDERFLOW + 2.0 * bound)
    causal = lax.broadcasted_iota(jnp.int32, (t, t), 0) >= lax.broadcasted_iota(jnp.int32, (t, t), 1)

    def first_block(i):
        c0 = cs_ref[base + i]
        return lax.fori_loop(0, i, lambda j, n: n + (c0 - ce_ref[base + j] < thresh).astype(jnp.int32), 0)

    def scores(q, j, masked):
        k0 = pl.multiple_of(j * t, t)
        s = _dot_nt(q, k_ref[0, 0, pl.ds(k0, t), :])
        if masked:
            s = jnp.where(causal, s, NEG_BIG)
        return s, v_ref[0, 0, pl.ds(k0, t), :]

    def finish(q0):
        acc = acc_ref[...]
        o_ref[0, pl.ds(q0, t), :] = (acc / acc[:, HEAD_DIM:HEAD_DIM + 1]).astype(BF16)

    def fixed_shift(i, carry):
        q0 = pl.multiple_of(i * t, t)
        q = q_ref[0, 0, pl.ds(q0, t), :]

        acc_ref[...] = jnp.zeros_like(acc_ref)

        def weighted(j, masked):
            s, vb = scores(q, j, masked)
            return _dot(jnp.exp(s).astype(BF16), vb)

        def kv(j, c):
            acc_ref[...] += weighted(j, False)
            return c

        lax.fori_loop(first_block(i), i, kv, 0)
        acc_ref[...] += weighted(i, True)
        finish(q0)
        return carry

    def running_max(i, carry):
        q0 = pl.multiple_of(i * t, t)
        q = q_ref[0, 0, pl.ds(q0, t), :]
        m_ref[...] = jnp.full_like(m_ref, -jnp.inf)
        acc_ref[...] = jnp.zeros_like(acc_ref)

        def step(j, masked):
            s, vb = scores(q, j, masked)
            m_prev = m_ref[...]
            m_new = jnp.maximum(m_prev, jnp.max(s, axis=-1, keepdims=True))
            p = jnp.exp(s - m_new)
            acc_ref[...] = jnp.exp(m_prev - m_new) * acc_ref[...] + _dot(p.astype(BF16), vb)
            m_ref[...] = m_new

        def kv(j, c):
            step(j, False)
            return c

        lax.fori_loop(first_block(i), i, kv, 0)
        step(i, True)
        finish(q0)
        return carry

    @pl.when(bound <= FIXED_SHIFT_BOUND)
    def _():
        lax.fori_loop(0, nb, fixed_shift, 0)

    @pl.when(bound > FIXED_SHIFT_BOUND)
    def _():
        lax.fori_loop(0, nb, running_max, 0)


def _attention(cs, ce, prm, qa, ka, v):
    b, nh, s, hp = qa.shape
    t = ATTN_TILE
    seq = pl.BlockSpec((1, 1, s, hp), lambda bi, hi, *_: (bi, hi, 0, 0))
    return pl.pallas_call(
        _attn_kernel,
        grid_spec=pltpu.PrefetchScalarGridSpec(
            num_scalar_prefetch=3,
            grid=(b, nh),
            in_specs=[seq, seq, seq],
            out_specs=pl.BlockSpec((1, s, hp), lambda bi, hi, *_: (bi, 0, hi)),
            scratch_shapes=[pltpu.VMEM((t, hp), F32), pltpu.VMEM((t, 1), F32)]),
        out_shape=jax.ShapeDtypeStruct((b, s, nh * hp), BF16),
        compiler_params=_params("arbitrary", "arbitrary"),
        name="attn",
    )(cs, ce, prm, qa, ka, v)


def _merge_kernel(u_ref, halo_ref, o_ref, sgc_ref, sga_ref, x_ref, mod_ref, wdw_ref, bdw_ref,
                  gng_ref, gnb_ref, gg_ref, wco_ref, wao_ref, wout_ref, x1_ref, buf_ref):
    tm = ROW_TILE
    halo = halo_ref[0].astype(F32)
    halo = jnp.where(pl.program_id(1) == 0, jnp.zeros_like(halo), halo)
    ucur = u_ref[0].astype(F32)
    for cb in range(CONV_DIM // LANES):
        buf_ref[cb, 0:CONV_HALO, :] = halo[:, cb * LANES:(cb + 1) * LANES]
        buf_ref[cb, CONV_HALO:, :] = ucur[:, cb * LANES:(cb + 1) * LANES]

    base = CONV_HALO - (CONV_WIDTH - 1)
    ys = []
    for cb in range(CONV_DIM // LANES):
        acc = jnp.zeros((tm, LANES), F32)
        for j in range(CONV_WIDTH):
            acc = acc + wdw_ref[j:j + 1, cb * LANES:(cb + 1) * LANES] * buf_ref[cb, base + j:base + j + tm, :]
        ys.append(acc)
    y = jnp.concatenate(ys, axis=1) + bdw_ref[...]

    gg = gg_ref[...]
    y_hi, y_lo = _split2(y)
    dlt = y - (_dot(y_hi, gg) + _dot(y_lo, gg))
    s_hi, s_lo = _split2(dlt * dlt)
    var = _dot(s_hi, gg) + _dot(s_lo, gg)
    yn = dlt * lax.rsqrt(var + EPS) * gng_ref[...] + gnb_ref[...]
    y_conv = _dot(_silu(yn).astype(BF16), wco_ref[...])

    per = HEAD_PAD // HEAD_DIM
    lane = lax.broadcasted_iota(jnp.int32, (1, HEAD_PAD), 1)
    packed = []
    for g in range(N_HEADS // per):
        tile = o_ref[0, :, g * per * HEAD_PAD:(g * per + 1) * HEAD_PAD].astype(F32)
        for part in range(1, per):
            nxt = o_ref[0, :, (g * per + part) * HEAD_PAD:(g * per + part + 1) * HEAD_PAD].astype(F32)
            tile = jnp.where(lane < part * HEAD_DIM, tile, pltpu.roll(nxt, part * HEAD_DIM, 1))
        packed.append(tile)
    y_attn = _dot(jnp.concatenate(packed, axis=1).astype(BF16), wao_ref[...])
    merged = sgc_ref[0].astype(F32) * y_conv + sga_ref[0].astype(F32) * y_attn
    mix = _dot(merged.astype(BF16), wout_ref[...])
    x1_ref[0] = x_ref[0] + mod_ref[0, 2:3, :] * mix


def _merge(u, o, sgc, sga, x, mod, wdw, bdw, gng, gnb, gg, wco, wao, wout):
    b, s, d = x.shape
    tm = ROW_TILE
    per = tm // CONV_HALO
    row = lambda w: pl.BlockSpec((1, tm, w), lambda bi, i: (bi, i, 0))
    consts = [wdw, bdw, gng, gnb, gg, wco, wao, wout]
    return pl.pallas_call(
        _merge_kernel,
        grid=(b, s // tm),
        in_specs=[row(CONV_DIM),
                  pl.BlockSpec((1, CONV_HALO, CONV_DIM),
                               lambda bi, i: (bi, jnp.maximum(i * per - 1, 0), 0)),
                  row(o.shape[2]), row(d), row(d), row(d),
                  pl.BlockSpec((1, 6, d), lambda bi, i: (bi, 0, 0))]
                 + [_const_spec(a.shape) for a in consts],
        out_specs=row(d),
        out_shape=jax.ShapeDtypeStruct((b, s, d), F32),
        scratch_shapes=[pltpu.VMEM((CONV_DIM // LANES, CONV_HALO + tm, LANES), F32)],
        compiler_params=_params("arbitrary", "arbitrary"),
        name="merge",
    )(u, u, o, sgc, sga, x, mod, *consts)


def _router_kernel(x1_ref, mod_ref, g2_ref, wr_hi_ref, wr_lo_ref, rb_ref, h2_ref, h2w_ref, comb_ref):
    x = x1_ref[0]
    ms = jnp.mean(x * x, axis=-1, keepdims=True)
    h = (x * lax.rsqrt(ms + EPS) * g2_ref[...]) * (1.0 + mod_ref[0, 4:5, :]) + mod_ref[0, 3:4, :]
    h2_ref[0] = h.astype(BF16)
    h2w_ref[0] = _pack_halves(h)

    h_hi, h_lo = _split2(h)
    logits = _dot_nt(wr_hi_ref[...], h_hi) + _dot_nt(wr_hi_ref[...], h_lo) + _dot_nt(wr_lo_ref[...], h_hi)
    scores = _sigmoid(logits)
    biased = scores + rb_ref[...]

    per = N_EXPERTS // N_GROUPS
    rows = lax.broadcasted_iota(jnp.int32, (per, biased.shape[1]), 0)
    gscore = []
    for g in range(N_GROUPS):
        blk = biased[g * per:(g + 1) * per, :]
        top1 = jnp.max(blk, axis=0, keepdims=True)
        first = jnp.min(jnp.where(blk == top1, rows, per), axis=0, keepdims=True)
        top2 = jnp.max(jnp.where(rows == first, -jnp.inf, blk), axis=0, keepdims=True)
        gscore.append(top1 + top2)

    cand = []
    for g in range(N_GROUPS):
        rank = jnp.zeros_like(gscore[g], dtype=jnp.int32)
        for g2 in range(N_GROUPS):
            if g2 == g:
                continue
            ahead = gscore[g2] > gscore[g]
            if g2 < g:
                ahead = ahead | (gscore[g2] == gscore[g])
            rank = rank + ahead.astype(jnp.int32)
        keep = rank < TOPK_GROUPS
        cand.append(jnp.where(keep, biased[g * per:(g + 1) * per, :], -jnp.inf))
    cand = jnp.concatenate(cand, axis=0)

    eidx = lax.broadcasted_iota(jnp.int32, cand.shape, 0)
    work = cand
    for _ in range(TOP_K):
        top = jnp.max(work, axis=0, keepdims=True)
        first = jnp.min(jnp.where(work == top, eidx, N_EXPERTS), axis=0, keepdims=True)
        work = jnp.where(eidx == first, -jnp.inf, work)
    sel = (work != cand) & (cand > -jnp.inf)
    w = jnp.where(sel, scores, 0.0)
    comb_ref[...] = w / jnp.sum(w, axis=0, keepdims=True) * ROUTED_SCALE


def _router(x1, mod, g2, wr_hi, wr_lo, rb):
    b, s, d = x1.shape
    tm = ROW_TILE
    nt = s // tm
    return pl.pallas_call(
        _router_kernel,
        grid=(b, nt),
        in_specs=[pl.BlockSpec((1, tm, d), lambda bi, i: (bi, i, 0)),
                  pl.BlockSpec((1, 6, d), lambda bi, i: (bi, 0, 0)),
                  _const_spec(g2.shape), _const_spec(wr_hi.shape), _const_spec(wr_lo.shape),
                  _const_spec(rb.shape)],
        out_specs=[pl.BlockSpec((1, tm, d), lambda bi, i: (bi, i, 0)),
                   pl.BlockSpec((1, tm, d // 2), lambda bi, i: (bi, i, 0)),
                   pl.BlockSpec((N_EXPERTS, tm), lambda bi, i: (0, bi * nt + i))],
        out_shape=[jax.ShapeDtypeStruct((b, s, d), BF16),
                   jax.ShapeDtypeStruct((b, s, d // 2), jnp.int32),
                   jax.ShapeDtypeStruct((N_EXPERTS, b * s), F32)],
        compiler_params=_params("arbitrary", "arbitrary"),
        name="router",
    )(x1, mod, g2, wr_hi, wr_lo, rb)


def _pos_kernel(comb_ref, tri_ref, lstrict_ref, posk_ref, wk_ref, gend_ref, cnt_ref, base_ref):
    phase = pl.program_id(0)
    i = pl.program_id(1)
    tp = comb_ref.shape[1]
    comb = comb_ref[...]
    sel = comb != 0.0
    selb = jnp.where(sel, 1.0, 0.0).astype(BF16)
    tile_count = _dot(selb, jnp.ones((tp, tp), BF16))

    @pl.when(phase == 0)
    def _():
        @pl.when(i == 0)
        def _():
            cnt_ref[...] = jnp.zeros_like(cnt_ref)
            gend_ref[...] = jnp.zeros_like(gend_ref)

        cnt_ref[...] += tile_count
        posk_ref[...] = jnp.zeros_like(posk_ref)
        wk_ref[...] = jnp.zeros_like(wk_ref)

    @pl.when(phase == 1)
    def _():
        @pl.when(i == 0)
        def _():
            seg = jnp.floor((cnt_ref[...] + (EXPERT_TILE - 1.0)) * (1.0 / EXPERT_TILE)) * EXPERT_TILE
            s_hi, s_mid, s_lo = _split3(seg)
            ls = lstrict_ref[...]
            start = _dot(ls, s_hi) + _dot(ls, s_mid) + _dot(ls, s_lo)
            base_ref[...] = start
            gend_ref[...] = (start + seg)[:, :LANES]

        rank = _dot(selb, tri_ref[...])
        pos = base_ref[...] + rank - 1.0
        base_ref[...] += tile_count
        slot = _dot(lstrict_ref[...], selb)
        rows_p, rows_w = [], []
        for k in range(TOP_K):
            m = sel & (slot == k)
            rows_p.append(jnp.sum(jnp.where(m, pos, 0.0), axis=0, keepdims=True))
            rows_w.append(jnp.sum(jnp.where(m, comb, 0.0), axis=0, keepdims=True))
        posk_ref[...] = jnp.concatenate(rows_p, axis=0).astype(jnp.int32)
        wk_ref[...] = jnp.concatenate(rows_w, axis=0)


def _positions(comb_t, tri, lstrict):
    ne, t = comb_t.shape
    tp = POS_TILE
    out_tok = pl.BlockSpec((TOP_K, tp), lambda p, i: (0, i * p))
    return pl.pallas_call(
        _pos_kernel,
        grid=(2, t // tp),
        in_specs=[pl.BlockSpec((ne, tp), lambda p, i: (0, i)),
                  _const_spec(tri.shape), _const_spec(lstrict.shape)],
        out_specs=[out_tok, out_tok, pl.BlockSpec((ne, LANES), lambda p, i: (0, 0))],
        out_shape=[jax.ShapeDtypeStruct((TOP_K, t), jnp.int32),
                   jax.ShapeDtypeStruct((TOP_K, t), F32),
                   jax.ShapeDtypeStruct((ne, LANES), F32)],
        scratch_shapes=[pltpu.VMEM((ne, tp), F32), pltpu.VMEM((ne, tp), F32)],
        compiler_params=_params("arbitrary", "arbitrary"),
        name="positions",
    )(comb_t, tri, lstrict)


def _sc_workers():
    info = plsc.get_sparse_core_info()
    return info.num_cores, info.num_cores * info.num_subcores


def _sc_scatter_rows(rows, pos, n_out):
    nc, nw = _sc_workers()
    n, w = rows.shape
    nk = pos.shape[0]
    ch = SC_CHUNK
    per_w = n // nw
    assert per_w * nw == n and per_w % ch == 0

    @functools.partial(
        pl.kernel, mesh=plsc.VectorSubcoreMesh(core_axis_name="c", subcore_axis_name="s"),
        out_type=jax.ShapeDtypeStruct((n_out, w), rows.dtype),
        scratch_types=[pltpu.VMEM((nk, ch), jnp.int32), pltpu.VMEM((ch, w), rows.dtype),
                       pltpu.SemaphoreType.DMA])
    def scatter(rows_hbm, pos_hbm, out_hbm, idx_v, rows_v, sem):
        base = (lax.axis_index("s") * nc + lax.axis_index("c")) * per_w

        @pl.loop(0, per_w // ch)
        def _(ci):
            off = pl.multiple_of(base + ci * ch, ch)
            pltpu.sync_copy(pos_hbm.at[:, pl.ds(off, ch)], idx_v)
            pltpu.sync_copy(rows_hbm.at[pl.ds(off, ch)], rows_v)
            copies = [pltpu.make_async_copy(rows_v, out_hbm.at[idx_v.at[k]], sem) for k in range(nk)]
            for cp in copies:
                cp.start()
            for cp in copies:
                cp.wait()

    return scatter(rows, pos)


def _sc_gather_rows(table, idx):
    nc, nw = _sc_workers()
    n = idx.shape[0]
    w = table.shape[1]
    ch = SC_CHUNK
    per_w = n // nw
    assert per_w * nw == n and per_w % ch == 0

    @functools.partial(
        pl.kernel, mesh=plsc.VectorSubcoreMesh(core_axis_name="c", subcore_axis_name="s"),
        out_type=jax.ShapeDtypeStruct((n, w), table.dtype),
        scratch_types=[pltpu.VMEM((ch,), jnp.int32), pltpu.VMEM((ch, w), table.dtype),
                       pltpu.SemaphoreType.DMA])
    def gather(table_hbm, idx_hbm, out_hbm, idx_v, rows_v, sem):
        base = (lax.axis_index("s") * nc + lax.axis_index("c")) * per_w

        @pl.loop(0, per_w // ch)
        def _(ci):
            off = pl.multiple_of(base + ci * ch, ch)
            pltpu.sync_copy(idx_hbm.at[pl.ds(off, ch)], idx_v)
            cp = pltpu.make_async_copy(table_hbm.at[idx_v], rows_v, sem)
            cp.start()
            cp.wait()
            pltpu.sync_copy(rows_v, out_hbm.at[pl.ds(off, ch)])

    return gather(table, idx)


def _expert_kernel(te_ref, nu_ref, xs_ref, wgu_ref, wd_ref, y_ref, wgu_b, wd_b):
    i = pl.program_id(0)

    @pl.when(i < nu_ref[0])
    def _():
        @pl.when((i == 0) | (te_ref[i] != te_ref[jnp.maximum(i - 1, 0)]))
        def _():
            wgu_b[...] = wgu_ref[0].astype(BF16)
            wd_b[...] = wd_ref[0].astype(BF16)

        lo, hi = _unpack_halves(xs_ref[...])
        half = wgu_b.shape[0] // 2
        gu = _dot(lo.astype(BF16), wgu_b[:half, :]) + _dot(hi.astype(BF16), wgu_b[half:, :])
        act = _silu(gu[:, :EXPERT_DIM]) * gu[:, EXPERT_DIM:]
        y_ref[...] = _pack_halves(_dot(act.astype(BF16), wd_b[...]))


def _experts(tile_e, n_used, xs, wgu, wd):
    r, w = xs.shape
    tm = EXPERT_TILE
    d = wgu.shape[1]
    rows = pl.BlockSpec((tm, w), lambda i, te, nu: (jnp.minimum(i, nu[0] - 1), 0))
    return pl.pallas_call(
        _expert_kernel,
        grid_spec=pltpu.PrefetchScalarGridSpec(
            num_scalar_prefetch=2,
            grid=(r // tm,),
            in_specs=[rows,
                      pl.BlockSpec((1, d, 2 * EXPERT_DIM), lambda i, te, nu: (te[i], 0, 0)),
                      pl.BlockSpec((1, EXPERT_DIM, d), lambda i, te, nu: (te[i], 0, 0))],
            out_specs=rows,
            scratch_shapes=[pltpu.VMEM((d, 2 * EXPERT_DIM), BF16), pltpu.VMEM((EXPERT_DIM, d), BF16)]),
        out_shape=jax.ShapeDtypeStruct((r, w), jnp.int32),
        compiler_params=_params("arbitrary"),
        name="experts",
    )(tile_e, n_used, xs, wgu, wd)


def _final_kernel(yk_ref, wk_ref, h_ref, x1_ref, mod_ref, wsgu_ref, wsd_ref, out_ref):
    half = yk_ref.shape[2]
    acc_lo = jnp.zeros((yk_ref.shape[1], half), F32)
    acc_hi = jnp.zeros((yk_ref.shape[1], half), F32)
    for k in range(TOP_K):
        lo, hi = _unpack_halves(yk_ref[k])
        w = wk_ref[:, k:k + 1]
        acc_lo = acc_lo + jnp.where(w != 0.0, w * lo, 0.0)
        acc_hi = acc_hi + jnp.where(w != 0.0, w * hi, 0.0)
    routed = jnp.concatenate([acc_lo, acc_hi], axis=1)
    sgu = _dot(h_ref[...], wsgu_ref[...])
    act = _silu(sgu[:, :SHARED_DIM]) * sgu[:, SHARED_DIM:]
    shared = _dot(act.astype(BF16), wsd_ref[...])
    out_ref[...] = x1_ref[...] + mod_ref[0, 5:6, :] * (routed + shared)


def _final(yk, wk_t, h2, x1, mod, wsgu, wsd, tiles_per_batch):
    t, d = h2.shape
    tm = ROW_TILE
    row = lambda w: pl.BlockSpec((tm, w), lambda i: (i, 0))
    return pl.pallas_call(
        _final_kernel,
        grid=(t // tm,),
        in_specs=[pl.BlockSpec((TOP_K, tm, d // 2), lambda i: (0, i, 0)),
                  row(TOP_K), row(d), row(d),
                  pl.BlockSpec((1, 6, d), lambda i: (i // tiles_per_batch, 0, 0)),
                  _const_spec(wsgu.shape), _const_spec(wsd.shape)],
        out_specs=row(d),
        out_shape=jax.ShapeDtypeStruct((t, d), F32),
        compiler_params=_params("arbitrary"),
        name="final",
    )(yk, wk_t, h2, x1, mod, wsgu, wsd)


def _pad_heads(w):
    d = w.shape[0]
    w = w.reshape(d, N_HEADS, HEAD_DIM)
    return jnp.pad(w, ((0, 0), (0, 0), (0, HEAD_PAD - HEAD_DIM))).reshape(d, N_HEADS * HEAD_PAD)


def _placement():
    pq = np.zeros((LANES, N_HEADS * HEAD_PAD), np.float32)
    pk = np.zeros((LANES, N_HEADS * HEAD_PAD), np.float32)
    cq = np.zeros((1, N_HEADS * HEAD_PAD), np.float32)
    ck = np.zeros((1, N_HEADS * HEAD_PAD), np.float32)
    for hd in range(N_HEADS):
        for k in range(3):
            pq[k * N_HEADS + hd, hd * HEAD_PAD + AUG0 + k] = 1.0
            ck[0, hd * HEAD_PAD + AUG0 + k] = 1.0
            pk[k * N_HEADS + hd, hd * HEAD_PAD + AUG0 + 3 + k] = -1.0
            cq[0, hd * HEAD_PAD + AUG0 + 3 + k] = 1.0
    return (jnp.asarray(pq, BF16), jnp.asarray(pk, BF16), jnp.asarray(cq), jnp.asarray(ck))


def kernel(x, c, w_ada, b_ada, norm1_g, w_in, w_dw, b_dw, conv_gn_g, conv_gn_b, w_conv_out,
           q_norm_g, k_norm_g, b_forget, w_attn_out, w_out, norm2_g, w_router, router_bias,
           w_experts_gate_up, w_experts_down, w_shared_gate_up, w_shared_down):
    depth = w_ada.shape[0]
    b, s, d = x.shape
    off_q = 2 * CONV_DIM
    off_f = off_q + 3 * ATTN_DIM
    off_gc = off_f + N_HEADS
    off_ga = off_gc + d

    pq, pk, cq, ck = _placement()
    tri = jnp.asarray(np.tril(np.ones((ROW_TILE, ROW_TILE), np.float32)), BF16)
    grp = np.arange(CONV_DIM) // (CONV_DIM // CONV_GROUPS)
    gg = jnp.asarray((grp[:, None] == grp[None, :]).astype(np.float32) / (CONV_DIM // CONV_GROUPS), BF16)
    c_pad = jnp.pad(c, ((0, SUBLANES - b), (0, 0)))
    tri_pos = jnp.asarray(np.triu(np.ones((POS_TILE, POS_TILE), np.float32)), BF16)
    lstrict = jnp.asarray(np.tril(np.ones((N_EXPERTS, N_EXPERTS), np.float32), -1), BF16)

    for l in range(depth):
        mod = _ada(c_pad, w_ada[l], b_ada[l][None, :])[:b].reshape(b, 6, d)

        wi = w_in[l]
        bf = jnp.pad(b_forget[l][None, :], ((0, 0), (0, LANES - N_HEADS)))
        wf = jnp.pad(wi[:, off_f:off_gc], ((0, 0), (0, LANES - N_HEADS))).astype(BF16)
        gpad = lambda g, sc: jnp.tile(jnp.pad(g * sc, (0, HEAD_PAD - HEAD_DIM)), N_HEADS)[None, :]
        qscale = HEAD_DIM ** -0.5
        u, qa, ka, v, sgc, sga, cum = _inproj(
            x, mod, norm1_g[l][None, :],
            wi[:, :off_q].astype(BF16),
            _pad_heads(wi[:, off_q:off_q + ATTN_DIM]).astype(BF16),
            _pad_heads(wi[:, off_q + ATTN_DIM:off_q + 2 * ATTN_DIM]).astype(BF16),
            wi[:, off_q + 2 * ATTN_DIM:off_f].astype(BF16),
            wf, wi[:, off_gc:off_ga].astype(BF16), wi[:, off_ga:].astype(BF16),
            bf, gpad(q_norm_g[l], qscale), gpad(k_norm_g[l], 1.0),
            pq, pk, cq, ck, tri)

        flat = lambda a: a[:, :, :N_HEADS].transpose(0, 2, 1).reshape(-1)
        cs = flat(cum[:, 0::ATTN_TILE])
        ce = flat(cum[:, ATTN_TILE - 1::ATTN_TILE])
        bound = (1.02 * HEAD_DIM * qscale) * jnp.max(jnp.abs(q_norm_g[l])) * jnp.max(jnp.abs(k_norm_g[l]))
        o = _attention(cs, ce, bound.reshape(1), qa, ka, v)

        wdw = jnp.pad(w_dw[l], ((0, CONV_HALO - CONV_WIDTH), (0, 0)))
        x1 = _merge(u, o, sgc, sga, x, mod, wdw, b_dw[l][None, :], conv_gn_g[l][None, :],
                    conv_gn_b[l][None, :], gg, w_conv_out[l].astype(BF16), w_attn_out[l].astype(BF16),
                    w_out[l].astype(BF16))

        wr = w_router[l].T
        wr_hi = wr.astype(BF16)
        wr_lo = (wr - wr_hi.astype(F32)).astype(BF16)
        h2, h2w, comb_t = _router(x1, mod, norm2_g[l][None, :], wr_hi, wr_lo, router_bias[l][:, None])

        t = b * s
        posk, wk, gend = _positions(comb_t, tri_pos, lstrict)
        n_tiles = (t * TOP_K) // EXPERT_TILE + N_EXPERTS
        seg_end = gend[:, 0].astype(jnp.int32)
        n_used = seg_end[-1:] // EXPERT_TILE
        tile_start = jnp.arange(n_tiles, dtype=jnp.int32) * EXPERT_TILE
        tile_start = jnp.minimum(tile_start, seg_end[-1] - EXPERT_TILE)
        tile_e = jnp.sum((seg_end[None, :] <= tile_start[:, None]).astype(jnp.int32), axis=1)

        xs = _sc_scatter_rows(h2w.reshape(t, d // 2), posk, n_tiles * EXPERT_TILE)
        ys = _experts(tile_e, n_used, xs, w_experts_gate_up[l], w_experts_down[l])
        yk = _sc_gather_rows(ys, posk.reshape(-1)).reshape(TOP_K, t, d // 2)
        out = _final(yk, wk.T, h2.reshape(t, d), x1.reshape(t, d), mod,
                     w_shared_gate_up[l].astype(BF16), w_shared_down[l].astype(BF16), s // ROW_TILE)
        x = out.reshape(b, s, d)
    return x
```

```python
import functools

import numpy as np
import jax
import jax.numpy as jnp
from jax import lax
from jax.experimental import pallas as pl
from jax.experimental.pallas import tpu as pltpu
from jax.experimental.pallas import tpu_sc as plsc

F32 = jnp.float32
BF16 = jnp.bfloat16

CONV_DIM = 512
CONV_WIDTH = 31
CONV_GROUPS = 8
N_HEADS = 8
HEAD_DIM = 64
ATTN_DIM = N_HEADS * HEAD_DIM
N_EXPERTS = 64
TOP_K = 8
N_GROUPS = 8
TOPK_GROUPS = 4
EXPERT_DIM = 256
SHARED_DIM = 256
ROUTED_SCALE = 2.5
EPS = 1e-6

LANES = 128
SUBLANES = 8
HEAD_PAD = LANES
AUG0 = HEAD_DIM
VMEM_LIMIT = 56 * 1024 * 1024

ROW_TILE = 512
ATTN_TILE = 512
EXPERT_TILE = 512
POS_TILE = 1024
SC_CHUNK = 128
CONV_HALO = 32

NEG_BIG = -1e30
EXP_UNDERFLOW = 104.0
FIXED_SHIFT_BOUND = 40.0


def _dot(a, b):
    return jnp.dot(a, b, preferred_element_type=F32)


def _dot_nt(a, b):
    return lax.dot_general(a, b, (((1,), (1,)), ((), ())), preferred_element_type=F32)


def _split2(x):
    hi = x.astype(BF16)
    lo = (x - hi.astype(F32)).astype(BF16)
    return hi, lo


def _split3(x):
    hi = x.astype(BF16)
    r = x - hi.astype(F32)
    mid = r.astype(BF16)
    lo = (r - mid.astype(F32)).astype(BF16)
    return hi, mid, lo


def _pack_halves(v):
    n = v.shape[1] // 2
    lo = lax.bitcast_convert_type(v[:, :n].astype(BF16).astype(F32), jnp.uint32)
    hi = lax.bitcast_convert_type(v[:, n:].astype(BF16).astype(F32), jnp.uint32)
    return lax.bitcast_convert_type(hi | lax.shift_right_logical(lo, jnp.uint32(16)), jnp.int32)


def _unpack_halves(w):
    u = lax.bitcast_convert_type(w, jnp.uint32)
    lo = lax.bitcast_convert_type(lax.shift_left(u, jnp.uint32(16)), F32)
    hi = lax.bitcast_convert_type(u & jnp.uint32(0xFFFF0000), F32)
    return lo, hi


def _sigmoid(x):
    return 1.0 / (1.0 + jnp.exp(-x))


def _silu(x):
    return x * _sigmoid(x)


def _params(*sem):
    return pltpu.CompilerParams(dimension_semantics=sem, vmem_limit_bytes=VMEM_LIMIT)


def _const_spec(shape):
    n = len(shape)
    return pl.BlockSpec(shape, lambda *_: (0,) * n, pipeline_mode=pl.Buffered(1))


def _ada_kernel(c_ref, w_ref, b_ref, o_ref):
    c = c_ref[...]
    a_hi, a_lo = _split2(_silu(c))
    w_hi, w_lo = _split2(w_ref[...])
    o_ref[...] = _dot(a_hi, w_hi) + _dot(a_hi, w_lo) + _dot(a_lo, w_hi) + b_ref[...]


def _ada(c_pad, w_ada, b_ada):
    d = c_pad.shape[1]
    n = w_ada.shape[1]
    return pl.pallas_call(
        _ada_kernel,
        grid=(n // d,),
        in_specs=[_const_spec(c_pad.shape),
                  pl.BlockSpec((d, d), lambda j: (0, j)),
                  pl.BlockSpec((1, d), lambda j: (0, j))],
        out_specs=pl.BlockSpec((c_pad.shape[0], d), lambda j: (0, j)),
        out_shape=jax.ShapeDtypeStruct((c_pad.shape[0], n), F32),
        compiler_params=_params("arbitrary"),
        name="ada",
    )(c_pad, w_ada, b_ada)


def _lane_pieces(x):
    hi, mid, lo = _split3(x)
    return (hi.astype(F32) + pltpu.roll(mid.astype(F32), N_HEADS, 1)
            + pltpu.roll(lo.astype(F32), 2 * N_HEADS, 1)).astype(BF16)


def _head_tile(p, col0, hd):
    per = HEAD_PAD // HEAD_DIM
    g, part = divmod(hd, per)
    blk = p[:, col0 + g * HEAD_PAD:col0 + (g + 1) * HEAD_PAD]
    if part:
        blk = pltpu.roll(blk, HEAD_PAD - part * HEAD_DIM, 1)
    lane = lax.broadcasted_iota(jnp.int32, (1, HEAD_PAD), 1)
    return jnp.where(lane < HEAD_DIM, blk, 0.0)


def _inproj_kernel(x_ref, mod_ref, g1_ref, wcv_ref, wq_ref, wk_ref, wv_ref, wf_ref, wgc_ref,
                   wga_ref, bf_ref, qg_ref, kg_ref, pq_ref, pk_ref, cq_ref, ck_ref, tri_ref,
                   u_ref, qa_ref, ka_ref, v_ref, sgc_ref, sga_ref, cum_ref, carry_ref):
    @pl.when(pl.program_id(1) == 0)
    def _():
        carry_ref[...] = jnp.zeros_like(carry_ref)

    x = x_ref[0]
    ms = jnp.mean(x * x, axis=-1, keepdims=True)
    h = (x * lax.rsqrt(ms + EPS) * g1_ref[...]) * (1.0 + mod_ref[0, 1:2, :]) + mod_ref[0, 0:1, :]
    hb = h.astype(BF16)

    pc = _dot(hb, wcv_ref[...])
    u_ref[0] = (pc[:, :CONV_DIM] * _sigmoid(pc[:, CONV_DIM:])).astype(BF16)

    z = _dot(hb, wf_ref[...]) + bf_ref[...]
    lf = jnp.minimum(z, 0.0) - jnp.log1p(jnp.exp(-jnp.abs(z)))
    lane = lax.broadcasted_iota(jnp.int32, (1, LANES), 1)
    psum = _dot(tri_ref[...], _lane_pieces(jnp.where(lane < N_HEADS, lf, 0.0)))
    cum = (psum + pltpu.roll(psum, LANES - N_HEADS, 1) + pltpu.roll(psum, LANES - 2 * N_HEADS, 1)
           + carry_ref[...])
    cum = jnp.where(lane < N_HEADS, cum, 0.0)
    carry_ref[...] = cum[ROW_TILE - 1:ROW_TILE, :]
    cum_ref[0] = cum

    pieces = _lane_pieces(cum)
    addq = _dot(pieces, pq_ref[...]) + cq_ref[...]
    addk = _dot(pieces, pk_ref[...]) + ck_ref[...]

    pq = _dot(hb, wq_ref[...])
    pk = _dot(hb, wk_ref[...])
    pv = _dot(hb, wv_ref[...])
    inv_hd = 1.0 / HEAD_DIM
    vone = (lax.broadcasted_iota(jnp.int32, (1, HEAD_PAD), 1) == HEAD_DIM).astype(F32)
    for hd in range(N_HEADS):
        sl = slice(hd * HEAD_PAD, (hd + 1) * HEAD_PAD)
        qb = pq[:, sl]
        qn = qb * lax.rsqrt(jnp.sum(qb * qb, axis=-1, keepdims=True) * inv_hd + EPS) * qg_ref[:, sl]
        qa_ref[0, hd] = (qn + addq[:, sl]).astype(BF16)
        kb = pk[:, sl]
        kn = kb * lax.rsqrt(jnp.sum(kb * kb, axis=-1, keepdims=True) * inv_hd + EPS) * kg_ref[:, sl]
        ka_ref[0, hd] = (kn + addk[:, sl]).astype(BF16)
        v_ref[0, hd] = (_head_tile(pv, 0, hd) + vone).astype(BF16)

    sgc_ref[0] = _sigmoid(_dot(hb, wgc_ref[...])).astype(BF16)
    sga_ref[0] = _sigmoid(_dot(hb, wga_ref[...])).astype(BF16)


def _inproj(x, mod, g1, wcv, wq, wk, wv, wf, wgc, wga, bf, qg, kg, pq, pk, cq, ck, tri):
    b, s, d = x.shape
    tm = ROW_TILE
    row = lambda w: pl.BlockSpec((1, tm, w), lambda bi, i: (bi, i, 0))
    head = pl.BlockSpec((1, N_HEADS, tm, HEAD_PAD), lambda bi, i: (bi, 0, i, 0))
    consts = [g1, wcv, wq, wk, wv, wf, wgc, wga, bf, qg, kg, pq, pk, cq, ck, tri]
    return pl.pallas_call(
        _inproj_kernel,
        grid=(b, s // tm),
        in_specs=[row(d), pl.BlockSpec((1, 6, d), lambda bi, i: (bi, 0, 0))]
                 + [_const_spec(a.shape) for a in consts],
        out_specs=[row(CONV_DIM), head, head, head, row(d), row(d), row(LANES)],
        out_shape=[jax.ShapeDtypeStruct((b, s, CONV_DIM), BF16),
                   jax.ShapeDtypeStruct((b, N_HEADS, s, HEAD_PAD), BF16),
                   jax.ShapeDtypeStruct((b, N_HEADS, s, HEAD_PAD), BF16),
                   jax.ShapeDtypeStruct((b, N_HEADS, s, HEAD_PAD), BF16),
                   jax.ShapeDtypeStruct((b, s, d), BF16),
                   jax.ShapeDtypeStruct((b, s, d), BF16),
                   jax.ShapeDtypeStruct((b, s, LANES), F32)],
        scratch_shapes=[pltpu.VMEM((1, LANES), F32)],
        compiler_params=_params("arbitrary", "arbitrary"),
        name="inproj",
    )(x, mod, *consts)


def _attn_kernel(cs_ref, ce_ref, prm_ref, q_ref, k_ref, v_ref, o_ref, acc_ref, m_ref):
    t = ATTN_TILE
    nb = q_ref.shape[2] // t
    base = (pl.program_id(0) * pl.num_programs(1) + pl.program_id(1)) * nb
    bound = prm_ref[0]
    thresh = -(EXP_UNDERFLOW + 2.0 * bound)
    causal = lax.broadcasted_iota(jnp.int32, (t, t), 0) >= lax.broadcasted_iota(jnp.int32, (t, t), 1)

    def first_block(i):
        c0 = cs_ref[base + i]
        return lax.fori_loop(0, i, lambda j, n: n + (c0 - ce_ref[base + j] < thresh).astype(jnp.int32), 0)

    def scores(q, j, masked):
        k0 = pl.multiple_of(j * t, t)
        s = _dot_nt(q, k_ref[0, 0, pl.ds(k0, t), :])
        if masked:
            s = jnp.where(causal, s, NEG_BIG)
        return s, v_ref[0, 0, pl.ds(k0, t), :]

    def finish(q0):
        acc = acc_ref[...]
        o_ref[0, pl.ds(q0, t), :] = (acc / acc[:, HEAD_DIM:HEAD_DIM + 1]).astype(BF16)

    def fixed_shift(i, carry):
        q0 = pl.multiple_of(i * t, t)
        q = q_ref[0, 0, pl.ds(q0, t), :]

        acc_ref[...] = jnp.zeros_like(acc_ref)

        def weighted(j, masked):
            s, vb = scores(q, j, masked)
            return _dot(jnp.exp(s).astype(BF16), vb)

        def kv(j, c):
            acc_ref[...] += weighted(j, False)
            return c

        lax.fori_loop(first_block(i), i, kv, 0)
        acc_ref[...] += weighted(i, True)
        finish(q0)
        return carry

    def running_max(i, carry):
        q0 = pl.multiple_of(i * t, t)
        q = q_ref[0, 0, pl.ds(q0, t), :]
        m_ref[...] = jnp.full_like(m_ref, -jnp.inf)
        acc_ref[...] = jnp.zeros_like(acc_ref)

        def step(j, masked):
            s, vb = scores(q, j, masked)
            m_prev = m_ref[...]
            m_new = jnp.maximum(m_prev, jnp.max(s, axis=-1, keepdims=True))
            p = jnp.exp(s - m_new)
            acc_ref[...] = jnp.exp(m_prev - m_new) * acc_ref[...] + _dot(p.astype(BF16), vb)
            m_ref[...] = m_new

        def kv(j, c):
            step(j, False)
            return c

        lax.fori_loop(first_block(i), i, kv, 0)
        step(i, True)
        finish(q0)
        return carry

    @pl.when(bound <= FIXED_SHIFT_BOUND)
    def _():
        lax.fori_loop(0, nb, fixed_shift, 0)

    @pl.when(bound > FIXED_SHIFT_BOUND)
    def _():
        lax.fori_loop(0, nb, running_max, 0)


def _attention(cs, ce, prm, qa, ka, v):
    b, nh, s, hp = qa.shape
    t = ATTN_TILE
    seq = pl.BlockSpec((1, 1, s, hp), lambda bi, hi, *_: (bi, hi, 0, 0))
    return pl.pallas_call(
        _attn_kernel,
        grid_spec=pltpu.PrefetchScalarGridSpec(
            num_scalar_prefetch=3,
            grid=(b, nh),
            in_specs=[seq, seq, seq],
            out_specs=pl.BlockSpec((1, s, hp), lambda bi, hi, *_: (bi, 0, hi)),
            scratch_shapes=[pltpu.VMEM((t, hp), F32), pltpu.VMEM((t, 1), F32)]),
        out_shape=jax.ShapeDtypeStruct((b, s, nh * hp), BF16),
        compiler_params=_params("arbitrary", "arbitrary"),
        name="attn",
    )(cs, ce, prm, qa, ka, v)


def _merge_kernel(u_ref, halo_ref, o_ref, sgc_ref, sga_ref, x_ref, mod_ref, wdw_ref, bdw_ref,
                  gng_ref, gnb_ref, gg_ref, wco_ref, wao_ref, wout_ref, x1_ref, buf_ref):
    tm = ROW_TILE
    halo = halo_ref[0].astype(F32)
    halo = jnp.where(pl.program_id(1) == 0, jnp.zeros_like(halo), halo)
    ucur = u_ref[0].astype(F32)
    for cb in range(CONV_DIM // LANES):
        buf_ref[cb, 0:CONV_HALO, :] = halo[:, cb * LANES:(cb + 1) * LANES]
        buf_ref[cb, CONV_HALO:, :] = ucur[:, cb * LANES:(cb + 1) * LANES]

    base = CONV_HALO - (CONV_WIDTH - 1)
    ys = []
    for cb in range(CONV_DIM // LANES):
        acc = jnp.zeros((tm, LANES), F32)
        for j in range(CONV_WIDTH):
            acc = acc + wdw_ref[j:j + 1, cb * LANES:(cb + 1) * LANES] * buf_ref[cb, base + j:base + j + tm, :]
        ys.append(acc)
    y = jnp.concatenate(ys, axis=1) + bdw_ref[...]

    gg = gg_ref[...]
    y_hi, y_lo = _split2(y)
    dlt = y - (_dot(y_hi, gg) + _dot(y_lo, gg))
    s_hi, s_lo = _split2(dlt * dlt)
    var = _dot(s_hi, gg) + _dot(s_lo, gg)
    yn = dlt * lax.rsqrt(var + EPS) * gng_ref[...] + gnb_ref[...]
    y_conv = _dot(_silu(yn).astype(BF16), wco_ref[...])

    per = HEAD_PAD // HEAD_DIM
    lane = lax.broadcasted_iota(jnp.int32, (1, HEAD_PAD), 1)
    packed = []
    for g in range(N_HEADS // per):
        tile = o_ref[0, :, g * per * HEAD_PAD:(g * per + 1) * HEAD_PAD].astype(F32)
        for part in range(1, per):
            nxt = o_ref[0, :, (g * per + part) * HEAD_PAD:(g * per + part + 1) * HEAD_PAD].astype(F32)
            tile = jnp.where(lane < part * HEAD_DIM, tile, pltpu.roll(nxt, part * HEAD_DIM, 1))
        packed.append(tile)
    y_attn = _dot(jnp.concatenate(packed, axis=1).astype(BF16), wao_ref[...])
    merged = sgc_ref[0].astype(F32) * y_conv + sga_ref[0].astype(F32) * y_attn
    mix = _dot(merged.astype(BF16), wout_ref[...])
    x1_ref[0] = x_ref[0] + mod_ref[0, 2:3, :] * mix


def _merge(u, o, sgc, sga, x, mod, wdw, bdw, gng, gnb, gg, wco, wao, wout):
    b, s, d = x.shape
    tm = ROW_TILE
    per = tm // CONV_HALO
    row = lambda w: pl.BlockSpec((1, tm, w), lambda bi, i: (bi, i, 0))
    consts = [wdw, bdw, gng, gnb, gg, wco, wao, wout]
    return pl.pallas_call(
        _merge_kernel,
        grid=(b, s // tm),
        in_specs=[row(CONV_DIM),
                  pl.BlockSpec((1, CONV_HALO, CONV_DIM),
                               lambda bi, i: (bi, jnp.maximum(i * per - 1, 0), 0)),
                  row(o.shape[2]), row(d), row(d), row(d),
                  pl.BlockSpec((1, 6, d), lambda bi, i: (bi, 0, 0))]
                 + [_const_spec(a.shape) for a in consts],
        out_specs=row(d),
        out_shape=jax.ShapeDtypeStruct((b, s, d), F32),
        scratch_shapes=[pltpu.VMEM((CONV_DIM // LANES, CONV_HALO + tm, LANES), F32)],
        compiler_params=_params("arbitrary", "arbitrary"),
        name="merge",
    )(u, u, o, sgc, sga, x, mod, *consts)


def _router_kernel(x1_ref, mod_ref, g2_ref, wr_hi_ref, wr_lo_ref, rb_ref, h2_ref, h2w_ref, comb_ref, cnt_ref):
    x = x1_ref[0]
    ms = jnp.mean(x * x, axis=-1, keepdims=True)
    h = (x * lax.rsqrt(ms + EPS) * g2_ref[...]) * (1.0 + mod_ref[0, 4:5, :]) + mod_ref[0, 3:4, :]
    h2_ref[0] = h.astype(BF16)
    h2w_ref[0] = _pack_halves(h)

    h_hi, h_lo = _split2(h)
    logits = _dot_nt(wr_hi_ref[...], h_hi) + _dot_nt(wr_hi_ref[...], h_lo) + _dot_nt(wr_lo_ref[...], h_hi)
    scores = _sigmoid(logits)
    biased = scores + rb_ref[...]

    per = N_EXPERTS // N_GROUPS
    rows = lax.broadcasted_iota(jnp.int32, (per, biased.shape[1]), 0)
    gscore = []
    for g in range(N_GROUPS):
        blk = biased[g * per:(g + 1) * per, :]
        top1 = jnp.max(blk, axis=0, keepdims=True)
        first = jnp.min(jnp.where(blk == top1, rows, per), axis=0, keepdims=True)
        top2 = jnp.max(jnp.where(rows == first, -jnp.inf, blk), axis=0, keepdims=True)
        gscore.append(top1 + top2)

    cand = []
    for g in range(N_GROUPS):
        rank = jnp.zeros_like(gscore[g], dtype=jnp.int32)
        for g2 in range(N_GROUPS):
            if g2 == g:
                continue
            ahead = gscore[g2] > gscore[g]
            if g2 < g:
                ahead = ahead | (gscore[g2] == gscore[g])
            rank = rank + ahead.astype(jnp.int32)
        keep = rank < TOPK_GROUPS
        cand.append(jnp.where(keep, biased[g * per:(g + 1) * per, :], -jnp.inf))
    cand = jnp.concatenate(cand, axis=0)

    eidx = lax.broadcasted_iota(jnp.int32, cand.shape, 0)
    work = cand
    for _ in range(TOP_K):
        top = jnp.max(work, axis=0, keepdims=True)
        first = jnp.min(jnp.where(work == top, eidx, N_EXPERTS), axis=0, keepdims=True)
        work = jnp.where(eidx == first, -jnp.inf, work)
    sel = (work != cand) & (cand > -jnp.inf)
    w = jnp.where(sel, scores, 0.0)
    comb = w / jnp.sum(w, axis=0, keepdims=True) * ROUTED_SCALE
    comb_ref[...] = comb

    @pl.when((pl.program_id(0) == 0) & (pl.program_id(1) == 0))
    def _():
        cnt_ref[...] = jnp.zeros_like(cnt_ref)

    hit = jnp.where(comb != 0.0, 1.0, 0.0).astype(BF16)
    cnt_ref[...] += _dot(hit, jnp.ones((hit.shape[1], LANES), BF16))


def _router(x1, mod, g2, wr_hi, wr_lo, rb):
    b, s, d = x1.shape
    tm = ROW_TILE
    nt = s // tm
    return pl.pallas_call(
        _router_kernel,
        grid=(b, nt),
        in_specs=[pl.BlockSpec((1, tm, d), lambda bi, i: (bi, i, 0)),
                  pl.BlockSpec((1, 6, d), lambda bi, i: (bi, 0, 0)),
                  _const_spec(g2.shape), _const_spec(wr_hi.shape), _const_spec(wr_lo.shape),
                  _const_spec(rb.shape)],
        out_specs=[pl.BlockSpec((1, tm, d), lambda bi, i: (bi, i, 0)),
                   pl.BlockSpec((1, tm, d // 2), lambda bi, i: (bi, i, 0)),
                   pl.BlockSpec((N_EXPERTS, tm), lambda bi, i: (0, bi * nt + i)),
                   pl.BlockSpec((N_EXPERTS, LANES), lambda bi, i: (0, 0))],
        out_shape=[jax.ShapeDtypeStruct((b, s, d), BF16),
                   jax.ShapeDtypeStruct((b, s, d // 2), jnp.int32),
                   jax.ShapeDtypeStruct((N_EXPERTS, b * s), F32),
                   jax.ShapeDtypeStruct((N_EXPERTS, LANES), F32)],
        compiler_params=_params("arbitrary", "arbitrary"),
        name="router",
    )(x1, mod, g2, wr_hi, wr_lo, rb)


def _pos_kernel(comb_ref, cnt_ref, tri_ref, lstrict_ref, posk_ref, wk_ref, gend_ref, base_ref):
    tp = comb_ref.shape[1]
    comb = comb_ref[...]
    sel = comb != 0.0
    selb = jnp.where(sel, 1.0, 0.0).astype(BF16)

    @pl.when(pl.program_id(0) == 0)
    def _():
        seg = jnp.floor((cnt_ref[...] + (EXPERT_TILE - 1.0)) * (1.0 / EXPERT_TILE)) * EXPERT_TILE
        s_hi, s_mid, s_lo = _split3(seg)
        ls = lstrict_ref[...]
        start = _dot(ls, s_hi) + _dot(ls, s_mid) + _dot(ls, s_lo)
        base_ref[...] = start
        gend_ref[...] = start + seg

    rank = _dot(selb, tri_ref[...])
    pos = base_ref[:, 0:1] + rank - 1.0
    base_ref[...] += _dot(selb, jnp.ones((tp, LANES), BF16))
    slot = _dot(lstrict_ref[...], selb)
    rows_p, rows_w = [], []
    for k in range(TOP_K):
        m = sel & (slot == k)
        rows_p.append(jnp.sum(jnp.where(m, pos, 0.0), axis=0, keepdims=True))
        rows_w.append(jnp.sum(jnp.where(m, comb, 0.0), axis=0, keepdims=True))
    posk_ref[...] = jnp.concatenate(rows_p, axis=0).astype(jnp.int32)
    wk_ref[...] = jnp.concatenate(rows_w, axis=0)


def _positions(comb_t, cnt, tri, lstrict):
    ne, t = comb_t.shape
    tp = POS_TILE
    tok = lambda rows: pl.BlockSpec((rows, tp), lambda i: (0, i))
    return pl.pallas_call(
        _pos_kernel,
        grid=(t // tp,),
        in_specs=[tok(ne), _const_spec(cnt.shape), _const_spec(tri.shape), _const_spec(lstrict.shape)],
        out_specs=[tok(TOP_K), tok(TOP_K), pl.BlockSpec((ne, LANES), lambda i: (0, 0))],
        out_shape=[jax.ShapeDtypeStruct((TOP_K, t), jnp.int32),
                   jax.ShapeDtypeStruct((TOP_K, t), F32),
                   jax.ShapeDtypeStruct((ne, LANES), F32)],
        scratch_shapes=[pltpu.VMEM((ne, LANES), F32)],
        compiler_params=_params("arbitrary"),
        name="positions",
    )(comb_t, cnt, tri, lstrict)


def _sc_workers():
    info = plsc.get_sparse_core_info()
    return info.num_cores, info.num_cores * info.num_subcores


def _sc_scatter_rows(rows, pos, n_out):
    nc, nw = _sc_workers()
    n, w = rows.shape
    nk = pos.shape[0]
    ch = SC_CHUNK
    per_w = n // nw
    assert per_w * nw == n and per_w % ch == 0

    @functools.partial(
        pl.kernel, mesh=plsc.VectorSubcoreMesh(core_axis_name="c", subcore_axis_name="s"),
        out_type=jax.ShapeDtypeStruct((n_out, w), rows.dtype),
        scratch_types=[pltpu.VMEM((nk, ch), jnp.int32), pltpu.VMEM((ch, w), rows.dtype),
                       pltpu.SemaphoreType.DMA])
    def scatter(rows_hbm, pos_hbm, out_hbm, idx_v, rows_v, sem):
        base = (lax.axis_index("s") * nc + lax.axis_index("c")) * per_w

        @pl.loop(0, per_w // ch)
        def _(ci):
            off = pl.multiple_of(base + ci * ch, ch)
            pltpu.sync_copy(pos_hbm.at[:, pl.ds(off, ch)], idx_v)
            pltpu.sync_copy(rows_hbm.at[pl.ds(off, ch)], rows_v)
            copies = [pltpu.make_async_copy(rows_v, out_hbm.at[idx_v.at[k]], sem) for k in range(nk)]
            for cp in copies:
                cp.start()
            for cp in copies:
                cp.wait()

    return scatter(rows, pos)


def _sc_gather_rows(table, idx):
    nc, nw = _sc_workers()
    n = idx.shape[0]
    w = table.shape[1]
    ch = SC_CHUNK
    per_w = n // nw
    assert per_w * nw == n and per_w % ch == 0

    @functools.partial(
        pl.kernel, mesh=plsc.VectorSubcoreMesh(core_axis_name="c", subcore_axis_name="s"),
        out_type=jax.ShapeDtypeStruct((n, w), table.dtype),
        scratch_types=[pltpu.VMEM((ch,), jnp.int32), pltpu.VMEM((ch, w), table.dtype),
                       pltpu.SemaphoreType.DMA])
    def gather(table_hbm, idx_hbm, out_hbm, idx_v, rows_v, sem):
        base = (lax.axis_index("s") * nc + lax.axis_index("c")) * per_w

        @pl.loop(0, per_w // ch)
        def _(ci):
            off = pl.multiple_of(base + ci * ch, ch)
            pltpu.sync_copy(idx_hbm.at[pl.ds(off, ch)], idx_v)
            cp = pltpu.make_async_copy(table_hbm.at[idx_v], rows_v, sem)
            cp.start()
            cp.wait()
            pltpu.sync_copy(rows_v, out_hbm.at[pl.ds(off, ch)])

    return gather(table, idx)


def _expert_kernel(te_ref, nu_ref, xs_ref, wgu_ref, wd_ref, y_ref, wgu_b, wd_b):
    i = pl.program_id(0)

    @pl.when(i < nu_ref[0])
    def _():
        @pl.when((i == 0) | (te_ref[i] != te_ref[jnp.maximum(i - 1, 0)]))
        def _():
            wgu_b[...] = wgu_ref[0].astype(BF16)
            wd_b[...] = wd_ref[0].astype(BF16)

        lo, hi = _unpack_halves(xs_ref[...])
        half = wgu_b.shape[0] // 2
        gu = _dot(lo.astype(BF16), wgu_b[:half, :]) + _dot(hi.astype(BF16), wgu_b[half:, :])
        act = _silu(gu[:, :EXPERT_DIM]) * gu[:, EXPERT_DIM:]
        y_ref[...] = _pack_halves(_dot(act.astype(BF16), wd_b[...]))


def _experts(tile_e, n_used, xs, wgu, wd):
    r, w = xs.shape
    tm = EXPERT_TILE
    d = wgu.shape[1]
    rows = pl.BlockSpec((tm, w), lambda i, te, nu: (jnp.minimum(i, nu[0] - 1), 0))
    return pl.pallas_call(
        _expert_kernel,
        grid_spec=pltpu.PrefetchScalarGridSpec(
            num_scalar_prefetch=2,
            grid=(r // tm,),
            in_specs=[rows,
                      pl.BlockSpec((1, d, 2 * EXPERT_DIM), lambda i, te, nu: (te[i], 0, 0)),
                      pl.BlockSpec((1, EXPERT_DIM, d), lambda i, te, nu: (te[i], 0, 0))],
            out_specs=rows,
            scratch_shapes=[pltpu.VMEM((d, 2 * EXPERT_DIM), BF16), pltpu.VMEM((EXPERT_DIM, d), BF16)]),
        out_shape=jax.ShapeDtypeStruct((r, w), jnp.int32),
        compiler_params=_params("arbitrary"),
        name="experts",
    )(tile_e, n_used, xs, wgu, wd)


def _final_kernel(yk_ref, wk_ref, h_ref, x1_ref, mod_ref, wsgu_ref, wsd_ref, out_ref):
    half = yk_ref.shape[2]
    acc_lo = jnp.zeros((yk_ref.shape[1], half), F32)
    acc_hi = jnp.zeros((yk_ref.shape[1], half), F32)
    for k in range(TOP_K):
        lo, hi = _unpack_halves(yk_ref[k])
        w = wk_ref[:, k:k + 1]
        acc_lo = acc_lo + jnp.where(w != 0.0, w * lo, 0.0)
        acc_hi = acc_hi + jnp.where(w != 0.0, w * hi, 0.0)
    routed = jnp.concatenate([acc_lo, acc_hi], axis=1)
    sgu = _dot(h_ref[...], wsgu_ref[...])
    act = _silu(sgu[:, :SHARED_DIM]) * sgu[:, SHARED_DIM:]
    shared = _dot(act.astype(BF16), wsd_ref[...])
    out_ref[...] = x1_ref[...] + mod_ref[0, 5:6, :] * (routed + shared)


def _final(yk, wk_t, h2, x1, mod, wsgu, wsd, tiles_per_batch):
    t, d = h2.shape
    tm = ROW_TILE
    row = lambda w: pl.BlockSpec((tm, w), lambda i: (i, 0))
    return pl.pallas_call(
        _final_kernel,
        grid=(t // tm,),
        in_specs=[pl.BlockSpec((TOP_K, tm, d // 2), lambda i: (0, i, 0)),
                  row(TOP_K), row(d), row(d),
                  pl.BlockSpec((1, 6, d), lambda i: (i // tiles_per_batch, 0, 0)),
                  _const_spec(wsgu.shape), _const_spec(wsd.shape)],
        out_specs=row(d),
        out_shape=jax.ShapeDtypeStruct((t, d), F32),
        compiler_params=_params("arbitrary"),
        name="final",
    )(yk, wk_t, h2, x1, mod, wsgu, wsd)


def _pad_heads(w):
    d = w.shape[0]
    w = w.reshape(d, N_HEADS, HEAD_DIM)
    return jnp.pad(w, ((0, 0), (0, 0), (0, HEAD_PAD - HEAD_DIM))).reshape(d, N_HEADS * HEAD_PAD)


def _placement():
    pq = np.zeros((LANES, N_HEADS * HEAD_PAD), np.float32)
    pk = np.zeros((LANES, N_HEADS * HEAD_PAD), np.float32)
    cq = np.zeros((1, N_HEADS * HEAD_PAD), np.float32)
    ck = np.zeros((1, N_HEADS * HEAD_PAD), np.float32)
    for hd in range(N_HEADS):
        for k in range(3):
            pq[k * N_HEADS + hd, hd * HEAD_PAD + AUG0 + k] = 1.0
            ck[0, hd * HEAD_PAD + AUG0 + k] = 1.0
            pk[k * N_HEADS + hd, hd * HEAD_PAD + AUG0 + 3 + k] = -1.0
            cq[0, hd * HEAD_PAD + AUG0 + 3 + k] = 1.0
    return (jnp.asarray(pq, BF16), jnp.asarray(pk, BF16), jnp.asarray(cq), jnp.asarray(ck))


def kernel(x, c, w_ada, b_ada, norm1_g, w_in, w_dw, b_dw, conv_gn_g, conv_gn_b, w_conv_out,
           q_norm_g, k_norm_g, b_forget, w_attn_out, w_out, norm2_g, w_router, router_bias,
           w_experts_gate_up, w_experts_down, w_shared_gate_up, w_shared_down):
    depth = w_ada.shape[0]
    b, s, d = x.shape
    off_q = 2 * CONV_DIM
    off_f = off_q + 3 * ATTN_DIM
    off_gc = off_f + N_HEADS
    off_ga = off_gc + d

    pq, pk, cq, ck = _placement()
    tri = jnp.asarray(np.tril(np.ones((ROW_TILE, ROW_TILE), np.float32)), BF16)
    grp = np.arange(CONV_DIM) // (CONV_DIM // CONV_GROUPS)
    gg = jnp.asarray((grp[:, None] == grp[None, :]).astype(np.float32) / (CONV_DIM // CONV_GROUPS), BF16)
    c_pad = jnp.pad(c, ((0, SUBLANES - b), (0, 0)))
    tri_pos = jnp.asarray(np.triu(np.ones((POS_TILE, POS_TILE), np.float32)), BF16)
    lstrict = jnp.asarray(np.tril(np.ones((N_EXPERTS, N_EXPERTS), np.float32), -1), BF16)

    for l in range(depth):
        mod = _ada(c_pad, w_ada[l], b_ada[l][None, :])[:b].reshape(b, 6, d)

        wi = w_in[l]
        bf = jnp.pad(b_forget[l][None, :], ((0, 0), (0, LANES - N_HEADS)))
        wf = jnp.pad(wi[:, off_f:off_gc], ((0, 0), (0, LANES - N_HEADS))).astype(BF16)
        gpad = lambda g, sc: jnp.tile(jnp.pad(g * sc, (0, HEAD_PAD - HEAD_DIM)), N_HEADS)[None, :]
        qscale = HEAD_DIM ** -0.5
        u, qa, ka, v, sgc, sga, cum = _inproj(
            x, mod, norm1_g[l][None, :],
            wi[:, :off_q].astype(BF16),
            _pad_heads(wi[:, off_q:off_q + ATTN_DIM]).astype(BF16),
            _pad_heads(wi[:, off_q + ATTN_DIM:off_q + 2 * ATTN_DIM]).astype(BF16),
            wi[:, off_q + 2 * ATTN_DIM:off_f].astype(BF16),
            wf, wi[:, off_gc:off_ga].astype(BF16), wi[:, off_ga:].astype(BF16),
            bf, gpad(q_norm_g[l], qscale), gpad(k_norm_g[l], 1.0),
            pq, pk, cq, ck, tri)

        flat = lambda a: a[:, :, :N_HEADS].transpose(0, 2, 1).reshape(-1)
        cs = flat(cum[:, 0::ATTN_TILE])
        ce = flat(cum[:, ATTN_TILE - 1::ATTN_TILE])
        bound = (1.02 * HEAD_DIM * qscale) * jnp.max(jnp.abs(q_norm_g[l])) * jnp.max(jnp.abs(k_norm_g[l]))
        o = _attention(cs, ce, bound.reshape(1), qa, ka, v)

        wdw = jnp.pad(w_dw[l], ((0, CONV_HALO - CONV_WIDTH), (0, 0)))
        x1 = _merge(u, o, sgc, sga, x, mod, wdw, b_dw[l][None, :], conv_gn_g[l][None, :],
                    conv_gn_b[l][None, :], gg, w_conv_out[l].astype(BF16), w_attn_out[l].astype(BF16),
                    w_out[l].astype(BF16))

        wr = w_router[l].T
        wr_hi = wr.astype(BF16)
        wr_lo = (wr - wr_hi.astype(F32)).astype(BF16)
        h2, h2w, comb_t, cnt = _router(x1, mod, norm2_g[l][None, :], wr_hi, wr_lo, router_bias[l][:, None])

        t = b * s
        posk, wk, gend = _positions(comb_t, cnt, tri_pos, lstrict)
        n_tiles = (t * TOP_K) // EXPERT_TILE + N_EXPERTS
        seg_end = gend[:, 0].astype(jnp.int32)
        n_used = seg_end[-1:] // EXPERT_TILE
        tile_start = jnp.arange(n_tiles, dtype=jnp.int32) * EXPERT_TILE
        tile_start = jnp.minimum(tile_start, seg_end[-1] - EXPERT_TILE)
        tile_e = jnp.sum((seg_end[None, :] <= tile_start[:, None]).astype(jnp.int32), axis=1)

        xs = _sc_scatter_rows(h2w.reshape(t, d // 2), posk, n_tiles * EXPERT_TILE)
        ys = _experts(tile_e, n_used, xs, w_experts_gate_up[l], w_experts_down[l])
        yk = _sc_gather_rows(ys, posk.reshape(-1)).reshape(TOP_K, t, d // 2)
        out = _final(yk, wk.T, h2.reshape(t, d), x1.reshape(t, d), mod,
                     w_shared_gate_up[l].astype(BF16), w_shared_down[l].astype(BF16), s // ROW_TILE)
        x = out.reshape(b, s, d)
    return x
```

```python
import functools

import numpy as np
import jax
import jax.numpy as jnp
from jax import lax
from jax.experimental import pallas as pl
from jax.experimental.pallas import tpu as pltpu
from jax.experimental.pallas import tpu_sc as plsc

F32 = jnp.float32
BF16 = jnp.bfloat16

CONV_DIM = 512
CONV_WIDTH = 31
CONV_GROUPS = 8
N_HEADS = 8
HEAD_DIM = 64
ATTN_DIM = N_HEADS * HEAD_DIM
N_EXPERTS = 64
TOP_K = 8
N_GROUPS = 8
TOPK_GROUPS = 4
EXPERT_DIM = 256
SHARED_DIM = 256
ROUTED_SCALE = 2.5
EPS = 1e-6

LANES = 128
SUBLANES = 8
HEAD_PAD = LANES
AUG0 = HEAD_DIM
VMEM_LIMIT = 56 * 1024 * 1024

ROW_TILE = 512
ATTN_TILE = 1024
EXPERT_TILE = 512
POS_TILE = 1024
SC_CHUNK = 128
CONV_HALO = 32

NEG_BIG = -1e30
EXP_UNDERFLOW = 104.0
FIXED_SHIFT_BOUND = 40.0


def _dot(a, b):
    return jnp.dot(a, b, preferred_element_type=F32)


def _dot_nt(a, b):
    return lax.dot_general(a, b, (((1,), (1,)), ((), ())), preferred_element_type=F32)


def _split2(x):
    hi = x.astype(BF16)
    lo = (x - hi.astype(F32)).astype(BF16)
    return hi, lo


def _split3(x):
    hi = x.astype(BF16)
    r = x - hi.astype(F32)
    mid = r.astype(BF16)
    lo = (r - mid.astype(F32)).astype(BF16)
    return hi, mid, lo


def _pack_halves(v):
    n = v.shape[1] // 2
    lo = lax.bitcast_convert_type(v[:, :n].astype(BF16).astype(F32), jnp.uint32)
    hi = lax.bitcast_convert_type(v[:, n:].astype(BF16).astype(F32), jnp.uint32)
    return lax.bitcast_convert_type(hi | lax.shift_right_logical(lo, jnp.uint32(16)), jnp.int32)


def _unpack_halves(w):
    u = lax.bitcast_convert_type(w, jnp.uint32)
    lo = lax.bitcast_convert_type(lax.shift_left(u, jnp.uint32(16)), F32)
    hi = lax.bitcast_convert_type(u & jnp.uint32(0xFFFF0000), F32)
    return lo, hi


def _sigmoid(x):
    return 1.0 / (1.0 + jnp.exp(-x))


def _silu(x):
    return x * _sigmoid(x)


def _params(*sem):
    return pltpu.CompilerParams(dimension_semantics=sem, vmem_limit_bytes=VMEM_LIMIT)


def _const_spec(shape):
    n = len(shape)
    return pl.BlockSpec(shape, lambda *_: (0,) * n, pipeline_mode=pl.Buffered(1))


def _ada_kernel(c_ref, w_ref, b_ref, o_ref):
    c = c_ref[...]
    a_hi, a_lo = _split2(_silu(c))
    w_hi, w_lo = _split2(w_ref[...])
    o_ref[...] = _dot(a_hi, w_hi) + _dot(a_hi, w_lo) + _dot(a_lo, w_hi) + b_ref[...]


def _ada(c_pad, w_ada, b_ada):
    d = c_pad.shape[1]
    n = w_ada.shape[1]
    return pl.pallas_call(
        _ada_kernel,
        grid=(n // d,),
        in_specs=[_const_spec(c_pad.shape),
                  pl.BlockSpec((d, d), lambda j: (0, j)),
                  pl.BlockSpec((1, d), lambda j: (0, j))],
        out_specs=pl.BlockSpec((c_pad.shape[0], d), lambda j: (0, j)),
        out_shape=jax.ShapeDtypeStruct((c_pad.shape[0], n), F32),
        compiler_params=_params("arbitrary"),
        name="ada",
    )(c_pad, w_ada, b_ada)


def _lane_pieces(x):
    hi, mid, lo = _split3(x)
    return (hi.astype(F32) + pltpu.roll(mid.astype(F32), N_HEADS, 1)
            + pltpu.roll(lo.astype(F32), 2 * N_HEADS, 1)).astype(BF16)


def _head_tile(p, col0, hd):
    per = HEAD_PAD // HEAD_DIM
    g, part = divmod(hd, per)
    blk = p[:, col0 + g * HEAD_PAD:col0 + (g + 1) * HEAD_PAD]
    if part:
        blk = pltpu.roll(blk, HEAD_PAD - part * HEAD_DIM, 1)
    lane = lax.broadcasted_iota(jnp.int32, (1, HEAD_PAD), 1)
    return jnp.where(lane < HEAD_DIM, blk, 0.0)


def _inproj_kernel(x_ref, mod_ref, g1_ref, wcv_ref, wq_ref, wk_ref, wv_ref, wf_ref, wgc_ref,
                   wga_ref, bf_ref, qg_ref, kg_ref, pq_ref, pk_ref, cq_ref, ck_ref, tri_ref,
                   u_ref, qa_ref, ka_ref, v_ref, sgc_ref, sga_ref, cum_ref, carry_ref):
    @pl.when(pl.program_id(1) == 0)
    def _():
        carry_ref[...] = jnp.zeros_like(carry_ref)

    x = x_ref[0]
    ms = jnp.mean(x * x, axis=-1, keepdims=True)
    h = (x * lax.rsqrt(ms + EPS) * g1_ref[...]) * (1.0 + mod_ref[0, 1:2, :]) + mod_ref[0, 0:1, :]
    hb = h.astype(BF16)

    pc = _dot(hb, wcv_ref[...])
    u_ref[0] = (pc[:, :CONV_DIM] * _sigmoid(pc[:, CONV_DIM:])).astype(BF16)

    z = _dot(hb, wf_ref[...]) + bf_ref[...]
    lf = jnp.minimum(z, 0.0) - jnp.log1p(jnp.exp(-jnp.abs(z)))
    lane = lax.broadcasted_iota(jnp.int32, (1, LANES), 1)
    psum = _dot(tri_ref[...], _lane_pieces(jnp.where(lane < N_HEADS, lf, 0.0)))
    cum = (psum + pltpu.roll(psum, LANES - N_HEADS, 1) + pltpu.roll(psum, LANES - 2 * N_HEADS, 1)
           + carry_ref[...])
    cum = jnp.where(lane < N_HEADS, cum, 0.0)
    carry_ref[...] = cum[ROW_TILE - 1:ROW_TILE, :]
    cum_ref[0] = cum

    pieces = _lane_pieces(cum)
    addq = _dot(pieces, pq_ref[...]) + cq_ref[...]
    addk = _dot(pieces, pk_ref[...]) + ck_ref[...]

    pq = _dot(hb, wq_ref[...])
    pk = _dot(hb, wk_ref[...])
    pv = _dot(hb, wv_ref[...])
    inv_hd = 1.0 / HEAD_DIM
    vone = (lax.broadcasted_iota(jnp.int32, (1, HEAD_PAD), 1) == HEAD_DIM).astype(F32)
    for hd in range(N_HEADS):
        sl = slice(hd * HEAD_PAD, (hd + 1) * HEAD_PAD)
        qb = pq[:, sl]
        qn = qb * lax.rsqrt(jnp.sum(qb * qb, axis=-1, keepdims=True) * inv_hd + EPS) * qg_ref[:, sl]
        qa_ref[0, hd] = (qn + addq[:, sl]).astype(BF16)
        kb = pk[:, sl]
        kn = kb * lax.rsqrt(jnp.sum(kb * kb, axis=-1, keepdims=True) * inv_hd + EPS) * kg_ref[:, sl]
        ka_ref[0, hd] = (kn + addk[:, sl]).astype(BF16)
        v_ref[0, hd] = (_head_tile(pv, 0, hd) + vone).astype(BF16)

    sgc_ref[0] = _sigmoid(_dot(hb, wgc_ref[...])).astype(BF16)
    sga_ref[0] = _sigmoid(_dot(hb, wga_ref[...])).astype(BF16)


def _inproj(x, mod, g1, wcv, wq, wk, wv, wf, wgc, wga, bf, qg, kg, pq, pk, cq, ck, tri):
    b, s, d = x.shape
    tm = ROW_TILE
    row = lambda w: pl.BlockSpec((1, tm, w), lambda bi, i: (bi, i, 0))
    head = pl.BlockSpec((1, N_HEADS, tm, HEAD_PAD), lambda bi, i: (bi, 0, i, 0))
    consts = [g1, wcv, wq, wk, wv, wf, wgc, wga, bf, qg, kg, pq, pk, cq, ck, tri]
    return pl.pallas_call(
        _inproj_kernel,
        grid=(b, s // tm),
        in_specs=[row(d), pl.BlockSpec((1, 6, d), lambda bi, i: (bi, 0, 0))]
                 + [_const_spec(a.shape) for a in consts],
        out_specs=[row(CONV_DIM), head, head, head, row(d), row(d), row(LANES)],
        out_shape=[jax.ShapeDtypeStruct((b, s, CONV_DIM), BF16),
                   jax.ShapeDtypeStruct((b, N_HEADS, s, HEAD_PAD), BF16),
                   jax.ShapeDtypeStruct((b, N_HEADS, s, HEAD_PAD), BF16),
                   jax.ShapeDtypeStruct((b, N_HEADS, s, HEAD_PAD), BF16),
                   jax.ShapeDtypeStruct((b, s, d), BF16),
                   jax.ShapeDtypeStruct((b, s, d), BF16),
                   jax.ShapeDtypeStruct((b, s, LANES), F32)],
        scratch_shapes=[pltpu.VMEM((1, LANES), F32)],
        compiler_params=_params("arbitrary", "arbitrary"),
        name="inproj",
    )(x, mod, *consts)


def _attn_kernel(cs_ref, ce_ref, prm_ref, q_ref, k_ref, v_ref, o_ref, acc_ref, m_ref):
    t = ATTN_TILE
    nb = q_ref.shape[2] // t
    base = (pl.program_id(0) * pl.num_programs(1) + pl.program_id(1)) * nb
    bound = prm_ref[0]
    thresh = -(EXP_UNDERFLOW + 2.0 * bound)
    causal = lax.broadcasted_iota(jnp.int32, (t, t), 0) >= lax.broadcasted_iota(jnp.int32, (t, t), 1)

    def first_block(i):
        c0 = cs_ref[base + i]
        return lax.fori_loop(0, i, lambda j, n: n + (c0 - ce_ref[base + j] < thresh).astype(jnp.int32), 0)

    def scores(q, j, masked):
        k0 = pl.multiple_of(j * t, t)
        s = _dot_nt(q, k_ref[0, 0, pl.ds(k0, t), :])
        if masked:
            s = jnp.where(causal, s, NEG_BIG)
        return s, v_ref[0, 0, pl.ds(k0, t), :]

    def finish(q0):
        acc = acc_ref[...]
        o_ref[0, pl.ds(q0, t), :] = (acc / acc[:, HEAD_DIM:HEAD_DIM + 1]).astype(BF16)

    def fixed_shift(i, carry):
        q0 = pl.multiple_of(i * t, t)
        q = q_ref[0, 0, pl.ds(q0, t), :]

        acc_ref[...] = jnp.zeros_like(acc_ref)

        def weighted(j, masked):
            s, vb = scores(q, j, masked)
            return _dot(jnp.exp(s).astype(BF16), vb)

        def kv(j, c):
            acc_ref[...] += weighted(j, False)
            return c

        lax.fori_loop(first_block(i), i, kv, 0)
        acc_ref[...] += weighted(i, True)
        finish(q0)
        return carry

    def running_max(i, carry):
        q0 = pl.multiple_of(i * t, t)
        q = q_ref[0, 0, pl.ds(q0, t), :]
        m_ref[...] = jnp.full_like(m_ref, -jnp.inf)
        acc_ref[...] = jnp.zeros_like(acc_ref)

        def step(j, masked):
            s, vb = scores(q, j, masked)
            m_prev = m_ref[...]
            m_new = jnp.maximum(m_prev, jnp.max(s, axis=-1, keepdims=True))
            p = jnp.exp(s - m_new)
            acc_ref[...] = jnp.exp(m_prev - m_new) * acc_ref[...] + _dot(p.astype(BF16), vb)
            m_ref[...] = m_new

        def kv(j, c):
            step(j, False)
            return c

        lax.fori_loop(first_block(i), i, kv, 0)
        step(i, True)
        finish(q0)
        return carry

    @pl.when(bound <= FIXED_SHIFT_BOUND)
    def _():
        lax.fori_loop(0, nb, fixed_shift, 0)

    @pl.when(bound > FIXED_SHIFT_BOUND)
    def _():
        lax.fori_loop(0, nb, running_max, 0)


def _attention(cs, ce, prm, qa, ka, v):
    b, nh, s, hp = qa.shape
    t = ATTN_TILE
    seq = pl.BlockSpec((1, 1, s, hp), lambda bi, hi, *_: (bi, hi, 0, 0))
    return pl.pallas_call(
        _attn_kernel,
        grid_spec=pltpu.PrefetchScalarGridSpec(
            num_scalar_prefetch=3,
            grid=(b, nh),
            in_specs=[seq, seq, seq],
            out_specs=pl.BlockSpec((1, s, hp), lambda bi, hi, *_: (bi, 0, hi)),
            scratch_shapes=[pltpu.VMEM((t, hp), F32), pltpu.VMEM((t, 1), F32)]),
        out_shape=jax.ShapeDtypeStruct((b, s, nh * hp), BF16),
        compiler_params=_params("arbitrary", "arbitrary"),
        name="attn",
    )(cs, ce, prm, qa, ka, v)


def _merge_kernel(u_ref, halo_ref, o_ref, sgc_ref, sga_ref, x_ref, mod_ref, wdw_ref, bdw_ref,
                  gng_ref, gnb_ref, gg_ref, wco_ref, wao_ref, wout_ref, x1_ref, buf_ref):
    tm = ROW_TILE
    halo = halo_ref[0].astype(F32)
    halo = jnp.where(pl.program_id(1) == 0, jnp.zeros_like(halo), halo)
    ucur = u_ref[0].astype(F32)
    for cb in range(CONV_DIM // LANES):
        buf_ref[cb, 0:CONV_HALO, :] = halo[:, cb * LANES:(cb + 1) * LANES]
        buf_ref[cb, CONV_HALO:, :] = ucur[:, cb * LANES:(cb + 1) * LANES]

    base = CONV_HALO - (CONV_WIDTH - 1)
    ys = []
    for cb in range(CONV_DIM // LANES):
        acc = jnp.zeros((tm, LANES), F32)
        for j in range(CONV_WIDTH):
            acc = acc + wdw_ref[j:j + 1, cb * LANES:(cb + 1) * LANES] * buf_ref[cb, base + j:base + j + tm, :]
        ys.append(acc)
    y = jnp.concatenate(ys, axis=1) + bdw_ref[...]

    gg = gg_ref[...]
    y_hi, y_lo = _split2(y)
    dlt = y - (_dot(y_hi, gg) + _dot(y_lo, gg))
    s_hi, s_lo = _split2(dlt * dlt)
    var = _dot(s_hi, gg) + _dot(s_lo, gg)
    yn = dlt * lax.rsqrt(var + EPS) * gng_ref[...] + gnb_ref[...]
    y_conv = _dot(_silu(yn).astype(BF16), wco_ref[...])

    per = HEAD_PAD // HEAD_DIM
    lane = lax.broadcasted_iota(jnp.int32, (1, HEAD_PAD), 1)
    packed = []
    for g in range(N_HEADS // per):
        tile = o_ref[0, :, g * per * HEAD_PAD:(g * per + 1) * HEAD_PAD].astype(F32)
        for part in range(1, per):
            nxt = o_ref[0, :, (g * per + part) * HEAD_PAD:(g * per + part + 1) * HEAD_PAD].astype(F32)
            tile = jnp.where(lane < part * HEAD_DIM, tile, pltpu.roll(nxt, part * HEAD_DIM, 1))
        packed.append(tile)
    y_attn = _dot(jnp.concatenate(packed, axis=1).astype(BF16), wao_ref[...])
    merged = sgc_ref[0].astype(F32) * y_conv + sga_ref[0].astype(F32) * y_attn
    mix = _dot(merged.astype(BF16), wout_ref[...])
    x1_ref[0] = x_ref[0] + mod_ref[0, 2:3, :] * mix


def _merge(u, o, sgc, sga, x, mod, wdw, bdw, gng, gnb, gg, wco, wao, wout):
    b, s, d = x.shape
    tm = ROW_TILE
    per = tm // CONV_HALO
    row = lambda w: pl.BlockSpec((1, tm, w), lambda bi, i: (bi, i, 0))
    consts = [wdw, bdw, gng, gnb, gg, wco, wao, wout]
    return pl.pallas_call(
        _merge_kernel,
        grid=(b, s // tm),
        in_specs=[row(CONV_DIM),
                  pl.BlockSpec((1, CONV_HALO, CONV_DIM),
                               lambda bi, i: (bi, jnp.maximum(i * per - 1, 0), 0)),
                  row(o.shape[2]), row(d), row(d), row(d),
                  pl.BlockSpec((1, 6, d), lambda bi, i: (bi, 0, 0))]
                 + [_const_spec(a.shape) for a in consts],
        out_specs=row(d),
        out_shape=jax.ShapeDtypeStruct((b, s, d), F32),
        scratch_shapes=[pltpu.VMEM((CONV_DIM // LANES, CONV_HALO + tm, LANES), F32)],
        compiler_params=_params("arbitrary", "arbitrary"),
        name="merge",
    )(u, u, o, sgc, sga, x, mod, *consts)


def _router_kernel(x1_ref, mod_ref, g2_ref, wr_hi_ref, wr_lo_ref, rb_ref, h2_ref, h2w_ref, comb_ref, cnt_ref):
    x = x1_ref[0]
    ms = jnp.mean(x * x, axis=-1, keepdims=True)
    h = (x * lax.rsqrt(ms + EPS) * g2_ref[...]) * (1.0 + mod_ref[0, 4:5, :]) + mod_ref[0, 3:4, :]
    h2_ref[0] = h.astype(BF16)
    h2w_ref[0] = _pack_halves(h)

    h_hi, h_lo = _split2(h)
    logits = _dot_nt(wr_hi_ref[...], h_hi) + _dot_nt(wr_hi_ref[...], h_lo) + _dot_nt(wr_lo_ref[...], h_hi)
    scores = _sigmoid(logits)
    biased = scores + rb_ref[...]

    per = N_EXPERTS // N_GROUPS
    rows = lax.broadcasted_iota(jnp.int32, (per, biased.shape[1]), 0)
    gscore = []
    for g in range(N_GROUPS):
        blk = biased[g * per:(g + 1) * per, :]
        top1 = jnp.max(blk, axis=0, keepdims=True)
        first = jnp.min(jnp.where(blk == top1, rows, per), axis=0, keepdims=True)
        top2 = jnp.max(jnp.where(rows == first, -jnp.inf, blk), axis=0, keepdims=True)
        gscore.append(top1 + top2)

    cand = []
    for g in range(N_GROUPS):
        rank = jnp.zeros_like(gscore[g], dtype=jnp.int32)
        for g2 in range(N_GROUPS):
            if g2 == g:
                continue
            ahead = gscore[g2] > gscore[g]
            if g2 < g:
                ahead = ahead | (gscore[g2] == gscore[g])
            rank = rank + ahead.astype(jnp.int32)
        keep = rank < TOPK_GROUPS
        cand.append(jnp.where(keep, biased[g * per:(g + 1) * per, :], -jnp.inf))
    cand = jnp.concatenate(cand, axis=0)

    eidx = lax.broadcasted_iota(jnp.int32, cand.shape, 0)
    work = cand
    for _ in range(TOP_K):
        top = jnp.max(work, axis=0, keepdims=True)
        first = jnp.min(jnp.where(work == top, eidx, N_EXPERTS), axis=0, keepdims=True)
        work = jnp.where(eidx == first, -jnp.inf, work)
    sel = (work != cand) & (cand > -jnp.inf)
    w = jnp.where(sel, scores, 0.0)
    comb = w / jnp.sum(w, axis=0, keepdims=True) * ROUTED_SCALE
    comb_ref[...] = comb

    @pl.when((pl.program_id(0) == 0) & (pl.program_id(1) == 0))
    def _():
        cnt_ref[...] = jnp.zeros_like(cnt_ref)

    hit = jnp.where(comb != 0.0, 1.0, 0.0).astype(BF16)
    cnt_ref[...] += _dot(hit, jnp.ones((hit.shape[1], LANES), BF16))


def _router(x1, mod, g2, wr_hi, wr_lo, rb):
    b, s, d = x1.shape
    tm = ROW_TILE
    nt = s // tm
    return pl.pallas_call(
        _router_kernel,
        grid=(b, nt),
        in_specs=[pl.BlockSpec((1, tm, d), lambda bi, i: (bi, i, 0)),
                  pl.BlockSpec((1, 6, d), lambda bi, i: (bi, 0, 0)),
                  _const_spec(g2.shape), _const_spec(wr_hi.shape), _const_spec(wr_lo.shape),
                  _const_spec(rb.shape)],
        out_specs=[pl.BlockSpec((1, tm, d), lambda bi, i: (bi, i, 0)),
                   pl.BlockSpec((1, tm, d // 2), lambda bi, i: (bi, i, 0)),
                   pl.BlockSpec((N_EXPERTS, tm), lambda bi, i: (0, bi * nt + i)),
                   pl.BlockSpec((N_EXPERTS, LANES), lambda bi, i: (0, 0))],
        out_shape=[jax.ShapeDtypeStruct((b, s, d), BF16),
                   jax.ShapeDtypeStruct((b, s, d // 2), jnp.int32),
                   jax.ShapeDtypeStruct((N_EXPERTS, b * s), F32),
                   jax.ShapeDtypeStruct((N_EXPERTS, LANES), F32)],
        compiler_params=_params("arbitrary", "arbitrary"),
        name="router",
    )(x1, mod, g2, wr_hi, wr_lo, rb)


def _pos_kernel(comb_ref, cnt_ref, tri_ref, lstrict_ref, posk_ref, wk_ref, gend_ref, base_ref):
    tp = comb_ref.shape[1]
    comb = comb_ref[...]
    sel = comb != 0.0
    selb = jnp.where(sel, 1.0, 0.0).astype(BF16)

    @pl.when(pl.program_id(0) == 0)
    def _():
        seg = jnp.floor((cnt_ref[...] + (EXPERT_TILE - 1.0)) * (1.0 / EXPERT_TILE)) * EXPERT_TILE
        s_hi, s_mid, s_lo = _split3(seg)
        ls = lstrict_ref[...]
        start = _dot(ls, s_hi) + _dot(ls, s_mid) + _dot(ls, s_lo)
        base_ref[...] = start
        gend_ref[...] = start + seg

    rank = _dot(selb, tri_ref[...])
    pos = base_ref[:, 0:1] + rank - 1.0
    base_ref[...] += _dot(selb, jnp.ones((tp, LANES), BF16))
    slot = _dot(lstrict_ref[...], selb)
    rows_p, rows_w = [], []
    for k in range(TOP_K):
        m = sel & (slot == k)
        rows_p.append(jnp.sum(jnp.where(m, pos, 0.0), axis=0, keepdims=True))
        rows_w.append(jnp.sum(jnp.where(m, comb, 0.0), axis=0, keepdims=True))
    posk_ref[...] = jnp.concatenate(rows_p, axis=0).astype(jnp.int32)
    wk_ref[...] = jnp.concatenate(rows_w, axis=0)


def _positions(comb_t, cnt, tri, lstrict):
    ne, t = comb_t.shape
    tp = POS_TILE
    tok = lambda rows: pl.BlockSpec((rows, tp), lambda i: (0, i))
    return pl.pallas_call(
        _pos_kernel,
        grid=(t // tp,),
        in_specs=[tok(ne), _const_spec(cnt.shape), _const_spec(tri.shape), _const_spec(lstrict.shape)],
        out_specs=[tok(TOP_K), tok(TOP_K), pl.BlockSpec((ne, LANES), lambda i: (0, 0))],
        out_shape=[jax.ShapeDtypeStruct((TOP_K, t), jnp.int32),
                   jax.ShapeDtypeStruct((TOP_K, t), F32),
                   jax.ShapeDtypeStruct((ne, LANES), F32)],
        scratch_shapes=[pltpu.VMEM((ne, LANES), F32)],
        compiler_params=_params("arbitrary"),
        name="positions",
    )(comb_t, cnt, tri, lstrict)


def _sc_workers():
    info = plsc.get_sparse_core_info()
    return info.num_cores, info.num_cores * info.num_subcores


def _sc_scatter_rows(rows, pos, n_out):
    nc, nw = _sc_workers()
    n, w = rows.shape
    nk = pos.shape[0]
    ch = SC_CHUNK
    per_w = n // nw
    assert per_w * nw == n and per_w % ch == 0

    @functools.partial(
        pl.kernel, mesh=plsc.VectorSubcoreMesh(core_axis_name="c", subcore_axis_name="s"),
        out_type=jax.ShapeDtypeStruct((n_out, w), rows.dtype),
        scratch_types=[pltpu.VMEM((nk, ch), jnp.int32), pltpu.VMEM((ch, w), rows.dtype),
                       pltpu.SemaphoreType.DMA])
    def scatter(rows_hbm, pos_hbm, out_hbm, idx_v, rows_v, sem):
        base = (lax.axis_index("s") * nc + lax.axis_index("c")) * per_w

        @pl.loop(0, per_w // ch)
        def _(ci):
            off = pl.multiple_of(base + ci * ch, ch)
            pltpu.sync_copy(pos_hbm.at[:, pl.ds(off, ch)], idx_v)
            pltpu.sync_copy(rows_hbm.at[pl.ds(off, ch)], rows_v)
            copies = [pltpu.make_async_copy(rows_v, out_hbm.at[idx_v.at[k]], sem) for k in range(nk)]
            for cp in copies:
                cp.start()
            for cp in copies:
                cp.wait()

    return scatter(rows, pos)


def _sc_gather_rows(table, idx):
    nc, nw = _sc_workers()
    n = idx.shape[0]
    w = table.shape[1]
    ch = SC_CHUNK
    per_w = n // nw
    assert per_w * nw == n and per_w % ch == 0

    @functools.partial(
        pl.kernel, mesh=plsc.VectorSubcoreMesh(core_axis_name="c", subcore_axis_name="s"),
        out_type=jax.ShapeDtypeStruct((n, w), table.dtype),
        scratch_types=[pltpu.VMEM((ch,), jnp.int32), pltpu.VMEM((ch, w), table.dtype),
                       pltpu.SemaphoreType.DMA])
    def gather(table_hbm, idx_hbm, out_hbm, idx_v, rows_v, sem):
        base = (lax.axis_index("s") * nc + lax.axis_index("c")) * per_w

        @pl.loop(0, per_w // ch)
        def _(ci):
            off = pl.multiple_of(base + ci * ch, ch)
            pltpu.sync_copy(idx_hbm.at[pl.ds(off, ch)], idx_v)
            cp = pltpu.make_async_copy(table_hbm.at[idx_v], rows_v, sem)
            cp.start()
            cp.wait()
            pltpu.sync_copy(rows_v, out_hbm.at[pl.ds(off, ch)])

    return gather(table, idx)


def _expert_kernel(te_ref, nu_ref, xs_ref, wgu_ref, wd_ref, y_ref, wgu_b, wd_b):
    i = pl.program_id(0)

    @pl.when(i < nu_ref[0])
    def _():
        @pl.when((i == 0) | (te_ref[i] != te_ref[jnp.maximum(i - 1, 0)]))
        def _():
            wgu_b[...] = wgu_ref[0].astype(BF16)
            wd_b[...] = wd_ref[0].astype(BF16)

        lo, hi = _unpack_halves(xs_ref[...])
        half = wgu_b.shape[0] // 2
        gu = _dot(lo.astype(BF16), wgu_b[:half, :]) + _dot(hi.astype(BF16), wgu_b[half:, :])
        act = _silu(gu[:, :EXPERT_DIM]) * gu[:, EXPERT_DIM:]
        y_ref[...] = _pack_halves(_dot(act.astype(BF16), wd_b[...]))


def _experts(tile_e, n_used, xs, wgu, wd):
    r, w = xs.shape
    tm = EXPERT_TILE
    d = wgu.shape[1]
    rows = pl.BlockSpec((tm, w), lambda i, te, nu: (jnp.minimum(i, nu[0] - 1), 0))
    return pl.pallas_call(
        _expert_kernel,
        grid_spec=pltpu.PrefetchScalarGridSpec(
            num_scalar_prefetch=2,
            grid=(r // tm,),
            in_specs=[rows,
                      pl.BlockSpec((1, d, 2 * EXPERT_DIM), lambda i, te, nu: (te[i], 0, 0)),
                      pl.BlockSpec((1, EXPERT_DIM, d), lambda i, te, nu: (te[i], 0, 0))],
            out_specs=rows,
            scratch_shapes=[pltpu.VMEM((d, 2 * EXPERT_DIM), BF16), pltpu.VMEM((EXPERT_DIM, d), BF16)]),
        out_shape=jax.ShapeDtypeStruct((r, w), jnp.int32),
        compiler_params=_params("arbitrary"),
        name="experts",
    )(tile_e, n_used, xs, wgu, wd)


def _final_kernel(yk_ref, wk_ref, h_ref, x1_ref, mod_ref, wsgu_ref, wsd_ref, out_ref):
    half = yk_ref.shape[2]
    acc_lo = jnp.zeros((yk_ref.shape[1], half), F32)
    acc_hi = jnp.zeros((yk_ref.shape[1], half), F32)
    for k in range(TOP_K):
        lo, hi = _unpack_halves(yk_ref[k])
        w = wk_ref[:, k:k + 1]
        acc_lo = acc_lo + jnp.where(w != 0.0, w * lo, 0.0)
        acc_hi = acc_hi + jnp.where(w != 0.0, w * hi, 0.0)
    routed = jnp.concatenate([acc_lo, acc_hi], axis=1)
    sgu = _dot(h_ref[...], wsgu_ref[...])
    act = _silu(sgu[:, :SHARED_DIM]) * sgu[:, SHARED_DIM:]
    shared = _dot(act.astype(BF16), wsd_ref[...])
    out_ref[...] = x1_ref[...] + mod_ref[0, 5:6, :] * (routed + shared)


def _final(yk, wk_t, h2, x1, mod, wsgu, wsd, tiles_per_batch):
    t, d = h2.shape
    tm = ROW_TILE
    row = lambda w: pl.BlockSpec((tm, w), lambda i: (i, 0))
    return pl.pallas_call(
        _final_kernel,
        grid=(t // tm,),
        in_specs=[pl.BlockSpec((TOP_K, tm, d // 2), lambda i: (0, i, 0)),
                  row(TOP_K), row(d), row(d),
                  pl.BlockSpec((1, 6, d), lambda i: (i // tiles_per_batch, 0, 0)),
                  _const_spec(wsgu.shape), _const_spec(wsd.shape)],
        out_specs=row(d),
        out_shape=jax.ShapeDtypeStruct((t, d), F32),
        compiler_params=_params("arbitrary"),
        name="final",
    )(yk, wk_t, h2, x1, mod, wsgu, wsd)


def _pad_heads(w):
    d = w.shape[0]
    w = w.reshape(d, N_HEADS, HEAD_DIM)
    return jnp.pad(w, ((0, 0), (0, 0), (0, HEAD_PAD - HEAD_DIM))).reshape(d, N_HEADS * HEAD_PAD)


def _placement():
    pq = np.zeros((LANES, N_HEADS * HEAD_PAD), np.float32)
    pk = np.zeros((LANES, N_HEADS * HEAD_PAD), np.float32)
    cq = np.zeros((1, N_HEADS * HEAD_PAD), np.float32)
    ck = np.zeros((1, N_HEADS * HEAD_PAD), np.float32)
    for hd in range(N_HEADS):
        for k in range(3):
            pq[k * N_HEADS + hd, hd * HEAD_PAD + AUG0 + k] = 1.0
            ck[0, hd * HEAD_PAD + AUG0 + k] = 1.0
            pk[k * N_HEADS + hd, hd * HEAD_PAD + AUG0 + 3 + k] = -1.0
            cq[0, hd * HEAD_PAD + AUG0 + 3 + k] = 1.0
    return (jnp.asarray(pq, BF16), jnp.asarray(pk, BF16), jnp.asarray(cq), jnp.asarray(ck))


def kernel(x, c, w_ada, b_ada, norm1_g, w_in, w_dw, b_dw, conv_gn_g, conv_gn_b, w_conv_out,
           q_norm_g, k_norm_g, b_forget, w_attn_out, w_out, norm2_g, w_router, router_bias,
           w_experts_gate_up, w_experts_down, w_shared_gate_up, w_shared_down):
    depth = w_ada.shape[0]
    b, s, d = x.shape
    off_q = 2 * CONV_DIM
    off_f = off_q + 3 * ATTN_DIM
    off_gc = off_f + N_HEADS
    off_ga = off_gc + d

    pq, pk, cq, ck = _placement()
    tri = jnp.asarray(np.tril(np.ones((ROW_TILE, ROW_TILE), np.float32)), BF16)
    grp = np.arange(CONV_DIM) // (CONV_DIM // CONV_GROUPS)
    gg = jnp.asarray((grp[:, None] == grp[None, :]).astype(np.float32) / (CONV_DIM // CONV_GROUPS), BF16)
    c_pad = jnp.pad(c, ((0, SUBLANES - b), (0, 0)))
    tri_pos = jnp.asarray(np.triu(np.ones((POS_TILE, POS_TILE), np.float32)), BF16)
    lstrict = jnp.asarray(np.tril(np.ones((N_EXPERTS, N_EXPERTS), np.float32), -1), BF16)

    for l in range(depth):
        mod = _ada(c_pad, w_ada[l], b_ada[l][None, :])[:b].reshape(b, 6, d)

        wi = w_in[l]
        bf = jnp.pad(b_forget[l][None, :], ((0, 0), (0, LANES - N_HEADS)))
        wf = jnp.pad(wi[:, off_f:off_gc], ((0, 0), (0, LANES - N_HEADS))).astype(BF16)
        gpad = lambda g, sc: jnp.tile(jnp.pad(g * sc, (0, HEAD_PAD - HEAD_DIM)), N_HEADS)[None, :]
        qscale = HEAD_DIM ** -0.5
        u, qa, ka, v, sgc, sga, cum = _inproj(
            x, mod, norm1_g[l][None, :],
            wi[:, :off_q].astype(BF16),
            _pad_heads(wi[:, off_q:off_q + ATTN_DIM]).astype(BF16),
            _pad_heads(wi[:, off_q + ATTN_DIM:off_q + 2 * ATTN_DIM]).astype(BF16),
            wi[:, off_q + 2 * ATTN_DIM:off_f].astype(BF16),
            wf, wi[:, off_gc:off_ga].astype(BF16), wi[:, off_ga:].astype(BF16),
            bf, gpad(q_norm_g[l], qscale), gpad(k_norm_g[l], 1.0),
            pq, pk, cq, ck, tri)

        flat = lambda a: a[:, :, :N_HEADS].transpose(0, 2, 1).reshape(-1)
        cs = flat(cum[:, 0::ATTN_TILE])
        ce = flat(cum[:, ATTN_TILE - 1::ATTN_TILE])
        bound = (1.02 * HEAD_DIM * qscale) * jnp.max(jnp.abs(q_norm_g[l])) * jnp.max(jnp.abs(k_norm_g[l]))
        o = _attention(cs, ce, bound.reshape(1), qa, ka, v)

        wdw = jnp.pad(w_dw[l], ((0, CONV_HALO - CONV_WIDTH), (0, 0)))
        x1 = _merge(u, o, sgc, sga, x, mod, wdw, b_dw[l][None, :], conv_gn_g[l][None, :],
                    conv_gn_b[l][None, :], gg, w_conv_out[l].astype(BF16), w_attn_out[l].astype(BF16),
                    w_out[l].astype(BF16))

        wr = w_router[l].T
        wr_hi = wr.astype(BF16)
        wr_lo = (wr - wr_hi.astype(F32)).astype(BF16)
        h2, h2w, comb_t, cnt = _router(x1, mod, norm2_g[l][None, :], wr_hi, wr_lo, router_bias[l][:, None])

        t = b * s
        posk, wk, gend = _positions(comb_t, cnt, tri_pos, lstrict)
        n_tiles = (t * TOP_K) // EXPERT_TILE + N_EXPERTS
        seg_end = gend[:, 0].astype(jnp.int32)
        n_used = seg_end[-1:] // EXPERT_TILE
        tile_start = jnp.arange(n_tiles, dtype=jnp.int32) * EXPERT_TILE
        tile_start = jnp.minimum(tile_start, seg_end[-1] - EXPERT_TILE)
        tile_e = jnp.sum((seg_end[None, :] <= tile_start[:, None]).astype(jnp.int32), axis=1)

        xs = _sc_scatter_rows(h2w.reshape(t, d // 2), posk, n_tiles * EXPERT_TILE)
        ys = _experts(tile_e, n_used, xs, w_experts_gate_up[l], w_experts_down[l])
        yk = _sc_gather_rows(ys, posk.reshape(-1)).reshape(TOP_K, t, d // 2)
        out = _final(yk, wk.T, h2.reshape(t, d), x1.reshape(t, d), mod,
                     w_shared_gate_up[l].astype(BF16), w_shared_down[l].astype(BF16), s // ROW_TILE)
        x = out.reshape(b, s, d)
    return x
```

```python
import functools

import numpy as np
import jax
import jax.numpy as jnp
from jax import lax
from jax.experimental import pallas as pl
from jax.experimental.pallas import tpu as pltpu
from jax.experimental.pallas import tpu_sc as plsc

F32 = jnp.float32
BF16 = jnp.bfloat16

CONV_DIM = 512
CONV_WIDTH = 31
CONV_GROUPS = 8
N_HEADS = 8
HEAD_DIM = 64
ATTN_DIM = N_HEADS * HEAD_DIM
N_EXPERTS = 64
TOP_K = 8
N_GROUPS = 8
TOPK_GROUPS = 4
EXPERT_DIM = 256
SHARED_DIM = 256
ROUTED_SCALE = 2.5
EPS = 1e-6

LANES = 128
SUBLANES = 8
HEAD_PAD = LANES
AUG0 = HEAD_DIM
VMEM_LIMIT = 56 * 1024 * 1024

ROW_TILE = 512
ATTN_TILE = 1024
DIAG_PARTS = 2
EXPERT_TILE = 512
POS_TILE = 1024
SC_CHUNK = 128
CONV_HALO = 32

NEG_BIG = -1e30
EXP_UNDERFLOW = 104.0
FIXED_SHIFT_BOUND = 40.0


def _dot(a, b):
    return jnp.dot(a, b, preferred_element_type=F32)


def _dot_nt(a, b):
    return lax.dot_general(a, b, (((1,), (1,)), ((), ())), preferred_element_type=F32)


def _split2(x):
    hi = x.astype(BF16)
    lo = (x - hi.astype(F32)).astype(BF16)
    return hi, lo


def _split3(x):
    hi = x.astype(BF16)
    r = x - hi.astype(F32)
    mid = r.astype(BF16)
    lo = (r - mid.astype(F32)).astype(BF16)
    return hi, mid, lo


def _pack_halves(v):
    n = v.shape[1] // 2
    lo = lax.bitcast_convert_type(v[:, :n].astype(BF16).astype(F32), jnp.uint32)
    hi = lax.bitcast_convert_type(v[:, n:].astype(BF16).astype(F32), jnp.uint32)
    return lax.bitcast_convert_type(hi | lax.shift_right_logical(lo, jnp.uint32(16)), jnp.int32)


def _unpack_halves(w):
    u = lax.bitcast_convert_type(w, jnp.uint32)
    lo = lax.bitcast_convert_type(lax.shift_left(u, jnp.uint32(16)), F32)
    hi = lax.bitcast_convert_type(u & jnp.uint32(0xFFFF0000), F32)
    return lo, hi


def _sigmoid(x):
    return 1.0 / (1.0 + jnp.exp(-x))


def _silu(x):
    return x * _sigmoid(x)


def _params(*sem):
    return pltpu.CompilerParams(dimension_semantics=sem, vmem_limit_bytes=VMEM_LIMIT)


def _const_spec(shape):
    n = len(shape)
    return pl.BlockSpec(shape, lambda *_: (0,) * n, pipeline_mode=pl.Buffered(1))


def _ada_kernel(c_ref, w_ref, b_ref, o_ref):
    c = c_ref[...]
    a_hi, a_lo = _split2(_silu(c))
    w_hi, w_lo = _split2(w_ref[...])
    o_ref[...] = _dot(a_hi, w_hi) + _dot(a_hi, w_lo) + _dot(a_lo, w_hi) + b_ref[...]


def _ada(c_pad, w_ada, b_ada):
    d = c_pad.shape[1]
    n = w_ada.shape[1]
    return pl.pallas_call(
        _ada_kernel,
        grid=(n // d,),
        in_specs=[_const_spec(c_pad.shape),
                  pl.BlockSpec((d, d), lambda j: (0, j)),
                  pl.BlockSpec((1, d), lambda j: (0, j))],
        out_specs=pl.BlockSpec((c_pad.shape[0], d), lambda j: (0, j)),
        out_shape=jax.ShapeDtypeStruct((c_pad.shape[0], n), F32),
        compiler_params=_params("arbitrary"),
        name="ada",
    )(c_pad, w_ada, b_ada)


def _lane_pieces(x):
    hi, mid, lo = _split3(x)
    return (hi.astype(F32) + pltpu.roll(mid.astype(F32), N_HEADS, 1)
            + pltpu.roll(lo.astype(F32), 2 * N_HEADS, 1)).astype(BF16)


def _head_tile(p, col0, hd):
    per = HEAD_PAD // HEAD_DIM
    g, part = divmod(hd, per)
    blk = p[:, col0 + g * HEAD_PAD:col0 + (g + 1) * HEAD_PAD]
    if part:
        blk = pltpu.roll(blk, HEAD_PAD - part * HEAD_DIM, 1)
    lane = lax.broadcasted_iota(jnp.int32, (1, HEAD_PAD), 1)
    return jnp.where(lane < HEAD_DIM, blk, 0.0)


def _inproj_kernel(x_ref, mod_ref, g1_ref, wcv_ref, wq_ref, wk_ref, wv_ref, wf_ref, wgc_ref,
                   wga_ref, bf_ref, qg_ref, kg_ref, pq_ref, pk_ref, cq_ref, ck_ref, tri_ref,
                   u_ref, qa_ref, ka_ref, v_ref, sgc_ref, sga_ref, cum_ref, carry_ref):
    @pl.when(pl.program_id(1) == 0)
    def _():
        carry_ref[...] = jnp.zeros_like(carry_ref)

    x = x_ref[0]
    ms = jnp.mean(x * x, axis=-1, keepdims=True)
    h = (x * lax.rsqrt(ms + EPS) * g1_ref[...]) * (1.0 + mod_ref[0, 1:2, :]) + mod_ref[0, 0:1, :]
    hb = h.astype(BF16)

    pc = _dot(hb, wcv_ref[...])
    u_ref[0] = (pc[:, :CONV_DIM] * _sigmoid(pc[:, CONV_DIM:])).astype(BF16)

    z = _dot(hb, wf_ref[...]) + bf_ref[...]
    lf = jnp.minimum(z, 0.0) - jnp.log1p(jnp.exp(-jnp.abs(z)))
    lane = lax.broadcasted_iota(jnp.int32, (1, LANES), 1)
    psum = _dot(tri_ref[...], _lane_pieces(jnp.where(lane < N_HEADS, lf, 0.0)))
    cum = (psum + pltpu.roll(psum, LANES - N_HEADS, 1) + pltpu.roll(psum, LANES - 2 * N_HEADS, 1)
           + carry_ref[...])
    cum = jnp.where(lane < N_HEADS, cum, 0.0)
    carry_ref[...] = cum[ROW_TILE - 1:ROW_TILE, :]
    cum_ref[0] = cum

    pieces = _lane_pieces(cum)
    addq = _dot(pieces, pq_ref[...]) + cq_ref[...]
    addk = _dot(pieces, pk_ref[...]) + ck_ref[...]

    pq = _dot(hb, wq_ref[...])
    pk = _dot(hb, wk_ref[...])
    pv = _dot(hb, wv_ref[...])
    inv_hd = 1.0 / HEAD_DIM
    vone = (lax.broadcasted_iota(jnp.int32, (1, HEAD_PAD), 1) == HEAD_DIM).astype(F32)
    for hd in range(N_HEADS):
        sl = slice(hd * HEAD_PAD, (hd + 1) * HEAD_PAD)
        qb = pq[:, sl]
        qn = qb * lax.rsqrt(jnp.sum(qb * qb, axis=-1, keepdims=True) * inv_hd + EPS) * qg_ref[:, sl]
        qa_ref[0, hd] = (qn + addq[:, sl]).astype(BF16)
        kb = pk[:, sl]
        kn = kb * lax.rsqrt(jnp.sum(kb * kb, axis=-1, keepdims=True) * inv_hd + EPS) * kg_ref[:, sl]
        ka_ref[0, hd] = (kn + addk[:, sl]).astype(BF16)
        v_ref[0, hd] = (_head_tile(pv, 0, hd) + vone).astype(BF16)

    sgc_ref[0] = _sigmoid(_dot(hb, wgc_ref[...])).astype(BF16)
    sga_ref[0] = _sigmoid(_dot(hb, wga_ref[...])).astype(BF16)


def _inproj(x, mod, g1, wcv, wq, wk, wv, wf, wgc, wga, bf, qg, kg, pq, pk, cq, ck, tri):
    b, s, d = x.shape
    tm = ROW_TILE
    row = lambda w: pl.BlockSpec((1, tm, w), lambda bi, i: (bi, i, 0))
    head = pl.BlockSpec((1, N_HEADS, tm, HEAD_PAD), lambda bi, i: (bi, 0, i, 0))
    consts = [g1, wcv, wq, wk, wv, wf, wgc, wga, bf, qg, kg, pq, pk, cq, ck, tri]
    return pl.pallas_call(
        _inproj_kernel,
        grid=(b, s // tm),
        in_specs=[row(d), pl.BlockSpec((1, 6, d), lambda bi, i: (bi, 0, 0))]
                 + [_const_spec(a.shape) for a in consts],
        out_specs=[row(CONV_DIM), head, head, head, row(d), row(d), row(LANES)],
        out_shape=[jax.ShapeDtypeStruct((b, s, CONV_DIM), BF16),
                   jax.ShapeDtypeStruct((b, N_HEADS, s, HEAD_PAD), BF16),
                   jax.ShapeDtypeStruct((b, N_HEADS, s, HEAD_PAD), BF16),
                   jax.ShapeDtypeStruct((b, N_HEADS, s, HEAD_PAD), BF16),
                   jax.ShapeDtypeStruct((b, s, d), BF16),
                   jax.ShapeDtypeStruct((b, s, d), BF16),
                   jax.ShapeDtypeStruct((b, s, LANES), F32)],
        scratch_shapes=[pltpu.VMEM((1, LANES), F32)],
        compiler_params=_params("arbitrary", "arbitrary"),
        name="inproj",
    )(x, mod, *consts)


def _attn_kernel(cs_ref, ce_ref, prm_ref, q_ref, k_ref, v_ref, o_ref, acc_ref, m_ref):
    t = ATTN_TILE
    nb = q_ref.shape[2] // t
    base = (pl.program_id(0) * pl.num_programs(1) + pl.program_id(1)) * nb
    bound = prm_ref[0]
    thresh = -(EXP_UNDERFLOW + 2.0 * bound)
    causal = lax.broadcasted_iota(jnp.int32, (t, t), 0) >= lax.broadcasted_iota(jnp.int32, (t, t), 1)

    def first_block(i):
        c0 = cs_ref[base + i]
        return lax.fori_loop(0, i, lambda j, n: n + (c0 - ce_ref[base + j] < thresh).astype(jnp.int32), 0)

    def scores(q, j, masked):
        k0 = pl.multiple_of(j * t, t)
        s = _dot_nt(q, k_ref[0, 0, pl.ds(k0, t), :])
        if masked:
            s = jnp.where(causal, s, NEG_BIG)
        return s, v_ref[0, 0, pl.ds(k0, t), :]

    def finish(q0):
        acc = acc_ref[...]
        o_ref[0, pl.ds(q0, t), :] = (acc / acc[:, HEAD_DIM:HEAD_DIM + 1]).astype(BF16)

    def fixed_shift(i, carry):
        q0 = pl.multiple_of(i * t, t)
        q = q_ref[0, 0, pl.ds(q0, t), :]

        acc_ref[...] = jnp.zeros_like(acc_ref)

        def weighted(j, masked):
            s, vb = scores(q, j, masked)
            return _dot(jnp.exp(s).astype(BF16), vb)

        def kv(j, c):
            acc_ref[...] += weighted(j, False)
            return c

        lax.fori_loop(first_block(i), i, kv, 0)
        rp = t // DIAG_PARTS
        for r in range(DIAG_PARTS):
            nk = (r + 1) * rp
            s = _dot_nt(q[r * rp:(r + 1) * rp, :], k_ref[0, 0, pl.ds(q0, nk), :])
            seen = (lax.broadcasted_iota(jnp.int32, (rp, nk), 0) + r * rp
                    >= lax.broadcasted_iota(jnp.int32, (rp, nk), 1))
            p = jnp.exp(jnp.where(seen, s, NEG_BIG)).astype(BF16)
            acc_ref[r * rp:(r + 1) * rp, :] += _dot(p, v_ref[0, 0, pl.ds(q0, nk), :])
        finish(q0)
        return carry

    def running_max(i, carry):
        q0 = pl.multiple_of(i * t, t)
        q = q_ref[0, 0, pl.ds(q0, t), :]
        m_ref[...] = jnp.full_like(m_ref, -jnp.inf)
        acc_ref[...] = jnp.zeros_like(acc_ref)

        def step(j, masked):
            s, vb = scores(q, j, masked)
            m_prev = m_ref[...]
            m_new = jnp.maximum(m_prev, jnp.max(s, axis=-1, keepdims=True))
            p = jnp.exp(s - m_new)
            acc_ref[...] = jnp.exp(m_prev - m_new) * acc_ref[...] + _dot(p.astype(BF16), vb)
            m_ref[...] = m_new

        def kv(j, c):
            step(j, False)
            return c

        lax.fori_loop(first_block(i), i, kv, 0)
        step(i, True)
        finish(q0)
        return carry

    @pl.when(bound <= FIXED_SHIFT_BOUND)
    def _():
        lax.fori_loop(0, nb, fixed_shift, 0)

    @pl.when(bound > FIXED_SHIFT_BOUND)
    def _():
        lax.fori_loop(0, nb, running_max, 0)


def _attention(cs, ce, prm, qa, ka, v):
    b, nh, s, hp = qa.shape
    t = ATTN_TILE
    seq = pl.BlockSpec((1, 1, s, hp), lambda bi, hi, *_: (bi, hi, 0, 0))
    return pl.pallas_call(
        _attn_kernel,
        grid_spec=pltpu.PrefetchScalarGridSpec(
            num_scalar_prefetch=3,
            grid=(b, nh),
            in_specs=[seq, seq, seq],
            out_specs=pl.BlockSpec((1, s, hp), lambda bi, hi, *_: (bi, 0, hi)),
            scratch_shapes=[pltpu.VMEM((t, hp), F32), pltpu.VMEM((t, 1), F32)]),
        out_shape=jax.ShapeDtypeStruct((b, s, nh * hp), BF16),
        compiler_params=_params("arbitrary", "arbitrary"),
        name="attn",
    )(cs, ce, prm, qa, ka, v)


def _merge_kernel(u_ref, halo_ref, o_ref, sgc_ref, sga_ref, x_ref, mod_ref, wdw_ref, bdw_ref,
                  gng_ref, gnb_ref, gg_ref, wco_ref, wao_ref, wout_ref, x1_ref, buf_ref):
    tm = ROW_TILE
    halo = halo_ref[0].astype(F32)
    halo = jnp.where(pl.program_id(1) == 0, jnp.zeros_like(halo), halo)
    ucur = u_ref[0].astype(F32)
    for cb in range(CONV_DIM // LANES):
        buf_ref[cb, 0:CONV_HALO, :] = halo[:, cb * LANES:(cb + 1) * LANES]
        buf_ref[cb, CONV_HALO:, :] = ucur[:, cb * LANES:(cb + 1) * LANES]

    base = CONV_HALO - (CONV_WIDTH - 1)
    ys = []
    for cb in range(CONV_DIM // LANES):
        acc = jnp.zeros((tm, LANES), F32)
        for j in range(CONV_WIDTH):
            acc = acc + wdw_ref[j:j + 1, cb * LANES:(cb + 1) * LANES] * buf_ref[cb, base + j:base + j + tm, :]
        ys.append(acc)
    y = jnp.concatenate(ys, axis=1) + bdw_ref[...]

    gg = gg_ref[...]
    y_hi, y_lo = _split2(y)
    dlt = y - (_dot(y_hi, gg) + _dot(y_lo, gg))
    s_hi, s_lo = _split2(dlt * dlt)
    var = _dot(s_hi, gg) + _dot(s_lo, gg)
    yn = dlt * lax.rsqrt(var + EPS) * gng_ref[...] + gnb_ref[...]
    y_conv = _dot(_silu(yn).astype(BF16), wco_ref[...])

    per = HEAD_PAD // HEAD_DIM
    lane = lax.broadcasted_iota(jnp.int32, (1, HEAD_PAD), 1)
    packed = []
    for g in range(N_HEADS // per):
        tile = o_ref[0, :, g * per * HEAD_PAD:(g * per + 1) * HEAD_PAD].astype(F32)
        for part in range(1, per):
            nxt = o_ref[0, :, (g * per + part) * HEAD_PAD:(g * per + part + 1) * HEAD_PAD].astype(F32)
            tile = jnp.where(lane < part * HEAD_DIM, tile, pltpu.roll(nxt, part * HEAD_DIM, 1))
        packed.append(tile)
    y_attn = _dot(jnp.concatenate(packed, axis=1).astype(BF16), wao_ref[...])
    merged = sgc_ref[0].astype(F32) * y_conv + sga_ref[0].astype(F32) * y_attn
    mix = _dot(merged.astype(BF16), wout_ref[...])
    x1_ref[0] = x_ref[0] + mod_ref[0, 2:3, :] * mix


def _merge(u, o, sgc, sga, x, mod, wdw, bdw, gng, gnb, gg, wco, wao, wout):
    b, s, d = x.shape
    tm = ROW_TILE
    per = tm // CONV_HALO
    row = lambda w: pl.BlockSpec((1, tm, w), lambda bi, i: (bi, i, 0))
    consts = [wdw, bdw, gng, gnb, gg, wco, wao, wout]
    return pl.pallas_call(
        _merge_kernel,
        grid=(b, s // tm),
        in_specs=[row(CONV_DIM),
                  pl.BlockSpec((1, CONV_HALO, CONV_DIM),
                               lambda bi, i: (bi, jnp.maximum(i * per - 1, 0), 0)),
                  row(o.shape[2]), row(d), row(d), row(d),
                  pl.BlockSpec((1, 6, d), lambda bi, i: (bi, 0, 0))]
                 + [_const_spec(a.shape) for a in consts],
        out_specs=row(d),
        out_shape=jax.ShapeDtypeStruct((b, s, d), F32),
        scratch_shapes=[pltpu.VMEM((CONV_DIM // LANES, CONV_HALO + tm, LANES), F32)],
        compiler_params=_params("arbitrary", "arbitrary"),
        name="merge",
    )(u, u, o, sgc, sga, x, mod, *consts)


def _router_kernel(x1_ref, mod_ref, g2_ref, wr_hi_ref, wr_lo_ref, rb_ref, h2_ref, h2w_ref, comb_ref, cnt_ref):
    x = x1_ref[0]
    ms = jnp.mean(x * x, axis=-1, keepdims=True)
    h = (x * lax.rsqrt(ms + EPS) * g2_ref[...]) * (1.0 + mod_ref[0, 4:5, :]) + mod_ref[0, 3:4, :]
    h2_ref[0] = h.astype(BF16)
    h2w_ref[0] = _pack_halves(h)

    h_hi, h_lo = _split2(h)
    logits = _dot_nt(wr_hi_ref[...], h_hi) + _dot_nt(wr_hi_ref[...], h_lo) + _dot_nt(wr_lo_ref[...], h_hi)
    scores = _sigmoid(logits)
    biased = scores + rb_ref[...]

    per = N_EXPERTS // N_GROUPS
    rows = lax.broadcasted_iota(jnp.int32, (per, biased.shape[1]), 0)
    gscore = []
    for g in range(N_GROUPS):
        blk = biased[g * per:(g + 1) * per, :]
        top1 = jnp.max(blk, axis=0, keepdims=True)
        first = jnp.min(jnp.where(blk == top1, rows, per), axis=0, keepdims=True)
        top2 = jnp.max(jnp.where(rows == first, -jnp.inf, blk), axis=0, keepdims=True)
        gscore.append(top1 + top2)

    cand = []
    for g in range(N_GROUPS):
        rank = jnp.zeros_like(gscore[g], dtype=jnp.int32)
        for g2 in range(N_GROUPS):
            if g2 == g:
                continue
            ahead = gscore[g2] > gscore[g]
            if g2 < g:
                ahead = ahead | (gscore[g2] == gscore[g])
            rank = rank + ahead.astype(jnp.int32)
        keep = rank < TOPK_GROUPS
        cand.append(jnp.where(keep, biased[g * per:(g + 1) * per, :], -jnp.inf))
    cand = jnp.concatenate(cand, axis=0)

    eidx = lax.broadcasted_iota(jnp.int32, cand.shape, 0)
    work = cand
    for _ in range(TOP_K):
        top = jnp.max(work, axis=0, keepdims=True)
        first = jnp.min(jnp.where(work == top, eidx, N_EXPERTS), axis=0, keepdims=True)
        work = jnp.where(eidx == first, -jnp.inf, work)
    sel = (work != cand) & (cand > -jnp.inf)
    w = jnp.where(sel, scores, 0.0)
    comb = w / jnp.sum(w, axis=0, keepdims=True) * ROUTED_SCALE
    comb_ref[...] = comb

    @pl.when((pl.program_id(0) == 0) & (pl.program_id(1) == 0))
    def _():
        cnt_ref[...] = jnp.zeros_like(cnt_ref)

    hit = jnp.where(comb != 0.0, 1.0, 0.0).astype(BF16)
    cnt_ref[...] += _dot(hit, jnp.ones((hit.shape[1], LANES), BF16))


def _router(x1, mod, g2, wr_hi, wr_lo, rb):
    b, s, d = x1.shape
    tm = ROW_TILE
    nt = s // tm
    return pl.pallas_call(
        _router_kernel,
        grid=(b, nt),
        in_specs=[pl.BlockSpec((1, tm, d), lambda bi, i: (bi, i, 0)),
                  pl.BlockSpec((1, 6, d), lambda bi, i: (bi, 0, 0)),
                  _const_spec(g2.shape), _const_spec(wr_hi.shape), _const_spec(wr_lo.shape),
                  _const_spec(rb.shape)],
        out_specs=[pl.BlockSpec((1, tm, d), lambda bi, i: (bi, i, 0)),
                   pl.BlockSpec((1, tm, d // 2), lambda bi, i: (bi, i, 0)),
                   pl.BlockSpec((N_EXPERTS, tm), lambda bi, i: (0, bi * nt + i)),
                   pl.BlockSpec((N_EXPERTS, LANES), lambda bi, i: (0, 0))],
        out_shape=[jax.ShapeDtypeStruct((b, s, d), BF16),
                   jax.ShapeDtypeStruct((b, s, d // 2), jnp.int32),
                   jax.ShapeDtypeStruct((N_EXPERTS, b * s), F32),
                   jax.ShapeDtypeStruct((N_EXPERTS, LANES), F32)],
        compiler_params=_params("arbitrary", "arbitrary"),
        name="router",
    )(x1, mod, g2, wr_hi, wr_lo, rb)


def _pos_kernel(comb_ref, cnt_ref, tri_ref, lstrict_ref, posk_ref, wk_ref, gend_ref, base_ref):
    tp = comb_ref.shape[1]
    comb = comb_ref[...]
    sel = comb != 0.0
    selb = jnp.where(sel, 1.0, 0.0).astype(BF16)

    @pl.when(pl.program_id(0) == 0)
    def _():
        seg = jnp.floor((cnt_ref[...] + (EXPERT_TILE - 1.0)) * (1.0 / EXPERT_TILE)) * EXPERT_TILE
        s_hi, s_mid, s_lo = _split3(seg)
        ls = lstrict_ref[...]
        start = _dot(ls, s_hi) + _dot(ls, s_mid) + _dot(ls, s_lo)
        base_ref[...] = start
        gend_ref[...] = start + seg

    rank = _dot(selb, tri_ref[...])
    pos = base_ref[:, 0:1] + rank - 1.0
    base_ref[...] += _dot(selb, jnp.ones((tp, LANES), BF16))
    slot = _dot(lstrict_ref[...], selb)
    rows_p, rows_w = [], []
    for k in range(TOP_K):
        m = sel & (slot == k)
        rows_p.append(jnp.sum(jnp.where(m, pos, 0.0), axis=0, keepdims=True))
        rows_w.append(jnp.sum(jnp.where(m, comb, 0.0), axis=0, keepdims=True))
    posk_ref[...] = jnp.concatenate(rows_p, axis=0).astype(jnp.int32)
    wk_ref[...] = jnp.concatenate(rows_w, axis=0)


def _positions(comb_t, cnt, tri, lstrict):
    ne, t = comb_t.shape
    tp = POS_TILE
    tok = lambda rows: pl.BlockSpec((rows, tp), lambda i: (0, i))
    return pl.pallas_call(
        _pos_kernel,
        grid=(t // tp,),
        in_specs=[tok(ne), _const_spec(cnt.shape), _const_spec(tri.shape), _const_spec(lstrict.shape)],
        out_specs=[tok(TOP_K), tok(TOP_K), pl.BlockSpec((ne, LANES), lambda i: (0, 0))],
        out_shape=[jax.ShapeDtypeStruct((TOP_K, t), jnp.int32),
                   jax.ShapeDtypeStruct((TOP_K, t), F32),
                   jax.ShapeDtypeStruct((ne, LANES), F32)],
        scratch_shapes=[pltpu.VMEM((ne, LANES), F32)],
        compiler_params=_params("arbitrary"),
        name="positions",
    )(comb_t, cnt, tri, lstrict)


def _sc_workers():
    info = plsc.get_sparse_core_info()
    return info.num_cores, info.num_cores * info.num_subcores


def _sc_scatter_rows(rows, pos, n_out):
    nc, nw = _sc_workers()
    n, w = rows.shape
    nk = pos.shape[0]
    ch = SC_CHUNK
    per_w = n // nw
    assert per_w * nw == n and per_w % ch == 0

    @functools.partial(
        pl.kernel, mesh=plsc.VectorSubcoreMesh(core_axis_name="c", subcore_axis_name="s"),
        out_type=jax.ShapeDtypeStruct((n_out, w), rows.dtype),
        scratch_types=[pltpu.VMEM((nk, ch), jnp.int32), pltpu.VMEM((ch, w), rows.dtype),
                       pltpu.SemaphoreType.DMA])
    def scatter(rows_hbm, pos_hbm, out_hbm, idx_v, rows_v, sem):
        base = (lax.axis_index("s") * nc + lax.axis_index("c")) * per_w

        @pl.loop(0, per_w // ch)
        def _(ci):
            off = pl.multiple_of(base + ci * ch, ch)
            pltpu.sync_copy(pos_hbm.at[:, pl.ds(off, ch)], idx_v)
            pltpu.sync_copy(rows_hbm.at[pl.ds(off, ch)], rows_v)
            copies = [pltpu.make_async_copy(rows_v, out_hbm.at[idx_v.at[k]], sem) for k in range(nk)]
            for cp in copies:
                cp.start()
            for cp in copies:
                cp.wait()

    return scatter(rows, pos)


def _sc_gather_rows(table, idx):
    nc, nw = _sc_workers()
    n = idx.shape[0]
    w = table.shape[1]
    ch = SC_CHUNK
    per_w = n // nw
    assert per_w * nw == n and per_w % ch == 0

    @functools.partial(
        pl.kernel, mesh=plsc.VectorSubcoreMesh(core_axis_name="c", subcore_axis_name="s"),
        out_type=jax.ShapeDtypeStruct((n, w), table.dtype),
        scratch_types=[pltpu.VMEM((ch,), jnp.int32), pltpu.VMEM((ch, w), table.dtype),
                       pltpu.SemaphoreType.DMA])
    def gather(table_hbm, idx_hbm, out_hbm, idx_v, rows_v, sem):
        base = (lax.axis_index("s") * nc + lax.axis_index("c")) * per_w

        @pl.loop(0, per_w // ch)
        def _(ci):
            off = pl.multiple_of(base + ci * ch, ch)
            pltpu.sync_copy(idx_hbm.at[pl.ds(off, ch)], idx_v)
            cp = pltpu.make_async_copy(table_hbm.at[idx_v], rows_v, sem)
            cp.start()
            cp.wait()
            pltpu.sync_copy(rows_v, out_hbm.at[pl.ds(off, ch)])

    return gather(table, idx)


def _expert_kernel(te_ref, nu_ref, xs_ref, wgu_ref, wd_ref, y_ref, wgu_b, wd_b):
    i = pl.program_id(0)

    @pl.when(i < nu_ref[0])
    def _():
        @pl.when((i == 0) | (te_ref[i] != te_ref[jnp.maximum(i - 1, 0)]))
        def _():
            wgu_b[...] = wgu_ref[0].astype(BF16)
            wd_b[...] = wd_ref[0].astype(BF16)

        lo, hi = _unpack_halves(xs_ref[...])
        half = wgu_b.shape[0] // 2
        gu = _dot(lo.astype(BF16), wgu_b[:half, :]) + _dot(hi.astype(BF16), wgu_b[half:, :])
        act = _silu(gu[:, :EXPERT_DIM]) * gu[:, EXPERT_DIM:]
        y_ref[...] = _pack_halves(_dot(act.astype(BF16), wd_b[...]))


def _experts(tile_e, n_used, xs, wgu, wd):
    r, w = xs.shape
    tm = EXPERT_TILE
    d = wgu.shape[1]
    rows = pl.BlockSpec((tm, w), lambda i, te, nu: (jnp.minimum(i, nu[0] - 1), 0))
    return pl.pallas_call(
        _expert_kernel,
        grid_spec=pltpu.PrefetchScalarGridSpec(
            num_scalar_prefetch=2,
            grid=(r // tm,),
            in_specs=[rows,
                      pl.BlockSpec((1, d, 2 * EXPERT_DIM), lambda i, te, nu: (te[i], 0, 0)),
                      pl.BlockSpec((1, EXPERT_DIM, d), lambda i, te, nu: (te[i], 0, 0))],
            out_specs=rows,
            scratch_shapes=[pltpu.VMEM((d, 2 * EXPERT_DIM), BF16), pltpu.VMEM((EXPERT_DIM, d), BF16)]),
        out_shape=jax.ShapeDtypeStruct((r, w), jnp.int32),
        compiler_params=_params("arbitrary"),
        name="experts",
    )(tile_e, n_used, xs, wgu, wd)


def _final_kernel(yk_ref, wk_ref, h_ref, x1_ref, mod_ref, wsgu_ref, wsd_ref, out_ref):
    half = yk_ref.shape[2]
    acc_lo = jnp.zeros((yk_ref.shape[1], half), F32)
    acc_hi = jnp.zeros((yk_ref.shape[1], half), F32)
    for k in range(TOP_K):
        lo, hi = _unpack_halves(yk_ref[k])
        w = wk_ref[:, k:k + 1]
        acc_lo = acc_lo + jnp.where(w != 0.0, w * lo, 0.0)
        acc_hi = acc_hi + jnp.where(w != 0.0, w * hi, 0.0)
    routed = jnp.concatenate([acc_lo, acc_hi], axis=1)
    sgu = _dot(h_ref[...], wsgu_ref[...])
    act = _silu(sgu[:, :SHARED_DIM]) * sgu[:, SHARED_DIM:]
    shared = _dot(act.astype(BF16), wsd_ref[...])
    out_ref[...] = x1_ref[...] + mod_ref[0, 5:6, :] * (routed + shared)


def _final(yk, wk_t, h2, x1, mod, wsgu, wsd, tiles_per_batch):
    t, d = h2.shape
    tm = ROW_TILE
    row = lambda w: pl.BlockSpec((tm, w), lambda i: (i, 0))
    return pl.pallas_call(
        _final_kernel,
        grid=(t // tm,),
        in_specs=[pl.BlockSpec((TOP_K, tm, d // 2), lambda i: (0, i, 0)),
                  row(TOP_K), row(d), row(d),
                  pl.BlockSpec((1, 6, d), lambda i: (i // tiles_per_batch, 0, 0)),
                  _const_spec(wsgu.shape), _const_spec(wsd.shape)],
        out_specs=row(d),
        out_shape=jax.ShapeDtypeStruct((t, d), F32),
        compiler_params=_params("arbitrary"),
        name="final",
    )(yk, wk_t, h2, x1, mod, wsgu, wsd)


def _pad_heads(w):
    d = w.shape[0]
    w = w.reshape(d, N_HEADS, HEAD_DIM)
    return jnp.pad(w, ((0, 0), (0, 0), (0, HEAD_PAD - HEAD_DIM))).reshape(d, N_HEADS * HEAD_PAD)


def _placement():
    pq = np.zeros((LANES, N_HEADS * HEAD_PAD), np.float32)
    pk = np.zeros((LANES, N_HEADS * HEAD_PAD), np.float32)
    cq = np.zeros((1, N_HEADS * HEAD_PAD), np.float32)
    ck = np.zeros((1, N_HEADS * HEAD_PAD), np.float32)
    for hd in range(N_HEADS):
        for k in range(3):
            pq[k * N_HEADS + hd, hd * HEAD_PAD + AUG0 + k] = 1.0
            ck[0, hd * HEAD_PAD + AUG0 + k] = 1.0
            pk[k * N_HEADS + hd, hd * HEAD_PAD + AUG0 + 3 + k] = -1.0
            cq[0, hd * HEAD_PAD + AUG0 + 3 + k] = 1.0
    return (jnp.asarray(pq, BF16), jnp.asarray(pk, BF16), jnp.asarray(cq), jnp.asarray(ck))


def kernel(x, c, w_ada, b_ada, norm1_g, w_in, w_dw, b_dw, conv_gn_g, conv_gn_b, w_conv_out,
           q_norm_g, k_norm_g, b_forget, w_attn_out, w_out, norm2_g, w_router, router_bias,
           w_experts_gate_up, w_experts_down, w_shared_gate_up, w_shared_down):
    depth = w_ada.shape[0]
    b, s, d = x.shape
    off_q = 2 * CONV_DIM
    off_f = off_q + 3 * ATTN_DIM
    off_gc = off_f + N_HEADS
    off_ga = off_gc + d

    pq, pk, cq, ck = _placement()
    tri = jnp.asarray(np.tril(np.ones((ROW_TILE, ROW_TILE), np.float32)), BF16)
    grp = np.arange(CONV_DIM) // (CONV_DIM // CONV_GROUPS)
    gg = jnp.asarray((grp[:, None] == grp[None, :]).astype(np.float32) / (CONV_DIM // CONV_GROUPS), BF16)
    c_pad = jnp.pad(c, ((0, SUBLANES - b), (0, 0)))
    tri_pos = jnp.asarray(np.triu(np.ones((POS_TILE, POS_TILE), np.float32)), BF16)
    lstrict = jnp.asarray(np.tril(np.ones((N_EXPERTS, N_EXPERTS), np.float32), -1), BF16)

    for l in range(depth):
        mod = _ada(c_pad, w_ada[l], b_ada[l][None, :])[:b].reshape(b, 6, d)

        wi = w_in[l]
        bf = jnp.pad(b_forget[l][None, :], ((0, 0), (0, LANES - N_HEADS)))
        wf = jnp.pad(wi[:, off_f:off_gc], ((0, 0), (0, LANES - N_HEADS))).astype(BF16)
        gpad = lambda g, sc: jnp.tile(jnp.pad(g * sc, (0, HEAD_PAD - HEAD_DIM)), N_HEADS)[None, :]
        qscale = HEAD_DIM ** -0.5
        u, qa, ka, v, sgc, sga, cum = _inproj(
            x, mod, norm1_g[l][None, :],
            wi[:, :off_q].astype(BF16),
            _pad_heads(wi[:, off_q:off_q + ATTN_DIM]).astype(BF16),
            _pad_heads(wi[:, off_q + ATTN_DIM:off_q + 2 * ATTN_DIM]).astype(BF16),
            wi[:, off_q + 2 * ATTN_DIM:off_f].astype(BF16),
            wf, wi[:, off_gc:off_ga].astype(BF16), wi[:, off_ga:].astype(BF16),
            bf, gpad(q_norm_g[l], qscale), gpad(k_norm_g[l], 1.0),
            pq, pk, cq, ck, tri)

        flat = lambda a: a[:, :, :N_HEADS].transpose(0, 2, 1).reshape(-1)
        cs = flat(cum[:, 0::ATTN_TILE])
        ce = flat(cum[:, ATTN_TILE - 1::ATTN_TILE])
        bound = (1.02 * HEAD_DIM * qscale) * jnp.max(jnp.abs(q_norm_g[l])) * jnp.max(jnp.abs(k_norm_g[l]))
        o = _attention(cs, ce, bound.reshape(1), qa, ka, v)

        wdw = jnp.pad(w_dw[l], ((0, CONV_HALO - CONV_WIDTH), (0, 0)))
        x1 = _merge(u, o, sgc, sga, x, mod, wdw, b_dw[l][None, :], conv_gn_g[l][None, :],
                    conv_gn_b[l][None, :], gg, w_conv_out[l].astype(BF16), w_attn_out[l].astype(BF16),
                    w_out[l].astype(BF16))

        wr = w_router[l].T
        wr_hi = wr.astype(BF16)
        wr_lo = (wr - wr_hi.astype(F32)).astype(BF16)
        h2, h2w, comb_t, cnt = _router(x1, mod, norm2_g[l][None, :], wr_hi, wr_lo, router_bias[l][:, None])

        t = b * s
        posk, wk, gend = _positions(comb_t, cnt, tri_pos, lstrict)
        n_tiles = (t * TOP_K) // EXPERT_TILE + N_EXPERTS
        seg_end = gend[:, 0].astype(jnp.int32)
        n_used = seg_end[-1:] // EXPERT_TILE
        tile_start = jnp.arange(n_tiles, dtype=jnp.int32) * EXPERT_TILE
        tile_start = jnp.minimum(tile_start, seg_end[-1] - EXPERT_TILE)
        tile_e = jnp.sum((seg_end[None, :] <= tile_start[:, None]).astype(jnp.int32), axis=1)

        xs = _sc_scatter_rows(h2w.reshape(t, d // 2), posk, n_tiles * EXPERT_TILE)
        ys = _experts(tile_e, n_used, xs, w_experts_gate_up[l], w_experts_down[l])
        yk = _sc_gather_rows(ys, posk.reshape(-1)).reshape(TOP_K, t, d // 2)
        out = _final(yk, wk.T, h2.reshape(t, d), x1.reshape(t, d), mod,
                     w_shared_gate_up[l].astype(BF16), w_shared_down[l].astype(BF16), s // ROW_TILE)
        x = out.reshape(b, s, d)
    return x
```

```python
import functools

import numpy as np
import jax
import jax.numpy as jnp
from jax import lax
from jax.experimental import pallas as pl
from jax.experimental.pallas import tpu as pltpu
from jax.experimental.pallas import tpu_sc as plsc

F32 = jnp.float32
BF16 = jnp.bfloat16

CONV_DIM = 512
CONV_WIDTH = 31
CONV_GROUPS = 8
N_HEADS = 8
HEAD_DIM = 64
ATTN_DIM = N_HEADS * HEAD_DIM
N_EXPERTS = 64
TOP_K = 8
N_GROUPS = 8
TOPK_GROUPS = 4
EXPERT_DIM = 256
SHARED_DIM = 256
ROUTED_SCALE = 2.5
EPS = 1e-6

LANES = 128
SUBLANES = 8
HEAD_PAD = LANES
AUG0 = HEAD_DIM
VMEM_LIMIT = 56 * 1024 * 1024

ROW_TILE = 512
ATTN_TILE = 1024
DIAG_PARTS = 2
EXPERT_TILE = 512
POS_TILE = 1024
SC_CHUNK = 128
CONV_HALO = 32

NEG_BIG = -1e30
EXP_UNDERFLOW = 104.0
FIXED_SHIFT_BOUND = 40.0


def _dot(a, b):
    return jnp.dot(a, b, preferred_element_type=F32)


def _dot_nt(a, b):
    return lax.dot_general(a, b, (((1,), (1,)), ((), ())), preferred_element_type=F32)


def _split2(x):
    hi = x.astype(BF16)
    lo = (x - hi.astype(F32)).astype(BF16)
    return hi, lo


def _split3(x):
    hi = x.astype(BF16)
    r = x - hi.astype(F32)
    mid = r.astype(BF16)
    lo = (r - mid.astype(F32)).astype(BF16)
    return hi, mid, lo


def _pack_halves(v):
    n = v.shape[1] // 2
    lo = lax.bitcast_convert_type(v[:, :n].astype(BF16).astype(F32), jnp.uint32)
    hi = lax.bitcast_convert_type(v[:, n:].astype(BF16).astype(F32), jnp.uint32)
    return lax.bitcast_convert_type(hi | lax.shift_right_logical(lo, jnp.uint32(16)), jnp.int32)


def _unpack_halves(w):
    u = lax.bitcast_convert_type(w, jnp.uint32)
    lo = lax.bitcast_convert_type(lax.shift_left(u, jnp.uint32(16)), F32)
    hi = lax.bitcast_convert_type(u & jnp.uint32(0xFFFF0000), F32)
    return lo, hi


def _sigmoid(x):
    return 1.0 / (1.0 + jnp.exp(-x))


def _silu(x):
    return x * _sigmoid(x)


def _params(*sem):
    return pltpu.CompilerParams(dimension_semantics=sem, vmem_limit_bytes=VMEM_LIMIT)


def _const_spec(shape):
    n = len(shape)
    return pl.BlockSpec(shape, lambda *_: (0,) * n, pipeline_mode=pl.Buffered(1))


def _ada_kernel(c_ref, w_ref, b_ref, o_ref):
    c = c_ref[...]
    a_hi, a_lo = _split2(_silu(c))
    w_hi, w_lo = _split2(w_ref[...])
    o_ref[...] = _dot(a_hi, w_hi) + _dot(a_hi, w_lo) + _dot(a_lo, w_hi) + b_ref[...]


def _ada(c_pad, w_ada, b_ada):
    d = c_pad.shape[1]
    n = w_ada.shape[1]
    return pl.pallas_call(
        _ada_kernel,
        grid=(n // d,),
        in_specs=[_const_spec(c_pad.shape),
                  pl.BlockSpec((d, d), lambda j: (0, j)),
                  pl.BlockSpec((1, d), lambda j: (0, j))],
        out_specs=pl.BlockSpec((c_pad.shape[0], d), lambda j: (0, j)),
        out_shape=jax.ShapeDtypeStruct((c_pad.shape[0], n), F32),
        compiler_params=_params("arbitrary"),
        name="ada",
    )(c_pad, w_ada, b_ada)


def _lane_pieces(x):
    hi, mid, lo = _split3(x)
    return (hi.astype(F32) + pltpu.roll(mid.astype(F32), N_HEADS, 1)
            + pltpu.roll(lo.astype(F32), 2 * N_HEADS, 1)).astype(BF16)


def _head_tile(p, col0, hd):
    per = HEAD_PAD // HEAD_DIM
    g, part = divmod(hd, per)
    blk = p[:, col0 + g * HEAD_PAD:col0 + (g + 1) * HEAD_PAD]
    if part:
        blk = pltpu.roll(blk, HEAD_PAD - part * HEAD_DIM, 1)
    lane = lax.broadcasted_iota(jnp.int32, (1, HEAD_PAD), 1)
    return jnp.where(lane < HEAD_DIM, blk, 0.0)


def _inproj_kernel(x_ref, mod_ref, g1_ref, wcv_ref, wq_ref, wk_ref, wv_ref, wf_ref, wgc_ref,
                   wga_ref, bf_ref, qg_ref, kg_ref, pq_ref, pk_ref, cq_ref, ck_ref, tri_ref,
                   u_ref, qa_ref, ka_ref, v_ref, sgc_ref, sga_ref, cum_ref, carry_ref):
    @pl.when(pl.program_id(1) == 0)
    def _():
        carry_ref[...] = jnp.zeros_like(carry_ref)

    x = x_ref[0]
    ms = jnp.mean(x * x, axis=-1, keepdims=True)
    h = (x * lax.rsqrt(ms + EPS) * g1_ref[...]) * (1.0 + mod_ref[0, 1:2, :]) + mod_ref[0, 0:1, :]
    hb = h.astype(BF16)

    pc = _dot(hb, wcv_ref[...])
    u_ref[0] = (pc[:, :CONV_DIM] * _sigmoid(pc[:, CONV_DIM:])).astype(BF16)

    z = _dot(hb, wf_ref[...]) + bf_ref[...]
    lf = jnp.minimum(z, 0.0) - jnp.log1p(jnp.exp(-jnp.abs(z)))
    lane = lax.broadcasted_iota(jnp.int32, (1, LANES), 1)
    psum = _dot(tri_ref[...], _lane_pieces(jnp.where(lane < N_HEADS, lf, 0.0)))
    cum = (psum + pltpu.roll(psum, LANES - N_HEADS, 1) + pltpu.roll(psum, LANES - 2 * N_HEADS, 1)
           + carry_ref[...])
    cum = jnp.where(lane < N_HEADS, cum, 0.0)
    carry_ref[...] = cum[ROW_TILE - 1:ROW_TILE, :]
    cum_ref[0] = cum

    pieces = _lane_pieces(cum)
    addq = _dot(pieces, pq_ref[...]) + cq_ref[...]
    addk = _dot(pieces, pk_ref[...]) + ck_ref[...]

    pq = _dot(hb, wq_ref[...])
    pk = _dot(hb, wk_ref[...])
    pv = _dot(hb, wv_ref[...])
    inv_hd = 1.0 / HEAD_DIM
    vone = (lax.broadcasted_iota(jnp.int32, (1, HEAD_PAD), 1) == HEAD_DIM).astype(F32)
    for hd in range(N_HEADS):
        sl = slice(hd * HEAD_PAD, (hd + 1) * HEAD_PAD)
        qb = pq[:, sl]
        qn = qb * lax.rsqrt(jnp.sum(qb * qb, axis=-1, keepdims=True) * inv_hd + EPS) * qg_ref[:, sl]
        qa_ref[0, hd] = (qn + addq[:, sl]).astype(BF16)
        kb = pk[:, sl]
        kn = kb * lax.rsqrt(jnp.sum(kb * kb, axis=-1, keepdims=True) * inv_hd + EPS) * kg_ref[:, sl]
        ka_ref[0, hd] = (kn + addk[:, sl]).astype(BF16)
        v_ref[0, hd] = (_head_tile(pv, 0, hd) + vone).astype(BF16)

    sgc_ref[0] = _sigmoid(_dot(hb, wgc_ref[...])).astype(BF16)
    sga_ref[0] = _sigmoid(_dot(hb, wga_ref[...])).astype(BF16)


def _inproj(x, mod, g1, wcv, wq, wk, wv, wf, wgc, wga, bf, qg, kg, pq, pk, cq, ck, tri):
    b, s, d = x.shape
    tm = ROW_TILE
    row = lambda w: pl.BlockSpec((1, tm, w), lambda bi, i: (bi, i, 0))
    head = pl.BlockSpec((1, N_HEADS, tm, HEAD_PAD), lambda bi, i: (bi, 0, i, 0))
    consts = [g1, wcv, wq, wk, wv, wf, wgc, wga, bf, qg, kg, pq, pk, cq, ck, tri]
    return pl.pallas_call(
        _inproj_kernel,
        grid=(b, s // tm),
        in_specs=[row(d), pl.BlockSpec((1, 6, d), lambda bi, i: (bi, 0, 0))]
                 + [_const_spec(a.shape) for a in consts],
        out_specs=[row(CONV_DIM), head, head, head, row(d), row(d), row(LANES)],
        out_shape=[jax.ShapeDtypeStruct((b, s, CONV_DIM), BF16),
                   jax.ShapeDtypeStruct((b, N_HEADS, s, HEAD_PAD), BF16),
                   jax.ShapeDtypeStruct((b, N_HEADS, s, HEAD_PAD), BF16),
                   jax.ShapeDtypeStruct((b, N_HEADS, s, HEAD_PAD), BF16),
                   jax.ShapeDtypeStruct((b, s, d), BF16),
                   jax.ShapeDtypeStruct((b, s, d), BF16),
                   jax.ShapeDtypeStruct((b, s, LANES), F32)],
        scratch_shapes=[pltpu.VMEM((1, LANES), F32)],
        compiler_params=_params("arbitrary", "arbitrary"),
        name="inproj",
    )(x, mod, *consts)


def _attn_kernel(cs_ref, ce_ref, prm_ref, q_ref, k_ref, v_ref, o_ref, acc_ref, m_ref):
    t = ATTN_TILE
    nb = q_ref.shape[2] // t
    base = (pl.program_id(0) * pl.num_programs(1) + pl.program_id(1)) * nb
    bound = prm_ref[0]
    thresh = -(EXP_UNDERFLOW + 2.0 * bound)
    causal = lax.broadcasted_iota(jnp.int32, (t, t), 0) >= lax.broadcasted_iota(jnp.int32, (t, t), 1)

    def first_block(i):
        c0 = cs_ref[base + i]
        return lax.fori_loop(0, i, lambda j, n: n + (c0 - ce_ref[base + j] < thresh).astype(jnp.int32), 0)

    def scores(q, j, masked):
        k0 = pl.multiple_of(j * t, t)
        s = _dot_nt(q, k_ref[0, 0, pl.ds(k0, t), :])
        if masked:
            s = jnp.where(causal, s, NEG_BIG)
        return s, v_ref[0, 0, pl.ds(k0, t), :]

    def finish(q0):
        acc = acc_ref[...]
        o_ref[0, pl.ds(q0, t), :] = (acc / acc[:, HEAD_DIM:HEAD_DIM + 1]).astype(BF16)

    def fixed_shift(i, carry):
        q0 = pl.multiple_of(i * t, t)
        q = q_ref[0, 0, pl.ds(q0, t), :]

        acc_ref[...] = jnp.zeros_like(acc_ref)

        def weighted(j, masked):
            s, vb = scores(q, j, masked)
            return _dot(jnp.exp(s).astype(BF16), vb)

        def kv(j, c):
            acc_ref[...] += weighted(j, False)
            return c

        lax.fori_loop(first_block(i), i, kv, 0)
        rp = t // DIAG_PARTS
        for r in range(DIAG_PARTS):
            nk = (r + 1) * rp
            s = _dot_nt(q[r * rp:(r + 1) * rp, :], k_ref[0, 0, pl.ds(q0, nk), :])
            seen = (lax.broadcasted_iota(jnp.int32, (rp, nk), 0) + r * rp
                    >= lax.broadcasted_iota(jnp.int32, (rp, nk), 1))
            p = jnp.exp(jnp.where(seen, s, NEG_BIG)).astype(BF16)
            acc_ref[r * rp:(r + 1) * rp, :] += _dot(p, v_ref[0, 0, pl.ds(q0, nk), :])
        finish(q0)
        return carry

    def running_max(i, carry):
        q0 = pl.multiple_of(i * t, t)
        q = q_ref[0, 0, pl.ds(q0, t), :]
        m_ref[...] = jnp.full_like(m_ref, -jnp.inf)
        acc_ref[...] = jnp.zeros_like(acc_ref)

        def step(j, masked):
            s, vb = scores(q, j, masked)
            m_prev = m_ref[...]
            m_new = jnp.maximum(m_prev, jnp.max(s, axis=-1, keepdims=True))
            p = jnp.exp(s - m_new)
            acc_ref[...] = jnp.exp(m_prev - m_new) * acc_ref[...] + _dot(p.astype(BF16), vb)
            m_ref[...] = m_new

        def kv(j, c):
            step(j, False)
            return c

        lax.fori_loop(first_block(i), i, kv, 0)
        step(i, True)
        finish(q0)
        return carry

    @pl.when(bound <= FIXED_SHIFT_BOUND)
    def _():
        lax.fori_loop(0, nb, fixed_shift, 0)

    @pl.when(bound > FIXED_SHIFT_BOUND)
    def _():
        lax.fori_loop(0, nb, running_max, 0)


def _attention(cs, ce, prm, qa, ka, v):
    b, nh, s, hp = qa.shape
    t = ATTN_TILE
    seq = pl.BlockSpec((1, 1, s, hp), lambda bi, hi, *_: (bi, hi, 0, 0))
    return pl.pallas_call(
        _attn_kernel,
        grid_spec=pltpu.PrefetchScalarGridSpec(
            num_scalar_prefetch=3,
            grid=(b, nh),
            in_specs=[seq, seq, seq],
            out_specs=pl.BlockSpec((1, s, hp), lambda bi, hi, *_: (bi, 0, hi)),
            scratch_shapes=[pltpu.VMEM((t, hp), F32), pltpu.VMEM((t, 1), F32)]),
        out_shape=jax.ShapeDtypeStruct((b, s, nh * hp), BF16),
        compiler_params=_params("arbitrary", "arbitrary"),
        name="attn",
    )(cs, ce, prm, qa, ka, v)


def _merge_kernel(u_ref, halo_ref, o_ref, sgc_ref, sga_ref, x_ref, mod_ref, wdw_ref, bdw_ref,
                  gng_ref, gnb_ref, gg_ref, wco_ref, wao_ref, wout_ref, x1_ref, buf_ref):
    tm = ROW_TILE
    halo = halo_ref[0].astype(F32)
    halo = jnp.where(pl.program_id(1) == 0, jnp.zeros_like(halo), halo)
    ucur = u_ref[0].astype(F32)
    for cb in range(CONV_DIM // LANES):
        buf_ref[cb, 0:CONV_HALO, :] = halo[:, cb * LANES:(cb + 1) * LANES]
        buf_ref[cb, CONV_HALO:, :] = ucur[:, cb * LANES:(cb + 1) * LANES]

    base = CONV_HALO - (CONV_WIDTH - 1)
    ys = []
    for cb in range(CONV_DIM // LANES):
        acc = jnp.zeros((tm, LANES), F32)
        for j in range(CONV_WIDTH):
            acc = acc + wdw_ref[j:j + 1, cb * LANES:(cb + 1) * LANES] * buf_ref[cb, base + j:base + j + tm, :]
        ys.append(acc)
    y = jnp.concatenate(ys, axis=1) + bdw_ref[...]

    gg = gg_ref[...]
    y_hi, y_lo = _split2(y)
    dlt = y - (_dot(y_hi, gg) + _dot(y_lo, gg))
    s_hi, s_lo = _split2(dlt * dlt)
    var = _dot(s_hi, gg) + _dot(s_lo, gg)
    yn = dlt * lax.rsqrt(var + EPS) * gng_ref[...] + gnb_ref[...]
    y_conv = _dot(_silu(yn).astype(BF16), wco_ref[...])

    per = HEAD_PAD // HEAD_DIM
    lane = lax.broadcasted_iota(jnp.int32, (1, HEAD_PAD), 1)
    packed = []
    for g in range(N_HEADS // per):
        tile = o_ref[0, :, g * per * HEAD_PAD:(g * per + 1) * HEAD_PAD].astype(F32)
        for part in range(1, per):
            nxt = o_ref[0, :, (g * per + part) * HEAD_PAD:(g * per + part + 1) * HEAD_PAD].astype(F32)
            tile = jnp.where(lane < part * HEAD_DIM, tile, pltpu.roll(nxt, part * HEAD_DIM, 1))
        packed.append(tile)
    y_attn = _dot(jnp.concatenate(packed, axis=1).astype(BF16), wao_ref[...])
    merged = sgc_ref[0].astype(F32) * y_conv + sga_ref[0].astype(F32) * y_attn
    mix = _dot(merged.astype(BF16), wout_ref[...])
    x1_ref[0] = x_ref[0] + mod_ref[0, 2:3, :] * mix


def _merge(u, o, sgc, sga, x, mod, wdw, bdw, gng, gnb, gg, wco, wao, wout):
    b, s, d = x.shape
    tm = ROW_TILE
    per = tm // CONV_HALO
    row = lambda w: pl.BlockSpec((1, tm, w), lambda bi, i: (bi, i, 0))
    consts = [wdw, bdw, gng, gnb, gg, wco, wao, wout]
    return pl.pallas_call(
        _merge_kernel,
        grid=(b, s // tm),
        in_specs=[row(CONV_DIM),
                  pl.BlockSpec((1, CONV_HALO, CONV_DIM),
                               lambda bi, i: (bi, jnp.maximum(i * per - 1, 0), 0)),
                  row(o.shape[2]), row(d), row(d), row(d),
                  pl.BlockSpec((1, 6, d), lambda bi, i: (bi, 0, 0))]
                 + [_const_spec(a.shape) for a in consts],
        out_specs=row(d),
        out_shape=jax.ShapeDtypeStruct((b, s, d), F32),
        scratch_shapes=[pltpu.VMEM((CONV_DIM // LANES, CONV_HALO + tm, LANES), F32)],
        compiler_params=_params("arbitrary", "arbitrary"),
        name="merge",
    )(u, u, o, sgc, sga, x, mod, *consts)


def _router_kernel(x1_ref, mod_ref, g2_ref, wr_hi_ref, wr_lo_ref, rb_ref, h2_ref, h2w_ref, comb_ref, cnt_ref):
    x = x1_ref[0]
    ms = jnp.mean(x * x, axis=-1, keepdims=True)
    h = (x * lax.rsqrt(ms + EPS) * g2_ref[...]) * (1.0 + mod_ref[0, 4:5, :]) + mod_ref[0, 3:4, :]
    h2_ref[0] = h.astype(BF16)
    h2w_ref[0] = _pack_halves(h)

    h_hi, h_lo = _split2(h)
    logits = _dot_nt(wr_hi_ref[...], h_hi) + _dot_nt(wr_hi_ref[...], h_lo) + _dot_nt(wr_lo_ref[...], h_hi)
    scores = _sigmoid(logits)
    biased = scores + rb_ref[...]

    per = N_EXPERTS // N_GROUPS
    rows = lax.broadcasted_iota(jnp.int32, (per, biased.shape[1]), 0)
    gscore = []
    for g in range(N_GROUPS):
        blk = biased[g * per:(g + 1) * per, :]
        top1 = jnp.max(blk, axis=0, keepdims=True)
        first = jnp.min(jnp.where(blk == top1, rows, per), axis=0, keepdims=True)
        top2 = jnp.max(jnp.where(rows == first, -jnp.inf, blk), axis=0, keepdims=True)
        gscore.append(top1 + top2)

    cand = []
    for g in range(N_GROUPS):
        rank = jnp.zeros_like(gscore[g], dtype=jnp.int32)
        for g2 in range(N_GROUPS):
            if g2 == g:
                continue
            ahead = gscore[g2] > gscore[g]
            if g2 < g:
                ahead = ahead | (gscore[g2] == gscore[g])
            rank = rank + ahead.astype(jnp.int32)
        keep = rank < TOPK_GROUPS
        cand.append(jnp.where(keep, biased[g * per:(g + 1) * per, :], -jnp.inf))
    cand = jnp.concatenate(cand, axis=0)

    eidx = lax.broadcasted_iota(jnp.int32, cand.shape, 0)
    work = cand
    for _ in range(TOP_K):
        top = jnp.max(work, axis=0, keepdims=True)
        first = jnp.min(jnp.where(work == top, eidx, N_EXPERTS), axis=0, keepdims=True)
        work = jnp.where(eidx == first, -jnp.inf, work)
    sel = (work != cand) & (cand > -jnp.inf)
    w = jnp.where(sel, scores, 0.0)
    comb = w / jnp.sum(w, axis=0, keepdims=True) * ROUTED_SCALE
    comb_ref[...] = comb

    @pl.when((pl.program_id(0) == 0) & (pl.program_id(1) == 0))
    def _():
        cnt_ref[...] = jnp.zeros_like(cnt_ref)

    hit = jnp.where(comb != 0.0, 1.0, 0.0).astype(BF16)
    cnt_ref[...] += _dot(hit, jnp.ones((hit.shape[1], LANES), BF16))


def _router(x1, mod, g2, wr_hi, wr_lo, rb):
    b, s, d = x1.shape
    tm = ROW_TILE
    nt = s // tm
    return pl.pallas_call(
        _router_kernel,
        grid=(b, nt),
        in_specs=[pl.BlockSpec((1, tm, d), lambda bi, i: (bi, i, 0)),
                  pl.BlockSpec((1, 6, d), lambda bi, i: (bi, 0, 0)),
                  _const_spec(g2.shape), _const_spec(wr_hi.shape), _const_spec(wr_lo.shape),
                  _const_spec(rb.shape)],
        out_specs=[pl.BlockSpec((1, tm, d), lambda bi, i: (bi, i, 0)),
                   pl.BlockSpec((1, tm, d // 2), lambda bi, i: (bi, i, 0)),
                   pl.BlockSpec((N_EXPERTS, tm), lambda bi, i: (0, bi * nt + i)),
                   pl.BlockSpec((N_EXPERTS, LANES), lambda bi, i: (0, 0))],
        out_shape=[jax.ShapeDtypeStruct((b, s, d), BF16),
                   jax.ShapeDtypeStruct((b, s, d // 2), jnp.int32),
                   jax.ShapeDtypeStruct((N_EXPERTS, b * s), F32),
                   jax.ShapeDtypeStruct((N_EXPERTS, LANES), F32)],
        compiler_params=_params("arbitrary", "arbitrary"),
        name="router",
    )(x1, mod, g2, wr_hi, wr_lo, rb)


def _pos_kernel(comb_ref, cnt_ref, tri_ref, lstrict_ref, posk_ref, wk_ref, gend_ref, base_ref, *, spare_row):
    tp = comb_ref.shape[1]
    comb = comb_ref[...]
    sel = comb != 0.0
    selb = jnp.where(sel, 1.0, 0.0).astype(BF16)

    @pl.when(pl.program_id(0) == 0)
    def _():
        seg = jnp.floor((cnt_ref[...] + (EXPERT_TILE - 1.0)) * (1.0 / EXPERT_TILE)) * EXPERT_TILE
        s_hi, s_mid, s_lo = _split3(seg)
        ls = lstrict_ref[...]
        start = _dot(ls, s_hi) + _dot(ls, s_mid) + _dot(ls, s_lo)
        base_ref[...] = start
        gend_ref[...] = start + seg

    rank = _dot(selb, tri_ref[...])
    pos = base_ref[:, 0:1] + rank - 1.0
    base_ref[...] += _dot(selb, jnp.ones((tp, LANES), BF16))
    slot = _dot(lstrict_ref[...], selb)
    rows_p, rows_w = [], []
    for k in range(TOP_K):
        m = sel & (slot == k)
        rows_p.append(jnp.sum(jnp.where(m, pos - spare_row, 0.0), axis=0, keepdims=True) + spare_row)
        rows_w.append(jnp.sum(jnp.where(m, comb, 0.0), axis=0, keepdims=True))
    posk_ref[...] = jnp.concatenate(rows_p, axis=0).astype(jnp.int32)
    wk_ref[...] = jnp.concatenate(rows_w, axis=0)


def _positions(comb_t, cnt, tri, lstrict, n_rows):
    ne, t = comb_t.shape
    tp = POS_TILE
    tok = lambda rows: pl.BlockSpec((rows, tp), lambda i: (0, i))
    return pl.pallas_call(
        functools.partial(_pos_kernel, spare_row=float(n_rows - 1)),
        grid=(t // tp,),
        in_specs=[tok(ne), _const_spec(cnt.shape), _const_spec(tri.shape), _const_spec(lstrict.shape)],
        out_specs=[tok(TOP_K), tok(TOP_K), pl.BlockSpec((ne, LANES), lambda i: (0, 0))],
        out_shape=[jax.ShapeDtypeStruct((TOP_K, t), jnp.int32),
                   jax.ShapeDtypeStruct((TOP_K, t), F32),
                   jax.ShapeDtypeStruct((ne, LANES), F32)],
        scratch_shapes=[pltpu.VMEM((ne, LANES), F32)],
        compiler_params=_params("arbitrary"),
        name="positions",
    )(comb_t, cnt, tri, lstrict)


def _sc_workers():
    info = plsc.get_sparse_core_info()
    return info.num_cores, info.num_cores * info.num_subcores


def _sc_scatter_rows(rows, pos, n_out):
    nc, nw = _sc_workers()
    n, w = rows.shape
    nk = pos.shape[0]
    ch = SC_CHUNK
    per_w = n // nw
    assert per_w * nw == n and per_w % ch == 0

    @functools.partial(
        pl.kernel, mesh=plsc.VectorSubcoreMesh(core_axis_name="c", subcore_axis_name="s"),
        out_type=jax.ShapeDtypeStruct((n_out, w), rows.dtype),
        scratch_types=[pltpu.VMEM((nk, ch), jnp.int32), pltpu.VMEM((ch, w), rows.dtype),
                       pltpu.SemaphoreType.DMA])
    def scatter(rows_hbm, pos_hbm, out_hbm, idx_v, rows_v, sem):
        base = (lax.axis_index("s") * nc + lax.axis_index("c")) * per_w

        @pl.loop(0, per_w // ch)
        def _(ci):
            off = pl.multiple_of(base + ci * ch, ch)
            pltpu.sync_copy(pos_hbm.at[:, pl.ds(off, ch)], idx_v)
            pltpu.sync_copy(rows_hbm.at[pl.ds(off, ch)], rows_v)
            copies = [pltpu.make_async_copy(rows_v, out_hbm.at[idx_v.at[k]], sem) for k in range(nk)]
            for cp in copies:
                cp.start()
            for cp in copies:
                cp.wait()

    return scatter(rows, pos)


def _sc_gather_rows(table, idx):
    nc, nw = _sc_workers()
    n = idx.shape[0]
    w = table.shape[1]
    ch = SC_CHUNK
    per_w = n // nw
    assert per_w * nw == n and per_w % ch == 0

    @functools.partial(
        pl.kernel, mesh=plsc.VectorSubcoreMesh(core_axis_name="c", subcore_axis_name="s"),
        out_type=jax.ShapeDtypeStruct((n, w), table.dtype),
        scratch_types=[pltpu.VMEM((ch,), jnp.int32), pltpu.VMEM((ch, w), table.dtype),
                       pltpu.SemaphoreType.DMA])
    def gather(table_hbm, idx_hbm, out_hbm, idx_v, rows_v, sem):
        base = (lax.axis_index("s") * nc + lax.axis_index("c")) * per_w

        @pl.loop(0, per_w // ch)
        def _(ci):
            off = pl.multiple_of(base + ci * ch, ch)
            pltpu.sync_copy(idx_hbm.at[pl.ds(off, ch)], idx_v)
            cp = pltpu.make_async_copy(table_hbm.at[idx_v], rows_v, sem)
            cp.start()
            cp.wait()
            pltpu.sync_copy(rows_v, out_hbm.at[pl.ds(off, ch)])

    return gather(table, idx)


def _expert_kernel(te_ref, nu_ref, xs_ref, wgu_ref, wd_ref, y_ref, wgu_b, wd_b):
    i = pl.program_id(0)

    @pl.when(i < nu_ref[0])
    def _():
        @pl.when((i == 0) | (te_ref[i] != te_ref[jnp.maximum(i - 1, 0)]))
        def _():
            wgu_b[...] = wgu_ref[0].astype(BF16)
            wd_b[...] = wd_ref[0].astype(BF16)

        lo, hi = _unpack_halves(xs_ref[...])
        half = wgu_b.shape[0] // 2
        gu = _dot(lo.astype(BF16), wgu_b[:half, :]) + _dot(hi.astype(BF16), wgu_b[half:, :])
        act = _silu(gu[:, :EXPERT_DIM]) * gu[:, EXPERT_DIM:]
        y_ref[...] = _pack_halves(_dot(act.astype(BF16), wd_b[...]))


def _experts(tile_e, n_used, xs, wgu, wd):
    r, w = xs.shape
    tm = EXPERT_TILE
    d = wgu.shape[1]
    rows = pl.BlockSpec((tm, w), lambda i, te, nu: (jnp.minimum(i, nu[0] - 1), 0))
    return pl.pallas_call(
        _expert_kernel,
        grid_spec=pltpu.PrefetchScalarGridSpec(
            num_scalar_prefetch=2,
            grid=(r // tm,),
            in_specs=[rows,
                      pl.BlockSpec((1, d, 2 * EXPERT_DIM), lambda i, te, nu: (te[i], 0, 0)),
                      pl.BlockSpec((1, EXPERT_DIM, d), lambda i, te, nu: (te[i], 0, 0))],
            out_specs=rows,
            scratch_shapes=[pltpu.VMEM((d, 2 * EXPERT_DIM), BF16), pltpu.VMEM((EXPERT_DIM, d), BF16)]),
        out_shape=jax.ShapeDtypeStruct((r, w), jnp.int32),
        compiler_params=_params("arbitrary"),
        name="experts",
    )(tile_e, n_used, xs, wgu, wd)


def _final_kernel(yk_ref, wk_ref, h_ref, x1_ref, mod_ref, wsgu_ref, wsd_ref, out_ref):
    half = yk_ref.shape[2]
    acc_lo = jnp.zeros((yk_ref.shape[1], half), F32)
    acc_hi = jnp.zeros((yk_ref.shape[1], half), F32)
    for k in range(TOP_K):
        lo, hi = _unpack_halves(yk_ref[k])
        w = wk_ref[:, k:k + 1]
        acc_lo = acc_lo + jnp.where(w != 0.0, w * lo, 0.0)
        acc_hi = acc_hi + jnp.where(w != 0.0, w * hi, 0.0)
    routed = jnp.concatenate([acc_lo, acc_hi], axis=1)
    sgu = _dot(h_ref[...], wsgu_ref[...])
    act = _silu(sgu[:, :SHARED_DIM]) * sgu[:, SHARED_DIM:]
    shared = _dot(act.astype(BF16), wsd_ref[...])
    out_ref[...] = x1_ref[...] + mod_ref[0, 5:6, :] * (routed + shared)


def _final(yk, wk_t, h2, x1, mod, wsgu, wsd, tiles_per_batch):
    t, d = h2.shape
    tm = ROW_TILE
    row = lambda w: pl.BlockSpec((tm, w), lambda i: (i, 0))
    return pl.pallas_call(
        _final_kernel,
        grid=(t // tm,),
        in_specs=[pl.BlockSpec((TOP_K, tm, d // 2), lambda i: (0, i, 0)),
                  row(TOP_K), row(d), row(d),
                  pl.BlockSpec((1, 6, d), lambda i: (i // tiles_per_batch, 0, 0)),
                  _const_spec(wsgu.shape), _const_spec(wsd.shape)],
        out_specs=row(d),
        out_shape=jax.ShapeDtypeStruct((t, d), F32),
        compiler_params=_params("arbitrary"),
        name="final",
    )(yk, wk_t, h2, x1, mod, wsgu, wsd)


def _pad_heads(w):
    d = w.shape[0]
    w = w.reshape(d, N_HEADS, HEAD_DIM)
    return jnp.pad(w, ((0, 0), (0, 0), (0, HEAD_PAD - HEAD_DIM))).reshape(d, N_HEADS * HEAD_PAD)


def _placement():
    pq = np.zeros((LANES, N_HEADS * HEAD_PAD), np.float32)
    pk = np.zeros((LANES, N_HEADS * HEAD_PAD), np.float32)
    cq = np.zeros((1, N_HEADS * HEAD_PAD), np.float32)
    ck = np.zeros((1, N_HEADS * HEAD_PAD), np.float32)
    for hd in range(N_HEADS):
        for k in range(3):
            pq[k * N_HEADS + hd, hd * HEAD_PAD + AUG0 + k] = 1.0
            ck[0, hd * HEAD_PAD + AUG0 + k] = 1.0
            pk[k * N_HEADS + hd, hd * HEAD_PAD + AUG0 + 3 + k] = -1.0
            cq[0, hd * HEAD_PAD + AUG0 + 3 + k] = 1.0
    return (jnp.asarray(pq, BF16), jnp.asarray(pk, BF16), jnp.asarray(cq), jnp.asarray(ck))


def kernel(x, c, w_ada, b_ada, norm1_g, w_in, w_dw, b_dw, conv_gn_g, conv_gn_b, w_conv_out,
           q_norm_g, k_norm_g, b_forget, w_attn_out, w_out, norm2_g, w_router, router_bias,
           w_experts_gate_up, w_experts_down, w_shared_gate_up, w_shared_down):
    depth = w_ada.shape[0]
    b, s, d = x.shape
    off_q = 2 * CONV_DIM
    off_f = off_q + 3 * ATTN_DIM
    off_gc = off_f + N_HEADS
    off_ga = off_gc + d

    pq, pk, cq, ck = _placement()
    tri = jnp.asarray(np.tril(np.ones((ROW_TILE, ROW_TILE), np.float32)), BF16)
    grp = np.arange(CONV_DIM) // (CONV_DIM // CONV_GROUPS)
    gg = jnp.asarray((grp[:, None] == grp[None, :]).astype(np.float32) / (CONV_DIM // CONV_GROUPS), BF16)
    c_pad = jnp.pad(c, ((0, SUBLANES - b), (0, 0)))
    tri_pos = jnp.asarray(np.triu(np.ones((POS_TILE, POS_TILE), np.float32)), BF16)
    lstrict = jnp.asarray(np.tril(np.ones((N_EXPERTS, N_EXPERTS), np.float32), -1), BF16)

    for l in range(depth):
        mod = _ada(c_pad, w_ada[l], b_ada[l][None, :])[:b].reshape(b, 6, d)

        wi = w_in[l]
        bf = jnp.pad(b_forget[l][None, :], ((0, 0), (0, LANES - N_HEADS)))
        wf = jnp.pad(wi[:, off_f:off_gc], ((0, 0), (0, LANES - N_HEADS))).astype(BF16)
        gpad = lambda g, sc: jnp.tile(jnp.pad(g * sc, (0, HEAD_PAD - HEAD_DIM)), N_HEADS)[None, :]
        qscale = HEAD_DIM ** -0.5
        u, qa, ka, v, sgc, sga, cum = _inproj(
            x, mod, norm1_g[l][None, :],
            wi[:, :off_q].astype(BF16),
            _pad_heads(wi[:, off_q:off_q + ATTN_DIM]).astype(BF16),
            _pad_heads(wi[:, off_q + ATTN_DIM:off_q + 2 * ATTN_DIM]).astype(BF16),
            wi[:, off_q + 2 * ATTN_DIM:off_f].astype(BF16),
            wf, wi[:, off_gc:off_ga].astype(BF16), wi[:, off_ga:].astype(BF16),
            bf, gpad(q_norm_g[l], qscale), gpad(k_norm_g[l], 1.0),
            pq, pk, cq, ck, tri)

        flat = lambda a: a[:, :, :N_HEADS].transpose(0, 2, 1).reshape(-1)
        cs = flat(cum[:, 0::ATTN_TILE])
        ce = flat(cum[:, ATTN_TILE - 1::ATTN_TILE])
        bound = (1.02 * HEAD_DIM * qscale) * jnp.max(jnp.abs(q_norm_g[l])) * jnp.max(jnp.abs(k_norm_g[l]))
        o = _attention(cs, ce, bound.reshape(1), qa, ka, v)

        wdw = jnp.pad(w_dw[l], ((0, CONV_HALO - CONV_WIDTH), (0, 0)))
        x1 = _merge(u, o, sgc, sga, x, mod, wdw, b_dw[l][None, :], conv_gn_g[l][None, :],
                    conv_gn_b[l][None, :], gg, w_conv_out[l].astype(BF16), w_attn_out[l].astype(BF16),
                    w_out[l].astype(BF16))

        wr = w_router[l].T
        wr_hi = wr.astype(BF16)
        wr_lo = (wr - wr_hi.astype(F32)).astype(BF16)
        h2, h2w, comb_t, cnt = _router(x1, mod, norm2_g[l][None, :], wr_hi, wr_lo, router_bias[l][:, None])

        t = b * s
        n_tiles = (t * TOP_K) // EXPERT_TILE + N_EXPERTS
        posk, wk, gend = _positions(comb_t, cnt, tri_pos, lstrict, n_tiles * EXPERT_TILE)
        seg_end = gend[:, 0].astype(jnp.int32)
        n_used = seg_end[-1:] // EXPERT_TILE
        tile_start = jnp.arange(n_tiles, dtype=jnp.int32) * EXPERT_TILE
        tile_start = jnp.minimum(tile_start, seg_end[-1] - EXPERT_TILE)
        tile_e = jnp.sum((seg_end[None, :] <= tile_start[:, None]).astype(jnp.int32), axis=1)

        xs = _sc_scatter_rows(h2w.reshape(t, d // 2), posk, n_tiles * EXPERT_TILE)
        ys = _experts(tile_e, n_used, xs, w_experts_gate_up[l], w_experts_down[l])
        yk = _sc_gather_rows(ys, posk.reshape(-1)).reshape(TOP_K, t, d // 2)
        out = _final(yk, wk.T, h2.reshape(t, d), x1.reshape(t, d), mod,
                     w_shared_gate_up[l].astype(BF16), w_shared_down[l].astype(BF16), s // ROW_TILE)
        x = out.reshape(b, s, d)
    return x
```

```python
import functools

import numpy as np
import jax
import jax.numpy as jnp
from jax import lax
from jax.experimental import pallas as pl
from jax.experimental.pallas import tpu as pltpu
from jax.experimental.pallas import tpu_sc as plsc

F32 = jnp.float32
BF16 = jnp.bfloat16

CONV_DIM = 512
CONV_WIDTH = 31
CONV_GROUPS = 8
N_HEADS = 8
HEAD_DIM = 64
ATTN_DIM = N_HEADS * HEAD_DIM
N_EXPERTS = 64
TOP_K = 8
N_GROUPS = 8
TOPK_GROUPS = 4
EXPERT_DIM = 256
SHARED_DIM = 256
ROUTED_SCALE = 2.5
EPS = 1e-6

LANES = 128
SUBLANES = 8
HEAD_PAD = LANES
AUG0 = HEAD_DIM
VMEM_LIMIT = 56 * 1024 * 1024

ROW_TILE = 512
ATTN_TILE = 1024
DIAG_PARTS = 2
EXPERT_TILE = 512
POS_TILE = 1024
SC_CHUNK = 128
CONV_HALO = 32

NEG_BIG = -1e30
EXP_UNDERFLOW = 104.0
FIXED_SHIFT_BOUND = 40.0


def _dot(a, b):
    return jnp.dot(a, b, preferred_element_type=F32)


def _dot_nt(a, b):
    return lax.dot_general(a, b, (((1,), (1,)), ((), ())), preferred_element_type=F32)


def _split2(x):
    hi = x.astype(BF16)
    lo = (x - hi.astype(F32)).astype(BF16)
    return hi, lo


def _split3(x):
    hi = x.astype(BF16)
    r = x - hi.astype(F32)
    mid = r.astype(BF16)
    lo = (r - mid.astype(F32)).astype(BF16)
    return hi, mid, lo


def _pack_halves(v):
    n = v.shape[1] // 2
    lo = lax.bitcast_convert_type(v[:, :n].astype(BF16).astype(F32), jnp.uint32)
    hi = lax.bitcast_convert_type(v[:, n:].astype(BF16).astype(F32), jnp.uint32)
    return lax.bitcast_convert_type(hi | lax.shift_right_logical(lo, jnp.uint32(16)), jnp.int32)


def _unpack_halves(w):
    u = lax.bitcast_convert_type(w, jnp.uint32)
    lo = lax.bitcast_convert_type(lax.shift_left(u, jnp.uint32(16)), F32)
    hi = lax.bitcast_convert_type(u & jnp.uint32(0xFFFF0000), F32)
    return lo, hi


def _sigmoid(x):
    return 1.0 / (1.0 + jnp.exp(-x))


def _silu(x):
    return x * _sigmoid(x)


def _params(*sem):
    return pltpu.CompilerParams(dimension_semantics=sem, vmem_limit_bytes=VMEM_LIMIT)


def _const_spec(shape):
    n = len(shape)
    return pl.BlockSpec(shape, lambda *_: (0,) * n, pipeline_mode=pl.Buffered(1))


def _ada_kernel(c_ref, w_ref, b_ref, o_ref):
    c = c_ref[...]
    a_hi, a_lo = _split2(_silu(c))
    w_hi, w_lo = _split2(w_ref[...])
    o_ref[...] = _dot(a_hi, w_hi) + _dot(a_hi, w_lo) + _dot(a_lo, w_hi) + b_ref[...]


def _ada(c_pad, w_ada, b_ada):
    d = c_pad.shape[1]
    n = w_ada.shape[1]
    return pl.pallas_call(
        _ada_kernel,
        grid=(n // d,),
        in_specs=[_const_spec(c_pad.shape),
                  pl.BlockSpec((d, d), lambda j: (0, j)),
                  pl.BlockSpec((1, d), lambda j: (0, j))],
        out_specs=pl.BlockSpec((c_pad.shape[0], d), lambda j: (0, j)),
        out_shape=jax.ShapeDtypeStruct((c_pad.shape[0], n), F32),
        compiler_params=_params("arbitrary"),
        name="ada",
    )(c_pad, w_ada, b_ada)


def _lane_pieces(x):
    hi, mid, lo = _split3(x)
    return (hi.astype(F32) + pltpu.roll(mid.astype(F32), N_HEADS, 1)
            + pltpu.roll(lo.astype(F32), 2 * N_HEADS, 1)).astype(BF16)


def _head_tile(p, col0, hd):
    per = HEAD_PAD // HEAD_DIM
    g, part = divmod(hd, per)
    blk = p[:, col0 + g * HEAD_PAD:col0 + (g + 1) * HEAD_PAD]
    if part:
        blk = pltpu.roll(blk, HEAD_PAD - part * HEAD_DIM, 1)
    lane = lax.broadcasted_iota(jnp.int32, (1, HEAD_PAD), 1)
    return jnp.where(lane < HEAD_DIM, blk, 0.0)


def _inproj_kernel(x_ref, mod_ref, g1_ref, wcv_ref, wq_ref, wk_ref, wv_ref, wf_ref, wgc_ref,
                   wga_ref, bf_ref, qg_ref, kg_ref, pq_ref, pk_ref, cq_ref, ck_ref, tri_ref,
                   u_ref, qa_ref, ka_ref, v_ref, sgc_ref, sga_ref, cum_ref, carry_ref):
    @pl.when(pl.program_id(1) == 0)
    def _():
        carry_ref[...] = jnp.zeros_like(carry_ref)

    x = x_ref[0]
    ms = jnp.mean(x * x, axis=-1, keepdims=True)
    h = (x * lax.rsqrt(ms + EPS) * g1_ref[...]) * (1.0 + mod_ref[0, 1:2, :]) + mod_ref[0, 0:1, :]
    hb = h.astype(BF16)

    pc = _dot(hb, wcv_ref[...])
    u_ref[0] = (pc[:, :CONV_DIM] * _sigmoid(pc[:, CONV_DIM:])).astype(BF16)

    z = _dot(hb, wf_ref[...]) + bf_ref[...]
    lf = jnp.minimum(z, 0.0) - jnp.log1p(jnp.exp(-jnp.abs(z)))
    lane = lax.broadcasted_iota(jnp.int32, (1, LANES), 1)
    psum = _dot(tri_ref[...], _lane_pieces(jnp.where(lane < N_HEADS, lf, 0.0)))
    cum = (psum + pltpu.roll(psum, LANES - N_HEADS, 1) + pltpu.roll(psum, LANES - 2 * N_HEADS, 1)
           + carry_ref[...])
    cum = jnp.where(lane < N_HEADS, cum, 0.0)
    carry_ref[...] = cum[ROW_TILE - 1:ROW_TILE, :]
    cum_ref[0] = cum

    pieces = _lane_pieces(cum)
    addq = _dot(pieces, pq_ref[...]) + cq_ref[...]
    addk = _dot(pieces, pk_ref[...]) + ck_ref[...]

    pq = _dot(hb, wq_ref[...])
    pk = _dot(hb, wk_ref[...])
    pv = _dot(hb, wv_ref[...])
    inv_hd = 1.0 / HEAD_DIM
    vone = (lax.broadcasted_iota(jnp.int32, (1, HEAD_PAD), 1) == HEAD_DIM).astype(F32)
    for hd in range(N_HEADS):
        sl = slice(hd * HEAD_PAD, (hd + 1) * HEAD_PAD)
        qb = pq[:, sl]
        qn = qb * lax.rsqrt(jnp.sum(qb * qb, axis=-1, keepdims=True) * inv_hd + EPS) * qg_ref[:, sl]
        qa_ref[0, hd] = (qn + addq[:, sl]).astype(BF16)
        kb = pk[:, sl]
        kn = kb * lax.rsqrt(jnp.sum(kb * kb, axis=-1, keepdims=True) * inv_hd + EPS) * kg_ref[:, sl]
        ka_ref[0, hd] = (kn + addk[:, sl]).astype(BF16)
        v_ref[0, hd] = (_head_tile(pv, 0, hd) + vone).astype(BF16)

    sgc_ref[0] = _sigmoid(_dot(hb, wgc_ref[...])).astype(BF16)
    sga_ref[0] = _sigmoid(_dot(hb, wga_ref[...])).astype(BF16)


def _inproj(x, mod, g1, wcv, wq, wk, wv, wf, wgc, wga, bf, qg, kg, pq, pk, cq, ck, tri):
    b, s, d = x.shape
    tm = ROW_TILE
    row = lambda w: pl.BlockSpec((1, tm, w), lambda bi, i: (bi, i, 0))
    head = pl.BlockSpec((1, N_HEADS, tm, HEAD_PAD), lambda bi, i: (bi, 0, i, 0))
    consts = [g1, wcv, wq, wk, wv, wf, wgc, wga, bf, qg, kg, pq, pk, cq, ck, tri]
    return pl.pallas_call(
        _inproj_kernel,
        grid=(b, s // tm),
        in_specs=[row(d), pl.BlockSpec((1, 6, d), lambda bi, i: (bi, 0, 0))]
                 + [_const_spec(a.shape) for a in consts],
        out_specs=[row(CONV_DIM), head, head, head, row(d), row(d), row(LANES)],
        out_shape=[jax.ShapeDtypeStruct((b, s, CONV_DIM), BF16),
                   jax.ShapeDtypeStruct((b, N_HEADS, s, HEAD_PAD), BF16),
                   jax.ShapeDtypeStruct((b, N_HEADS, s, HEAD_PAD), BF16),
                   jax.ShapeDtypeStruct((b, N_HEADS, s, HEAD_PAD), BF16),
                   jax.ShapeDtypeStruct((b, s, d), BF16),
                   jax.ShapeDtypeStruct((b, s, d), BF16),
                   jax.ShapeDtypeStruct((b, s, LANES), F32)],
        scratch_shapes=[pltpu.VMEM((1, LANES), F32)],
        compiler_params=_params("arbitrary", "arbitrary"),
        name="inproj",
    )(x, mod, *consts)


def _attn_kernel(cs_ref, ce_ref, prm_ref, q_ref, k_ref, v_ref, o_ref, acc_ref, m_ref):
    t = ATTN_TILE
    nb = q_ref.shape[2] // t
    base = (pl.program_id(0) * pl.num_programs(1) + pl.program_id(1)) * nb
    bound = prm_ref[0]
    thresh = -(EXP_UNDERFLOW + 2.0 * bound)
    causal = lax.broadcasted_iota(jnp.int32, (t, t), 0) >= lax.broadcasted_iota(jnp.int32, (t, t), 1)

    def first_block(i):
        c0 = cs_ref[base + i]
        return lax.fori_loop(0, i, lambda j, n: n + (c0 - ce_ref[base + j] < thresh).astype(jnp.int32), 0)

    def scores(q, j, masked):
        k0 = pl.multiple_of(j * t, t)
        s = _dot_nt(q, k_ref[0, 0, pl.ds(k0, t), :])
        if masked:
            s = jnp.where(causal, s, NEG_BIG)
        return s, v_ref[0, 0, pl.ds(k0, t), :]

    def finish(q0):
        acc = acc_ref[...]
        o_ref[0, pl.ds(q0, t), :] = (acc / acc[:, HEAD_DIM:HEAD_DIM + 1]).astype(BF16)

    def fixed_shift(i, carry):
        q0 = pl.multiple_of(i * t, t)
        q = q_ref[0, 0, pl.ds(q0, t), :]

        acc_ref[...] = jnp.zeros_like(acc_ref)

        def weighted(j, masked):
            s, vb = scores(q, j, masked)
            return _dot(jnp.exp(s).astype(BF16), vb)

        def kv(j, c):
            acc_ref[...] += weighted(j, False)
            return c

        lax.fori_loop(first_block(i), i, kv, 0)
        rp = t // DIAG_PARTS
        for r in range(DIAG_PARTS):
            nk = (r + 1) * rp
            s = _dot_nt(q[r * rp:(r + 1) * rp, :], k_ref[0, 0, pl.ds(q0, nk), :])
            seen = (lax.broadcasted_iota(jnp.int32, (rp, nk), 0) + r * rp
                    >= lax.broadcasted_iota(jnp.int32, (rp, nk), 1))
            p = jnp.exp(jnp.where(seen, s, NEG_BIG)).astype(BF16)
            acc_ref[r * rp:(r + 1) * rp, :] += _dot(p, v_ref[0, 0, pl.ds(q0, nk), :])
        finish(q0)
        return carry

    def running_max(i, carry):
        q0 = pl.multiple_of(i * t, t)
        q = q_ref[0, 0, pl.ds(q0, t), :]
        m_ref[...] = jnp.full_like(m_ref, -jnp.inf)
        acc_ref[...] = jnp.zeros_like(acc_ref)

        def step(j, masked):
            s, vb = scores(q, j, masked)
            m_prev = m_ref[...]
            m_new = jnp.maximum(m_prev, jnp.max(s, axis=-1, keepdims=True))
            p = jnp.exp(s - m_new)
            acc_ref[...] = jnp.exp(m_prev - m_new) * acc_ref[...] + _dot(p.astype(BF16), vb)
            m_ref[...] = m_new

        def kv(j, c):
            step(j, False)
            return c

        lax.fori_loop(first_block(i), i, kv, 0)
        step(i, True)
        finish(q0)
        return carry

    @pl.when(bound <= FIXED_SHIFT_BOUND)
    def _():
        lax.fori_loop(0, nb, fixed_shift, 0)

    @pl.when(bound > FIXED_SHIFT_BOUND)
    def _():
        lax.fori_loop(0, nb, running_max, 0)


def _attention(cs, ce, prm, qa, ka, v):
    b, nh, s, hp = qa.shape
    t = ATTN_TILE
    seq = pl.BlockSpec((1, 1, s, hp), lambda bi, hi, *_: (bi, hi, 0, 0))
    return pl.pallas_call(
        _attn_kernel,
        grid_spec=pltpu.PrefetchScalarGridSpec(
            num_scalar_prefetch=3,
            grid=(b, nh),
            in_specs=[seq, seq, seq],
            out_specs=pl.BlockSpec((1, s, hp), lambda bi, hi, *_: (bi, 0, hi)),
            scratch_shapes=[pltpu.VMEM((t, hp), F32), pltpu.VMEM((t, 1), F32)]),
        out_shape=jax.ShapeDtypeStruct((b, s, nh * hp), BF16),
        compiler_params=_params("arbitrary", "arbitrary"),
        name="attn",
    )(cs, ce, prm, qa, ka, v)


def _merge_kernel(u_ref, halo_ref, o_ref, sgc_ref, sga_ref, x_ref, mod_ref, wdw_ref, bdw_ref,
                  gng_ref, gnb_ref, gg_ref, wco_ref, wao_ref, wout_ref, x1_ref, buf_ref):
    tm = ROW_TILE
    halo = halo_ref[0].astype(F32)
    halo = jnp.where(pl.program_id(1) == 0, jnp.zeros_like(halo), halo)
    ucur = u_ref[0].astype(F32)
    for cb in range(CONV_DIM // LANES):
        buf_ref[cb, 0:CONV_HALO, :] = halo[:, cb * LANES:(cb + 1) * LANES]
        buf_ref[cb, CONV_HALO:, :] = ucur[:, cb * LANES:(cb + 1) * LANES]

    base = CONV_HALO - (CONV_WIDTH - 1)
    ys = []
    for cb in range(CONV_DIM // LANES):
        acc = jnp.zeros((tm, LANES), F32)
        for j in range(CONV_WIDTH):
            acc = acc + wdw_ref[j:j + 1, cb * LANES:(cb + 1) * LANES] * buf_ref[cb, base + j:base + j + tm, :]
        ys.append(acc)
    y = jnp.concatenate(ys, axis=1) + bdw_ref[...]

    gg = gg_ref[...]
    y_hi, y_lo = _split2(y)
    dlt = y - (_dot(y_hi, gg) + _dot(y_lo, gg))
    s_hi, s_lo = _split2(dlt * dlt)
    var = _dot(s_hi, gg) + _dot(s_lo, gg)
    yn = dlt * lax.rsqrt(var + EPS) * gng_ref[...] + gnb_ref[...]
    y_conv = _dot(_silu(yn).astype(BF16), wco_ref[...])

    per = HEAD_PAD // HEAD_DIM
    lane = lax.broadcasted_iota(jnp.int32, (1, HEAD_PAD), 1)
    packed = []
    for g in range(N_HEADS // per):
        tile = o_ref[0, :, g * per * HEAD_PAD:(g * per + 1) * HEAD_PAD].astype(F32)
        for part in range(1, per):
            nxt = o_ref[0, :, (g * per + part) * HEAD_PAD:(g * per + part + 1) * HEAD_PAD].astype(F32)
            tile = jnp.where(lane < part * HEAD_DIM, tile, pltpu.roll(nxt, part * HEAD_DIM, 1))
        packed.append(tile)
    y_attn = _dot(jnp.concatenate(packed, axis=1).astype(BF16), wao_ref[...])
    merged = sgc_ref[0].astype(F32) * y_conv + sga_ref[0].astype(F32) * y_attn
    mix = _dot(merged.astype(BF16), wout_ref[...])
    x1_ref[0] = x_ref[0] + mod_ref[0, 2:3, :] * mix


def _merge(u, o, sgc, sga, x, mod, wdw, bdw, gng, gnb, gg, wco, wao, wout):
    b, s, d = x.shape
    tm = ROW_TILE
    per = tm // CONV_HALO
    row = lambda w: pl.BlockSpec((1, tm, w), lambda bi, i: (bi, i, 0))
    consts = [wdw, bdw, gng, gnb, gg, wco, wao, wout]
    return pl.pallas_call(
        _merge_kernel,
        grid=(b, s // tm),
        in_specs=[row(CONV_DIM),
                  pl.BlockSpec((1, CONV_HALO, CONV_DIM),
                               lambda bi, i: (bi, jnp.maximum(i * per - 1, 0), 0)),
                  row(o.shape[2]), row(d), row(d), row(d),
                  pl.BlockSpec((1, 6, d), lambda bi, i: (bi, 0, 0))]
                 + [_const_spec(a.shape) for a in consts],
        out_specs=row(d),
        out_shape=jax.ShapeDtypeStruct((b, s, d), F32),
        scratch_shapes=[pltpu.VMEM((CONV_DIM // LANES, CONV_HALO + tm, LANES), F32)],
        compiler_params=_params("arbitrary", "arbitrary"),
        name="merge",
    )(u, u, o, sgc, sga, x, mod, *consts)


def _router_kernel(x1_ref, mod_ref, g2_ref, wr_hi_ref, wr_lo_ref, rb_ref, h2_ref, h2w_ref, comb_ref, cnt_ref):
    x = x1_ref[0]
    ms = jnp.mean(x * x, axis=-1, keepdims=True)
    h = (x * lax.rsqrt(ms + EPS) * g2_ref[...]) * (1.0 + mod_ref[0, 4:5, :]) + mod_ref[0, 3:4, :]
    h2_ref[0] = h.astype(BF16)
    h2w_ref[0] = _pack_halves(h)

    h_hi, h_lo = _split2(h)
    logits = _dot_nt(wr_hi_ref[...], h_hi) + _dot_nt(wr_hi_ref[...], h_lo) + _dot_nt(wr_lo_ref[...], h_hi)
    scores = _sigmoid(logits)
    biased = scores + rb_ref[...]

    per = N_EXPERTS // N_GROUPS
    rows = lax.broadcasted_iota(jnp.int32, (per, biased.shape[1]), 0)
    gscore = []
    for g in range(N_GROUPS):
        blk = biased[g * per:(g + 1) * per, :]
        top1 = jnp.max(blk, axis=0, keepdims=True)
        first = jnp.min(jnp.where(blk == top1, rows, per), axis=0, keepdims=True)
        top2 = jnp.max(jnp.where(rows == first, -jnp.inf, blk), axis=0, keepdims=True)
        gscore.append(top1 + top2)

    cand = []
    for g in range(N_GROUPS):
        rank = jnp.zeros_like(gscore[g], dtype=jnp.int32)
        for g2 in range(N_GROUPS):
            if g2 == g:
                continue
            ahead = gscore[g2] > gscore[g]
            if g2 < g:
                ahead = ahead | (gscore[g2] == gscore[g])
            rank = rank + ahead.astype(jnp.int32)
        keep = rank < TOPK_GROUPS
        cand.append(jnp.where(keep, biased[g * per:(g + 1) * per, :], -jnp.inf))
    cand = jnp.concatenate(cand, axis=0)

    eidx = lax.broadcasted_iota(jnp.int32, cand.shape, 0)
    work = cand
    for _ in range(TOP_K):
        top = jnp.max(work, axis=0, keepdims=True)
        first = jnp.min(jnp.where(work == top, eidx, N_EXPERTS), axis=0, keepdims=True)
        work = jnp.where(eidx == first, -jnp.inf, work)
    sel = (work != cand) & (cand > -jnp.inf)
    w = jnp.where(sel, scores, 0.0)
    comb = w / jnp.sum(w, axis=0, keepdims=True) * ROUTED_SCALE
    comb_ref[...] = comb

    @pl.when((pl.program_id(0) == 0) & (pl.program_id(1) == 0))
    def _():
        cnt_ref[...] = jnp.zeros_like(cnt_ref)

    hit = jnp.where(comb != 0.0, 1.0, 0.0).astype(BF16)
    cnt_ref[...] += _dot(hit, jnp.ones((hit.shape[1], LANES), BF16))


def _router(x1, mod, g2, wr_hi, wr_lo, rb):
    b, s, d = x1.shape
    tm = ROW_TILE
    nt = s // tm
    return pl.pallas_call(
        _router_kernel,
        grid=(b, nt),
        in_specs=[pl.BlockSpec((1, tm, d), lambda bi, i: (bi, i, 0)),
                  pl.BlockSpec((1, 6, d), lambda bi, i: (bi, 0, 0)),
                  _const_spec(g2.shape), _const_spec(wr_hi.shape), _const_spec(wr_lo.shape),
                  _const_spec(rb.shape)],
        out_specs=[pl.BlockSpec((1, tm, d), lambda bi, i: (bi, i, 0)),
                   pl.BlockSpec((1, tm, d // 2), lambda bi, i: (bi, i, 0)),
                   pl.BlockSpec((N_EXPERTS, tm), lambda bi, i: (0, bi * nt + i)),
                   pl.BlockSpec((N_EXPERTS, LANES), lambda bi, i: (0, 0))],
        out_shape=[jax.ShapeDtypeStruct((b, s, d), BF16),
                   jax.ShapeDtypeStruct((b, s, d // 2), jnp.int32),
                   jax.ShapeDtypeStruct((N_EXPERTS, b * s), F32),
                   jax.ShapeDtypeStruct((N_EXPERTS, LANES), F32)],
        compiler_params=_params("arbitrary", "arbitrary"),
        name="router",
    )(x1, mod, g2, wr_hi, wr_lo, rb)


def _pos_kernel(comb_ref, cnt_ref, tri_ref, lstrict_ref, posk_ref, wk_ref, gend_ref, base_ref, *, spare_row):
    tp = comb_ref.shape[1]
    comb = comb_ref[...]
    sel = comb != 0.0
    selb = jnp.where(sel, 1.0, 0.0).astype(BF16)

    @pl.when(pl.program_id(0) == 0)
    def _():
        seg = jnp.floor((cnt_ref[...] + (EXPERT_TILE - 1.0)) * (1.0 / EXPERT_TILE)) * EXPERT_TILE
        s_hi, s_mid, s_lo = _split3(seg)
        ls = lstrict_ref[...]
        start = _dot(ls, s_hi) + _dot(ls, s_mid) + _dot(ls, s_lo)
        base_ref[...] = start
        gend_ref[...] = start + seg

    rank = _dot(selb, tri_ref[...])
    pos = base_ref[:, 0:1] + rank - 1.0
    base_ref[...] += _dot(selb, jnp.ones((tp, LANES), BF16))
    slot = _dot(lstrict_ref[...], selb)
    rows_p, rows_w = [], []
    for k in range(TOP_K):
        m = sel & (slot == k)
        rows_p.append(jnp.sum(jnp.where(m, pos - spare_row, 0.0), axis=0, keepdims=True) + spare_row)
        rows_w.append(jnp.sum(jnp.where(m, comb, 0.0), axis=0, keepdims=True))
    posk_ref[...] = jnp.concatenate(rows_p, axis=0).astype(jnp.int32)
    wk_ref[...] = jnp.concatenate(rows_w, axis=0)


def _positions(comb_t, cnt, tri, lstrict, n_rows):
    ne, t = comb_t.shape
    tp = POS_TILE
    tok = lambda rows: pl.BlockSpec((rows, tp), lambda i: (0, i))
    return pl.pallas_call(
        functools.partial(_pos_kernel, spare_row=float(n_rows - 1)),
        grid=(t // tp,),
        in_specs=[tok(ne), _const_spec(cnt.shape), _const_spec(tri.shape), _const_spec(lstrict.shape)],
        out_specs=[tok(TOP_K), tok(TOP_K), pl.BlockSpec((ne, LANES), lambda i: (0, 0))],
        out_shape=[jax.ShapeDtypeStruct((TOP_K, t), jnp.int32),
                   jax.ShapeDtypeStruct((TOP_K, t), F32),
                   jax.ShapeDtypeStruct((ne, LANES), F32)],
        scratch_shapes=[pltpu.VMEM((ne, LANES), F32)],
        compiler_params=_params("arbitrary"),
        name="positions",
    )(comb_t, cnt, tri, lstrict)


def _sc_workers():
    info = plsc.get_sparse_core_info()
    return info.num_cores, info.num_cores * info.num_subcores


def _sc_scatter_rows(rows, pos, n_out):
    nc, nw = _sc_workers()
    n, w = rows.shape
    nk = pos.shape[0]
    ch = SC_CHUNK
    per_w = n // nw
    assert per_w * nw == n and per_w % ch == 0

    @functools.partial(
        pl.kernel, mesh=plsc.VectorSubcoreMesh(core_axis_name="c", subcore_axis_name="s"),
        out_type=jax.ShapeDtypeStruct((n_out, w), rows.dtype),
        scratch_types=[pltpu.VMEM((nk, ch), jnp.int32), pltpu.VMEM((ch, w), rows.dtype),
                       pltpu.SemaphoreType.DMA])
    def scatter(rows_hbm, pos_hbm, out_hbm, idx_v, rows_v, sem):
        base = (lax.axis_index("s") * nc + lax.axis_index("c")) * per_w

        @pl.loop(0, per_w // ch)
        def _(ci):
            off = pl.multiple_of(base + ci * ch, ch)
            pltpu.sync_copy(pos_hbm.at[:, pl.ds(off, ch)], idx_v)
            pltpu.sync_copy(rows_hbm.at[pl.ds(off, ch)], rows_v)
            copies = [pltpu.make_async_copy(rows_v, out_hbm.at[idx_v.at[k]], sem) for k in range(nk)]
            for cp in copies:
                cp.start()
            for cp in copies:
                cp.wait()

    return scatter(rows, pos)


def _sc_gather_rows(table, idx):
    nc, nw = _sc_workers()
    n = idx.shape[0]
    w = table.shape[1]
    ch = SC_CHUNK
    per_w = n // nw
    assert per_w * nw == n and per_w % ch == 0
    nch = per_w // ch

    @functools.partial(
        pl.kernel, mesh=plsc.VectorSubcoreMesh(core_axis_name="c", subcore_axis_name="s"),
        out_type=jax.ShapeDtypeStruct((n, w), table.dtype),
        scratch_types=[pltpu.VMEM((nch, ch), jnp.int32), pltpu.VMEM((ch, w), table.dtype),
                       pltpu.SemaphoreType.DMA])
    def gather(table_hbm, idx_hbm, out_hbm, idx_v, rows_v, sem):
        wid = lax.axis_index("s") * nc + lax.axis_index("c")
        base = wid * per_w
        pltpu.sync_copy(idx_hbm.at[pl.ds(wid * nch, nch)], idx_v)

        @pl.loop(0, nch)
        def _(ci):
            off = pl.multiple_of(base + ci * ch, ch)
            cp = pltpu.make_async_copy(table_hbm.at[idx_v.at[ci]], rows_v, sem)
            cp.start()
            cp.wait()
            pltpu.sync_copy(rows_v, out_hbm.at[pl.ds(off, ch)])

    return gather(table, idx.reshape(n // ch, ch))


def _expert_kernel(te_ref, nu_ref, xs_ref, wgu_ref, wd_ref, y_ref, wgu_b, wd_b):
    i = pl.program_id(0)

    @pl.when(i < nu_ref[0])
    def _():
        @pl.when((i == 0) | (te_ref[i] != te_ref[jnp.maximum(i - 1, 0)]))
        def _():
            wgu_b[...] = wgu_ref[0].astype(BF16)
            wd_b[...] = wd_ref[0].astype(BF16)

        lo, hi = _unpack_halves(xs_ref[...])
        half = wgu_b.shape[0] // 2
        gu = _dot(lo.astype(BF16), wgu_b[:half, :]) + _dot(hi.astype(BF16), wgu_b[half:, :])
        act = _silu(gu[:, :EXPERT_DIM]) * gu[:, EXPERT_DIM:]
        y_ref[...] = _pack_halves(_dot(act.astype(BF16), wd_b[...]))


def _experts(tile_e, n_used, xs, wgu, wd):
    r, w = xs.shape
    tm = EXPERT_TILE
    d = wgu.shape[1]
    rows = pl.BlockSpec((tm, w), lambda i, te, nu: (jnp.minimum(i, nu[0] - 1), 0))
    return pl.pallas_call(
        _expert_kernel,
        grid_spec=pltpu.PrefetchScalarGridSpec(
            num_scalar_prefetch=2,
            grid=(r // tm,),
            in_specs=[rows,
                      pl.BlockSpec((1, d, 2 * EXPERT_DIM), lambda i, te, nu: (te[i], 0, 0)),
                      pl.BlockSpec((1, EXPERT_DIM, d), lambda i, te, nu: (te[i], 0, 0))],
            out_specs=rows,
            scratch_shapes=[pltpu.VMEM((d, 2 * EXPERT_DIM), BF16), pltpu.VMEM((EXPERT_DIM, d), BF16)]),
        out_shape=jax.ShapeDtypeStruct((r, w), jnp.int32),
        compiler_params=_params("arbitrary"),
        name="experts",
    )(tile_e, n_used, xs, wgu, wd)


def _final_kernel(yk_ref, wk_ref, h_ref, x1_ref, mod_ref, wsgu_ref, wsd_ref, out_ref):
    half = yk_ref.shape[2]
    acc_lo = jnp.zeros((yk_ref.shape[1], half), F32)
    acc_hi = jnp.zeros((yk_ref.shape[1], half), F32)
    for k in range(TOP_K):
        lo, hi = _unpack_halves(yk_ref[k])
        w = wk_ref[:, k:k + 1]
        acc_lo = acc_lo + jnp.where(w != 0.0, w * lo, 0.0)
        acc_hi = acc_hi + jnp.where(w != 0.0, w * hi, 0.0)
    routed = jnp.concatenate([acc_lo, acc_hi], axis=1)
    sgu = _dot(h_ref[...], wsgu_ref[...])
    act = _silu(sgu[:, :SHARED_DIM]) * sgu[:, SHARED_DIM:]
    shared = _dot(act.astype(BF16), wsd_ref[...])
    out_ref[...] = x1_ref[...] + mod_ref[0, 5:6, :] * (routed + shared)


def _final(yk, wk_t, h2, x1, mod, wsgu, wsd, tiles_per_batch):
    t, d = h2.shape
    tm = ROW_TILE
    row = lambda w: pl.BlockSpec((tm, w), lambda i: (i, 0))
    return pl.pallas_call(
        _final_kernel,
        grid=(t // tm,),
        in_specs=[pl.BlockSpec((TOP_K, tm, d // 2), lambda i: (0, i, 0)),
                  row(TOP_K), row(d), row(d),
                  pl.BlockSpec((1, 6, d), lambda i: (i // tiles_per_batch, 0, 0)),
                  _const_spec(wsgu.shape), _const_spec(wsd.shape)],
        out_specs=row(d),
        out_shape=jax.ShapeDtypeStruct((t, d), F32),
        compiler_params=_params("arbitrary"),
        name="final",
    )(yk, wk_t, h2, x1, mod, wsgu, wsd)


def _pad_heads(w):
    d = w.shape[0]
    w = w.reshape(d, N_HEADS, HEAD_DIM)
    return jnp.pad(w, ((0, 0), (0, 0), (0, HEAD_PAD - HEAD_DIM))).reshape(d, N_HEADS * HEAD_PAD)


def _placement():
    pq = np.zeros((LANES, N_HEADS * HEAD_PAD), np.float32)
    pk = np.zeros((LANES, N_HEADS * HEAD_PAD), np.float32)
    cq = np.zeros((1, N_HEADS * HEAD_PAD), np.float32)
    ck = np.zeros((1, N_HEADS * HEAD_PAD), np.float32)
    for hd in range(N_HEADS):
        for k in range(3):
            pq[k * N_HEADS + hd, hd * HEAD_PAD + AUG0 + k] = 1.0
            ck[0, hd * HEAD_PAD + AUG0 + k] = 1.0
            pk[k * N_HEADS + hd, hd * HEAD_PAD + AUG0 + 3 + k] = -1.0
            cq[0, hd * HEAD_PAD + AUG0 + 3 + k] = 1.0
    return (jnp.asarray(pq, BF16), jnp.asarray(pk, BF16), jnp.asarray(cq), jnp.asarray(ck))


def kernel(x, c, w_ada, b_ada, norm1_g, w_in, w_dw, b_dw, conv_gn_g, conv_gn_b, w_conv_out,
           q_norm_g, k_norm_g, b_forget, w_attn_out, w_out, norm2_g, w_router, router_bias,
           w_experts_gate_up, w_experts_down, w_shared_gate_up, w_shared_down):
    depth = w_ada.shape[0]
    b, s, d = x.shape
    off_q = 2 * CONV_DIM
    off_f = off_q + 3 * ATTN_DIM
    off_gc = off_f + N_HEADS
    off_ga = off_gc + d

    pq, pk, cq, ck = _placement()
    tri = jnp.asarray(np.tril(np.ones((ROW_TILE, ROW_TILE), np.float32)), BF16)
    grp = np.arange(CONV_DIM) // (CONV_DIM // CONV_GROUPS)
    gg = jnp.asarray((grp[:, None] == grp[None, :]).astype(np.float32) / (CONV_DIM // CONV_GROUPS), BF16)
    c_pad = jnp.pad(c, ((0, SUBLANES - b), (0, 0)))
    tri_pos = jnp.asarray(np.triu(np.ones((POS_TILE, POS_TILE), np.float32)), BF16)
    lstrict = jnp.asarray(np.tril(np.ones((N_EXPERTS, N_EXPERTS), np.float32), -1), BF16)

    for l in range(depth):
        mod = _ada(c_pad, w_ada[l], b_ada[l][None, :])[:b].reshape(b, 6, d)

        wi = w_in[l]
        bf = jnp.pad(b_forget[l][None, :], ((0, 0), (0, LANES - N_HEADS)))
        wf = jnp.pad(wi[:, off_f:off_gc], ((0, 0), (0, LANES - N_HEADS))).astype(BF16)
        gpad = lambda g, sc: jnp.tile(jnp.pad(g * sc, (0, HEAD_PAD - HEAD_DIM)), N_HEADS)[None, :]
        qscale = HEAD_DIM ** -0.5
        u, qa, ka, v, sgc, sga, cum = _inproj(
            x, mod, norm1_g[l][None, :],
            wi[:, :off_q].astype(BF16),
            _pad_heads(wi[:, off_q:off_q + ATTN_DIM]).astype(BF16),
            _pad_heads(wi[:, off_q + ATTN_DIM:off_q + 2 * ATTN_DIM]).astype(BF16),
            wi[:, off_q + 2 * ATTN_DIM:off_f].astype(BF16),
            wf, wi[:, off_gc:off_ga].astype(BF16), wi[:, off_ga:].astype(BF16),
            bf, gpad(q_norm_g[l], qscale), gpad(k_norm_g[l], 1.0),
            pq, pk, cq, ck, tri)

        flat = lambda a: a[:, :, :N_HEADS].transpose(0, 2, 1).reshape(-1)
        cs = flat(cum[:, 0::ATTN_TILE])
        ce = flat(cum[:, ATTN_TILE - 1::ATTN_TILE])
        bound = (1.02 * HEAD_DIM * qscale) * jnp.max(jnp.abs(q_norm_g[l])) * jnp.max(jnp.abs(k_norm_g[l]))
        o = _attention(cs, ce, bound.reshape(1), qa, ka, v)

        wdw = jnp.pad(w_dw[l], ((0, CONV_HALO - CONV_WIDTH), (0, 0)))
        x1 = _merge(u, o, sgc, sga, x, mod, wdw, b_dw[l][None, :], conv_gn_g[l][None, :],
                    conv_gn_b[l][None, :], gg, w_conv_out[l].astype(BF16), w_attn_out[l].astype(BF16),
                    w_out[l].astype(BF16))

        wr = w_router[l].T
        wr_hi = wr.astype(BF16)
        wr_lo = (wr - wr_hi.astype(F32)).astype(BF16)
        h2, h2w, comb_t, cnt = _router(x1, mod, norm2_g[l][None, :], wr_hi, wr_lo, router_bias[l][:, None])

        t = b * s
        n_tiles = (t * TOP_K) // EXPERT_TILE + N_EXPERTS
        posk, wk, gend = _positions(comb_t, cnt, tri_pos, lstrict, n_tiles * EXPERT_TILE)
        seg_end = gend[:, 0].astype(jnp.int32)
        n_used = seg_end[-1:] // EXPERT_TILE
        tile_start = jnp.arange(n_tiles, dtype=jnp.int32) * EXPERT_TILE
        tile_start = jnp.minimum(tile_start, seg_end[-1] - EXPERT_TILE)
        tile_e = jnp.sum((seg_end[None, :] <= tile_start[:, None]).astype(jnp.int32), axis=1)

        xs = _sc_scatter_rows(h2w.reshape(t, d // 2), posk, n_tiles * EXPERT_TILE)
        ys = _experts(tile_e, n_used, xs, w_experts_gate_up[l], w_experts_down[l])
        yk = _sc_gather_rows(ys, posk.reshape(-1)).reshape(TOP_K, t, d // 2)
        out = _final(yk, wk.T, h2.reshape(t, d), x1.reshape(t, d), mod,
                     w_shared_gate_up[l].astype(BF16), w_shared_down[l].astype(BF16), s // ROW_TILE)
        x = out.reshape(b, s, d)
    return x
```

```python
import functools

import numpy as np
import jax
import jax.numpy as jnp
from jax import lax
from jax.experimental import pallas as pl
from jax.experimental.pallas import tpu as pltpu
from jax.experimental.pallas import tpu_sc as plsc

F32 = jnp.float32
BF16 = jnp.bfloat16

CONV_DIM = 512
CONV_WIDTH = 31
CONV_GROUPS = 8
N_HEADS = 8
HEAD_DIM = 64
ATTN_DIM = N_HEADS * HEAD_DIM
N_EXPERTS = 64
TOP_K = 8
N_GROUPS = 8
TOPK_GROUPS = 4
EXPERT_DIM = 256
SHARED_DIM = 256
ROUTED_SCALE = 2.5
EPS = 1e-6

LANES = 128
SUBLANES = 8
HEAD_PAD = LANES
AUG0 = HEAD_DIM
VMEM_LIMIT = 56 * 1024 * 1024

ROW_TILE = 512
ATTN_TILE = 1024
DIAG_PARTS = 2
EXPERT_TILE = 512
POS_TILE = 1024
SC_CHUNK = 128
CONV_HALO = 32

NEG_BIG = -1e30
EXP_UNDERFLOW = 104.0
FIXED_SHIFT_BOUND = 40.0


def _dot(a, b):
    return jnp.dot(a, b, preferred_element_type=F32)


def _dot_nt(a, b):
    return lax.dot_general(a, b, (((1,), (1,)), ((), ())), preferred_element_type=F32)


def _split2(x):
    hi = x.astype(BF16)
    lo = (x - hi.astype(F32)).astype(BF16)
    return hi, lo


def _split3(x):
    hi = x.astype(BF16)
    r = x - hi.astype(F32)
    mid = r.astype(BF16)
    lo = (r - mid.astype(F32)).astype(BF16)
    return hi, mid, lo


def _pack_halves(v):
    n = v.shape[1] // 2
    lo = lax.bitcast_convert_type(v[:, :n].astype(BF16).astype(F32), jnp.uint32)
    hi = lax.bitcast_convert_type(v[:, n:].astype(BF16).astype(F32), jnp.uint32)
    return lax.bitcast_convert_type(hi | lax.shift_right_logical(lo, jnp.uint32(16)), jnp.int32)


def _unpack_halves(w):
    u = lax.bitcast_convert_type(w, jnp.uint32)
    lo = lax.bitcast_convert_type(lax.shift_left(u, jnp.uint32(16)), F32)
    hi = lax.bitcast_convert_type(u & jnp.uint32(0xFFFF0000), F32)
    return lo, hi


def _sigmoid(x):
    return 1.0 / (1.0 + jnp.exp(-x))


def _silu(x):
    return x * _sigmoid(x)


def _params(*sem):
    return pltpu.CompilerParams(dimension_semantics=sem, vmem_limit_bytes=VMEM_LIMIT)


def _const_spec(shape):
    n = len(shape)
    return pl.BlockSpec(shape, lambda *_: (0,) * n, pipeline_mode=pl.Buffered(1))


def _ada_kernel(c_ref, w_ref, b_ref, o_ref):
    c = c_ref[...]
    a_hi, a_lo = _split2(_silu(c))
    w_hi, w_lo = _split2(w_ref[...])
    o_ref[...] = _dot(a_hi, w_hi) + _dot(a_hi, w_lo) + _dot(a_lo, w_hi) + b_ref[...]


def _ada(c_pad, w_ada, b_ada):
    d = c_pad.shape[1]
    n = w_ada.shape[1]
    return pl.pallas_call(
        _ada_kernel,
        grid=(n // d,),
        in_specs=[_const_spec(c_pad.shape),
                  pl.BlockSpec((d, d), lambda j: (0, j)),
                  pl.BlockSpec((1, d), lambda j: (0, j))],
        out_specs=pl.BlockSpec((c_pad.shape[0], d), lambda j: (0, j)),
        out_shape=jax.ShapeDtypeStruct((c_pad.shape[0], n), F32),
        compiler_params=_params("arbitrary"),
        name="ada",
    )(c_pad, w_ada, b_ada)


def _lane_pieces(x):
    hi, mid, lo = _split3(x)
    return (hi.astype(F32) + pltpu.roll(mid.astype(F32), N_HEADS, 1)
            + pltpu.roll(lo.astype(F32), 2 * N_HEADS, 1)).astype(BF16)


def _head_tile(p, col0, hd):
    per = HEAD_PAD // HEAD_DIM
    g, part = divmod(hd, per)
    blk = p[:, col0 + g * HEAD_PAD:col0 + (g + 1) * HEAD_PAD]
    if part:
        blk = pltpu.roll(blk, HEAD_PAD - part * HEAD_DIM, 1)
    lane = lax.broadcasted_iota(jnp.int32, (1, HEAD_PAD), 1)
    return jnp.where(lane < HEAD_DIM, blk, 0.0)


def _inproj_kernel(x_ref, mod_ref, g1_ref, w_ref, bf_ref, qg_ref, kg_ref, pq_ref, pk_ref, cq_ref, ck_ref, tri_ref,
                   u_ref, qa_ref, ka_ref, v_ref, sgc_ref, sga_ref, cum_ref, carry_ref):
    @pl.when(pl.program_id(1) == 0)
    def _():
        carry_ref[...] = jnp.zeros_like(carry_ref)

    x = x_ref[0]
    ms = jnp.mean(x * x, axis=-1, keepdims=True)
    h = (x * lax.rsqrt(ms + EPS) * g1_ref[...]) * (1.0 + mod_ref[0, 1:2, :]) + mod_ref[0, 0:1, :]
    hb = h.astype(BF16)

    hp = N_HEADS * HEAD_PAD
    o_q = 2 * CONV_DIM
    o_k, o_v = o_q + hp, o_q + 2 * hp
    o_f = o_v + ATTN_DIM
    o_gc = o_f + LANES
    o_ga = o_gc + x.shape[1]
    pc = _dot(hb, w_ref[:, :o_q])
    u_ref[0] = (pc[:, :CONV_DIM] * _sigmoid(pc[:, CONV_DIM:])).astype(BF16)

    z = _dot(hb, w_ref[:, o_f:o_gc]) + bf_ref[...]
    lf = jnp.minimum(z, 0.0) - jnp.log1p(jnp.exp(-jnp.abs(z)))
    lane = lax.broadcasted_iota(jnp.int32, (1, LANES), 1)
    psum = _dot(tri_ref[...], _lane_pieces(jnp.where(lane < N_HEADS, lf, 0.0)))
    cum = (psum + pltpu.roll(psum, LANES - N_HEADS, 1) + pltpu.roll(psum, LANES - 2 * N_HEADS, 1)
           + carry_ref[...])
    cum = jnp.where(lane < N_HEADS, cum, 0.0)
    carry_ref[...] = cum[ROW_TILE - 1:ROW_TILE, :]
    cum_ref[0] = cum

    pieces = _lane_pieces(cum)
    addq = _dot(pieces, pq_ref[...]) + cq_ref[...]
    addk = _dot(pieces, pk_ref[...]) + ck_ref[...]

    pq = _dot(hb, w_ref[:, o_q:o_k])
    pk = _dot(hb, w_ref[:, o_k:o_v])
    pv = _dot(hb, w_ref[:, o_v:o_f])
    inv_hd = 1.0 / HEAD_DIM
    vone = (lax.broadcasted_iota(jnp.int32, (1, HEAD_PAD), 1) == HEAD_DIM).astype(F32)
    for hd in range(N_HEADS):
        sl = slice(hd * HEAD_PAD, (hd + 1) * HEAD_PAD)
        qb = pq[:, sl]
        qn = qb * lax.rsqrt(jnp.sum(qb * qb, axis=-1, keepdims=True) * inv_hd + EPS) * qg_ref[:, sl]
        qa_ref[0, hd] = (qn + addq[:, sl]).astype(BF16)
        kb = pk[:, sl]
        kn = kb * lax.rsqrt(jnp.sum(kb * kb, axis=-1, keepdims=True) * inv_hd + EPS) * kg_ref[:, sl]
        ka_ref[0, hd] = (kn + addk[:, sl]).astype(BF16)
        v_ref[0, hd] = (_head_tile(pv, 0, hd) + vone).astype(BF16)

    sgc_ref[0] = _sigmoid(_dot(hb, w_ref[:, o_gc:o_ga])).astype(BF16)
    sga_ref[0] = _sigmoid(_dot(hb, w_ref[:, o_ga:])).astype(BF16)


def _inproj_weight(wi):
    d = wi.shape[0]
    o_q = 2 * CONV_DIM
    o_f = o_q + 3 * ATTN_DIM
    o_gc = o_f + N_HEADS
    heads = lambda w: jnp.pad(w.reshape(d, N_HEADS, HEAD_DIM),
                              ((0, 0), (0, 0), (0, HEAD_PAD - HEAD_DIM))).reshape(d, N_HEADS * HEAD_PAD)
    return jnp.concatenate(
        [wi[:, :o_q], heads(wi[:, o_q:o_q + ATTN_DIM]), heads(wi[:, o_q + ATTN_DIM:o_q + 2 * ATTN_DIM]),
         wi[:, o_q + 2 * ATTN_DIM:o_f], jnp.pad(wi[:, o_f:o_gc], ((0, 0), (0, LANES - N_HEADS))),
         wi[:, o_gc:]], axis=1).astype(BF16)


def _inproj(x, mod, g1, w, bf, qg, kg, pq, pk, cq, ck, tri):
    b, s, d = x.shape
    tm = ROW_TILE
    row = lambda w: pl.BlockSpec((1, tm, w), lambda bi, i: (bi, i, 0))
    head = pl.BlockSpec((1, N_HEADS, tm, HEAD_PAD), lambda bi, i: (bi, 0, i, 0))
    consts = [g1, w, bf, qg, kg, pq, pk, cq, ck, tri]
    return pl.pallas_call(
        _inproj_kernel,
        grid=(b, s // tm),
        in_specs=[row(d), pl.BlockSpec((1, 6, d), lambda bi, i: (bi, 0, 0))]
                 + [_const_spec(a.shape) for a in consts],
        out_specs=[row(CONV_DIM), head, head, head, row(d), row(d), row(LANES)],
        out_shape=[jax.ShapeDtypeStruct((b, s, CONV_DIM), BF16),
                   jax.ShapeDtypeStruct((b, N_HEADS, s, HEAD_PAD), BF16),
                   jax.ShapeDtypeStruct((b, N_HEADS, s, HEAD_PAD), BF16),
                   jax.ShapeDtypeStruct((b, N_HEADS, s, HEAD_PAD), BF16),
                   jax.ShapeDtypeStruct((b, s, d), BF16),
                   jax.ShapeDtypeStruct((b, s, d), BF16),
                   jax.ShapeDtypeStruct((b, s, LANES), F32)],
        scratch_shapes=[pltpu.VMEM((1, LANES), F32)],
        compiler_params=_params("arbitrary", "arbitrary"),
        name="inproj",
    )(x, mod, *consts)


def _attn_kernel(cs_ref, ce_ref, prm_ref, q_ref, k_ref, v_ref, o_ref, acc_ref, m_ref):
    t = ATTN_TILE
    nb = q_ref.shape[2] // t
    base = (pl.program_id(0) * pl.num_programs(1) + pl.program_id(1)) * nb
    bound = prm_ref[0]
    thresh = -(EXP_UNDERFLOW + 2.0 * bound)
    causal = lax.broadcasted_iota(jnp.int32, (t, t), 0) >= lax.broadcasted_iota(jnp.int32, (t, t), 1)

    def first_block(i):
        c0 = cs_ref[base + i]
        return lax.fori_loop(0, i, lambda j, n: n + (c0 - ce_ref[base + j] < thresh).astype(jnp.int32), 0)

    def scores(q, j, masked):
        k0 = pl.multiple_of(j * t, t)
        s = _dot_nt(q, k_ref[0, 0, pl.ds(k0, t), :])
        if masked:
            s = jnp.where(causal, s, NEG_BIG)
        return s, v_ref[0, 0, pl.ds(k0, t), :]

    def finish(q0):
        acc = acc_ref[...]
        o_ref[0, pl.ds(q0, t), :] = (acc / acc[:, HEAD_DIM:HEAD_DIM + 1]).astype(BF16)

    def fixed_shift(i, carry):
        q0 = pl.multiple_of(i * t, t)
        q = q_ref[0, 0, pl.ds(q0, t), :]

        acc_ref[...] = jnp.zeros_like(acc_ref)

        def weighted(j, masked):
            s, vb = scores(q, j, masked)
            return _dot(jnp.exp(s).astype(BF16), vb)

        def kv(j, c):
            acc_ref[...] += weighted(j, False)
            return c

        lax.fori_loop(first_block(i), i, kv, 0)
        rp = t // DIAG_PARTS
        for r in range(DIAG_PARTS):
            nk = (r + 1) * rp
            s = _dot_nt(q[r * rp:(r + 1) * rp, :], k_ref[0, 0, pl.ds(q0, nk), :])
            seen = (lax.broadcasted_iota(jnp.int32, (rp, nk), 0) + r * rp
                    >= lax.broadcasted_iota(jnp.int32, (rp, nk), 1))
            p = jnp.exp(jnp.where(seen, s, NEG_BIG)).astype(BF16)
            acc_ref[r * rp:(r + 1) * rp, :] += _dot(p, v_ref[0, 0, pl.ds(q0, nk), :])
        finish(q0)
        return carry

    def running_max(i, carry):
        q0 = pl.multiple_of(i * t, t)
        q = q_ref[0, 0, pl.ds(q0, t), :]
        m_ref[...] = jnp.full_like(m_ref, -jnp.inf)
        acc_ref[...] = jnp.zeros_like(acc_ref)

        def step(j, masked):
            s, vb = scores(q, j, masked)
            m_prev = m_ref[...]
            m_new = jnp.maximum(m_prev, jnp.max(s, axis=-1, keepdims=True))
            p = jnp.exp(s - m_new)
            acc_ref[...] = jnp.exp(m_prev - m_new) * acc_ref[...] + _dot(p.astype(BF16), vb)
            m_ref[...] = m_new

        def kv(j, c):
            step(j, False)
            return c

        lax.fori_loop(first_block(i), i, kv, 0)
        step(i, True)
        finish(q0)
        return carry

    @pl.when(bound <= FIXED_SHIFT_BOUND)
    def _():
        lax.fori_loop(0, nb, fixed_shift, 0)

    @pl.when(bound > FIXED_SHIFT_BOUND)
    def _():
        lax.fori_loop(0, nb, running_max, 0)


def _attention(cs, ce, prm, qa, ka, v):
    b, nh, s, hp = qa.shape
    t = ATTN_TILE
    seq = pl.BlockSpec((1, 1, s, hp), lambda bi, hi, *_: (bi, hi, 0, 0))
    return pl.pallas_call(
        _attn_kernel,
        grid_spec=pltpu.PrefetchScalarGridSpec(
            num_scalar_prefetch=3,
            grid=(b, nh),
            in_specs=[seq, seq, seq],
            out_specs=pl.BlockSpec((1, s, hp), lambda bi, hi, *_: (bi, 0, hi)),
            scratch_shapes=[pltpu.VMEM((t, hp), F32), pltpu.VMEM((t, 1), F32)]),
        out_shape=jax.ShapeDtypeStruct((b, s, nh * hp), BF16),
        compiler_params=_params("arbitrary", "arbitrary"),
        name="attn",
    )(cs, ce, prm, qa, ka, v)


def _merge_kernel(u_ref, halo_ref, o_ref, sgc_ref, sga_ref, x_ref, mod_ref, wdw_ref, bdw_ref,
                  gng_ref, gnb_ref, gg_ref, wco_ref, wao_ref, wout_ref, x1_ref, buf_ref):
    tm = ROW_TILE
    halo = halo_ref[0].astype(F32)
    halo = jnp.where(pl.program_id(1) == 0, jnp.zeros_like(halo), halo)
    ucur = u_ref[0].astype(F32)
    for cb in range(CONV_DIM // LANES):
        buf_ref[cb, 0:CONV_HALO, :] = halo[:, cb * LANES:(cb + 1) * LANES]
        buf_ref[cb, CONV_HALO:, :] = ucur[:, cb * LANES:(cb + 1) * LANES]

    base = CONV_HALO - (CONV_WIDTH - 1)
    ys = []
    for cb in range(CONV_DIM // LANES):
        acc = jnp.zeros((tm, LANES), F32)
        for j in range(CONV_WIDTH):
            acc = acc + wdw_ref[j:j + 1, cb * LANES:(cb + 1) * LANES] * buf_ref[cb, base + j:base + j + tm, :]
        ys.append(acc)
    y = jnp.concatenate(ys, axis=1) + bdw_ref[...]

    gg = gg_ref[...]
    y_hi, y_lo = _split2(y)
    dlt = y - (_dot(y_hi, gg) + _dot(y_lo, gg))
    s_hi, s_lo = _split2(dlt * dlt)
    var = _dot(s_hi, gg) + _dot(s_lo, gg)
    yn = dlt * lax.rsqrt(var + EPS) * gng_ref[...] + gnb_ref[...]
    y_conv = _dot(_silu(yn).astype(BF16), wco_ref[...])

    per = HEAD_PAD // HEAD_DIM
    lane = lax.broadcasted_iota(jnp.int32, (1, HEAD_PAD), 1)
    packed = []
    for g in range(N_HEADS // per):
        tile = o_ref[0, :, g * per * HEAD_PAD:(g * per + 1) * HEAD_PAD].astype(F32)
        for part in range(1, per):
            nxt = o_ref[0, :, (g * per + part) * HEAD_PAD:(g * per + part + 1) * HEAD_PAD].astype(F32)
            tile = jnp.where(lane < part * HEAD_DIM, tile, pltpu.roll(nxt, part * HEAD_DIM, 1))
        packed.append(tile)
    y_attn = _dot(jnp.concatenate(packed, axis=1).astype(BF16), wao_ref[...])
    merged = sgc_ref[0].astype(F32) * y_conv + sga_ref[0].astype(F32) * y_attn
    mix = _dot(merged.astype(BF16), wout_ref[...])
    x1_ref[0] = x_ref[0] + mod_ref[0, 2:3, :] * mix


def _merge(u, o, sgc, sga, x, mod, wdw, bdw, gng, gnb, gg, wco, wao, wout):
    b, s, d = x.shape
    tm = ROW_TILE
    per = tm // CONV_HALO
    row = lambda w: pl.BlockSpec((1, tm, w), lambda bi, i: (bi, i, 0))
    consts = [wdw, bdw, gng, gnb, gg, wco, wao, wout]
    return pl.pallas_call(
        _merge_kernel,
        grid=(b, s // tm),
        in_specs=[row(CONV_DIM),
                  pl.BlockSpec((1, CONV_HALO, CONV_DIM),
                               lambda bi, i: (bi, jnp.maximum(i * per - 1, 0), 0)),
                  row(o.shape[2]), row(d), row(d), row(d),
                  pl.BlockSpec((1, 6, d), lambda bi, i: (bi, 0, 0))]
                 + [_const_spec(a.shape) for a in consts],
        out_specs=row(d),
        out_shape=jax.ShapeDtypeStruct((b, s, d), F32),
        scratch_shapes=[pltpu.VMEM((CONV_DIM // LANES, CONV_HALO + tm, LANES), F32)],
        compiler_params=_params("arbitrary", "arbitrary"),
        name="merge",
    )(u, u, o, sgc, sga, x, mod, *consts)


def _router_kernel(x1_ref, mod_ref, g2_ref, wr_hi_ref, wr_lo_ref, rb_ref, h2_ref, h2w_ref, comb_ref, cnt_ref):
    x = x1_ref[0]
    ms = jnp.mean(x * x, axis=-1, keepdims=True)
    h = (x * lax.rsqrt(ms + EPS) * g2_ref[...]) * (1.0 + mod_ref[0, 4:5, :]) + mod_ref[0, 3:4, :]
    h2_ref[0] = h.astype(BF16)
    h2w_ref[0] = _pack_halves(h)

    h_hi, h_lo = _split2(h)
    logits = _dot_nt(wr_hi_ref[...], h_hi) + _dot_nt(wr_hi_ref[...], h_lo) + _dot_nt(wr_lo_ref[...], h_hi)
    scores = _sigmoid(logits)
    biased = scores + rb_ref[...]

    per = N_EXPERTS // N_GROUPS
    rows = lax.broadcasted_iota(jnp.int32, (per, biased.shape[1]), 0)
    gscore = []
    for g in range(N_GROUPS):
        blk = biased[g * per:(g + 1) * per, :]
        top1 = jnp.max(blk, axis=0, keepdims=True)
        first = jnp.min(jnp.where(blk == top1, rows, per), axis=0, keepdims=True)
        top2 = jnp.max(jnp.where(rows == first, -jnp.inf, blk), axis=0, keepdims=True)
        gscore.append(top1 + top2)

    cand = []
    for g in range(N_GROUPS):
        rank = jnp.zeros_like(gscore[g], dtype=jnp.int32)
        for g2 in range(N_GROUPS):
            if g2 == g:
                continue
            ahead = gscore[g2] > gscore[g]
            if g2 < g:
                ahead = ahead | (gscore[g2] == gscore[g])
            rank = rank + ahead.astype(jnp.int32)
        keep = rank < TOPK_GROUPS
        cand.append(jnp.where(keep, biased[g * per:(g + 1) * per, :], -jnp.inf))
    cand = jnp.concatenate(cand, axis=0)

    eidx = lax.broadcasted_iota(jnp.int32, cand.shape, 0)
    work = cand
    for _ in range(TOP_K):
        top = jnp.max(work, axis=0, keepdims=True)
        first = jnp.min(jnp.where(work == top, eidx, N_EXPERTS), axis=0, keepdims=True)
        work = jnp.where(eidx == first, -jnp.inf, work)
    sel = (work != cand) & (cand > -jnp.inf)
    w = jnp.where(sel, scores, 0.0)
    comb = w / jnp.sum(w, axis=0, keepdims=True) * ROUTED_SCALE
    comb_ref[...] = comb

    @pl.when((pl.program_id(0) == 0) & (pl.program_id(1) == 0))
    def _():
        cnt_ref[...] = jnp.zeros_like(cnt_ref)

    hit = jnp.where(comb != 0.0, 1.0, 0.0).astype(BF16)
    cnt_ref[...] += _dot(hit, jnp.ones((hit.shape[1], LANES), BF16))


def _router(x1, mod, g2, wr_hi, wr_lo, rb):
    b, s, d = x1.shape
    tm = ROW_TILE
    nt = s // tm
    return pl.pallas_call(
        _router_kernel,
        grid=(b, nt),
        in_specs=[pl.BlockSpec((1, tm, d), lambda bi, i: (bi, i, 0)),
                  pl.BlockSpec((1, 6, d), lambda bi, i: (bi, 0, 0)),
                  _const_spec(g2.shape), _const_spec(wr_hi.shape), _const_spec(wr_lo.shape),
                  _const_spec(rb.shape)],
        out_specs=[pl.BlockSpec((1, tm, d), lambda bi, i: (bi, i, 0)),
                   pl.BlockSpec((1, tm, d // 2), lambda bi, i: (bi, i, 0)),
                   pl.BlockSpec((N_EXPERTS, tm), lambda bi, i: (0, bi * nt + i)),
                   pl.BlockSpec((N_EXPERTS, LANES), lambda bi, i: (0, 0))],
        out_shape=[jax.ShapeDtypeStruct((b, s, d), BF16),
                   jax.ShapeDtypeStruct((b, s, d // 2), jnp.int32),
                   jax.ShapeDtypeStruct((N_EXPERTS, b * s), F32),
                   jax.ShapeDtypeStruct((N_EXPERTS, LANES), F32)],
        compiler_params=_params("arbitrary", "arbitrary"),
        name="router",
    )(x1, mod, g2, wr_hi, wr_lo, rb)


def _pos_kernel(comb_ref, cnt_ref, tri_ref, lstrict_ref, posk_ref, wk_ref, gend_ref, base_ref, *, spare_row):
    tp = comb_ref.shape[1]
    comb = comb_ref[...]
    sel = comb != 0.0
    selb = jnp.where(sel, 1.0, 0.0).astype(BF16)

    @pl.when(pl.program_id(0) == 0)
    def _():
        seg = jnp.floor((cnt_ref[...] + (EXPERT_TILE - 1.0)) * (1.0 / EXPERT_TILE)) * EXPERT_TILE
        s_hi, s_mid, s_lo = _split3(seg)
        ls = lstrict_ref[...]
        start = _dot(ls, s_hi) + _dot(ls, s_mid) + _dot(ls, s_lo)
        base_ref[...] = start
        gend_ref[...] = start + seg

    rank = _dot(selb, tri_ref[...])
    pos = base_ref[:, 0:1] + rank - 1.0
    base_ref[...] += _dot(selb, jnp.ones((tp, LANES), BF16))
    slot = _dot(lstrict_ref[...], selb)
    rows_p, rows_w = [], []
    for k in range(TOP_K):
        m = sel & (slot == k)
        rows_p.append(jnp.sum(jnp.where(m, pos - spare_row, 0.0), axis=0, keepdims=True) + spare_row)
        rows_w.append(jnp.sum(jnp.where(m, comb, 0.0), axis=0, keepdims=True))
    posk_ref[...] = jnp.concatenate(rows_p, axis=0).astype(jnp.int32)
    wk_ref[...] = jnp.concatenate(rows_w, axis=0)


def _positions(comb_t, cnt, tri, lstrict, n_rows):
    ne, t = comb_t.shape
    tp = POS_TILE
    tok = lambda rows: pl.BlockSpec((rows, tp), lambda i: (0, i))
    return pl.pallas_call(
        functools.partial(_pos_kernel, spare_row=float(n_rows - 1)),
        grid=(t // tp,),
        in_specs=[tok(ne), _const_spec(cnt.shape), _const_spec(tri.shape), _const_spec(lstrict.shape)],
        out_specs=[tok(TOP_K), tok(TOP_K), pl.BlockSpec((ne, LANES), lambda i: (0, 0))],
        out_shape=[jax.ShapeDtypeStruct((TOP_K, t), jnp.int32),
                   jax.ShapeDtypeStruct((TOP_K, t), F32),
                   jax.ShapeDtypeStruct((ne, LANES), F32)],
        scratch_shapes=[pltpu.VMEM((ne, LANES), F32)],
        compiler_params=_params("arbitrary"),
        name="positions",
    )(comb_t, cnt, tri, lstrict)


def _sc_workers():
    info = plsc.get_sparse_core_info()
    return info.num_cores, info.num_cores * info.num_subcores


def _sc_scatter_rows(rows, pos, n_out):
    nc, nw = _sc_workers()
    n, w = rows.shape
    nk = pos.shape[0]
    ch = SC_CHUNK
    per_w = n // nw
    assert per_w * nw == n and per_w % ch == 0

    @functools.partial(
        pl.kernel, mesh=plsc.VectorSubcoreMesh(core_axis_name="c", subcore_axis_name="s"),
        out_type=jax.ShapeDtypeStruct((n_out, w), rows.dtype),
        scratch_types=[pltpu.VMEM((nk, ch), jnp.int32), pltpu.VMEM((ch, w), rows.dtype),
                       pltpu.SemaphoreType.DMA])
    def scatter(rows_hbm, pos_hbm, out_hbm, idx_v, rows_v, sem):
        base = (lax.axis_index("s") * nc + lax.axis_index("c")) * per_w

        @pl.loop(0, per_w // ch)
        def _(ci):
            off = pl.multiple_of(base + ci * ch, ch)
            pltpu.sync_copy(pos_hbm.at[:, pl.ds(off, ch)], idx_v)
            pltpu.sync_copy(rows_hbm.at[pl.ds(off, ch)], rows_v)
            copies = [pltpu.make_async_copy(rows_v, out_hbm.at[idx_v.at[k]], sem) for k in range(nk)]
            for cp in copies:
                cp.start()
            for cp in copies:
                cp.wait()

    return scatter(rows, pos)


def _sc_gather_rows(table, idx):
    nc, nw = _sc_workers()
    n = idx.shape[0]
    w = table.shape[1]
    ch = SC_CHUNK
    per_w = n // nw
    assert per_w * nw == n and per_w % ch == 0
    nch = per_w // ch

    @functools.partial(
        pl.kernel, mesh=plsc.VectorSubcoreMesh(core_axis_name="c", subcore_axis_name="s"),
        out_type=jax.ShapeDtypeStruct((n, w), table.dtype),
        scratch_types=[pltpu.VMEM((nch, ch), jnp.int32), pltpu.VMEM((ch, w), table.dtype),
                       pltpu.SemaphoreType.DMA])
    def gather(table_hbm, idx_hbm, out_hbm, idx_v, rows_v, sem):
        wid = lax.axis_index("s") * nc + lax.axis_index("c")
        base = wid * per_w
        pltpu.sync_copy(idx_hbm.at[pl.ds(wid * nch, nch)], idx_v)

        @pl.loop(0, nch)
        def _(ci):
            off = pl.multiple_of(base + ci * ch, ch)
            cp = pltpu.make_async_copy(table_hbm.at[idx_v.at[ci]], rows_v, sem)
            cp.start()
            cp.wait()
            pltpu.sync_copy(rows_v, out_hbm.at[pl.ds(off, ch)])

    return gather(table, idx.reshape(n // ch, ch))


def _expert_kernel(te_ref, nu_ref, xs_ref, wgu_ref, wd_ref, y_ref, wgu_b, wd_b):
    i = pl.program_id(0)

    @pl.when(i < nu_ref[0])
    def _():
        @pl.when((i == 0) | (te_ref[i] != te_ref[jnp.maximum(i - 1, 0)]))
        def _():
            wgu_b[...] = wgu_ref[0].astype(BF16)
            wd_b[...] = wd_ref[0].astype(BF16)

        lo, hi = _unpack_halves(xs_ref[...])
        half = wgu_b.shape[0] // 2
        gu = _dot(lo.astype(BF16), wgu_b[:half, :]) + _dot(hi.astype(BF16), wgu_b[half:, :])
        act = _silu(gu[:, :EXPERT_DIM]) * gu[:, EXPERT_DIM:]
        y_ref[...] = _pack_halves(_dot(act.astype(BF16), wd_b[...]))


def _experts(tile_e, n_used, xs, wgu, wd):
    r, w = xs.shape
    tm = EXPERT_TILE
    d = wgu.shape[1]
    rows = pl.BlockSpec((tm, w), lambda i, te, nu: (jnp.minimum(i, nu[0] - 1), 0))
    return pl.pallas_call(
        _expert_kernel,
        grid_spec=pltpu.PrefetchScalarGridSpec(
            num_scalar_prefetch=2,
            grid=(r // tm,),
            in_specs=[rows,
                      pl.BlockSpec((1, d, 2 * EXPERT_DIM), lambda i, te, nu: (te[i], 0, 0)),
                      pl.BlockSpec((1, EXPERT_DIM, d), lambda i, te, nu: (te[i], 0, 0))],
            out_specs=rows,
            scratch_shapes=[pltpu.VMEM((d, 2 * EXPERT_DIM), BF16), pltpu.VMEM((EXPERT_DIM, d), BF16)]),
        out_shape=jax.ShapeDtypeStruct((r, w), jnp.int32),
        compiler_params=_params("arbitrary"),
        name="experts",
    )(tile_e, n_used, xs, wgu, wd)


def _final_kernel(yk_ref, wk_ref, h_ref, x1_ref, mod_ref, wsgu_ref, wsd_ref, out_ref):
    half = yk_ref.shape[2]
    acc_lo = jnp.zeros((yk_ref.shape[1], half), F32)
    acc_hi = jnp.zeros((yk_ref.shape[1], half), F32)
    for k in range(TOP_K):
        lo, hi = _unpack_halves(yk_ref[k])
        w = wk_ref[:, k:k + 1]
        acc_lo = acc_lo + jnp.where(w != 0.0, w * lo, 0.0)
        acc_hi = acc_hi + jnp.where(w != 0.0, w * hi, 0.0)
    routed = jnp.concatenate([acc_lo, acc_hi], axis=1)
    sgu = _dot(h_ref[...], wsgu_ref[...])
    act = _silu(sgu[:, :SHARED_DIM]) * sgu[:, SHARED_DIM:]
    shared = _dot(act.astype(BF16), wsd_ref[...])
    out_ref[...] = x1_ref[...] + mod_ref[0, 5:6, :] * (routed + shared)


def _final(yk, wk_t, h2, x1, mod, wsgu, wsd, tiles_per_batch):
    t, d = h2.shape
    tm = ROW_TILE
    row = lambda w: pl.BlockSpec((tm, w), lambda i: (i, 0))
    return pl.pallas_call(
        _final_kernel,
        grid=(t // tm,),
        in_specs=[pl.BlockSpec((TOP_K, tm, d // 2), lambda i: (0, i, 0)),
                  row(TOP_K), row(d), row(d),
                  pl.BlockSpec((1, 6, d), lambda i: (i // tiles_per_batch, 0, 0)),
                  _const_spec(wsgu.shape), _const_spec(wsd.shape)],
        out_specs=row(d),
        out_shape=jax.ShapeDtypeStruct((t, d), F32),
        compiler_params=_params("arbitrary"),
        name="final",
    )(yk, wk_t, h2, x1, mod, wsgu, wsd)


def _placement():
    pq = np.zeros((LANES, N_HEADS * HEAD_PAD), np.float32)
    pk = np.zeros((LANES, N_HEADS * HEAD_PAD), np.float32)
    cq = np.zeros((1, N_HEADS * HEAD_PAD), np.float32)
    ck = np.zeros((1, N_HEADS * HEAD_PAD), np.float32)
    for hd in range(N_HEADS):
        for k in range(3):
            pq[k * N_HEADS + hd, hd * HEAD_PAD + AUG0 + k] = 1.0
            ck[0, hd * HEAD_PAD + AUG0 + k] = 1.0
            pk[k * N_HEADS + hd, hd * HEAD_PAD + AUG0 + 3 + k] = -1.0
            cq[0, hd * HEAD_PAD + AUG0 + 3 + k] = 1.0
    return (jnp.asarray(pq, BF16), jnp.asarray(pk, BF16), jnp.asarray(cq), jnp.asarray(ck))


def kernel(x, c, w_ada, b_ada, norm1_g, w_in, w_dw, b_dw, conv_gn_g, conv_gn_b, w_conv_out,
           q_norm_g, k_norm_g, b_forget, w_attn_out, w_out, norm2_g, w_router, router_bias,
           w_experts_gate_up, w_experts_down, w_shared_gate_up, w_shared_down):
    depth = w_ada.shape[0]
    b, s, d = x.shape

    pq, pk, cq, ck = _placement()
    tri = jnp.asarray(np.tril(np.ones((ROW_TILE, ROW_TILE), np.float32)), BF16)
    grp = np.arange(CONV_DIM) // (CONV_DIM // CONV_GROUPS)
    gg = jnp.asarray((grp[:, None] == grp[None, :]).astype(np.float32) / (CONV_DIM // CONV_GROUPS), BF16)
    c_pad = jnp.pad(c, ((0, SUBLANES - b), (0, 0)))
    tri_pos = jnp.asarray(np.triu(np.ones((POS_TILE, POS_TILE), np.float32)), BF16)
    lstrict = jnp.asarray(np.tril(np.ones((N_EXPERTS, N_EXPERTS), np.float32), -1), BF16)

    for l in range(depth):
        mod = _ada(c_pad, w_ada[l], b_ada[l][None, :])[:b].reshape(b, 6, d)

        bf = jnp.pad(b_forget[l][None, :], ((0, 0), (0, LANES - N_HEADS)))
        gpad = lambda g, sc: jnp.tile(jnp.pad(g * sc, (0, HEAD_PAD - HEAD_DIM)), N_HEADS)[None, :]
        qscale = HEAD_DIM ** -0.5
        u, qa, ka, v, sgc, sga, cum = _inproj(
            x, mod, norm1_g[l][None, :], _inproj_weight(w_in[l]),
            bf, gpad(q_norm_g[l], qscale), gpad(k_norm_g[l], 1.0),
            pq, pk, cq, ck, tri)

        flat = lambda a: a[:, :, :N_HEADS].transpose(0, 2, 1).reshape(-1)
        cs = flat(cum[:, 0::ATTN_TILE])
        ce = flat(cum[:, ATTN_TILE - 1::ATTN_TILE])
        bound = (1.02 * HEAD_DIM * qscale) * jnp.max(jnp.abs(q_norm_g[l])) * jnp.max(jnp.abs(k_norm_g[l]))
        o = _attention(cs, ce, bound.reshape(1), qa, ka, v)

        wdw = jnp.pad(w_dw[l], ((0, CONV_HALO - CONV_WIDTH), (0, 0)))
        x1 = _merge(u, o, sgc, sga, x, mod, wdw, b_dw[l][None, :], conv_gn_g[l][None, :],
                    conv_gn_b[l][None, :], gg, w_conv_out[l].astype(BF16), w_attn_out[l].astype(BF16),
                    w_out[l].astype(BF16))

        wr = w_router[l].T
        wr_hi = wr.astype(BF16)
        wr_lo = (wr - wr_hi.astype(F32)).astype(BF16)
        h2, h2w, comb_t, cnt = _router(x1, mod, norm2_g[l][None, :], wr_hi, wr_lo, router_bias[l][:, None])

        t = b * s
        n_tiles = (t * TOP_K) // EXPERT_TILE + N_EXPERTS
        posk, wk, gend = _positions(comb_t, cnt, tri_pos, lstrict, n_tiles * EXPERT_TILE)
        seg_end = gend[:, 0].astype(jnp.int32)
        n_used = seg_end[-1:] // EXPERT_TILE
        tile_start = jnp.arange(n_tiles, dtype=jnp.int32) * EXPERT_TILE
        tile_start = jnp.minimum(tile_start, seg_end[-1] - EXPERT_TILE)
        tile_e = jnp.sum((seg_end[None, :] <= tile_start[:, None]).astype(jnp.int32), axis=1)

        xs = _sc_scatter_rows(h2w.reshape(t, d // 2), posk, n_tiles * EXPERT_TILE)
        ys = _experts(tile_e, n_used, xs, w_experts_gate_up[l], w_experts_down[l])
        yk = _sc_gather_rows(ys, posk.reshape(-1)).reshape(TOP_K, t, d // 2)
        out = _final(yk, wk.T, h2.reshape(t, d), x1.reshape(t, d), mod,
                     w_shared_gate_up[l].astype(BF16), w_shared_down[l].astype(BF16), s // ROW_TILE)
        x = out.reshape(b, s, d)
    return x
```

```python
import functools

import numpy as np
import jax
import jax.numpy as jnp
from jax import lax
from jax.experimental import pallas as pl
from jax.experimental.pallas import tpu as pltpu
from jax.experimental.pallas import tpu_sc as plsc

F32 = jnp.float32
BF16 = jnp.bfloat16

CONV_DIM = 512
CONV_WIDTH = 31
CONV_GROUPS = 8
N_HEADS = 8
HEAD_DIM = 64
ATTN_DIM = N_HEADS * HEAD_DIM
N_EXPERTS = 64
TOP_K = 8
N_GROUPS = 8
TOPK_GROUPS = 4
EXPERT_DIM = 256
SHARED_DIM = 256
ROUTED_SCALE = 2.5
EPS = 1e-6

LANES = 128
SUBLANES = 8
HEAD_PAD = LANES
AUG0 = HEAD_DIM
VMEM_LIMIT = 56 * 1024 * 1024

ROW_TILE = 512
ATTN_TILE = 1024
DIAG_PARTS = 2
EXPERT_TILE = 512
POS_TILE = 1024
SC_CHUNK = 128
CONV_HALO = 32

NEG_BIG = -1e30
EXP_UNDERFLOW = 104.0
FIXED_SHIFT_BOUND = 40.0


def _dot(a, b):
    return jnp.dot(a, b, preferred_element_type=F32)


def _dot_nt(a, b):
    return lax.dot_general(a, b, (((1,), (1,)), ((), ())), preferred_element_type=F32)


def _split2(x):
    hi = x.astype(BF16)
    lo = (x - hi.astype(F32)).astype(BF16)
    return hi, lo


def _split3(x):
    hi = x.astype(BF16)
    r = x - hi.astype(F32)
    mid = r.astype(BF16)
    lo = (r - mid.astype(F32)).astype(BF16)
    return hi, mid, lo


def _pack_halves(v):
    n = v.shape[1] // 2
    lo = lax.bitcast_convert_type(v[:, :n].astype(BF16).astype(F32), jnp.uint32)
    hi = lax.bitcast_convert_type(v[:, n:].astype(BF16).astype(F32), jnp.uint32)
    return lax.bitcast_convert_type(hi | lax.shift_right_logical(lo, jnp.uint32(16)), jnp.int32)


def _unpack_halves(w):
    u = lax.bitcast_convert_type(w, jnp.uint32)
    lo = lax.bitcast_convert_type(lax.shift_left(u, jnp.uint32(16)), F32)
    hi = lax.bitcast_convert_type(u & jnp.uint32(0xFFFF0000), F32)
    return lo, hi


def _sigmoid(x):
    return 1.0 / (1.0 + jnp.exp(-x))


def _silu(x):
    return x * _sigmoid(x)


def _params(*sem):
    return pltpu.CompilerParams(dimension_semantics=sem, vmem_limit_bytes=VMEM_LIMIT)


def _const_spec(shape):
    n = len(shape)
    return pl.BlockSpec(shape, lambda *_: (0,) * n, pipeline_mode=pl.Buffered(1))


def _ada_kernel(c_ref, w_ref, b_ref, o_ref):
    c = c_ref[...]
    a_hi, a_lo = _split2(_silu(c))
    w_hi, w_lo = _split2(w_ref[...])
    o_ref[...] = _dot(a_hi, w_hi) + _dot(a_hi, w_lo) + _dot(a_lo, w_hi) + b_ref[...]


def _ada(c_pad, w_ada, b_ada):
    d = c_pad.shape[1]
    n = w_ada.shape[1]
    return pl.pallas_call(
        _ada_kernel,
        grid=(n // d,),
        in_specs=[_const_spec(c_pad.shape),
                  pl.BlockSpec((d, d), lambda j: (0, j)),
                  pl.BlockSpec((1, d), lambda j: (0, j))],
        out_specs=pl.BlockSpec((c_pad.shape[0], d), lambda j: (0, j)),
        out_shape=jax.ShapeDtypeStruct((c_pad.shape[0], n), F32),
        compiler_params=_params("arbitrary"),
        name="ada",
    )(c_pad, w_ada, b_ada)


def _lane_pieces(x):
    hi, mid, lo = _split3(x)
    return (hi.astype(F32) + pltpu.roll(mid.astype(F32), N_HEADS, 1)
            + pltpu.roll(lo.astype(F32), 2 * N_HEADS, 1)).astype(BF16)


def _head_tile(p, col0, hd):
    per = HEAD_PAD // HEAD_DIM
    g, part = divmod(hd, per)
    blk = p[:, col0 + g * HEAD_PAD:col0 + (g + 1) * HEAD_PAD]
    if part:
        blk = pltpu.roll(blk, HEAD_PAD - part * HEAD_DIM, 1)
    lane = lax.broadcasted_iota(jnp.int32, (1, HEAD_PAD), 1)
    return jnp.where(lane < HEAD_DIM, blk, 0.0)


def _inproj_kernel(x_ref, mod_ref, g1_ref, w_ref, bf_ref, qg_ref, kg_ref, pq_ref, pk_ref, cq_ref, ck_ref, tri_ref,
                   u_ref, qa_ref, ka_ref, v_ref, sgc_ref, sga_ref, cum_ref, carry_ref):
    @pl.when(pl.program_id(1) == 0)
    def _():
        carry_ref[...] = jnp.zeros_like(carry_ref)

    x = x_ref[0]
    ms = jnp.mean(x * x, axis=-1, keepdims=True)
    h = (x * lax.rsqrt(ms + EPS) * g1_ref[...]) * (1.0 + mod_ref[0, 1:2, :]) + mod_ref[0, 0:1, :]
    hb = h.astype(BF16)

    hp = N_HEADS * HEAD_PAD
    o_q = 2 * CONV_DIM
    o_k, o_v = o_q + hp, o_q + 2 * hp
    o_f = o_v + ATTN_DIM
    o_gc = o_f + LANES
    o_ga = o_gc + x.shape[1]
    pc = _dot(hb, w_ref[:, :o_q])
    u_ref[0] = (pc[:, :CONV_DIM] * _sigmoid(pc[:, CONV_DIM:])).astype(BF16)

    z = _dot(hb, w_ref[:, o_f:o_gc]) + bf_ref[...]
    lf = jnp.minimum(z, 0.0) - jnp.log1p(jnp.exp(-jnp.abs(z)))
    lane = lax.broadcasted_iota(jnp.int32, (1, LANES), 1)
    psum = _dot(tri_ref[...], _lane_pieces(jnp.where(lane < N_HEADS, lf, 0.0)))
    cum = (psum + pltpu.roll(psum, LANES - N_HEADS, 1) + pltpu.roll(psum, LANES - 2 * N_HEADS, 1)
           + carry_ref[...])
    cum = jnp.where(lane < N_HEADS, cum, 0.0)
    carry_ref[...] = cum[ROW_TILE - 1:ROW_TILE, :]
    cum_ref[0] = cum

    pieces = _lane_pieces(cum)
    addq = _dot(pieces, pq_ref[...]) + cq_ref[...]
    addk = _dot(pieces, pk_ref[...]) + ck_ref[...]

    pq = _dot(hb, w_ref[:, o_q:o_k])
    pk = _dot(hb, w_ref[:, o_k:o_v])
    pv = _dot(hb, w_ref[:, o_v:o_f])
    inv_hd = 1.0 / HEAD_DIM
    vone = (lax.broadcasted_iota(jnp.int32, (1, HEAD_PAD), 1) == HEAD_DIM).astype(F32)
    for hd in range(N_HEADS):
        sl = slice(hd * HEAD_PAD, (hd + 1) * HEAD_PAD)
        qb = pq[:, sl]
        qn = qb * lax.rsqrt(jnp.sum(qb * qb, axis=-1, keepdims=True) * inv_hd + EPS) * qg_ref[:, sl]
        qa_ref[0, hd] = (qn + addq[:, sl]).astype(BF16)
        kb = pk[:, sl]
        kn = kb * lax.rsqrt(jnp.sum(kb * kb, axis=-1, keepdims=True) * inv_hd + EPS) * kg_ref[:, sl]
        ka_ref[0, hd] = (kn + addk[:, sl]).astype(BF16)
        v_ref[0, hd] = (_head_tile(pv, 0, hd) + vone).astype(BF16)

    sgc_ref[0] = _sigmoid(_dot(hb, w_ref[:, o_gc:o_ga])).astype(BF16)
    sga_ref[0] = _sigmoid(_dot(hb, w_ref[:, o_ga:])).astype(BF16)


def _inproj_weight(wi):
    d = wi.shape[0]
    o_q = 2 * CONV_DIM
    o_f = o_q + 3 * ATTN_DIM
    o_gc = o_f + N_HEADS
    heads = lambda w: jnp.pad(w.reshape(d, N_HEADS, HEAD_DIM),
                              ((0, 0), (0, 0), (0, HEAD_PAD - HEAD_DIM))).reshape(d, N_HEADS * HEAD_PAD)
    return jnp.concatenate(
        [wi[:, :o_q], heads(wi[:, o_q:o_q + ATTN_DIM]), heads(wi[:, o_q + ATTN_DIM:o_q + 2 * ATTN_DIM]),
         wi[:, o_q + 2 * ATTN_DIM:o_f], jnp.pad(wi[:, o_f:o_gc], ((0, 0), (0, LANES - N_HEADS))),
         wi[:, o_gc:]], axis=1).astype(BF16)


def _inproj(x, mod, g1, w, bf, qg, kg, pq, pk, cq, ck, tri):
    b, s, d = x.shape
    tm = ROW_TILE
    row = lambda w: pl.BlockSpec((1, tm, w), lambda bi, i: (bi, i, 0))
    head = pl.BlockSpec((1, N_HEADS, tm, HEAD_PAD), lambda bi, i: (bi, 0, i, 0))
    consts = [g1, w, bf, qg, kg, pq, pk, cq, ck, tri]
    return pl.pallas_call(
        _inproj_kernel,
        grid=(b, s // tm),
        in_specs=[row(d), pl.BlockSpec((1, 6, d), lambda bi, i: (bi, 0, 0))]
                 + [_const_spec(a.shape) for a in consts],
        out_specs=[row(CONV_DIM), head, head, head, row(d), row(d), row(LANES)],
        out_shape=[jax.ShapeDtypeStruct((b, s, CONV_DIM), BF16),
                   jax.ShapeDtypeStruct((b, N_HEADS, s, HEAD_PAD), BF16),
                   jax.ShapeDtypeStruct((b, N_HEADS, s, HEAD_PAD), BF16),
                   jax.ShapeDtypeStruct((b, N_HEADS, s, HEAD_PAD), BF16),
                   jax.ShapeDtypeStruct((b, s, d), BF16),
                   jax.ShapeDtypeStruct((b, s, d), BF16),
                   jax.ShapeDtypeStruct((b, s, LANES), F32)],
        scratch_shapes=[pltpu.VMEM((1, LANES), F32)],
        compiler_params=_params("arbitrary", "arbitrary"),
        name="inproj",
    )(x, mod, *consts)


def _attn_kernel(cs_ref, ce_ref, prm_ref, q_ref, k_ref, v_ref, o_ref, acc_ref, m_ref):
    t = ATTN_TILE
    nb = q_ref.shape[2] // t
    base = (pl.program_id(0) * pl.num_programs(1) + pl.program_id(1)) * nb
    bound = prm_ref[0]
    thresh = -(EXP_UNDERFLOW + 2.0 * bound)
    causal = lax.broadcasted_iota(jnp.int32, (t, t), 0) >= lax.broadcasted_iota(jnp.int32, (t, t), 1)

    dp = DIAG_PARTS

    def first_block(i):
        c0 = cs_ref[dp * (base + i)]
        return lax.fori_loop(
            0, i, lambda j, n: n + (c0 - ce_ref[dp * (base + j) + dp - 1] < thresh).astype(jnp.int32), 0)

    def scores(q, j, masked):
        k0 = pl.multiple_of(j * t, t)
        s = _dot_nt(q, k_ref[0, 0, pl.ds(k0, t), :])
        if masked:
            s = jnp.where(causal, s, NEG_BIG)
        return s, v_ref[0, 0, pl.ds(k0, t), :]

    def finish(q0):
        acc = acc_ref[...]
        o_ref[0, pl.ds(q0, t), :] = (acc / acc[:, HEAD_DIM:HEAD_DIM + 1]).astype(BF16)

    def fixed_shift(i, carry):
        q0 = pl.multiple_of(i * t, t)
        q = q_ref[0, 0, pl.ds(q0, t), :]

        acc_ref[...] = jnp.zeros_like(acc_ref)

        def weighted(j, masked):
            s, vb = scores(q, j, masked)
            return _dot(jnp.exp(s).astype(BF16), vb)

        def kv(j, c):
            acc_ref[...] += weighted(j, False)
            return c

        lax.fori_loop(first_block(i), i - 1, kv, 0)
        rp = t // DIAG_PARTS

        @pl.when(i == 0)
        def _():
            for r in range(DIAG_PARTS):
                nk = (r + 1) * rp
                s = _dot_nt(q[r * rp:(r + 1) * rp, :], k_ref[0, 0, pl.ds(q0, nk), :])
                seen = (lax.broadcasted_iota(jnp.int32, (rp, nk), 0) + r * rp
                        >= lax.broadcasted_iota(jnp.int32, (rp, nk), 1))
                p = jnp.exp(jnp.where(seen, s, NEG_BIG)).astype(BF16)
                acc_ref[r * rp:(r + 1) * rp, :] += _dot(p, v_ref[0, 0, pl.ds(q0, nk), :])
            finish(q0)

        @pl.when(i > 0)
        def _():
            nk = t + rp
            for r in range(DIAG_PARTS):
                k0 = pl.multiple_of(q0 - t + r * rp, rp)
                s = _dot_nt(q[r * rp:(r + 1) * rp, :], k_ref[0, 0, pl.ds(k0, nk), :])
                seen = (lax.broadcasted_iota(jnp.int32, (rp, nk), 0) + t
                        >= lax.broadcasted_iota(jnp.int32, (rp, nk), 1))
                p = jnp.exp(jnp.where(seen, s, NEG_BIG)).astype(BF16)
                acc_ref[r * rp:(r + 1) * rp, :] += _dot(p, v_ref[0, 0, pl.ds(k0, nk), :])

            for r in range(1, DIAG_PARTS):
                need = cs_ref[dp * (base + i) + r] - ce_ref[dp * (base + i - 1) + r - 1] >= thresh

                @pl.when(need)
                def _():
                    kp = pl.multiple_of(q0 - t, t)
                    s = _dot_nt(q[r * rp:(r + 1) * rp, :], k_ref[0, 0, pl.ds(kp, r * rp), :])
                    acc_ref[r * rp:(r + 1) * rp, :] += _dot(jnp.exp(s).astype(BF16),
                                                            v_ref[0, 0, pl.ds(kp, r * rp), :])

            finish(q0)

        return carry

    def running_max(i, carry):
        q0 = pl.multiple_of(i * t, t)
        q = q_ref[0, 0, pl.ds(q0, t), :]
        m_ref[...] = jnp.full_like(m_ref, -jnp.inf)
        acc_ref[...] = jnp.zeros_like(acc_ref)

        def step(j, masked):
            s, vb = scores(q, j, masked)
            m_prev = m_ref[...]
            m_new = jnp.maximum(m_prev, jnp.max(s, axis=-1, keepdims=True))
            p = jnp.exp(s - m_new)
            acc_ref[...] = jnp.exp(m_prev - m_new) * acc_ref[...] + _dot(p.astype(BF16), vb)
            m_ref[...] = m_new

        def kv(j, c):
            step(j, False)
            return c

        lax.fori_loop(first_block(i), i, kv, 0)
        step(i, True)
        finish(q0)
        return carry

    @pl.when(bound <= FIXED_SHIFT_BOUND)
    def _():
        lax.fori_loop(0, nb, fixed_shift, 0)

    @pl.when(bound > FIXED_SHIFT_BOUND)
    def _():
        lax.fori_loop(0, nb, running_max, 0)


def _attention(cs, ce, prm, qa, ka, v):
    b, nh, s, hp = qa.shape
    t = ATTN_TILE
    seq = pl.BlockSpec((1, 1, s, hp), lambda bi, hi, *_: (bi, hi, 0, 0))
    return pl.pallas_call(
        _attn_kernel,
        grid_spec=pltpu.PrefetchScalarGridSpec(
            num_scalar_prefetch=3,
            grid=(b, nh),
            in_specs=[seq, seq, seq],
            out_specs=pl.BlockSpec((1, s, hp), lambda bi, hi, *_: (bi, 0, hi)),
            scratch_shapes=[pltpu.VMEM((t, hp), F32), pltpu.VMEM((t, 1), F32)]),
        out_shape=jax.ShapeDtypeStruct((b, s, nh * hp), BF16),
        compiler_params=_params("arbitrary", "arbitrary"),
        name="attn",
    )(cs, ce, prm, qa, ka, v)


def _merge_kernel(u_ref, halo_ref, o_ref, sgc_ref, sga_ref, x_ref, mod_ref, wdw_ref, bdw_ref,
                  gng_ref, gnb_ref, gg_ref, wco_ref, wao_ref, wout_ref, x1_ref, buf_ref):
    tm = ROW_TILE
    halo = halo_ref[0].astype(F32)
    halo = jnp.where(pl.program_id(1) == 0, jnp.zeros_like(halo), halo)
    ucur = u_ref[0].astype(F32)
    for cb in range(CONV_DIM // LANES):
        buf_ref[cb, 0:CONV_HALO, :] = halo[:, cb * LANES:(cb + 1) * LANES]
        buf_ref[cb, CONV_HALO:, :] = ucur[:, cb * LANES:(cb + 1) * LANES]

    base = CONV_HALO - (CONV_WIDTH - 1)
    ys = []
    for cb in range(CONV_DIM // LANES):
        acc = jnp.zeros((tm, LANES), F32)
        for j in range(CONV_WIDTH):
            acc = acc + wdw_ref[j:j + 1, cb * LANES:(cb + 1) * LANES] * buf_ref[cb, base + j:base + j + tm, :]
        ys.append(acc)
    y = jnp.concatenate(ys, axis=1) + bdw_ref[...]

    gg = gg_ref[...]
    y_hi, y_lo = _split2(y)
    dlt = y - (_dot(y_hi, gg) + _dot(y_lo, gg))
    s_hi, s_lo = _split2(dlt * dlt)
    var = _dot(s_hi, gg) + _dot(s_lo, gg)
    yn = dlt * lax.rsqrt(var + EPS) * gng_ref[...] + gnb_ref[...]
    y_conv = _dot(_silu(yn).astype(BF16), wco_ref[...])

    per = HEAD_PAD // HEAD_DIM
    lane = lax.broadcasted_iota(jnp.int32, (1, HEAD_PAD), 1)
    packed = []
    for g in range(N_HEADS // per):
        tile = o_ref[0, :, g * per * HEAD_PAD:(g * per + 1) * HEAD_PAD].astype(F32)
        for part in range(1, per):
            nxt = o_ref[0, :, (g * per + part) * HEAD_PAD:(g * per + part + 1) * HEAD_PAD].astype(F32)
            tile = jnp.where(lane < part * HEAD_DIM, tile, pltpu.roll(nxt, part * HEAD_DIM, 1))
        packed.append(tile)
    y_attn = _dot(jnp.concatenate(packed, axis=1).astype(BF16), wao_ref[...])
    merged = sgc_ref[0].astype(F32) * y_conv + sga_ref[0].astype(F32) * y_attn
    mix = _dot(merged.astype(BF16), wout_ref[...])
    x1_ref[0] = x_ref[0] + mod_ref[0, 2:3, :] * mix


def _merge(u, o, sgc, sga, x, mod, wdw, bdw, gng, gnb, gg, wco, wao, wout):
    b, s, d = x.shape
    tm = ROW_TILE
    per = tm // CONV_HALO
    row = lambda w: pl.BlockSpec((1, tm, w), lambda bi, i: (bi, i, 0))
    consts = [wdw, bdw, gng, gnb, gg, wco, wao, wout]
    return pl.pallas_call(
        _merge_kernel,
        grid=(b, s // tm),
        in_specs=[row(CONV_DIM),
                  pl.BlockSpec((1, CONV_HALO, CONV_DIM),
                               lambda bi, i: (bi, jnp.maximum(i * per - 1, 0), 0)),
                  row(o.shape[2]), row(d), row(d), row(d),
                  pl.BlockSpec((1, 6, d), lambda bi, i: (bi, 0, 0))]
                 + [_const_spec(a.shape) for a in consts],
        out_specs=row(d),
        out_shape=jax.ShapeDtypeStruct((b, s, d), F32),
        scratch_shapes=[pltpu.VMEM((CONV_DIM // LANES, CONV_HALO + tm, LANES), F32)],
        compiler_params=_params("arbitrary", "arbitrary"),
        name="merge",
    )(u, u, o, sgc, sga, x, mod, *consts)


def _router_kernel(x1_ref, mod_ref, g2_ref, wr_hi_ref, wr_lo_ref, rb_ref, h2_ref, h2w_ref, comb_ref, cnt_ref):
    x = x1_ref[0]
    ms = jnp.mean(x * x, axis=-1, keepdims=True)
    h = (x * lax.rsqrt(ms + EPS) * g2_ref[...]) * (1.0 + mod_ref[0, 4:5, :]) + mod_ref[0, 3:4, :]
    h2_ref[0] = h.astype(BF16)
    h2w_ref[0] = _pack_halves(h)

    h_hi, h_lo = _split2(h)
    logits = _dot_nt(wr_hi_ref[...], h_hi) + _dot_nt(wr_hi_ref[...], h_lo) + _dot_nt(wr_lo_ref[...], h_hi)
    scores = _sigmoid(logits)
    biased = scores + rb_ref[...]

    per = N_EXPERTS // N_GROUPS
    rows = lax.broadcasted_iota(jnp.int32, (per, biased.shape[1]), 0)
    gscore = []
    for g in range(N_GROUPS):
        blk = biased[g * per:(g + 1) * per, :]
        top1 = jnp.max(blk, axis=0, keepdims=True)
        first = jnp.min(jnp.where(blk == top1, rows, per), axis=0, keepdims=True)
        top2 = jnp.max(jnp.where(rows == first, -jnp.inf, blk), axis=0, keepdims=True)
        gscore.append(top1 + top2)

    cand = []
    for g in range(N_GROUPS):
        rank = jnp.zeros_like(gscore[g], dtype=jnp.int32)
        for g2 in range(N_GROUPS):
            if g2 == g:
                continue
            ahead = gscore[g2] > gscore[g]
            if g2 < g:
                ahead = ahead | (gscore[g2] == gscore[g])
            rank = rank + ahead.astype(jnp.int32)
        keep = rank < TOPK_GROUPS
        cand.append(jnp.where(keep, biased[g * per:(g + 1) * per, :], -jnp.inf))
    cand = jnp.concatenate(cand, axis=0)

    eidx = lax.broadcasted_iota(jnp.int32, cand.shape, 0)
    work = cand
    for _ in range(TOP_K):
        top = jnp.max(work, axis=0, keepdims=True)
        first = jnp.min(jnp.where(work == top, eidx, N_EXPERTS), axis=0, keepdims=True)
        work = jnp.where(eidx == first, -jnp.inf, work)
    sel = (work != cand) & (cand > -jnp.inf)
    w = jnp.where(sel, scores, 0.0)
    comb = w / jnp.sum(w, axis=0, keepdims=True) * ROUTED_SCALE
    comb_ref[...] = comb

    @pl.when((pl.program_id(0) == 0) & (pl.program_id(1) == 0))
    def _():
        cnt_ref[...] = jnp.zeros_like(cnt_ref)

    hit = jnp.where(comb != 0.0, 1.0, 0.0).astype(BF16)
    cnt_ref[...] += _dot(hit, jnp.ones((hit.shape[1], LANES), BF16))


def _router(x1, mod, g2, wr_hi, wr_lo, rb):
    b, s, d = x1.shape
    tm = ROW_TILE
    nt = s // tm
    return pl.pallas_call(
        _router_kernel,
        grid=(b, nt),
        in_specs=[pl.BlockSpec((1, tm, d), lambda bi, i: (bi, i, 0)),
                  pl.BlockSpec((1, 6, d), lambda bi, i: (bi, 0, 0)),
                  _const_spec(g2.shape), _const_spec(wr_hi.shape), _const_spec(wr_lo.shape),
                  _const_spec(rb.shape)],
        out_specs=[pl.BlockSpec((1, tm, d), lambda bi, i: (bi, i, 0)),
                   pl.BlockSpec((1, tm, d // 2), lambda bi, i: (bi, i, 0)),
                   pl.BlockSpec((N_EXPERTS, tm), lambda bi, i: (0, bi * nt + i)),
                   pl.BlockSpec((N_EXPERTS, LANES), lambda bi, i: (0, 0))],
        out_shape=[jax.ShapeDtypeStruct((b, s, d), BF16),
                   jax.ShapeDtypeStruct((b, s, d // 2), jnp.int32),
                   jax.ShapeDtypeStruct((N_EXPERTS, b * s), F32),
                   jax.ShapeDtypeStruct((N_EXPERTS, LANES), F32)],
        compiler_params=_params("arbitrary", "arbitrary"),
        name="router",
    )(x1, mod, g2, wr_hi, wr_lo, rb)


def _pos_kernel(comb_ref, cnt_ref, tri_ref, lstrict_ref, posk_ref, wk_ref, gend_ref, base_ref, *, spare_row):
    tp = comb_ref.shape[1]
    comb = comb_ref[...]
    sel = comb != 0.0
    selb = jnp.where(sel, 1.0, 0.0).astype(BF16)

    @pl.when(pl.program_id(0) == 0)
    def _():
        seg = jnp.floor((cnt_ref[...] + (EXPERT_TILE - 1.0)) * (1.0 / EXPERT_TILE)) * EXPERT_TILE
        s_hi, s_mid, s_lo = _split3(seg)
        ls = lstrict_ref[...]
        start = _dot(ls, s_hi) + _dot(ls, s_mid) + _dot(ls, s_lo)
        base_ref[...] = start
        gend_ref[...] = start + seg

    rank = _dot(selb, tri_ref[...])
    pos = base_ref[:, 0:1] + rank - 1.0
    base_ref[...] += _dot(selb, jnp.ones((tp, LANES), BF16))
    slot = _dot(lstrict_ref[...], selb)
    rows_p, rows_w = [], []
    for k in range(TOP_K):
        m = sel & (slot == k)
        rows_p.append(jnp.sum(jnp.where(m, pos - spare_row, 0.0), axis=0, keepdims=True) + spare_row)
        rows_w.append(jnp.sum(jnp.where(m, comb, 0.0), axis=0, keepdims=True))
    posk_ref[...] = jnp.concatenate(rows_p, axis=0).astype(jnp.int32)
    wk_ref[...] = jnp.concatenate(rows_w, axis=0)


def _positions(comb_t, cnt, tri, lstrict, n_rows):
    ne, t = comb_t.shape
    tp = POS_TILE
    tok = lambda rows: pl.BlockSpec((rows, tp), lambda i: (0, i))
    return pl.pallas_call(
        functools.partial(_pos_kernel, spare_row=float(n_rows - 1)),
        grid=(t // tp,),
        in_specs=[tok(ne), _const_spec(cnt.shape), _const_spec(tri.shape), _const_spec(lstrict.shape)],
        out_specs=[tok(TOP_K), tok(TOP_K), pl.BlockSpec((ne, LANES), lambda i: (0, 0))],
        out_shape=[jax.ShapeDtypeStruct((TOP_K, t), jnp.int32),
                   jax.ShapeDtypeStruct((TOP_K, t), F32),
                   jax.ShapeDtypeStruct((ne, LANES), F32)],
        scratch_shapes=[pltpu.VMEM((ne, LANES), F32)],
        compiler_params=_params("arbitrary"),
        name="positions",
    )(comb_t, cnt, tri, lstrict)


def _sc_workers():
    info = plsc.get_sparse_core_info()
    return info.num_cores, info.num_cores * info.num_subcores


def _sc_scatter_rows(rows, pos, n_out):
    nc, nw = _sc_workers()
    n, w = rows.shape
    nk = pos.shape[0]
    ch = SC_CHUNK
    per_w = n // nw
    assert per_w * nw == n and per_w % ch == 0

    @functools.partial(
        pl.kernel, mesh=plsc.VectorSubcoreMesh(core_axis_name="c", subcore_axis_name="s"),
        out_type=jax.ShapeDtypeStruct((n_out, w), rows.dtype),
        scratch_types=[pltpu.VMEM((nk, ch), jnp.int32), pltpu.VMEM((ch, w), rows.dtype),
                       pltpu.SemaphoreType.DMA])
    def scatter(rows_hbm, pos_hbm, out_hbm, idx_v, rows_v, sem):
        base = (lax.axis_index("s") * nc + lax.axis_index("c")) * per_w

        @pl.loop(0, per_w // ch)
        def _(ci):
            off = pl.multiple_of(base + ci * ch, ch)
            pltpu.sync_copy(pos_hbm.at[:, pl.ds(off, ch)], idx_v)
            pltpu.sync_copy(rows_hbm.at[pl.ds(off, ch)], rows_v)
            copies = [pltpu.make_async_copy(rows_v, out_hbm.at[idx_v.at[k]], sem) for k in range(nk)]
            for cp in copies:
                cp.start()
            for cp in copies:
                cp.wait()

    return scatter(rows, pos)


def _sc_gather_rows(table, idx):
    nc, nw = _sc_workers()
    n = idx.shape[0]
    w = table.shape[1]
    ch = SC_CHUNK
    per_w = n // nw
    assert per_w * nw == n and per_w % ch == 0
    nch = per_w // ch

    @functools.partial(
        pl.kernel, mesh=plsc.VectorSubcoreMesh(core_axis_name="c", subcore_axis_name="s"),
        out_type=jax.ShapeDtypeStruct((n, w), table.dtype),
        scratch_types=[pltpu.VMEM((nch, ch), jnp.int32), pltpu.VMEM((ch, w), table.dtype),
                       pltpu.SemaphoreType.DMA])
    def gather(table_hbm, idx_hbm, out_hbm, idx_v, rows_v, sem):
        wid = lax.axis_index("s") * nc + lax.axis_index("c")
        base = wid * per_w
        pltpu.sync_copy(idx_hbm.at[pl.ds(wid * nch, nch)], idx_v)

        @pl.loop(0, nch)
        def _(ci):
            off = pl.multiple_of(base + ci * ch, ch)
            cp = pltpu.make_async_copy(table_hbm.at[idx_v.at[ci]], rows_v, sem)
            cp.start()
            cp.wait()
            pltpu.sync_copy(rows_v, out_hbm.at[pl.ds(off, ch)])

    return gather(table, idx.reshape(n // ch, ch))


def _expert_kernel(te_ref, nu_ref, xs_ref, wgu_ref, wd_ref, y_ref, wgu_b, wd_b):
    i = pl.program_id(0)

    @pl.when(i < nu_ref[0])
    def _():
        @pl.when((i == 0) | (te_ref[i] != te_ref[jnp.maximum(i - 1, 0)]))
        def _():
            wgu_b[...] = wgu_ref[0].astype(BF16)
            wd_b[...] = wd_ref[0].astype(BF16)

        lo, hi = _unpack_halves(xs_ref[...])
        half = wgu_b.shape[0] // 2
        gu = _dot(lo.astype(BF16), wgu_b[:half, :]) + _dot(hi.astype(BF16), wgu_b[half:, :])
        act = _silu(gu[:, :EXPERT_DIM]) * gu[:, EXPERT_DIM:]
        y_ref[...] = _pack_halves(_dot(act.astype(BF16), wd_b[...]))


def _experts(tile_e, n_used, xs, wgu, wd):
    r, w = xs.shape
    tm = EXPERT_TILE
    d = wgu.shape[1]
    rows = pl.BlockSpec((tm, w), lambda i, te, nu: (jnp.minimum(i, nu[0] - 1), 0))
    return pl.pallas_call(
        _expert_kernel,
        grid_spec=pltpu.PrefetchScalarGridSpec(
            num_scalar_prefetch=2,
            grid=(r // tm,),
            in_specs=[rows,
                      pl.BlockSpec((1, d, 2 * EXPERT_DIM), lambda i, te, nu: (te[i], 0, 0)),
                      pl.BlockSpec((1, EXPERT_DIM, d), lambda i, te, nu: (te[i], 0, 0))],
            out_specs=rows,
            scratch_shapes=[pltpu.VMEM((d, 2 * EXPERT_DIM), BF16), pltpu.VMEM((EXPERT_DIM, d), BF16)]),
        out_shape=jax.ShapeDtypeStruct((r, w), jnp.int32),
        compiler_params=_params("arbitrary"),
        name="experts",
    )(tile_e, n_used, xs, wgu, wd)


def _final_kernel(yk_ref, wk_ref, h_ref, x1_ref, mod_ref, wsgu_ref, wsd_ref, out_ref):
    half = yk_ref.shape[2]
    acc_lo = jnp.zeros((yk_ref.shape[1], half), F32)
    acc_hi = jnp.zeros((yk_ref.shape[1], half), F32)
    for k in range(TOP_K):
        lo, hi = _unpack_halves(yk_ref[k])
        w = wk_ref[:, k:k + 1]
        acc_lo = acc_lo + jnp.where(w != 0.0, w * lo, 0.0)
        acc_hi = acc_hi + jnp.where(w != 0.0, w * hi, 0.0)
    routed = jnp.concatenate([acc_lo, acc_hi], axis=1)
    sgu = _dot(h_ref[...], wsgu_ref[...])
    act = _silu(sgu[:, :SHARED_DIM]) * sgu[:, SHARED_DIM:]
    shared = _dot(act.astype(BF16), wsd_ref[...])
    out_ref[...] = x1_ref[...] + mod_ref[0, 5:6, :] * (routed + shared)


def _final(yk, wk_t, h2, x1, mod, wsgu, wsd, tiles_per_batch):
    t, d = h2.shape
    tm = ROW_TILE
    row = lambda w: pl.BlockSpec((tm, w), lambda i: (i, 0))
    return pl.pallas_call(
        _final_kernel,
        grid=(t // tm,),
        in_specs=[pl.BlockSpec((TOP_K, tm, d // 2), lambda i: (0, i, 0)),
                  row(TOP_K), row(d), row(d),
                  pl.BlockSpec((1, 6, d), lambda i: (i // tiles_per_batch, 0, 0)),
                  _const_spec(wsgu.shape), _const_spec(wsd.shape)],
        out_specs=row(d),
        out_shape=jax.ShapeDtypeStruct((t, d), F32),
        compiler_params=_params("arbitrary"),
        name="final",
    )(yk, wk_t, h2, x1, mod, wsgu, wsd)


def _placement():
    pq = np.zeros((LANES, N_HEADS * HEAD_PAD), np.float32)
    pk = np.zeros((LANES, N_HEADS * HEAD_PAD), np.float32)
    cq = np.zeros((1, N_HEADS * HEAD_PAD), np.float32)
    ck = np.zeros((1, N_HEADS * HEAD_PAD), np.float32)
    for hd in range(N_HEADS):
        for k in range(3):
            pq[k * N_HEADS + hd, hd * HEAD_PAD + AUG0 + k] = 1.0
            ck[0, hd * HEAD_PAD + AUG0 + k] = 1.0
            pk[k * N_HEADS + hd, hd * HEAD_PAD + AUG0 + 3 + k] = -1.0
            cq[0, hd * HEAD_PAD + AUG0 + 3 + k] = 1.0
    return (jnp.asarray(pq, BF16), jnp.asarray(pk, BF16), jnp.asarray(cq), jnp.asarray(ck))


def kernel(x, c, w_ada, b_ada, norm1_g, w_in, w_dw, b_dw, conv_gn_g, conv_gn_b, w_conv_out,
           q_norm_g, k_norm_g, b_forget, w_attn_out, w_out, norm2_g, w_router, router_bias,
           w_experts_gate_up, w_experts_down, w_shared_gate_up, w_shared_down):
    depth = w_ada.shape[0]
    b, s, d = x.shape

    pq, pk, cq, ck = _placement()
    tri = jnp.asarray(np.tril(np.ones((ROW_TILE, ROW_TILE), np.float32)), BF16)
    grp = np.arange(CONV_DIM) // (CONV_DIM // CONV_GROUPS)
    gg = jnp.asarray((grp[:, None] == grp[None, :]).astype(np.float32) / (CONV_DIM // CONV_GROUPS), BF16)
    c_pad = jnp.pad(c, ((0, SUBLANES - b), (0, 0)))
    tri_pos = jnp.asarray(np.triu(np.ones((POS_TILE, POS_TILE), np.float32)), BF16)
    lstrict = jnp.asarray(np.tril(np.ones((N_EXPERTS, N_EXPERTS), np.float32), -1), BF16)

    for l in range(depth):
        mod = _ada(c_pad, w_ada[l], b_ada[l][None, :])[:b].reshape(b, 6, d)

        bf = jnp.pad(b_forget[l][None, :], ((0, 0), (0, LANES - N_HEADS)))
        gpad = lambda g, sc: jnp.tile(jnp.pad(g * sc, (0, HEAD_PAD - HEAD_DIM)), N_HEADS)[None, :]
        qscale = HEAD_DIM ** -0.5
        u, qa, ka, v, sgc, sga, cum = _inproj(
            x, mod, norm1_g[l][None, :], _inproj_weight(w_in[l]),
            bf, gpad(q_norm_g[l], qscale), gpad(k_norm_g[l], 1.0),
            pq, pk, cq, ck, tri)

        flat = lambda a: a[:, :, :N_HEADS].transpose(0, 2, 1).reshape(-1)
        part = ATTN_TILE // DIAG_PARTS
        cs = flat(cum[:, 0::part])
        ce = flat(cum[:, part - 1::part])
        bound = (1.02 * HEAD_DIM * qscale) * jnp.max(jnp.abs(q_norm_g[l])) * jnp.max(jnp.abs(k_norm_g[l]))
        o = _attention(cs, ce, bound.reshape(1), qa, ka, v)

        wdw = jnp.pad(w_dw[l], ((0, CONV_HALO - CONV_WIDTH), (0, 0)))
        x1 = _merge(u, o, sgc, sga, x, mod, wdw, b_dw[l][None, :], conv_gn_g[l][None, :],
                    conv_gn_b[l][None, :], gg, w_conv_out[l].astype(BF16), w_attn_out[l].astype(BF16),
                    w_out[l].astype(BF16))

        wr = w_router[l].T
        wr_hi = wr.astype(BF16)
        wr_lo = (wr - wr_hi.astype(F32)).astype(BF16)
        h2, h2w, comb_t, cnt = _router(x1, mod, norm2_g[l][None, :], wr_hi, wr_lo, router_bias[l][:, None])

        t = b * s
        n_tiles = (t * TOP_K) // EXPERT_TILE + N_EXPERTS
        posk, wk, gend = _positions(comb_t, cnt, tri_pos, lstrict, n_tiles * EXPERT_TILE)
        seg_end = gend[:, 0].astype(jnp.int32)
        n_used = seg_end[-1:] // EXPERT_TILE
        tile_start = jnp.arange(n_tiles, dtype=jnp.int32) * EXPERT_TILE
        tile_start = jnp.minimum(tile_start, seg_end[-1] - EXPERT_TILE)
        tile_e = jnp.sum((seg_end[None, :] <= tile_start[:, None]).astype(jnp.int32), axis=1)

        xs = _sc_scatter_rows(h2w.reshape(t, d // 2), posk, n_tiles * EXPERT_TILE)
        ys = _experts(tile_e, n_used, xs, w_experts_gate_up[l], w_experts_down[l])
        yk = _sc_gather_rows(ys, posk.reshape(-1)).reshape(TOP_K, t, d // 2)
        out = _final(yk, wk.T, h2.reshape(t, d), x1.reshape(t, d), mod,
                     w_shared_gate_up[l].astype(BF16), w_shared_down[l].astype(BF16), s // ROW_TILE)
        x = out.reshape(b, s, d)
    return x
```

```python
import functools

import numpy as np
import jax
import jax.numpy as jnp
from jax import lax
from jax.experimental import pallas as pl
from jax.experimental.pallas import tpu as pltpu
from jax.experimental.pallas import tpu_sc as plsc

F32 = jnp.float32
BF16 = jnp.bfloat16

CONV_DIM = 512
CONV_WIDTH = 31
CONV_GROUPS = 8
N_HEADS = 8
HEAD_DIM = 64
ATTN_DIM = N_HEADS * HEAD_DIM
N_EXPERTS = 64
TOP_K = 8
N_GROUPS = 8
TOPK_GROUPS = 4
EXPERT_DIM = 256
SHARED_DIM = 256
ROUTED_SCALE = 2.5
EPS = 1e-6

LANES = 128
SUBLANES = 8
HEAD_PAD = LANES
AUG0 = HEAD_DIM
VMEM_LIMIT = 56 * 1024 * 1024

ROW_TILE = 512
ATTN_TILE = 1024
DIAG_PARTS = 4
EXPERT_TILE = 512
POS_TILE = 1024
SC_CHUNK = 128
CONV_HALO = 32

NEG_BIG = -1e30
EXP_UNDERFLOW = 104.0
FIXED_SHIFT_BOUND = 40.0


def _dot(a, b):
    return jnp.dot(a, b, preferred_element_type=F32)


def _dot_nt(a, b):
    return lax.dot_general(a, b, (((1,), (1,)), ((), ())), preferred_element_type=F32)


def _split2(x):
    hi = x.astype(BF16)
    lo = (x - hi.astype(F32)).astype(BF16)
    return hi, lo


def _split3(x):
    hi = x.astype(BF16)
    r = x - hi.astype(F32)
    mid = r.astype(BF16)
    lo = (r - mid.astype(F32)).astype(BF16)
    return hi, mid, lo


def _pack_halves(v):
    n = v.shape[1] // 2
    lo = lax.bitcast_convert_type(v[:, :n].astype(BF16).astype(F32), jnp.uint32)
    hi = lax.bitcast_convert_type(v[:, n:].astype(BF16).astype(F32), jnp.uint32)
    return lax.bitcast_convert_type(hi | lax.shift_right_logical(lo, jnp.uint32(16)), jnp.int32)


def _unpack_halves(w):
    u = lax.bitcast_convert_type(w, jnp.uint32)
    lo = lax.bitcast_convert_type(lax.shift_left(u, jnp.uint32(16)), F32)
    hi = lax.bitcast_convert_type(u & jnp.uint32(0xFFFF0000), F32)
    return lo, hi


def _sigmoid(x):
    return 1.0 / (1.0 + jnp.exp(-x))


def _silu(x):
    return x * _sigmoid(x)


def _params(*sem):
    return pltpu.CompilerParams(dimension_semantics=sem, vmem_limit_bytes=VMEM_LIMIT)


def _const_spec(shape):
    n = len(shape)
    return pl.BlockSpec(shape, lambda *_: (0,) * n, pipeline_mode=pl.Buffered(1))


def _ada_kernel(c_ref, w_ref, b_ref, o_ref):
    c = c_ref[...]
    a_hi, a_lo = _split2(_silu(c))
    w_hi, w_lo = _split2(w_ref[...])
    o_ref[...] = _dot(a_hi, w_hi) + _dot(a_hi, w_lo) + _dot(a_lo, w_hi) + b_ref[...]


def _ada(c_pad, w_ada, b_ada):
    d = c_pad.shape[1]
    n = w_ada.shape[1]
    return pl.pallas_call(
        _ada_kernel,
        grid=(n // d,),
        in_specs=[_const_spec(c_pad.shape),
                  pl.BlockSpec((d, d), lambda j: (0, j)),
                  pl.BlockSpec((1, d), lambda j: (0, j))],
        out_specs=pl.BlockSpec((c_pad.shape[0], d), lambda j: (0, j)),
        out_shape=jax.ShapeDtypeStruct((c_pad.shape[0], n), F32),
        compiler_params=_params("arbitrary"),
        name="ada",
    )(c_pad, w_ada, b_ada)


def _lane_pieces(x):
    hi, mid, lo = _split3(x)
    return (hi.astype(F32) + pltpu.roll(mid.astype(F32), N_HEADS, 1)
            + pltpu.roll(lo.astype(F32), 2 * N_HEADS, 1)).astype(BF16)


def _head_tile(p, col0, hd):
    per = HEAD_PAD // HEAD_DIM
    g, part = divmod(hd, per)
    blk = p[:, col0 + g * HEAD_PAD:col0 + (g + 1) * HEAD_PAD]
    if part:
        blk = pltpu.roll(blk, HEAD_PAD - part * HEAD_DIM, 1)
    lane = lax.broadcasted_iota(jnp.int32, (1, HEAD_PAD), 1)
    return jnp.where(lane < HEAD_DIM, blk, 0.0)


def _inproj_kernel(x_ref, mod_ref, g1_ref, w_ref, bf_ref, qg_ref, kg_ref, pq_ref, pk_ref, cq_ref, ck_ref, tri_ref,
                   u_ref, qa_ref, ka_ref, v_ref, sgc_ref, sga_ref, cum_ref, carry_ref):
    @pl.when(pl.program_id(1) == 0)
    def _():
        carry_ref[...] = jnp.zeros_like(carry_ref)

    x = x_ref[0]
    ms = jnp.mean(x * x, axis=-1, keepdims=True)
    h = (x * lax.rsqrt(ms + EPS) * g1_ref[...]) * (1.0 + mod_ref[0, 1:2, :]) + mod_ref[0, 0:1, :]
    hb = h.astype(BF16)

    hp = N_HEADS * HEAD_PAD
    o_q = 2 * CONV_DIM
    o_k, o_v = o_q + hp, o_q + 2 * hp
    o_f = o_v + ATTN_DIM
    o_gc = o_f + LANES
    o_ga = o_gc + x.shape[1]
    pc = _dot(hb, w_ref[:, :o_q])
    u_ref[0] = (pc[:, :CONV_DIM] * _sigmoid(pc[:, CONV_DIM:])).astype(BF16)

    z = _dot(hb, w_ref[:, o_f:o_gc]) + bf_ref[...]
    lf = jnp.minimum(z, 0.0) - jnp.log1p(jnp.exp(-jnp.abs(z)))
    lane = lax.broadcasted_iota(jnp.int32, (1, LANES), 1)
    psum = _dot(tri_ref[...], _lane_pieces(jnp.where(lane < N_HEADS, lf, 0.0)))
    cum = (psum + pltpu.roll(psum, LANES - N_HEADS, 1) + pltpu.roll(psum, LANES - 2 * N_HEADS, 1)
           + carry_ref[...])
    cum = jnp.where(lane < N_HEADS, cum, 0.0)
    carry_ref[...] = cum[ROW_TILE - 1:ROW_TILE, :]
    cum_ref[0] = cum

    pieces = _lane_pieces(cum)
    addq = _dot(pieces, pq_ref[...]) + cq_ref[...]
    addk = _dot(pieces, pk_ref[...]) + ck_ref[...]

    pq = _dot(hb, w_ref[:, o_q:o_k])
    pk = _dot(hb, w_ref[:, o_k:o_v])
    pv = _dot(hb, w_ref[:, o_v:o_f])
    inv_hd = 1.0 / HEAD_DIM
    vone = (lax.broadcasted_iota(jnp.int32, (1, HEAD_PAD), 1) == HEAD_DIM).astype(F32)
    for hd in range(N_HEADS):
        sl = slice(hd * HEAD_PAD, (hd + 1) * HEAD_PAD)
        qb = pq[:, sl]
        qn = qb * lax.rsqrt(jnp.sum(qb * qb, axis=-1, keepdims=True) * inv_hd + EPS) * qg_ref[:, sl]
        qa_ref[0, hd] = (qn + addq[:, sl]).astype(BF16)
        kb = pk[:, sl]
        kn = kb * lax.rsqrt(jnp.sum(kb * kb, axis=-1, keepdims=True) * inv_hd + EPS) * kg_ref[:, sl]
        ka_ref[0, hd] = (kn + addk[:, sl]).astype(BF16)
        v_ref[0, hd] = (_head_tile(pv, 0, hd) + vone).astype(BF16)

    sgc_ref[0] = _sigmoid(_dot(hb, w_ref[:, o_gc:o_ga])).astype(BF16)
    sga_ref[0] = _sigmoid(_dot(hb, w_ref[:, o_ga:])).astype(BF16)


def _inproj_weight(wi):
    d = wi.shape[0]
    o_q = 2 * CONV_DIM
    o_f = o_q + 3 * ATTN_DIM
    o_gc = o_f + N_HEADS
    heads = lambda w: jnp.pad(w.reshape(d, N_HEADS, HEAD_DIM),
                              ((0, 0), (0, 0), (0, HEAD_PAD - HEAD_DIM))).reshape(d, N_HEADS * HEAD_PAD)
    return jnp.concatenate(
        [wi[:, :o_q], heads(wi[:, o_q:o_q + ATTN_DIM]), heads(wi[:, o_q + ATTN_DIM:o_q + 2 * ATTN_DIM]),
         wi[:, o_q + 2 * ATTN_DIM:o_f], jnp.pad(wi[:, o_f:o_gc], ((0, 0), (0, LANES - N_HEADS))),
         wi[:, o_gc:]], axis=1).astype(BF16)


def _inproj(x, mod, g1, w, bf, qg, kg, pq, pk, cq, ck, tri):
    b, s, d = x.shape
    tm = ROW_TILE
    row = lambda w: pl.BlockSpec((1, tm, w), lambda bi, i: (bi, i, 0))
    head = pl.BlockSpec((1, N_HEADS, tm, HEAD_PAD), lambda bi, i: (bi, 0, i, 0))
    consts = [g1, w, bf, qg, kg, pq, pk, cq, ck, tri]
    return pl.pallas_call(
        _inproj_kernel,
        grid=(b, s // tm),
        in_specs=[row(d), pl.BlockSpec((1, 6, d), lambda bi, i: (bi, 0, 0))]
                 + [_const_spec(a.shape) for a in consts],
        out_specs=[row(CONV_DIM), head, head, head, row(d), row(d), row(LANES)],
        out_shape=[jax.ShapeDtypeStruct((b, s, CONV_DIM), BF16),
                   jax.ShapeDtypeStruct((b, N_HEADS, s, HEAD_PAD), BF16),
                   jax.ShapeDtypeStruct((b, N_HEADS, s, HEAD_PAD), BF16),
                   jax.ShapeDtypeStruct((b, N_HEADS, s, HEAD_PAD), BF16),
                   jax.ShapeDtypeStruct((b, s, d), BF16),
                   jax.ShapeDtypeStruct((b, s, d), BF16),
                   jax.ShapeDtypeStruct((b, s, LANES), F32)],
        scratch_shapes=[pltpu.VMEM((1, LANES), F32)],
        compiler_params=_params("arbitrary", "arbitrary"),
        name="inproj",
    )(x, mod, *consts)


def _attn_kernel(cs_ref, ce_ref, prm_ref, q_ref, k_ref, v_ref, o_ref, acc_ref, m_ref):
    t = ATTN_TILE
    nb = q_ref.shape[2] // t
    base = (pl.program_id(0) * pl.num_programs(1) + pl.program_id(1)) * nb
    bound = prm_ref[0]
    thresh = -(EXP_UNDERFLOW + 2.0 * bound)
    causal = lax.broadcasted_iota(jnp.int32, (t, t), 0) >= lax.broadcasted_iota(jnp.int32, (t, t), 1)

    dp = DIAG_PARTS

    def first_block(i):
        c0 = cs_ref[dp * (base + i)]
        return lax.fori_loop(
            0, i, lambda j, n: n + (c0 - ce_ref[dp * (base + j) + dp - 1] < thresh).astype(jnp.int32), 0)

    def scores(q, j, masked):
        k0 = pl.multiple_of(j * t, t)
        s = _dot_nt(q, k_ref[0, 0, pl.ds(k0, t), :])
        if masked:
            s = jnp.where(causal, s, NEG_BIG)
        return s, v_ref[0, 0, pl.ds(k0, t), :]

    def finish(q0):
        acc = acc_ref[...]
        o_ref[0, pl.ds(q0, t), :] = (acc / acc[:, HEAD_DIM:HEAD_DIM + 1]).astype(BF16)

    def fixed_shift(i, carry):
        q0 = pl.multiple_of(i * t, t)
        q = q_ref[0, 0, pl.ds(q0, t), :]

        acc_ref[...] = jnp.zeros_like(acc_ref)

        def weighted(j, masked):
            s, vb = scores(q, j, masked)
            return _dot(jnp.exp(s).astype(BF16), vb)

        def kv(j, c):
            acc_ref[...] += weighted(j, False)
            return c

        lax.fori_loop(first_block(i), i - 1, kv, 0)
        rp = t // DIAG_PARTS

        @pl.when(i == 0)
        def _():
            for r in range(DIAG_PARTS):
                nk = (r + 1) * rp
                s = _dot_nt(q[r * rp:(r + 1) * rp, :], k_ref[0, 0, pl.ds(q0, nk), :])
                seen = (lax.broadcasted_iota(jnp.int32, (rp, nk), 0) + r * rp
                        >= lax.broadcasted_iota(jnp.int32, (rp, nk), 1))
                p = jnp.exp(jnp.where(seen, s, NEG_BIG)).astype(BF16)
                acc_ref[r * rp:(r + 1) * rp, :] += _dot(p, v_ref[0, 0, pl.ds(q0, nk), :])
            finish(q0)

        @pl.when(i > 0)
        def _():
            nk = t + rp
            for r in range(DIAG_PARTS):
                k0 = pl.multiple_of(q0 - t + r * rp, rp)
                s = _dot_nt(q[r * rp:(r + 1) * rp, :], k_ref[0, 0, pl.ds(k0, nk), :])
                seen = (lax.broadcasted_iota(jnp.int32, (rp, nk), 0) + t
                        >= lax.broadcasted_iota(jnp.int32, (rp, nk), 1))
                p = jnp.exp(jnp.where(seen, s, NEG_BIG)).astype(BF16)
                acc_ref[r * rp:(r + 1) * rp, :] += _dot(p, v_ref[0, 0, pl.ds(k0, nk), :])

            for r in range(1, DIAG_PARTS):
                need = cs_ref[dp * (base + i) + r] - ce_ref[dp * (base + i - 1) + r - 1] >= thresh

                @pl.when(need)
                def _():
                    kp = pl.multiple_of(q0 - t, t)
                    s = _dot_nt(q[r * rp:(r + 1) * rp, :], k_ref[0, 0, pl.ds(kp, r * rp), :])
                    acc_ref[r * rp:(r + 1) * rp, :] += _dot(jnp.exp(s).astype(BF16),
                                                            v_ref[0, 0, pl.ds(kp, r * rp), :])

            finish(q0)

        return carry

    def running_max(i, carry):
        q0 = pl.multiple_of(i * t, t)
        q = q_ref[0, 0, pl.ds(q0, t), :]
        m_ref[...] = jnp.full_like(m_ref, -jnp.inf)
        acc_ref[...] = jnp.zeros_like(acc_ref)

        def step(j, masked):
            s, vb = scores(q, j, masked)
            m_prev = m_ref[...]
            m_new = jnp.maximum(m_prev, jnp.max(s, axis=-1, keepdims=True))
            p = jnp.exp(s - m_new)
            acc_ref[...] = jnp.exp(m_prev - m_new) * acc_ref[...] + _dot(p.astype(BF16), vb)
            m_ref[...] = m_new

        def kv(j, c):
            step(j, False)
            return c

        lax.fori_loop(first_block(i), i, kv, 0)
        step(i, True)
        finish(q0)
        return carry

    @pl.when(bound <= FIXED_SHIFT_BOUND)
    def _():
        lax.fori_loop(0, nb, fixed_shift, 0)

    @pl.when(bound > FIXED_SHIFT_BOUND)
    def _():
        lax.fori_loop(0, nb, running_max, 0)


def _attention(cs, ce, prm, qa, ka, v):
    b, nh, s, hp = qa.shape
    t = ATTN_TILE
    seq = pl.BlockSpec((1, 1, s, hp), lambda bi, hi, *_: (bi, hi, 0, 0))
    return pl.pallas_call(
        _attn_kernel,
        grid_spec=pltpu.PrefetchScalarGridSpec(
            num_scalar_prefetch=3,
            grid=(b, nh),
            in_specs=[seq, seq, seq],
            out_specs=pl.BlockSpec((1, s, hp), lambda bi, hi, *_: (bi, 0, hi)),
            scratch_shapes=[pltpu.VMEM((t, hp), F32), pltpu.VMEM((t, 1), F32)]),
        out_shape=jax.ShapeDtypeStruct((b, s, nh * hp), BF16),
        compiler_params=_params("arbitrary", "arbitrary"),
        name="attn",
    )(cs, ce, prm, qa, ka, v)


def _merge_kernel(u_ref, halo_ref, o_ref, sgc_ref, sga_ref, x_ref, mod_ref, wdw_ref, bdw_ref,
                  gng_ref, gnb_ref, gg_ref, wco_ref, wao_ref, wout_ref, x1_ref, buf_ref):
    tm = ROW_TILE
    halo = halo_ref[0].astype(F32)
    halo = jnp.where(pl.program_id(1) == 0, jnp.zeros_like(halo), halo)
    ucur = u_ref[0].astype(F32)
    for cb in range(CONV_DIM // LANES):
        buf_ref[cb, 0:CONV_HALO, :] = halo[:, cb * LANES:(cb + 1) * LANES]
        buf_ref[cb, CONV_HALO:, :] = ucur[:, cb * LANES:(cb + 1) * LANES]

    base = CONV_HALO - (CONV_WIDTH - 1)
    ys = []
    for cb in range(CONV_DIM // LANES):
        acc = jnp.zeros((tm, LANES), F32)
        for j in range(CONV_WIDTH):
            acc = acc + wdw_ref[j:j + 1, cb * LANES:(cb + 1) * LANES] * buf_ref[cb, base + j:base + j + tm, :]
        ys.append(acc)
    y = jnp.concatenate(ys, axis=1) + bdw_ref[...]

    gg = gg_ref[...]
    y_hi, y_lo = _split2(y)
    dlt = y - (_dot(y_hi, gg) + _dot(y_lo, gg))
    s_hi, s_lo = _split2(dlt * dlt)
    var = _dot(s_hi, gg) + _dot(s_lo, gg)
    yn = dlt * lax.rsqrt(var + EPS) * gng_ref[...] + gnb_ref[...]
    y_conv = _dot(_silu(yn).astype(BF16), wco_ref[...])

    per = HEAD_PAD // HEAD_DIM
    lane = lax.broadcasted_iota(jnp.int32, (1, HEAD_PAD), 1)
    packed = []
    for g in range(N_HEADS // per):
        tile = o_ref[0, :, g * per * HEAD_PAD:(g * per + 1) * HEAD_PAD].astype(F32)
        for part in range(1, per):
            nxt = o_ref[0, :, (g * per + part) * HEAD_PAD:(g * per + part + 1) * HEAD_PAD].astype(F32)
            tile = jnp.where(lane < part * HEAD_DIM, tile, pltpu.roll(nxt, part * HEAD_DIM, 1))
        packed.append(tile)
    y_attn = _dot(jnp.concatenate(packed, axis=1).astype(BF16), wao_ref[...])
    merged = sgc_ref[0].astype(F32) * y_conv + sga_ref[0].astype(F32) * y_attn
    mix = _dot(merged.astype(BF16), wout_ref[...])
    x1_ref[0] = x_ref[0] + mod_ref[0, 2:3, :] * mix


def _merge(u, o, sgc, sga, x, mod, wdw, bdw, gng, gnb, gg, wco, wao, wout):
    b, s, d = x.shape
    tm = ROW_TILE
    per = tm // CONV_HALO
    row = lambda w: pl.BlockSpec((1, tm, w), lambda bi, i: (bi, i, 0))
    consts = [wdw, bdw, gng, gnb, gg, wco, wao, wout]
    return pl.pallas_call(
        _merge_kernel,
        grid=(b, s // tm),
        in_specs=[row(CONV_DIM),
                  pl.BlockSpec((1, CONV_HALO, CONV_DIM),
                               lambda bi, i: (bi, jnp.maximum(i * per - 1, 0), 0)),
                  row(o.shape[2]), row(d), row(d), row(d),
                  pl.BlockSpec((1, 6, d), lambda bi, i: (bi, 0, 0))]
                 + [_const_spec(a.shape) for a in consts],
        out_specs=row(d),
        out_shape=jax.ShapeDtypeStruct((b, s, d), F32),
        scratch_shapes=[pltpu.VMEM((CONV_DIM // LANES, CONV_HALO + tm, LANES), F32)],
        compiler_params=_params("arbitrary", "arbitrary"),
        name="merge",
    )(u, u, o, sgc, sga, x, mod, *consts)


def _router_kernel(x1_ref, mod_ref, g2_ref, wr_hi_ref, wr_lo_ref, rb_ref, h2_ref, h2w_ref, comb_ref, cnt_ref):
    x = x1_ref[0]
    ms = jnp.mean(x * x, axis=-1, keepdims=True)
    h = (x * lax.rsqrt(ms + EPS) * g2_ref[...]) * (1.0 + mod_ref[0, 4:5, :]) + mod_ref[0, 3:4, :]
    h2_ref[0] = h.astype(BF16)
    h2w_ref[0] = _pack_halves(h)

    h_hi, h_lo = _split2(h)
    logits = _dot_nt(wr_hi_ref[...], h_hi) + _dot_nt(wr_hi_ref[...], h_lo) + _dot_nt(wr_lo_ref[...], h_hi)
    scores = _sigmoid(logits)
    biased = scores + rb_ref[...]

    per = N_EXPERTS // N_GROUPS
    rows = lax.broadcasted_iota(jnp.int32, (per, biased.shape[1]), 0)
    gscore = []
    for g in range(N_GROUPS):
        blk = biased[g * per:(g + 1) * per, :]
        top1 = jnp.max(blk, axis=0, keepdims=True)
        first = jnp.min(jnp.where(blk == top1, rows, per), axis=0, keepdims=True)
        top2 = jnp.max(jnp.where(rows == first, -jnp.inf, blk), axis=0, keepdims=True)
        gscore.append(top1 + top2)

    cand = []
    for g in range(N_GROUPS):
        rank = jnp.zeros_like(gscore[g], dtype=jnp.int32)
        for g2 in range(N_GROUPS):
            if g2 == g:
                continue
            ahead = gscore[g2] > gscore[g]
            if g2 < g:
                ahead = ahead | (gscore[g2] == gscore[g])
            rank = rank + ahead.astype(jnp.int32)
        keep = rank < TOPK_GROUPS
        cand.append(jnp.where(keep, biased[g * per:(g + 1) * per, :], -jnp.inf))
    cand = jnp.concatenate(cand, axis=0)

    eidx = lax.broadcasted_iota(jnp.int32, cand.shape, 0)
    work = cand
    for _ in range(TOP_K):
        top = jnp.max(work, axis=0, keepdims=True)
        first = jnp.min(jnp.where(work == top, eidx, N_EXPERTS), axis=0, keepdims=True)
        work = jnp.where(eidx == first, -jnp.inf, work)
    sel = (work != cand) & (cand > -jnp.inf)
    w = jnp.where(sel, scores, 0.0)
    comb = w / jnp.sum(w, axis=0, keepdims=True) * ROUTED_SCALE
    comb_ref[...] = comb

    @pl.when((pl.program_id(0) == 0) & (pl.program_id(1) == 0))
    def _():
        cnt_ref[...] = jnp.zeros_like(cnt_ref)

    hit = jnp.where(comb != 0.0, 1.0, 0.0).astype(BF16)
    cnt_ref[...] += _dot(hit, jnp.ones((hit.shape[1], LANES), BF16))


def _router(x1, mod, g2, wr_hi, wr_lo, rb):
    b, s, d = x1.shape
    tm = ROW_TILE
    nt = s // tm
    return pl.pallas_call(
        _router_kernel,
        grid=(b, nt),
        in_specs=[pl.BlockSpec((1, tm, d), lambda bi, i: (bi, i, 0)),
                  pl.BlockSpec((1, 6, d), lambda bi, i: (bi, 0, 0)),
                  _const_spec(g2.shape), _const_spec(wr_hi.shape), _const_spec(wr_lo.shape),
                  _const_spec(rb.shape)],
        out_specs=[pl.BlockSpec((1, tm, d), lambda bi, i: (bi, i, 0)),
                   pl.BlockSpec((1, tm, d // 2), lambda bi, i: (bi, i, 0)),
                   pl.BlockSpec((N_EXPERTS, tm), lambda bi, i: (0, bi * nt + i)),
                   pl.BlockSpec((N_EXPERTS, LANES), lambda bi, i: (0, 0))],
        out_shape=[jax.ShapeDtypeStruct((b, s, d), BF16),
                   jax.ShapeDtypeStruct((b, s, d // 2), jnp.int32),
                   jax.ShapeDtypeStruct((N_EXPERTS, b * s), F32),
                   jax.ShapeDtypeStruct((N_EXPERTS, LANES), F32)],
        compiler_params=_params("arbitrary", "arbitrary"),
        name="router",
    )(x1, mod, g2, wr_hi, wr_lo, rb)


def _pos_kernel(comb_ref, cnt_ref, tri_ref, lstrict_ref, posk_ref, wk_ref, gend_ref, base_ref, *, spare_row):
    tp = comb_ref.shape[1]
    comb = comb_ref[...]
    sel = comb != 0.0
    selb = jnp.where(sel, 1.0, 0.0).astype(BF16)

    @pl.when(pl.program_id(0) == 0)
    def _():
        seg = jnp.floor((cnt_ref[...] + (EXPERT_TILE - 1.0)) * (1.0 / EXPERT_TILE)) * EXPERT_TILE
        s_hi, s_mid, s_lo = _split3(seg)
        ls = lstrict_ref[...]
        start = _dot(ls, s_hi) + _dot(ls, s_mid) + _dot(ls, s_lo)
        base_ref[...] = start
        gend_ref[...] = start + seg

    rank = _dot(selb, tri_ref[...])
    pos = base_ref[:, 0:1] + rank - 1.0
    base_ref[...] += _dot(selb, jnp.ones((tp, LANES), BF16))
    slot = _dot(lstrict_ref[...], selb)
    rows_p, rows_w = [], []
    for k in range(TOP_K):
        m = sel & (slot == k)
        rows_p.append(jnp.sum(jnp.where(m, pos - spare_row, 0.0), axis=0, keepdims=True) + spare_row)
        rows_w.append(jnp.sum(jnp.where(m, comb, 0.0), axis=0, keepdims=True))
    posk_ref[...] = jnp.concatenate(rows_p, axis=0).astype(jnp.int32)
    wk_ref[...] = jnp.concatenate(rows_w, axis=0)


def _positions(comb_t, cnt, tri, lstrict, n_rows):
    ne, t = comb_t.shape
    tp = POS_TILE
    tok = lambda rows: pl.BlockSpec((rows, tp), lambda i: (0, i))
    return pl.pallas_call(
        functools.partial(_pos_kernel, spare_row=float(n_rows - 1)),
        grid=(t // tp,),
        in_specs=[tok(ne), _const_spec(cnt.shape), _const_spec(tri.shape), _const_spec(lstrict.shape)],
        out_specs=[tok(TOP_K), tok(TOP_K), pl.BlockSpec((ne, LANES), lambda i: (0, 0))],
        out_shape=[jax.ShapeDtypeStruct((TOP_K, t), jnp.int32),
                   jax.ShapeDtypeStruct((TOP_K, t), F32),
                   jax.ShapeDtypeStruct((ne, LANES), F32)],
        scratch_shapes=[pltpu.VMEM((ne, LANES), F32)],
        compiler_params=_params("arbitrary"),
        name="positions",
    )(comb_t, cnt, tri, lstrict)


def _sc_workers():
    info = plsc.get_sparse_core_info()
    return info.num_cores, info.num_cores * info.num_subcores


def _sc_scatter_rows(rows, pos, n_out):
    nc, nw = _sc_workers()
    n, w = rows.shape
    nk = pos.shape[0]
    ch = SC_CHUNK
    per_w = n // nw
    assert per_w * nw == n and per_w % ch == 0

    @functools.partial(
        pl.kernel, mesh=plsc.VectorSubcoreMesh(core_axis_name="c", subcore_axis_name="s"),
        out_type=jax.ShapeDtypeStruct((n_out, w), rows.dtype),
        scratch_types=[pltpu.VMEM((nk, ch), jnp.int32), pltpu.VMEM((ch, w), rows.dtype),
                       pltpu.SemaphoreType.DMA])
    def scatter(rows_hbm, pos_hbm, out_hbm, idx_v, rows_v, sem):
        base = (lax.axis_index("s") * nc + lax.axis_index("c")) * per_w

        @pl.loop(0, per_w // ch)
        def _(ci):
            off = pl.multiple_of(base + ci * ch, ch)
            pltpu.sync_copy(pos_hbm.at[:, pl.ds(off, ch)], idx_v)
            pltpu.sync_copy(rows_hbm.at[pl.ds(off, ch)], rows_v)
            copies = [pltpu.make_async_copy(rows_v, out_hbm.at[idx_v.at[k]], sem) for k in range(nk)]
            for cp in copies:
                cp.start()
            for cp in copies:
                cp.wait()

    return scatter(rows, pos)


def _sc_gather_rows(table, idx):
    nc, nw = _sc_workers()
    n = idx.shape[0]
    w = table.shape[1]
    ch = SC_CHUNK
    per_w = n // nw
    assert per_w * nw == n and per_w % ch == 0
    nch = per_w // ch

    @functools.partial(
        pl.kernel, mesh=plsc.VectorSubcoreMesh(core_axis_name="c", subcore_axis_name="s"),
        out_type=jax.ShapeDtypeStruct((n, w), table.dtype),
        scratch_types=[pltpu.VMEM((nch, ch), jnp.int32), pltpu.VMEM((ch, w), table.dtype),
                       pltpu.SemaphoreType.DMA])
    def gather(table_hbm, idx_hbm, out_hbm, idx_v, rows_v, sem):
        wid = lax.axis_index("s") * nc + lax.axis_index("c")
        base = wid * per_w
        pltpu.sync_copy(idx_hbm.at[pl.ds(wid * nch, nch)], idx_v)

        @pl.loop(0, nch)
        def _(ci):
            off = pl.multiple_of(base + ci * ch, ch)
            cp = pltpu.make_async_copy(table_hbm.at[idx_v.at[ci]], rows_v, sem)
            cp.start()
            cp.wait()
            pltpu.sync_copy(rows_v, out_hbm.at[pl.ds(off, ch)])

    return gather(table, idx.reshape(n // ch, ch))


def _expert_kernel(te_ref, nu_ref, xs_ref, wgu_ref, wd_ref, y_ref, wgu_b, wd_b):
    i = pl.program_id(0)

    @pl.when(i < nu_ref[0])
    def _():
        @pl.when((i == 0) | (te_ref[i] != te_ref[jnp.maximum(i - 1, 0)]))
        def _():
            wgu_b[...] = wgu_ref[0].astype(BF16)
            wd_b[...] = wd_ref[0].astype(BF16)

        lo, hi = _unpack_halves(xs_ref[...])
        half = wgu_b.shape[0] // 2
        gu = _dot(lo.astype(BF16), wgu_b[:half, :]) + _dot(hi.astype(BF16), wgu_b[half:, :])
        act = _silu(gu[:, :EXPERT_DIM]) * gu[:, EXPERT_DIM:]
        y_ref[...] = _pack_halves(_dot(act.astype(BF16), wd_b[...]))


def _experts(tile_e, n_used, xs, wgu, wd):
    r, w = xs.shape
    tm = EXPERT_TILE
    d = wgu.shape[1]
    rows = pl.BlockSpec((tm, w), lambda i, te, nu: (jnp.minimum(i, nu[0] - 1), 0))
    return pl.pallas_call(
        _expert_kernel,
        grid_spec=pltpu.PrefetchScalarGridSpec(
            num_scalar_prefetch=2,
            grid=(r // tm,),
            in_specs=[rows,
                      pl.BlockSpec((1, d, 2 * EXPERT_DIM), lambda i, te, nu: (te[i], 0, 0)),
                      pl.BlockSpec((1, EXPERT_DIM, d), lambda i, te, nu: (te[i], 0, 0))],
            out_specs=rows,
            scratch_shapes=[pltpu.VMEM((d, 2 * EXPERT_DIM), BF16), pltpu.VMEM((EXPERT_DIM, d), BF16)]),
        out_shape=jax.ShapeDtypeStruct((r, w), jnp.int32),
        compiler_params=_params("arbitrary"),
        name="experts",
    )(tile_e, n_used, xs, wgu, wd)


def _final_kernel(yk_ref, wk_ref, h_ref, x1_ref, mod_ref, wsgu_ref, wsd_ref, out_ref):
    half = yk_ref.shape[2]
    acc_lo = jnp.zeros((yk_ref.shape[1], half), F32)
    acc_hi = jnp.zeros((yk_ref.shape[1], half), F32)
    for k in range(TOP_K):
        lo, hi = _unpack_halves(yk_ref[k])
        w = wk_ref[:, k:k + 1]
        acc_lo = acc_lo + jnp.where(w != 0.0, w * lo, 0.0)
        acc_hi = acc_hi + jnp.where(w != 0.0, w * hi, 0.0)
    routed = jnp.concatenate([acc_lo, acc_hi], axis=1)
    sgu = _dot(h_ref[...], wsgu_ref[...])
    act = _silu(sgu[:, :SHARED_DIM]) * sgu[:, SHARED_DIM:]
    shared = _dot(act.astype(BF16), wsd_ref[...])
    out_ref[...] = x1_ref[...] + mod_ref[0, 5:6, :] * (routed + shared)


def _final(yk, wk_t, h2, x1, mod, wsgu, wsd, tiles_per_batch):
    t, d = h2.shape
    tm = ROW_TILE
    row = lambda w: pl.BlockSpec((tm, w), lambda i: (i, 0))
    return pl.pallas_call(
        _final_kernel,
        grid=(t // tm,),
        in_specs=[pl.BlockSpec((TOP_K, tm, d // 2), lambda i: (0, i, 0)),
                  row(TOP_K), row(d), row(d),
                  pl.BlockSpec((1, 6, d), lambda i: (i // tiles_per_batch, 0, 0)),
                  _const_spec(wsgu.shape), _const_spec(wsd.shape)],
        out_specs=row(d),
        out_shape=jax.ShapeDtypeStruct((t, d), F32),
        compiler_params=_params("arbitrary"),
        name="final",
    )(yk, wk_t, h2, x1, mod, wsgu, wsd)


def _placement():
    pq = np.zeros((LANES, N_HEADS * HEAD_PAD), np.float32)
    pk = np.zeros((LANES, N_HEADS * HEAD_PAD), np.float32)
    cq = np.zeros((1, N_HEADS * HEAD_PAD), np.float32)
    ck = np.zeros((1, N_HEADS * HEAD_PAD), np.float32)
    for hd in range(N_HEADS):
        for k in range(3):
            pq[k * N_HEADS + hd, hd * HEAD_PAD + AUG0 + k] = 1.0
            ck[0, hd * HEAD_PAD + AUG0 + k] = 1.0
            pk[k * N_HEADS + hd, hd * HEAD_PAD + AUG0 + 3 + k] = -1.0
            cq[0, hd * HEAD_PAD + AUG0 + 3 + k] = 1.0
    return (jnp.asarray(pq, BF16), jnp.asarray(pk, BF16), jnp.asarray(cq), jnp.asarray(ck))


def kernel(x, c, w_ada, b_ada, norm1_g, w_in, w_dw, b_dw, conv_gn_g, conv_gn_b, w_conv_out,
           q_norm_g, k_norm_g, b_forget, w_attn_out, w_out, norm2_g, w_router, router_bias,
           w_experts_gate_up, w_experts_down, w_shared_gate_up, w_shared_down):
    depth = w_ada.shape[0]
    b, s, d = x.shape

    pq, pk, cq, ck = _placement()
    tri = jnp.asarray(np.tril(np.ones((ROW_TILE, ROW_TILE), np.float32)), BF16)
    grp = np.arange(CONV_DIM) // (CONV_DIM // CONV_GROUPS)
    gg = jnp.asarray((grp[:, None] == grp[None, :]).astype(np.float32) / (CONV_DIM // CONV_GROUPS), BF16)
    c_pad = jnp.pad(c, ((0, SUBLANES - b), (0, 0)))
    tri_pos = jnp.asarray(np.triu(np.ones((POS_TILE, POS_TILE), np.float32)), BF16)
    lstrict = jnp.asarray(np.tril(np.ones((N_EXPERTS, N_EXPERTS), np.float32), -1), BF16)

    for l in range(depth):
        mod = _ada(c_pad, w_ada[l], b_ada[l][None, :])[:b].reshape(b, 6, d)

        bf = jnp.pad(b_forget[l][None, :], ((0, 0), (0, LANES - N_HEADS)))
        gpad = lambda g, sc: jnp.tile(jnp.pad(g * sc, (0, HEAD_PAD - HEAD_DIM)), N_HEADS)[None, :]
        qscale = HEAD_DIM ** -0.5
        u, qa, ka, v, sgc, sga, cum = _inproj(
            x, mod, norm1_g[l][None, :], _inproj_weight(w_in[l]),
            bf, gpad(q_norm_g[l], qscale), gpad(k_norm_g[l], 1.0),
            pq, pk, cq, ck, tri)

        flat = lambda a: a[:, :, :N_HEADS].transpose(0, 2, 1).reshape(-1)
        part = ATTN_TILE // DIAG_PARTS
        cs = flat(cum[:, 0::part])
        ce = flat(cum[:, part - 1::part])
        bound = (1.02 * HEAD_DIM * qscale) * jnp.max(jnp.abs(q_norm_g[l])) * jnp.max(jnp.abs(k_norm_g[l]))
        o = _attention(cs, ce, bound.reshape(1), qa, ka, v)

        wdw = jnp.pad(w_dw[l], ((0, CONV_HALO - CONV_WIDTH), (0, 0)))
        x1 = _merge(u, o, sgc, sga, x, mod, wdw, b_dw[l][None, :], conv_gn_g[l][None, :],
                    conv_gn_b[l][None, :], gg, w_conv_out[l].astype(BF16), w_attn_out[l].astype(BF16),
                    w_out[l].astype(BF16))

        wr = w_router[l].T
        wr_hi = wr.astype(BF16)
        wr_lo = (wr - wr_hi.astype(F32)).astype(BF16)
        h2, h2w, comb_t, cnt = _router(x1, mod, norm2_g[l][None, :], wr_hi, wr_lo, router_bias[l][:, None])

        t = b * s
        n_tiles = (t * TOP_K) // EXPERT_TILE + N_EXPERTS
        posk, wk, gend = _positions(comb_t, cnt, tri_pos, lstrict, n_tiles * EXPERT_TILE)
        seg_end = gend[:, 0].astype(jnp.int32)
        n_used = seg_end[-1:] // EXPERT_TILE
        tile_start = jnp.arange(n_tiles, dtype=jnp.int32) * EXPERT_TILE
        tile_start = jnp.minimum(tile_start, seg_end[-1] - EXPERT_TILE)
        tile_e = jnp.sum((seg_end[None, :] <= tile_start[:, None]).astype(jnp.int32), axis=1)

        xs = _sc_scatter_rows(h2w.reshape(t, d // 2), posk, n_tiles * EXPERT_TILE)
        ys = _experts(tile_e, n_used, xs, w_experts_gate_up[l], w_experts_down[l])
        yk = _sc_gather_rows(ys, posk.reshape(-1)).reshape(TOP_K, t, d // 2)
        out = _final(yk, wk.T, h2.reshape(t, d), x1.reshape(t, d), mod,
                     w_shared_gate_up[l].astype(BF16), w_shared_down[l].astype(BF16), s // ROW_TILE)
        x = out.reshape(b, s, d)
    return x
```

```python
import functools

import numpy as np
import jax
import jax.numpy as jnp
from jax import lax
from jax.experimental import pallas as pl
from jax.experimental.pallas import tpu as pltpu
from jax.experimental.pallas import tpu_sc as plsc

F32 = jnp.float32
BF16 = jnp.bfloat16

CONV_DIM = 512
CONV_WIDTH = 31
CONV_GROUPS = 8
N_HEADS = 8
HEAD_DIM = 64
ATTN_DIM = N_HEADS * HEAD_DIM
N_EXPERTS = 64
TOP_K = 8
N_GROUPS = 8
TOPK_GROUPS = 4
EXPERT_DIM = 256
SHARED_DIM = 256
ROUTED_SCALE = 2.5
EPS = 1e-6

LANES = 128
SUBLANES = 8
HEAD_PAD = LANES
AUG0 = HEAD_DIM
VMEM_LIMIT = 56 * 1024 * 1024

ROW_TILE = 512
ATTN_TILE = 1024
DIAG_PARTS = 4
EXPERT_TILE = 512
POS_TILE = 1024
SC_CHUNK = 128
CONV_HALO = 32

NEG_BIG = -1e30
EXP_UNDERFLOW = 104.0
FIXED_SHIFT_BOUND = 40.0


def _dot(a, b):
    return jnp.dot(a, b, preferred_element_type=F32)


def _dot_nt(a, b):
    return lax.dot_general(a, b, (((1,), (1,)), ((), ())), preferred_element_type=F32)


def _split2(x):
    hi = x.astype(BF16)
    lo = (x - hi.astype(F32)).astype(BF16)
    return hi, lo


def _split3(x):
    hi = x.astype(BF16)
    r = x - hi.astype(F32)
    mid = r.astype(BF16)
    lo = (r - mid.astype(F32)).astype(BF16)
    return hi, mid, lo


def _pack_halves(v):
    n = v.shape[1] // 2
    lo = lax.bitcast_convert_type(v[:, :n].astype(BF16).astype(F32), jnp.uint32)
    hi = lax.bitcast_convert_type(v[:, n:].astype(BF16).astype(F32), jnp.uint32)
    return lax.bitcast_convert_type(hi | lax.shift_right_logical(lo, jnp.uint32(16)), jnp.int32)


def _unpack_halves(w):
    u = lax.bitcast_convert_type(w, jnp.uint32)
    lo = lax.bitcast_convert_type(lax.shift_left(u, jnp.uint32(16)), F32)
    hi = lax.bitcast_convert_type(u & jnp.uint32(0xFFFF0000), F32)
    return lo, hi


def _sigmoid(x):
    return 1.0 / (1.0 + jnp.exp(-x))


def _silu(x):
    return x * _sigmoid(x)


def _params(*sem):
    return pltpu.CompilerParams(dimension_semantics=sem, vmem_limit_bytes=VMEM_LIMIT)


def _const_spec(shape):
    n = len(shape)
    return pl.BlockSpec(shape, lambda *_: (0,) * n, pipeline_mode=pl.Buffered(1))


def _ada_kernel(c_ref, w_ref, b_ref, o_ref):
    c = c_ref[...]
    a_hi, a_lo = _split2(_silu(c))
    w_hi, w_lo = _split2(w_ref[...])
    o_ref[...] = _dot(a_hi, w_hi) + _dot(a_hi, w_lo) + _dot(a_lo, w_hi) + b_ref[...]


def _ada(c_pad, w_ada, b_ada):
    d = c_pad.shape[1]
    n = w_ada.shape[1]
    return pl.pallas_call(
        _ada_kernel,
        grid=(n // d,),
        in_specs=[_const_spec(c_pad.shape),
                  pl.BlockSpec((d, d), lambda j: (0, j)),
                  pl.BlockSpec((1, d), lambda j: (0, j))],
        out_specs=pl.BlockSpec((c_pad.shape[0], d), lambda j: (0, j)),
        out_shape=jax.ShapeDtypeStruct((c_pad.shape[0], n), F32),
        compiler_params=_params("arbitrary"),
        name="ada",
    )(c_pad, w_ada, b_ada)


def _lane_pieces(x):
    hi, mid, lo = _split3(x)
    return (hi.astype(F32) + pltpu.roll(mid.astype(F32), N_HEADS, 1)
            + pltpu.roll(lo.astype(F32), 2 * N_HEADS, 1)).astype(BF16)


def _head_tile(p, col0, hd):
    per = HEAD_PAD // HEAD_DIM
    g, part = divmod(hd, per)
    blk = p[:, col0 + g * HEAD_PAD:col0 + (g + 1) * HEAD_PAD]
    if part:
        blk = pltpu.roll(blk, HEAD_PAD - part * HEAD_DIM, 1)
    lane = lax.broadcasted_iota(jnp.int32, (1, HEAD_PAD), 1)
    return jnp.where(lane < HEAD_DIM, blk, 0.0)


def _inproj_kernel(x_ref, mod_ref, g1_ref, w_ref, bf_ref, qg_ref, kg_ref, pq_ref, cq_ref, ck_ref, tri_ref,
                   u_ref, qa_ref, ka_ref, v_ref, sgc_ref, sga_ref, cum_ref, carry_ref):
    @pl.when(pl.program_id(1) == 0)
    def _():
        carry_ref[...] = jnp.zeros_like(carry_ref)

    x = x_ref[0]
    ms = jnp.mean(x * x, axis=-1, keepdims=True)
    h = (x * lax.rsqrt(ms + EPS) * g1_ref[...]) * (1.0 + mod_ref[0, 1:2, :]) + mod_ref[0, 0:1, :]
    hb = h.astype(BF16)

    hp = N_HEADS * HEAD_PAD
    o_q = 2 * CONV_DIM
    o_k, o_v = o_q + hp, o_q + 2 * hp
    o_f = o_v + ATTN_DIM
    o_gc = o_f + LANES
    o_ga = o_gc + x.shape[1]
    pc = _dot(hb, w_ref[:, :o_q])
    u_ref[0] = (pc[:, :CONV_DIM] * _sigmoid(pc[:, CONV_DIM:])).astype(BF16)

    z = _dot(hb, w_ref[:, o_f:o_gc]) + bf_ref[...]
    lf = jnp.minimum(z, 0.0) - jnp.log1p(jnp.exp(-jnp.abs(z)))
    lane = lax.broadcasted_iota(jnp.int32, (1, LANES), 1)
    psum = _dot(tri_ref[...], _lane_pieces(jnp.where(lane < N_HEADS, lf, 0.0)))
    cum = (psum + pltpu.roll(psum, LANES - N_HEADS, 1) + pltpu.roll(psum, LANES - 2 * N_HEADS, 1)
           + carry_ref[...])
    cum = jnp.where(lane < N_HEADS, cum, 0.0)
    carry_ref[...] = cum[ROW_TILE - 1:ROW_TILE, :]
    cum_ref[0] = cum

    pieces = _lane_pieces(cum)
    placed = _dot(pieces, pq_ref[...])
    addq = placed + cq_ref[...]
    addk = ck_ref[...] - pltpu.roll(placed, 3, 1)

    pq = _dot(hb, w_ref[:, o_q:o_k])
    pk = _dot(hb, w_ref[:, o_k:o_v])
    pv = _dot(hb, w_ref[:, o_v:o_f])
    inv_hd = 1.0 / HEAD_DIM
    vone = (lax.broadcasted_iota(jnp.int32, (1, HEAD_PAD), 1) == HEAD_DIM).astype(F32)
    for hd in range(N_HEADS):
        sl = slice(hd * HEAD_PAD, (hd + 1) * HEAD_PAD)
        qb = pq[:, sl]
        qn = qb * lax.rsqrt(jnp.sum(qb * qb, axis=-1, keepdims=True) * inv_hd + EPS) * qg_ref[:, sl]
        qa_ref[0, hd] = (qn + addq[:, sl]).astype(BF16)
        kb = pk[:, sl]
        kn = kb * lax.rsqrt(jnp.sum(kb * kb, axis=-1, keepdims=True) * inv_hd + EPS) * kg_ref[:, sl]
        ka_ref[0, hd] = (kn + addk[:, sl]).astype(BF16)
        v_ref[0, hd] = (_head_tile(pv, 0, hd) + vone).astype(BF16)

    sgc_ref[0] = _sigmoid(_dot(hb, w_ref[:, o_gc:o_ga])).astype(BF16)
    sga_ref[0] = _sigmoid(_dot(hb, w_ref[:, o_ga:])).astype(BF16)


def _inproj_weight(wi):
    d = wi.shape[0]
    o_q = 2 * CONV_DIM
    o_f = o_q + 3 * ATTN_DIM
    o_gc = o_f + N_HEADS
    heads = lambda w: jnp.pad(w.reshape(d, N_HEADS, HEAD_DIM),
                              ((0, 0), (0, 0), (0, HEAD_PAD - HEAD_DIM))).reshape(d, N_HEADS * HEAD_PAD)
    return jnp.concatenate(
        [wi[:, :o_q], heads(wi[:, o_q:o_q + ATTN_DIM]), heads(wi[:, o_q + ATTN_DIM:o_q + 2 * ATTN_DIM]),
         wi[:, o_q + 2 * ATTN_DIM:o_f], jnp.pad(wi[:, o_f:o_gc], ((0, 0), (0, LANES - N_HEADS))),
         wi[:, o_gc:]], axis=1).astype(BF16)


def _inproj(x, mod, g1, w, bf, qg, kg, pq, cq, ck, tri):
    b, s, d = x.shape
    tm = ROW_TILE
    row = lambda w: pl.BlockSpec((1, tm, w), lambda bi, i: (bi, i, 0))
    head = pl.BlockSpec((1, N_HEADS, tm, HEAD_PAD), lambda bi, i: (bi, 0, i, 0))
    consts = [g1, w, bf, qg, kg, pq, cq, ck, tri]
    return pl.pallas_call(
        _inproj_kernel,
        grid=(b, s // tm),
        in_specs=[row(d), pl.BlockSpec((1, 6, d), lambda bi, i: (bi, 0, 0))]
                 + [_const_spec(a.shape) for a in consts],
        out_specs=[row(CONV_DIM), head, head, head, row(d), row(d), row(LANES)],
        out_shape=[jax.ShapeDtypeStruct((b, s, CONV_DIM), BF16),
                   jax.ShapeDtypeStruct((b, N_HEADS, s, HEAD_PAD), BF16),
                   jax.ShapeDtypeStruct((b, N_HEADS, s, HEAD_PAD), BF16),
                   jax.ShapeDtypeStruct((b, N_HEADS, s, HEAD_PAD), BF16),
                   jax.ShapeDtypeStruct((b, s, d), BF16),
                   jax.ShapeDtypeStruct((b, s, d), BF16),
                   jax.ShapeDtypeStruct((b, s, LANES), F32)],
        scratch_shapes=[pltpu.VMEM((1, LANES), F32)],
        compiler_params=_params("arbitrary", "arbitrary"),
        name="inproj",
    )(x, mod, *consts)


def _attn_kernel(cs_ref, ce_ref, prm_ref, q_ref, k_ref, v_ref, o_ref, acc_ref, m_ref):
    t = ATTN_TILE
    nb = q_ref.shape[2] // t
    base = (pl.program_id(0) * pl.num_programs(1) + pl.program_id(1)) * nb
    bound = prm_ref[0]
    thresh = -(EXP_UNDERFLOW + 2.0 * bound)
    causal = lax.broadcasted_iota(jnp.int32, (t, t), 0) >= lax.broadcasted_iota(jnp.int32, (t, t), 1)

    dp = DIAG_PARTS

    def first_block(i):
        c0 = cs_ref[dp * (base + i)]
        return lax.fori_loop(
            0, i, lambda j, n: n + (c0 - ce_ref[dp * (base + j) + dp - 1] < thresh).astype(jnp.int32), 0)

    def scores(q, j, masked):
        k0 = pl.multiple_of(j * t, t)
        s = _dot_nt(q, k_ref[0, 0, pl.ds(k0, t), :])
        if masked:
            s = jnp.where(causal, s, NEG_BIG)
        return s, v_ref[0, 0, pl.ds(k0, t), :]

    def finish(q0):
        acc = acc_ref[...]
        o_ref[0, pl.ds(q0, t), :] = (acc / acc[:, HEAD_DIM:HEAD_DIM + 1]).astype(BF16)

    def fixed_shift(i, carry):
        q0 = pl.multiple_of(i * t, t)
        q = q_ref[0, 0, pl.ds(q0, t), :]

        acc_ref[...] = jnp.zeros_like(acc_ref)

        def weighted(j, masked):
            s, vb = scores(q, j, masked)
            return _dot(jnp.exp(s).astype(BF16), vb)

        def kv(j, c):
            acc_ref[...] += weighted(j, False)
            return c

        lax.fori_loop(first_block(i), i - 1, kv, 0)
        rp = t // DIAG_PARTS

        @pl.when(i == 0)
        def _():
            for r in range(DIAG_PARTS):
                nk = (r + 1) * rp
                s = _dot_nt(q[r * rp:(r + 1) * rp, :], k_ref[0, 0, pl.ds(q0, nk), :])
                seen = (lax.broadcasted_iota(jnp.int32, (rp, nk), 0) + r * rp
                        >= lax.broadcasted_iota(jnp.int32, (rp, nk), 1))
                p = jnp.exp(jnp.where(seen, s, NEG_BIG)).astype(BF16)
                acc_ref[r * rp:(r + 1) * rp, :] += _dot(p, v_ref[0, 0, pl.ds(q0, nk), :])
            finish(q0)

        @pl.when(i > 0)
        def _():
            nk = t + rp
            for r in range(DIAG_PARTS):
                k0 = pl.multiple_of(q0 - t + r * rp, rp)
                s = _dot_nt(q[r * rp:(r + 1) * rp, :], k_ref[0, 0, pl.ds(k0, nk), :])
                seen = (lax.broadcasted_iota(jnp.int32, (rp, nk), 0) + t
                        >= lax.broadcasted_iota(jnp.int32, (rp, nk), 1))
                p = jnp.exp(jnp.where(seen, s, NEG_BIG)).astype(BF16)
                acc_ref[r * rp:(r + 1) * rp, :] += _dot(p, v_ref[0, 0, pl.ds(k0, nk), :])

            for r in range(1, DIAG_PARTS):
                need = cs_ref[dp * (base + i) + r] - ce_ref[dp * (base + i - 1) + r - 1] >= thresh

                @pl.when(need)
                def _():
                    kp = pl.multiple_of(q0 - t, t)
                    s = _dot_nt(q[r * rp:(r + 1) * rp, :], k_ref[0, 0, pl.ds(kp, r * rp), :])
                    acc_ref[r * rp:(r + 1) * rp, :] += _dot(jnp.exp(s).astype(BF16),
                                                            v_ref[0, 0, pl.ds(kp, r * rp), :])

            finish(q0)

        return carry

    def running_max(i, carry):
        q0 = pl.multiple_of(i * t, t)
        q = q_ref[0, 0, pl.ds(q0, t), :]
        m_ref[...] = jnp.full_like(m_ref, -jnp.inf)
        acc_ref[...] = jnp.zeros_like(acc_ref)

        def step(j, masked):
            s, vb = scores(q, j, masked)
            m_prev = m_ref[...]
            m_new = jnp.maximum(m_prev, jnp.max(s, axis=-1, keepdims=True))
            p = jnp.exp(s - m_new)
            acc_ref[...] = jnp.exp(m_prev - m_new) * acc_ref[...] + _dot(p.astype(BF16), vb)
            m_ref[...] = m_new

        def kv(j, c):
            step(j, False)
            return c

        lax.fori_loop(first_block(i), i, kv, 0)
        step(i, True)
        finish(q0)
        return carry

    @pl.when(bound <= FIXED_SHIFT_BOUND)
    def _():
        lax.fori_loop(0, nb, fixed_shift, 0)

    @pl.when(bound > FIXED_SHIFT_BOUND)
    def _():
        lax.fori_loop(0, nb, running_max, 0)


def _attention(cs, ce, prm, qa, ka, v):
    b, nh, s, hp = qa.shape
    t = ATTN_TILE
    seq = pl.BlockSpec((1, 1, s, hp), lambda bi, hi, *_: (bi, hi, 0, 0))
    return pl.pallas_call(
        _attn_kernel,
        grid_spec=pltpu.PrefetchScalarGridSpec(
            num_scalar_prefetch=3,
            grid=(b, nh),
            in_specs=[seq, seq, seq],
            out_specs=pl.BlockSpec((1, s, hp), lambda bi, hi, *_: (bi, 0, hi)),
            scratch_shapes=[pltpu.VMEM((t, hp), F32), pltpu.VMEM((t, 1), F32)]),
        out_shape=jax.ShapeDtypeStruct((b, s, nh * hp), BF16),
        compiler_params=_params("arbitrary", "arbitrary"),
        name="attn",
    )(cs, ce, prm, qa, ka, v)


def _merge_kernel(u_ref, halo_ref, o_ref, sgc_ref, sga_ref, x_ref, mod_ref, wdw_ref, bdw_ref,
                  gng_ref, gnb_ref, gg_ref, wco_ref, wao_ref, wout_ref, x1_ref, buf_ref):
    tm = ROW_TILE
    halo = halo_ref[0].astype(F32)
    halo = jnp.where(pl.program_id(1) == 0, jnp.zeros_like(halo), halo)
    ucur = u_ref[0].astype(F32)
    for cb in range(CONV_DIM // LANES):
        buf_ref[cb, 0:CONV_HALO, :] = halo[:, cb * LANES:(cb + 1) * LANES]
        buf_ref[cb, CONV_HALO:, :] = ucur[:, cb * LANES:(cb + 1) * LANES]

    base = CONV_HALO - (CONV_WIDTH - 1)
    ys = []
    for cb in range(CONV_DIM // LANES):
        acc = jnp.zeros((tm, LANES), F32)
        for j in range(CONV_WIDTH):
            acc = acc + wdw_ref[j:j + 1, cb * LANES:(cb + 1) * LANES] * buf_ref[cb, base + j:base + j + tm, :]
        ys.append(acc)
    y = jnp.concatenate(ys, axis=1) + bdw_ref[...]

    gg = gg_ref[...]
    y_hi, y_lo = _split2(y)
    dlt = y - (_dot(y_hi, gg) + _dot(y_lo, gg))
    s_hi, s_lo = _split2(dlt * dlt)
    var = _dot(s_hi, gg) + _dot(s_lo, gg)
    yn = dlt * lax.rsqrt(var + EPS) * gng_ref[...] + gnb_ref[...]
    y_conv = _dot(_silu(yn).astype(BF16), wco_ref[...])

    per = HEAD_PAD // HEAD_DIM
    lane = lax.broadcasted_iota(jnp.int32, (1, HEAD_PAD), 1)
    packed = []
    for g in range(N_HEADS // per):
        tile = o_ref[0, :, g * per * HEAD_PAD:(g * per + 1) * HEAD_PAD].astype(F32)
        for part in range(1, per):
            nxt = o_ref[0, :, (g * per + part) * HEAD_PAD:(g * per + part + 1) * HEAD_PAD].astype(F32)
            tile = jnp.where(lane < part * HEAD_DIM, tile, pltpu.roll(nxt, part * HEAD_DIM, 1))
        packed.append(tile)
    y_attn = _dot(jnp.concatenate(packed, axis=1).astype(BF16), wao_ref[...])
    merged = sgc_ref[0].astype(F32) * y_conv + sga_ref[0].astype(F32) * y_attn
    mix = _dot(merged.astype(BF16), wout_ref[...])
    x1_ref[0] = x_ref[0] + mod_ref[0, 2:3, :] * mix


def _merge(u, o, sgc, sga, x, mod, wdw, bdw, gng, gnb, gg, wco, wao, wout):
    b, s, d = x.shape
    tm = ROW_TILE
    per = tm // CONV_HALO
    row = lambda w: pl.BlockSpec((1, tm, w), lambda bi, i: (bi, i, 0))
    consts = [wdw, bdw, gng, gnb, gg, wco, wao, wout]
    return pl.pallas_call(
        _merge_kernel,
        grid=(b, s // tm),
        in_specs=[row(CONV_DIM),
                  pl.BlockSpec((1, CONV_HALO, CONV_DIM),
                               lambda bi, i: (bi, jnp.maximum(i * per - 1, 0), 0)),
                  row(o.shape[2]), row(d), row(d), row(d),
                  pl.BlockSpec((1, 6, d), lambda bi, i: (bi, 0, 0))]
                 + [_const_spec(a.shape) for a in consts],
        out_specs=row(d),
        out_shape=jax.ShapeDtypeStruct((b, s, d), F32),
        scratch_shapes=[pltpu.VMEM((CONV_DIM // LANES, CONV_HALO + tm, LANES), F32)],
        compiler_params=_params("arbitrary", "arbitrary"),
        name="merge",
    )(u, u, o, sgc, sga, x, mod, *consts)


def _router_kernel(x1_ref, mod_ref, g2_ref, wr_hi_ref, wr_lo_ref, rb_ref, h2_ref, h2w_ref, comb_ref, cnt_ref):
    x = x1_ref[0]
    ms = jnp.mean(x * x, axis=-1, keepdims=True)
    h = (x * lax.rsqrt(ms + EPS) * g2_ref[...]) * (1.0 + mod_ref[0, 4:5, :]) + mod_ref[0, 3:4, :]
    h2_ref[0] = h.astype(BF16)
    h2w_ref[0] = _pack_halves(h)

    h_hi, h_lo = _split2(h)
    logits = _dot_nt(wr_hi_ref[...], h_hi) + _dot_nt(wr_hi_ref[...], h_lo) + _dot_nt(wr_lo_ref[...], h_hi)
    scores = _sigmoid(logits)
    biased = scores + rb_ref[...]

    per = N_EXPERTS // N_GROUPS
    rows = lax.broadcasted_iota(jnp.int32, (per, biased.shape[1]), 0)
    gscore = []
    for g in range(N_GROUPS):
        blk = biased[g * per:(g + 1) * per, :]
        top1 = jnp.max(blk, axis=0, keepdims=True)
        first = jnp.min(jnp.where(blk == top1, rows, per), axis=0, keepdims=True)
        top2 = jnp.max(jnp.where(rows == first, -jnp.inf, blk), axis=0, keepdims=True)
        gscore.append(top1 + top2)

    cand = []
    for g in range(N_GROUPS):
        rank = jnp.zeros_like(gscore[g], dtype=jnp.int32)
        for g2 in range(N_GROUPS):
            if g2 == g:
                continue
            ahead = gscore[g2] > gscore[g]
            if g2 < g:
                ahead = ahead | (gscore[g2] == gscore[g])
            rank = rank + ahead.astype(jnp.int32)
        keep = rank < TOPK_GROUPS
        cand.append(jnp.where(keep, biased[g * per:(g + 1) * per, :], -jnp.inf))
    cand = jnp.concatenate(cand, axis=0)

    eidx = lax.broadcasted_iota(jnp.int32, cand.shape, 0)
    work = cand
    for _ in range(TOP_K):
        top = jnp.max(work, axis=0, keepdims=True)
        first = jnp.min(jnp.where(work == top, eidx, N_EXPERTS), axis=0, keepdims=True)
        work = jnp.where(eidx == first, -jnp.inf, work)
    sel = (work != cand) & (cand > -jnp.inf)
    w = jnp.where(sel, scores, 0.0)
    comb = w / jnp.sum(w, axis=0, keepdims=True) * ROUTED_SCALE
    comb_ref[...] = comb

    @pl.when((pl.program_id(0) == 0) & (pl.program_id(1) == 0))
    def _():
        cnt_ref[...] = jnp.zeros_like(cnt_ref)

    hit = jnp.where(comb != 0.0, 1.0, 0.0).astype(BF16)
    cnt_ref[...] += _dot(hit, jnp.ones((hit.shape[1], LANES), BF16))


def _router(x1, mod, g2, wr_hi, wr_lo, rb):
    b, s, d = x1.shape
    tm = ROW_TILE
    nt = s // tm
    return pl.pallas_call(
        _router_kernel,
        grid=(b, nt),
        in_specs=[pl.BlockSpec((1, tm, d), lambda bi, i: (bi, i, 0)),
                  pl.BlockSpec((1, 6, d), lambda bi, i: (bi, 0, 0)),
                  _const_spec(g2.shape), _const_spec(wr_hi.shape), _const_spec(wr_lo.shape),
                  _const_spec(rb.shape)],
        out_specs=[pl.BlockSpec((1, tm, d), lambda bi, i: (bi, i, 0)),
                   pl.BlockSpec((1, tm, d // 2), lambda bi, i: (bi, i, 0)),
                   pl.BlockSpec((N_EXPERTS, tm), lambda bi, i: (0, bi * nt + i)),
                   pl.BlockSpec((N_EXPERTS, LANES), lambda bi, i: (0, 0))],
        out_shape=[jax.ShapeDtypeStruct((b, s, d), BF16),
                   jax.ShapeDtypeStruct((b, s, d // 2), jnp.int32),
                   jax.ShapeDtypeStruct((N_EXPERTS, b * s), F32),
                   jax.ShapeDtypeStruct((N_EXPERTS, LANES), F32)],
        compiler_params=_params("arbitrary", "arbitrary"),
        name="router",
    )(x1, mod, g2, wr_hi, wr_lo, rb)


def _pos_kernel(comb_ref, cnt_ref, tri_ref, lstrict_ref, posk_ref, wk_ref, gend_ref, base_ref, *, spare_row):
    tp = comb_ref.shape[1]
    comb = comb_ref[...]
    sel = comb != 0.0
    selb = jnp.where(sel, 1.0, 0.0).astype(BF16)

    @pl.when(pl.program_id(0) == 0)
    def _():
        seg = jnp.floor((cnt_ref[...] + (EXPERT_TILE - 1.0)) * (1.0 / EXPERT_TILE)) * EXPERT_TILE
        s_hi, s_mid, s_lo = _split3(seg)
        ls = lstrict_ref[...]
        start = _dot(ls, s_hi) + _dot(ls, s_mid) + _dot(ls, s_lo)
        base_ref[...] = start
        gend_ref[...] = start + seg

    rank = _dot(selb, tri_ref[...])
    pos = base_ref[:, 0:1] + rank - 1.0
    base_ref[...] += _dot(selb, jnp.ones((tp, LANES), BF16))
    slot = _dot(lstrict_ref[...], selb)
    rows_p, rows_w = [], []
    for k in range(TOP_K):
        m = sel & (slot == k)
        rows_p.append(jnp.sum(jnp.where(m, pos - spare_row, 0.0), axis=0, keepdims=True) + spare_row)
        rows_w.append(jnp.sum(jnp.where(m, comb, 0.0), axis=0, keepdims=True))
    posk_ref[...] = jnp.concatenate(rows_p, axis=0).astype(jnp.int32)
    wk_ref[...] = jnp.concatenate(rows_w, axis=0)


def _positions(comb_t, cnt, tri, lstrict, n_rows):
    ne, t = comb_t.shape
    tp = POS_TILE
    tok = lambda rows: pl.BlockSpec((rows, tp), lambda i: (0, i))
    return pl.pallas_call(
        functools.partial(_pos_kernel, spare_row=float(n_rows - 1)),
        grid=(t // tp,),
        in_specs=[tok(ne), _const_spec(cnt.shape), _const_spec(tri.shape), _const_spec(lstrict.shape)],
        out_specs=[tok(TOP_K), tok(TOP_K), pl.BlockSpec((ne, LANES), lambda i: (0, 0))],
        out_shape=[jax.ShapeDtypeStruct((TOP_K, t), jnp.int32),
                   jax.ShapeDtypeStruct((TOP_K, t), F32),
                   jax.ShapeDtypeStruct((ne, LANES), F32)],
        scratch_shapes=[pltpu.VMEM((ne, LANES), F32)],
        compiler_params=_params("arbitrary"),
        name="positions",
    )(comb_t, cnt, tri, lstrict)


def _sc_workers():
    info = plsc.get_sparse_core_info()
    return info.num_cores, info.num_cores * info.num_subcores


def _sc_scatter_rows(rows, pos, n_out):
    nc, nw = _sc_workers()
    n, w = rows.shape
    nk = pos.shape[0]
    ch = SC_CHUNK
    per_w = n // nw
    assert per_w * nw == n and per_w % ch == 0

    @functools.partial(
        pl.kernel, mesh=plsc.VectorSubcoreMesh(core_axis_name="c", subcore_axis_name="s"),
        out_type=jax.ShapeDtypeStruct((n_out, w), rows.dtype),
        scratch_types=[pltpu.VMEM((nk, ch), jnp.int32), pltpu.VMEM((ch, w), rows.dtype),
                       pltpu.SemaphoreType.DMA])
    def scatter(rows_hbm, pos_hbm, out_hbm, idx_v, rows_v, sem):
        base = (lax.axis_index("s") * nc + lax.axis_index("c")) * per_w

        @pl.loop(0, per_w // ch)
        def _(ci):
            off = pl.multiple_of(base + ci * ch, ch)
            pltpu.sync_copy(pos_hbm.at[:, pl.ds(off, ch)], idx_v)
            pltpu.sync_copy(rows_hbm.at[pl.ds(off, ch)], rows_v)
            copies = [pltpu.make_async_copy(rows_v, out_hbm.at[idx_v.at[k]], sem) for k in range(nk)]
            for cp in copies:
                cp.start()
            for cp in copies:
                cp.wait()

    return scatter(rows, pos)


def _sc_gather_rows(table, idx):
    nc, nw = _sc_workers()
    n = idx.shape[0]
    w = table.shape[1]
    ch = SC_CHUNK
    per_w = n // nw
    assert per_w * nw == n and per_w % ch == 0
    nch = per_w // ch

    @functools.partial(
        pl.kernel, mesh=plsc.VectorSubcoreMesh(core_axis_name="c", subcore_axis_name="s"),
        out_type=jax.ShapeDtypeStruct((n, w), table.dtype),
        scratch_types=[pltpu.VMEM((nch, ch), jnp.int32), pltpu.VMEM((ch, w), table.dtype),
                       pltpu.SemaphoreType.DMA])
    def gather(table_hbm, idx_hbm, out_hbm, idx_v, rows_v, sem):
        wid = lax.axis_index("s") * nc + lax.axis_index("c")
        base = wid * per_w
        pltpu.sync_copy(idx_hbm.at[pl.ds(wid * nch, nch)], idx_v)

        @pl.loop(0, nch)
        def _(ci):
            off = pl.multiple_of(base + ci * ch, ch)
            cp = pltpu.make_async_copy(table_hbm.at[idx_v.at[ci]], rows_v, sem)
            cp.start()
            cp.wait()
            pltpu.sync_copy(rows_v, out_hbm.at[pl.ds(off, ch)])

    return gather(table, idx.reshape(n // ch, ch))


def _expert_kernel(te_ref, nu_ref, xs_ref, wgu_ref, wd_ref, y_ref, wgu_b, wd_b):
    i = pl.program_id(0)

    @pl.when(i < nu_ref[0])
    def _():
        @pl.when((i == 0) | (te_ref[i] != te_ref[jnp.maximum(i - 1, 0)]))
        def _():
            wgu_b[...] = wgu_ref[0].astype(BF16)
            wd_b[...] = wd_ref[0].astype(BF16)

        lo, hi = _unpack_halves(xs_ref[...])
        half = wgu_b.shape[0] // 2
        gu = _dot(lo.astype(BF16), wgu_b[:half, :]) + _dot(hi.astype(BF16), wgu_b[half:, :])
        act = _silu(gu[:, :EXPERT_DIM]) * gu[:, EXPERT_DIM:]
        y_ref[...] = _pack_halves(_dot(act.astype(BF16), wd_b[...]))


def _experts(tile_e, n_used, xs, wgu, wd):
    r, w = xs.shape
    tm = EXPERT_TILE
    d = wgu.shape[1]
    rows = pl.BlockSpec((tm, w), lambda i, te, nu: (jnp.minimum(i, nu[0] - 1), 0))
    return pl.pallas_call(
        _expert_kernel,
        grid_spec=pltpu.PrefetchScalarGridSpec(
            num_scalar_prefetch=2,
            grid=(r // tm,),
            in_specs=[rows,
                      pl.BlockSpec((1, d, 2 * EXPERT_DIM), lambda i, te, nu: (te[i], 0, 0)),
                      pl.BlockSpec((1, EXPERT_DIM, d), lambda i, te, nu: (te[i], 0, 0))],
            out_specs=rows,
            scratch_shapes=[pltpu.VMEM((d, 2 * EXPERT_DIM), BF16), pltpu.VMEM((EXPERT_DIM, d), BF16)]),
        out_shape=jax.ShapeDtypeStruct((r, w), jnp.int32),
        compiler_params=_params("arbitrary"),
        name="experts",
    )(tile_e, n_used, xs, wgu, wd)


def _final_kernel(yk_ref, wk_ref, h_ref, x1_ref, mod_ref, wsgu_ref, wsd_ref, out_ref):
    half = yk_ref.shape[2]
    acc_lo = jnp.zeros((yk_ref.shape[1], half), F32)
    acc_hi = jnp.zeros((yk_ref.shape[1], half), F32)
    for k in range(TOP_K):
        lo, hi = _unpack_halves(yk_ref[k])
        w = wk_ref[:, k:k + 1]
        acc_lo = acc_lo + jnp.where(w != 0.0, w * lo, 0.0)
        acc_hi = acc_hi + jnp.where(w != 0.0, w * hi, 0.0)
    routed = jnp.concatenate([acc_lo, acc_hi], axis=1)
    sgu = _dot(h_ref[...], wsgu_ref[...])
    act = _silu(sgu[:, :SHARED_DIM]) * sgu[:, SHARED_DIM:]
    shared = _dot(act.astype(BF16), wsd_ref[...])
    out_ref[...] = x1_ref[...] + mod_ref[0, 5:6, :] * (routed + shared)


def _final(yk, wk_t, h2, x1, mod, wsgu, wsd, tiles_per_batch):
    t, d = h2.shape
    tm = ROW_TILE
    row = lambda w: pl.BlockSpec((tm, w), lambda i: (i, 0))
    return pl.pallas_call(
        _final_kernel,
        grid=(t // tm,),
        in_specs=[pl.BlockSpec((TOP_K, tm, d // 2), lambda i: (0, i, 0)),
                  row(TOP_K), row(d), row(d),
                  pl.BlockSpec((1, 6, d), lambda i: (i // tiles_per_batch, 0, 0)),
                  _const_spec(wsgu.shape), _const_spec(wsd.shape)],
        out_specs=row(d),
        out_shape=jax.ShapeDtypeStruct((t, d), F32),
        compiler_params=_params("arbitrary"),
        name="final",
    )(yk, wk_t, h2, x1, mod, wsgu, wsd)


def _placement():
    pq = np.zeros((LANES, N_HEADS * HEAD_PAD), np.float32)
    cq = np.zeros((1, N_HEADS * HEAD_PAD), np.float32)
    ck = np.zeros((1, N_HEADS * HEAD_PAD), np.float32)
    for hd in range(N_HEADS):
        for k in range(3):
            pq[k * N_HEADS + hd, hd * HEAD_PAD + AUG0 + k] = 1.0
            ck[0, hd * HEAD_PAD + AUG0 + k] = 1.0
            cq[0, hd * HEAD_PAD + AUG0 + 3 + k] = 1.0
    return jnp.asarray(pq, BF16), jnp.asarray(cq), jnp.asarray(ck)


def kernel(x, c, w_ada, b_ada, norm1_g, w_in, w_dw, b_dw, conv_gn_g, conv_gn_b, w_conv_out,
           q_norm_g, k_norm_g, b_forget, w_attn_out, w_out, norm2_g, w_router, router_bias,
           w_experts_gate_up, w_experts_down, w_shared_gate_up, w_shared_down):
    depth = w_ada.shape[0]
    b, s, d = x.shape

    pq, cq, ck = _placement()
    tri = jnp.asarray(np.tril(np.ones((ROW_TILE, ROW_TILE), np.float32)), BF16)
    grp = np.arange(CONV_DIM) // (CONV_DIM // CONV_GROUPS)
    gg = jnp.asarray((grp[:, None] == grp[None, :]).astype(np.float32) / (CONV_DIM // CONV_GROUPS), BF16)
    c_pad = jnp.pad(c, ((0, SUBLANES - b), (0, 0)))
    tri_pos = jnp.asarray(np.triu(np.ones((POS_TILE, POS_TILE), np.float32)), BF16)
    lstrict = jnp.asarray(np.tril(np.ones((N_EXPERTS, N_EXPERTS), np.float32), -1), BF16)

    for l in range(depth):
        mod = _ada(c_pad, w_ada[l], b_ada[l][None, :])[:b].reshape(b, 6, d)

        bf = jnp.pad(b_forget[l][None, :], ((0, 0), (0, LANES - N_HEADS)))
        gpad = lambda g, sc: jnp.tile(jnp.pad(g * sc, (0, HEAD_PAD - HEAD_DIM)), N_HEADS)[None, :]
        qscale = HEAD_DIM ** -0.5
        u, qa, ka, v, sgc, sga, cum = _inproj(
            x, mod, norm1_g[l][None, :], _inproj_weight(w_in[l]),
            bf, gpad(q_norm_g[l], qscale), gpad(k_norm_g[l], 1.0),
            pq, cq, ck, tri)

        flat = lambda a: a[:, :, :N_HEADS].transpose(0, 2, 1).reshape(-1)
        part = ATTN_TILE // DIAG_PARTS
        cs = flat(cum[:, 0::part])
        ce = flat(cum[:, part - 1::part])
        bound = (1.02 * HEAD_DIM * qscale) * jnp.max(jnp.abs(q_norm_g[l])) * jnp.max(jnp.abs(k_norm_g[l]))
        o = _attention(cs, ce, bound.reshape(1), qa, ka, v)

        wdw = jnp.pad(w_dw[l], ((0, CONV_HALO - CONV_WIDTH), (0, 0)))
        x1 = _merge(u, o, sgc, sga, x, mod, wdw, b_dw[l][None, :], conv_gn_g[l][None, :],
                    conv_gn_b[l][None, :], gg, w_conv_out[l].astype(BF16), w_attn_out[l].astype(BF16),
                    w_out[l].astype(BF16))

        wr = w_router[l].T
        wr_hi = wr.astype(BF16)
        wr_lo = (wr - wr_hi.astype(F32)).astype(BF16)
        h2, h2w, comb_t, cnt = _router(x1, mod, norm2_g[l][None, :], wr_hi, wr_lo, router_bias[l][:, None])

        t = b * s
        n_tiles = (t * TOP_K) // EXPERT_TILE + N_EXPERTS
        posk, wk, gend = _positions(comb_t, cnt, tri_pos, lstrict, n_tiles * EXPERT_TILE)
        seg_end = gend[:, 0].astype(jnp.int32)
        n_used = seg_end[-1:] // EXPERT_TILE
        tile_start = jnp.arange(n_tiles, dtype=jnp.int32) * EXPERT_TILE
        tile_start = jnp.minimum(tile_start, seg_end[-1] - EXPERT_TILE)
        tile_e = jnp.sum((seg_end[None, :] <= tile_start[:, None]).astype(jnp.int32), axis=1)

        xs = _sc_scatter_rows(h2w.reshape(t, d // 2), posk, n_tiles * EXPERT_TILE)
        ys = _experts(tile_e, n_used, xs, w_experts_gate_up[l], w_experts_down[l])
        yk = _sc_gather_rows(ys, posk.reshape(-1)).reshape(TOP_K, t, d // 2)
        out = _final(yk, wk.T, h2.reshape(t, d), x1.reshape(t, d), mod,
                     w_shared_gate_up[l].astype(BF16), w_shared_down[l].astype(BF16), s // ROW_TILE)
        x = out.reshape(b, s, d)
    return x
```

```python
import functools

import numpy as np
import jax
import jax.numpy as jnp
from jax import lax
from jax.experimental import pallas as pl
from jax.experimental.pallas import tpu as pltpu
from jax.experimental.pallas import tpu_sc as plsc

F32 = jnp.float32
BF16 = jnp.bfloat16

CONV_DIM = 512
CONV_WIDTH = 31
CONV_GROUPS = 8
N_HEADS = 8
HEAD_DIM = 64
ATTN_DIM = N_HEADS * HEAD_DIM
N_EXPERTS = 64
TOP_K = 8
N_GROUPS = 8
TOPK_GROUPS = 4
EXPERT_DIM = 256
SHARED_DIM = 256
ROUTED_SCALE = 2.5
EPS = 1e-6

LANES = 128
SUBLANES = 8
HEAD_PAD = LANES
AUG0 = HEAD_DIM
F_LANE0 = HEAD_DIM + 8
VMEM_LIMIT = 56 * 1024 * 1024

ROW_TILE = 512
ATTN_TILE = 1024
DIAG_PARTS = 4
EXPERT_TILE = 512
POS_TILE = 1024
SC_CHUNK = 128
CONV_HALO = 32

NEG_BIG = -1e30
EXP_UNDERFLOW = 104.0
FIXED_SHIFT_BOUND = 40.0


def _dot(a, b):
    return jnp.dot(a, b, preferred_element_type=F32)


def _dot_nt(a, b):
    return lax.dot_general(a, b, (((1,), (1,)), ((), ())), preferred_element_type=F32)


def _split2(x):
    hi = x.astype(BF16)
    lo = (x - hi.astype(F32)).astype(BF16)
    return hi, lo


def _split3(x):
    hi = x.astype(BF16)
    r = x - hi.astype(F32)
    mid = r.astype(BF16)
    lo = (r - mid.astype(F32)).astype(BF16)
    return hi, mid, lo


def _pack_halves(v):
    n = v.shape[1] // 2
    lo = lax.bitcast_convert_type(v[:, :n].astype(BF16).astype(F32), jnp.uint32)
    hi = lax.bitcast_convert_type(v[:, n:].astype(BF16).astype(F32), jnp.uint32)
    return lax.bitcast_convert_type(hi | lax.shift_right_logical(lo, jnp.uint32(16)), jnp.int32)


def _unpack_halves(w):
    u = lax.bitcast_convert_type(w, jnp.uint32)
    lo = lax.bitcast_convert_type(lax.shift_left(u, jnp.uint32(16)), F32)
    hi = lax.bitcast_convert_type(u & jnp.uint32(0xFFFF0000), F32)
    return lo, hi


def _sigmoid(x):
    return 1.0 / (1.0 + jnp.exp(-x))


def _silu(x):
    return x * _sigmoid(x)


def _params(*sem):
    return pltpu.CompilerParams(dimension_semantics=sem, vmem_limit_bytes=VMEM_LIMIT)


def _const_spec(shape):
    n = len(shape)
    return pl.BlockSpec(shape, lambda *_: (0,) * n, pipeline_mode=pl.Buffered(1))


def _ada_kernel(c_ref, w_ref, b_ref, o_ref):
    c = c_ref[...]
    a_hi, a_lo = _split2(_silu(c))
    w_hi, w_lo = _split2(w_ref[...])
    o_ref[...] = _dot(a_hi, w_hi) + _dot(a_hi, w_lo) + _dot(a_lo, w_hi) + b_ref[...]


def _ada(c_pad, w_ada, b_ada):
    d = c_pad.shape[1]
    n = w_ada.shape[1]
    return pl.pallas_call(
        _ada_kernel,
        grid=(n // d,),
        in_specs=[_const_spec(c_pad.shape),
                  pl.BlockSpec((d, d), lambda j: (0, j)),
                  pl.BlockSpec((1, d), lambda j: (0, j))],
        out_specs=pl.BlockSpec((c_pad.shape[0], d), lambda j: (0, j)),
        out_shape=jax.ShapeDtypeStruct((c_pad.shape[0], n), F32),
        compiler_params=_params("arbitrary"),
        name="ada",
    )(c_pad, w_ada, b_ada)


def _lane_pieces(x):
    hi, mid, lo = _split3(x)
    return (hi.astype(F32) + pltpu.roll(mid.astype(F32), N_HEADS, 1)
            + pltpu.roll(lo.astype(F32), 2 * N_HEADS, 1)).astype(BF16)


def _head_tile(p, col0, hd):
    per = HEAD_PAD // HEAD_DIM
    g, part = divmod(hd, per)
    blk = p[:, col0 + g * HEAD_PAD:col0 + (g + 1) * HEAD_PAD]
    if part:
        blk = pltpu.roll(blk, HEAD_PAD - part * HEAD_DIM, 1)
    lane = lax.broadcasted_iota(jnp.int32, (1, HEAD_PAD), 1)
    return jnp.where(lane < HEAD_DIM, blk, 0.0)


def _inproj_kernel(x_ref, mod_ref, g1_ref, w_ref, bf_ref, qg_ref, kg_ref, pq_ref, cq_ref, ck_ref, tri_ref,
                   u_ref, qa_ref, ka_ref, v_ref, sgc_ref, sga_ref, cum_ref, carry_ref):
    @pl.when(pl.program_id(1) == 0)
    def _():
        carry_ref[...] = jnp.zeros_like(carry_ref)

    x = x_ref[0]
    ms = jnp.mean(x * x, axis=-1, keepdims=True)
    h = (x * lax.rsqrt(ms + EPS) * g1_ref[...]) * (1.0 + mod_ref[0, 1:2, :]) + mod_ref[0, 0:1, :]
    hb = h.astype(BF16)

    hp = N_HEADS * HEAD_PAD
    o_q = 2 * CONV_DIM
    o_k, o_v = o_q + hp, o_q + 2 * hp
    o_gc = o_v + ATTN_DIM
    o_ga = o_gc + x.shape[1]
    pc = _dot(hb, w_ref[:, :o_q])
    u_ref[0] = (pc[:, :CONV_DIM] * _sigmoid(pc[:, CONV_DIM:])).astype(BF16)

    pq = _dot(hb, w_ref[:, o_q:o_k])
    z = pltpu.roll(pq[:, :HEAD_PAD], HEAD_PAD - F_LANE0, 1) + bf_ref[...]
    lf = jnp.minimum(z, 0.0) - jnp.log1p(jnp.exp(-jnp.abs(z)))
    lane = lax.broadcasted_iota(jnp.int32, (1, LANES), 1)
    psum = _dot(tri_ref[...], _lane_pieces(jnp.where(lane < N_HEADS, lf, 0.0)))
    cum = (psum + pltpu.roll(psum, LANES - N_HEADS, 1) + pltpu.roll(psum, LANES - 2 * N_HEADS, 1)
           + carry_ref[...])
    cum = jnp.where(lane < N_HEADS, cum, 0.0)
    carry_ref[...] = cum[ROW_TILE - 1:ROW_TILE, :]
    cum_ref[0] = cum

    pieces = _lane_pieces(cum)
    placed = _dot(pieces, pq_ref[...])
    addq = placed + cq_ref[...]
    addk = ck_ref[...] - pltpu.roll(placed, 3, 1)

    pk = _dot(hb, w_ref[:, o_k:o_v])
    pv = _dot(hb, w_ref[:, o_v:o_gc])
    inv_hd = 1.0 / HEAD_DIM
    hlane = lax.broadcasted_iota(jnp.int32, (1, HEAD_PAD), 1)
    vone = (hlane == HEAD_DIM).astype(F32)
    for hd in range(N_HEADS):
        sl = slice(hd * HEAD_PAD, (hd + 1) * HEAD_PAD)
        qb = pq[:, sl]
        if hd == 0:
            qb = jnp.where(hlane < HEAD_DIM, qb, 0.0)
        qn = qb * lax.rsqrt(jnp.sum(qb * qb, axis=-1, keepdims=True) * inv_hd + EPS) * qg_ref[:, sl]
        qa_ref[0, hd] = (qn + addq[:, sl]).astype(BF16)
        kb = pk[:, sl]
        kn = kb * lax.rsqrt(jnp.sum(kb * kb, axis=-1, keepdims=True) * inv_hd + EPS) * kg_ref[:, sl]
        ka_ref[0, hd] = (kn + addk[:, sl]).astype(BF16)
        v_ref[0, hd] = (_head_tile(pv, 0, hd) + vone).astype(BF16)

    sgc_ref[0] = _sigmoid(_dot(hb, w_ref[:, o_gc:o_ga])).astype(BF16)
    sga_ref[0] = _sigmoid(_dot(hb, w_ref[:, o_ga:])).astype(BF16)


def _inproj_weight(wi):
    d = wi.shape[0]
    o_q = 2 * CONV_DIM
    o_f = o_q + 3 * ATTN_DIM
    o_gc = o_f + N_HEADS
    heads = lambda w: jnp.pad(w.reshape(d, N_HEADS, HEAD_DIM), ((0, 0), (0, 0), (0, HEAD_PAD - HEAD_DIM)))
    wq = heads(wi[:, o_q:o_q + ATTN_DIM]).at[:, 0, F_LANE0:F_LANE0 + N_HEADS].set(wi[:, o_f:o_gc])
    wk = heads(wi[:, o_q + ATTN_DIM:o_q + 2 * ATTN_DIM])
    flat = lambda w: w.reshape(d, N_HEADS * HEAD_PAD)
    return jnp.concatenate(
        [wi[:, :o_q], flat(wq), flat(wk), wi[:, o_q + 2 * ATTN_DIM:o_f], wi[:, o_gc:]], axis=1).astype(BF16)


def _inproj(x, mod, g1, w, bf, qg, kg, pq, cq, ck, tri):
    b, s, d = x.shape
    tm = ROW_TILE
    row = lambda w: pl.BlockSpec((1, tm, w), lambda bi, i: (bi, i, 0))
    head = pl.BlockSpec((1, N_HEADS, tm, HEAD_PAD), lambda bi, i: (bi, 0, i, 0))
    consts = [g1, w, bf, qg, kg, pq, cq, ck, tri]
    return pl.pallas_call(
        _inproj_kernel,
        grid=(b, s // tm),
        in_specs=[row(d), pl.BlockSpec((1, 6, d), lambda bi, i: (bi, 0, 0))]
                 + [_const_spec(a.shape) for a in consts],
        out_specs=[row(CONV_DIM), head, head, head, row(d), row(d), row(LANES)],
        out_shape=[jax.ShapeDtypeStruct((b, s, CONV_DIM), BF16),
                   jax.ShapeDtypeStruct((b, N_HEADS, s, HEAD_PAD), BF16),
                   jax.ShapeDtypeStruct((b, N_HEADS, s, HEAD_PAD), BF16),
                   jax.ShapeDtypeStruct((b, N_HEADS, s, HEAD_PAD), BF16),
                   jax.ShapeDtypeStruct((b, s, d), BF16),
                   jax.ShapeDtypeStruct((b, s, d), BF16),
                   jax.ShapeDtypeStruct((b, s, LANES), F32)],
        scratch_shapes=[pltpu.VMEM((1, LANES), F32)],
        compiler_params=_params("arbitrary", "arbitrary"),
        name="inproj",
    )(x, mod, *consts)


def _attn_kernel(cs_ref, ce_ref, prm_ref, q_ref, k_ref, v_ref, o_ref, acc_ref, m_ref):
    t = ATTN_TILE
    nb = q_ref.shape[2] // t
    base = (pl.program_id(0) * pl.num_programs(1) + pl.program_id(1)) * nb
    bound = prm_ref[0]
    thresh = -(EXP_UNDERFLOW + 2.0 * bound)
    causal = lax.broadcasted_iota(jnp.int32, (t, t), 0) >= lax.broadcasted_iota(jnp.int32, (t, t), 1)

    dp = DIAG_PARTS

    def first_block(i):
        c0 = cs_ref[dp * (base + i)]
        return lax.fori_loop(
            0, i, lambda j, n: n + (c0 - ce_ref[dp * (base + j) + dp - 1] < thresh).astype(jnp.int32), 0)

    def scores(q, j, masked):
        k0 = pl.multiple_of(j * t, t)
        s = _dot_nt(q, k_ref[0, 0, pl.ds(k0, t), :])
        if masked:
            s = jnp.where(causal, s, NEG_BIG)
        return s, v_ref[0, 0, pl.ds(k0, t), :]

    def finish(q0):
        acc = acc_ref[...]
        o_ref[0, pl.ds(q0, t), :] = (acc / acc[:, HEAD_DIM:HEAD_DIM + 1]).astype(BF16)

    def fixed_shift(i, carry):
        q0 = pl.multiple_of(i * t, t)
        q = q_ref[0, 0, pl.ds(q0, t), :]

        acc_ref[...] = jnp.zeros_like(acc_ref)

        def weighted(j, masked):
            s, vb = scores(q, j, masked)
            return _dot(jnp.exp(s).astype(BF16), vb)

        def kv(j, c):
            acc_ref[...] += weighted(j, False)
            return c

        lax.fori_loop(first_block(i), i - 1, kv, 0)
        rp = t // DIAG_PARTS

        @pl.when(i == 0)
        def _():
            for r in range(DIAG_PARTS):
                nk = (r + 1) * rp
                s = _dot_nt(q[r * rp:(r + 1) * rp, :], k_ref[0, 0, pl.ds(q0, nk), :])
                seen = (lax.broadcasted_iota(jnp.int32, (rp, nk), 0) + r * rp
                        >= lax.broadcasted_iota(jnp.int32, (rp, nk), 1))
                p = jnp.exp(jnp.where(seen, s, NEG_BIG)).astype(BF16)
                acc_ref[r * rp:(r + 1) * rp, :] += _dot(p, v_ref[0, 0, pl.ds(q0, nk), :])
            finish(q0)

        @pl.when(i > 0)
        def _():
            nk = t + rp
            for r in range(DIAG_PARTS):
                k0 = pl.multiple_of(q0 - t + r * rp, rp)
                s = _dot_nt(q[r * rp:(r + 1) * rp, :], k_ref[0, 0, pl.ds(k0, nk), :])
                seen = (lax.broadcasted_iota(jnp.int32, (rp, nk), 0) + t
                        >= lax.broadcasted_iota(jnp.int32, (rp, nk), 1))
                p = jnp.exp(jnp.where(seen, s, NEG_BIG)).astype(BF16)
                acc_ref[r * rp:(r + 1) * rp, :] += _dot(p, v_ref[0, 0, pl.ds(k0, nk), :])

            for r in range(1, DIAG_PARTS):
                need = cs_ref[dp * (base + i) + r] - ce_ref[dp * (base + i - 1) + r - 1] >= thresh

                @pl.when(need)
                def _():
                    kp = pl.multiple_of(q0 - t, t)
                    s = _dot_nt(q[r * rp:(r + 1) * rp, :], k_ref[0, 0, pl.ds(kp, r * rp), :])
                    acc_ref[r * rp:(r + 1) * rp, :] += _dot(jnp.exp(s).astype(BF16),
                                                            v_ref[0, 0, pl.ds(kp, r * rp), :])

            finish(q0)

        return carry

    def running_max(i, carry):
        q0 = pl.multiple_of(i * t, t)
        q = q_ref[0, 0, pl.ds(q0, t), :]
        m_ref[...] = jnp.full_like(m_ref, -jnp.inf)
        acc_ref[...] = jnp.zeros_like(acc_ref)

        def step(j, masked):
            s, vb = scores(q, j, masked)
            m_prev = m_ref[...]
            m_new = jnp.maximum(m_prev, jnp.max(s, axis=-1, keepdims=True))
            p = jnp.exp(s - m_new)
            acc_ref[...] = jnp.exp(m_prev - m_new) * acc_ref[...] + _dot(p.astype(BF16), vb)
            m_ref[...] = m_new

        def kv(j, c):
            step(j, False)
            return c

        lax.fori_loop(first_block(i), i, kv, 0)
        step(i, True)
        finish(q0)
        return carry

    @pl.when(bound <= FIXED_SHIFT_BOUND)
    def _():
        lax.fori_loop(0, nb, fixed_shift, 0)

    @pl.when(bound > FIXED_SHIFT_BOUND)
    def _():
        lax.fori_loop(0, nb, running_max, 0)


def _attention(cs, ce, prm, qa, ka, v):
    b, nh, s, hp = qa.shape
    t = ATTN_TILE
    seq = pl.BlockSpec((1, 1, s, hp), lambda bi, hi, *_: (bi, hi, 0, 0))
    return pl.pallas_call(
        _attn_kernel,
        grid_spec=pltpu.PrefetchScalarGridSpec(
            num_scalar_prefetch=3,
            grid=(b, nh),
            in_specs=[seq, seq, seq],
            out_specs=pl.BlockSpec((1, s, hp), lambda bi, hi, *_: (bi, 0, hi)),
            scratch_shapes=[pltpu.VMEM((t, hp), F32), pltpu.VMEM((t, 1), F32)]),
        out_shape=jax.ShapeDtypeStruct((b, s, nh * hp), BF16),
        compiler_params=_params("arbitrary", "arbitrary"),
        name="attn",
    )(cs, ce, prm, qa, ka, v)


def _merge_kernel(u_ref, halo_ref, o_ref, sgc_ref, sga_ref, x_ref, mod_ref, wdw_ref, bdw_ref,
                  gng_ref, gnb_ref, gg_ref, wco_ref, wao_ref, wout_ref, x1_ref, buf_ref):
    tm = ROW_TILE
    halo = halo_ref[0].astype(F32)
    halo = jnp.where(pl.program_id(1) == 0, jnp.zeros_like(halo), halo)
    ucur = u_ref[0].astype(F32)
    for cb in range(CONV_DIM // LANES):
        buf_ref[cb, 0:CONV_HALO, :] = halo[:, cb * LANES:(cb + 1) * LANES]
        buf_ref[cb, CONV_HALO:, :] = ucur[:, cb * LANES:(cb + 1) * LANES]

    base = CONV_HALO - (CONV_WIDTH - 1)
    ys = []
    for cb in range(CONV_DIM // LANES):
        acc = jnp.zeros((tm, LANES), F32)
        for j in range(CONV_WIDTH):
            acc = acc + wdw_ref[j:j + 1, cb * LANES:(cb + 1) * LANES] * buf_ref[cb, base + j:base + j + tm, :]
        ys.append(acc)
    y = jnp.concatenate(ys, axis=1) + bdw_ref[...]

    gg = gg_ref[...]
    y_hi, y_lo = _split2(y)
    dlt = y - (_dot(y_hi, gg) + _dot(y_lo, gg))
    s_hi, s_lo = _split2(dlt * dlt)
    var = _dot(s_hi, gg) + _dot(s_lo, gg)
    yn = dlt * lax.rsqrt(var + EPS) * gng_ref[...] + gnb_ref[...]
    y_conv = _dot(_silu(yn).astype(BF16), wco_ref[...])

    per = HEAD_PAD // HEAD_DIM
    lane = lax.broadcasted_iota(jnp.int32, (1, HEAD_PAD), 1)
    packed = []
    for g in range(N_HEADS // per):
        tile = o_ref[0, :, g * per * HEAD_PAD:(g * per + 1) * HEAD_PAD].astype(F32)
        for part in range(1, per):
            nxt = o_ref[0, :, (g * per + part) * HEAD_PAD:(g * per + part + 1) * HEAD_PAD].astype(F32)
            tile = jnp.where(lane < part * HEAD_DIM, tile, pltpu.roll(nxt, part * HEAD_DIM, 1))
        packed.append(tile)
    y_attn = _dot(jnp.concatenate(packed, axis=1).astype(BF16), wao_ref[...])
    merged = sgc_ref[0].astype(F32) * y_conv + sga_ref[0].astype(F32) * y_attn
    mix = _dot(merged.astype(BF16), wout_ref[...])
    x1_ref[0] = x_ref[0] + mod_ref[0, 2:3, :] * mix


def _merge(u, o, sgc, sga, x, mod, wdw, bdw, gng, gnb, gg, wco, wao, wout):
    b, s, d = x.shape
    tm = ROW_TILE
    per = tm // CONV_HALO
    row = lambda w: pl.BlockSpec((1, tm, w), lambda bi, i: (bi, i, 0))
    consts = [wdw, bdw, gng, gnb, gg, wco, wao, wout]
    return pl.pallas_call(
        _merge_kernel,
        grid=(b, s // tm),
        in_specs=[row(CONV_DIM),
                  pl.BlockSpec((1, CONV_HALO, CONV_DIM),
                               lambda bi, i: (bi, jnp.maximum(i * per - 1, 0), 0)),
                  row(o.shape[2]), row(d), row(d), row(d),
                  pl.BlockSpec((1, 6, d), lambda bi, i: (bi, 0, 0))]
                 + [_const_spec(a.shape) for a in consts],
        out_specs=row(d),
        out_shape=jax.ShapeDtypeStruct((b, s, d), F32),
        scratch_shapes=[pltpu.VMEM((CONV_DIM // LANES, CONV_HALO + tm, LANES), F32)],
        compiler_params=_params("arbitrary", "arbitrary"),
        name="merge",
    )(u, u, o, sgc, sga, x, mod, *consts)


def _router_kernel(x1_ref, mod_ref, g2_ref, wr_hi_ref, wr_lo_ref, rb_ref, h2_ref, h2w_ref, comb_ref, cnt_ref):
    x = x1_ref[0]
    ms = jnp.mean(x * x, axis=-1, keepdims=True)
    h = (x * lax.rsqrt(ms + EPS) * g2_ref[...]) * (1.0 + mod_ref[0, 4:5, :]) + mod_ref[0, 3:4, :]
    h2_ref[0] = h.astype(BF16)
    h2w_ref[0] = _pack_halves(h)

    h_hi, h_lo = _split2(h)
    logits = _dot_nt(wr_hi_ref[...], h_hi) + _dot_nt(wr_hi_ref[...], h_lo) + _dot_nt(wr_lo_ref[...], h_hi)
    scores = _sigmoid(logits)
    biased = scores + rb_ref[...]

    per = N_EXPERTS // N_GROUPS
    rows = lax.broadcasted_iota(jnp.int32, (per, biased.shape[1]), 0)
    gscore = []
    for g in range(N_GROUPS):
        blk = biased[g * per:(g + 1) * per, :]
        top1 = jnp.max(blk, axis=0, keepdims=True)
        first = jnp.min(jnp.where(blk == top1, rows, per), axis=0, keepdims=True)
        top2 = jnp.max(jnp.where(rows == first, -jnp.inf, blk), axis=0, keepdims=True)
        gscore.append(top1 + top2)

    cand = []
    for g in range(N_GROUPS):
        rank = jnp.zeros_like(gscore[g], dtype=jnp.int32)
        for g2 in range(N_GROUPS):
            if g2 == g:
                continue
            ahead = gscore[g2] > gscore[g]
            if g2 < g:
                ahead = ahead | (gscore[g2] == gscore[g])
            rank = rank + ahead.astype(jnp.int32)
        keep = rank < TOPK_GROUPS
        cand.append(jnp.where(keep, biased[g * per:(g + 1) * per, :], -jnp.inf))
    cand = jnp.concatenate(cand, axis=0)

    eidx = lax.broadcasted_iota(jnp.int32, cand.shape, 0)
    work = cand
    for _ in range(TOP_K):
        top = jnp.max(work, axis=0, keepdims=True)
        first = jnp.min(jnp.where(work == top, eidx, N_EXPERTS), axis=0, keepdims=True)
        work = jnp.where(eidx == first, -jnp.inf, work)
    sel = (work != cand) & (cand > -jnp.inf)
    w = jnp.where(sel, scores, 0.0)
    comb = w / jnp.sum(w, axis=0, keepdims=True) * ROUTED_SCALE
    comb_ref[...] = comb

    @pl.when((pl.program_id(0) == 0) & (pl.program_id(1) == 0))
    def _():
        cnt_ref[...] = jnp.zeros_like(cnt_ref)

    hit = jnp.where(comb != 0.0, 1.0, 0.0).astype(BF16)
    cnt_ref[...] += _dot(hit, jnp.ones((hit.shape[1], LANES), BF16))


def _router(x1, mod, g2, wr_hi, wr_lo, rb):
    b, s, d = x1.shape
    tm = ROW_TILE
    nt = s // tm
    return pl.pallas_call(
        _router_kernel,
        grid=(b, nt),
        in_specs=[pl.BlockSpec((1, tm, d), lambda bi, i: (bi, i, 0)),
                  pl.BlockSpec((1, 6, d), lambda bi, i: (bi, 0, 0)),
                  _const_spec(g2.shape), _const_spec(wr_hi.shape), _const_spec(wr_lo.shape),
                  _const_spec(rb.shape)],
        out_specs=[pl.BlockSpec((1, tm, d), lambda bi, i: (bi, i, 0)),
                   pl.BlockSpec((1, tm, d // 2), lambda bi, i: (bi, i, 0)),
                   pl.BlockSpec((N_EXPERTS, tm), lambda bi, i: (0, bi * nt + i)),
                   pl.BlockSpec((N_EXPERTS, LANES), lambda bi, i: (0, 0))],
        out_shape=[jax.ShapeDtypeStruct((b, s, d), BF16),
                   jax.ShapeDtypeStruct((b, s, d // 2), jnp.int32),
                   jax.ShapeDtypeStruct((N_EXPERTS, b * s), F32),
                   jax.ShapeDtypeStruct((N_EXPERTS, LANES), F32)],
        compiler_params=_params("arbitrary", "arbitrary"),
        name="router",
    )(x1, mod, g2, wr_hi, wr_lo, rb)


def _pos_kernel(comb_ref, cnt_ref, tri_ref, lstrict_ref, posk_ref, wk_ref, gend_ref, base_ref, *, spare_row):
    tp = comb_ref.shape[1]
    comb = comb_ref[...]
    sel = comb != 0.0
    selb = jnp.where(sel, 1.0, 0.0).astype(BF16)

    @pl.when(pl.program_id(0) == 0)
    def _():
        seg = jnp.floor((cnt_ref[...] + (EXPERT_TILE - 1.0)) * (1.0 / EXPERT_TILE)) * EXPERT_TILE
        s_hi, s_mid, s_lo = _split3(seg)
        ls = lstrict_ref[...]
        start = _dot(ls, s_hi) + _dot(ls, s_mid) + _dot(ls, s_lo)
        base_ref[...] = start
        gend_ref[...] = start + seg

    rank = _dot(selb, tri_ref[...])
    pos = base_ref[:, 0:1] + rank - 1.0
    base_ref[...] += _dot(selb, jnp.ones((tp, LANES), BF16))
    slot = _dot(lstrict_ref[...], selb)
    rows_p, rows_w = [], []
    for k in range(TOP_K):
        m = sel & (slot == k)
        rows_p.append(jnp.sum(jnp.where(m, pos - spare_row, 0.0), axis=0, keepdims=True) + spare_row)
        rows_w.append(jnp.sum(jnp.where(m, comb, 0.0), axis=0, keepdims=True))
    posk_ref[...] = jnp.concatenate(rows_p, axis=0).astype(jnp.int32)
    wk_ref[...] = jnp.concatenate(rows_w, axis=0)


def _positions(comb_t, cnt, tri, lstrict, n_rows):
    ne, t = comb_t.shape
    tp = POS_TILE
    tok = lambda rows: pl.BlockSpec((rows, tp), lambda i: (0, i))
    return pl.pallas_call(
        functools.partial(_pos_kernel, spare_row=float(n_rows - 1)),
        grid=(t // tp,),
        in_specs=[tok(ne), _const_spec(cnt.shape), _const_spec(tri.shape), _const_spec(lstrict.shape)],
        out_specs=[tok(TOP_K), tok(TOP_K), pl.BlockSpec((ne, LANES), lambda i: (0, 0))],
        out_shape=[jax.ShapeDtypeStruct((TOP_K, t), jnp.int32),
                   jax.ShapeDtypeStruct((TOP_K, t), F32),
                   jax.ShapeDtypeStruct((ne, LANES), F32)],
        scratch_shapes=[pltpu.VMEM((ne, LANES), F32)],
        compiler_params=_params("arbitrary"),
        name="positions",
    )(comb_t, cnt, tri, lstrict)


def _sc_workers():
    info = plsc.get_sparse_core_info()
    return info.num_cores, info.num_cores * info.num_subcores


def _sc_scatter_rows(rows, pos, n_out):
    nc, nw = _sc_workers()
    n, w = rows.shape
    nk = pos.shape[0]
    ch = SC_CHUNK
    per_w = n // nw
    assert per_w * nw == n and per_w % ch == 0

    @functools.partial(
        pl.kernel, mesh=plsc.VectorSubcoreMesh(core_axis_name="c", subcore_axis_name="s"),
        out_type=jax.ShapeDtypeStruct((n_out, w), rows.dtype),
        scratch_types=[pltpu.VMEM((nk, ch), jnp.int32), pltpu.VMEM((ch, w), rows.dtype),
                       pltpu.SemaphoreType.DMA])
    def scatter(rows_hbm, pos_hbm, out_hbm, idx_v, rows_v, sem):
        base = (lax.axis_index("s") * nc + lax.axis_index("c")) * per_w

        @pl.loop(0, per_w // ch)
        def _(ci):
            off = pl.multiple_of(base + ci * ch, ch)
            pltpu.sync_copy(pos_hbm.at[:, pl.ds(off, ch)], idx_v)
            pltpu.sync_copy(rows_hbm.at[pl.ds(off, ch)], rows_v)
            copies = [pltpu.make_async_copy(rows_v, out_hbm.at[idx_v.at[k]], sem) for k in range(nk)]
            for cp in copies:
                cp.start()
            for cp in copies:
                cp.wait()

    return scatter(rows, pos)


def _sc_gather_rows(table, idx):
    nc, nw = _sc_workers()
    n = idx.shape[0]
    w = table.shape[1]
    ch = SC_CHUNK
    per_w = n // nw
    assert per_w * nw == n and per_w % ch == 0
    nch = per_w // ch

    @functools.partial(
        pl.kernel, mesh=plsc.VectorSubcoreMesh(core_axis_name="c", subcore_axis_name="s"),
        out_type=jax.ShapeDtypeStruct((n, w), table.dtype),
        scratch_types=[pltpu.VMEM((nch, ch), jnp.int32), pltpu.VMEM((ch, w), table.dtype),
                       pltpu.SemaphoreType.DMA])
    def gather(table_hbm, idx_hbm, out_hbm, idx_v, rows_v, sem):
        wid = lax.axis_index("s") * nc + lax.axis_index("c")
        base = wid * per_w
        pltpu.sync_copy(idx_hbm.at[pl.ds(wid * nch, nch)], idx_v)

        @pl.loop(0, nch)
        def _(ci):
            off = pl.multiple_of(base + ci * ch, ch)
            cp = pltpu.make_async_copy(table_hbm.at[idx_v.at[ci]], rows_v, sem)
            cp.start()
            cp.wait()
            pltpu.sync_copy(rows_v, out_hbm.at[pl.ds(off, ch)])

    return gather(table, idx.reshape(n // ch, ch))


def _expert_kernel(te_ref, nu_ref, xs_ref, wgu_ref, wd_ref, y_ref, wgu_b, wd_b):
    i = pl.program_id(0)

    @pl.when(i < nu_ref[0])
    def _():
        @pl.when((i == 0) | (te_ref[i] != te_ref[jnp.maximum(i - 1, 0)]))
        def _():
            wgu_b[...] = wgu_ref[0].astype(BF16)
            wd_b[...] = wd_ref[0].astype(BF16)

        lo, hi = _unpack_halves(xs_ref[...])
        half = wgu_b.shape[0] // 2
        gu = _dot(lo.astype(BF16), wgu_b[:half, :]) + _dot(hi.astype(BF16), wgu_b[half:, :])
        act = _silu(gu[:, :EXPERT_DIM]) * gu[:, EXPERT_DIM:]
        y_ref[...] = _pack_halves(_dot(act.astype(BF16), wd_b[...]))


def _experts(tile_e, n_used, xs, wgu, wd):
    r, w = xs.shape
    tm = EXPERT_TILE
    d = wgu.shape[1]
    rows = pl.BlockSpec((tm, w), lambda i, te, nu: (jnp.minimum(i, nu[0] - 1), 0))
    return pl.pallas_call(
        _expert_kernel,
        grid_spec=pltpu.PrefetchScalarGridSpec(
            num_scalar_prefetch=2,
            grid=(r // tm,),
            in_specs=[rows,
                      pl.BlockSpec((1, d, 2 * EXPERT_DIM), lambda i, te, nu: (te[i], 0, 0)),
                      pl.BlockSpec((1, EXPERT_DIM, d), lambda i, te, nu: (te[i], 0, 0))],
            out_specs=rows,
            scratch_shapes=[pltpu.VMEM((d, 2 * EXPERT_DIM), BF16), pltpu.VMEM((EXPERT_DIM, d), BF16)]),
        out_shape=jax.ShapeDtypeStruct((r, w), jnp.int32),
        compiler_params=_params("arbitrary"),
        name="experts",
    )(tile_e, n_used, xs, wgu, wd)


def _final_kernel(yk_ref, wk_ref, h_ref, x1_ref, mod_ref, wsgu_ref, wsd_ref, out_ref):
    half = yk_ref.shape[2]
    acc_lo = jnp.zeros((yk_ref.shape[1], half), F32)
    acc_hi = jnp.zeros((yk_ref.shape[1], half), F32)
    for k in range(TOP_K):
        lo, hi = _unpack_halves(yk_ref[k])
        w = wk_ref[:, k:k + 1]
        acc_lo = acc_lo + jnp.where(w != 0.0, w * lo, 0.0)
        acc_hi = acc_hi + jnp.where(w != 0.0, w * hi, 0.0)
    routed = jnp.concatenate([acc_lo, acc_hi], axis=1)
    sgu = _dot(h_ref[...], wsgu_ref[...])
    act = _silu(sgu[:, :SHARED_DIM]) * sgu[:, SHARED_DIM:]
    shared = _dot(act.astype(BF16), wsd_ref[...])
    out_ref[...] = x1_ref[...] + mod_ref[0, 5:6, :] * (routed + shared)


def _final(yk, wk_t, h2, x1, mod, wsgu, wsd, tiles_per_batch):
    t, d = h2.shape
    tm = ROW_TILE
    row = lambda w: pl.BlockSpec((tm, w), lambda i: (i, 0))
    return pl.pallas_call(
        _final_kernel,
        grid=(t // tm,),
        in_specs=[pl.BlockSpec((TOP_K, tm, d // 2), lambda i: (0, i, 0)),
                  row(TOP_K), row(d), row(d),
                  pl.BlockSpec((1, 6, d), lambda i: (i // tiles_per_batch, 0, 0)),
                  _const_spec(wsgu.shape), _const_spec(wsd.shape)],
        out_specs=row(d),
        out_shape=jax.ShapeDtypeStruct((t, d), F32),
        compiler_params=_params("arbitrary"),
        name="final",
    )(yk, wk_t, h2, x1, mod, wsgu, wsd)


def _placement():
    pq = np.zeros((LANES, N_HEADS * HEAD_PAD), np.float32)
    cq = np.zeros((1, N_HEADS * HEAD_PAD), np.float32)
    ck = np.zeros((1, N_HEADS * HEAD_PAD), np.float32)
    for hd in range(N_HEADS):
        for k in range(3):
            pq[k * N_HEADS + hd, hd * HEAD_PAD + AUG0 + k] = 1.0
            ck[0, hd * HEAD_PAD + AUG0 + k] = 1.0
            cq[0, hd * HEAD_PAD + AUG0 + 3 + k] = 1.0
    return jnp.asarray(pq, BF16), jnp.asarray(cq), jnp.asarray(ck)


def kernel(x, c, w_ada, b_ada, norm1_g, w_in, w_dw, b_dw, conv_gn_g, conv_gn_b, w_conv_out,
           q_norm_g, k_norm_g, b_forget, w_attn_out, w_out, norm2_g, w_router, router_bias,
           w_experts_gate_up, w_experts_down, w_shared_gate_up, w_shared_down):
    depth = w_ada.shape[0]
    b, s, d = x.shape

    pq, cq, ck = _placement()
    tri = jnp.asarray(np.tril(np.ones((ROW_TILE, ROW_TILE), np.float32)), BF16)
    grp = np.arange(CONV_DIM) // (CONV_DIM // CONV_GROUPS)
    gg = jnp.asarray((grp[:, None] == grp[None, :]).astype(np.float32) / (CONV_DIM // CONV_GROUPS), BF16)
    c_pad = jnp.pad(c, ((0, SUBLANES - b), (0, 0)))
    tri_pos = jnp.asarray(np.triu(np.ones((POS_TILE, POS_TILE), np.float32)), BF16)
    lstrict = jnp.asarray(np.tril(np.ones((N_EXPERTS, N_EXPERTS), np.float32), -1), BF16)

    for l in range(depth):
        mod = _ada(c_pad, w_ada[l], b_ada[l][None, :])[:b].reshape(b, 6, d)

        bf = jnp.pad(b_forget[l][None, :], ((0, 0), (0, LANES - N_HEADS)))
        gpad = lambda g, sc: jnp.tile(jnp.pad(g * sc, (0, HEAD_PAD - HEAD_DIM)), N_HEADS)[None, :]
        qscale = HEAD_DIM ** -0.5
        u, qa, ka, v, sgc, sga, cum = _inproj(
            x, mod, norm1_g[l][None, :], _inproj_weight(w_in[l]),
            bf, gpad(q_norm_g[l], qscale), gpad(k_norm_g[l], 1.0),
            pq, cq, ck, tri)

        flat = lambda a: a[:, :, :N_HEADS].transpose(0, 2, 1).reshape(-1)
        part = ATTN_TILE // DIAG_PARTS
        cs = flat(cum[:, 0::part])
        ce = flat(cum[:, part - 1::part])
        bound = (1.02 * HEAD_DIM * qscale) * jnp.max(jnp.abs(q_norm_g[l])) * jnp.max(jnp.abs(k_norm_g[l]))
        o = _attention(cs, ce, bound.reshape(1), qa, ka, v)

        wdw = jnp.pad(w_dw[l], ((0, CONV_HALO - CONV_WIDTH), (0, 0)))
        x1 = _merge(u, o, sgc, sga, x, mod, wdw, b_dw[l][None, :], conv_gn_g[l][None, :],
                    conv_gn_b[l][None, :], gg, w_conv_out[l].astype(BF16), w_attn_out[l].astype(BF16),
                    w_out[l].astype(BF16))

        wr = w_router[l].T
        wr_hi = wr.astype(BF16)
        wr_lo = (wr - wr_hi.astype(F32)).astype(BF16)
        h2, h2w, comb_t, cnt = _router(x1, mod, norm2_g[l][None, :], wr_hi, wr_lo, router_bias[l][:, None])

        t = b * s
        n_tiles = (t * TOP_K) // EXPERT_TILE + N_EXPERTS
        posk, wk, gend = _positions(comb_t, cnt, tri_pos, lstrict, n_tiles * EXPERT_TILE)
        seg_end = gend[:, 0].astype(jnp.int32)
        n_used = seg_end[-1:] // EXPERT_TILE
        tile_start = jnp.arange(n_tiles, dtype=jnp.int32) * EXPERT_TILE
        tile_start = jnp.minimum(tile_start, seg_end[-1] - EXPERT_TILE)
        tile_e = jnp.sum((seg_end[None, :] <= tile_start[:, None]).astype(jnp.int32), axis=1)

        xs = _sc_scatter_rows(h2w.reshape(t, d // 2), posk, n_tiles * EXPERT_TILE)
        ys = _experts(tile_e, n_used, xs, w_experts_gate_up[l], w_experts_down[l])
        yk = _sc_gather_rows(ys, posk.reshape(-1)).reshape(TOP_K, t, d // 2)
        out = _final(yk, wk.T, h2.reshape(t, d), x1.reshape(t, d), mod,
                     w_shared_gate_up[l].astype(BF16), w_shared_down[l].astype(BF16), s // ROW_TILE)
        x = out.reshape(b, s, d)
    return x
```

```python
import functools

import numpy as np
import jax
import jax.numpy as jnp
from jax import lax
from jax.experimental import pallas as pl
from jax.experimental.pallas import tpu as pltpu
from jax.experimental.pallas import tpu_sc as plsc

F32 = jnp.float32
BF16 = jnp.bfloat16

CONV_DIM = 512
CONV_WIDTH = 31
CONV_GROUPS = 8
N_HEADS = 8
HEAD_DIM = 64
ATTN_DIM = N_HEADS * HEAD_DIM
N_EXPERTS = 64
TOP_K = 8
N_GROUPS = 8
TOPK_GROUPS = 4
EXPERT_DIM = 256
SHARED_DIM = 256
ROUTED_SCALE = 2.5
EPS = 1e-6

LANES = 128
SUBLANES = 8
HEAD_PAD = LANES
AUG0 = HEAD_DIM
F_LANE0 = HEAD_DIM + 8
VMEM_LIMIT = 56 * 1024 * 1024

ROW_TILE = 512
ATTN_TILE = 1024
DIAG_PARTS = 4
EXPERT_TILE = 512
WPREP_ROWS = 256
POS_TILE = 1024
SC_CHUNK = 128
CONV_HALO = 32

NEG_BIG = -1e30
EXP_UNDERFLOW = 104.0
FIXED_SHIFT_BOUND = 40.0


def _dot(a, b):
    return jnp.dot(a, b, preferred_element_type=F32)


def _dot_nt(a, b):
    return lax.dot_general(a, b, (((1,), (1,)), ((), ())), preferred_element_type=F32)


def _split2(x):
    hi = x.astype(BF16)
    lo = (x - hi.astype(F32)).astype(BF16)
    return hi, lo


def _split3(x):
    hi = x.astype(BF16)
    r = x - hi.astype(F32)
    mid = r.astype(BF16)
    lo = (r - mid.astype(F32)).astype(BF16)
    return hi, mid, lo


def _pack_halves(v):
    n = v.shape[1] // 2
    lo = lax.bitcast_convert_type(v[:, :n].astype(BF16).astype(F32), jnp.uint32)
    hi = lax.bitcast_convert_type(v[:, n:].astype(BF16).astype(F32), jnp.uint32)
    return lax.bitcast_convert_type(hi | lax.shift_right_logical(lo, jnp.uint32(16)), jnp.int32)


def _unpack_halves(w):
    u = lax.bitcast_convert_type(w, jnp.uint32)
    lo = lax.bitcast_convert_type(lax.shift_left(u, jnp.uint32(16)), F32)
    hi = lax.bitcast_convert_type(u & jnp.uint32(0xFFFF0000), F32)
    return lo, hi


def _sigmoid(x):
    return 1.0 / (1.0 + jnp.exp(-x))


def _silu(x):
    return x * _sigmoid(x)


def _params(*sem):
    return pltpu.CompilerParams(dimension_semantics=sem, vmem_limit_bytes=VMEM_LIMIT)


def _const_spec(shape):
    n = len(shape)
    return pl.BlockSpec(shape, lambda *_: (0,) * n, pipeline_mode=pl.Buffered(1))


def _ada_kernel(c_ref, w_ref, b_ref, o_ref):
    c = c_ref[...]
    a_hi, a_lo = _split2(_silu(c))
    w_hi, w_lo = _split2(w_ref[...])
    o_ref[...] = _dot(a_hi, w_hi) + _dot(a_hi, w_lo) + _dot(a_lo, w_hi) + b_ref[...]


def _ada(c_pad, w_ada, b_ada):
    d = c_pad.shape[1]
    n = w_ada.shape[1]
    return pl.pallas_call(
        _ada_kernel,
        grid=(n // d,),
        in_specs=[_const_spec(c_pad.shape),
                  pl.BlockSpec((d, d), lambda j: (0, j)),
                  pl.BlockSpec((1, d), lambda j: (0, j))],
        out_specs=pl.BlockSpec((c_pad.shape[0], d), lambda j: (0, j)),
        out_shape=jax.ShapeDtypeStruct((c_pad.shape[0], n), F32),
        compiler_params=_params("arbitrary"),
        name="ada",
    )(c_pad, w_ada, b_ada)


def _lane_pieces(x):
    hi, mid, lo = _split3(x)
    return (hi.astype(F32) + pltpu.roll(mid.astype(F32), N_HEADS, 1)
            + pltpu.roll(lo.astype(F32), 2 * N_HEADS, 1)).astype(BF16)


def _head_tile(p, col0, hd):
    per = HEAD_PAD // HEAD_DIM
    g, part = divmod(hd, per)
    blk = p[:, col0 + g * HEAD_PAD:col0 + (g + 1) * HEAD_PAD]
    if part:
        blk = pltpu.roll(blk, HEAD_PAD - part * HEAD_DIM, 1)
    lane = lax.broadcasted_iota(jnp.int32, (1, HEAD_PAD), 1)
    return jnp.where(lane < HEAD_DIM, blk, 0.0)


def _inproj_kernel(x_ref, mod_ref, g1_ref, w_ref, bf_ref, qg_ref, kg_ref, pq_ref, cq_ref, ck_ref, tri_ref,
                   u_ref, qa_ref, ka_ref, v_ref, sgc_ref, sga_ref, cum_ref, carry_ref):
    @pl.when(pl.program_id(1) == 0)
    def _():
        carry_ref[...] = jnp.zeros_like(carry_ref)

    x = x_ref[0]
    ms = jnp.mean(x * x, axis=-1, keepdims=True)
    h = (x * lax.rsqrt(ms + EPS) * g1_ref[...]) * (1.0 + mod_ref[0, 1:2, :]) + mod_ref[0, 0:1, :]
    hb = h.astype(BF16)

    hp = N_HEADS * HEAD_PAD
    o_q = 2 * CONV_DIM
    o_k, o_v = o_q + hp, o_q + 2 * hp
    o_gc = o_v + ATTN_DIM
    o_ga = o_gc + x.shape[1]
    pc = _dot(hb, w_ref[:, :o_q])
    u_ref[0] = (pc[:, :CONV_DIM] * _sigmoid(pc[:, CONV_DIM:])).astype(BF16)

    pq = _dot(hb, w_ref[:, o_q:o_k])
    z = pltpu.roll(pq[:, :HEAD_PAD], HEAD_PAD - F_LANE0, 1) + bf_ref[...]
    lf = jnp.minimum(z, 0.0) - jnp.log1p(jnp.exp(-jnp.abs(z)))
    lane = lax.broadcasted_iota(jnp.int32, (1, LANES), 1)
    psum = _dot(tri_ref[...], _lane_pieces(jnp.where(lane < N_HEADS, lf, 0.0)))
    cum = (psum + pltpu.roll(psum, LANES - N_HEADS, 1) + pltpu.roll(psum, LANES - 2 * N_HEADS, 1)
           + carry_ref[...])
    cum = jnp.where(lane < N_HEADS, cum, 0.0)
    carry_ref[...] = cum[ROW_TILE - 1:ROW_TILE, :]
    cum_ref[0] = cum

    pieces = _lane_pieces(cum)
    placed = _dot(pieces, pq_ref[...])
    addq = placed + cq_ref[...]
    addk = ck_ref[...] - pltpu.roll(placed, 3, 1)

    pk = _dot(hb, w_ref[:, o_k:o_v])
    pv = _dot(hb, w_ref[:, o_v:o_gc])
    inv_hd = 1.0 / HEAD_DIM
    hlane = lax.broadcasted_iota(jnp.int32, (1, HEAD_PAD), 1)
    vone = (hlane == HEAD_DIM).astype(F32)
    for hd in range(N_HEADS):
        sl = slice(hd * HEAD_PAD, (hd + 1) * HEAD_PAD)
        qb = pq[:, sl]
        if hd == 0:
            qb = jnp.where(hlane < HEAD_DIM, qb, 0.0)
        qn = qb * lax.rsqrt(jnp.sum(qb * qb, axis=-1, keepdims=True) * inv_hd + EPS) * qg_ref[:, sl]
        qa_ref[0, hd] = (qn + addq[:, sl]).astype(BF16)
        kb = pk[:, sl]
        kn = kb * lax.rsqrt(jnp.sum(kb * kb, axis=-1, keepdims=True) * inv_hd + EPS) * kg_ref[:, sl]
        ka_ref[0, hd] = (kn + addk[:, sl]).astype(BF16)
        v_ref[0, hd] = (_head_tile(pv, 0, hd) + vone).astype(BF16)

    sgc_ref[0] = _sigmoid(_dot(hb, w_ref[:, o_gc:o_ga])).astype(BF16)
    sga_ref[0] = _sigmoid(_dot(hb, w_ref[:, o_ga:])).astype(BF16)


def _inproj_weight(wi):
    d, n_in = wi.shape
    n_out = 2 * CONV_DIM + 2 * N_HEADS * HEAD_PAD + ATTN_DIM + 2 * d
    rows = WPREP_ROWS
    return pl.pallas_call(
        _wprep_kernel,
        grid=(d // rows,),
        in_specs=[pl.BlockSpec((rows, n_in), lambda i: (i, 0))],
        out_specs=pl.BlockSpec((rows, n_out), lambda i: (i, 0)),
        out_shape=jax.ShapeDtypeStruct((d, n_out), BF16),
        compiler_params=_params("arbitrary"),
        name="wprep",
    )(wi)


def _wprep_kernel(w_ref, o_ref):
    rows, n_in = w_ref.shape
    o_q = 2 * CONV_DIM
    o_f = o_q + 3 * ATTN_DIM
    o_gc = o_f + N_HEADS
    lane = lax.broadcasted_iota(jnp.int32, (1, LANES), 1)
    per = HEAD_PAD // HEAD_DIM

    o_ref[:, :o_q] = w_ref[:, :o_q].astype(BF16)
    out = o_q
    fcols = pltpu.roll(w_ref[:, o_f:o_f + LANES], F_LANE0, 1)
    for blk in range(2):
        src0 = o_q + blk * ATTN_DIM
        for hd in range(N_HEADS):
            g, part = divmod(hd, per)
            tile = w_ref[:, src0 + g * HEAD_PAD:src0 + (g + 1) * HEAD_PAD]
            if part:
                tile = pltpu.roll(tile, HEAD_PAD - part * HEAD_DIM, 1)
            tile = jnp.where(lane < HEAD_DIM, tile, 0.0)
            if blk == 0 and hd == 0:
                tile = jnp.where((lane >= F_LANE0) & (lane < F_LANE0 + N_HEADS), fcols, tile)
            o_ref[:, out:out + HEAD_PAD] = tile.astype(BF16)
            out += HEAD_PAD
    o_ref[:, out:out + ATTN_DIM] = w_ref[:, o_q + 2 * ATTN_DIM:o_f].astype(BF16)
    out += ATTN_DIM
    shift = LANES - (o_gc % LANES)
    for j in range((n_in - o_gc) // LANES):
        a0 = (o_gc // LANES + j) * LANES
        a = pltpu.roll(w_ref[:, a0:a0 + LANES], shift, 1)
        width = min(LANES, n_in - (a0 + LANES))
        b = w_ref[:, a0 + LANES:a0 + LANES + width]
        if width < LANES:
            b = jnp.concatenate([b, jnp.zeros((rows, LANES - width), F32)], axis=1)
        b = pltpu.roll(b, shift, 1)
        o_ref[:, out:out + LANES] = jnp.where(lane < shift, a, b).astype(BF16)
        out += LANES


def _inproj(x, mod, g1, w, bf, qg, kg, pq, cq, ck, tri):
    b, s, d = x.shape
    tm = ROW_TILE
    row = lambda w: pl.BlockSpec((1, tm, w), lambda bi, i: (bi, i, 0))
    head = pl.BlockSpec((1, N_HEADS, tm, HEAD_PAD), lambda bi, i: (bi, 0, i, 0))
    consts = [g1, w, bf, qg, kg, pq, cq, ck, tri]
    return pl.pallas_call(
        _inproj_kernel,
        grid=(b, s // tm),
        in_specs=[row(d), pl.BlockSpec((1, 6, d), lambda bi, i: (bi, 0, 0))]
                 + [_const_spec(a.shape) for a in consts],
        out_specs=[row(CONV_DIM), head, head, head, row(d), row(d), row(LANES)],
        out_shape=[jax.ShapeDtypeStruct((b, s, CONV_DIM), BF16),
                   jax.ShapeDtypeStruct((b, N_HEADS, s, HEAD_PAD), BF16),
                   jax.ShapeDtypeStruct((b, N_HEADS, s, HEAD_PAD), BF16),
                   jax.ShapeDtypeStruct((b, N_HEADS, s, HEAD_PAD), BF16),
                   jax.ShapeDtypeStruct((b, s, d), BF16),
                   jax.ShapeDtypeStruct((b, s, d), BF16),
                   jax.ShapeDtypeStruct((b, s, LANES), F32)],
        scratch_shapes=[pltpu.VMEM((1, LANES), F32)],
        compiler_params=_params("arbitrary", "arbitrary"),
        name="inproj",
    )(x, mod, *consts)


def _attn_kernel(cs_ref, ce_ref, prm_ref, q_ref, k_ref, v_ref, o_ref, acc_ref, m_ref):
    t = ATTN_TILE
    nb = q_ref.shape[2] // t
    base = (pl.program_id(0) * pl.num_programs(1) + pl.program_id(1)) * nb
    bound = prm_ref[0]
    thresh = -(EXP_UNDERFLOW + 2.0 * bound)
    causal = lax.broadcasted_iota(jnp.int32, (t, t), 0) >= lax.broadcasted_iota(jnp.int32, (t, t), 1)

    dp = DIAG_PARTS

    def first_block(i):
        c0 = cs_ref[dp * (base + i)]
        return lax.fori_loop(
            0, i, lambda j, n: n + (c0 - ce_ref[dp * (base + j) + dp - 1] < thresh).astype(jnp.int32), 0)

    def scores(q, j, masked):
        k0 = pl.multiple_of(j * t, t)
        s = _dot_nt(q, k_ref[0, 0, pl.ds(k0, t), :])
        if masked:
            s = jnp.where(causal, s, NEG_BIG)
        return s, v_ref[0, 0, pl.ds(k0, t), :]

    def finish(q0):
        acc = acc_ref[...]
        o_ref[0, pl.ds(q0, t), :] = (acc / acc[:, HEAD_DIM:HEAD_DIM + 1]).astype(BF16)

    def fixed_shift(i, carry):
        q0 = pl.multiple_of(i * t, t)
        q = q_ref[0, 0, pl.ds(q0, t), :]

        acc_ref[...] = jnp.zeros_like(acc_ref)

        def weighted(j, masked):
            s, vb = scores(q, j, masked)
            return _dot(jnp.exp(s).astype(BF16), vb)

        def kv(j, c):
            acc_ref[...] += weighted(j, False)
            return c

        lax.fori_loop(first_block(i), i - 1, kv, 0)
        rp = t // DIAG_PARTS

        @pl.when(i == 0)
        def _():
            for r in range(DIAG_PARTS):
                nk = (r + 1) * rp
                s = _dot_nt(q[r * rp:(r + 1) * rp, :], k_ref[0, 0, pl.ds(q0, nk), :])
                seen = (lax.broadcasted_iota(jnp.int32, (rp, nk), 0) + r * rp
                        >= lax.broadcasted_iota(jnp.int32, (rp, nk), 1))
                p = jnp.exp(jnp.where(seen, s, NEG_BIG)).astype(BF16)
                acc_ref[r * rp:(r + 1) * rp, :] += _dot(p, v_ref[0, 0, pl.ds(q0, nk), :])
            finish(q0)

        @pl.when(i > 0)
        def _():
            nk = t + rp
            for r in range(DIAG_PARTS):
                k0 = pl.multiple_of(q0 - t + r * rp, rp)
                s = _dot_nt(q[r * rp:(r + 1) * rp, :], k_ref[0, 0, pl.ds(k0, nk), :])
                seen = (lax.broadcasted_iota(jnp.int32, (rp, nk), 0) + t
                        >= lax.broadcasted_iota(jnp.int32, (rp, nk), 1))
                p = jnp.exp(jnp.where(seen, s, NEG_BIG)).astype(BF16)
                acc_ref[r * rp:(r + 1) * rp, :] += _dot(p, v_ref[0, 0, pl.ds(k0, nk), :])

            for r in range(1, DIAG_PARTS):
                need = cs_ref[dp * (base + i) + r] - ce_ref[dp * (base + i - 1) + r - 1] >= thresh

                @pl.when(need)
                def _():
                    kp = pl.multiple_of(q0 - t, t)
                    s = _dot_nt(q[r * rp:(r + 1) * rp, :], k_ref[0, 0, pl.ds(kp, r * rp), :])
                    acc_ref[r * rp:(r + 1) * rp, :] += _dot(jnp.exp(s).astype(BF16),
                                                            v_ref[0, 0, pl.ds(kp, r * rp), :])

            finish(q0)

        return carry

    def running_max(i, carry):
        q0 = pl.multiple_of(i * t, t)
        q = q_ref[0, 0, pl.ds(q0, t), :]
        m_ref[...] = jnp.full_like(m_ref, -jnp.inf)
        acc_ref[...] = jnp.zeros_like(acc_ref)

        def step(j, masked):
            s, vb = scores(q, j, masked)
            m_prev = m_ref[...]
            m_new = jnp.maximum(m_prev, jnp.max(s, axis=-1, keepdims=True))
            p = jnp.exp(s - m_new)
            acc_ref[...] = jnp.exp(m_prev - m_new) * acc_ref[...] + _dot(p.astype(BF16), vb)
            m_ref[...] = m_new

        def kv(j, c):
            step(j, False)
            return c

        lax.fori_loop(first_block(i), i, kv, 0)
        step(i, True)
        finish(q0)
        return carry

    @pl.when(bound <= FIXED_SHIFT_BOUND)
    def _():
        lax.fori_loop(0, nb, fixed_shift, 0)

    @pl.when(bound > FIXED_SHIFT_BOUND)
    def _():
        lax.fori_loop(0, nb, running_max, 0)


def _attention(cs, ce, prm, qa, ka, v):
    b, nh, s, hp = qa.shape
    t = ATTN_TILE
    seq = pl.BlockSpec((1, 1, s, hp), lambda bi, hi, *_: (bi, hi, 0, 0))
    return pl.pallas_call(
        _attn_kernel,
        grid_spec=pltpu.PrefetchScalarGridSpec(
            num_scalar_prefetch=3,
            grid=(b, nh),
            in_specs=[seq, seq, seq],
            out_specs=pl.BlockSpec((1, s, hp), lambda bi, hi, *_: (bi, 0, hi)),
            scratch_shapes=[pltpu.VMEM((t, hp), F32), pltpu.VMEM((t, 1), F32)]),
        out_shape=jax.ShapeDtypeStruct((b, s, nh * hp), BF16),
        compiler_params=_params("arbitrary", "arbitrary"),
        name="attn",
    )(cs, ce, prm, qa, ka, v)


def _merge_kernel(u_ref, halo_ref, o_ref, sgc_ref, sga_ref, x_ref, mod_ref, wdw_ref, bdw_ref,
                  gng_ref, gnb_ref, gg_ref, wco_ref, wao_ref, wout_ref, x1_ref, buf_ref):
    tm = ROW_TILE
    halo = halo_ref[0].astype(F32)
    halo = jnp.where(pl.program_id(1) == 0, jnp.zeros_like(halo), halo)
    ucur = u_ref[0].astype(F32)
    for cb in range(CONV_DIM // LANES):
        buf_ref[cb, 0:CONV_HALO, :] = halo[:, cb * LANES:(cb + 1) * LANES]
        buf_ref[cb, CONV_HALO:, :] = ucur[:, cb * LANES:(cb + 1) * LANES]

    base = CONV_HALO - (CONV_WIDTH - 1)
    ys = []
    for cb in range(CONV_DIM // LANES):
        acc = jnp.zeros((tm, LANES), F32)
        for j in range(CONV_WIDTH):
            acc = acc + wdw_ref[j:j + 1, cb * LANES:(cb + 1) * LANES] * buf_ref[cb, base + j:base + j + tm, :]
        ys.append(acc)
    y = jnp.concatenate(ys, axis=1) + bdw_ref[...]

    gg = gg_ref[...]
    y_hi, y_lo = _split2(y)
    dlt = y - (_dot(y_hi, gg) + _dot(y_lo, gg))
    s_hi, s_lo = _split2(dlt * dlt)
    var = _dot(s_hi, gg) + _dot(s_lo, gg)
    yn = dlt * lax.rsqrt(var + EPS) * gng_ref[...] + gnb_ref[...]
    y_conv = _dot(_silu(yn).astype(BF16), wco_ref[...])

    per = HEAD_PAD // HEAD_DIM
    lane = lax.broadcasted_iota(jnp.int32, (1, HEAD_PAD), 1)
    packed = []
    for g in range(N_HEADS // per):
        tile = o_ref[0, :, g * per * HEAD_PAD:(g * per + 1) * HEAD_PAD].astype(F32)
        for part in range(1, per):
            nxt = o_ref[0, :, (g * per + part) * HEAD_PAD:(g * per + part + 1) * HEAD_PAD].astype(F32)
            tile = jnp.where(lane < part * HEAD_DIM, tile, pltpu.roll(nxt, part * HEAD_DIM, 1))
        packed.append(tile)
    y_attn = _dot(jnp.concatenate(packed, axis=1).astype(BF16), wao_ref[...])
    merged = sgc_ref[0].astype(F32) * y_conv + sga_ref[0].astype(F32) * y_attn
    mix = _dot(merged.astype(BF16), wout_ref[...])
    x1_ref[0] = x_ref[0] + mod_ref[0, 2:3, :] * mix


def _merge(u, o, sgc, sga, x, mod, wdw, bdw, gng, gnb, gg, wco, wao, wout):
    b, s, d = x.shape
    tm = ROW_TILE
    per = tm // CONV_HALO
    row = lambda w: pl.BlockSpec((1, tm, w), lambda bi, i: (bi, i, 0))
    consts = [wdw, bdw, gng, gnb, gg, wco, wao, wout]
    return pl.pallas_call(
        _merge_kernel,
        grid=(b, s // tm),
        in_specs=[row(CONV_DIM),
                  pl.BlockSpec((1, CONV_HALO, CONV_DIM),
                               lambda bi, i: (bi, jnp.maximum(i * per - 1, 0), 0)),
                  row(o.shape[2]), row(d), row(d), row(d),
                  pl.BlockSpec((1, 6, d), lambda bi, i: (bi, 0, 0))]
                 + [_const_spec(a.shape) for a in consts],
        out_specs=row(d),
        out_shape=jax.ShapeDtypeStruct((b, s, d), F32),
        scratch_shapes=[pltpu.VMEM((CONV_DIM // LANES, CONV_HALO + tm, LANES), F32)],
        compiler_params=_params("arbitrary", "arbitrary"),
        name="merge",
    )(u, u, o, sgc, sga, x, mod, *consts)


def _router_kernel(x1_ref, mod_ref, g2_ref, wr_hi_ref, wr_lo_ref, rb_ref, h2_ref, h2w_ref, comb_ref, cnt_ref):
    x = x1_ref[0]
    ms = jnp.mean(x * x, axis=-1, keepdims=True)
    h = (x * lax.rsqrt(ms + EPS) * g2_ref[...]) * (1.0 + mod_ref[0, 4:5, :]) + mod_ref[0, 3:4, :]
    h2_ref[0] = h.astype(BF16)
    h2w_ref[0] = _pack_halves(h)

    h_hi, h_lo = _split2(h)
    logits = _dot_nt(wr_hi_ref[...], h_hi) + _dot_nt(wr_hi_ref[...], h_lo) + _dot_nt(wr_lo_ref[...], h_hi)
    scores = _sigmoid(logits)
    biased = scores + rb_ref[...]

    per = N_EXPERTS // N_GROUPS
    rows = lax.broadcasted_iota(jnp.int32, (per, biased.shape[1]), 0)
    gscore = []
    for g in range(N_GROUPS):
        blk = biased[g * per:(g + 1) * per, :]
        top1 = jnp.max(blk, axis=0, keepdims=True)
        first = jnp.min(jnp.where(blk == top1, rows, per), axis=0, keepdims=True)
        top2 = jnp.max(jnp.where(rows == first, -jnp.inf, blk), axis=0, keepdims=True)
        gscore.append(top1 + top2)

    cand = []
    for g in range(N_GROUPS):
        rank = jnp.zeros_like(gscore[g], dtype=jnp.int32)
        for g2 in range(N_GROUPS):
            if g2 == g:
                continue
            ahead = gscore[g2] > gscore[g]
            if g2 < g:
                ahead = ahead | (gscore[g2] == gscore[g])
            rank = rank + ahead.astype(jnp.int32)
        keep = rank < TOPK_GROUPS
        cand.append(jnp.where(keep, biased[g * per:(g + 1) * per, :], -jnp.inf))
    cand = jnp.concatenate(cand, axis=0)

    eidx = lax.broadcasted_iota(jnp.int32, cand.shape, 0)
    work = cand
    for _ in range(TOP_K):
        top = jnp.max(work, axis=0, keepdims=True)
        first = jnp.min(jnp.where(work == top, eidx, N_EXPERTS), axis=0, keepdims=True)
        work = jnp.where(eidx == first, -jnp.inf, work)
    sel = (work != cand) & (cand > -jnp.inf)
    w = jnp.where(sel, scores, 0.0)
    comb = w / jnp.sum(w, axis=0, keepdims=True) * ROUTED_SCALE
    comb_ref[...] = comb

    @pl.when((pl.program_id(0) == 0) & (pl.program_id(1) == 0))
    def _():
        cnt_ref[...] = jnp.zeros_like(cnt_ref)

    hit = jnp.where(comb != 0.0, 1.0, 0.0).astype(BF16)
    cnt_ref[...] += _dot(hit, jnp.ones((hit.shape[1], LANES), BF16))


def _router(x1, mod, g2, wr_hi, wr_lo, rb):
    b, s, d = x1.shape
    tm = ROW_TILE
    nt = s // tm
    return pl.pallas_call(
        _router_kernel,
        grid=(b, nt),
        in_specs=[pl.BlockSpec((1, tm, d), lambda bi, i: (bi, i, 0)),
                  pl.BlockSpec((1, 6, d), lambda bi, i: (bi, 0, 0)),
                  _const_spec(g2.shape), _const_spec(wr_hi.shape), _const_spec(wr_lo.shape),
                  _const_spec(rb.shape)],
        out_specs=[pl.BlockSpec((1, tm, d), lambda bi, i: (bi, i, 0)),
                   pl.BlockSpec((1, tm, d // 2), lambda bi, i: (bi, i, 0)),
                   pl.BlockSpec((N_EXPERTS, tm), lambda bi, i: (0, bi * nt + i)),
                   pl.BlockSpec((N_EXPERTS, LANES), lambda bi, i: (0, 0))],
        out_shape=[jax.ShapeDtypeStruct((b, s, d), BF16),
                   jax.ShapeDtypeStruct((b, s, d // 2), jnp.int32),
                   jax.ShapeDtypeStruct((N_EXPERTS, b * s), F32),
                   jax.ShapeDtypeStruct((N_EXPERTS, LANES), F32)],
        compiler_params=_params("arbitrary", "arbitrary"),
        name="router",
    )(x1, mod, g2, wr_hi, wr_lo, rb)


def _pos_kernel(comb_ref, cnt_ref, tri_ref, lstrict_ref, posk_ref, wk_ref, gend_ref, base_ref, *, spare_row):
    tp = comb_ref.shape[1]
    comb = comb_ref[...]
    sel = comb != 0.0
    selb = jnp.where(sel, 1.0, 0.0).astype(BF16)

    @pl.when(pl.program_id(0) == 0)
    def _():
        seg = jnp.floor((cnt_ref[...] + (EXPERT_TILE - 1.0)) * (1.0 / EXPERT_TILE)) * EXPERT_TILE
        s_hi, s_mid, s_lo = _split3(seg)
        ls = lstrict_ref[...]
        start = _dot(ls, s_hi) + _dot(ls, s_mid) + _dot(ls, s_lo)
        base_ref[...] = start
        gend_ref[...] = start + seg

    rank = _dot(selb, tri_ref[...])
    pos = base_ref[:, 0:1] + rank - 1.0
    base_ref[...] += _dot(selb, jnp.ones((tp, LANES), BF16))
    slot = _dot(lstrict_ref[...], selb)
    rows_p, rows_w = [], []
    for k in range(TOP_K):
        m = sel & (slot == k)
        rows_p.append(jnp.sum(jnp.where(m, pos - spare_row, 0.0), axis=0, keepdims=True) + spare_row)
        rows_w.append(jnp.sum(jnp.where(m, comb, 0.0), axis=0, keepdims=True))
    posk_ref[...] = jnp.concatenate(rows_p, axis=0).astype(jnp.int32)
    wk_ref[...] = jnp.concatenate(rows_w, axis=0)


def _positions(comb_t, cnt, tri, lstrict, n_rows):
    ne, t = comb_t.shape
    tp = POS_TILE
    tok = lambda rows: pl.BlockSpec((rows, tp), lambda i: (0, i))
    return pl.pallas_call(
        functools.partial(_pos_kernel, spare_row=float(n_rows - 1)),
        grid=(t // tp,),
        in_specs=[tok(ne), _const_spec(cnt.shape), _const_spec(tri.shape), _const_spec(lstrict.shape)],
        out_specs=[tok(TOP_K), tok(TOP_K), pl.BlockSpec((ne, LANES), lambda i: (0, 0))],
        out_shape=[jax.ShapeDtypeStruct((TOP_K, t), jnp.int32),
                   jax.ShapeDtypeStruct((TOP_K, t), F32),
                   jax.ShapeDtypeStruct((ne, LANES), F32)],
        scratch_shapes=[pltpu.VMEM((ne, LANES), F32)],
        compiler_params=_params("arbitrary"),
        name="positions",
    )(comb_t, cnt, tri, lstrict)


def _sc_workers():
    info = plsc.get_sparse_core_info()
    return info.num_cores, info.num_cores * info.num_subcores


def _sc_scatter_rows(rows, pos, n_out):
    nc, nw = _sc_workers()
    n, w = rows.shape
    nk = pos.shape[0]
    ch = SC_CHUNK
    per_w = n // nw
    assert per_w * nw == n and per_w % ch == 0

    @functools.partial(
        pl.kernel, mesh=plsc.VectorSubcoreMesh(core_axis_name="c", subcore_axis_name="s"),
        out_type=jax.ShapeDtypeStruct((n_out, w), rows.dtype),
        scratch_types=[pltpu.VMEM((nk, ch), jnp.int32), pltpu.VMEM((ch, w), rows.dtype),
                       pltpu.SemaphoreType.DMA])
    def scatter(rows_hbm, pos_hbm, out_hbm, idx_v, rows_v, sem):
        base = (lax.axis_index("s") * nc + lax.axis_index("c")) * per_w

        @pl.loop(0, per_w // ch)
        def _(ci):
            off = pl.multiple_of(base + ci * ch, ch)
            pltpu.sync_copy(pos_hbm.at[:, pl.ds(off, ch)], idx_v)
            pltpu.sync_copy(rows_hbm.at[pl.ds(off, ch)], rows_v)
            copies = [pltpu.make_async_copy(rows_v, out_hbm.at[idx_v.at[k]], sem) for k in range(nk)]
            for cp in copies:
                cp.start()
            for cp in copies:
                cp.wait()

    return scatter(rows, pos)


def _sc_gather_rows(table, idx):
    nc, nw = _sc_workers()
    n = idx.shape[0]
    w = table.shape[1]
    ch = SC_CHUNK
    per_w = n // nw
    assert per_w * nw == n and per_w % ch == 0
    nch = per_w // ch

    @functools.partial(
        pl.kernel, mesh=plsc.VectorSubcoreMesh(core_axis_name="c", subcore_axis_name="s"),
        out_type=jax.ShapeDtypeStruct((n, w), table.dtype),
        scratch_types=[pltpu.VMEM((nch, ch), jnp.int32), pltpu.VMEM((ch, w), table.dtype),
                       pltpu.SemaphoreType.DMA])
    def gather(table_hbm, idx_hbm, out_hbm, idx_v, rows_v, sem):
        wid = lax.axis_index("s") * nc + lax.axis_index("c")
        base = wid * per_w
        pltpu.sync_copy(idx_hbm.at[pl.ds(wid * nch, nch)], idx_v)

        @pl.loop(0, nch)
        def _(ci):
            off = pl.multiple_of(base + ci * ch, ch)
            cp = pltpu.make_async_copy(table_hbm.at[idx_v.at[ci]], rows_v, sem)
            cp.start()
            cp.wait()
            pltpu.sync_copy(rows_v, out_hbm.at[pl.ds(off, ch)])

    return gather(table, idx.reshape(n // ch, ch))


def _expert_kernel(te_ref, nu_ref, xs_ref, wgu_ref, wd_ref, y_ref, wgu_b, wd_b):
    i = pl.program_id(0)

    @pl.when(i < nu_ref[0])
    def _():
        @pl.when((i == 0) | (te_ref[i] != te_ref[jnp.maximum(i - 1, 0)]))
        def _():
            wgu_b[...] = wgu_ref[0].astype(BF16)
            wd_b[...] = wd_ref[0].astype(BF16)

        lo, hi = _unpack_halves(xs_ref[...])
        half = wgu_b.shape[0] // 2
        gu = _dot(lo.astype(BF16), wgu_b[:half, :]) + _dot(hi.astype(BF16), wgu_b[half:, :])
        act = _silu(gu[:, :EXPERT_DIM]) * gu[:, EXPERT_DIM:]
        y_ref[...] = _pack_halves(_dot(act.astype(BF16), wd_b[...]))


def _experts(tile_e, n_used, xs, wgu, wd):
    r, w = xs.shape
    tm = EXPERT_TILE
    d = wgu.shape[1]
    rows = pl.BlockSpec((tm, w), lambda i, te, nu: (jnp.minimum(i, nu[0] - 1), 0))
    return pl.pallas_call(
        _expert_kernel,
        grid_spec=pltpu.PrefetchScalarGridSpec(
            num_scalar_prefetch=2,
            grid=(r // tm,),
            in_specs=[rows,
                      pl.BlockSpec((1, d, 2 * EXPERT_DIM), lambda i, te, nu: (te[i], 0, 0)),
                      pl.BlockSpec((1, EXPERT_DIM, d), lambda i, te, nu: (te[i], 0, 0))],
            out_specs=rows,
            scratch_shapes=[pltpu.VMEM((d, 2 * EXPERT_DIM), BF16), pltpu.VMEM((EXPERT_DIM, d), BF16)]),
        out_shape=jax.ShapeDtypeStruct((r, w), jnp.int32),
        compiler_params=_params("arbitrary"),
        name="experts",
    )(tile_e, n_used, xs, wgu, wd)


def _final_kernel(yk_ref, wk_ref, h_ref, x1_ref, mod_ref, wsgu_ref, wsd_ref, out_ref):
    half = yk_ref.shape[2]
    acc_lo = jnp.zeros((yk_ref.shape[1], half), F32)
    acc_hi = jnp.zeros((yk_ref.shape[1], half), F32)
    for k in range(TOP_K):
        lo, hi = _unpack_halves(yk_ref[k])
        w = wk_ref[:, k:k + 1]
        acc_lo = acc_lo + jnp.where(w != 0.0, w * lo, 0.0)
        acc_hi = acc_hi + jnp.where(w != 0.0, w * hi, 0.0)
    routed = jnp.concatenate([acc_lo, acc_hi], axis=1)
    sgu = _dot(h_ref[...], wsgu_ref[...])
    act = _silu(sgu[:, :SHARED_DIM]) * sgu[:, SHARED_DIM:]
    shared = _dot(act.astype(BF16), wsd_ref[...])
    out_ref[...] = x1_ref[...] + mod_ref[0, 5:6, :] * (routed + shared)


def _final(yk, wk_t, h2, x1, mod, wsgu, wsd, tiles_per_batch):
    t, d = h2.shape
    tm = ROW_TILE
    row = lambda w: pl.BlockSpec((tm, w), lambda i: (i, 0))
    return pl.pallas_call(
        _final_kernel,
        grid=(t // tm,),
        in_specs=[pl.BlockSpec((TOP_K, tm, d // 2), lambda i: (0, i, 0)),
                  row(TOP_K), row(d), row(d),
                  pl.BlockSpec((1, 6, d), lambda i: (i // tiles_per_batch, 0, 0)),
                  _const_spec(wsgu.shape), _const_spec(wsd.shape)],
        out_specs=row(d),
        out_shape=jax.ShapeDtypeStruct((t, d), F32),
        compiler_params=_params("arbitrary"),
        name="final",
    )(yk, wk_t, h2, x1, mod, wsgu, wsd)


def _placement():
    pq = np.zeros((LANES, N_HEADS * HEAD_PAD), np.float32)
    cq = np.zeros((1, N_HEADS * HEAD_PAD), np.float32)
    ck = np.zeros((1, N_HEADS * HEAD_PAD), np.float32)
    for hd in range(N_HEADS):
        for k in range(3):
            pq[k * N_HEADS + hd, hd * HEAD_PAD + AUG0 + k] = 1.0
            ck[0, hd * HEAD_PAD + AUG0 + k] = 1.0
            cq[0, hd * HEAD_PAD + AUG0 + 3 + k] = 1.0
    return jnp.asarray(pq, BF16), jnp.asarray(cq), jnp.asarray(ck)


def kernel(x, c, w_ada, b_ada, norm1_g, w_in, w_dw, b_dw, conv_gn_g, conv_gn_b, w_conv_out,
           q_norm_g, k_norm_g, b_forget, w_attn_out, w_out, norm2_g, w_router, router_bias,
           w_experts_gate_up, w_experts_down, w_shared_gate_up, w_shared_down):
    depth = w_ada.shape[0]
    b, s, d = x.shape

    pq, cq, ck = _placement()
    tri = jnp.asarray(np.tril(np.ones((ROW_TILE, ROW_TILE), np.float32)), BF16)
    grp = np.arange(CONV_DIM) // (CONV_DIM // CONV_GROUPS)
    gg = jnp.asarray((grp[:, None] == grp[None, :]).astype(np.float32) / (CONV_DIM // CONV_GROUPS), BF16)
    c_pad = jnp.pad(c, ((0, SUBLANES - b), (0, 0)))
    tri_pos = jnp.asarray(np.triu(np.ones((POS_TILE, POS_TILE), np.float32)), BF16)
    lstrict = jnp.asarray(np.tril(np.ones((N_EXPERTS, N_EXPERTS), np.float32), -1), BF16)

    for l in range(depth):
        mod = _ada(c_pad, w_ada[l], b_ada[l][None, :])[:b].reshape(b, 6, d)

        bf = jnp.pad(b_forget[l][None, :], ((0, 0), (0, LANES - N_HEADS)))
        gpad = lambda g, sc: jnp.tile(jnp.pad(g * sc, (0, HEAD_PAD - HEAD_DIM)), N_HEADS)[None, :]
        qscale = HEAD_DIM ** -0.5
        u, qa, ka, v, sgc, sga, cum = _inproj(
            x, mod, norm1_g[l][None, :], _inproj_weight(w_in[l]),
            bf, gpad(q_norm_g[l], qscale), gpad(k_norm_g[l], 1.0),
            pq, cq, ck, tri)

        flat = lambda a: a[:, :, :N_HEADS].transpose(0, 2, 1).reshape(-1)
        part = ATTN_TILE // DIAG_PARTS
        cs = flat(cum[:, 0::part])
        ce = flat(cum[:, part - 1::part])
        bound = (1.02 * HEAD_DIM * qscale) * jnp.max(jnp.abs(q_norm_g[l])) * jnp.max(jnp.abs(k_norm_g[l]))
        o = _attention(cs, ce, bound.reshape(1), qa, ka, v)

        wdw = jnp.pad(w_dw[l], ((0, CONV_HALO - CONV_WIDTH), (0, 0)))
        x1 = _merge(u, o, sgc, sga, x, mod, wdw, b_dw[l][None, :], conv_gn_g[l][None, :],
                    conv_gn_b[l][None, :], gg, w_conv_out[l].astype(BF16), w_attn_out[l].astype(BF16),
                    w_out[l].astype(BF16))

        wr = w_router[l].T
        wr_hi = wr.astype(BF16)
        wr_lo = (wr - wr_hi.astype(F32)).astype(BF16)
        h2, h2w, comb_t, cnt = _router(x1, mod, norm2_g[l][None, :], wr_hi, wr_lo, router_bias[l][:, None])

        t = b * s
        n_tiles = (t * TOP_K) // EXPERT_TILE + N_EXPERTS
        posk, wk, gend = _positions(comb_t, cnt, tri_pos, lstrict, n_tiles * EXPERT_TILE)
        seg_end = gend[:, 0].astype(jnp.int32)
        n_used = seg_end[-1:] // EXPERT_TILE
        tile_start = jnp.arange(n_tiles, dtype=jnp.int32) * EXPERT_TILE
        tile_start = jnp.minimum(tile_start, seg_end[-1] - EXPERT_TILE)
        tile_e = jnp.sum((seg_end[None, :] <= tile_start[:, None]).astype(jnp.int32), axis=1)

        xs = _sc_scatter_rows(h2w.reshape(t, d // 2), posk, n_tiles * EXPERT_TILE)
        ys = _experts(tile_e, n_used, xs, w_experts_gate_up[l], w_experts_down[l])
        yk = _sc_gather_rows(ys, posk.reshape(-1)).reshape(TOP_K, t, d // 2)
        out = _final(yk, wk.T, h2.reshape(t, d), x1.reshape(t, d), mod,
                     w_shared_gate_up[l].astype(BF16), w_shared_down[l].astype(BF16), s // ROW_TILE)
        x = out.reshape(b, s, d)
    return x
```

```python
import functools

import numpy as np
import jax
import jax.numpy as jnp
from jax import lax
from jax.experimental import pallas as pl
from jax.experimental.pallas import tpu as pltpu
from jax.experimental.pallas import tpu_sc as plsc

F32 = jnp.float32
BF16 = jnp.bfloat16

CONV_DIM = 512
CONV_WIDTH = 31
CONV_GROUPS = 8
N_HEADS = 8
HEAD_DIM = 64
ATTN_DIM = N_HEADS * HEAD_DIM
N_EXPERTS = 64
TOP_K = 8
N_GROUPS = 8
TOPK_GROUPS = 4
EXPERT_DIM = 256
SHARED_DIM = 256
ROUTED_SCALE = 2.5
EPS = 1e-6

LANES = 128
SUBLANES = 8
HEAD_PAD = LANES
AUG0 = HEAD_DIM
F_LANE0 = HEAD_DIM + 8
VMEM_LIMIT = 56 * 1024 * 1024

ROW_TILE = 512
ATTN_TILE = 1024
DIAG_PARTS = 4
EXPERT_TILE = 1024
WPREP_ROWS = 256
POS_TILE = 1024
SC_CHUNK = 128
CONV_HALO = 32

NEG_BIG = -1e30
EXP_UNDERFLOW = 104.0
FIXED_SHIFT_BOUND = 40.0


def _dot(a, b):
    return jnp.dot(a, b, preferred_element_type=F32)


def _dot_nt(a, b):
    return lax.dot_general(a, b, (((1,), (1,)), ((), ())), preferred_element_type=F32)


def _split2(x):
    hi = x.astype(BF16)
    lo = (x - hi.astype(F32)).astype(BF16)
    return hi, lo


def _split3(x):
    hi = x.astype(BF16)
    r = x - hi.astype(F32)
    mid = r.astype(BF16)
    lo = (r - mid.astype(F32)).astype(BF16)
    return hi, mid, lo


def _pack_halves(v):
    n = v.shape[1] // 2
    lo = lax.bitcast_convert_type(v[:, :n].astype(BF16).astype(F32), jnp.uint32)
    hi = lax.bitcast_convert_type(v[:, n:].astype(BF16).astype(F32), jnp.uint32)
    return lax.bitcast_convert_type(hi | lax.shift_right_logical(lo, jnp.uint32(16)), jnp.int32)


def _unpack_halves(w):
    u = lax.bitcast_convert_type(w, jnp.uint32)
    lo = lax.bitcast_convert_type(lax.shift_left(u, jnp.uint32(16)), F32)
    hi = lax.bitcast_convert_type(u & jnp.uint32(0xFFFF0000), F32)
    return lo, hi


def _sigmoid(x):
    return 1.0 / (1.0 + jnp.exp(-x))


def _silu(x):
    return x * _sigmoid(x)


def _params(*sem):
    return pltpu.CompilerParams(dimension_semantics=sem, vmem_limit_bytes=VMEM_LIMIT)


def _const_spec(shape):
    n = len(shape)
    return pl.BlockSpec(shape, lambda *_: (0,) * n, pipeline_mode=pl.Buffered(1))


def _ada_kernel(c_ref, w_ref, b_ref, o_ref):
    c = c_ref[...]
    a_hi, a_lo = _split2(_silu(c))
    w_hi, w_lo = _split2(w_ref[...])
    o_ref[...] = _dot(a_hi, w_hi) + _dot(a_hi, w_lo) + _dot(a_lo, w_hi) + b_ref[...]


def _ada(c_pad, w_ada, b_ada):
    d = c_pad.shape[1]
    n = w_ada.shape[1]
    return pl.pallas_call(
        _ada_kernel,
        grid=(n // d,),
        in_specs=[_const_spec(c_pad.shape),
                  pl.BlockSpec((d, d), lambda j: (0, j)),
                  pl.BlockSpec((1, d), lambda j: (0, j))],
        out_specs=pl.BlockSpec((c_pad.shape[0], d), lambda j: (0, j)),
        out_shape=jax.ShapeDtypeStruct((c_pad.shape[0], n), F32),
        compiler_params=_params("arbitrary"),
        name="ada",
    )(c_pad, w_ada, b_ada)


def _lane_pieces(x):
    hi, mid, lo = _split3(x)
    return (hi.astype(F32) + pltpu.roll(mid.astype(F32), N_HEADS, 1)
            + pltpu.roll(lo.astype(F32), 2 * N_HEADS, 1)).astype(BF16)


def _head_tile(p, col0, hd):
    per = HEAD_PAD // HEAD_DIM
    g, part = divmod(hd, per)
    blk = p[:, col0 + g * HEAD_PAD:col0 + (g + 1) * HEAD_PAD]
    if part:
        blk = pltpu.roll(blk, HEAD_PAD - part * HEAD_DIM, 1)
    lane = lax.broadcasted_iota(jnp.int32, (1, HEAD_PAD), 1)
    return jnp.where(lane < HEAD_DIM, blk, 0.0)


def _inproj_kernel(x_ref, mod_ref, g1_ref, w_ref, bf_ref, qg_ref, kg_ref, pq_ref, cq_ref, ck_ref, tri_ref,
                   u_ref, qa_ref, ka_ref, v_ref, sgc_ref, sga_ref, cum_ref, carry_ref):
    @pl.when(pl.program_id(1) == 0)
    def _():
        carry_ref[...] = jnp.zeros_like(carry_ref)

    x = x_ref[0]
    ms = jnp.mean(x * x, axis=-1, keepdims=True)
    h = (x * lax.rsqrt(ms + EPS) * g1_ref[...]) * (1.0 + mod_ref[0, 1:2, :]) + mod_ref[0, 0:1, :]
    hb = h.astype(BF16)

    hp = N_HEADS * HEAD_PAD
    o_q = 2 * CONV_DIM
    o_k, o_v = o_q + hp, o_q + 2 * hp
    o_gc = o_v + ATTN_DIM
    o_ga = o_gc + x.shape[1]
    pc = _dot(hb, w_ref[:, :o_q])
    u_ref[0] = (pc[:, :CONV_DIM] * _sigmoid(pc[:, CONV_DIM:])).astype(BF16)

    pq = _dot(hb, w_ref[:, o_q:o_k])
    z = pltpu.roll(pq[:, :HEAD_PAD], HEAD_PAD - F_LANE0, 1) + bf_ref[...]
    lf = jnp.minimum(z, 0.0) - jnp.log1p(jnp.exp(-jnp.abs(z)))
    lane = lax.broadcasted_iota(jnp.int32, (1, LANES), 1)
    psum = _dot(tri_ref[...], _lane_pieces(jnp.where(lane < N_HEADS, lf, 0.0)))
    cum = (psum + pltpu.roll(psum, LANES - N_HEADS, 1) + pltpu.roll(psum, LANES - 2 * N_HEADS, 1)
           + carry_ref[...])
    cum = jnp.where(lane < N_HEADS, cum, 0.0)
    carry_ref[...] = cum[ROW_TILE - 1:ROW_TILE, :]
    cum_ref[0] = cum

    pieces = _lane_pieces(cum)
    placed = _dot(pieces, pq_ref[...])
    addq = placed + cq_ref[...]
    addk = ck_ref[...] - pltpu.roll(placed, 3, 1)

    pk = _dot(hb, w_ref[:, o_k:o_v])
    pv = _dot(hb, w_ref[:, o_v:o_gc])
    inv_hd = 1.0 / HEAD_DIM
    hlane = lax.broadcasted_iota(jnp.int32, (1, HEAD_PAD), 1)
    vone = (hlane == HEAD_DIM).astype(F32)
    for hd in range(N_HEADS):
        sl = slice(hd * HEAD_PAD, (hd + 1) * HEAD_PAD)
        qb = pq[:, sl]
        if hd == 0:
            qb = jnp.where(hlane < HEAD_DIM, qb, 0.0)
        qn = qb * lax.rsqrt(jnp.sum(qb * qb, axis=-1, keepdims=True) * inv_hd + EPS) * qg_ref[:, sl]
        qa_ref[0, hd] = (qn + addq[:, sl]).astype(BF16)
        kb = pk[:, sl]
        kn = kb * lax.rsqrt(jnp.sum(kb * kb, axis=-1, keepdims=True) * inv_hd + EPS) * kg_ref[:, sl]
        ka_ref[0, hd] = (kn + addk[:, sl]).astype(BF16)
        v_ref[0, hd] = (_head_tile(pv, 0, hd) + vone).astype(BF16)

    sgc_ref[0] = _sigmoid(_dot(hb, w_ref[:, o_gc:o_ga])).astype(BF16)
    sga_ref[0] = _sigmoid(_dot(hb, w_ref[:, o_ga:])).astype(BF16)


def _inproj_weight(wi):
    d, n_in = wi.shape
    n_out = 2 * CONV_DIM + 2 * N_HEADS * HEAD_PAD + ATTN_DIM + 2 * d
    rows = WPREP_ROWS
    return pl.pallas_call(
        _wprep_kernel,
        grid=(d // rows,),
        in_specs=[pl.BlockSpec((rows, n_in), lambda i: (i, 0))],
        out_specs=pl.BlockSpec((rows, n_out), lambda i: (i, 0)),
        out_shape=jax.ShapeDtypeStruct((d, n_out), BF16),
        compiler_params=_params("arbitrary"),
        name="wprep",
    )(wi)


def _wprep_kernel(w_ref, o_ref):
    rows, n_in = w_ref.shape
    o_q = 2 * CONV_DIM
    o_f = o_q + 3 * ATTN_DIM
    o_gc = o_f + N_HEADS
    lane = lax.broadcasted_iota(jnp.int32, (1, LANES), 1)
    per = HEAD_PAD // HEAD_DIM

    o_ref[:, :o_q] = w_ref[:, :o_q].astype(BF16)
    out = o_q
    fcols = pltpu.roll(w_ref[:, o_f:o_f + LANES], F_LANE0, 1)
    for blk in range(2):
        src0 = o_q + blk * ATTN_DIM
        for hd in range(N_HEADS):
            g, part = divmod(hd, per)
            tile = w_ref[:, src0 + g * HEAD_PAD:src0 + (g + 1) * HEAD_PAD]
            if part:
                tile = pltpu.roll(tile, HEAD_PAD - part * HEAD_DIM, 1)
            tile = jnp.where(lane < HEAD_DIM, tile, 0.0)
            if blk == 0 and hd == 0:
                tile = jnp.where((lane >= F_LANE0) & (lane < F_LANE0 + N_HEADS), fcols, tile)
            o_ref[:, out:out + HEAD_PAD] = tile.astype(BF16)
            out += HEAD_PAD
    o_ref[:, out:out + ATTN_DIM] = w_ref[:, o_q + 2 * ATTN_DIM:o_f].astype(BF16)
    out += ATTN_DIM
    shift = LANES - (o_gc % LANES)
    for j in range((n_in - o_gc) // LANES):
        a0 = (o_gc // LANES + j) * LANES
        a = pltpu.roll(w_ref[:, a0:a0 + LANES], shift, 1)
        width = min(LANES, n_in - (a0 + LANES))
        b = w_ref[:, a0 + LANES:a0 + LANES + width]
        if width < LANES:
            b = jnp.concatenate([b, jnp.zeros((rows, LANES - width), F32)], axis=1)
        b = pltpu.roll(b, shift, 1)
        o_ref[:, out:out + LANES] = jnp.where(lane < shift, a, b).astype(BF16)
        out += LANES


def _inproj(x, mod, g1, w, bf, qg, kg, pq, cq, ck, tri):
    b, s, d = x.shape
    tm = ROW_TILE
    row = lambda w: pl.BlockSpec((1, tm, w), lambda bi, i: (bi, i, 0))
    head = pl.BlockSpec((1, N_HEADS, tm, HEAD_PAD), lambda bi, i: (bi, 0, i, 0))
    consts = [g1, w, bf, qg, kg, pq, cq, ck, tri]
    return pl.pallas_call(
        _inproj_kernel,
        grid=(b, s // tm),
        in_specs=[row(d), pl.BlockSpec((1, 6, d), lambda bi, i: (bi, 0, 0))]
                 + [_const_spec(a.shape) for a in consts],
        out_specs=[row(CONV_DIM), head, head, head, row(d), row(d), row(LANES)],
        out_shape=[jax.ShapeDtypeStruct((b, s, CONV_DIM), BF16),
                   jax.ShapeDtypeStruct((b, N_HEADS, s, HEAD_PAD), BF16),
                   jax.ShapeDtypeStruct((b, N_HEADS, s, HEAD_PAD), BF16),
                   jax.ShapeDtypeStruct((b, N_HEADS, s, HEAD_PAD), BF16),
                   jax.ShapeDtypeStruct((b, s, d), BF16),
                   jax.ShapeDtypeStruct((b, s, d), BF16),
                   jax.ShapeDtypeStruct((b, s, LANES), F32)],
        scratch_shapes=[pltpu.VMEM((1, LANES), F32)],
        compiler_params=_params("arbitrary", "arbitrary"),
        name="inproj",
    )(x, mod, *consts)


def _attn_kernel(cs_ref, ce_ref, prm_ref, q_ref, k_ref, v_ref, o_ref, acc_ref, m_ref):
    t = ATTN_TILE
    nb = q_ref.shape[2] // t
    base = (pl.program_id(0) * pl.num_programs(1) + pl.program_id(1)) * nb
    bound = prm_ref[0]
    thresh = -(EXP_UNDERFLOW + 2.0 * bound)
    causal = lax.broadcasted_iota(jnp.int32, (t, t), 0) >= lax.broadcasted_iota(jnp.int32, (t, t), 1)

    dp = DIAG_PARTS

    def first_block(i):
        c0 = cs_ref[dp * (base + i)]
        return lax.fori_loop(
            0, i, lambda j, n: n + (c0 - ce_ref[dp * (base + j) + dp - 1] < thresh).astype(jnp.int32), 0)

    def scores(q, j, masked):
        k0 = pl.multiple_of(j * t, t)
        s = _dot_nt(q, k_ref[0, 0, pl.ds(k0, t), :])
        if masked:
            s = jnp.where(causal, s, NEG_BIG)
        return s, v_ref[0, 0, pl.ds(k0, t), :]

    def finish(q0):
        acc = acc_ref[...]
        o_ref[0, pl.ds(q0, t), :] = (acc / acc[:, HEAD_DIM:HEAD_DIM + 1]).astype(BF16)

    def fixed_shift(i, carry):
        q0 = pl.multiple_of(i * t, t)
        q = q_ref[0, 0, pl.ds(q0, t), :]

        acc_ref[...] = jnp.zeros_like(acc_ref)

        def weighted(j, masked):
            s, vb = scores(q, j, masked)
            return _dot(jnp.exp(s).astype(BF16), vb)

        def kv(j, c):
            acc_ref[...] += weighted(j, False)
            return c

        lax.fori_loop(first_block(i), i - 1, kv, 0)
        rp = t // DIAG_PARTS

        @pl.when(i == 0)
        def _():
            for r in range(DIAG_PARTS):
                nk = (r + 1) * rp
                s = _dot_nt(q[r * rp:(r + 1) * rp, :], k_ref[0, 0, pl.ds(q0, nk), :])
                seen = (lax.broadcasted_iota(jnp.int32, (rp, nk), 0) + r * rp
                        >= lax.broadcasted_iota(jnp.int32, (rp, nk), 1))
                p = jnp.exp(jnp.where(seen, s, NEG_BIG)).astype(BF16)
                acc_ref[r * rp:(r + 1) * rp, :] += _dot(p, v_ref[0, 0, pl.ds(q0, nk), :])
            finish(q0)

        @pl.when(i > 0)
        def _():
            nk = t + rp
            for r in range(DIAG_PARTS):
                k0 = pl.multiple_of(q0 - t + r * rp, rp)
                s = _dot_nt(q[r * rp:(r + 1) * rp, :], k_ref[0, 0, pl.ds(k0, nk), :])
                seen = (lax.broadcasted_iota(jnp.int32, (rp, nk), 0) + t
                        >= lax.broadcasted_iota(jnp.int32, (rp, nk), 1))
                p = jnp.exp(jnp.where(seen, s, NEG_BIG)).astype(BF16)
                acc_ref[r * rp:(r + 1) * rp, :] += _dot(p, v_ref[0, 0, pl.ds(k0, nk), :])

            for r in range(1, DIAG_PARTS):
                need = cs_ref[dp * (base + i) + r] - ce_ref[dp * (base + i - 1) + r - 1] >= thresh

                @pl.when(need)
                def _():
                    kp = pl.multiple_of(q0 - t, t)
                    s = _dot_nt(q[r * rp:(r + 1) * rp, :], k_ref[0, 0, pl.ds(kp, r * rp), :])
                    acc_ref[r * rp:(r + 1) * rp, :] += _dot(jnp.exp(s).astype(BF16),
                                                            v_ref[0, 0, pl.ds(kp, r * rp), :])

            finish(q0)

        return carry

    def running_max(i, carry):
        q0 = pl.multiple_of(i * t, t)
        q = q_ref[0, 0, pl.ds(q0, t), :]
        m_ref[...] = jnp.full_like(m_ref, -jnp.inf)
        acc_ref[...] = jnp.zeros_like(acc_ref)

        def step(j, masked):
            s, vb = scores(q, j, masked)
            m_prev = m_ref[...]
            m_new = jnp.maximum(m_prev, jnp.max(s, axis=-1, keepdims=True))
            p = jnp.exp(s - m_new)
            acc_ref[...] = jnp.exp(m_prev - m_new) * acc_ref[...] + _dot(p.astype(BF16), vb)
            m_ref[...] = m_new

        def kv(j, c):
            step(j, False)
            return c

        lax.fori_loop(first_block(i), i, kv, 0)
        step(i, True)
        finish(q0)
        return carry

    @pl.when(bound <= FIXED_SHIFT_BOUND)
    def _():
        lax.fori_loop(0, nb, fixed_shift, 0)

    @pl.when(bound > FIXED_SHIFT_BOUND)
    def _():
        lax.fori_loop(0, nb, running_max, 0)


def _attention(cs, ce, prm, qa, ka, v):
    b, nh, s, hp = qa.shape
    t = ATTN_TILE
    seq = pl.BlockSpec((1, 1, s, hp), lambda bi, hi, *_: (bi, hi, 0, 0))
    return pl.pallas_call(
        _attn_kernel,
        grid_spec=pltpu.PrefetchScalarGridSpec(
            num_scalar_prefetch=3,
            grid=(b, nh),
            in_specs=[seq, seq, seq],
            out_specs=pl.BlockSpec((1, s, hp), lambda bi, hi, *_: (bi, 0, hi)),
            scratch_shapes=[pltpu.VMEM((t, hp), F32), pltpu.VMEM((t, 1), F32)]),
        out_shape=jax.ShapeDtypeStruct((b, s, nh * hp), BF16),
        compiler_params=_params("arbitrary", "arbitrary"),
        name="attn",
    )(cs, ce, prm, qa, ka, v)


def _merge_kernel(u_ref, halo_ref, o_ref, sgc_ref, sga_ref, x_ref, mod_ref, wdw_ref, bdw_ref,
                  gng_ref, gnb_ref, gg_ref, wco_ref, wao_ref, wout_ref, x1_ref, buf_ref):
    tm = ROW_TILE
    halo = halo_ref[0].astype(F32)
    halo = jnp.where(pl.program_id(1) == 0, jnp.zeros_like(halo), halo)
    ucur = u_ref[0].astype(F32)
    for cb in range(CONV_DIM // LANES):
        buf_ref[cb, 0:CONV_HALO, :] = halo[:, cb * LANES:(cb + 1) * LANES]
        buf_ref[cb, CONV_HALO:, :] = ucur[:, cb * LANES:(cb + 1) * LANES]

    base = CONV_HALO - (CONV_WIDTH - 1)
    ys = []
    for cb in range(CONV_DIM // LANES):
        acc = jnp.zeros((tm, LANES), F32)
        for j in range(CONV_WIDTH):
            acc = acc + wdw_ref[j:j + 1, cb * LANES:(cb + 1) * LANES] * buf_ref[cb, base + j:base + j + tm, :]
        ys.append(acc)
    y = jnp.concatenate(ys, axis=1) + bdw_ref[...]

    gg = gg_ref[...]
    y_hi, y_lo = _split2(y)
    dlt = y - (_dot(y_hi, gg) + _dot(y_lo, gg))
    s_hi, s_lo = _split2(dlt * dlt)
    var = _dot(s_hi, gg) + _dot(s_lo, gg)
    yn = dlt * lax.rsqrt(var + EPS) * gng_ref[...] + gnb_ref[...]
    y_conv = _dot(_silu(yn).astype(BF16), wco_ref[...])

    per = HEAD_PAD // HEAD_DIM
    lane = lax.broadcasted_iota(jnp.int32, (1, HEAD_PAD), 1)
    packed = []
    for g in range(N_HEADS // per):
        tile = o_ref[0, :, g * per * HEAD_PAD:(g * per + 1) * HEAD_PAD].astype(F32)
        for part in range(1, per):
            nxt = o_ref[0, :, (g * per + part) * HEAD_PAD:(g * per + part + 1) * HEAD_PAD].astype(F32)
            tile = jnp.where(lane < part * HEAD_DIM, tile, pltpu.roll(nxt, part * HEAD_DIM, 1))
        packed.append(tile)
    y_attn = _dot(jnp.concatenate(packed, axis=1).astype(BF16), wao_ref[...])
    merged = sgc_ref[0].astype(F32) * y_conv + sga_ref[0].astype(F32) * y_attn
    mix = _dot(merged.astype(BF16), wout_ref[...])
    x1_ref[0] = x_ref[0] + mod_ref[0, 2:3, :] * mix


def _merge(u, o, sgc, sga, x, mod, wdw, bdw, gng, gnb, gg, wco, wao, wout):
    b, s, d = x.shape
    tm = ROW_TILE
    per = tm // CONV_HALO
    row = lambda w: pl.BlockSpec((1, tm, w), lambda bi, i: (bi, i, 0))
    consts = [wdw, bdw, gng, gnb, gg, wco, wao, wout]
    return pl.pallas_call(
        _merge_kernel,
        grid=(b, s // tm),
        in_specs=[row(CONV_DIM),
                  pl.BlockSpec((1, CONV_HALO, CONV_DIM),
                               lambda bi, i: (bi, jnp.maximum(i * per - 1, 0), 0)),
                  row(o.shape[2]), row(d), row(d), row(d),
                  pl.BlockSpec((1, 6, d), lambda bi, i: (bi, 0, 0))]
                 + [_const_spec(a.shape) for a in consts],
        out_specs=row(d),
        out_shape=jax.ShapeDtypeStruct((b, s, d), F32),
        scratch_shapes=[pltpu.VMEM((CONV_DIM // LANES, CONV_HALO + tm, LANES), F32)],
        compiler_params=_params("arbitrary", "arbitrary"),
        name="merge",
    )(u, u, o, sgc, sga, x, mod, *consts)


def _router_kernel(x1_ref, mod_ref, g2_ref, wr_hi_ref, wr_lo_ref, rb_ref, h2_ref, h2w_ref, comb_ref, cnt_ref):
    x = x1_ref[0]
    ms = jnp.mean(x * x, axis=-1, keepdims=True)
    h = (x * lax.rsqrt(ms + EPS) * g2_ref[...]) * (1.0 + mod_ref[0, 4:5, :]) + mod_ref[0, 3:4, :]
    h2_ref[0] = h.astype(BF16)
    h2w_ref[0] = _pack_halves(h)

    h_hi, h_lo = _split2(h)
    logits = _dot_nt(wr_hi_ref[...], h_hi) + _dot_nt(wr_hi_ref[...], h_lo) + _dot_nt(wr_lo_ref[...], h_hi)
    scores = _sigmoid(logits)
    biased = scores + rb_ref[...]

    per = N_EXPERTS // N_GROUPS
    rows = lax.broadcasted_iota(jnp.int32, (per, biased.shape[1]), 0)
    gscore = []
    for g in range(N_GROUPS):
        blk = biased[g * per:(g + 1) * per, :]
        top1 = jnp.max(blk, axis=0, keepdims=True)
        first = jnp.min(jnp.where(blk == top1, rows, per), axis=0, keepdims=True)
        top2 = jnp.max(jnp.where(rows == first, -jnp.inf, blk), axis=0, keepdims=True)
        gscore.append(top1 + top2)

    cand = []
    for g in range(N_GROUPS):
        rank = jnp.zeros_like(gscore[g], dtype=jnp.int32)
        for g2 in range(N_GROUPS):
            if g2 == g:
                continue
            ahead = gscore[g2] > gscore[g]
            if g2 < g:
                ahead = ahead | (gscore[g2] == gscore[g])
            rank = rank + ahead.astype(jnp.int32)
        keep = rank < TOPK_GROUPS
        cand.append(jnp.where(keep, biased[g * per:(g + 1) * per, :], -jnp.inf))
    cand = jnp.concatenate(cand, axis=0)

    eidx = lax.broadcasted_iota(jnp.int32, cand.shape, 0)
    work = cand
    for _ in range(TOP_K):
        top = jnp.max(work, axis=0, keepdims=True)
        first = jnp.min(jnp.where(work == top, eidx, N_EXPERTS), axis=0, keepdims=True)
        work = jnp.where(eidx == first, -jnp.inf, work)
    sel = (work != cand) & (cand > -jnp.inf)
    w = jnp.where(sel, scores, 0.0)
    comb = w / jnp.sum(w, axis=0, keepdims=True) * ROUTED_SCALE
    comb_ref[...] = comb

    @pl.when((pl.program_id(0) == 0) & (pl.program_id(1) == 0))
    def _():
        cnt_ref[...] = jnp.zeros_like(cnt_ref)

    hit = jnp.where(comb != 0.0, 1.0, 0.0).astype(BF16)
    cnt_ref[...] += _dot(hit, jnp.ones((hit.shape[1], LANES), BF16))


def _router(x1, mod, g2, wr_hi, wr_lo, rb):
    b, s, d = x1.shape
    tm = ROW_TILE
    nt = s // tm
    return pl.pallas_call(
        _router_kernel,
        grid=(b, nt),
        in_specs=[pl.BlockSpec((1, tm, d), lambda bi, i: (bi, i, 0)),
                  pl.BlockSpec((1, 6, d), lambda bi, i: (bi, 0, 0)),
                  _const_spec(g2.shape), _const_spec(wr_hi.shape), _const_spec(wr_lo.shape),
                  _const_spec(rb.shape)],
        out_specs=[pl.BlockSpec((1, tm, d), lambda bi, i: (bi, i, 0)),
                   pl.BlockSpec((1, tm, d // 2), lambda bi, i: (bi, i, 0)),
                   pl.BlockSpec((N_EXPERTS, tm), lambda bi, i: (0, bi * nt + i)),
                   pl.BlockSpec((N_EXPERTS, LANES), lambda bi, i: (0, 0))],
        out_shape=[jax.ShapeDtypeStruct((b, s, d), BF16),
                   jax.ShapeDtypeStruct((b, s, d // 2), jnp.int32),
                   jax.ShapeDtypeStruct((N_EXPERTS, b * s), F32),
                   jax.ShapeDtypeStruct((N_EXPERTS, LANES), F32)],
        compiler_params=_params("arbitrary", "arbitrary"),
        name="router",
    )(x1, mod, g2, wr_hi, wr_lo, rb)


def _pos_kernel(comb_ref, cnt_ref, tri_ref, lstrict_ref, posk_ref, wk_ref, gend_ref, base_ref, *, spare_row):
    tp = comb_ref.shape[1]
    comb = comb_ref[...]
    sel = comb != 0.0
    selb = jnp.where(sel, 1.0, 0.0).astype(BF16)

    @pl.when(pl.program_id(0) == 0)
    def _():
        seg = jnp.floor((cnt_ref[...] + (EXPERT_TILE - 1.0)) * (1.0 / EXPERT_TILE)) * EXPERT_TILE
        s_hi, s_mid, s_lo = _split3(seg)
        ls = lstrict_ref[...]
        start = _dot(ls, s_hi) + _dot(ls, s_mid) + _dot(ls, s_lo)
        base_ref[...] = start
        gend_ref[...] = start + seg

    rank = _dot(selb, tri_ref[...])
    pos = base_ref[:, 0:1] + rank - 1.0
    base_ref[...] += _dot(selb, jnp.ones((tp, LANES), BF16))
    slot = _dot(lstrict_ref[...], selb)
    rows_p, rows_w = [], []
    for k in range(TOP_K):
        m = sel & (slot == k)
        rows_p.append(jnp.sum(jnp.where(m, pos - spare_row, 0.0), axis=0, keepdims=True) + spare_row)
        rows_w.append(jnp.sum(jnp.where(m, comb, 0.0), axis=0, keepdims=True))
    posk_ref[...] = jnp.concatenate(rows_p, axis=0).astype(jnp.int32)
    wk_ref[...] = jnp.concatenate(rows_w, axis=0)


def _positions(comb_t, cnt, tri, lstrict, n_rows):
    ne, t = comb_t.shape
    tp = POS_TILE
    tok = lambda rows: pl.BlockSpec((rows, tp), lambda i: (0, i))
    return pl.pallas_call(
        functools.partial(_pos_kernel, spare_row=float(n_rows - 1)),
        grid=(t // tp,),
        in_specs=[tok(ne), _const_spec(cnt.shape), _const_spec(tri.shape), _const_spec(lstrict.shape)],
        out_specs=[tok(TOP_K), tok(TOP_K), pl.BlockSpec((ne, LANES), lambda i: (0, 0))],
        out_shape=[jax.ShapeDtypeStruct((TOP_K, t), jnp.int32),
                   jax.ShapeDtypeStruct((TOP_K, t), F32),
                   jax.ShapeDtypeStruct((ne, LANES), F32)],
        scratch_shapes=[pltpu.VMEM((ne, LANES), F32)],
        compiler_params=_params("arbitrary"),
        name="positions",
    )(comb_t, cnt, tri, lstrict)


def _sc_workers():
    info = plsc.get_sparse_core_info()
    return info.num_cores, info.num_cores * info.num_subcores


def _sc_scatter_rows(rows, pos, n_out):
    nc, nw = _sc_workers()
    n, w = rows.shape
    nk = pos.shape[0]
    ch = SC_CHUNK
    per_w = n // nw
    assert per_w * nw == n and per_w % ch == 0

    @functools.partial(
        pl.kernel, mesh=plsc.VectorSubcoreMesh(core_axis_name="c", subcore_axis_name="s"),
        out_type=jax.ShapeDtypeStruct((n_out, w), rows.dtype),
        scratch_types=[pltpu.VMEM((nk, ch), jnp.int32), pltpu.VMEM((ch, w), rows.dtype),
                       pltpu.SemaphoreType.DMA])
    def scatter(rows_hbm, pos_hbm, out_hbm, idx_v, rows_v, sem):
        base = (lax.axis_index("s") * nc + lax.axis_index("c")) * per_w

        @pl.loop(0, per_w // ch)
        def _(ci):
            off = pl.multiple_of(base + ci * ch, ch)
            pltpu.sync_copy(pos_hbm.at[:, pl.ds(off, ch)], idx_v)
            pltpu.sync_copy(rows_hbm.at[pl.ds(off, ch)], rows_v)
            copies = [pltpu.make_async_copy(rows_v, out_hbm.at[idx_v.at[k]], sem) for k in range(nk)]
            for cp in copies:
                cp.start()
            for cp in copies:
                cp.wait()

    return scatter(rows, pos)


def _sc_gather_rows(table, idx):
    nc, nw = _sc_workers()
    n = idx.shape[0]
    w = table.shape[1]
    ch = SC_CHUNK
    per_w = n // nw
    assert per_w * nw == n and per_w % ch == 0
    nch = per_w // ch

    @functools.partial(
        pl.kernel, mesh=plsc.VectorSubcoreMesh(core_axis_name="c", subcore_axis_name="s"),
        out_type=jax.ShapeDtypeStruct((n, w), table.dtype),
        scratch_types=[pltpu.VMEM((nch, ch), jnp.int32), pltpu.VMEM((ch, w), table.dtype),
                       pltpu.SemaphoreType.DMA])
    def gather(table_hbm, idx_hbm, out_hbm, idx_v, rows_v, sem):
        wid = lax.axis_index("s") * nc + lax.axis_index("c")
        base = wid * per_w
        pltpu.sync_copy(idx_hbm.at[pl.ds(wid * nch, nch)], idx_v)

        @pl.loop(0, nch)
        def _(ci):
            off = pl.multiple_of(base + ci * ch, ch)
            cp = pltpu.make_async_copy(table_hbm.at[idx_v.at[ci]], rows_v, sem)
            cp.start()
            cp.wait()
            pltpu.sync_copy(rows_v, out_hbm.at[pl.ds(off, ch)])

    return gather(table, idx.reshape(n // ch, ch))


def _expert_kernel(te_ref, nu_ref, xs_ref, wgu_ref, wd_ref, y_ref, wgu_b, wd_b):
    i = pl.program_id(0)

    @pl.when(i < nu_ref[0])
    def _():
        @pl.when((i == 0) | (te_ref[i] != te_ref[jnp.maximum(i - 1, 0)]))
        def _():
            wgu_b[...] = wgu_ref[0].astype(BF16)
            wd_b[...] = wd_ref[0].astype(BF16)

        lo, hi = _unpack_halves(xs_ref[...])
        half = wgu_b.shape[0] // 2
        gu = _dot(lo.astype(BF16), wgu_b[:half, :]) + _dot(hi.astype(BF16), wgu_b[half:, :])
        act = _silu(gu[:, :EXPERT_DIM]) * gu[:, EXPERT_DIM:]
        y_ref[...] = _pack_halves(_dot(act.astype(BF16), wd_b[...]))


def _experts(tile_e, n_used, xs, wgu, wd):
    r, w = xs.shape
    tm = EXPERT_TILE
    d = wgu.shape[1]
    rows = pl.BlockSpec((tm, w), lambda i, te, nu: (jnp.minimum(i, nu[0] - 1), 0))
    return pl.pallas_call(
        _expert_kernel,
        grid_spec=pltpu.PrefetchScalarGridSpec(
            num_scalar_prefetch=2,
            grid=(r // tm,),
            in_specs=[rows,
                      pl.BlockSpec((1, d, 2 * EXPERT_DIM), lambda i, te, nu: (te[i], 0, 0)),
                      pl.BlockSpec((1, EXPERT_DIM, d), lambda i, te, nu: (te[i], 0, 0))],
            out_specs=rows,
            scratch_shapes=[pltpu.VMEM((d, 2 * EXPERT_DIM), BF16), pltpu.VMEM((EXPERT_DIM, d), BF16)]),
        out_shape=jax.ShapeDtypeStruct((r, w), jnp.int32),
        compiler_params=_params("arbitrary"),
        name="experts",
    )(tile_e, n_used, xs, wgu, wd)


def _final_kernel(yk_ref, wk_ref, h_ref, x1_ref, mod_ref, wsgu_ref, wsd_ref, out_ref):
    half = yk_ref.shape[2]
    acc_lo = jnp.zeros((yk_ref.shape[1], half), F32)
    acc_hi = jnp.zeros((yk_ref.shape[1], half), F32)
    for k in range(TOP_K):
        lo, hi = _unpack_halves(yk_ref[k])
        w = wk_ref[:, k:k + 1]
        acc_lo = acc_lo + jnp.where(w != 0.0, w * lo, 0.0)
        acc_hi = acc_hi + jnp.where(w != 0.0, w * hi, 0.0)
    routed = jnp.concatenate([acc_lo, acc_hi], axis=1)
    sgu = _dot(h_ref[...], wsgu_ref[...])
    act = _silu(sgu[:, :SHARED_DIM]) * sgu[:, SHARED_DIM:]
    shared = _dot(act.astype(BF16), wsd_ref[...])
    out_ref[...] = x1_ref[...] + mod_ref[0, 5:6, :] * (routed + shared)


def _final(yk, wk_t, h2, x1, mod, wsgu, wsd, tiles_per_batch):
    t, d = h2.shape
    tm = ROW_TILE
    row = lambda w: pl.BlockSpec((tm, w), lambda i: (i, 0))
    return pl.pallas_call(
        _final_kernel,
        grid=(t // tm,),
        in_specs=[pl.BlockSpec((TOP_K, tm, d // 2), lambda i: (0, i, 0)),
                  row(TOP_K), row(d), row(d),
                  pl.BlockSpec((1, 6, d), lambda i: (i // tiles_per_batch, 0, 0)),
                  _const_spec(wsgu.shape), _const_spec(wsd.shape)],
        out_specs=row(d),
        out_shape=jax.ShapeDtypeStruct((t, d), F32),
        compiler_params=_params("arbitrary"),
        name="final",
    )(yk, wk_t, h2, x1, mod, wsgu, wsd)


def _placement():
    pq = np.zeros((LANES, N_HEADS * HEAD_PAD), np.float32)
    cq = np.zeros((1, N_HEADS * HEAD_PAD), np.float32)
    ck = np.zeros((1, N_HEADS * HEAD_PAD), np.float32)
    for hd in range(N_HEADS):
        for k in range(3):
            pq[k * N_HEADS + hd, hd * HEAD_PAD + AUG0 + k] = 1.0
            ck[0, hd * HEAD_PAD + AUG0 + k] = 1.0
            cq[0, hd * HEAD_PAD + AUG0 + 3 + k] = 1.0
    return jnp.asarray(pq, BF16), jnp.asarray(cq), jnp.asarray(ck)


def kernel(x, c, w_ada, b_ada, norm1_g, w_in, w_dw, b_dw, conv_gn_g, conv_gn_b, w_conv_out,
           q_norm_g, k_norm_g, b_forget, w_attn_out, w_out, norm2_g, w_router, router_bias,
           w_experts_gate_up, w_experts_down, w_shared_gate_up, w_shared_down):
    depth = w_ada.shape[0]
    b, s, d = x.shape

    pq, cq, ck = _placement()
    tri = jnp.asarray(np.tril(np.ones((ROW_TILE, ROW_TILE), np.float32)), BF16)
    grp = np.arange(CONV_DIM) // (CONV_DIM // CONV_GROUPS)
    gg = jnp.asarray((grp[:, None] == grp[None, :]).astype(np.float32) / (CONV_DIM // CONV_GROUPS), BF16)
    c_pad = jnp.pad(c, ((0, SUBLANES - b), (0, 0)))
    tri_pos = jnp.asarray(np.triu(np.ones((POS_TILE, POS_TILE), np.float32)), BF16)
    lstrict = jnp.asarray(np.tril(np.ones((N_EXPERTS, N_EXPERTS), np.float32), -1), BF16)

    for l in range(depth):
        mod = _ada(c_pad, w_ada[l], b_ada[l][None, :])[:b].reshape(b, 6, d)

        bf = jnp.pad(b_forget[l][None, :], ((0, 0), (0, LANES - N_HEADS)))
        gpad = lambda g, sc: jnp.tile(jnp.pad(g * sc, (0, HEAD_PAD - HEAD_DIM)), N_HEADS)[None, :]
        qscale = HEAD_DIM ** -0.5
        u, qa, ka, v, sgc, sga, cum = _inproj(
            x, mod, norm1_g[l][None, :], _inproj_weight(w_in[l]),
            bf, gpad(q_norm_g[l], qscale), gpad(k_norm_g[l], 1.0),
            pq, cq, ck, tri)

        flat = lambda a: a[:, :, :N_HEADS].transpose(0, 2, 1).reshape(-1)
        part = ATTN_TILE // DIAG_PARTS
        cs = flat(cum[:, 0::part])
        ce = flat(cum[:, part - 1::part])
        bound = (1.02 * HEAD_DIM * qscale) * jnp.max(jnp.abs(q_norm_g[l])) * jnp.max(jnp.abs(k_norm_g[l]))
        o = _attention(cs, ce, bound.reshape(1), qa, ka, v)

        wdw = jnp.pad(w_dw[l], ((0, CONV_HALO - CONV_WIDTH), (0, 0)))
        x1 = _merge(u, o, sgc, sga, x, mod, wdw, b_dw[l][None, :], conv_gn_g[l][None, :],
                    conv_gn_b[l][None, :], gg, w_conv_out[l].astype(BF16), w_attn_out[l].astype(BF16),
                    w_out[l].astype(BF16))

        wr = w_router[l].T
        wr_hi = wr.astype(BF16)
        wr_lo = (wr - wr_hi.astype(F32)).astype(BF16)
        h2, h2w, comb_t, cnt = _router(x1, mod, norm2_g[l][None, :], wr_hi, wr_lo, router_bias[l][:, None])

        t = b * s
        n_tiles = (t * TOP_K) // EXPERT_TILE + N_EXPERTS
        posk, wk, gend = _positions(comb_t, cnt, tri_pos, lstrict, n_tiles * EXPERT_TILE)
        seg_end = gend[:, 0].astype(jnp.int32)
        n_used = seg_end[-1:] // EXPERT_TILE
        tile_start = jnp.arange(n_tiles, dtype=jnp.int32) * EXPERT_TILE
        tile_start = jnp.minimum(tile_start, seg_end[-1] - EXPERT_TILE)
        tile_e = jnp.sum((seg_end[None, :] <= tile_start[:, None]).astype(jnp.int32), axis=1)

        xs = _sc_scatter_rows(h2w.reshape(t, d // 2), posk, n_tiles * EXPERT_TILE)
        ys = _experts(tile_e, n_used, xs, w_experts_gate_up[l], w_experts_down[l])
        yk = _sc_gather_rows(ys, posk.reshape(-1)).reshape(TOP_K, t, d // 2)
        out = _final(yk, wk.T, h2.reshape(t, d), x1.reshape(t, d), mod,
                     w_shared_gate_up[l].astype(BF16), w_shared_down[l].astype(BF16), s // ROW_TILE)
        x = out.reshape(b, s, d)
    return x
```

```python
import functools

import numpy as np
import jax
import jax.numpy as jnp
from jax import lax
from jax.experimental import pallas as pl
from jax.experimental.pallas import tpu as pltpu
from jax.experimental.pallas import tpu_sc as plsc

F32 = jnp.float32
BF16 = jnp.bfloat16

CONV_DIM = 512
CONV_WIDTH = 31
CONV_GROUPS = 8
N_HEADS = 8
HEAD_DIM = 64
ATTN_DIM = N_HEADS * HEAD_DIM
N_EXPERTS = 64
TOP_K = 8
N_GROUPS = 8
TOPK_GROUPS = 4
EXPERT_DIM = 256
SHARED_DIM = 256
ROUTED_SCALE = 2.5
EPS = 1e-6

LANES = 128
SUBLANES = 8
HEAD_PAD = LANES
AUG0 = HEAD_DIM
F_LANE0 = HEAD_DIM + 8
VMEM_LIMIT = 56 * 1024 * 1024

ROW_TILE = 512
ATTN_TILE = 1024
DIAG_PARTS = 4
EXPERT_TILE = 2048
WPREP_ROWS = 256
POS_TILE = 1024
SC_CHUNK = 128
CONV_HALO = 32

NEG_BIG = -1e30
EXP_UNDERFLOW = 104.0
FIXED_SHIFT_BOUND = 40.0


def _dot(a, b):
    return jnp.dot(a, b, preferred_element_type=F32)


def _dot_nt(a, b):
    return lax.dot_general(a, b, (((1,), (1,)), ((), ())), preferred_element_type=F32)


def _split2(x):
    hi = x.astype(BF16)
    lo = (x - hi.astype(F32)).astype(BF16)
    return hi, lo


def _split3(x):
    hi = x.astype(BF16)
    r = x - hi.astype(F32)
    mid = r.astype(BF16)
    lo = (r - mid.astype(F32)).astype(BF16)
    return hi, mid, lo


def _pack_halves(v):
    n = v.shape[1] // 2
    lo = lax.bitcast_convert_type(v[:, :n].astype(BF16).astype(F32), jnp.uint32)
    hi = lax.bitcast_convert_type(v[:, n:].astype(BF16).astype(F32), jnp.uint32)
    return lax.bitcast_convert_type(hi | lax.shift_right_logical(lo, jnp.uint32(16)), jnp.int32)


def _unpack_halves(w):
    u = lax.bitcast_convert_type(w, jnp.uint32)
    lo = lax.bitcast_convert_type(lax.shift_left(u, jnp.uint32(16)), F32)
    hi = lax.bitcast_convert_type(u & jnp.uint32(0xFFFF0000), F32)
    return lo, hi


def _sigmoid(x):
    return 1.0 / (1.0 + jnp.exp(-x))


def _silu(x):
    return x * _sigmoid(x)


def _params(*sem):
    return pltpu.CompilerParams(dimension_semantics=sem, vmem_limit_bytes=VMEM_LIMIT)


def _const_spec(shape):
    n = len(shape)
    return pl.BlockSpec(shape, lambda *_: (0,) * n, pipeline_mode=pl.Buffered(1))


def _ada_kernel(c_ref, w_ref, b_ref, o_ref):
    c = c_ref[...]
    a_hi, a_lo = _split2(_silu(c))
    w_hi, w_lo = _split2(w_ref[...])
    o_ref[...] = _dot(a_hi, w_hi) + _dot(a_hi, w_lo) + _dot(a_lo, w_hi) + b_ref[...]


def _ada(c_pad, w_ada, b_ada):
    d = c_pad.shape[1]
    n = w_ada.shape[1]
    return pl.pallas_call(
        _ada_kernel,
        grid=(n // d,),
        in_specs=[_const_spec(c_pad.shape),
                  pl.BlockSpec((d, d), lambda j: (0, j)),
                  pl.BlockSpec((1, d), lambda j: (0, j))],
        out_specs=pl.BlockSpec((c_pad.shape[0], d), lambda j: (0, j)),
        out_shape=jax.ShapeDtypeStruct((c_pad.shape[0], n), F32),
        compiler_params=_params("arbitrary"),
        name="ada",
    )(c_pad, w_ada, b_ada)


def _lane_pieces(x):
    hi, mid, lo = _split3(x)
    return (hi.astype(F32) + pltpu.roll(mid.astype(F32), N_HEADS, 1)
            + pltpu.roll(lo.astype(F32), 2 * N_HEADS, 1)).astype(BF16)


def _head_tile(p, col0, hd):
    per = HEAD_PAD // HEAD_DIM
    g, part = divmod(hd, per)
    blk = p[:, col0 + g * HEAD_PAD:col0 + (g + 1) * HEAD_PAD]
    if part:
        blk = pltpu.roll(blk, HEAD_PAD - part * HEAD_DIM, 1)
    lane = lax.broadcasted_iota(jnp.int32, (1, HEAD_PAD), 1)
    return jnp.where(lane < HEAD_DIM, blk, 0.0)


def _inproj_kernel(x_ref, mod_ref, g1_ref, w_ref, bf_ref, qg_ref, kg_ref, pq_ref, cq_ref, ck_ref, tri_ref,
                   u_ref, qa_ref, ka_ref, v_ref, sgc_ref, sga_ref, cum_ref, carry_ref):
    @pl.when(pl.program_id(1) == 0)
    def _():
        carry_ref[...] = jnp.zeros_like(carry_ref)

    x = x_ref[0]
    ms = jnp.mean(x * x, axis=-1, keepdims=True)
    h = (x * lax.rsqrt(ms + EPS) * g1_ref[...]) * (1.0 + mod_ref[0, 1:2, :]) + mod_ref[0, 0:1, :]
    hb = h.astype(BF16)

    hp = N_HEADS * HEAD_PAD
    o_q = 2 * CONV_DIM
    o_k, o_v = o_q + hp, o_q + 2 * hp
    o_gc = o_v + ATTN_DIM
    o_ga = o_gc + x.shape[1]
    pc = _dot(hb, w_ref[:, :o_q])
    u_ref[0] = (pc[:, :CONV_DIM] * _sigmoid(pc[:, CONV_DIM:])).astype(BF16)

    pq = _dot(hb, w_ref[:, o_q:o_k])
    z = pltpu.roll(pq[:, :HEAD_PAD], HEAD_PAD - F_LANE0, 1) + bf_ref[...]
    lf = jnp.minimum(z, 0.0) - jnp.log1p(jnp.exp(-jnp.abs(z)))
    lane = lax.broadcasted_iota(jnp.int32, (1, LANES), 1)
    psum = _dot(tri_ref[...], _lane_pieces(jnp.where(lane < N_HEADS, lf, 0.0)))
    cum = (psum + pltpu.roll(psum, LANES - N_HEADS, 1) + pltpu.roll(psum, LANES - 2 * N_HEADS, 1)
           + carry_ref[...])
    cum = jnp.where(lane < N_HEADS, cum, 0.0)
    carry_ref[...] = cum[ROW_TILE - 1:ROW_TILE, :]
    cum_ref[0] = cum

    pieces = _lane_pieces(cum)
    placed = _dot(pieces, pq_ref[...])
    addq = placed + cq_ref[...]
    addk = ck_ref[...] - pltpu.roll(placed, 3, 1)

    pk = _dot(hb, w_ref[:, o_k:o_v])
    pv = _dot(hb, w_ref[:, o_v:o_gc])
    inv_hd = 1.0 / HEAD_DIM
    hlane = lax.broadcasted_iota(jnp.int32, (1, HEAD_PAD), 1)
    vone = (hlane == HEAD_DIM).astype(F32)
    for hd in range(N_HEADS):
        sl = slice(hd * HEAD_PAD, (hd + 1) * HEAD_PAD)
        qb = pq[:, sl]
        if hd == 0:
            qb = jnp.where(hlane < HEAD_DIM, qb, 0.0)
        qn = qb * lax.rsqrt(jnp.sum(qb * qb, axis=-1, keepdims=True) * inv_hd + EPS) * qg_ref[:, sl]
        qa_ref[0, hd] = (qn + addq[:, sl]).astype(BF16)
        kb = pk[:, sl]
        kn = kb * lax.rsqrt(jnp.sum(kb * kb, axis=-1, keepdims=True) * inv_hd + EPS) * kg_ref[:, sl]
        ka_ref[0, hd] = (kn + addk[:, sl]).astype(BF16)
        v_ref[0, hd] = (_head_tile(pv, 0, hd) + vone).astype(BF16)

    sgc_ref[0] = _sigmoid(_dot(hb, w_ref[:, o_gc:o_ga])).astype(BF16)
    sga_ref[0] = _sigmoid(_dot(hb, w_ref[:, o_ga:])).astype(BF16)


def _inproj_weight(wi):
    d, n_in = wi.shape
    n_out = 2 * CONV_DIM + 2 * N_HEADS * HEAD_PAD + ATTN_DIM + 2 * d
    rows = WPREP_ROWS
    return pl.pallas_call(
        _wprep_kernel,
        grid=(d // rows,),
        in_specs=[pl.BlockSpec((rows, n_in), lambda i: (i, 0))],
        out_specs=pl.BlockSpec((rows, n_out), lambda i: (i, 0)),
        out_shape=jax.ShapeDtypeStruct((d, n_out), BF16),
        compiler_params=_params("arbitrary"),
        name="wprep",
    )(wi)


def _wprep_kernel(w_ref, o_ref):
    rows, n_in = w_ref.shape
    o_q = 2 * CONV_DIM
    o_f = o_q + 3 * ATTN_DIM
    o_gc = o_f + N_HEADS
    lane = lax.broadcasted_iota(jnp.int32, (1, LANES), 1)
    per = HEAD_PAD // HEAD_DIM

    o_ref[:, :o_q] = w_ref[:, :o_q].astype(BF16)
    out = o_q
    fcols = pltpu.roll(w_ref[:, o_f:o_f + LANES], F_LANE0, 1)
    for blk in range(2):
        src0 = o_q + blk * ATTN_DIM
        for hd in range(N_HEADS):
            g, part = divmod(hd, per)
            tile = w_ref[:, src0 + g * HEAD_PAD:src0 + (g + 1) * HEAD_PAD]
            if part:
                tile = pltpu.roll(tile, HEAD_PAD - part * HEAD_DIM, 1)
            tile = jnp.where(lane < HEAD_DIM, tile, 0.0)
            if blk == 0 and hd == 0:
                tile = jnp.where((lane >= F_LANE0) & (lane < F_LANE0 + N_HEADS), fcols, tile)
            o_ref[:, out:out + HEAD_PAD] = tile.astype(BF16)
            out += HEAD_PAD
    o_ref[:, out:out + ATTN_DIM] = w_ref[:, o_q + 2 * ATTN_DIM:o_f].astype(BF16)
    out += ATTN_DIM
    shift = LANES - (o_gc % LANES)
    for j in range((n_in - o_gc) // LANES):
        a0 = (o_gc // LANES + j) * LANES
        a = pltpu.roll(w_ref[:, a0:a0 + LANES], shift, 1)
        width = min(LANES, n_in - (a0 + LANES))
        b = w_ref[:, a0 + LANES:a0 + LANES + width]
        if width < LANES:
            b = jnp.concatenate([b, jnp.zeros((rows, LANES - width), F32)], axis=1)
        b = pltpu.roll(b, shift, 1)
        o_ref[:, out:out + LANES] = jnp.where(lane < shift, a, b).astype(BF16)
        out += LANES


def _inproj(x, mod, g1, w, bf, qg, kg, pq, cq, ck, tri):
    b, s, d = x.shape
    tm = ROW_TILE
    row = lambda w: pl.BlockSpec((1, tm, w), lambda bi, i: (bi, i, 0))
    head = pl.BlockSpec((1, N_HEADS, tm, HEAD_PAD), lambda bi, i: (bi, 0, i, 0))
    consts = [g1, w, bf, qg, kg, pq, cq, ck, tri]
    return pl.pallas_call(
        _inproj_kernel,
        grid=(b, s // tm),
        in_specs=[row(d), pl.BlockSpec((1, 6, d), lambda bi, i: (bi, 0, 0))]
                 + [_const_spec(a.shape) for a in consts],
        out_specs=[row(CONV_DIM), head, head, head, row(d), row(d), row(LANES)],
        out_shape=[jax.ShapeDtypeStruct((b, s, CONV_DIM), BF16),
                   jax.ShapeDtypeStruct((b, N_HEADS, s, HEAD_PAD), BF16),
                   jax.ShapeDtypeStruct((b, N_HEADS, s, HEAD_PAD), BF16),
                   jax.ShapeDtypeStruct((b, N_HEADS, s, HEAD_PAD), BF16),
                   jax.ShapeDtypeStruct((b, s, d), BF16),
                   jax.ShapeDtypeStruct((b, s, d), BF16),
                   jax.ShapeDtypeStruct((b, s, LANES), F32)],
        scratch_shapes=[pltpu.VMEM((1, LANES), F32)],
        compiler_params=_params("arbitrary", "arbitrary"),
        name="inproj",
    )(x, mod, *consts)


def _attn_kernel(cs_ref, ce_ref, prm_ref, q_ref, k_ref, v_ref, o_ref, acc_ref, m_ref):
    t = ATTN_TILE
    nb = q_ref.shape[2] // t
    base = (pl.program_id(0) * pl.num_programs(1) + pl.program_id(1)) * nb
    bound = prm_ref[0]
    thresh = -(EXP_UNDERFLOW + 2.0 * bound)
    causal = lax.broadcasted_iota(jnp.int32, (t, t), 0) >= lax.broadcasted_iota(jnp.int32, (t, t), 1)

    dp = DIAG_PARTS

    def first_block(i):
        c0 = cs_ref[dp * (base + i)]
        return lax.fori_loop(
            0, i, lambda j, n: n + (c0 - ce_ref[dp * (base + j) + dp - 1] < thresh).astype(jnp.int32), 0)

    def scores(q, j, masked):
        k0 = pl.multiple_of(j * t, t)
        s = _dot_nt(q, k_ref[0, 0, pl.ds(k0, t), :])
        if masked:
            s = jnp.where(causal, s, NEG_BIG)
        return s, v_ref[0, 0, pl.ds(k0, t), :]

    def finish(q0):
        acc = acc_ref[...]
        o_ref[0, pl.ds(q0, t), :] = (acc / acc[:, HEAD_DIM:HEAD_DIM + 1]).astype(BF16)

    def fixed_shift(i, carry):
        q0 = pl.multiple_of(i * t, t)
        q = q_ref[0, 0, pl.ds(q0, t), :]

        acc_ref[...] = jnp.zeros_like(acc_ref)

        def weighted(j, masked):
            s, vb = scores(q, j, masked)
            return _dot(jnp.exp(s).astype(BF16), vb)

        def kv(j, c):
            acc_ref[...] += weighted(j, False)
            return c

        lax.fori_loop(first_block(i), i - 1, kv, 0)
        rp = t // DIAG_PARTS

        @pl.when(i == 0)
        def _():
            for r in range(DIAG_PARTS):
                nk = (r + 1) * rp
                s = _dot_nt(q[r * rp:(r + 1) * rp, :], k_ref[0, 0, pl.ds(q0, nk), :])
                seen = (lax.broadcasted_iota(jnp.int32, (rp, nk), 0) + r * rp
                        >= lax.broadcasted_iota(jnp.int32, (rp, nk), 1))
                p = jnp.exp(jnp.where(seen, s, NEG_BIG)).astype(BF16)
                acc_ref[r * rp:(r + 1) * rp, :] += _dot(p, v_ref[0, 0, pl.ds(q0, nk), :])
            finish(q0)

        @pl.when(i > 0)
        def _():
            nk = t + rp
            for r in range(DIAG_PARTS):
                k0 = pl.multiple_of(q0 - t + r * rp, rp)
                s = _dot_nt(q[r * rp:(r + 1) * rp, :], k_ref[0, 0, pl.ds(k0, nk), :])
                seen = (lax.broadcasted_iota(jnp.int32, (rp, nk), 0) + t
                        >= lax.broadcasted_iota(jnp.int32, (rp, nk), 1))
                p = jnp.exp(jnp.where(seen, s, NEG_BIG)).astype(BF16)
                acc_ref[r * rp:(r + 1) * rp, :] += _dot(p, v_ref[0, 0, pl.ds(k0, nk), :])

            for r in range(1, DIAG_PARTS):
                need = cs_ref[dp * (base + i) + r] - ce_ref[dp * (base + i - 1) + r - 1] >= thresh

                @pl.when(need)
                def _():
                    kp = pl.multiple_of(q0 - t, t)
                    s = _dot_nt(q[r * rp:(r + 1) * rp, :], k_ref[0, 0, pl.ds(kp, r * rp), :])
                    acc_ref[r * rp:(r + 1) * rp, :] += _dot(jnp.exp(s).astype(BF16),
                                                            v_ref[0, 0, pl.ds(kp, r * rp), :])

            finish(q0)

        return carry

    def running_max(i, carry):
        q0 = pl.multiple_of(i * t, t)
        q = q_ref[0, 0, pl.ds(q0, t), :]
        m_ref[...] = jnp.full_like(m_ref, -jnp.inf)
        acc_ref[...] = jnp.zeros_like(acc_ref)

        def step(j, masked):
            s, vb = scores(q, j, masked)
            m_prev = m_ref[...]
            m_new = jnp.maximum(m_prev, jnp.max(s, axis=-1, keepdims=True))
            p = jnp.exp(s - m_new)
            acc_ref[...] = jnp.exp(m_prev - m_new) * acc_ref[...] + _dot(p.astype(BF16), vb)
            m_ref[...] = m_new

        def kv(j, c):
            step(j, False)
            return c

        lax.fori_loop(first_block(i), i, kv, 0)
        step(i, True)
        finish(q0)
        return carry

    @pl.when(bound <= FIXED_SHIFT_BOUND)
    def _():
        lax.fori_loop(0, nb, fixed_shift, 0)

    @pl.when(bound > FIXED_SHIFT_BOUND)
    def _():
        lax.fori_loop(0, nb, running_max, 0)


def _attention(cs, ce, prm, qa, ka, v):
    b, nh, s, hp = qa.shape
    t = ATTN_TILE
    seq = pl.BlockSpec((1, 1, s, hp), lambda bi, hi, *_: (bi, hi, 0, 0))
    return pl.pallas_call(
        _attn_kernel,
        grid_spec=pltpu.PrefetchScalarGridSpec(
            num_scalar_prefetch=3,
            grid=(b, nh),
            in_specs=[seq, seq, seq],
            out_specs=pl.BlockSpec((1, s, hp), lambda bi, hi, *_: (bi, 0, hi)),
            scratch_shapes=[pltpu.VMEM((t, hp), F32), pltpu.VMEM((t, 1), F32)]),
        out_shape=jax.ShapeDtypeStruct((b, s, nh * hp), BF16),
        compiler_params=_params("arbitrary", "arbitrary"),
        name="attn",
    )(cs, ce, prm, qa, ka, v)


def _merge_kernel(u_ref, halo_ref, o_ref, sgc_ref, sga_ref, x_ref, mod_ref, wdw_ref, bdw_ref,
                  gng_ref, gnb_ref, gg_ref, wco_ref, wao_ref, wout_ref, x1_ref, buf_ref):
    tm = ROW_TILE
    halo = halo_ref[0].astype(F32)
    halo = jnp.where(pl.program_id(1) == 0, jnp.zeros_like(halo), halo)
    ucur = u_ref[0].astype(F32)
    for cb in range(CONV_DIM // LANES):
        buf_ref[cb, 0:CONV_HALO, :] = halo[:, cb * LANES:(cb + 1) * LANES]
        buf_ref[cb, CONV_HALO:, :] = ucur[:, cb * LANES:(cb + 1) * LANES]

    base = CONV_HALO - (CONV_WIDTH - 1)
    ys = []
    for cb in range(CONV_DIM // LANES):
        acc = jnp.zeros((tm, LANES), F32)
        for j in range(CONV_WIDTH):
            acc = acc + wdw_ref[j:j + 1, cb * LANES:(cb + 1) * LANES] * buf_ref[cb, base + j:base + j + tm, :]
        ys.append(acc)
    y = jnp.concatenate(ys, axis=1) + bdw_ref[...]

    gg = gg_ref[...]
    y_hi, y_lo = _split2(y)
    dlt = y - (_dot(y_hi, gg) + _dot(y_lo, gg))
    s_hi, s_lo = _split2(dlt * dlt)
    var = _dot(s_hi, gg) + _dot(s_lo, gg)
    yn = dlt * lax.rsqrt(var + EPS) * gng_ref[...] + gnb_ref[...]
    y_conv = _dot(_silu(yn).astype(BF16), wco_ref[...])

    per = HEAD_PAD // HEAD_DIM
    lane = lax.broadcasted_iota(jnp.int32, (1, HEAD_PAD), 1)
    packed = []
    for g in range(N_HEADS // per):
        tile = o_ref[0, :, g * per * HEAD_PAD:(g * per + 1) * HEAD_PAD].astype(F32)
        for part in range(1, per):
            nxt = o_ref[0, :, (g * per + part) * HEAD_PAD:(g * per + part + 1) * HEAD_PAD].astype(F32)
            tile = jnp.where(lane < part * HEAD_DIM, tile, pltpu.roll(nxt, part * HEAD_DIM, 1))
        packed.append(tile)
    y_attn = _dot(jnp.concatenate(packed, axis=1).astype(BF16), wao_ref[...])
    merged = sgc_ref[0].astype(F32) * y_conv + sga_ref[0].astype(F32) * y_attn
    mix = _dot(merged.astype(BF16), wout_ref[...])
    x1_ref[0] = x_ref[0] + mod_ref[0, 2:3, :] * mix


def _merge(u, o, sgc, sga, x, mod, wdw, bdw, gng, gnb, gg, wco, wao, wout):
    b, s, d = x.shape
    tm = ROW_TILE
    per = tm // CONV_HALO
    row = lambda w: pl.BlockSpec((1, tm, w), lambda bi, i: (bi, i, 0))
    consts = [wdw, bdw, gng, gnb, gg, wco, wao, wout]
    return pl.pallas_call(
        _merge_kernel,
        grid=(b, s // tm),
        in_specs=[row(CONV_DIM),
                  pl.BlockSpec((1, CONV_HALO, CONV_DIM),
                               lambda bi, i: (bi, jnp.maximum(i * per - 1, 0), 0)),
                  row(o.shape[2]), row(d), row(d), row(d),
                  pl.BlockSpec((1, 6, d), lambda bi, i: (bi, 0, 0))]
                 + [_const_spec(a.shape) for a in consts],
        out_specs=row(d),
        out_shape=jax.ShapeDtypeStruct((b, s, d), F32),
        scratch_shapes=[pltpu.VMEM((CONV_DIM // LANES, CONV_HALO + tm, LANES), F32)],
        compiler_params=_params("arbitrary", "arbitrary"),
        name="merge",
    )(u, u, o, sgc, sga, x, mod, *consts)


def _router_kernel(x1_ref, mod_ref, g2_ref, wr_hi_ref, wr_lo_ref, rb_ref, h2_ref, h2w_ref, comb_ref, cnt_ref):
    x = x1_ref[0]
    ms = jnp.mean(x * x, axis=-1, keepdims=True)
    h = (x * lax.rsqrt(ms + EPS) * g2_ref[...]) * (1.0 + mod_ref[0, 4:5, :]) + mod_ref[0, 3:4, :]
    h2_ref[0] = h.astype(BF16)
    h2w_ref[0] = _pack_halves(h)

    h_hi, h_lo = _split2(h)
    logits = _dot_nt(wr_hi_ref[...], h_hi) + _dot_nt(wr_hi_ref[...], h_lo) + _dot_nt(wr_lo_ref[...], h_hi)
    scores = _sigmoid(logits)
    biased = scores + rb_ref[...]

    per = N_EXPERTS // N_GROUPS
    rows = lax.broadcasted_iota(jnp.int32, (per, biased.shape[1]), 0)
    gscore = []
    for g in range(N_GROUPS):
        blk = biased[g * per:(g + 1) * per, :]
        top1 = jnp.max(blk, axis=0, keepdims=True)
        first = jnp.min(jnp.where(blk == top1, rows, per), axis=0, keepdims=True)
        top2 = jnp.max(jnp.where(rows == first, -jnp.inf, blk), axis=0, keepdims=True)
        gscore.append(top1 + top2)

    cand = []
    for g in range(N_GROUPS):
        rank = jnp.zeros_like(gscore[g], dtype=jnp.int32)
        for g2 in range(N_GROUPS):
            if g2 == g:
                continue
            ahead = gscore[g2] > gscore[g]
            if g2 < g:
                ahead = ahead | (gscore[g2] == gscore[g])
            rank = rank + ahead.astype(jnp.int32)
        keep = rank < TOPK_GROUPS
        cand.append(jnp.where(keep, biased[g * per:(g + 1) * per, :], -jnp.inf))
    cand = jnp.concatenate(cand, axis=0)

    eidx = lax.broadcasted_iota(jnp.int32, cand.shape, 0)
    work = cand
    for _ in range(TOP_K):
        top = jnp.max(work, axis=0, keepdims=True)
        first = jnp.min(jnp.where(work == top, eidx, N_EXPERTS), axis=0, keepdims=True)
        work = jnp.where(eidx == first, -jnp.inf, work)
    sel = (work != cand) & (cand > -jnp.inf)
    w = jnp.where(sel, scores, 0.0)
    comb = w / jnp.sum(w, axis=0, keepdims=True) * ROUTED_SCALE
    comb_ref[...] = comb

    @pl.when((pl.program_id(0) == 0) & (pl.program_id(1) == 0))
    def _():
        cnt_ref[...] = jnp.zeros_like(cnt_ref)

    hit = jnp.where(comb != 0.0, 1.0, 0.0).astype(BF16)
    cnt_ref[...] += _dot(hit, jnp.ones((hit.shape[1], LANES), BF16))


def _router(x1, mod, g2, wr_hi, wr_lo, rb):
    b, s, d = x1.shape
    tm = ROW_TILE
    nt = s // tm
    return pl.pallas_call(
        _router_kernel,
        grid=(b, nt),
        in_specs=[pl.BlockSpec((1, tm, d), lambda bi, i: (bi, i, 0)),
                  pl.BlockSpec((1, 6, d), lambda bi, i: (bi, 0, 0)),
                  _const_spec(g2.shape), _const_spec(wr_hi.shape), _const_spec(wr_lo.shape),
                  _const_spec(rb.shape)],
        out_specs=[pl.BlockSpec((1, tm, d), lambda bi, i: (bi, i, 0)),
                   pl.BlockSpec((1, tm, d // 2), lambda bi, i: (bi, i, 0)),
                   pl.BlockSpec((N_EXPERTS, tm), lambda bi, i: (0, bi * nt + i)),
                   pl.BlockSpec((N_EXPERTS, LANES), lambda bi, i: (0, 0))],
        out_shape=[jax.ShapeDtypeStruct((b, s, d), BF16),
                   jax.ShapeDtypeStruct((b, s, d // 2), jnp.int32),
                   jax.ShapeDtypeStruct((N_EXPERTS, b * s), F32),
                   jax.ShapeDtypeStruct((N_EXPERTS, LANES), F32)],
        compiler_params=_params("arbitrary", "arbitrary"),
        name="router",
    )(x1, mod, g2, wr_hi, wr_lo, rb)


def _pos_kernel(comb_ref, cnt_ref, tri_ref, lstrict_ref, posk_ref, wk_ref, gend_ref, base_ref, *, spare_row):
    tp = comb_ref.shape[1]
    comb = comb_ref[...]
    sel = comb != 0.0
    selb = jnp.where(sel, 1.0, 0.0).astype(BF16)

    @pl.when(pl.program_id(0) == 0)
    def _():
        seg = jnp.floor((cnt_ref[...] + (EXPERT_TILE - 1.0)) * (1.0 / EXPERT_TILE)) * EXPERT_TILE
        s_hi, s_mid, s_lo = _split3(seg)
        ls = lstrict_ref[...]
        start = _dot(ls, s_hi) + _dot(ls, s_mid) + _dot(ls, s_lo)
        base_ref[...] = start
        gend_ref[...] = start + seg

    rank = _dot(selb, tri_ref[...])
    pos = base_ref[:, 0:1] + rank - 1.0
    base_ref[...] += _dot(selb, jnp.ones((tp, LANES), BF16))
    slot = _dot(lstrict_ref[...], selb)
    rows_p, rows_w = [], []
    for k in range(TOP_K):
        m = sel & (slot == k)
        rows_p.append(jnp.sum(jnp.where(m, pos - spare_row, 0.0), axis=0, keepdims=True) + spare_row)
        rows_w.append(jnp.sum(jnp.where(m, comb, 0.0), axis=0, keepdims=True))
    posk_ref[...] = jnp.concatenate(rows_p, axis=0).astype(jnp.int32)
    wk_ref[...] = jnp.concatenate(rows_w, axis=0)


def _positions(comb_t, cnt, tri, lstrict, n_rows):
    ne, t = comb_t.shape
    tp = POS_TILE
    tok = lambda rows: pl.BlockSpec((rows, tp), lambda i: (0, i))
    return pl.pallas_call(
        functools.partial(_pos_kernel, spare_row=float(n_rows - 1)),
        grid=(t // tp,),
        in_specs=[tok(ne), _const_spec(cnt.shape), _const_spec(tri.shape), _const_spec(lstrict.shape)],
        out_specs=[tok(TOP_K), tok(TOP_K), pl.BlockSpec((ne, LANES), lambda i: (0, 0))],
        out_shape=[jax.ShapeDtypeStruct((TOP_K, t), jnp.int32),
                   jax.ShapeDtypeStruct((TOP_K, t), F32),
                   jax.ShapeDtypeStruct((ne, LANES), F32)],
        scratch_shapes=[pltpu.VMEM((ne, LANES), F32)],
        compiler_params=_params("arbitrary"),
        name="positions",
    )(comb_t, cnt, tri, lstrict)


def _sc_workers():
    info = plsc.get_sparse_core_info()
    return info.num_cores, info.num_cores * info.num_subcores


def _sc_scatter_rows(rows, pos, n_out):
    nc, nw = _sc_workers()
    n, w = rows.shape
    nk = pos.shape[0]
    ch = SC_CHUNK
    per_w = n // nw
    assert per_w * nw == n and per_w % ch == 0

    @functools.partial(
        pl.kernel, mesh=plsc.VectorSubcoreMesh(core_axis_name="c", subcore_axis_name="s"),
        out_type=jax.ShapeDtypeStruct((n_out, w), rows.dtype),
        scratch_types=[pltpu.VMEM((nk, ch), jnp.int32), pltpu.VMEM((ch, w), rows.dtype),
                       pltpu.SemaphoreType.DMA])
    def scatter(rows_hbm, pos_hbm, out_hbm, idx_v, rows_v, sem):
        base = (lax.axis_index("s") * nc + lax.axis_index("c")) * per_w

        @pl.loop(0, per_w // ch)
        def _(ci):
            off = pl.multiple_of(base + ci * ch, ch)
            pltpu.sync_copy(pos_hbm.at[:, pl.ds(off, ch)], idx_v)
            pltpu.sync_copy(rows_hbm.at[pl.ds(off, ch)], rows_v)
            copies = [pltpu.make_async_copy(rows_v, out_hbm.at[idx_v.at[k]], sem) for k in range(nk)]
            for cp in copies:
                cp.start()
            for cp in copies:
                cp.wait()

    return scatter(rows, pos)


def _sc_gather_rows(table, idx):
    nc, nw = _sc_workers()
    n = idx.shape[0]
    w = table.shape[1]
    ch = SC_CHUNK
    per_w = n // nw
    assert per_w * nw == n and per_w % ch == 0
    nch = per_w // ch

    @functools.partial(
        pl.kernel, mesh=plsc.VectorSubcoreMesh(core_axis_name="c", subcore_axis_name="s"),
        out_type=jax.ShapeDtypeStruct((n, w), table.dtype),
        scratch_types=[pltpu.VMEM((nch, ch), jnp.int32), pltpu.VMEM((ch, w), table.dtype),
                       pltpu.SemaphoreType.DMA])
    def gather(table_hbm, idx_hbm, out_hbm, idx_v, rows_v, sem):
        wid = lax.axis_index("s") * nc + lax.axis_index("c")
        base = wid * per_w
        pltpu.sync_copy(idx_hbm.at[pl.ds(wid * nch, nch)], idx_v)

        @pl.loop(0, nch)
        def _(ci):
            off = pl.multiple_of(base + ci * ch, ch)
            cp = pltpu.make_async_copy(table_hbm.at[idx_v.at[ci]], rows_v, sem)
            cp.start()
            cp.wait()
            pltpu.sync_copy(rows_v, out_hbm.at[pl.ds(off, ch)])

    return gather(table, idx.reshape(n // ch, ch))


def _expert_kernel(te_ref, nu_ref, xs_ref, wgu_ref, wd_ref, y_ref, wgu_b, wd_b):
    i = pl.program_id(0)

    @pl.when(i < nu_ref[0])
    def _():
        @pl.when((i == 0) | (te_ref[i] != te_ref[jnp.maximum(i - 1, 0)]))
        def _():
            wgu_b[...] = wgu_ref[0].astype(BF16)
            wd_b[...] = wd_ref[0].astype(BF16)

        lo, hi = _unpack_halves(xs_ref[...])
        half = wgu_b.shape[0] // 2
        gu = _dot(lo.astype(BF16), wgu_b[:half, :]) + _dot(hi.astype(BF16), wgu_b[half:, :])
        act = _silu(gu[:, :EXPERT_DIM]) * gu[:, EXPERT_DIM:]
        y_ref[...] = _pack_halves(_dot(act.astype(BF16), wd_b[...]))


def _experts(tile_e, n_used, xs, wgu, wd):
    r, w = xs.shape
    tm = EXPERT_TILE
    d = wgu.shape[1]
    rows = pl.BlockSpec((tm, w), lambda i, te, nu: (jnp.minimum(i, nu[0] - 1), 0))
    return pl.pallas_call(
        _expert_kernel,
        grid_spec=pltpu.PrefetchScalarGridSpec(
            num_scalar_prefetch=2,
            grid=(r // tm,),
            in_specs=[rows,
                      pl.BlockSpec((1, d, 2 * EXPERT_DIM), lambda i, te, nu: (te[i], 0, 0)),
                      pl.BlockSpec((1, EXPERT_DIM, d), lambda i, te, nu: (te[i], 0, 0))],
            out_specs=rows,
            scratch_shapes=[pltpu.VMEM((d, 2 * EXPERT_DIM), BF16), pltpu.VMEM((EXPERT_DIM, d), BF16)]),
        out_shape=jax.ShapeDtypeStruct((r, w), jnp.int32),
        compiler_params=_params("arbitrary"),
        name="experts",
    )(tile_e, n_used, xs, wgu, wd)


def _final_kernel(yk_ref, wk_ref, h_ref, x1_ref, mod_ref, wsgu_ref, wsd_ref, out_ref):
    half = yk_ref.shape[2]
    acc_lo = jnp.zeros((yk_ref.shape[1], half), F32)
    acc_hi = jnp.zeros((yk_ref.shape[1], half), F32)
    for k in range(TOP_K):
        lo, hi = _unpack_halves(yk_ref[k])
        w = wk_ref[:, k:k + 1]
        acc_lo = acc_lo + jnp.where(w != 0.0, w * lo, 0.0)
        acc_hi = acc_hi + jnp.where(w != 0.0, w * hi, 0.0)
    routed = jnp.concatenate([acc_lo, acc_hi], axis=1)
    sgu = _dot(h_ref[...], wsgu_ref[...])
    act = _silu(sgu[:, :SHARED_DIM]) * sgu[:, SHARED_DIM:]
    shared = _dot(act.astype(BF16), wsd_ref[...])
    out_ref[...] = x1_ref[...] + mod_ref[0, 5:6, :] * (routed + shared)


def _final(yk, wk_t, h2, x1, mod, wsgu, wsd, tiles_per_batch):
    t, d = h2.shape
    tm = ROW_TILE
    row = lambda w: pl.BlockSpec((tm, w), lambda i: (i, 0))
    return pl.pallas_call(
        _final_kernel,
        grid=(t // tm,),
        in_specs=[pl.BlockSpec((TOP_K, tm, d // 2), lambda i: (0, i, 0)),
                  row(TOP_K), row(d), row(d),
                  pl.BlockSpec((1, 6, d), lambda i: (i // tiles_per_batch, 0, 0)),
                  _const_spec(wsgu.shape), _const_spec(wsd.shape)],
        out_specs=row(d),
        out_shape=jax.ShapeDtypeStruct((t, d), F32),
        compiler_params=_params("arbitrary"),
        name="final",
    )(yk, wk_t, h2, x1, mod, wsgu, wsd)


def _placement():
    pq = np.zeros((LANES, N_HEADS * HEAD_PAD), np.float32)
    cq = np.zeros((1, N_HEADS * HEAD_PAD), np.float32)
    ck = np.zeros((1, N_HEADS * HEAD_PAD), np.float32)
    for hd in range(N_HEADS):
        for k in range(3):
            pq[k * N_HEADS + hd, hd * HEAD_PAD + AUG0 + k] = 1.0
            ck[0, hd * HEAD_PAD + AUG0 + k] = 1.0
            cq[0, hd * HEAD_PAD + AUG0 + 3 + k] = 1.0
    return jnp.asarray(pq, BF16), jnp.asarray(cq), jnp.asarray(ck)


def kernel(x, c, w_ada, b_ada, norm1_g, w_in, w_dw, b_dw, conv_gn_g, conv_gn_b, w_conv_out,
           q_norm_g, k_norm_g, b_forget, w_attn_out, w_out, norm2_g, w_router, router_bias,
           w_experts_gate_up, w_experts_down, w_shared_gate_up, w_shared_down):
    depth = w_ada.shape[0]
    b, s, d = x.shape

    pq, cq, ck = _placement()
    tri = jnp.asarray(np.tril(np.ones((ROW_TILE, ROW_TILE), np.float32)), BF16)
    grp = np.arange(CONV_DIM) // (CONV_DIM // CONV_GROUPS)
    gg = jnp.asarray((grp[:, None] == grp[None, :]).astype(np.float32) / (CONV_DIM // CONV_GROUPS), BF16)
    c_pad = jnp.pad(c, ((0, SUBLANES - b), (0, 0)))
    tri_pos = jnp.asarray(np.triu(np.ones((POS_TILE, POS_TILE), np.float32)), BF16)
    lstrict = jnp.asarray(np.tril(np.ones((N_EXPERTS, N_EXPERTS), np.float32), -1), BF16)

    for l in range(depth):
        mod = _ada(c_pad, w_ada[l], b_ada[l][None, :])[:b].reshape(b, 6, d)

        bf = jnp.pad(b_forget[l][None, :], ((0, 0), (0, LANES - N_HEADS)))
        gpad = lambda g, sc: jnp.tile(jnp.pad(g * sc, (0, HEAD_PAD - HEAD_DIM)), N_HEADS)[None, :]
        qscale = HEAD_DIM ** -0.5
        u, qa, ka, v, sgc, sga, cum = _inproj(
            x, mod, norm1_g[l][None, :], _inproj_weight(w_in[l]),
            bf, gpad(q_norm_g[l], qscale), gpad(k_norm_g[l], 1.0),
            pq, cq, ck, tri)

        flat = lambda a: a[:, :, :N_HEADS].transpose(0, 2, 1).reshape(-1)
        part = ATTN_TILE // DIAG_PARTS
        cs = flat(cum[:, 0::part])
        ce = flat(cum[:, part - 1::part])
        bound = (1.02 * HEAD_DIM * qscale) * jnp.max(jnp.abs(q_norm_g[l])) * jnp.max(jnp.abs(k_norm_g[l]))
        o = _attention(cs, ce, bound.reshape(1), qa, ka, v)

        wdw = jnp.pad(w_dw[l], ((0, CONV_HALO - CONV_WIDTH), (0, 0)))
        x1 = _merge(u, o, sgc, sga, x, mod, wdw, b_dw[l][None, :], conv_gn_g[l][None, :],
                    conv_gn_b[l][None, :], gg, w_conv_out[l].astype(BF16), w_attn_out[l].astype(BF16),
                    w_out[l].astype(BF16))

        wr = w_router[l].T
        wr_hi = wr.astype(BF16)
        wr_lo = (wr - wr_hi.astype(F32)).astype(BF16)
        h2, h2w, comb_t, cnt = _router(x1, mod, norm2_g[l][None, :], wr_hi, wr_lo, router_bias[l][:, None])

        t = b * s
        n_tiles = (t * TOP_K) // EXPERT_TILE + N_EXPERTS
        posk, wk, gend = _positions(comb_t, cnt, tri_pos, lstrict, n_tiles * EXPERT_TILE)
        seg_end = gend[:, 0].astype(jnp.int32)
        n_used = seg_end[-1:] // EXPERT_TILE
        tile_start = jnp.arange(n_tiles, dtype=jnp.int32) * EXPERT_TILE
        tile_start = jnp.minimum(tile_start, seg_end[-1] - EXPERT_TILE)
        tile_e = jnp.sum((seg_end[None, :] <= tile_start[:, None]).astype(jnp.int32), axis=1)

        xs = _sc_scatter_rows(h2w.reshape(t, d // 2), posk, n_tiles * EXPERT_TILE)
        ys = _experts(tile_e, n_used, xs, w_experts_gate_up[l], w_experts_down[l])
        yk = _sc_gather_rows(ys, posk.reshape(-1)).reshape(TOP_K, t, d // 2)
        out = _final(yk, wk.T, h2.reshape(t, d), x1.reshape(t, d), mod,
                     w_shared_gate_up[l].astype(BF16), w_shared_down[l].astype(BF16), s // ROW_TILE)
        x = out.reshape(b, s, d)
    return x
```

```python
import functools

import numpy as np
import jax
import jax.numpy as jnp
from jax import lax
from jax.experimental import pallas as pl
from jax.experimental.pallas import tpu as pltpu
from jax.experimental.pallas import tpu_sc as plsc

F32 = jnp.float32
BF16 = jnp.bfloat16

CONV_DIM = 512
CONV_WIDTH = 31
CONV_GROUPS = 8
N_HEADS = 8
HEAD_DIM = 64
ATTN_DIM = N_HEADS * HEAD_DIM
N_EXPERTS = 64
TOP_K = 8
N_GROUPS = 8
TOPK_GROUPS = 4
EXPERT_DIM = 256
SHARED_DIM = 256
ROUTED_SCALE = 2.5
EPS = 1e-6

LANES = 128
SUBLANES = 8
HEAD_PAD = LANES
AUG0 = HEAD_DIM
F_LANE0 = HEAD_DIM + 8
VMEM_LIMIT = 56 * 1024 * 1024

ROW_TILE = 512
ATTN_TILE = 1024
DIAG_PARTS = 4
EXPERT_TILE = 1024
WPREP_ROWS = 256
POS_TILE = 1024
SC_CHUNK = 128
CONV_HALO = 32

NEG_BIG = -1e30
EXP_UNDERFLOW = 104.0
FIXED_SHIFT_BOUND = 40.0


def _dot(a, b):
    return jnp.dot(a, b, preferred_element_type=F32)


def _dot_nt(a, b):
    return lax.dot_general(a, b, (((1,), (1,)), ((), ())), preferred_element_type=F32)


def _split2(x):
    hi = x.astype(BF16)
    lo = (x - hi.astype(F32)).astype(BF16)
    return hi, lo


def _split3(x):
    hi = x.astype(BF16)
    r = x - hi.astype(F32)
    mid = r.astype(BF16)
    lo = (r - mid.astype(F32)).astype(BF16)
    return hi, mid, lo


def _pack_halves(v):
    n = v.shape[1] // 2
    lo = lax.bitcast_convert_type(v[:, :n].astype(BF16).astype(F32), jnp.uint32)
    hi = lax.bitcast_convert_type(v[:, n:].astype(BF16).astype(F32), jnp.uint32)
    return lax.bitcast_convert_type(hi | lax.shift_right_logical(lo, jnp.uint32(16)), jnp.int32)


def _unpack_halves(w):
    u = lax.bitcast_convert_type(w, jnp.uint32)
    lo = lax.bitcast_convert_type(lax.shift_left(u, jnp.uint32(16)), F32)
    hi = lax.bitcast_convert_type(u & jnp.uint32(0xFFFF0000), F32)
    return lo, hi


def _sigmoid(x):
    return 1.0 / (1.0 + jnp.exp(-x))


def _silu(x):
    return x * _sigmoid(x)


def _params(*sem):
    return pltpu.CompilerParams(dimension_semantics=sem, vmem_limit_bytes=VMEM_LIMIT)


def _const_spec(shape):
    n = len(shape)
    return pl.BlockSpec(shape, lambda *_: (0,) * n, pipeline_mode=pl.Buffered(1))


def _ada_kernel(c_ref, w_ref, b_ref, o_ref):
    c = c_ref[...]
    a_hi, a_lo = _split2(_silu(c))
    w_hi, w_lo = _split2(w_ref[...])
    o_ref[...] = _dot(a_hi, w_hi) + _dot(a_hi, w_lo) + _dot(a_lo, w_hi) + b_ref[...]


def _ada(c_pad, w_ada, b_ada):
    d = c_pad.shape[1]
    n = w_ada.shape[1]
    return pl.pallas_call(
        _ada_kernel,
        grid=(n // d,),
        in_specs=[_const_spec(c_pad.shape),
                  pl.BlockSpec((d, d), lambda j: (0, j)),
                  pl.BlockSpec((1, d), lambda j: (0, j))],
        out_specs=pl.BlockSpec((c_pad.shape[0], d), lambda j: (0, j)),
        out_shape=jax.ShapeDtypeStruct((c_pad.shape[0], n), F32),
        compiler_params=_params("arbitrary"),
        name="ada",
    )(c_pad, w_ada, b_ada)


def _lane_pieces(x):
    hi, mid, lo = _split3(x)
    return (hi.astype(F32) + pltpu.roll(mid.astype(F32), N_HEADS, 1)
            + pltpu.roll(lo.astype(F32), 2 * N_HEADS, 1)).astype(BF16)


def _head_tile(p, col0, hd):
    per = HEAD_PAD // HEAD_DIM
    g, part = divmod(hd, per)
    blk = p[:, col0 + g * HEAD_PAD:col0 + (g + 1) * HEAD_PAD]
    if part:
        blk = pltpu.roll(blk, HEAD_PAD - part * HEAD_DIM, 1)
    lane = lax.broadcasted_iota(jnp.int32, (1, HEAD_PAD), 1)
    return jnp.where(lane < HEAD_DIM, blk, 0.0)


def _inproj_kernel(x_ref, mod_ref, g1_ref, w_ref, bf_ref, qg_ref, kg_ref, pq_ref, cq_ref, ck_ref, tri_ref,
                   u_ref, qa_ref, ka_ref, v_ref, sgc_ref, sga_ref, cum_ref, carry_ref):
    @pl.when(pl.program_id(1) == 0)
    def _():
        carry_ref[...] = jnp.zeros_like(carry_ref)

    x = x_ref[0]
    ms = jnp.mean(x * x, axis=-1, keepdims=True)
    h = (x * lax.rsqrt(ms + EPS) * g1_ref[...]) * (1.0 + mod_ref[0, 1:2, :]) + mod_ref[0, 0:1, :]
    hb = h.astype(BF16)

    hp = N_HEADS * HEAD_PAD
    o_q = 2 * CONV_DIM
    o_k, o_v = o_q + hp, o_q + 2 * hp
    o_gc = o_v + ATTN_DIM
    o_ga = o_gc + x.shape[1]
    pc = _dot(hb, w_ref[:, :o_q])
    u_ref[0] = (pc[:, :CONV_DIM] * _sigmoid(pc[:, CONV_DIM:])).astype(BF16)

    pq = _dot(hb, w_ref[:, o_q:o_k])
    z = pltpu.roll(pq[:, :HEAD_PAD], HEAD_PAD - F_LANE0, 1) + bf_ref[...]
    lf = jnp.minimum(z, 0.0) - jnp.log1p(jnp.exp(-jnp.abs(z)))
    lane = lax.broadcasted_iota(jnp.int32, (1, LANES), 1)
    psum = _dot(tri_ref[...], _lane_pieces(jnp.where(lane < N_HEADS, lf, 0.0)))
    cum = (psum + pltpu.roll(psum, LANES - N_HEADS, 1) + pltpu.roll(psum, LANES - 2 * N_HEADS, 1)
           + carry_ref[...])
    cum = jnp.where(lane < N_HEADS, cum, 0.0)
    carry_ref[...] = cum[ROW_TILE - 1:ROW_TILE, :]
    cum_ref[0] = cum

    pieces = _lane_pieces(cum)
    placed = _dot(pieces, pq_ref[...])
    addq = placed + cq_ref[...]
    addk = ck_ref[...] - pltpu.roll(placed, 3, 1)

    pk = _dot(hb, w_ref[:, o_k:o_v])
    pv = _dot(hb, w_ref[:, o_v:o_gc])
    inv_hd = 1.0 / HEAD_DIM
    hlane = lax.broadcasted_iota(jnp.int32, (1, HEAD_PAD), 1)
    vone = (hlane == HEAD_DIM).astype(F32)
    for hd in range(N_HEADS):
        sl = slice(hd * HEAD_PAD, (hd + 1) * HEAD_PAD)
        qb = pq[:, sl]
        if hd == 0:
            qb = jnp.where(hlane < HEAD_DIM, qb, 0.0)
        qn = qb * lax.rsqrt(jnp.sum(qb * qb, axis=-1, keepdims=True) * inv_hd + EPS) * qg_ref[:, sl]
        qa_ref[0, hd] = (qn + addq[:, sl]).astype(BF16)
        kb = pk[:, sl]
        kn = kb * lax.rsqrt(jnp.sum(kb * kb, axis=-1, keepdims=True) * inv_hd + EPS) * kg_ref[:, sl]
        ka_ref[0, hd] = (kn + addk[:, sl]).astype(BF16)
        v_ref[0, hd] = (_head_tile(pv, 0, hd) + vone).astype(BF16)

    sgc_ref[0] = _sigmoid(_dot(hb, w_ref[:, o_gc:o_ga])).astype(BF16)
    sga_ref[0] = _sigmoid(_dot(hb, w_ref[:, o_ga:])).astype(BF16)


def _inproj_weight(wi):
    d, n_in = wi.shape
    n_out = 2 * CONV_DIM + 2 * N_HEADS * HEAD_PAD + ATTN_DIM + 2 * d
    rows = WPREP_ROWS
    return pl.pallas_call(
        _wprep_kernel,
        grid=(d // rows,),
        in_specs=[pl.BlockSpec((rows, n_in), lambda i: (i, 0))],
        out_specs=pl.BlockSpec((rows, n_out), lambda i: (i, 0)),
        out_shape=jax.ShapeDtypeStruct((d, n_out), BF16),
        compiler_params=_params("arbitrary"),
        name="wprep",
    )(wi)


def _wprep_kernel(w_ref, o_ref):
    rows, n_in = w_ref.shape
    o_q = 2 * CONV_DIM
    o_f = o_q + 3 * ATTN_DIM
    o_gc = o_f + N_HEADS
    lane = lax.broadcasted_iota(jnp.int32, (1, LANES), 1)
    per = HEAD_PAD // HEAD_DIM

    o_ref[:, :o_q] = w_ref[:, :o_q].astype(BF16)
    out = o_q
    fcols = pltpu.roll(w_ref[:, o_f:o_f + LANES], F_LANE0, 1)
    for blk in range(2):
        src0 = o_q + blk * ATTN_DIM
        for hd in range(N_HEADS):
            g, part = divmod(hd, per)
            tile = w_ref[:, src0 + g * HEAD_PAD:src0 + (g + 1) * HEAD_PAD]
            if part:
                tile = pltpu.roll(tile, HEAD_PAD - part * HEAD_DIM, 1)
            tile = jnp.where(lane < HEAD_DIM, tile, 0.0)
            if blk == 0 and hd == 0:
                tile = jnp.where((lane >= F_LANE0) & (lane < F_LANE0 + N_HEADS), fcols, tile)
            o_ref[:, out:out + HEAD_PAD] = tile.astype(BF16)
            out += HEAD_PAD
    o_ref[:, out:out + ATTN_DIM] = w_ref[:, o_q + 2 * ATTN_DIM:o_f].astype(BF16)
    out += ATTN_DIM
    shift = LANES - (o_gc % LANES)
    for j in range((n_in - o_gc) // LANES):
        a0 = (o_gc // LANES + j) * LANES
        a = pltpu.roll(w_ref[:, a0:a0 + LANES], shift, 1)
        width = min(LANES, n_in - (a0 + LANES))
        b = w_ref[:, a0 + LANES:a0 + LANES + width]
        if width < LANES:
            b = jnp.concatenate([b, jnp.zeros((rows, LANES - width), F32)], axis=1)
        b = pltpu.roll(b, shift, 1)
        o_ref[:, out:out + LANES] = jnp.where(lane < shift, a, b).astype(BF16)
        out += LANES


def _inproj(x, mod, g1, w, bf, qg, kg, pq, cq, ck, tri):
    b, s, d = x.shape
    tm = ROW_TILE
    row = lambda w: pl.BlockSpec((1, tm, w), lambda bi, i: (bi, i, 0))
    head = pl.BlockSpec((1, N_HEADS, tm, HEAD_PAD), lambda bi, i: (bi, 0, i, 0))
    consts = [g1, w, bf, qg, kg, pq, cq, ck, tri]
    return pl.pallas_call(
        _inproj_kernel,
        grid=(b, s // tm),
        in_specs=[row(d), pl.BlockSpec((1, 6, d), lambda bi, i: (bi, 0, 0))]
                 + [_const_spec(a.shape) for a in consts],
        out_specs=[row(CONV_DIM), head, head, head, row(d), row(d), row(LANES)],
        out_shape=[jax.ShapeDtypeStruct((b, s, CONV_DIM), BF16),
                   jax.ShapeDtypeStruct((b, N_HEADS, s, HEAD_PAD), BF16),
                   jax.ShapeDtypeStruct((b, N_HEADS, s, HEAD_PAD), BF16),
                   jax.ShapeDtypeStruct((b, N_HEADS, s, HEAD_PAD), BF16),
                   jax.ShapeDtypeStruct((b, s, d), BF16),
                   jax.ShapeDtypeStruct((b, s, d), BF16),
                   jax.ShapeDtypeStruct((b, s, LANES), F32)],
        scratch_shapes=[pltpu.VMEM((1, LANES), F32)],
        compiler_params=_params("arbitrary", "arbitrary"),
        name="inproj",
    )(x, mod, *consts)


def _attn_kernel(cs_ref, ce_ref, prm_ref, q_ref, k_ref, v_ref, o_ref, acc_ref, m_ref):
    t = ATTN_TILE
    nb = q_ref.shape[2] // t
    base = (pl.program_id(0) * pl.num_programs(1) + pl.program_id(1)) * nb
    bound = prm_ref[0]
    thresh = -(EXP_UNDERFLOW + 2.0 * bound)
    causal = lax.broadcasted_iota(jnp.int32, (t, t), 0) >= lax.broadcasted_iota(jnp.int32, (t, t), 1)

    dp = DIAG_PARTS

    def first_block(i):
        c0 = cs_ref[dp * (base + i)]
        return lax.fori_loop(
            0, i, lambda j, n: n + (c0 - ce_ref[dp * (base + j) + dp - 1] < thresh).astype(jnp.int32), 0)

    def scores(q, j, masked):
        k0 = pl.multiple_of(j * t, t)
        s = _dot_nt(q, k_ref[0, 0, pl.ds(k0, t), :])
        if masked:
            s = jnp.where(causal, s, NEG_BIG)
        return s, v_ref[0, 0, pl.ds(k0, t), :]

    def finish(q0):
        acc = acc_ref[...]
        o_ref[0, pl.ds(q0, t), :] = (acc / acc[:, HEAD_DIM:HEAD_DIM + 1]).astype(BF16)

    def fixed_shift(i, carry):
        q0 = pl.multiple_of(i * t, t)
        q = q_ref[0, 0, pl.ds(q0, t), :]

        acc_ref[...] = jnp.zeros_like(acc_ref)

        def weighted(j, masked):
            s, vb = scores(q, j, masked)
            return _dot(jnp.exp(s).astype(BF16), vb)

        def kv(j, c):
            acc_ref[...] += weighted(j, False)
            return c

        lax.fori_loop(first_block(i), i - 1, kv, 0)
        rp = t // DIAG_PARTS

        @pl.when(i == 0)
        def _():
            for r in range(DIAG_PARTS):
                nk = (r + 1) * rp
                s = _dot_nt(q[r * rp:(r + 1) * rp, :], k_ref[0, 0, pl.ds(q0, nk), :])
                seen = (lax.broadcasted_iota(jnp.int32, (rp, nk), 0) + r * rp
                        >= lax.broadcasted_iota(jnp.int32, (rp, nk), 1))
                p = jnp.exp(jnp.where(seen, s, NEG_BIG)).astype(BF16)
                acc_ref[r * rp:(r + 1) * rp, :] += _dot(p, v_ref[0, 0, pl.ds(q0, nk), :])
            finish(q0)

        @pl.when(i > 0)
        def _():
            nk = t + rp
            for r in range(DIAG_PARTS):
                k0 = pl.multiple_of(q0 - t + r * rp, rp)
                s = _dot_nt(q[r * rp:(r + 1) * rp, :], k_ref[0, 0, pl.ds(k0, nk), :])
                seen = (lax.broadcasted_iota(jnp.int32, (rp, nk), 0) + t
                        >= lax.broadcasted_iota(jnp.int32, (rp, nk), 1))
                p = jnp.exp(jnp.where(seen, s, NEG_BIG)).astype(BF16)
                acc_ref[r * rp:(r + 1) * rp, :] += _dot(p, v_ref[0, 0, pl.ds(k0, nk), :])

            for r in range(1, DIAG_PARTS):
                need = cs_ref[dp * (base + i) + r] - ce_ref[dp * (base + i - 1) + r - 1] >= thresh

                @pl.when(need)
                def _():
                    kp = pl.multiple_of(q0 - t, t)
                    s = _dot_nt(q[r * rp:(r + 1) * rp, :], k_ref[0, 0, pl.ds(kp, r * rp), :])
                    acc_ref[r * rp:(r + 1) * rp, :] += _dot(jnp.exp(s).astype(BF16),
                                                            v_ref[0, 0, pl.ds(kp, r * rp), :])

            finish(q0)

        return carry

    def running_max(i, carry):
        q0 = pl.multiple_of(i * t, t)
        q = q_ref[0, 0, pl.ds(q0, t), :]
        m_ref[...] = jnp.full_like(m_ref, -jnp.inf)
        acc_ref[...] = jnp.zeros_like(acc_ref)

        def step(j, masked):
            s, vb = scores(q, j, masked)
            m_prev = m_ref[...]
            m_new = jnp.maximum(m_prev, jnp.max(s, axis=-1, keepdims=True))
            p = jnp.exp(s - m_new)
            acc_ref[...] = jnp.exp(m_prev - m_new) * acc_ref[...] + _dot(p.astype(BF16), vb)
            m_ref[...] = m_new

        def kv(j, c):
            step(j, False)
            return c

        lax.fori_loop(first_block(i), i, kv, 0)
        step(i, True)
        finish(q0)
        return carry

    @pl.when(bound <= FIXED_SHIFT_BOUND)
    def _():
        lax.fori_loop(0, nb, fixed_shift, 0)

    @pl.when(bound > FIXED_SHIFT_BOUND)
    def _():
        lax.fori_loop(0, nb, running_max, 0)


def _attention(cs, ce, prm, qa, ka, v):
    b, nh, s, hp = qa.shape
    t = ATTN_TILE
    seq = pl.BlockSpec((1, 1, s, hp), lambda bi, hi, *_: (bi, hi, 0, 0))
    return pl.pallas_call(
        _attn_kernel,
        grid_spec=pltpu.PrefetchScalarGridSpec(
            num_scalar_prefetch=3,
            grid=(b, nh),
            in_specs=[seq, seq, seq],
            out_specs=pl.BlockSpec((1, s, hp), lambda bi, hi, *_: (bi, 0, hi)),
            scratch_shapes=[pltpu.VMEM((t, hp), F32), pltpu.VMEM((t, 1), F32)]),
        out_shape=jax.ShapeDtypeStruct((b, s, nh * hp), BF16),
        compiler_params=_params("arbitrary", "arbitrary"),
        name="attn",
    )(cs, ce, prm, qa, ka, v)


def _merge_kernel(u_ref, halo_ref, o_ref, sgc_ref, sga_ref, x_ref, mod_ref, wdw_ref, bdw_ref,
                  gng_ref, gnb_ref, gg_ref, wco_ref, wao_ref, wout_ref, x1_ref, buf_ref):
    tm = ROW_TILE
    halo = halo_ref[0].astype(F32)
    halo = jnp.where(pl.program_id(1) == 0, jnp.zeros_like(halo), halo)
    ucur = u_ref[0].astype(F32)
    for cb in range(CONV_DIM // LANES):
        buf_ref[cb, 0:CONV_HALO, :] = halo[:, cb * LANES:(cb + 1) * LANES]
        buf_ref[cb, CONV_HALO:, :] = ucur[:, cb * LANES:(cb + 1) * LANES]

    base = CONV_HALO - (CONV_WIDTH - 1)
    ys = []
    for cb in range(CONV_DIM // LANES):
        acc = jnp.zeros((tm, LANES), F32)
        for j in range(CONV_WIDTH):
            acc = acc + wdw_ref[j:j + 1, cb * LANES:(cb + 1) * LANES] * buf_ref[cb, base + j:base + j + tm, :]
        ys.append(acc)
    y = jnp.concatenate(ys, axis=1) + bdw_ref[...]

    gg = gg_ref[...]
    y_hi, y_lo = _split2(y)
    dlt = y - (_dot(y_hi, gg) + _dot(y_lo, gg))
    s_hi, s_lo = _split2(dlt * dlt)
    var = _dot(s_hi, gg) + _dot(s_lo, gg)
    yn = dlt * lax.rsqrt(var + EPS) * gng_ref[...] + gnb_ref[...]
    y_conv = _dot(_silu(yn).astype(BF16), wco_ref[...])

    per = HEAD_PAD // HEAD_DIM
    lane = lax.broadcasted_iota(jnp.int32, (1, HEAD_PAD), 1)
    packed = []
    for g in range(N_HEADS // per):
        tile = o_ref[0, :, g * per * HEAD_PAD:(g * per + 1) * HEAD_PAD].astype(F32)
        for part in range(1, per):
            nxt = o_ref[0, :, (g * per + part) * HEAD_PAD:(g * per + part + 1) * HEAD_PAD].astype(F32)
            tile = jnp.where(lane < part * HEAD_DIM, tile, pltpu.roll(nxt, part * HEAD_DIM, 1))
        packed.append(tile)
    y_attn = _dot(jnp.concatenate(packed, axis=1).astype(BF16), wao_ref[...])
    merged = sgc_ref[0].astype(F32) * y_conv + sga_ref[0].astype(F32) * y_attn
    mix = _dot(merged.astype(BF16), wout_ref[...])
    x1_ref[0] = x_ref[0] + mod_ref[0, 2:3, :] * mix


def _merge(u, o, sgc, sga, x, mod, wdw, bdw, gng, gnb, gg, wco, wao, wout):
    b, s, d = x.shape
    tm = ROW_TILE
    per = tm // CONV_HALO
    row = lambda w: pl.BlockSpec((1, tm, w), lambda bi, i: (bi, i, 0))
    consts = [wdw, bdw, gng, gnb, gg, wco, wao, wout]
    return pl.pallas_call(
        _merge_kernel,
        grid=(b, s // tm),
        in_specs=[row(CONV_DIM),
                  pl.BlockSpec((1, CONV_HALO, CONV_DIM),
                               lambda bi, i: (bi, jnp.maximum(i * per - 1, 0), 0)),
                  row(o.shape[2]), row(d), row(d), row(d),
                  pl.BlockSpec((1, 6, d), lambda bi, i: (bi, 0, 0))]
                 + [_const_spec(a.shape) for a in consts],
        out_specs=row(d),
        out_shape=jax.ShapeDtypeStruct((b, s, d), F32),
        scratch_shapes=[pltpu.VMEM((CONV_DIM // LANES, CONV_HALO + tm, LANES), F32)],
        compiler_params=_params("arbitrary", "arbitrary"),
        name="merge",
    )(u, u, o, sgc, sga, x, mod, *consts)


def _router_kernel(x1_ref, mod_ref, g2_ref, wr_hi_ref, wr_lo_ref, rb_ref, h2_ref, h2w_ref, comb_ref, cnt_ref):
    x = x1_ref[0]
    ms = jnp.mean(x * x, axis=-1, keepdims=True)
    h = (x * lax.rsqrt(ms + EPS) * g2_ref[...]) * (1.0 + mod_ref[0, 4:5, :]) + mod_ref[0, 3:4, :]
    h2_ref[0] = h.astype(BF16)
    h2w_ref[0] = _pack_halves(h)

    h_hi, h_lo = _split2(h)
    logits = _dot_nt(wr_hi_ref[...], h_hi) + _dot_nt(wr_hi_ref[...], h_lo) + _dot_nt(wr_lo_ref[...], h_hi)
    scores = _sigmoid(logits)
    biased = scores + rb_ref[...]

    per = N_EXPERTS // N_GROUPS
    rows = lax.broadcasted_iota(jnp.int32, (per, biased.shape[1]), 0)
    gscore = []
    for g in range(N_GROUPS):
        blk = biased[g * per:(g + 1) * per, :]
        top1 = jnp.max(blk, axis=0, keepdims=True)
        first = jnp.min(jnp.where(blk == top1, rows, per), axis=0, keepdims=True)
        top2 = jnp.max(jnp.where(rows == first, -jnp.inf, blk), axis=0, keepdims=True)
        gscore.append(top1 + top2)

    cand = []
    for g in range(N_GROUPS):
        rank = jnp.zeros_like(gscore[g], dtype=jnp.int32)
        for g2 in range(N_GROUPS):
            if g2 == g:
                continue
            ahead = gscore[g2] > gscore[g]
            if g2 < g:
                ahead = ahead | (gscore[g2] == gscore[g])
            rank = rank + ahead.astype(jnp.int32)
        keep = rank < TOPK_GROUPS
        cand.append(jnp.where(keep, biased[g * per:(g + 1) * per, :], -jnp.inf))
    cand = jnp.concatenate(cand, axis=0)

    eidx = lax.broadcasted_iota(jnp.int32, cand.shape, 0)
    work = cand
    for _ in range(TOP_K):
        top = jnp.max(work, axis=0, keepdims=True)
        first = jnp.min(jnp.where(work == top, eidx, N_EXPERTS), axis=0, keepdims=True)
        work = jnp.where(eidx == first, -jnp.inf, work)
    sel = (work != cand) & (cand > -jnp.inf)
    w = jnp.where(sel, scores, 0.0)
    comb = w / jnp.sum(w, axis=0, keepdims=True) * ROUTED_SCALE
    comb_ref[...] = comb

    @pl.when((pl.program_id(0) == 0) & (pl.program_id(1) == 0))
    def _():
        cnt_ref[...] = jnp.zeros_like(cnt_ref)

    hit = jnp.where(comb != 0.0, 1.0, 0.0).astype(BF16)
    cnt_ref[...] += _dot(hit, jnp.ones((hit.shape[1], LANES), BF16))


def _router(x1, mod, g2, wr_hi, wr_lo, rb):
    b, s, d = x1.shape
    tm = ROW_TILE
    nt = s // tm
    return pl.pallas_call(
        _router_kernel,
        grid=(b, nt),
        in_specs=[pl.BlockSpec((1, tm, d), lambda bi, i: (bi, i, 0)),
                  pl.BlockSpec((1, 6, d), lambda bi, i: (bi, 0, 0)),
                  _const_spec(g2.shape), _const_spec(wr_hi.shape), _const_spec(wr_lo.shape),
                  _const_spec(rb.shape)],
        out_specs=[pl.BlockSpec((1, tm, d), lambda bi, i: (bi, i, 0)),
                   pl.BlockSpec((1, tm, d // 2), lambda bi, i: (bi, i, 0)),
                   pl.BlockSpec((N_EXPERTS, tm), lambda bi, i: (0, bi * nt + i)),
                   pl.BlockSpec((N_EXPERTS, LANES), lambda bi, i: (0, 0))],
        out_shape=[jax.ShapeDtypeStruct((b, s, d), BF16),
                   jax.ShapeDtypeStruct((b, s, d // 2), jnp.int32),
                   jax.ShapeDtypeStruct((N_EXPERTS, b * s), F32),
                   jax.ShapeDtypeStruct((N_EXPERTS, LANES), F32)],
        compiler_params=_params("arbitrary", "arbitrary"),
        name="router",
    )(x1, mod, g2, wr_hi, wr_lo, rb)


def _pos_kernel(comb_ref, cnt_ref, tri_ref, lstrict_ref, posk_ref, wk_ref, gend_ref, base_ref, *, spare_row):
    tp = comb_ref.shape[1]
    comb = comb_ref[...]
    sel = comb != 0.0
    selb = jnp.where(sel, 1.0, 0.0).astype(BF16)

    @pl.when(pl.program_id(0) == 0)
    def _():
        seg = jnp.floor((cnt_ref[...] + (EXPERT_TILE - 1.0)) * (1.0 / EXPERT_TILE)) * EXPERT_TILE
        s_hi, s_mid, s_lo = _split3(seg)
        ls = lstrict_ref[...]
        start = _dot(ls, s_hi) + _dot(ls, s_mid) + _dot(ls, s_lo)
        base_ref[...] = start
        gend_ref[...] = start + seg

    rank = _dot(selb, tri_ref[...])
    pos = base_ref[:, 0:1] + rank - 1.0
    base_ref[...] += _dot(selb, jnp.ones((tp, LANES), BF16))
    slot = _dot(lstrict_ref[...], selb)
    rows_p, rows_w = [], []
    for k in range(TOP_K):
        m = sel & (slot == k)
        rows_p.append(jnp.sum(jnp.where(m, pos - spare_row, 0.0), axis=0, keepdims=True) + spare_row)
        rows_w.append(jnp.sum(jnp.where(m, comb, 0.0), axis=0, keepdims=True))
    posk_ref[...] = jnp.concatenate(rows_p, axis=0).astype(jnp.int32)
    wk_ref[...] = jnp.concatenate(rows_w, axis=0)


def _positions(comb_t, cnt, tri, lstrict, n_rows):
    ne, t = comb_t.shape
    tp = POS_TILE
    tok = lambda rows: pl.BlockSpec((rows, tp), lambda i: (0, i))
    return pl.pallas_call(
        functools.partial(_pos_kernel, spare_row=float(n_rows - 1)),
        grid=(t // tp,),
        in_specs=[tok(ne), _const_spec(cnt.shape), _const_spec(tri.shape), _const_spec(lstrict.shape)],
        out_specs=[tok(TOP_K), tok(TOP_K), pl.BlockSpec((ne, LANES), lambda i: (0, 0))],
        out_shape=[jax.ShapeDtypeStruct((TOP_K, t), jnp.int32),
                   jax.ShapeDtypeStruct((TOP_K, t), F32),
                   jax.ShapeDtypeStruct((ne, LANES), F32)],
        scratch_shapes=[pltpu.VMEM((ne, LANES), F32)],
        compiler_params=_params("arbitrary"),
        name="positions",
    )(comb_t, cnt, tri, lstrict)


def _sc_workers():
    info = plsc.get_sparse_core_info()
    return info.num_cores, info.num_cores * info.num_subcores


def _sc_scatter_rows(rows, pos, n_out):
    nc, nw = _sc_workers()
    n, w = rows.shape
    nk = pos.shape[0]
    ch = SC_CHUNK
    per_w = n // nw
    assert per_w * nw == n and per_w % ch == 0

    @functools.partial(
        pl.kernel, mesh=plsc.VectorSubcoreMesh(core_axis_name="c", subcore_axis_name="s"),
        out_type=jax.ShapeDtypeStruct((n_out, w), rows.dtype),
        scratch_types=[pltpu.VMEM((nk, ch), jnp.int32), pltpu.VMEM((ch, w), rows.dtype),
                       pltpu.SemaphoreType.DMA])
    def scatter(rows_hbm, pos_hbm, out_hbm, idx_v, rows_v, sem):
        base = (lax.axis_index("s") * nc + lax.axis_index("c")) * per_w

        @pl.loop(0, per_w // ch)
        def _(ci):
            off = pl.multiple_of(base + ci * ch, ch)
            pltpu.sync_copy(pos_hbm.at[:, pl.ds(off, ch)], idx_v)
            pltpu.sync_copy(rows_hbm.at[pl.ds(off, ch)], rows_v)
            copies = [pltpu.make_async_copy(rows_v, out_hbm.at[idx_v.at[k]], sem) for k in range(nk)]
            for cp in copies:
                cp.start()
            for cp in copies:
                cp.wait()

    return scatter(rows, pos)


def _sc_gather_rows(table, idx):
    nc, nw = _sc_workers()
    n = idx.shape[0]
    w = table.shape[1]
    ch = SC_CHUNK
    per_w = n // nw
    assert per_w * nw == n and per_w % ch == 0
    nch = per_w // ch

    @functools.partial(
        pl.kernel, mesh=plsc.VectorSubcoreMesh(core_axis_name="c", subcore_axis_name="s"),
        out_type=jax.ShapeDtypeStruct((n, w), table.dtype),
        scratch_types=[pltpu.VMEM((nch, ch), jnp.int32), pltpu.VMEM((ch, w), table.dtype),
                       pltpu.SemaphoreType.DMA])
    def gather(table_hbm, idx_hbm, out_hbm, idx_v, rows_v, sem):
        wid = lax.axis_index("s") * nc + lax.axis_index("c")
        base = wid * per_w
        pltpu.sync_copy(idx_hbm.at[pl.ds(wid * nch, nch)], idx_v)

        @pl.loop(0, nch)
        def _(ci):
            off = pl.multiple_of(base + ci * ch, ch)
            cp = pltpu.make_async_copy(table_hbm.at[idx_v.at[ci]], rows_v, sem)
            cp.start()
            cp.wait()
            pltpu.sync_copy(rows_v, out_hbm.at[pl.ds(off, ch)])

    return gather(table, idx.reshape(n // ch, ch))


def _expert_kernel(te_ref, nu_ref, short_ref, xs_ref, wgu_ref, wd_ref, y_ref, wgu_b, wd_b):
    i = pl.program_id(0)

    def run(rows):
        lo, hi = _unpack_halves(xs_ref[:rows, :])
        half = wgu_b.shape[0] // 2
        gu = _dot(lo.astype(BF16), wgu_b[:half, :]) + _dot(hi.astype(BF16), wgu_b[half:, :])
        act = _silu(gu[:, :EXPERT_DIM]) * gu[:, EXPERT_DIM:]
        y_ref[:rows, :] = _pack_halves(_dot(act.astype(BF16), wd_b[...]))

    @pl.when(i < nu_ref[0])
    def _():
        @pl.when((i == 0) | (te_ref[i] != te_ref[jnp.maximum(i - 1, 0)]))
        def _():
            wgu_b[...] = wgu_ref[0].astype(BF16)
            wd_b[...] = wd_ref[0].astype(BF16)

        @pl.when(short_ref[i] == 0)
        def _():
            run(EXPERT_TILE)

        @pl.when(short_ref[i] != 0)
        def _():
            run(EXPERT_TILE // 2)


def _experts(tile_e, n_used, tile_short, xs, wgu, wd):
    r, w = xs.shape
    tm = EXPERT_TILE
    d = wgu.shape[1]
    rows = pl.BlockSpec((tm, w), lambda i, te, nu, sh: (jnp.minimum(i, nu[0] - 1), 0))
    return pl.pallas_call(
        _expert_kernel,
        grid_spec=pltpu.PrefetchScalarGridSpec(
            num_scalar_prefetch=3,
            grid=(r // tm,),
            in_specs=[rows,
                      pl.BlockSpec((1, d, 2 * EXPERT_DIM), lambda i, te, nu, sh: (te[i], 0, 0)),
                      pl.BlockSpec((1, EXPERT_DIM, d), lambda i, te, nu, sh: (te[i], 0, 0))],
            out_specs=rows,
            scratch_shapes=[pltpu.VMEM((d, 2 * EXPERT_DIM), BF16), pltpu.VMEM((EXPERT_DIM, d), BF16)]),
        out_shape=jax.ShapeDtypeStruct((r, w), jnp.int32),
        compiler_params=_params("arbitrary"),
        name="experts",
    )(tile_e, n_used, tile_short, xs, wgu, wd)


def _final_kernel(yk_ref, wk_ref, h_ref, x1_ref, mod_ref, wsgu_ref, wsd_ref, out_ref):
    half = yk_ref.shape[2]
    acc_lo = jnp.zeros((yk_ref.shape[1], half), F32)
    acc_hi = jnp.zeros((yk_ref.shape[1], half), F32)
    for k in range(TOP_K):
        lo, hi = _unpack_halves(yk_ref[k])
        w = wk_ref[:, k:k + 1]
        acc_lo = acc_lo + jnp.where(w != 0.0, w * lo, 0.0)
        acc_hi = acc_hi + jnp.where(w != 0.0, w * hi, 0.0)
    routed = jnp.concatenate([acc_lo, acc_hi], axis=1)
    sgu = _dot(h_ref[...], wsgu_ref[...])
    act = _silu(sgu[:, :SHARED_DIM]) * sgu[:, SHARED_DIM:]
    shared = _dot(act.astype(BF16), wsd_ref[...])
    out_ref[...] = x1_ref[...] + mod_ref[0, 5:6, :] * (routed + shared)


def _final(yk, wk_t, h2, x1, mod, wsgu, wsd, tiles_per_batch):
    t, d = h2.shape
    tm = ROW_TILE
    row = lambda w: pl.BlockSpec((tm, w), lambda i: (i, 0))
    return pl.pallas_call(
        _final_kernel,
        grid=(t // tm,),
        in_specs=[pl.BlockSpec((TOP_K, tm, d // 2), lambda i: (0, i, 0)),
                  row(TOP_K), row(d), row(d),
                  pl.BlockSpec((1, 6, d), lambda i: (i // tiles_per_batch, 0, 0)),
                  _const_spec(wsgu.shape), _const_spec(wsd.shape)],
        out_specs=row(d),
        out_shape=jax.ShapeDtypeStruct((t, d), F32),
        compiler_params=_params("arbitrary"),
        name="final",
    )(yk, wk_t, h2, x1, mod, wsgu, wsd)


def _placement():
    pq = np.zeros((LANES, N_HEADS * HEAD_PAD), np.float32)
    cq = np.zeros((1, N_HEADS * HEAD_PAD), np.float32)
    ck = np.zeros((1, N_HEADS * HEAD_PAD), np.float32)
    for hd in range(N_HEADS):
        for k in range(3):
            pq[k * N_HEADS + hd, hd * HEAD_PAD + AUG0 + k] = 1.0
            ck[0, hd * HEAD_PAD + AUG0 + k] = 1.0
            cq[0, hd * HEAD_PAD + AUG0 + 3 + k] = 1.0
    return jnp.asarray(pq, BF16), jnp.asarray(cq), jnp.asarray(ck)


def kernel(x, c, w_ada, b_ada, norm1_g, w_in, w_dw, b_dw, conv_gn_g, conv_gn_b, w_conv_out,
           q_norm_g, k_norm_g, b_forget, w_attn_out, w_out, norm2_g, w_router, router_bias,
           w_experts_gate_up, w_experts_down, w_shared_gate_up, w_shared_down):
    depth = w_ada.shape[0]
    b, s, d = x.shape

    pq, cq, ck = _placement()
    tri = jnp.asarray(np.tril(np.ones((ROW_TILE, ROW_TILE), np.float32)), BF16)
    grp = np.arange(CONV_DIM) // (CONV_DIM // CONV_GROUPS)
    gg = jnp.asarray((grp[:, None] == grp[None, :]).astype(np.float32) / (CONV_DIM // CONV_GROUPS), BF16)
    c_pad = jnp.pad(c, ((0, SUBLANES - b), (0, 0)))
    tri_pos = jnp.asarray(np.triu(np.ones((POS_TILE, POS_TILE), np.float32)), BF16)
    lstrict = jnp.asarray(np.tril(np.ones((N_EXPERTS, N_EXPERTS), np.float32), -1), BF16)

    for l in range(depth):
        mod = _ada(c_pad, w_ada[l], b_ada[l][None, :])[:b].reshape(b, 6, d)

        bf = jnp.pad(b_forget[l][None, :], ((0, 0), (0, LANES - N_HEADS)))
        gpad = lambda g, sc: jnp.tile(jnp.pad(g * sc, (0, HEAD_PAD - HEAD_DIM)), N_HEADS)[None, :]
        qscale = HEAD_DIM ** -0.5
        u, qa, ka, v, sgc, sga, cum = _inproj(
            x, mod, norm1_g[l][None, :], _inproj_weight(w_in[l]),
            bf, gpad(q_norm_g[l], qscale), gpad(k_norm_g[l], 1.0),
            pq, cq, ck, tri)

        flat = lambda a: a[:, :, :N_HEADS].transpose(0, 2, 1).reshape(-1)
        part = ATTN_TILE // DIAG_PARTS
        cs = flat(cum[:, 0::part])
        ce = flat(cum[:, part - 1::part])
        bound = (1.02 * HEAD_DIM * qscale) * jnp.max(jnp.abs(q_norm_g[l])) * jnp.max(jnp.abs(k_norm_g[l]))
        o = _attention(cs, ce, bound.reshape(1), qa, ka, v)

        wdw = jnp.pad(w_dw[l], ((0, CONV_HALO - CONV_WIDTH), (0, 0)))
        x1 = _merge(u, o, sgc, sga, x, mod, wdw, b_dw[l][None, :], conv_gn_g[l][None, :],
                    conv_gn_b[l][None, :], gg, w_conv_out[l].astype(BF16), w_attn_out[l].astype(BF16),
                    w_out[l].astype(BF16))

        wr = w_router[l].T
        wr_hi = wr.astype(BF16)
        wr_lo = (wr - wr_hi.astype(F32)).astype(BF16)
        h2, h2w, comb_t, cnt = _router(x1, mod, norm2_g[l][None, :], wr_hi, wr_lo, router_bias[l][:, None])

        t = b * s
        n_tiles = (t * TOP_K) // EXPERT_TILE + N_EXPERTS
        posk, wk, gend = _positions(comb_t, cnt, tri_pos, lstrict, n_tiles * EXPERT_TILE)
        seg_end = gend[:, 0].astype(jnp.int32)
        n_used = seg_end[-1:] // EXPERT_TILE
        tile_start = jnp.arange(n_tiles, dtype=jnp.int32) * EXPERT_TILE
        tile_start = jnp.minimum(tile_start, seg_end[-1] - EXPERT_TILE)
        tile_e = jnp.sum((seg_end[None, :] <= tile_start[:, None]).astype(jnp.int32), axis=1)
        seg_start = jnp.concatenate([jnp.zeros((1,), jnp.int32), seg_end[:-1]])
        real_end = seg_start + cnt[:, 0].astype(jnp.int32)
        tile_short = (jnp.take(real_end, jnp.minimum(tile_e, N_EXPERTS - 1)) - tile_start
                      <= EXPERT_TILE // 2).astype(jnp.int32)

        xs = _sc_scatter_rows(h2w.reshape(t, d // 2), posk, n_tiles * EXPERT_TILE)
        ys = _experts(tile_e, n_used, tile_short, xs, w_experts_gate_up[l], w_experts_down[l])
        yk = _sc_gather_rows(ys, posk.reshape(-1)).reshape(TOP_K, t, d // 2)
        out = _final(yk, wk.T, h2.reshape(t, d), x1.reshape(t, d), mod,
                     w_shared_gate_up[l].astype(BF16), w_shared_down[l].astype(BF16), s // ROW_TILE)
        x = out.reshape(b, s, d)
    return x
```

```python
import functools

import numpy as np
import jax
import jax.numpy as jnp
from jax import lax
from jax.experimental import pallas as pl
from jax.experimental.pallas import tpu as pltpu
from jax.experimental.pallas import tpu_sc as plsc

F32 = jnp.float32
BF16 = jnp.bfloat16

CONV_DIM = 512
CONV_WIDTH = 31
CONV_GROUPS = 8
N_HEADS = 8
HEAD_DIM = 64
ATTN_DIM = N_HEADS * HEAD_DIM
N_EXPERTS = 64
TOP_K = 8
N_GROUPS = 8
TOPK_GROUPS = 4
EXPERT_DIM = 256
SHARED_DIM = 256
ROUTED_SCALE = 2.5
EPS = 1e-6

LANES = 128
SUBLANES = 8
HEAD_PAD = LANES
AUG0 = HEAD_DIM
F_LANE0 = HEAD_DIM + 8
VMEM_LIMIT = 56 * 1024 * 1024

ROW_TILE = 512
WIDE_TILE = 1024
ATTN_TILE = 1024
DIAG_PARTS = 4
EXPERT_TILE = 2048
WPREP_ROWS = 256
POS_TILE = 1024
SC_CHUNK = 128
CONV_HALO = 32

NEG_BIG = -1e30
EXP_UNDERFLOW = 104.0
FIXED_SHIFT_BOUND = 40.0


def _dot(a, b):
    return jnp.dot(a, b, preferred_element_type=F32)


def _dot_nt(a, b):
    return lax.dot_general(a, b, (((1,), (1,)), ((), ())), preferred_element_type=F32)


def _split2(x):
    hi = x.astype(BF16)
    lo = (x - hi.astype(F32)).astype(BF16)
    return hi, lo


def _split3(x):
    hi = x.astype(BF16)
    r = x - hi.astype(F32)
    mid = r.astype(BF16)
    lo = (r - mid.astype(F32)).astype(BF16)
    return hi, mid, lo


def _pack_halves(v):
    n = v.shape[1] // 2
    lo = lax.bitcast_convert_type(v[:, :n].astype(BF16).astype(F32), jnp.uint32)
    hi = lax.bitcast_convert_type(v[:, n:].astype(BF16).astype(F32), jnp.uint32)
    return lax.bitcast_convert_type(hi | lax.shift_right_logical(lo, jnp.uint32(16)), jnp.int32)


def _unpack_halves(w):
    u = lax.bitcast_convert_type(w, jnp.uint32)
    lo = lax.bitcast_convert_type(lax.shift_left(u, jnp.uint32(16)), F32)
    hi = lax.bitcast_convert_type(u & jnp.uint32(0xFFFF0000), F32)
    return lo, hi


def _sigmoid(x):
    return 1.0 / (1.0 + jnp.exp(-x))


def _silu(x):
    return x * _sigmoid(x)


def _params(*sem):
    return pltpu.CompilerParams(dimension_semantics=sem, vmem_limit_bytes=VMEM_LIMIT)


def _const_spec(shape):
    n = len(shape)
    return pl.BlockSpec(shape, lambda *_: (0,) * n, pipeline_mode=pl.Buffered(1))


def _ada_kernel(c_ref, w_ref, b_ref, o_ref):
    c = c_ref[...]
    a_hi, a_lo = _split2(_silu(c))
    w_hi, w_lo = _split2(w_ref[...])
    o_ref[...] = _dot(a_hi, w_hi) + _dot(a_hi, w_lo) + _dot(a_lo, w_hi) + b_ref[...]


def _ada(c_pad, w_ada, b_ada):
    d = c_pad.shape[1]
    n = w_ada.shape[1]
    return pl.pallas_call(
        _ada_kernel,
        grid=(n // d,),
        in_specs=[_const_spec(c_pad.shape),
                  pl.BlockSpec((d, d), lambda j: (0, j)),
                  pl.BlockSpec((1, d), lambda j: (0, j))],
        out_specs=pl.BlockSpec((c_pad.shape[0], d), lambda j: (0, j)),
        out_shape=jax.ShapeDtypeStruct((c_pad.shape[0], n), F32),
        compiler_params=_params("arbitrary"),
        name="ada",
    )(c_pad, w_ada, b_ada)


def _lane_pieces(x):
    hi, mid, lo = _split3(x)
    return (hi.astype(F32) + pltpu.roll(mid.astype(F32), N_HEADS, 1)
            + pltpu.roll(lo.astype(F32), 2 * N_HEADS, 1)).astype(BF16)


def _head_tile(p, col0, hd):
    per = HEAD_PAD // HEAD_DIM
    g, part = divmod(hd, per)
    blk = p[:, col0 + g * HEAD_PAD:col0 + (g + 1) * HEAD_PAD]
    if part:
        blk = pltpu.roll(blk, HEAD_PAD - part * HEAD_DIM, 1)
    lane = lax.broadcasted_iota(jnp.int32, (1, HEAD_PAD), 1)
    return jnp.where(lane < HEAD_DIM, blk, 0.0)


def _inproj_kernel(x_ref, mod_ref, g1_ref, w_ref, bf_ref, qg_ref, kg_ref, pq_ref, cq_ref, ck_ref, tri_ref,
                   u_ref, qa_ref, ka_ref, v_ref, sgc_ref, sga_ref, cum_ref, carry_ref):
    @pl.when(pl.program_id(1) == 0)
    def _():
        carry_ref[...] = jnp.zeros_like(carry_ref)

    x = x_ref[0]
    ms = jnp.mean(x * x, axis=-1, keepdims=True)
    h = (x * lax.rsqrt(ms + EPS) * g1_ref[...]) * (1.0 + mod_ref[0, 1:2, :]) + mod_ref[0, 0:1, :]
    hb = h.astype(BF16)

    hp = N_HEADS * HEAD_PAD
    o_q = 2 * CONV_DIM
    o_k, o_v = o_q + hp, o_q + 2 * hp
    o_gc = o_v + ATTN_DIM
    o_ga = o_gc + x.shape[1]
    pc = _dot(hb, w_ref[:, :o_q])
    u_ref[0] = (pc[:, :CONV_DIM] * _sigmoid(pc[:, CONV_DIM:])).astype(BF16)

    pq = _dot(hb, w_ref[:, o_q:o_k])
    z = pltpu.roll(pq[:, :HEAD_PAD], HEAD_PAD - F_LANE0, 1) + bf_ref[...]
    lf = jnp.minimum(z, 0.0) - jnp.log1p(jnp.exp(-jnp.abs(z)))
    lane = lax.broadcasted_iota(jnp.int32, (1, LANES), 1)
    psum = _dot(tri_ref[...], _lane_pieces(jnp.where(lane < N_HEADS, lf, 0.0)))
    cum = (psum + pltpu.roll(psum, LANES - N_HEADS, 1) + pltpu.roll(psum, LANES - 2 * N_HEADS, 1)
           + carry_ref[...])
    cum = jnp.where(lane < N_HEADS, cum, 0.0)
    carry_ref[...] = cum[ROW_TILE - 1:ROW_TILE, :]
    cum_ref[0] = cum

    pieces = _lane_pieces(cum)
    placed = _dot(pieces, pq_ref[...])
    addq = placed + cq_ref[...]
    addk = ck_ref[...] - pltpu.roll(placed, 3, 1)

    pk = _dot(hb, w_ref[:, o_k:o_v])
    pv = _dot(hb, w_ref[:, o_v:o_gc])
    inv_hd = 1.0 / HEAD_DIM
    hlane = lax.broadcasted_iota(jnp.int32, (1, HEAD_PAD), 1)
    vone = (hlane == HEAD_DIM).astype(F32)
    for hd in range(N_HEADS):
        sl = slice(hd * HEAD_PAD, (hd + 1) * HEAD_PAD)
        qb = pq[:, sl]
        if hd == 0:
            qb = jnp.where(hlane < HEAD_DIM, qb, 0.0)
        qn = qb * lax.rsqrt(jnp.sum(qb * qb, axis=-1, keepdims=True) * inv_hd + EPS) * qg_ref[:, sl]
        qa_ref[0, hd] = (qn + addq[:, sl]).astype(BF16)
        kb = pk[:, sl]
        kn = kb * lax.rsqrt(jnp.sum(kb * kb, axis=-1, keepdims=True) * inv_hd + EPS) * kg_ref[:, sl]
        ka_ref[0, hd] = (kn + addk[:, sl]).astype(BF16)
        v_ref[0, hd] = (_head_tile(pv, 0, hd) + vone).astype(BF16)

    sgc_ref[0] = _sigmoid(_dot(hb, w_ref[:, o_gc:o_ga])).astype(BF16)
    sga_ref[0] = _sigmoid(_dot(hb, w_ref[:, o_ga:])).astype(BF16)


def _inproj_weight(wi):
    d, n_in = wi.shape
    n_out = 2 * CONV_DIM + 2 * N_HEADS * HEAD_PAD + ATTN_DIM + 2 * d
    rows = WPREP_ROWS
    return pl.pallas_call(
        _wprep_kernel,
        grid=(d // rows,),
        in_specs=[pl.BlockSpec((rows, n_in), lambda i: (i, 0))],
        out_specs=pl.BlockSpec((rows, n_out), lambda i: (i, 0)),
        out_shape=jax.ShapeDtypeStruct((d, n_out), BF16),
        compiler_params=_params("arbitrary"),
        name="wprep",
    )(wi)


def _wprep_kernel(w_ref, o_ref):
    rows, n_in = w_ref.shape
    o_q = 2 * CONV_DIM
    o_f = o_q + 3 * ATTN_DIM
    o_gc = o_f + N_HEADS
    lane = lax.broadcasted_iota(jnp.int32, (1, LANES), 1)
    per = HEAD_PAD // HEAD_DIM

    o_ref[:, :o_q] = w_ref[:, :o_q].astype(BF16)
    out = o_q
    fcols = pltpu.roll(w_ref[:, o_f:o_f + LANES], F_LANE0, 1)
    for blk in range(2):
        src0 = o_q + blk * ATTN_DIM
        for hd in range(N_HEADS):
            g, part = divmod(hd, per)
            tile = w_ref[:, src0 + g * HEAD_PAD:src0 + (g + 1) * HEAD_PAD]
            if part:
                tile = pltpu.roll(tile, HEAD_PAD - part * HEAD_DIM, 1)
            tile = jnp.where(lane < HEAD_DIM, tile, 0.0)
            if blk == 0 and hd == 0:
                tile = jnp.where((lane >= F_LANE0) & (lane < F_LANE0 + N_HEADS), fcols, tile)
            o_ref[:, out:out + HEAD_PAD] = tile.astype(BF16)
            out += HEAD_PAD
    o_ref[:, out:out + ATTN_DIM] = w_ref[:, o_q + 2 * ATTN_DIM:o_f].astype(BF16)
    out += ATTN_DIM
    shift = LANES - (o_gc % LANES)
    for j in range((n_in - o_gc) // LANES):
        a0 = (o_gc // LANES + j) * LANES
        a = pltpu.roll(w_ref[:, a0:a0 + LANES], shift, 1)
        width = min(LANES, n_in - (a0 + LANES))
        b = w_ref[:, a0 + LANES:a0 + LANES + width]
        if width < LANES:
            b = jnp.concatenate([b, jnp.zeros((rows, LANES - width), F32)], axis=1)
        b = pltpu.roll(b, shift, 1)
        o_ref[:, out:out + LANES] = jnp.where(lane < shift, a, b).astype(BF16)
        out += LANES


def _inproj(x, mod, g1, w, bf, qg, kg, pq, cq, ck, tri):
    b, s, d = x.shape
    tm = ROW_TILE
    row = lambda w: pl.BlockSpec((1, tm, w), lambda bi, i: (bi, i, 0))
    head = pl.BlockSpec((1, N_HEADS, tm, HEAD_PAD), lambda bi, i: (bi, 0, i, 0))
    consts = [g1, w, bf, qg, kg, pq, cq, ck, tri]
    return pl.pallas_call(
        _inproj_kernel,
        grid=(b, s // tm),
        in_specs=[row(d), pl.BlockSpec((1, 6, d), lambda bi, i: (bi, 0, 0))]
                 + [_const_spec(a.shape) for a in consts],
        out_specs=[row(CONV_DIM), head, head, head, row(d), row(d), row(LANES)],
        out_shape=[jax.ShapeDtypeStruct((b, s, CONV_DIM), BF16),
                   jax.ShapeDtypeStruct((b, N_HEADS, s, HEAD_PAD), BF16),
                   jax.ShapeDtypeStruct((b, N_HEADS, s, HEAD_PAD), BF16),
                   jax.ShapeDtypeStruct((b, N_HEADS, s, HEAD_PAD), BF16),
                   jax.ShapeDtypeStruct((b, s, d), BF16),
                   jax.ShapeDtypeStruct((b, s, d), BF16),
                   jax.ShapeDtypeStruct((b, s, LANES), F32)],
        scratch_shapes=[pltpu.VMEM((1, LANES), F32)],
        compiler_params=_params("arbitrary", "arbitrary"),
        name="inproj",
    )(x, mod, *consts)


def _attn_kernel(cs_ref, ce_ref, prm_ref, q_ref, k_ref, v_ref, o_ref, acc_ref, m_ref):
    t = ATTN_TILE
    nb = q_ref.shape[2] // t
    base = (pl.program_id(0) * pl.num_programs(1) + pl.program_id(1)) * nb
    bound = prm_ref[0]
    thresh = -(EXP_UNDERFLOW + 2.0 * bound)
    causal = lax.broadcasted_iota(jnp.int32, (t, t), 0) >= lax.broadcasted_iota(jnp.int32, (t, t), 1)

    dp = DIAG_PARTS

    def first_block(i):
        c0 = cs_ref[dp * (base + i)]
        return lax.fori_loop(
            0, i, lambda j, n: n + (c0 - ce_ref[dp * (base + j) + dp - 1] < thresh).astype(jnp.int32), 0)

    def scores(q, j, masked):
        k0 = pl.multiple_of(j * t, t)
        s = _dot_nt(q, k_ref[0, 0, pl.ds(k0, t), :])
        if masked:
            s = jnp.where(causal, s, NEG_BIG)
        return s, v_ref[0, 0, pl.ds(k0, t), :]

    def finish(q0):
        acc = acc_ref[...]
        o_ref[0, pl.ds(q0, t), :] = (acc / acc[:, HEAD_DIM:HEAD_DIM + 1]).astype(BF16)

    def fixed_shift(i, carry):
        q0 = pl.multiple_of(i * t, t)
        q = q_ref[0, 0, pl.ds(q0, t), :]

        acc_ref[...] = jnp.zeros_like(acc_ref)

        def weighted(j, masked):
            s, vb = scores(q, j, masked)
            return _dot(jnp.exp(s).astype(BF16), vb)

        def kv(j, c):
            acc_ref[...] += weighted(j, False)
            return c

        lax.fori_loop(first_block(i), i - 1, kv, 0)
        rp = t // DIAG_PARTS

        @pl.when(i == 0)
        def _():
            for r in range(DIAG_PARTS):
                nk = (r + 1) * rp
                s = _dot_nt(q[r * rp:(r + 1) * rp, :], k_ref[0, 0, pl.ds(q0, nk), :])
                seen = (lax.broadcasted_iota(jnp.int32, (rp, nk), 0) + r * rp
                        >= lax.broadcasted_iota(jnp.int32, (rp, nk), 1))
                p = jnp.exp(jnp.where(seen, s, NEG_BIG)).astype(BF16)
                acc_ref[r * rp:(r + 1) * rp, :] += _dot(p, v_ref[0, 0, pl.ds(q0, nk), :])
            finish(q0)

        @pl.when(i > 0)
        def _():
            nk = t + rp
            for r in range(DIAG_PARTS):
                k0 = pl.multiple_of(q0 - t + r * rp, rp)
                s = _dot_nt(q[r * rp:(r + 1) * rp, :], k_ref[0, 0, pl.ds(k0, nk), :])
                seen = (lax.broadcasted_iota(jnp.int32, (rp, nk), 0) + t
                        >= lax.broadcasted_iota(jnp.int32, (rp, nk), 1))
                p = jnp.exp(jnp.where(seen, s, NEG_BIG)).astype(BF16)
                acc_ref[r * rp:(r + 1) * rp, :] += _dot(p, v_ref[0, 0, pl.ds(k0, nk), :])

            for r in range(1, DIAG_PARTS):
                need = cs_ref[dp * (base + i) + r] - ce_ref[dp * (base + i - 1) + r - 1] >= thresh

                @pl.when(need)
                def _():
                    kp = pl.multiple_of(q0 - t, t)
                    s = _dot_nt(q[r * rp:(r + 1) * rp, :], k_ref[0, 0, pl.ds(kp, r * rp), :])
                    acc_ref[r * rp:(r + 1) * rp, :] += _dot(jnp.exp(s).astype(BF16),
                                                            v_ref[0, 0, pl.ds(kp, r * rp), :])

            finish(q0)

        return carry

    def running_max(i, carry):
        q0 = pl.multiple_of(i * t, t)
        q = q_ref[0, 0, pl.ds(q0, t), :]
        m_ref[...] = jnp.full_like(m_ref, -jnp.inf)
        acc_ref[...] = jnp.zeros_like(acc_ref)

        def step(j, masked):
            s, vb = scores(q, j, masked)
            m_prev = m_ref[...]
            m_new = jnp.maximum(m_prev, jnp.max(s, axis=-1, keepdims=True))
            p = jnp.exp(s - m_new)
            acc_ref[...] = jnp.exp(m_prev - m_new) * acc_ref[...] + _dot(p.astype(BF16), vb)
            m_ref[...] = m_new

        def kv(j, c):
            step(j, False)
            return c

        lax.fori_loop(first_block(i), i, kv, 0)
        step(i, True)
        finish(q0)
        return carry

    @pl.when(bound <= FIXED_SHIFT_BOUND)
    def _():
        lax.fori_loop(0, nb, fixed_shift, 0)

    @pl.when(bound > FIXED_SHIFT_BOUND)
    def _():
        lax.fori_loop(0, nb, running_max, 0)


def _attention(cs, ce, prm, qa, ka, v):
    b, nh, s, hp = qa.shape
    t = ATTN_TILE
    seq = pl.BlockSpec((1, 1, s, hp), lambda bi, hi, *_: (bi, hi, 0, 0))
    return pl.pallas_call(
        _attn_kernel,
        grid_spec=pltpu.PrefetchScalarGridSpec(
            num_scalar_prefetch=3,
            grid=(b, nh),
            in_specs=[seq, seq, seq],
            out_specs=pl.BlockSpec((1, s, hp), lambda bi, hi, *_: (bi, 0, hi)),
            scratch_shapes=[pltpu.VMEM((t, hp), F32), pltpu.VMEM((t, 1), F32)]),
        out_shape=jax.ShapeDtypeStruct((b, s, nh * hp), BF16),
        compiler_params=_params("arbitrary", "arbitrary"),
        name="attn",
    )(cs, ce, prm, qa, ka, v)


def _merge_kernel(u_ref, halo_ref, o_ref, sgc_ref, sga_ref, x_ref, mod_ref, wdw_ref, bdw_ref,
                  gng_ref, gnb_ref, gg_ref, wco_ref, wao_ref, wout_ref, x1_ref, buf_ref):
    tm = WIDE_TILE
    halo = halo_ref[0].astype(F32)
    halo = jnp.where(pl.program_id(1) == 0, jnp.zeros_like(halo), halo)
    ucur = u_ref[0].astype(F32)
    for cb in range(CONV_DIM // LANES):
        buf_ref[cb, 0:CONV_HALO, :] = halo[:, cb * LANES:(cb + 1) * LANES]
        buf_ref[cb, CONV_HALO:, :] = ucur[:, cb * LANES:(cb + 1) * LANES]

    base = CONV_HALO - (CONV_WIDTH - 1)
    ys = []
    for cb in range(CONV_DIM // LANES):
        acc = jnp.zeros((tm, LANES), F32)
        for j in range(CONV_WIDTH):
            acc = acc + wdw_ref[j:j + 1, cb * LANES:(cb + 1) * LANES] * buf_ref[cb, base + j:base + j + tm, :]
        ys.append(acc)
    y = jnp.concatenate(ys, axis=1) + bdw_ref[...]

    gg = gg_ref[...]
    y_hi, y_lo = _split2(y)
    dlt = y - (_dot(y_hi, gg) + _dot(y_lo, gg))
    s_hi, s_lo = _split2(dlt * dlt)
    var = _dot(s_hi, gg) + _dot(s_lo, gg)
    yn = dlt * lax.rsqrt(var + EPS) * gng_ref[...] + gnb_ref[...]
    y_conv = _dot(_silu(yn).astype(BF16), wco_ref[...])

    per = HEAD_PAD // HEAD_DIM
    lane = lax.broadcasted_iota(jnp.int32, (1, HEAD_PAD), 1)
    packed = []
    for g in range(N_HEADS // per):
        tile = o_ref[0, :, g * per * HEAD_PAD:(g * per + 1) * HEAD_PAD].astype(F32)
        for part in range(1, per):
            nxt = o_ref[0, :, (g * per + part) * HEAD_PAD:(g * per + part + 1) * HEAD_PAD].astype(F32)
            tile = jnp.where(lane < part * HEAD_DIM, tile, pltpu.roll(nxt, part * HEAD_DIM, 1))
        packed.append(tile)
    y_attn = _dot(jnp.concatenate(packed, axis=1).astype(BF16), wao_ref[...])
    merged = sgc_ref[0].astype(F32) * y_conv + sga_ref[0].astype(F32) * y_attn
    mix = _dot(merged.astype(BF16), wout_ref[...])
    x1_ref[0] = x_ref[0] + mod_ref[0, 2:3, :] * mix


def _merge(u, o, sgc, sga, x, mod, wdw, bdw, gng, gnb, gg, wco, wao, wout):
    b, s, d = x.shape
    tm = WIDE_TILE
    per = tm // CONV_HALO
    row = lambda w: pl.BlockSpec((1, tm, w), lambda bi, i: (bi, i, 0))
    consts = [wdw, bdw, gng, gnb, gg, wco, wao, wout]
    return pl.pallas_call(
        _merge_kernel,
        grid=(b, s // tm),
        in_specs=[row(CONV_DIM),
                  pl.BlockSpec((1, CONV_HALO, CONV_DIM),
                               lambda bi, i: (bi, jnp.maximum(i * per - 1, 0), 0)),
                  row(o.shape[2]), row(d), row(d), row(d),
                  pl.BlockSpec((1, 6, d), lambda bi, i: (bi, 0, 0))]
                 + [_const_spec(a.shape) for a in consts],
        out_specs=row(d),
        out_shape=jax.ShapeDtypeStruct((b, s, d), F32),
        scratch_shapes=[pltpu.VMEM((CONV_DIM // LANES, CONV_HALO + tm, LANES), F32)],
        compiler_params=_params("arbitrary", "arbitrary"),
        name="merge",
    )(u, u, o, sgc, sga, x, mod, *consts)


def _router_kernel(x1_ref, mod_ref, g2_ref, wr_hi_ref, wr_lo_ref, rb_ref, h2_ref, h2w_ref, comb_ref, cnt_ref):
    x = x1_ref[0]
    ms = jnp.mean(x * x, axis=-1, keepdims=True)
    h = (x * lax.rsqrt(ms + EPS) * g2_ref[...]) * (1.0 + mod_ref[0, 4:5, :]) + mod_ref[0, 3:4, :]
    h2_ref[0] = h.astype(BF16)
    h2w_ref[0] = _pack_halves(h)

    h_hi, h_lo = _split2(h)
    logits = _dot_nt(wr_hi_ref[...], h_hi) + _dot_nt(wr_hi_ref[...], h_lo) + _dot_nt(wr_lo_ref[...], h_hi)
    scores = _sigmoid(logits)
    biased = scores + rb_ref[...]

    per = N_EXPERTS // N_GROUPS
    rows = lax.broadcasted_iota(jnp.int32, (per, biased.shape[1]), 0)
    gscore = []
    for g in range(N_GROUPS):
        blk = biased[g * per:(g + 1) * per, :]
        top1 = jnp.max(blk, axis=0, keepdims=True)
        first = jnp.min(jnp.where(blk == top1, rows, per), axis=0, keepdims=True)
        top2 = jnp.max(jnp.where(rows == first, -jnp.inf, blk), axis=0, keepdims=True)
        gscore.append(top1 + top2)

    cand = []
    for g in range(N_GROUPS):
        rank = jnp.zeros_like(gscore[g], dtype=jnp.int32)
        for g2 in range(N_GROUPS):
            if g2 == g:
                continue
            ahead = gscore[g2] > gscore[g]
            if g2 < g:
                ahead = ahead | (gscore[g2] == gscore[g])
            rank = rank + ahead.astype(jnp.int32)
        keep = rank < TOPK_GROUPS
        cand.append(jnp.where(keep, biased[g * per:(g + 1) * per, :], -jnp.inf))
    cand = jnp.concatenate(cand, axis=0)

    eidx = lax.broadcasted_iota(jnp.int32, cand.shape, 0)
    work = cand
    for _ in range(TOP_K):
        top = jnp.max(work, axis=0, keepdims=True)
        first = jnp.min(jnp.where(work == top, eidx, N_EXPERTS), axis=0, keepdims=True)
        work = jnp.where(eidx == first, -jnp.inf, work)
    sel = (work != cand) & (cand > -jnp.inf)
    w = jnp.where(sel, scores, 0.0)
    comb = w / jnp.sum(w, axis=0, keepdims=True) * ROUTED_SCALE
    comb_ref[...] = comb

    @pl.when((pl.program_id(0) == 0) & (pl.program_id(1) == 0))
    def _():
        cnt_ref[...] = jnp.zeros_like(cnt_ref)

    hit = jnp.where(comb != 0.0, 1.0, 0.0).astype(BF16)
    cnt_ref[...] += _dot(hit, jnp.ones((hit.shape[1], LANES), BF16))


def _router(x1, mod, g2, wr_hi, wr_lo, rb):
    b, s, d = x1.shape
    tm = WIDE_TILE
    nt = s // tm
    return pl.pallas_call(
        _router_kernel,
        grid=(b, nt),
        in_specs=[pl.BlockSpec((1, tm, d), lambda bi, i: (bi, i, 0)),
                  pl.BlockSpec((1, 6, d), lambda bi, i: (bi, 0, 0)),
                  _const_spec(g2.shape), _const_spec(wr_hi.shape), _const_spec(wr_lo.shape),
                  _const_spec(rb.shape)],
        out_specs=[pl.BlockSpec((1, tm, d), lambda bi, i: (bi, i, 0)),
                   pl.BlockSpec((1, tm, d // 2), lambda bi, i: (bi, i, 0)),
                   pl.BlockSpec((N_EXPERTS, tm), lambda bi, i: (0, bi * nt + i)),
                   pl.BlockSpec((N_EXPERTS, LANES), lambda bi, i: (0, 0))],
        out_shape=[jax.ShapeDtypeStruct((b, s, d), BF16),
                   jax.ShapeDtypeStruct((b, s, d // 2), jnp.int32),
                   jax.ShapeDtypeStruct((N_EXPERTS, b * s), F32),
                   jax.ShapeDtypeStruct((N_EXPERTS, LANES), F32)],
        compiler_params=_params("arbitrary", "arbitrary"),
        name="router",
    )(x1, mod, g2, wr_hi, wr_lo, rb)


def _pos_kernel(comb_ref, cnt_ref, tri_ref, lstrict_ref, posk_ref, wk_ref, gend_ref, base_ref, *, spare_row):
    tp = comb_ref.shape[1]
    comb = comb_ref[...]
    sel = comb != 0.0
    selb = jnp.where(sel, 1.0, 0.0).astype(BF16)

    @pl.when(pl.program_id(0) == 0)
    def _():
        seg = jnp.floor((cnt_ref[...] + (EXPERT_TILE - 1.0)) * (1.0 / EXPERT_TILE)) * EXPERT_TILE
        s_hi, s_mid, s_lo = _split3(seg)
        ls = lstrict_ref[...]
        start = _dot(ls, s_hi) + _dot(ls, s_mid) + _dot(ls, s_lo)
        base_ref[...] = start
        gend_ref[...] = start + seg

    rank = _dot(selb, tri_ref[...])
    pos = base_ref[:, 0:1] + rank - 1.0
    base_ref[...] += _dot(selb, jnp.ones((tp, LANES), BF16))
    slot = _dot(lstrict_ref[...], selb)
    rows_p, rows_w = [], []
    for k in range(TOP_K):
        m = sel & (slot == k)
        rows_p.append(jnp.sum(jnp.where(m, pos - spare_row, 0.0), axis=0, keepdims=True) + spare_row)
        rows_w.append(jnp.sum(jnp.where(m, comb, 0.0), axis=0, keepdims=True))
    posk_ref[...] = jnp.concatenate(rows_p, axis=0).astype(jnp.int32)
    wk_ref[...] = jnp.concatenate(rows_w, axis=0)


def _positions(comb_t, cnt, tri, lstrict, n_rows):
    ne, t = comb_t.shape
    tp = POS_TILE
    tok = lambda rows: pl.BlockSpec((rows, tp), lambda i: (0, i))
    return pl.pallas_call(
        functools.partial(_pos_kernel, spare_row=float(n_rows - 1)),
        grid=(t // tp,),
        in_specs=[tok(ne), _const_spec(cnt.shape), _const_spec(tri.shape), _const_spec(lstrict.shape)],
        out_specs=[tok(TOP_K), tok(TOP_K), pl.BlockSpec((ne, LANES), lambda i: (0, 0))],
        out_shape=[jax.ShapeDtypeStruct((TOP_K, t), jnp.int32),
                   jax.ShapeDtypeStruct((TOP_K, t), F32),
                   jax.ShapeDtypeStruct((ne, LANES), F32)],
        scratch_shapes=[pltpu.VMEM((ne, LANES), F32)],
        compiler_params=_params("arbitrary"),
        name="positions",
    )(comb_t, cnt, tri, lstrict)


def _sc_workers():
    info = plsc.get_sparse_core_info()
    return info.num_cores, info.num_cores * info.num_subcores


def _sc_scatter_rows(rows, pos, n_out):
    nc, nw = _sc_workers()
    n, w = rows.shape
    nk = pos.shape[0]
    ch = SC_CHUNK
    per_w = n // nw
    assert per_w * nw == n and per_w % ch == 0

    @functools.partial(
        pl.kernel, mesh=plsc.VectorSubcoreMesh(core_axis_name="c", subcore_axis_name="s"),
        out_type=jax.ShapeDtypeStruct((n_out, w), rows.dtype),
        scratch_types=[pltpu.VMEM((nk, ch), jnp.int32), pltpu.VMEM((ch, w), rows.dtype),
                       pltpu.SemaphoreType.DMA])
    def scatter(rows_hbm, pos_hbm, out_hbm, idx_v, rows_v, sem):
        base = (lax.axis_index("s") * nc + lax.axis_index("c")) * per_w

        @pl.loop(0, per_w // ch)
        def _(ci):
            off = pl.multiple_of(base + ci * ch, ch)
            pltpu.sync_copy(pos_hbm.at[:, pl.ds(off, ch)], idx_v)
            pltpu.sync_copy(rows_hbm.at[pl.ds(off, ch)], rows_v)
            copies = [pltpu.make_async_copy(rows_v, out_hbm.at[idx_v.at[k]], sem) for k in range(nk)]
            for cp in copies:
                cp.start()
            for cp in copies:
                cp.wait()

    return scatter(rows, pos)


def _sc_gather_rows(table, idx):
    nc, nw = _sc_workers()
    n = idx.shape[0]
    w = table.shape[1]
    ch = SC_CHUNK
    per_w = n // nw
    assert per_w * nw == n and per_w % ch == 0
    nch = per_w // ch

    @functools.partial(
        pl.kernel, mesh=plsc.VectorSubcoreMesh(core_axis_name="c", subcore_axis_name="s"),
        out_type=jax.ShapeDtypeStruct((n, w), table.dtype),
        scratch_types=[pltpu.VMEM((nch, ch), jnp.int32), pltpu.VMEM((ch, w), table.dtype),
                       pltpu.SemaphoreType.DMA])
    def gather(table_hbm, idx_hbm, out_hbm, idx_v, rows_v, sem):
        wid = lax.axis_index("s") * nc + lax.axis_index("c")
        base = wid * per_w
        pltpu.sync_copy(idx_hbm.at[pl.ds(wid * nch, nch)], idx_v)

        @pl.loop(0, nch)
        def _(ci):
            off = pl.multiple_of(base + ci * ch, ch)
            cp = pltpu.make_async_copy(table_hbm.at[idx_v.at[ci]], rows_v, sem)
            cp.start()
            cp.wait()
            pltpu.sync_copy(rows_v, out_hbm.at[pl.ds(off, ch)])

    return gather(table, idx.reshape(n // ch, ch))


def _expert_kernel(te_ref, nu_ref, xs_ref, wgu_ref, wd_ref, y_ref, wgu_b, wd_b):
    i = pl.program_id(0)

    @pl.when(i < nu_ref[0])
    def _():
        @pl.when((i == 0) | (te_ref[i] != te_ref[jnp.maximum(i - 1, 0)]))
        def _():
            wgu_b[...] = wgu_ref[0].astype(BF16)
            wd_b[...] = wd_ref[0].astype(BF16)

        lo, hi = _unpack_halves(xs_ref[...])
        half = wgu_b.shape[0] // 2
        gu = _dot(lo.astype(BF16), wgu_b[:half, :]) + _dot(hi.astype(BF16), wgu_b[half:, :])
        act = _silu(gu[:, :EXPERT_DIM]) * gu[:, EXPERT_DIM:]
        y_ref[...] = _pack_halves(_dot(act.astype(BF16), wd_b[...]))


def _experts(tile_e, n_used, xs, wgu, wd):
    r, w = xs.shape
    tm = EXPERT_TILE
    d = wgu.shape[1]
    rows = pl.BlockSpec((tm, w), lambda i, te, nu: (jnp.minimum(i, nu[0] - 1), 0))
    return pl.pallas_call(
        _expert_kernel,
        grid_spec=pltpu.PrefetchScalarGridSpec(
            num_scalar_prefetch=2,
            grid=(r // tm,),
            in_specs=[rows,
                      pl.BlockSpec((1, d, 2 * EXPERT_DIM), lambda i, te, nu: (te[i], 0, 0)),
                      pl.BlockSpec((1, EXPERT_DIM, d), lambda i, te, nu: (te[i], 0, 0))],
            out_specs=rows,
            scratch_shapes=[pltpu.VMEM((d, 2 * EXPERT_DIM), BF16), pltpu.VMEM((EXPERT_DIM, d), BF16)]),
        out_shape=jax.ShapeDtypeStruct((r, w), jnp.int32),
        compiler_params=_params("arbitrary"),
        name="experts",
    )(tile_e, n_used, xs, wgu, wd)


def _final_kernel(yk_ref, wk_ref, h_ref, x1_ref, mod_ref, wsgu_ref, wsd_ref, out_ref):
    half = yk_ref.shape[2]
    acc_lo = jnp.zeros((yk_ref.shape[1], half), F32)
    acc_hi = jnp.zeros((yk_ref.shape[1], half), F32)
    for k in range(TOP_K):
        lo, hi = _unpack_halves(yk_ref[k])
        w = wk_ref[:, k:k + 1]
        acc_lo = acc_lo + jnp.where(w != 0.0, w * lo, 0.0)
        acc_hi = acc_hi + jnp.where(w != 0.0, w * hi, 0.0)
    routed = jnp.concatenate([acc_lo, acc_hi], axis=1)
    sgu = _dot(h_ref[...], wsgu_ref[...])
    act = _silu(sgu[:, :SHARED_DIM]) * sgu[:, SHARED_DIM:]
    shared = _dot(act.astype(BF16), wsd_ref[...])
    out_ref[...] = x1_ref[...] + mod_ref[0, 5:6, :] * (routed + shared)


def _final(yk, wk_t, h2, x1, mod, wsgu, wsd, tiles_per_batch):
    t, d = h2.shape
    tm = ROW_TILE
    row = lambda w: pl.BlockSpec((tm, w), lambda i: (i, 0))
    return pl.pallas_call(
        _final_kernel,
        grid=(t // tm,),
        in_specs=[pl.BlockSpec((TOP_K, tm, d // 2), lambda i: (0, i, 0)),
                  row(TOP_K), row(d), row(d),
                  pl.BlockSpec((1, 6, d), lambda i: (i // tiles_per_batch, 0, 0)),
                  _const_spec(wsgu.shape), _const_spec(wsd.shape)],
        out_specs=row(d),
        out_shape=jax.ShapeDtypeStruct((t, d), F32),
        compiler_params=_params("arbitrary"),
        name="final",
    )(yk, wk_t, h2, x1, mod, wsgu, wsd)


def _placement():
    pq = np.zeros((LANES, N_HEADS * HEAD_PAD), np.float32)
    cq = np.zeros((1, N_HEADS * HEAD_PAD), np.float32)
    ck = np.zeros((1, N_HEADS * HEAD_PAD), np.float32)
    for hd in range(N_HEADS):
        for k in range(3):
            pq[k * N_HEADS + hd, hd * HEAD_PAD + AUG0 + k] = 1.0
            ck[0, hd * HEAD_PAD + AUG0 + k] = 1.0
            cq[0, hd * HEAD_PAD + AUG0 + 3 + k] = 1.0
    return jnp.asarray(pq, BF16), jnp.asarray(cq), jnp.asarray(ck)


def kernel(x, c, w_ada, b_ada, norm1_g, w_in, w_dw, b_dw, conv_gn_g, conv_gn_b, w_conv_out,
           q_norm_g, k_norm_g, b_forget, w_attn_out, w_out, norm2_g, w_router, router_bias,
           w_experts_gate_up, w_experts_down, w_shared_gate_up, w_shared_down):
    depth = w_ada.shape[0]
    b, s, d = x.shape

    pq, cq, ck = _placement()
    tri = jnp.asarray(np.tril(np.ones((ROW_TILE, ROW_TILE), np.float32)), BF16)
    grp = np.arange(CONV_DIM) // (CONV_DIM // CONV_GROUPS)
    gg = jnp.asarray((grp[:, None] == grp[None, :]).astype(np.float32) / (CONV_DIM // CONV_GROUPS), BF16)
    c_pad = jnp.pad(c, ((0, SUBLANES - b), (0, 0)))
    tri_pos = jnp.asarray(np.triu(np.ones((POS_TILE, POS_TILE), np.float32)), BF16)
    lstrict = jnp.asarray(np.tril(np.ones((N_EXPERTS, N_EXPERTS), np.float32), -1), BF16)

    for l in range(depth):
        mod = _ada(c_pad, w_ada[l], b_ada[l][None, :])[:b].reshape(b, 6, d)

        bf = jnp.pad(b_forget[l][None, :], ((0, 0), (0, LANES - N_HEADS)))
        gpad = lambda g, sc: jnp.tile(jnp.pad(g * sc, (0, HEAD_PAD - HEAD_DIM)), N_HEADS)[None, :]
        qscale = HEAD_DIM ** -0.5
        u, qa, ka, v, sgc, sga, cum = _inproj(
            x, mod, norm1_g[l][None, :], _inproj_weight(w_in[l]),
            bf, gpad(q_norm_g[l], qscale), gpad(k_norm_g[l], 1.0),
            pq, cq, ck, tri)

        flat = lambda a: a[:, :, :N_HEADS].transpose(0, 2, 1).reshape(-1)
        part = ATTN_TILE // DIAG_PARTS
        cs = flat(cum[:, 0::part])
        ce = flat(cum[:, part - 1::part])
        bound = (1.02 * HEAD_DIM * qscale) * jnp.max(jnp.abs(q_norm_g[l])) * jnp.max(jnp.abs(k_norm_g[l]))
        o = _attention(cs, ce, bound.reshape(1), qa, ka, v)

        wdw = jnp.pad(w_dw[l], ((0, CONV_HALO - CONV_WIDTH), (0, 0)))
        x1 = _merge(u, o, sgc, sga, x, mod, wdw, b_dw[l][None, :], conv_gn_g[l][None, :],
                    conv_gn_b[l][None, :], gg, w_conv_out[l].astype(BF16), w_attn_out[l].astype(BF16),
                    w_out[l].astype(BF16))

        wr = w_router[l].T
        wr_hi = wr.astype(BF16)
        wr_lo = (wr - wr_hi.astype(F32)).astype(BF16)
        h2, h2w, comb_t, cnt = _router(x1, mod, norm2_g[l][None, :], wr_hi, wr_lo, router_bias[l][:, None])

        t = b * s
        n_tiles = (t * TOP_K) // EXPERT_TILE + N_EXPERTS
        posk, wk, gend = _positions(comb_t, cnt, tri_pos, lstrict, n_tiles * EXPERT_TILE)
        seg_end = gend[:, 0].astype(jnp.int32)
        n_used = seg_end[-1:] // EXPERT_TILE
        tile_start = jnp.arange(n_tiles, dtype=jnp.int32) * EXPERT_TILE
        tile_start = jnp.minimum(tile_start, seg_end[-1] - EXPERT_TILE)
        tile_e = jnp.sum((seg_end[None, :] <= tile_start[:, None]).astype(jnp.int32), axis=1)

        xs = _sc_scatter_rows(h2w.reshape(t, d // 2), posk, n_tiles * EXPERT_TILE)
        ys = _experts(tile_e, n_used, xs, w_experts_gate_up[l], w_experts_down[l])
        yk = _sc_gather_rows(ys, posk.reshape(-1)).reshape(TOP_K, t, d // 2)
        out = _final(yk, wk.T, h2.reshape(t, d), x1.reshape(t, d), mod,
                     w_shared_gate_up[l].astype(BF16), w_shared_down[l].astype(BF16), s // ROW_TILE)
        x = out.reshape(b, s, d)
    return x
```

```python
import functools

import numpy as np
import jax
import jax.numpy as jnp
from jax import lax
from jax.experimental import pallas as pl
from jax.experimental.pallas import tpu as pltpu
from jax.experimental.pallas import tpu_sc as plsc

F32 = jnp.float32
BF16 = jnp.bfloat16

CONV_DIM = 512
CONV_WIDTH = 31
CONV_GROUPS = 8
N_HEADS = 8
HEAD_DIM = 64
ATTN_DIM = N_HEADS * HEAD_DIM
N_EXPERTS = 64
TOP_K = 8
N_GROUPS = 8
TOPK_GROUPS = 4
EXPERT_DIM = 256
SHARED_DIM = 256
ROUTED_SCALE = 2.5
EPS = 1e-6

LANES = 128
SUBLANES = 8
HEAD_PAD = LANES
AUG0 = HEAD_DIM
F_LANE0 = HEAD_DIM + 8
VMEM_LIMIT = 56 * 1024 * 1024

ROW_TILE = 512
WIDE_TILE = 1024
ATTN_TILE = 1024
DIAG_PARTS = 4
EXPERT_TILE = 2048
WPREP_ROWS = 256
POS_TILE = 1024
SC_CHUNK = 128
CONV_HALO = 32

NEG_BIG = -1e30
EXP_UNDERFLOW = 104.0
FIXED_SHIFT_BOUND = 40.0


def _dot(a, b):
    return jnp.dot(a, b, preferred_element_type=F32)


def _dot_nt(a, b):
    return lax.dot_general(a, b, (((1,), (1,)), ((), ())), preferred_element_type=F32)


def _split2(x):
    hi = x.astype(BF16)
    lo = (x - hi.astype(F32)).astype(BF16)
    return hi, lo


def _split3(x):
    hi = x.astype(BF16)
    r = x - hi.astype(F32)
    mid = r.astype(BF16)
    lo = (r - mid.astype(F32)).astype(BF16)
    return hi, mid, lo


def _pack_halves(v):
    n = v.shape[1] // 2
    lo = lax.bitcast_convert_type(v[:, :n].astype(BF16).astype(F32), jnp.uint32)
    hi = lax.bitcast_convert_type(v[:, n:].astype(BF16).astype(F32), jnp.uint32)
    return lax.bitcast_convert_type(hi | lax.shift_right_logical(lo, jnp.uint32(16)), jnp.int32)


def _unpack_halves(w):
    u = lax.bitcast_convert_type(w, jnp.uint32)
    lo = lax.bitcast_convert_type(lax.shift_left(u, jnp.uint32(16)), F32)
    hi = lax.bitcast_convert_type(u & jnp.uint32(0xFFFF0000), F32)
    return lo, hi


def _sigmoid(x):
    return 1.0 / (1.0 + jnp.exp(-x))


def _silu(x):
    return x * _sigmoid(x)


def _params(*sem):
    return pltpu.CompilerParams(dimension_semantics=sem, vmem_limit_bytes=VMEM_LIMIT)


def _const_spec(shape):
    n = len(shape)
    return pl.BlockSpec(shape, lambda *_: (0,) * n, pipeline_mode=pl.Buffered(1))


def _ada_kernel(c_ref, w_ref, b_ref, o_ref):
    c = c_ref[...]
    a_hi, a_lo = _split2(_silu(c))
    w_hi, w_lo = _split2(w_ref[...])
    o_ref[...] = _dot(a_hi, w_hi) + _dot(a_hi, w_lo) + _dot(a_lo, w_hi) + b_ref[...]


def _ada(c_pad, w_ada, b_ada):
    d = c_pad.shape[1]
    n = w_ada.shape[1]
    return pl.pallas_call(
        _ada_kernel,
        grid=(n // d,),
        in_specs=[_const_spec(c_pad.shape),
                  pl.BlockSpec((d, d), lambda j: (0, j)),
                  pl.BlockSpec((1, d), lambda j: (0, j))],
        out_specs=pl.BlockSpec((c_pad.shape[0], d), lambda j: (0, j)),
        out_shape=jax.ShapeDtypeStruct((c_pad.shape[0], n), F32),
        compiler_params=_params("arbitrary"),
        name="ada",
    )(c_pad, w_ada, b_ada)


def _lane_pieces(x):
    hi, mid, lo = _split3(x)
    return (hi.astype(F32) + pltpu.roll(mid.astype(F32), N_HEADS, 1)
            + pltpu.roll(lo.astype(F32), 2 * N_HEADS, 1)).astype(BF16)


def _head_tile(p, col0, hd):
    per = HEAD_PAD // HEAD_DIM
    g, part = divmod(hd, per)
    blk = p[:, col0 + g * HEAD_PAD:col0 + (g + 1) * HEAD_PAD]
    if part:
        blk = pltpu.roll(blk, HEAD_PAD - part * HEAD_DIM, 1)
    lane = lax.broadcasted_iota(jnp.int32, (1, HEAD_PAD), 1)
    return jnp.where(lane < HEAD_DIM, blk, 0.0)


def _inproj_kernel(x_ref, mod_ref, g1_ref, w_ref, bf_ref, qg_ref, kg_ref, pq_ref, cq_ref, ck_ref, tri_ref,
                   u_ref, qa_ref, ka_ref, v_ref, sgc_ref, sga_ref, cum_ref, carry_ref):
    @pl.when(pl.program_id(1) == 0)
    def _():
        carry_ref[...] = jnp.zeros_like(carry_ref)

    x = x_ref[0]
    ms = jnp.mean(x * x, axis=-1, keepdims=True)
    h = (x * lax.rsqrt(ms + EPS) * g1_ref[...]) * (1.0 + mod_ref[0, 1:2, :]) + mod_ref[0, 0:1, :]
    hb = h.astype(BF16)

    hp = N_HEADS * HEAD_PAD
    o_q = 2 * CONV_DIM
    o_k, o_v = o_q + hp, o_q + 2 * hp
    o_gc = o_v + ATTN_DIM
    o_ga = o_gc + x.shape[1]
    pc = _dot(hb, w_ref[:, :o_q])
    u_ref[0] = (pc[:, :CONV_DIM] * _sigmoid(pc[:, CONV_DIM:])).astype(BF16)

    pq = _dot(hb, w_ref[:, o_q:o_k])
    z = pltpu.roll(pq[:, :HEAD_PAD], HEAD_PAD - F_LANE0, 1) + bf_ref[...]
    lf = jnp.minimum(z, 0.0) - jnp.log1p(jnp.exp(-jnp.abs(z)))
    lane = lax.broadcasted_iota(jnp.int32, (1, LANES), 1)
    psum = _dot(tri_ref[...], _lane_pieces(jnp.where(lane < N_HEADS, lf, 0.0)))
    cum = (psum + pltpu.roll(psum, LANES - N_HEADS, 1) + pltpu.roll(psum, LANES - 2 * N_HEADS, 1)
           + carry_ref[...])
    cum = jnp.where(lane < N_HEADS, cum, 0.0)
    carry_ref[...] = cum[ROW_TILE - 1:ROW_TILE, :]
    cum_ref[0] = cum

    pieces = _lane_pieces(cum)
    placed = _dot(pieces, pq_ref[...])
    addq = placed + cq_ref[...]
    addk = ck_ref[...] - pltpu.roll(placed, 3, 1)

    pk = _dot(hb, w_ref[:, o_k:o_v])
    pv = _dot(hb, w_ref[:, o_v:o_gc])
    inv_hd = 1.0 / HEAD_DIM
    hlane = lax.broadcasted_iota(jnp.int32, (1, HEAD_PAD), 1)
    vone = (hlane == HEAD_DIM).astype(F32)
    for hd in range(N_HEADS):
        sl = slice(hd * HEAD_PAD, (hd + 1) * HEAD_PAD)
        qb = pq[:, sl]
        if hd == 0:
            qb = jnp.where(hlane < HEAD_DIM, qb, 0.0)
        qn = qb * lax.rsqrt(jnp.sum(qb * qb, axis=-1, keepdims=True) * inv_hd + EPS) * qg_ref[:, sl]
        qa_ref[0, hd] = (qn + addq[:, sl]).astype(BF16)
        kb = pk[:, sl]
        kn = kb * lax.rsqrt(jnp.sum(kb * kb, axis=-1, keepdims=True) * inv_hd + EPS) * kg_ref[:, sl]
        ka_ref[0, hd] = (kn + addk[:, sl]).astype(BF16)
        v_ref[0, hd] = (_head_tile(pv, 0, hd) + vone).astype(BF16)

    sgc_ref[0] = _sigmoid(_dot(hb, w_ref[:, o_gc:o_ga])).astype(BF16)
    sga_ref[0] = _sigmoid(_dot(hb, w_ref[:, o_ga:])).astype(BF16)


def _inproj_weight(wi):
    d, n_in = wi.shape
    n_out = 2 * CONV_DIM + 2 * N_HEADS * HEAD_PAD + ATTN_DIM + 2 * d
    rows = WPREP_ROWS
    return pl.pallas_call(
        _wprep_kernel,
        grid=(d // rows,),
        in_specs=[pl.BlockSpec((rows, n_in), lambda i: (i, 0))],
        out_specs=pl.BlockSpec((rows, n_out), lambda i: (i, 0)),
        out_shape=jax.ShapeDtypeStruct((d, n_out), BF16),
        compiler_params=_params("arbitrary"),
        name="wprep",
    )(wi)


def _wprep_kernel(w_ref, o_ref):
    rows, n_in = w_ref.shape
    o_q = 2 * CONV_DIM
    o_f = o_q + 3 * ATTN_DIM
    o_gc = o_f + N_HEADS
    lane = lax.broadcasted_iota(jnp.int32, (1, LANES), 1)
    per = HEAD_PAD // HEAD_DIM

    o_ref[:, :o_q] = w_ref[:, :o_q].astype(BF16)
    out = o_q
    fcols = pltpu.roll(w_ref[:, o_f:o_f + LANES], F_LANE0, 1)
    for blk in range(2):
        src0 = o_q + blk * ATTN_DIM
        for hd in range(N_HEADS):
            g, part = divmod(hd, per)
            tile = w_ref[:, src0 + g * HEAD_PAD:src0 + (g + 1) * HEAD_PAD]
            if part:
                tile = pltpu.roll(tile, HEAD_PAD - part * HEAD_DIM, 1)
            tile = jnp.where(lane < HEAD_DIM, tile, 0.0)
            if blk == 0 and hd == 0:
                tile = jnp.where((lane >= F_LANE0) & (lane < F_LANE0 + N_HEADS), fcols, tile)
            o_ref[:, out:out + HEAD_PAD] = tile.astype(BF16)
            out += HEAD_PAD
    o_ref[:, out:out + ATTN_DIM] = w_ref[:, o_q + 2 * ATTN_DIM:o_f].astype(BF16)
    out += ATTN_DIM
    shift = LANES - (o_gc % LANES)
    for j in range((n_in - o_gc) // LANES):
        a0 = (o_gc // LANES + j) * LANES
        a = pltpu.roll(w_ref[:, a0:a0 + LANES], shift, 1)
        width = min(LANES, n_in - (a0 + LANES))
        b = w_ref[:, a0 + LANES:a0 + LANES + width]
        if width < LANES:
            b = jnp.concatenate([b, jnp.zeros((rows, LANES - width), F32)], axis=1)
        b = pltpu.roll(b, shift, 1)
        o_ref[:, out:out + LANES] = jnp.where(lane < shift, a, b).astype(BF16)
        out += LANES


def _inproj(x, mod, g1, w, bf, qg, kg, pq, cq, ck, tri):
    b, s, d = x.shape
    tm = ROW_TILE
    row = lambda w: pl.BlockSpec((1, tm, w), lambda bi, i: (bi, i, 0))
    head = pl.BlockSpec((1, N_HEADS, tm, HEAD_PAD), lambda bi, i: (bi, 0, i, 0))
    consts = [g1, w, bf, qg, kg, pq, cq, ck, tri]
    return pl.pallas_call(
        _inproj_kernel,
        grid=(b, s // tm),
        in_specs=[row(d), pl.BlockSpec((1, 6, d), lambda bi, i: (bi, 0, 0))]
                 + [_const_spec(a.shape) for a in consts],
        out_specs=[row(CONV_DIM), head, head, head, row(d), row(d), row(LANES)],
        out_shape=[jax.ShapeDtypeStruct((b, s, CONV_DIM), BF16),
                   jax.ShapeDtypeStruct((b, N_HEADS, s, HEAD_PAD), BF16),
                   jax.ShapeDtypeStruct((b, N_HEADS, s, HEAD_PAD), BF16),
                   jax.ShapeDtypeStruct((b, N_HEADS, s, HEAD_PAD), BF16),
                   jax.ShapeDtypeStruct((b, s, d), BF16),
                   jax.ShapeDtypeStruct((b, s, d), BF16),
                   jax.ShapeDtypeStruct((b, s, LANES), F32)],
        scratch_shapes=[pltpu.VMEM((1, LANES), F32)],
        compiler_params=_params("arbitrary", "arbitrary"),
        name="inproj",
    )(x, mod, *consts)


def _attn_kernel(cs_ref, ce_ref, prm_ref, q_ref, k_ref, v_ref, o_ref, acc_ref, m_ref):
    t = ATTN_TILE
    nb = q_ref.shape[2] // t
    base = (pl.program_id(0) * pl.num_programs(1) + pl.program_id(1)) * nb
    bound = prm_ref[0]
    thresh = -(EXP_UNDERFLOW + 2.0 * bound)
    causal = lax.broadcasted_iota(jnp.int32, (t, t), 0) >= lax.broadcasted_iota(jnp.int32, (t, t), 1)

    dp = DIAG_PARTS

    def first_block(i):
        c0 = cs_ref[dp * (base + i)]
        return lax.fori_loop(
            0, i, lambda j, n: n + (c0 - ce_ref[dp * (base + j) + dp - 1] < thresh).astype(jnp.int32), 0)

    def scores(q, j, masked):
        k0 = pl.multiple_of(j * t, t)
        s = _dot(q, k_ref[0, 0, :, pl.ds(k0, t)])
        if masked:
            s = jnp.where(causal, s, NEG_BIG)
        return s, v_ref[0, 0, pl.ds(k0, t), :]

    def finish(q0):
        acc = acc_ref[...]
        o_ref[0, pl.ds(q0, t), :] = (acc / acc[:, HEAD_DIM:HEAD_DIM + 1]).astype(BF16)

    def fixed_shift(i, carry):
        q0 = pl.multiple_of(i * t, t)
        q = q_ref[0, 0, pl.ds(q0, t), :]

        acc_ref[...] = jnp.zeros_like(acc_ref)

        def weighted(j, masked):
            s, vb = scores(q, j, masked)
            return _dot(jnp.exp(s).astype(BF16), vb)

        def kv(j, c):
            acc_ref[...] += weighted(j, False)
            return c

        lax.fori_loop(first_block(i), i - 1, kv, 0)
        rp = t // DIAG_PARTS

        @pl.when(i == 0)
        def _():
            for r in range(DIAG_PARTS):
                nk = (r + 1) * rp
                s = _dot(q[r * rp:(r + 1) * rp, :], k_ref[0, 0, :, pl.ds(q0, nk)])
                seen = (lax.broadcasted_iota(jnp.int32, (rp, nk), 0) + r * rp
                        >= lax.broadcasted_iota(jnp.int32, (rp, nk), 1))
                p = jnp.exp(jnp.where(seen, s, NEG_BIG)).astype(BF16)
                acc_ref[r * rp:(r + 1) * rp, :] += _dot(p, v_ref[0, 0, pl.ds(q0, nk), :])
            finish(q0)

        @pl.when(i > 0)
        def _():
            nk = t + rp
            for r in range(DIAG_PARTS):
                k0 = pl.multiple_of(q0 - t + r * rp, rp)
                s = _dot(q[r * rp:(r + 1) * rp, :], k_ref[0, 0, :, pl.ds(k0, nk)])
                seen = (lax.broadcasted_iota(jnp.int32, (rp, nk), 0) + t
                        >= lax.broadcasted_iota(jnp.int32, (rp, nk), 1))
                p = jnp.exp(jnp.where(seen, s, NEG_BIG)).astype(BF16)
                acc_ref[r * rp:(r + 1) * rp, :] += _dot(p, v_ref[0, 0, pl.ds(k0, nk), :])

            for r in range(1, DIAG_PARTS):
                need = cs_ref[dp * (base + i) + r] - ce_ref[dp * (base + i - 1) + r - 1] >= thresh

                @pl.when(need)
                def _():
                    kp = pl.multiple_of(q0 - t, t)
                    s = _dot(q[r * rp:(r + 1) * rp, :], k_ref[0, 0, :, pl.ds(kp, r * rp)])
                    acc_ref[r * rp:(r + 1) * rp, :] += _dot(jnp.exp(s).astype(BF16),
                                                            v_ref[0, 0, pl.ds(kp, r * rp), :])

            finish(q0)

        return carry

    def running_max(i, carry):
        q0 = pl.multiple_of(i * t, t)
        q = q_ref[0, 0, pl.ds(q0, t), :]
        m_ref[...] = jnp.full_like(m_ref, -jnp.inf)
        acc_ref[...] = jnp.zeros_like(acc_ref)

        def step(j, masked):
            s, vb = scores(q, j, masked)
            m_prev = m_ref[...]
            m_new = jnp.maximum(m_prev, jnp.max(s, axis=-1, keepdims=True))
            p = jnp.exp(s - m_new)
            acc_ref[...] = jnp.exp(m_prev - m_new) * acc_ref[...] + _dot(p.astype(BF16), vb)
            m_ref[...] = m_new

        def kv(j, c):
            step(j, False)
            return c

        lax.fori_loop(first_block(i), i, kv, 0)
        step(i, True)
        finish(q0)
        return carry

    @pl.when(bound <= FIXED_SHIFT_BOUND)
    def _():
        lax.fori_loop(0, nb, fixed_shift, 0)

    @pl.when(bound > FIXED_SHIFT_BOUND)
    def _():
        lax.fori_loop(0, nb, running_max, 0)


def _attention(cs, ce, prm, qa, ka, v):
    b, nh, s, hp = qa.shape
    t = ATTN_TILE
    seq = pl.BlockSpec((1, 1, s, hp), lambda bi, hi, *_: (bi, hi, 0, 0))
    return pl.pallas_call(
        _attn_kernel,
        grid_spec=pltpu.PrefetchScalarGridSpec(
            num_scalar_prefetch=3,
            grid=(b, nh),
            in_specs=[seq, pl.BlockSpec((1, 1, hp, s), lambda bi, hi, *_: (bi, hi, 0, 0)), seq],
            out_specs=pl.BlockSpec((1, s, hp), lambda bi, hi, *_: (bi, 0, hi)),
            scratch_shapes=[pltpu.VMEM((t, hp), F32), pltpu.VMEM((t, 1), F32)]),
        out_shape=jax.ShapeDtypeStruct((b, s, nh * hp), BF16),
        compiler_params=_params("arbitrary", "arbitrary"),
        name="attn",
    )(cs, ce, prm, qa, ka, v)


def _merge_kernel(u_ref, halo_ref, o_ref, sgc_ref, sga_ref, x_ref, mod_ref, wdw_ref, bdw_ref,
                  gng_ref, gnb_ref, gg_ref, wco_ref, wao_ref, wout_ref, x1_ref, buf_ref):
    tm = WIDE_TILE
    halo = halo_ref[0].astype(F32)
    halo = jnp.where(pl.program_id(1) == 0, jnp.zeros_like(halo), halo)
    ucur = u_ref[0].astype(F32)
    for cb in range(CONV_DIM // LANES):
        buf_ref[cb, 0:CONV_HALO, :] = halo[:, cb * LANES:(cb + 1) * LANES]
        buf_ref[cb, CONV_HALO:, :] = ucur[:, cb * LANES:(cb + 1) * LANES]

    base = CONV_HALO - (CONV_WIDTH - 1)
    ys = []
    for cb in range(CONV_DIM // LANES):
        acc = jnp.zeros((tm, LANES), F32)
        for j in range(CONV_WIDTH):
            acc = acc + wdw_ref[j:j + 1, cb * LANES:(cb + 1) * LANES] * buf_ref[cb, base + j:base + j + tm, :]
        ys.append(acc)
    y = jnp.concatenate(ys, axis=1) + bdw_ref[...]

    gg = gg_ref[...]
    y_hi, y_lo = _split2(y)
    dlt = y - (_dot(y_hi, gg) + _dot(y_lo, gg))
    s_hi, s_lo = _split2(dlt * dlt)
    var = _dot(s_hi, gg) + _dot(s_lo, gg)
    yn = dlt * lax.rsqrt(var + EPS) * gng_ref[...] + gnb_ref[...]
    y_conv = _dot(_silu(yn).astype(BF16), wco_ref[...])

    per = HEAD_PAD // HEAD_DIM
    lane = lax.broadcasted_iota(jnp.int32, (1, HEAD_PAD), 1)
    packed = []
    for g in range(N_HEADS // per):
        tile = o_ref[0, :, g * per * HEAD_PAD:(g * per + 1) * HEAD_PAD].astype(F32)
        for part in range(1, per):
            nxt = o_ref[0, :, (g * per + part) * HEAD_PAD:(g * per + part + 1) * HEAD_PAD].astype(F32)
            tile = jnp.where(lane < part * HEAD_DIM, tile, pltpu.roll(nxt, part * HEAD_DIM, 1))
        packed.append(tile)
    y_attn = _dot(jnp.concatenate(packed, axis=1).astype(BF16), wao_ref[...])
    merged = sgc_ref[0].astype(F32) * y_conv + sga_ref[0].astype(F32) * y_attn
    mix = _dot(merged.astype(BF16), wout_ref[...])
    x1_ref[0] = x_ref[0] + mod_ref[0, 2:3, :] * mix


def _merge(u, o, sgc, sga, x, mod, wdw, bdw, gng, gnb, gg, wco, wao, wout):
    b, s, d = x.shape
    tm = WIDE_TILE
    per = tm // CONV_HALO
    row = lambda w: pl.BlockSpec((1, tm, w), lambda bi, i: (bi, i, 0))
    consts = [wdw, bdw, gng, gnb, gg, wco, wao, wout]
    return pl.pallas_call(
        _merge_kernel,
        grid=(b, s // tm),
        in_specs=[row(CONV_DIM),
                  pl.BlockSpec((1, CONV_HALO, CONV_DIM),
                               lambda bi, i: (bi, jnp.maximum(i * per - 1, 0), 0)),
                  row(o.shape[2]), row(d), row(d), row(d),
                  pl.BlockSpec((1, 6, d), lambda bi, i: (bi, 0, 0))]
                 + [_const_spec(a.shape) for a in consts],
        out_specs=row(d),
        out_shape=jax.ShapeDtypeStruct((b, s, d), F32),
        scratch_shapes=[pltpu.VMEM((CONV_DIM // LANES, CONV_HALO + tm, LANES), F32)],
        compiler_params=_params("arbitrary", "arbitrary"),
        name="merge",
    )(u, u, o, sgc, sga, x, mod, *consts)


def _router_kernel(x1_ref, mod_ref, g2_ref, wr_hi_ref, wr_lo_ref, rb_ref, h2_ref, h2w_ref, comb_ref, cnt_ref):
    x = x1_ref[0]
    ms = jnp.mean(x * x, axis=-1, keepdims=True)
    h = (x * lax.rsqrt(ms + EPS) * g2_ref[...]) * (1.0 + mod_ref[0, 4:5, :]) + mod_ref[0, 3:4, :]
    h2_ref[0] = h.astype(BF16)
    h2w_ref[0] = _pack_halves(h)

    h_hi, h_lo = _split2(h)
    logits = _dot_nt(wr_hi_ref[...], h_hi) + _dot_nt(wr_hi_ref[...], h_lo) + _dot_nt(wr_lo_ref[...], h_hi)
    scores = _sigmoid(logits)
    biased = scores + rb_ref[...]

    per = N_EXPERTS // N_GROUPS
    rows = lax.broadcasted_iota(jnp.int32, (per, biased.shape[1]), 0)
    gscore = []
    for g in range(N_GROUPS):
        blk = biased[g * per:(g + 1) * per, :]
        top1 = jnp.max(blk, axis=0, keepdims=True)
        first = jnp.min(jnp.where(blk == top1, rows, per), axis=0, keepdims=True)
        top2 = jnp.max(jnp.where(rows == first, -jnp.inf, blk), axis=0, keepdims=True)
        gscore.append(top1 + top2)

    cand = []
    for g in range(N_GROUPS):
        rank = jnp.zeros_like(gscore[g], dtype=jnp.int32)
        for g2 in range(N_GROUPS):
            if g2 == g:
                continue
            ahead = gscore[g2] > gscore[g]
            if g2 < g:
                ahead = ahead | (gscore[g2] == gscore[g])
            rank = rank + ahead.astype(jnp.int32)
        keep = rank < TOPK_GROUPS
        cand.append(jnp.where(keep, biased[g * per:(g + 1) * per, :], -jnp.inf))
    cand = jnp.concatenate(cand, axis=0)

    eidx = lax.broadcasted_iota(jnp.int32, cand.shape, 0)
    work = cand
    for _ in range(TOP_K):
        top = jnp.max(work, axis=0, keepdims=True)
        first = jnp.min(jnp.where(work == top, eidx, N_EXPERTS), axis=0, keepdims=True)
        work = jnp.where(eidx == first, -jnp.inf, work)
    sel = (work != cand) & (cand > -jnp.inf)
    w = jnp.where(sel, scores, 0.0)
    comb = w / jnp.sum(w, axis=0, keepdims=True) * ROUTED_SCALE
    comb_ref[...] = comb

    @pl.when((pl.program_id(0) == 0) & (pl.program_id(1) == 0))
    def _():
        cnt_ref[...] = jnp.zeros_like(cnt_ref)

    hit = jnp.where(comb != 0.0, 1.0, 0.0).astype(BF16)
    cnt_ref[...] += _dot(hit, jnp.ones((hit.shape[1], LANES), BF16))


def _router(x1, mod, g2, wr_hi, wr_lo, rb):
    b, s, d = x1.shape
    tm = WIDE_TILE
    nt = s // tm
    return pl.pallas_call(
        _router_kernel,
        grid=(b, nt),
        in_specs=[pl.BlockSpec((1, tm, d), lambda bi, i: (bi, i, 0)),
                  pl.BlockSpec((1, 6, d), lambda bi, i: (bi, 0, 0)),
                  _const_spec(g2.shape), _const_spec(wr_hi.shape), _const_spec(wr_lo.shape),
                  _const_spec(rb.shape)],
        out_specs=[pl.BlockSpec((1, tm, d), lambda bi, i: (bi, i, 0)),
                   pl.BlockSpec((1, tm, d // 2), lambda bi, i: (bi, i, 0)),
                   pl.BlockSpec((N_EXPERTS, tm), lambda bi, i: (0, bi * nt + i)),
                   pl.BlockSpec((N_EXPERTS, LANES), lambda bi, i: (0, 0))],
        out_shape=[jax.ShapeDtypeStruct((b, s, d), BF16),
                   jax.ShapeDtypeStruct((b, s, d // 2), jnp.int32),
                   jax.ShapeDtypeStruct((N_EXPERTS, b * s), F32),
                   jax.ShapeDtypeStruct((N_EXPERTS, LANES), F32)],
        compiler_params=_params("arbitrary", "arbitrary"),
        name="router",
    )(x1, mod, g2, wr_hi, wr_lo, rb)


def _pos_kernel(comb_ref, cnt_ref, tri_ref, lstrict_ref, posk_ref, wk_ref, gend_ref, base_ref, *, spare_row):
    tp = comb_ref.shape[1]
    comb = comb_ref[...]
    sel = comb != 0.0
    selb = jnp.where(sel, 1.0, 0.0).astype(BF16)

    @pl.when(pl.program_id(0) == 0)
    def _():
        seg = jnp.floor((cnt_ref[...] + (EXPERT_TILE - 1.0)) * (1.0 / EXPERT_TILE)) * EXPERT_TILE
        s_hi, s_mid, s_lo = _split3(seg)
        ls = lstrict_ref[...]
        start = _dot(ls, s_hi) + _dot(ls, s_mid) + _dot(ls, s_lo)
        base_ref[...] = start
        gend_ref[...] = start + seg

    rank = _dot(selb, tri_ref[...])
    pos = base_ref[:, 0:1] + rank - 1.0
    base_ref[...] += _dot(selb, jnp.ones((tp, LANES), BF16))
    slot = _dot(lstrict_ref[...], selb)
    rows_p, rows_w = [], []
    for k in range(TOP_K):
        m = sel & (slot == k)
        rows_p.append(jnp.sum(jnp.where(m, pos - spare_row, 0.0), axis=0, keepdims=True) + spare_row)
        rows_w.append(jnp.sum(jnp.where(m, comb, 0.0), axis=0, keepdims=True))
    posk_ref[...] = jnp.concatenate(rows_p, axis=0).astype(jnp.int32)
    wk_ref[...] = jnp.concatenate(rows_w, axis=0)


def _positions(comb_t, cnt, tri, lstrict, n_rows):
    ne, t = comb_t.shape
    tp = POS_TILE
    tok = lambda rows: pl.BlockSpec((rows, tp), lambda i: (0, i))
    return pl.pallas_call(
        functools.partial(_pos_kernel, spare_row=float(n_rows - 1)),
        grid=(t // tp,),
        in_specs=[tok(ne), _const_spec(cnt.shape), _const_spec(tri.shape), _const_spec(lstrict.shape)],
        out_specs=[tok(TOP_K), tok(TOP_K), pl.BlockSpec((ne, LANES), lambda i: (0, 0))],
        out_shape=[jax.ShapeDtypeStruct((TOP_K, t), jnp.int32),
                   jax.ShapeDtypeStruct((TOP_K, t), F32),
                   jax.ShapeDtypeStruct((ne, LANES), F32)],
        scratch_shapes=[pltpu.VMEM((ne, LANES), F32)],
        compiler_params=_params("arbitrary"),
        name="positions",
    )(comb_t, cnt, tri, lstrict)


def _sc_workers():
    info = plsc.get_sparse_core_info()
    return info.num_cores, info.num_cores * info.num_subcores


def _sc_scatter_rows(rows, pos, n_out):
    nc, nw = _sc_workers()
    n, w = rows.shape
    nk = pos.shape[0]
    ch = SC_CHUNK
    per_w = n // nw
    assert per_w * nw == n and per_w % ch == 0

    @functools.partial(
        pl.kernel, mesh=plsc.VectorSubcoreMesh(core_axis_name="c", subcore_axis_name="s"),
        out_type=jax.ShapeDtypeStruct((n_out, w), rows.dtype),
        scratch_types=[pltpu.VMEM((nk, ch), jnp.int32), pltpu.VMEM((ch, w), rows.dtype),
                       pltpu.SemaphoreType.DMA])
    def scatter(rows_hbm, pos_hbm, out_hbm, idx_v, rows_v, sem):
        base = (lax.axis_index("s") * nc + lax.axis_index("c")) * per_w

        @pl.loop(0, per_w // ch)
        def _(ci):
            off = pl.multiple_of(base + ci * ch, ch)
            pltpu.sync_copy(pos_hbm.at[:, pl.ds(off, ch)], idx_v)
            pltpu.sync_copy(rows_hbm.at[pl.ds(off, ch)], rows_v)
            copies = [pltpu.make_async_copy(rows_v, out_hbm.at[idx_v.at[k]], sem) for k in range(nk)]
            for cp in copies:
                cp.start()
            for cp in copies:
                cp.wait()

    return scatter(rows, pos)


def _sc_gather_rows(table, idx):
    nc, nw = _sc_workers()
    n = idx.shape[0]
    w = table.shape[1]
    ch = SC_CHUNK
    per_w = n // nw
    assert per_w * nw == n and per_w % ch == 0
    nch = per_w // ch

    @functools.partial(
        pl.kernel, mesh=plsc.VectorSubcoreMesh(core_axis_name="c", subcore_axis_name="s"),
        out_type=jax.ShapeDtypeStruct((n, w), table.dtype),
        scratch_types=[pltpu.VMEM((nch, ch), jnp.int32), pltpu.VMEM((ch, w), table.dtype),
                       pltpu.SemaphoreType.DMA])
    def gather(table_hbm, idx_hbm, out_hbm, idx_v, rows_v, sem):
        wid = lax.axis_index("s") * nc + lax.axis_index("c")
        base = wid * per_w
        pltpu.sync_copy(idx_hbm.at[pl.ds(wid * nch, nch)], idx_v)

        @pl.loop(0, nch)
        def _(ci):
            off = pl.multiple_of(base + ci * ch, ch)
            cp = pltpu.make_async_copy(table_hbm.at[idx_v.at[ci]], rows_v, sem)
            cp.start()
            cp.wait()
            pltpu.sync_copy(rows_v, out_hbm.at[pl.ds(off, ch)])

    return gather(table, idx.reshape(n // ch, ch))


def _expert_kernel(te_ref, nu_ref, xs_ref, wgu_ref, wd_ref, y_ref, wgu_b, wd_b):
    i = pl.program_id(0)

    @pl.when(i < nu_ref[0])
    def _():
        @pl.when((i == 0) | (te_ref[i] != te_ref[jnp.maximum(i - 1, 0)]))
        def _():
            wgu_b[...] = wgu_ref[0].astype(BF16)
            wd_b[...] = wd_ref[0].astype(BF16)

        lo, hi = _unpack_halves(xs_ref[...])
        half = wgu_b.shape[0] // 2
        gu = _dot(lo.astype(BF16), wgu_b[:half, :]) + _dot(hi.astype(BF16), wgu_b[half:, :])
        act = _silu(gu[:, :EXPERT_DIM]) * gu[:, EXPERT_DIM:]
        y_ref[...] = _pack_halves(_dot(act.astype(BF16), wd_b[...]))


def _experts(tile_e, n_used, xs, wgu, wd):
    r, w = xs.shape
    tm = EXPERT_TILE
    d = wgu.shape[1]
    rows = pl.BlockSpec((tm, w), lambda i, te, nu: (jnp.minimum(i, nu[0] - 1), 0))
    return pl.pallas_call(
        _expert_kernel,
        grid_spec=pltpu.PrefetchScalarGridSpec(
            num_scalar_prefetch=2,
            grid=(r // tm,),
            in_specs=[rows,
                      pl.BlockSpec((1, d, 2 * EXPERT_DIM), lambda i, te, nu: (te[i], 0, 0)),
                      pl.BlockSpec((1, EXPERT_DIM, d), lambda i, te, nu: (te[i], 0, 0))],
            out_specs=rows,
            scratch_shapes=[pltpu.VMEM((d, 2 * EXPERT_DIM), BF16), pltpu.VMEM((EXPERT_DIM, d), BF16)]),
        out_shape=jax.ShapeDtypeStruct((r, w), jnp.int32),
        compiler_params=_params("arbitrary"),
        name="experts",
    )(tile_e, n_used, xs, wgu, wd)


def _final_kernel(yk_ref, wk_ref, h_ref, x1_ref, mod_ref, wsgu_ref, wsd_ref, out_ref):
    half = yk_ref.shape[2]
    acc_lo = jnp.zeros((yk_ref.shape[1], half), F32)
    acc_hi = jnp.zeros((yk_ref.shape[1], half), F32)
    for k in range(TOP_K):
        lo, hi = _unpack_halves(yk_ref[k])
        w = wk_ref[:, k:k + 1]
        acc_lo = acc_lo + jnp.where(w != 0.0, w * lo, 0.0)
        acc_hi = acc_hi + jnp.where(w != 0.0, w * hi, 0.0)
    routed = jnp.concatenate([acc_lo, acc_hi], axis=1)
    sgu = _dot(h_ref[...], wsgu_ref[...])
    act = _silu(sgu[:, :SHARED_DIM]) * sgu[:, SHARED_DIM:]
    shared = _dot(act.astype(BF16), wsd_ref[...])
    out_ref[...] = x1_ref[...] + mod_ref[0, 5:6, :] * (routed + shared)


def _final(yk, wk_t, h2, x1, mod, wsgu, wsd, tiles_per_batch):
    t, d = h2.shape
    tm = ROW_TILE
    row = lambda w: pl.BlockSpec((tm, w), lambda i: (i, 0))
    return pl.pallas_call(
        _final_kernel,
        grid=(t // tm,),
        in_specs=[pl.BlockSpec((TOP_K, tm, d // 2), lambda i: (0, i, 0)),
                  row(TOP_K), row(d), row(d),
                  pl.BlockSpec((1, 6, d), lambda i: (i // tiles_per_batch, 0, 0)),
                  _const_spec(wsgu.shape), _const_spec(wsd.shape)],
        out_specs=row(d),
        out_shape=jax.ShapeDtypeStruct((t, d), F32),
        compiler_params=_params("arbitrary"),
        name="final",
    )(yk, wk_t, h2, x1, mod, wsgu, wsd)


def _placement():
    pq = np.zeros((LANES, N_HEADS * HEAD_PAD), np.float32)
    cq = np.zeros((1, N_HEADS * HEAD_PAD), np.float32)
    ck = np.zeros((1, N_HEADS * HEAD_PAD), np.float32)
    for hd in range(N_HEADS):
        for k in range(3):
            pq[k * N_HEADS + hd, hd * HEAD_PAD + AUG0 + k] = 1.0
            ck[0, hd * HEAD_PAD + AUG0 + k] = 1.0
            cq[0, hd * HEAD_PAD + AUG0 + 3 + k] = 1.0
    return jnp.asarray(pq, BF16), jnp.asarray(cq), jnp.asarray(ck)


def kernel(x, c, w_ada, b_ada, norm1_g, w_in, w_dw, b_dw, conv_gn_g, conv_gn_b, w_conv_out,
           q_norm_g, k_norm_g, b_forget, w_attn_out, w_out, norm2_g, w_router, router_bias,
           w_experts_gate_up, w_experts_down, w_shared_gate_up, w_shared_down):
    depth = w_ada.shape[0]
    b, s, d = x.shape

    pq, cq, ck = _placement()
    tri = jnp.asarray(np.tril(np.ones((ROW_TILE, ROW_TILE), np.float32)), BF16)
    grp = np.arange(CONV_DIM) // (CONV_DIM // CONV_GROUPS)
    gg = jnp.asarray((grp[:, None] == grp[None, :]).astype(np.float32) / (CONV_DIM // CONV_GROUPS), BF16)
    c_pad = jnp.pad(c, ((0, SUBLANES - b), (0, 0)))
    tri_pos = jnp.asarray(np.triu(np.ones((POS_TILE, POS_TILE), np.float32)), BF16)
    lstrict = jnp.asarray(np.tril(np.ones((N_EXPERTS, N_EXPERTS), np.float32), -1), BF16)

    for l in range(depth):
        mod = _ada(c_pad, w_ada[l], b_ada[l][None, :])[:b].reshape(b, 6, d)

        bf = jnp.pad(b_forget[l][None, :], ((0, 0), (0, LANES - N_HEADS)))
        gpad = lambda g, sc: jnp.tile(jnp.pad(g * sc, (0, HEAD_PAD - HEAD_DIM)), N_HEADS)[None, :]
        qscale = HEAD_DIM ** -0.5
        u, qa, ka, v, sgc, sga, cum = _inproj(
            x, mod, norm1_g[l][None, :], _inproj_weight(w_in[l]),
            bf, gpad(q_norm_g[l], qscale), gpad(k_norm_g[l], 1.0),
            pq, cq, ck, tri)

        flat = lambda a: a[:, :, :N_HEADS].transpose(0, 2, 1).reshape(-1)
        part = ATTN_TILE // DIAG_PARTS
        cs = flat(cum[:, 0::part])
        ce = flat(cum[:, part - 1::part])
        bound = (1.02 * HEAD_DIM * qscale) * jnp.max(jnp.abs(q_norm_g[l])) * jnp.max(jnp.abs(k_norm_g[l]))
        o = _attention(cs, ce, bound.reshape(1), qa, ka.transpose(0, 1, 3, 2), v)

        wdw = jnp.pad(w_dw[l], ((0, CONV_HALO - CONV_WIDTH), (0, 0)))
        x1 = _merge(u, o, sgc, sga, x, mod, wdw, b_dw[l][None, :], conv_gn_g[l][None, :],
                    conv_gn_b[l][None, :], gg, w_conv_out[l].astype(BF16), w_attn_out[l].astype(BF16),
                    w_out[l].astype(BF16))

        wr = w_router[l].T
        wr_hi = wr.astype(BF16)
        wr_lo = (wr - wr_hi.astype(F32)).astype(BF16)
        h2, h2w, comb_t, cnt = _router(x1, mod, norm2_g[l][None, :], wr_hi, wr_lo, router_bias[l][:, None])

        t = b * s
        n_tiles = (t * TOP_K) // EXPERT_TILE + N_EXPERTS
        posk, wk, gend = _positions(comb_t, cnt, tri_pos, lstrict, n_tiles * EXPERT_TILE)
        seg_end = gend[:, 0].astype(jnp.int32)
        n_used = seg_end[-1:] // EXPERT_TILE
        tile_start = jnp.arange(n_tiles, dtype=jnp.int32) * EXPERT_TILE
        tile_start = jnp.minimum(tile_start, seg_end[-1] - EXPERT_TILE)
        tile_e = jnp.sum((seg_end[None, :] <= tile_start[:, None]).astype(jnp.int32), axis=1)

        xs = _sc_scatter_rows(h2w.reshape(t, d // 2), posk, n_tiles * EXPERT_TILE)
        ys = _experts(tile_e, n_used, xs, w_experts_gate_up[l], w_experts_down[l])
        yk = _sc_gather_rows(ys, posk.reshape(-1)).reshape(TOP_K, t, d // 2)
        out = _final(yk, wk.T, h2.reshape(t, d), x1.reshape(t, d), mod,
                     w_shared_gate_up[l].astype(BF16), w_shared_down[l].astype(BF16), s // ROW_TILE)
        x = out.reshape(b, s, d)
    return x
```
